```python
import math
import jax, jax.numpy as jnp
from jax import lax
import numpy as np

D_MODEL = 2048
BATCH = 8
SEQ = 4096
DEPTH = 2

D_MIX = D_MODEL
D_A = D_MIX // 2
D_B = D_MIX - D_A
A_GROUPS = 16
B_HEADS = 16
B_HEAD_DIM = D_B // B_HEADS
CONV_A_WIDTH = 3
CONV_B_WIDTH = 4
LRU_C = 8.0
D_IN_EVEN = 3 * D_A + 2 * D_B
SB_HEADS = 16
SB_HEAD_DIM = D_MODEL // SB_HEADS
Q_BLOCK = 128
D_FF = 4 * D_MODEL
NORM_EPS = 1e-6
N_EVEN = (DEPTH + 1) // 2
N_ODD = DEPTH // 2

kernel_name = "hybrid_conv_rglru_stickbreak_block"


def rms_norm(x, g):
    xf = x.astype(jnp.float32)
    y = xf * lax.rsqrt(jnp.mean(xf * xf, axis=-1, keepdims=True) + NORM_EPS)
    return (y * g.astype(jnp.float32)).astype(x.dtype)


def causal_dwconv(x, w, bias=None):
    k_width = w.shape[0]
    s = x.shape[1]
    xp = jnp.pad(x, ((0, 0), (k_width - 1, 0), (0, 0)))
    y = w[k_width - 1] * x
    for k in range(k_width - 1):
        y = y + w[k] * xp[:, k:k + s]
    if bias is not None:
        y = y + bias
    return y


def rg_lru(x, w_a, b_a, w_x, b_x, lam):
    bsz, s, _ = x.shape
    xf = x.astype(jnp.float32)
    xh = xf.reshape(bsz, s, B_HEADS, B_HEAD_DIM)
    r = jax.nn.sigmoid(jnp.einsum('bshi,hij->bshj', xh, w_a.astype(jnp.float32)).reshape(bsz, s, D_B)
                       + b_a.astype(jnp.float32))
    i = jax.nn.sigmoid(jnp.einsum('bshi,hij->bshj', xh, w_x.astype(jnp.float32)).reshape(bsz, s, D_B)
                       + b_x.astype(jnp.float32))
    log_a = LRU_C * r * jax.nn.log_sigmoid(lam.astype(jnp.float32))
    a = jnp.exp(log_a)
    mult = jnp.sqrt(-jnp.expm1(2.0 * log_a))
    b = mult * (i * xf)

    def combine(e1, e2):
        a1, b1 = e1
        a2, b2 = e2
        return a1 * a2, a2 * b1 + b2

    _, h = lax.associative_scan(combine, (a, b), axis=1)
    return h.astype(x.dtype)


def conv_lru_mixer(h, w_in, conv_a, conv_b, conv_b_bias, rg_w_a, rg_b_a, rg_w_x, rg_b_x, rg_lambda, w_out):
    proj = h @ w_in
    a_bgate, a_cgate, a_x, b_gate, b_x = jnp.split(
        proj, [D_A, 2 * D_A, 3 * D_A, 3 * D_A + D_B], axis=-1)
    y_a = a_bgate * causal_dwconv(a_cgate * a_x, conv_a)
    xr = causal_dwconv(b_x, conv_b, conv_b_bias)
    y_b = rg_lru(xr, rg_w_a, rg_b_a, rg_w_x, rg_b_x, rg_lambda) * jax.nn.gelu(b_gate, approximate=True)
    return jnp.concatenate([y_a, y_b], axis=-1) @ w_out


def stick_breaking_attention(h, w_qkv, w_o):
    bsz, s, _ = h.shape
    qkv = h @ w_qkv
    q, k, v = jnp.split(qkv, 3, axis=-1)
    to_heads = lambda t: t.reshape(bsz, s, SB_HEADS, SB_HEAD_DIM).transpose(0, 2, 1, 3)
    q, k, v = to_heads(q), to_heads(k), to_heads(v)
    n_blocks = s // Q_BLOCK
    q_blocks = q.reshape(bsz, SB_HEADS, n_blocks, Q_BLOCK, SB_HEAD_DIM).transpose(2, 0, 1, 3, 4)
    starts = jnp.arange(n_blocks, dtype=jnp.int32) * Q_BLOCK
    scale = 1.0 / math.sqrt(SB_HEAD_DIM)
    kf = k.astype(jnp.float32)
    vf = v.astype(jnp.float32)
    key_pos = jnp.arange(s, dtype=jnp.int32)[None, :]

    def block(args):
        q_blk, start = args
        z = jnp.einsum('bhqd,bhkd->bhqk', q_blk.astype(jnp.float32), kf) * scale
        q_pos = start + jnp.arange(Q_BLOCK, dtype=jnp.int32)[:, None]
        causal = key_pos < q_pos
        log_not = jnp.where(causal, jax.nn.log_sigmoid(-z), 0.0)
        suffix = lax.cumsum(log_not, axis=3, reverse=True) - log_not
        w = jnp.where(causal, jnp.exp(jax.nn.log_sigmoid(z) + suffix), 0.0)
        return jnp.einsum('bhqk,bhkd->bhqd', w, vf)

    out = lax.map(block, (q_blocks, starts))
    out = out.transpose(1, 0, 3, 2, 4).reshape(bsz, s, D_MODEL).astype(h.dtype)
    return out @ w_o


def sq_relu_mlp(h, w_up, w_down):
    u = jax.nn.relu(h @ w_up)
    return (u * u) @ w_down


def _fwd_setup_inputs(seed: int = 0) -> dict:
    key = jax.random.key(seed)
    ks = jax.random.split(key, 20)
    nrm = lambda k, shape, fan_in: jax.random.normal(k, shape, jnp.float32) * (fan_in ** -0.5)
    x = jax.random.normal(ks[0], (BATCH, SEQ, D_MODEL), jnp.float32)
    norm_gains = 1.0 + 0.05 * jax.random.normal(ks[1], (DEPTH, 4, D_MODEL), jnp.float32)
    hyb_w_in = nrm(ks[2], (N_EVEN, D_MODEL, D_IN_EVEN), D_MODEL)
    hyb_conv_a = nrm(ks[3], (N_EVEN, CONV_A_WIDTH, D_A), CONV_A_WIDTH)
    hyb_conv_b = nrm(ks[4], (N_EVEN, CONV_B_WIDTH, D_B), CONV_B_WIDTH)
    hyb_conv_b_bias = 0.02 * jax.random.normal(ks[5], (N_EVEN, D_B), jnp.float32)
    hyb_rg_w_a = nrm(ks[6], (N_EVEN, B_HEADS, B_HEAD_DIM, B_HEAD_DIM), B_HEAD_DIM)
    hyb_rg_b_a = 0.02 * jax.random.normal(ks[7], (N_EVEN, D_B), jnp.float32)
    hyb_rg_w_x = nrm(ks[8], (N_EVEN, B_HEADS, B_HEAD_DIM, B_HEAD_DIM), B_HEAD_DIM)
    hyb_rg_b_x = 0.02 * jax.random.normal(ks[9], (N_EVEN, D_B), jnp.float32)
    u = jax.random.uniform(ks[10], (N_EVEN, D_B), jnp.float32, 0.9, 0.999)
    sig = u ** (1.0 / LRU_C)
    hyb_rg_lambda = jnp.log(sig) - jnp.log1p(-sig)
    hyb_w_out = nrm(ks[11], (N_EVEN, D_MIX, D_MODEL), D_MIX)
    sb_w_qkv = nrm(ks[12], (N_ODD, D_MODEL, 3 * D_MODEL), D_MODEL)
    sb_w_o = nrm(ks[13], (N_ODD, D_MODEL, D_MODEL), D_MODEL)
    mlp_w_up = nrm(ks[14], (DEPTH, D_MODEL, D_FF), D_MODEL)
    mlp_w_down = nrm(ks[15], (DEPTH, D_FF, D_MODEL), D_FF)
    return {"x": x, "norm_gains": norm_gains, "hyb_w_in": hyb_w_in, "hyb_conv_a": hyb_conv_a,
            "hyb_conv_b": hyb_conv_b, "hyb_conv_b_bias": hyb_conv_b_bias, "hyb_rg_w_a": hyb_rg_w_a,
            "hyb_rg_b_a": hyb_rg_b_a, "hyb_rg_w_x": hyb_rg_w_x, "hyb_rg_b_x": hyb_rg_b_x,
            "hyb_rg_lambda": hyb_rg_lambda, "hyb_w_out": hyb_w_out, "sb_w_qkv": sb_w_qkv,
            "sb_w_o": sb_w_o, "mlp_w_up": mlp_w_up, "mlp_w_down": mlp_w_down}


def _fwd_reference(x, norm_gains, hyb_w_in, hyb_conv_a, hyb_conv_b, hyb_conv_b_bias, hyb_rg_w_a, hyb_rg_b_a,
              hyb_rg_w_x, hyb_rg_b_x, hyb_rg_lambda, hyb_w_out, sb_w_qkv, sb_w_o, mlp_w_up, mlp_w_down):
    for layer in range(DEPTH):
        g = norm_gains[layer]
        h = rms_norm(x, g[0])
        if layer % 2 == 0:
            e = layer // 2
            mix = conv_lru_mixer(h, hyb_w_in[e], hyb_conv_a[e], hyb_conv_b[e], hyb_conv_b_bias[e],
                                 hyb_rg_w_a[e], hyb_rg_b_a[e], hyb_rg_w_x[e], hyb_rg_b_x[e],
                                 hyb_rg_lambda[e], hyb_w_out[e])
        else:
            o = layer // 2
            mix = stick_breaking_attention(h, sb_w_qkv[o], sb_w_o[o])
        x = x + rms_norm(mix, g[1])
        h = rms_norm(x, g[2])
        x = x + rms_norm(sq_relu_mlp(h, mlp_w_up[layer], mlp_w_down[layer]), g[3])
    return x


import jax as _jax
import jax.numpy as _jnp

TWIN_FORMAT = 'train_step'
FWD_PARAMS = ['x', 'norm_gains', 'hyb_w_in', 'hyb_conv_a', 'hyb_conv_b', 'hyb_conv_b_bias', 'hyb_rg_w_a', 'hyb_rg_b_a', 'hyb_rg_w_x', 'hyb_rg_b_x', 'hyb_rg_lambda', 'hyb_w_out', 'sb_w_qkv', 'sb_w_o', 'mlp_w_up', 'mlp_w_down']
TWIN_WEIGHTS = ['norm_gains', 'hyb_w_in', 'hyb_conv_a', 'hyb_conv_b', 'hyb_conv_b_bias', 'hyb_rg_w_a', 'hyb_rg_b_a', 'hyb_rg_w_x', 'hyb_rg_b_x', 'hyb_rg_lambda', 'hyb_w_out', 'sb_w_qkv', 'sb_w_o', 'mlp_w_up', 'mlp_w_down']
TWIN_DIFF_INPUT = 'x'
TWIN_INPUTS = ['x', 'norm_gains', 'hyb_w_in', 'hyb_conv_a', 'hyb_conv_b', 'hyb_conv_b_bias', 'hyb_rg_w_a', 'hyb_rg_b_a', 'hyb_rg_w_x', 'hyb_rg_b_x', 'hyb_rg_lambda', 'hyb_w_out', 'sb_w_qkv', 'sb_w_o', 'mlp_w_up', 'mlp_w_down', 'loss_target', 'm_norm_gains', 'm_hyb_w_in', 'm_hyb_conv_a', 'm_hyb_conv_b', 'm_hyb_conv_b_bias', 'm_hyb_rg_w_a', 'm_hyb_rg_b_a', 'm_hyb_rg_w_x', 'm_hyb_rg_b_x', 'm_hyb_rg_lambda', 'm_hyb_w_out', 'm_sb_w_qkv', 'm_sb_w_o', 'm_mlp_w_up', 'm_mlp_w_down', 'v_norm_gains', 'v_hyb_w_in', 'v_hyb_conv_a', 'v_hyb_conv_b', 'v_hyb_conv_b_bias', 'v_hyb_rg_w_a', 'v_hyb_rg_b_a', 'v_hyb_rg_w_x', 'v_hyb_rg_b_x', 'v_hyb_rg_lambda', 'v_hyb_w_out', 'v_sb_w_qkv', 'v_sb_w_o', 'v_mlp_w_up', 'v_mlp_w_down']
TWIN_OUTPUTS = ['loss', 'grad_x', 'grad_norm_gains', 'grad_hyb_w_in', 'grad_hyb_conv_a', 'grad_hyb_conv_b', 'grad_hyb_conv_b_bias', 'grad_hyb_rg_w_a', 'grad_hyb_rg_b_a', 'grad_hyb_rg_w_x', 'grad_hyb_rg_b_x', 'grad_hyb_rg_lambda', 'grad_hyb_w_out', 'grad_sb_w_qkv', 'grad_sb_w_o', 'grad_mlp_w_up', 'grad_mlp_w_down', 'delta_norm_gains', 'delta_hyb_w_in', 'delta_hyb_conv_a', 'delta_hyb_conv_b', 'delta_hyb_conv_b_bias', 'delta_hyb_rg_w_a', 'delta_hyb_rg_b_a', 'delta_hyb_rg_w_x', 'delta_hyb_rg_b_x', 'delta_hyb_rg_lambda', 'delta_hyb_w_out', 'delta_sb_w_qkv', 'delta_sb_w_o', 'delta_mlp_w_up', 'delta_mlp_w_down', 'new_m_norm_gains', 'new_m_hyb_w_in', 'new_m_hyb_conv_a', 'new_m_hyb_conv_b', 'new_m_hyb_conv_b_bias', 'new_m_hyb_rg_w_a', 'new_m_hyb_rg_b_a', 'new_m_hyb_rg_w_x', 'new_m_hyb_rg_b_x', 'new_m_hyb_rg_lambda', 'new_m_hyb_w_out', 'new_m_sb_w_qkv', 'new_m_sb_w_o', 'new_m_mlp_w_up', 'new_m_mlp_w_down', 'new_v_norm_gains', 'new_v_hyb_w_in', 'new_v_hyb_conv_a', 'new_v_hyb_conv_b', 'new_v_hyb_conv_b_bias', 'new_v_hyb_rg_w_a', 'new_v_hyb_rg_b_a', 'new_v_hyb_rg_w_x', 'new_v_hyb_rg_b_x', 'new_v_hyb_rg_lambda', 'new_v_hyb_w_out', 'new_v_sb_w_qkv', 'new_v_sb_w_o', 'new_v_mlp_w_up', 'new_v_mlp_w_down']
TWIN_LEAF_KINDS = {'loss': 'loss', 'grad_x': 'grad_x', 'grad_norm_gains': 'grad_w', 'grad_hyb_w_in': 'grad_w', 'grad_hyb_conv_a': 'grad_w', 'grad_hyb_conv_b': 'grad_w', 'grad_hyb_conv_b_bias': 'grad_w', 'grad_hyb_rg_w_a': 'grad_w', 'grad_hyb_rg_b_a': 'grad_w', 'grad_hyb_rg_w_x': 'grad_w', 'grad_hyb_rg_b_x': 'grad_w', 'grad_hyb_rg_lambda': 'grad_w', 'grad_hyb_w_out': 'grad_w', 'grad_sb_w_qkv': 'grad_w', 'grad_sb_w_o': 'grad_w', 'grad_mlp_w_up': 'grad_w', 'grad_mlp_w_down': 'grad_w', 'delta_norm_gains': 'delta_w', 'delta_hyb_w_in': 'delta_w', 'delta_hyb_conv_a': 'delta_w', 'delta_hyb_conv_b': 'delta_w', 'delta_hyb_conv_b_bias': 'delta_w', 'delta_hyb_rg_w_a': 'delta_w', 'delta_hyb_rg_b_a': 'delta_w', 'delta_hyb_rg_w_x': 'delta_w', 'delta_hyb_rg_b_x': 'delta_w', 'delta_hyb_rg_lambda': 'delta_w', 'delta_hyb_w_out': 'delta_w', 'delta_sb_w_qkv': 'delta_w', 'delta_sb_w_o': 'delta_w', 'delta_mlp_w_up': 'delta_w', 'delta_mlp_w_down': 'delta_w', 'new_m_norm_gains': 'new_m', 'new_m_hyb_w_in': 'new_m', 'new_m_hyb_conv_a': 'new_m', 'new_m_hyb_conv_b': 'new_m', 'new_m_hyb_conv_b_bias': 'new_m', 'new_m_hyb_rg_w_a': 'new_m', 'new_m_hyb_rg_b_a': 'new_m', 'new_m_hyb_rg_w_x': 'new_m', 'new_m_hyb_rg_b_x': 'new_m', 'new_m_hyb_rg_lambda': 'new_m', 'new_m_hyb_w_out': 'new_m', 'new_m_sb_w_qkv': 'new_m', 'new_m_sb_w_o': 'new_m', 'new_m_mlp_w_up': 'new_m', 'new_m_mlp_w_down': 'new_m', 'new_v_norm_gains': 'new_v', 'new_v_hyb_w_in': 'new_v', 'new_v_hyb_conv_a': 'new_v', 'new_v_hyb_conv_b': 'new_v', 'new_v_hyb_conv_b_bias': 'new_v', 'new_v_hyb_rg_w_a': 'new_v', 'new_v_hyb_rg_b_a': 'new_v', 'new_v_hyb_rg_w_x': 'new_v', 'new_v_hyb_rg_b_x': 'new_v', 'new_v_hyb_rg_lambda': 'new_v', 'new_v_hyb_w_out': 'new_v', 'new_v_sb_w_qkv': 'new_v', 'new_v_sb_w_o': 'new_v', 'new_v_mlp_w_up': 'new_v', 'new_v_mlp_w_down': 'new_v'}


def _forward(args):
    return _fwd_reference(*[args[k] for k in FWD_PARAMS])


def _output_shape():
    out = _jax.eval_shape(lambda: _forward(_fwd_setup_inputs(0)))
    return out.shape, out.dtype

N_MICROBATCH = 1
ADAM_LR = 0.001
ADAM_B1 = 0.9
ADAM_B2 = 0.999
ADAM_EPS = 1e-08
ADAM_WD = 0.01
ADAM_STEP = 10
PER_EXAMPLE_BATCH_AXIS = {'x': 0, 'loss_target': 0}
SHARED_INPUTS = []
_WEIGHT_DTYPES = {'norm_gains': _jnp.float32, 'hyb_w_in': _jnp.float32, 'hyb_conv_a': _jnp.float32, 'hyb_conv_b': _jnp.float32, 'hyb_conv_b_bias': _jnp.float32, 'hyb_rg_w_a': _jnp.float32, 'hyb_rg_b_a': _jnp.float32, 'hyb_rg_w_x': _jnp.float32, 'hyb_rg_b_x': _jnp.float32, 'hyb_rg_lambda': _jnp.float32, 'hyb_w_out': _jnp.float32, 'sb_w_qkv': _jnp.float32, 'sb_w_o': _jnp.float32, 'mlp_w_up': _jnp.float32, 'mlp_w_down': _jnp.float32}
MOMENT_SCALE = {'norm_gains': 1.185088e+01, 'hyb_w_in': 4.016809e-01, 'hyb_conv_a': 5.036566e-01, 'hyb_conv_b': 1.787169e+00, 'hyb_conv_b_bias': 2.287449e+01, 'hyb_rg_w_a': 7.935479e-01, 'hyb_rg_b_a': 4.615457e-01, 'hyb_rg_w_x': 1.483653e+00, 'hyb_rg_b_x': 5.349719e-01, 'hyb_rg_lambda': 6.605971e-01, 'hyb_w_out': 1.446277e+00, 'sb_w_qkv': 2.099876e+00, 'sb_w_o': 3.623414e+00, 'mlp_w_up': 5.567755e-01, 'mlp_w_down': 4.728486e+00}


def _to_microbatches(a, axis):
    t = _jnp.moveaxis(a, axis, 0)
    t = t.reshape((N_MICROBATCH, t.shape[0] // N_MICROBATCH) + t.shape[1:])
    return _jnp.moveaxis(t, 1, axis + 1)


def setup_inputs(seed: int = 0) -> dict:
    inp = _fwd_setup_inputs(seed)
    key = _jax.random.fold_in(_jax.random.key(seed), 7919)
    shape, _ = _output_shape()
    out = dict(inp)
    out["loss_target"] = _jax.random.normal(_jax.random.fold_in(key, 0), shape, _jnp.float32)
    for i, name in enumerate(TWIN_WEIGHTS):
        w = inp[name].astype(_jnp.float32)
        if MOMENT_SCALE is None:
            s = _jnp.sqrt(_jnp.mean(_jnp.square(w)) + 1e-30)
        else:
            s = MOMENT_SCALE[name]
        km, kv = _jax.random.split(_jax.random.fold_in(key, i + 1))
        out[name] = w
        out["m_" + name] = s * _jax.random.normal(km, w.shape, _jnp.float32)
        out["v_" + name] = (s * s) * _jax.random.uniform(kv, w.shape, _jnp.float32, 0.5, 1.5)
    if N_MICROBATCH > 1:
        for name, axis in PER_EXAMPLE_BATCH_AXIS.items():
            out[name] = _to_microbatches(out[name], axis)
    return {'x': out['x'], 'norm_gains': out['norm_gains'], 'hyb_w_in': out['hyb_w_in'], 'hyb_conv_a': out['hyb_conv_a'], 'hyb_conv_b': out['hyb_conv_b'], 'hyb_conv_b_bias': out['hyb_conv_b_bias'], 'hyb_rg_w_a': out['hyb_rg_w_a'], 'hyb_rg_b_a': out['hyb_rg_b_a'], 'hyb_rg_w_x': out['hyb_rg_w_x'], 'hyb_rg_b_x': out['hyb_rg_b_x'], 'hyb_rg_lambda': out['hyb_rg_lambda'], 'hyb_w_out': out['hyb_w_out'], 'sb_w_qkv': out['sb_w_qkv'], 'sb_w_o': out['sb_w_o'], 'mlp_w_up': out['mlp_w_up'], 'mlp_w_down': out['mlp_w_down'], 'loss_target': out['loss_target'], 'm_norm_gains': out['m_norm_gains'], 'm_hyb_w_in': out['m_hyb_w_in'], 'm_hyb_conv_a': out['m_hyb_conv_a'], 'm_hyb_conv_b': out['m_hyb_conv_b'], 'm_hyb_conv_b_bias': out['m_hyb_conv_b_bias'], 'm_hyb_rg_w_a': out['m_hyb_rg_w_a'], 'm_hyb_rg_b_a': out['m_hyb_rg_b_a'], 'm_hyb_rg_w_x': out['m_hyb_rg_w_x'], 'm_hyb_rg_b_x': out['m_hyb_rg_b_x'], 'm_hyb_rg_lambda': out['m_hyb_rg_lambda'], 'm_hyb_w_out': out['m_hyb_w_out'], 'm_sb_w_qkv': out['m_sb_w_qkv'], 'm_sb_w_o': out['m_sb_w_o'], 'm_mlp_w_up': out['m_mlp_w_up'], 'm_mlp_w_down': out['m_mlp_w_down'], 'v_norm_gains': out['v_norm_gains'], 'v_hyb_w_in': out['v_hyb_w_in'], 'v_hyb_conv_a': out['v_hyb_conv_a'], 'v_hyb_conv_b': out['v_hyb_conv_b'], 'v_hyb_conv_b_bias': out['v_hyb_conv_b_bias'], 'v_hyb_rg_w_a': out['v_hyb_rg_w_a'], 'v_hyb_rg_b_a': out['v_hyb_rg_b_a'], 'v_hyb_rg_w_x': out['v_hyb_rg_w_x'], 'v_hyb_rg_b_x': out['v_hyb_rg_b_x'], 'v_hyb_rg_lambda': out['v_hyb_rg_lambda'], 'v_hyb_w_out': out['v_hyb_w_out'], 'v_sb_w_qkv': out['v_sb_w_qkv'], 'v_sb_w_o': out['v_sb_w_o'], 'v_mlp_w_up': out['v_mlp_w_up'], 'v_mlp_w_down': out['v_mlp_w_down']}


def _loss(weights, diff, rest, loss_target):
    with _jax.named_scope("forward"):
        args = {**rest, TWIN_DIFF_INPUT: diff, **{k: w.astype(_WEIGHT_DTYPES[k]) for k, w in weights.items()}}
        y = _forward(args)
    with _jax.named_scope("loss_head"):
        err = _jnp.square(y.astype(_jnp.float32) - loss_target)
        return 0.5 * _jnp.sum(_jnp.mean(err, axis=-1)) if err.ndim else 0.5 * err


def _adamw(w, g, m, v):
    m = ADAM_B1 * m + (1.0 - ADAM_B1) * g
    v = ADAM_B2 * v + (1.0 - ADAM_B2) * _jnp.square(g)
    m_hat = m / (1.0 - ADAM_B1 ** ADAM_STEP)
    v_hat = v / (1.0 - ADAM_B2 ** ADAM_STEP)
    delta = -ADAM_LR * (m_hat / (_jnp.sqrt(v_hat) + ADAM_EPS) + ADAM_WD * w)
    return delta, m, v


def reference(x, norm_gains, hyb_w_in, hyb_conv_a, hyb_conv_b, hyb_conv_b_bias, hyb_rg_w_a, hyb_rg_b_a, hyb_rg_w_x, hyb_rg_b_x, hyb_rg_lambda, hyb_w_out, sb_w_qkv, sb_w_o, mlp_w_up, mlp_w_down, loss_target, m_norm_gains, m_hyb_w_in, m_hyb_conv_a, m_hyb_conv_b, m_hyb_conv_b_bias, m_hyb_rg_w_a, m_hyb_rg_b_a, m_hyb_rg_w_x, m_hyb_rg_b_x, m_hyb_rg_lambda, m_hyb_w_out, m_sb_w_qkv, m_sb_w_o, m_mlp_w_up, m_mlp_w_down, v_norm_gains, v_hyb_w_in, v_hyb_conv_a, v_hyb_conv_b, v_hyb_conv_b_bias, v_hyb_rg_w_a, v_hyb_rg_b_a, v_hyb_rg_w_x, v_hyb_rg_b_x, v_hyb_rg_lambda, v_hyb_w_out, v_sb_w_qkv, v_sb_w_o, v_mlp_w_up, v_mlp_w_down):
    given = dict(x=x, norm_gains=norm_gains, hyb_w_in=hyb_w_in, hyb_conv_a=hyb_conv_a, hyb_conv_b=hyb_conv_b, hyb_conv_b_bias=hyb_conv_b_bias, hyb_rg_w_a=hyb_rg_w_a, hyb_rg_b_a=hyb_rg_b_a, hyb_rg_w_x=hyb_rg_w_x, hyb_rg_b_x=hyb_rg_b_x, hyb_rg_lambda=hyb_rg_lambda, hyb_w_out=hyb_w_out, sb_w_qkv=sb_w_qkv, sb_w_o=sb_w_o, mlp_w_up=mlp_w_up, mlp_w_down=mlp_w_down, loss_target=loss_target, m_norm_gains=m_norm_gains, m_hyb_w_in=m_hyb_w_in, m_hyb_conv_a=m_hyb_conv_a, m_hyb_conv_b=m_hyb_conv_b, m_hyb_conv_b_bias=m_hyb_conv_b_bias, m_hyb_rg_w_a=m_hyb_rg_w_a, m_hyb_rg_b_a=m_hyb_rg_b_a, m_hyb_rg_w_x=m_hyb_rg_w_x, m_hyb_rg_b_x=m_hyb_rg_b_x, m_hyb_rg_lambda=m_hyb_rg_lambda, m_hyb_w_out=m_hyb_w_out, m_sb_w_qkv=m_sb_w_qkv, m_sb_w_o=m_sb_w_o, m_mlp_w_up=m_mlp_w_up, m_mlp_w_down=m_mlp_w_down, v_norm_gains=v_norm_gains, v_hyb_w_in=v_hyb_w_in, v_hyb_conv_a=v_hyb_conv_a, v_hyb_conv_b=v_hyb_conv_b, v_hyb_conv_b_bias=v_hyb_conv_b_bias, v_hyb_rg_w_a=v_hyb_rg_w_a, v_hyb_rg_b_a=v_hyb_rg_b_a, v_hyb_rg_w_x=v_hyb_rg_w_x, v_hyb_rg_b_x=v_hyb_rg_b_x, v_hyb_rg_lambda=v_hyb_rg_lambda, v_hyb_w_out=v_hyb_w_out, v_sb_w_qkv=v_sb_w_qkv, v_sb_w_o=v_sb_w_o, v_mlp_w_up=v_mlp_w_up, v_mlp_w_down=v_mlp_w_down)
    weights = {n: given[n] for n in TWIN_WEIGHTS}
    shared = {n: given[n] for n in SHARED_INPUTS}
    per_example = {n: given[n] for n in ['x']}
    grad_fn = _jax.value_and_grad(_loss, argnums=(0, 1))

    def one_microbatch(ex, loss_target):
        ex = dict(ex)
        diff = ex.pop(TWIN_DIFF_INPUT)
        return grad_fn(weights, diff, {**shared, **ex}, loss_target)

    if N_MICROBATCH == 1:
        loss, (grad_w, grad_x) = one_microbatch(per_example, given["loss_target"])
    else:
        def body(carry, xs):
            loss_sum, grad_sum = carry
            l_k, (gw_k, gx_k) = one_microbatch(xs[0], xs[1])
            with _jax.named_scope("update"):
                return (loss_sum + l_k, _jax.tree.map(_jnp.add, grad_sum, gw_k)), gx_k

        init = (_jnp.zeros((), _jnp.float32), _jax.tree.map(_jnp.zeros_like, weights))
        (loss, grad_w), grad_x = _jax.lax.scan(body, init, (per_example, given["loss_target"]))
    with _jax.named_scope("update"):
        delta_w, new_m, new_v = {}, {}, {}
        for n in TWIN_WEIGHTS:
            delta_w[n], new_m[n], new_v[n] = _adamw(weights[n], grad_w[n], given["m_" + n], given["v_" + n])
    return (loss, grad_x, *[grad_w[n] for n in TWIN_WEIGHTS], *[delta_w[n] for n in TWIN_WEIGHTS],
            *[new_m[n] for n in TWIN_WEIGHTS], *[new_v[n] for n in TWIN_WEIGHTS])
```

```python
import functools
import math

import jax
import jax.numpy as jnp
from jax import lax
from jax.experimental import pallas as pl
from jax.experimental.pallas import tpu as pltpu

F32 = jnp.float32
BF16 = jnp.bfloat16

NORM_EPS = 1e-6
LRU_C = 8.0
ATT_HEAD_DIM = 128
RG_HEAD_DIM = 64
LANES = 128
SUBLANES = 8
N_DEV = 8
ADAM_LR = 0.001
ADAM_B1 = 0.9
ADAM_B2 = 0.999
ADAM_EPS = 1e-08
ADAM_WD = 0.01
ADAM_STEP = 10
VMEM_LIMIT = 56 * 1024 * 1024
MESH = pl.DeviceIdType.MESH


def _tile(n, pref):
    if n <= pref:
        return n
    t = (pref // LANES) * LANES
    while t > LANES and n % t:
        t -= LANES
    assert n % t == 0, (n, pref)
    return t


def _params(sem):
    return pltpu.CompilerParams(dimension_semantics=sem, vmem_limit_bytes=VMEM_LIMIT)


DIMS_NN = (((1,), (0,)), ((), ()))
DIMS_NT = (((1,), (1,)), ((), ()))
DIMS_TN = (((0,), (0,)), ((), ()))


def _matmul(name, ins, outs, grid, dims, acc_shape, epilogue=None):
    n_in, n_out, nk = len(ins), len(outs), grid[2]

    def body(*refs):
        a_ref, b_ref = refs[0], refs[1]
        extras = refs[2:n_in]
        out_refs = refs[n_in:n_in + n_out]
        acc = refs[n_in + n_out]
        k = pl.program_id(2)

        @pl.when(k == 0)
        def _():
            acc[...] = jnp.zeros_like(acc)

        acc[...] += lax.dot_general(a_ref[...], b_ref[...], dims, preferred_element_type=F32)

        @pl.when(k == nk - 1)
        def _():
            r = acc[...]
            res = epilogue(r, *[e[...] for e in extras]) if epilogue is not None else (r,)
            for o, v in zip(out_refs, res):
                o[...] = v.astype(o.dtype)

    res = pl.pallas_call(
        body, name=name, grid=grid,
        in_specs=[s for _, s in ins],
        out_specs=[s for _, s in outs],
        out_shape=[s for s, _ in outs],
        scratch_shapes=[pltpu.VMEM(acc_shape, F32)],
        compiler_params=_params(("parallel", "parallel", "arbitrary")),
    )(*[a for a, _ in ins])
    return res


def _mm_nn(name, a, b, out_dtype, *, a_seg=None, out_seg=None, tm=1024, tn=1024, tk=512, epilogue=None,
           extras=(), n_out=1, out_dtypes=None):
    if a_seg:
        _, M, ks = a.shape
        K = ks * a_seg
    else:
        M, K = a.shape
        ks = K
    N = b.shape[1]
    ns = N // out_seg if out_seg else N
    tm, tn, tk = _tile(M, tm), _tile(ns, tn), _tile(ks, tk)
    nks, nns = ks // tk, ns // tn
    grid = (M // tm, N // tn, K // tk)
    if a_seg:
        a_spec = pl.BlockSpec((None, tm, tk), lambda i, j, k: (k // nks, i, k % nks))
    else:
        a_spec = pl.BlockSpec((tm, tk), lambda i, j, k: (i, k))
    b_spec = pl.BlockSpec((tk, tn), lambda i, j, k: (k, j))
    if out_seg:
        o_spec = pl.BlockSpec((None, tm, tn), lambda i, j, k: (j // nns, i, j % nns))
        o_shape = (out_seg, M, ns)
    else:
        o_spec = pl.BlockSpec((tm, tn), lambda i, j, k: (i, j))
        o_shape = (M, N)
    dts = out_dtypes or [out_dtype] * n_out
    outs = [(jax.ShapeDtypeStruct(o_shape, dt), o_spec) for dt in dts]
    ins = [(a, a_spec), (b, b_spec)] + [(e, o_spec) for e in extras]
    return _matmul(name, ins, outs, grid, DIMS_NN, (tm, tn), epilogue)


def _mm_nt(name, a, b, out_dtype, *, a_seg=None, out_seg=None, tm=1024, tn=1024, tk=512, epilogue=None, extras=()):
    if a_seg:
        _, M, ks = a.shape
        K = ks * a_seg
    else:
        M, K = a.shape
        ks = K
    N = b.shape[0]
    ns = N // out_seg if out_seg else N
    tm, tn, tk = _tile(M, tm), _tile(ns, tn), _tile(ks, tk)
    nks, nns = ks // tk, ns // tn
    grid = (M // tm, N // tn, K // tk)
    if a_seg:
        a_spec = pl.BlockSpec((None, tm, tk), lambda i, j, k: (k // nks, i, k % nks))
    else:
        a_spec = pl.BlockSpec((tm, tk), lambda i, j, k: (i, k))
    b_spec = pl.BlockSpec((tn, tk), lambda i, j, k: (j, k))
    if out_seg:
        o_spec = pl.BlockSpec((None, tm, tn), lambda i, j, k: (j // nns, i, j % nns))
        o_shape = (out_seg, M, ns)
    else:
        o_spec = pl.BlockSpec((tm, tn), lambda i, j, k: (i, j))
        o_shape = (M, N)
    outs = [(jax.ShapeDtypeStruct(o_shape, out_dtype), o_spec)]
    ins = [(a, a_spec), (b, b_spec)] + [(e, o_spec) for e in extras]
    return _matmul(name, ins, outs, grid, DIMS_NT, (tm, tn), epilogue)[0]


def _mm_tn(name, a, b, out_dtype, *, a_seg=None, b_seg=None, out_blocks=None, tm=1024, tn=1024, tk=512):
    if a_seg:
        _, T, ms = a.shape
        M = ms * a_seg
    else:
        T, M = a.shape
        ms = M
    if b_seg:
        _, _, ns = b.shape
        N = ns * b_seg
    else:
        N = b.shape[1]
        ns = N
    nb_cols = N // out_blocks if out_blocks else N
    tm, tk = _tile(ms, tm), _tile(T, tk)
    tn = _tile(math.gcd(ns, nb_cols), tn)
    nms, nns, nbs = ms // tm, ns // tn, nb_cols // tn
    grid = (M // tm, N // tn, T // tk)
    if a_seg:
        a_spec = pl.BlockSpec((None, tk, tm), lambda i, j, k: (i // nms, k, i % nms))
    else:
        a_spec = pl.BlockSpec((tk, tm), lambda i, j, k: (k, i))
    if b_seg:
        b_spec = pl.BlockSpec((None, tk, tn), lambda i, j, k: (j // nns, k, j % nns))
    else:
        b_spec = pl.BlockSpec((tk, tn), lambda i, j, k: (k, j))
    if out_blocks:
        o_spec = pl.BlockSpec((None, tm, tn), lambda i, j, k: (j // nbs, i, j % nbs))
        o_shape = (out_blocks, M, nb_cols)
    else:
        o_spec = pl.BlockSpec((tm, tn), lambda i, j, k: (i, j))
        o_shape = (M, N)
    outs = [(jax.ShapeDtypeStruct(o_shape, out_dtype), o_spec)]
    return _matmul(name, [(a, a_spec), (b, b_spec)], outs, grid, DIMS_TN, (tm, tn))[0]


def _rms(x):
    return lax.rsqrt(jnp.mean(x * x, axis=-1, keepdims=True) + NORM_EPS)


def _row_tile(T):
    return _tile(T, 256)


def _norm_fwd(name, x, g):
    T, D = x.shape
    tr = _row_tile(T)

    def body(x_ref, g_ref, h_ref):
        xv = x_ref[...]
        h_ref[...] = (xv * _rms(xv) * g_ref[...]).astype(h_ref.dtype)

    row = pl.BlockSpec((tr, D), lambda i: (i, 0))
    vec = pl.BlockSpec((1, D), lambda i: (0, 0))
    return pl.pallas_call(body, name=name, grid=(T // tr,), in_specs=[row, vec], out_specs=row,
                          out_shape=jax.ShapeDtypeStruct((T, D), BF16), compiler_params=_params(("parallel",)))(x, g)


def _resid_norm(name, x, br, g_post, g_next):
    T, D = x.shape
    tr = _row_tile(T)

    def body(x_ref, br_ref, gp_ref, gn_ref, xn_ref, h_ref):
        b = br_ref[...]
        xn = x_ref[...] + b * _rms(b) * gp_ref[...]
        xn_ref[...] = xn
        h_ref[...] = (xn * _rms(xn) * gn_ref[...]).astype(h_ref.dtype)

    row = pl.BlockSpec((tr, D), lambda i: (i, 0))
    vec = pl.BlockSpec((1, D), lambda i: (0, 0))
    return pl.pallas_call(body, name=name, grid=(T // tr,), in_specs=[row, row, vec, vec], out_specs=[row, row],
                          out_shape=[jax.ShapeDtypeStruct((T, D), F32), jax.ShapeDtypeStruct((T, D), BF16)],
                          compiler_params=_params(("parallel",)))(x, br, g_post, g_next)


def _final_loss(name, x, br, g_post, target):
    T, D = x.shape
    tr = _row_tile(T)

    def body(x_ref, br_ref, gp_ref, t_ref, dy_ref, ls_ref):
        b = br_ref[...]
        err = x_ref[...] + b * _rms(b) * gp_ref[...] - t_ref[...]
        dy_ref[...] = err * (1.0 / D)

        @pl.when(pl.program_id(0) == 0)
        def _():
            ls_ref[...] = jnp.zeros_like(ls_ref)

        ls_ref[...] += jnp.sum(err * err)

    row = pl.BlockSpec((tr, D), lambda i: (i, 0))
    vec = pl.BlockSpec((1, D), lambda i: (0, 0))
    acc = pl.BlockSpec((SUBLANES, LANES), lambda i: (0, 0))
    return pl.pallas_call(body, name=name, grid=(T // tr,), in_specs=[row, row, vec, row], out_specs=[row, acc],
                          out_shape=[jax.ShapeDtypeStruct((T, D), F32), jax.ShapeDtypeStruct((SUBLANES, LANES), F32)],
                          compiler_params=_params(("arbitrary",)))(x, br, g_post, target)


def _norm_bwd(name, x, g, dy, add, out_dtype):
    T, D = x.shape
    tr = _row_tile(T)
    has_add = add is not None

    def body(*refs):
        if has_add:
            x_ref, g_ref, dy_ref, add_ref, dx_ref, dg_ref = refs
        else:
            x_ref, g_ref, dy_ref, dx_ref, dg_ref = refs
        xv = x_ref[...]
        r = _rms(xv)
        xhat = xv * r
        dyv = dy_ref[...].astype(F32)
        gdy = dyv * g_ref[...]
        dx = r * (gdy - xhat * jnp.mean(gdy * xhat, axis=-1, keepdims=True))
        if has_add:
            dx = dx + add_ref[...]
        dx_ref[...] = dx.astype(dx_ref.dtype)

        @pl.when(pl.program_id(0) == 0)
        def _():
            dg_ref[...] = jnp.zeros_like(dg_ref)

        dg_ref[...] += jnp.sum(dyv * xhat, axis=0, keepdims=True)

    row = pl.BlockSpec((tr, D), lambda i: (i, 0))
    vec = pl.BlockSpec((1, D), lambda i: (0, 0))
    ins = [x, g, dy] + ([add] if has_add else [])
    specs = [row, vec, row] + ([row] if has_add else [])
    return pl.pallas_call(body, name=name, grid=(T // tr,), in_specs=specs, out_specs=[row, vec],
                          out_shape=[jax.ShapeDtypeStruct((T, D), out_dtype), jax.ShapeDtypeStruct((1, D), F32)],
                          compiler_params=_params(("arbitrary",)))(*ins)


HALO = SUBLANES
TIME_CHUNK = 512


def _chunks(T):
    tc = min(TIME_CHUNK, T)
    assert T % tc == 0 and tc % SUBLANES == 0
    return [(t0, tc) for t0 in range(0, T, tc)]


def _log_sigmoid(x):
    return -(jnp.maximum(-x, 0.0) + jnp.log(1.0 + jnp.exp(-jnp.abs(x))))


def _one_minus_exp(x):
    series = -x * (1.0 + x * (0.5 + x * (1.0 / 6.0 + x * (1.0 / 24.0))))
    return jnp.where(x > -0.01, series, 1.0 - jnp.exp(x))


_GELU_C = math.sqrt(2.0 / math.pi)


def _gelu(x):
    return 0.5 * x * (1.0 + jnp.tanh(_GELU_C * (x + 0.044715 * x * x * x)))


def _gelu_grad(x):
    th = jnp.tanh(_GELU_C * (x + 0.044715 * x * x * x))
    return 0.5 * (1.0 + th) + 0.5 * x * (1.0 - th * th) * _GELU_C * (1.0 + 3.0 * 0.044715 * x * x)


def _tile_scan(a, b, reverse):
    rows = a.shape[0]
    pos = lax.broadcasted_iota(jnp.int32, a.shape, 0) & (SUBLANES - 1)
    for d in (1, 2, 4):
        if reverse:
            ok = pos < SUBLANES - d
            shift = rows - d
        else:
            ok = pos >= d
            shift = d
        a_sh = jnp.where(ok, pltpu.roll(a, shift, 0), 1.0)
        b_sh = jnp.where(ok, pltpu.roll(b, shift, 0), 0.0)
        b = a * b_sh + b
        a = a * a_sh
    return a, b


def _carry_scan(a_s, b_s, T, reverse):
    n = T // SUBLANES
    edge = 0 if reverse else SUBLANES - 1

    def step(j, carry):
        g = (n - 1 - j) if reverse else j
        r = pl.multiple_of(g * SUBLANES, SUBLANES)
        h = b_s[pl.ds(r, SUBLANES), :] + a_s[pl.ds(r, SUBLANES), :] * carry
        b_s[pl.ds(r, SUBLANES), :] = h
        return jnp.broadcast_to(h[edge:edge + 1, :], h.shape)

    lax.fori_loop(0, n, step, jnp.zeros((SUBLANES, a_s.shape[1]), F32))


def _seg_spec(T, seg, nblk):
    return pl.BlockSpec((None, T, LANES), lambda c: (seg, 0, c))


def _rows_to_tile(rows):
    idx = lax.broadcasted_iota(jnp.int32, (SUBLANES, LANES), 0)
    out = jnp.zeros((SUBLANES, LANES), F32)
    for k, r in enumerate(rows):
        out = jnp.where(idx == k, r, out)
    return out


def _mixer_a_fwd(proj, conv_a):
    _, T, C = proj.shape
    nblk = C // LANES
    chunks = _chunks(T)

    def body(bg_ref, cg_ref, ax_ref, w_ref, y_ref, p_s):
        p_s[pl.ds(0, HALO), :] = jnp.zeros((HALO, LANES), F32)
        for t0, tc in chunks:
            p_s[pl.ds(HALO + t0, tc), :] = cg_ref[pl.ds(t0, tc), :] * ax_ref[pl.ds(t0, tc), :]
        w = w_ref[...]
        for t0, tc in chunks:
            c = (w[2:3, :] * p_s[pl.ds(HALO + t0, tc), :] + w[1:2, :] * p_s[pl.ds(HALO + t0 - 1, tc), :]
                 + w[0:1, :] * p_s[pl.ds(HALO + t0 - 2, tc), :])
            y_ref[pl.ds(t0, tc), :] = (bg_ref[pl.ds(t0, tc), :] * c).astype(y_ref.dtype)

    return pl.pallas_call(
        body, name="mixer_a_fwd", grid=(nblk,),
        in_specs=[_seg_spec(T, 0, nblk), _seg_spec(T, 1, nblk), _seg_spec(T, 2, nblk),
                  pl.BlockSpec((3, LANES), lambda c: (0, c))],
        out_specs=pl.BlockSpec((T, LANES), lambda c: (0, c)),
        out_shape=jax.ShapeDtypeStruct((T, C), BF16),
        scratch_shapes=[pltpu.VMEM((T + HALO, LANES), F32)],
        compiler_params=_params(("parallel",)))(proj, proj, proj, conv_a)


def _mixer_a_bwd(proj, conv_a, dy):
    _, T, C = proj.shape
    nblk = C // LANES
    chunks = _chunks(T)

    def body(bg_ref, cg_ref, ax_ref, w_ref, dy_ref, dp_ref, dw_ref, p_s, dc_s):
        p_s[pl.ds(0, HALO), :] = jnp.zeros((HALO, LANES), F32)
        dc_s[pl.ds(T, HALO), :] = jnp.zeros((HALO, LANES), F32)
        for t0, tc in chunks:
            p_s[pl.ds(HALO + t0, tc), :] = cg_ref[pl.ds(t0, tc), :] * ax_ref[pl.ds(t0, tc), :]
        w = w_ref[...]
        for t0, tc in chunks:
            c = (w[2:3, :] * p_s[pl.ds(HALO + t0, tc), :] + w[1:2, :] * p_s[pl.ds(HALO + t0 - 1, tc), :]
                 + w[0:1, :] * p_s[pl.ds(HALO + t0 - 2, tc), :])
            dyv = dy_ref[pl.ds(t0, tc), :]
            dp_ref[0, pl.ds(t0, tc), :] = (dyv * c).astype(dp_ref.dtype)
            dc_s[pl.ds(t0, tc), :] = dyv * bg_ref[pl.ds(t0, tc), :]
        dw = [jnp.zeros((1, LANES), F32) for _ in range(3)]
        for t0, tc in chunks:
            dc = dc_s[pl.ds(t0, tc), :]
            dpv = w[2:3, :] * dc + w[1:2, :] * dc_s[pl.ds(t0 + 1, tc), :] + w[0:1, :] * dc_s[pl.ds(t0 + 2, tc), :]
            dp_ref[1, pl.ds(t0, tc), :] = (dpv * ax_ref[pl.ds(t0, tc), :]).astype(dp_ref.dtype)
            dp_ref[2, pl.ds(t0, tc), :] = (dpv * cg_ref[pl.ds(t0, tc), :]).astype(dp_ref.dtype)
            for k in range(3):
                dw[k] = dw[k] + jnp.sum(dc * p_s[pl.ds(HALO + t0 - (2 - k), tc), :], axis=0, keepdims=True)
        dw_ref[...] = _rows_to_tile(dw)

    return pl.pallas_call(
        body, name="mixer_a_bwd", grid=(nblk,),
        in_specs=[_seg_spec(T, 0, nblk), _seg_spec(T, 1, nblk), _seg_spec(T, 2, nblk),
                  pl.BlockSpec((3, LANES), lambda c: (0, c)), _seg_spec(T, 0, nblk)],
        out_specs=[pl.BlockSpec((3, T, LANES), lambda c: (0, 0, c)),
                   pl.BlockSpec((None, SUBLANES, LANES), lambda c: (c, 0, 0))],
        out_shape=[jax.ShapeDtypeStruct((3, T, C), BF16), jax.ShapeDtypeStruct((nblk, SUBLANES, LANES), F32)],
        scratch_shapes=[pltpu.VMEM((T + HALO, LANES), F32), pltpu.VMEM((T + HALO, LANES), F32)],
        compiler_params=_params(("parallel",)))(proj, proj, proj, conv_a, dy)


def _rg_gates(xr, wa, ba, wx, bx, ls):
    xb = xr.astype(BF16)
    r = jax.nn.sigmoid(jnp.dot(xb, wa, preferred_element_type=F32) + ba)
    i = jax.nn.sigmoid(jnp.dot(xb, wx, preferred_element_type=F32) + bx)
    log_a = LRU_C * r * ls
    a = jnp.exp(log_a)
    mult = jnp.sqrt(_one_minus_exp(2.0 * log_a))
    return r, i, a, mult


def _conv4(xh_s, cw, bias, t0, tc):
    return (cw[3:4, :] * xh_s[pl.ds(HALO + t0, tc), :] + cw[2:3, :] * xh_s[pl.ds(HALO + t0 - 1, tc), :]
            + cw[1:2, :] * xh_s[pl.ds(HALO + t0 - 2, tc), :] + cw[0:1, :] * xh_s[pl.ds(HALO + t0 - 3, tc), :] + bias)


def _mixer_b_specs(T, nblk):
    vec = pl.BlockSpec((1, LANES), lambda c: (0, c))
    mat = pl.BlockSpec((None, LANES, LANES), lambda c: (c, 0, 0))
    return [_seg_spec(T, 3, nblk), _seg_spec(T, 4, nblk), pl.BlockSpec((4, LANES), lambda c: (0, c)),
            vec, mat, vec, mat, vec, vec]


def _mixer_b_fwd(proj, conv_b, bias, wa, ba, wx, bx, lam):
    _, T, C = proj.shape
    nblk = C // LANES
    chunks = _chunks(T)

    def body(gate_ref, x_ref, cw_ref, cb_ref, wa_ref, ba_ref, wx_ref, bx_ref, lam_ref, y_ref, xh_s, a_s, b_s):
        xh_s[pl.ds(0, HALO), :] = jnp.zeros((HALO, LANES), F32)
        for t0, tc in chunks:
            xh_s[pl.ds(HALO + t0, tc), :] = x_ref[pl.ds(t0, tc), :]
        cw, bias_v = cw_ref[...], cb_ref[...]
        ls = _log_sigmoid(lam_ref[...])
        for t0, tc in chunks:
            xr = _conv4(xh_s, cw, bias_v, t0, tc)
            r, i, a, mult = _rg_gates(xr, wa_ref[...], ba_ref[...], wx_ref[...], bx_ref[...], ls)
            ac, hc = _tile_scan(a, mult * i * xr, reverse=False)
            a_s[pl.ds(t0, tc), :] = ac
            b_s[pl.ds(t0, tc), :] = hc
        _carry_scan(a_s, b_s, T, reverse=False)
        for t0, tc in chunks:
            y_ref[pl.ds(t0, tc), :] = (b_s[pl.ds(t0, tc), :] * _gelu(gate_ref[pl.ds(t0, tc), :])).astype(y_ref.dtype)

    return pl.pallas_call(
        body, name="mixer_b_fwd", grid=(nblk,), in_specs=_mixer_b_specs(T, nblk),
        out_specs=pl.BlockSpec((T, LANES), lambda c: (0, c)),
        out_shape=jax.ShapeDtypeStruct((T, C), BF16),
        scratch_shapes=[pltpu.VMEM((T + HALO, LANES), F32), pltpu.VMEM((T, LANES), F32), pltpu.VMEM((T, LANES), F32)],
        compiler_params=_params(("parallel",)))(proj, proj, conv_b, bias, wa, ba, wx, bx, lam)


_ROW_CONV, _ROW_BIAS, _ROW_BA, _ROW_BX, _ROW_LAM = 0, 4, 5, 6, 7


def _mixer_b_bwd(proj, conv_b, bias, wa, ba, wx, bx, lam, dy):
    _, T, C = proj.shape
    nblk = C // LANES
    chunks = _chunks(T)

    def body(gate_ref, x_ref, cw_ref, cb_ref, wa_ref, ba_ref, wx_ref, bx_ref, lam_ref, dy_ref,
             dp_ref, sm_ref, dwa_ref, dwx_ref, xh_s, xr_s, r_s, i_s, a_s, h_s, sa_s, sb_s, dx_s):
        zero_halo = jnp.zeros((HALO, LANES), F32)
        xh_s[pl.ds(0, HALO), :] = zero_halo
        h_s[pl.ds(0, HALO), :] = zero_halo
        a_s[pl.ds(T, HALO), :] = zero_halo
        dx_s[pl.ds(T, HALO), :] = zero_halo
        for t0, tc in chunks:
            xh_s[pl.ds(HALO + t0, tc), :] = x_ref[pl.ds(t0, tc), :]
        cw, bias_v = cw_ref[...], cb_ref[...]
        lam_v = lam_ref[...]
        ls = _log_sigmoid(lam_v)
        wa_v, wx_v, ba_v, bx_v = wa_ref[...], wx_ref[...], ba_ref[...], bx_ref[...]
        for t0, tc in chunks:
            xr = _conv4(xh_s, cw, bias_v, t0, tc)
            r, i, a, mult = _rg_gates(xr, wa_v, ba_v, wx_v, bx_v, ls)
            xr_s[pl.ds(t0, tc), :] = xr
            r_s[pl.ds(t0, tc), :] = r
            i_s[pl.ds(t0, tc), :] = i
            a_s[pl.ds(t0, tc), :] = a
            ac, hc = _tile_scan(a, mult * i * xr, reverse=False)
            sa_s[pl.ds(t0, tc), :] = ac
            sb_s[pl.ds(t0, tc), :] = hc
        _carry_scan(sa_s, sb_s, T, reverse=False)
        for t0, tc in chunks:
            h_s[pl.ds(HALO + t0, tc), :] = sb_s[pl.ds(t0, tc), :]
        for t0, tc in chunks:
            gv = gate_ref[pl.ds(t0, tc), :]
            dyv = dy_ref[pl.ds(t0, tc), :]
            dp_ref[0, pl.ds(t0, tc), :] = (dyv * h_s[pl.ds(HALO + t0, tc), :] * _gelu_grad(gv)).astype(dp_ref.dtype)
            ac, gc = _tile_scan(a_s[pl.ds(t0 + 1, tc), :], dyv * _gelu(gv), reverse=True)
            sa_s[pl.ds(t0, tc), :] = ac
            sb_s[pl.ds(t0, tc), :] = gc
        _carry_scan(sa_s, sb_s, T, reverse=True)
        acc = {k: jnp.zeros((1, LANES), F32) for k in ("bias", "ba", "bx", "lam")}
        dwa = jnp.zeros((LANES, LANES), F32)
        dwx = jnp.zeros((LANES, LANES), F32)
        for t0, tc in chunks:
            dht = sb_s[pl.ds(t0, tc), :]
            xr, r, i, a = xr_s[pl.ds(t0, tc), :], r_s[pl.ds(t0, tc), :], i_s[pl.ds(t0, tc), :], a_s[pl.ds(t0, tc), :]
            mult = jnp.sqrt(_one_minus_exp(2.0 * LRU_C * r * ls))
            da = dht * h_s[pl.ds(HALO + t0 - 1, tc), :]
            dmult = dht * i * xr
            di = dht * mult * xr
            dlog_a = da * a - dmult * a * a / mult
            dpa = dlog_a * (LRU_C * ls) * r * (1.0 - r)
            dpx = di * i * (1.0 - i)
            acc["lam"] = acc["lam"] + jnp.sum(dlog_a * r, axis=0, keepdims=True)
            acc["ba"] = acc["ba"] + jnp.sum(dpa, axis=0, keepdims=True)
            acc["bx"] = acc["bx"] + jnp.sum(dpx, axis=0, keepdims=True)
            xb, dpab, dpxb = xr.astype(BF16), dpa.astype(BF16), dpx.astype(BF16)
            dwa = dwa + lax.dot_general(xb, dpab, DIMS_TN, preferred_element_type=F32)
            dwx = dwx + lax.dot_general(xb, dpxb, DIMS_TN, preferred_element_type=F32)
            dxr = (dht * mult * i + lax.dot_general(dpab, wa_v, DIMS_NT, preferred_element_type=F32)
                   + lax.dot_general(dpxb, wx_v, DIMS_NT, preferred_element_type=F32))
            acc["bias"] = acc["bias"] + jnp.sum(dxr, axis=0, keepdims=True)
            dx_s[pl.ds(t0, tc), :] = dxr
        dcw = [jnp.zeros((1, LANES), F32) for _ in range(4)]
        for t0, tc in chunks:
            dxr = dx_s[pl.ds(t0, tc), :]
            dxin = (cw[3:4, :] * dxr + cw[2:3, :] * dx_s[pl.ds(t0 + 1, tc), :] + cw[1:2, :] * dx_s[pl.ds(t0 + 2, tc), :]
                    + cw[0:1, :] * dx_s[pl.ds(t0 + 3, tc), :])
            dp_ref[1, pl.ds(t0, tc), :] = dxin.astype(dp_ref.dtype)
            for k in range(4):
                dcw[k] = dcw[k] + jnp.sum(dxr * xh_s[pl.ds(HALO + t0 - (3 - k), tc), :], axis=0, keepdims=True)
        dlam = acc["lam"] * LRU_C * jax.nn.sigmoid(-lam_v)
        sm_ref[...] = _rows_to_tile(dcw + [acc["bias"], acc["ba"], acc["bx"], dlam])
        dwa_ref[...] = dwa
        dwx_ref[...] = dwx

    big = lambda halo: pltpu.VMEM((T + halo, LANES), F32)
    mat = pl.BlockSpec((None, LANES, LANES), lambda c: (c, 0, 0))
    return pl.pallas_call(
        body, name="mixer_b_bwd", grid=(nblk,),
        in_specs=_mixer_b_specs(T, nblk) + [_seg_spec(T, 1, nblk)],
        out_specs=[pl.BlockSpec((2, T, LANES), lambda c: (0, 0, c)),
                   pl.BlockSpec((None, SUBLANES, LANES), lambda c: (c, 0, 0)), mat, mat],
        out_shape=[jax.ShapeDtypeStruct((2, T, C), BF16), jax.ShapeDtypeStruct((nblk, SUBLANES, LANES), F32),
                   jax.ShapeDtypeStruct((nblk, LANES, LANES), F32), jax.ShapeDtypeStruct((nblk, LANES, LANES), F32)],
        scratch_shapes=[big(HALO), big(0), big(0), big(0), big(HALO), big(HALO), big(0), big(0), big(HALO)],
        compiler_params=_params(("parallel",)))(proj, proj, conv_b, bias, wa, ba, wx, bx, lam, dy)


ATT_BLOCK = 128


def _split_dot(x, m):
    hi = x.astype(BF16)
    lo = (x - hi.astype(F32)).astype(BF16)
    return jnp.dot(hi, m, preferred_element_type=F32) + jnp.dot(lo, m, preferred_element_type=F32)


def _att_scores(q, k, q0, k0, scale):
    z = lax.dot_general(q, k, DIMS_NT, preferred_element_type=F32) * scale
    row = lax.broadcasted_iota(jnp.int32, z.shape, 0)
    col = lax.broadcasted_iota(jnp.int32, z.shape, 1)
    mask = (k0 + col) < (q0 + row)
    n = jnp.where(mask, -(jnp.maximum(z, 0.0) + jnp.log(1.0 + jnp.exp(-jnp.abs(z)))), 0.0)
    return z, mask, n


def _head_spec(T, seg, heads):
    return pl.BlockSpec((None, T, ATT_HEAD_DIM), lambda h: (seg, 0, h))


def _attention_fwd(qkv):
    _, T, D = qkv.shape
    heads = D // ATT_HEAD_DIM
    nq = T // ATT_BLOCK
    assert nq <= LANES
    scale = 1.0 / math.sqrt(ATT_HEAD_DIM)

    def body(q_ref, k_ref, v_ref, o_ref, r_ref, acc_s, run_s):
        rr = lax.broadcasted_iota(jnp.int32, (ATT_BLOCK, ATT_BLOCK), 0)
        cc = lax.broadcasted_iota(jnp.int32, (ATT_BLOCK, ATT_BLOCK), 1)
        upper = jnp.where(rr > cc, 1.0, 0.0).astype(BF16)
        lane = lax.broadcasted_iota(jnp.int32, (ATT_BLOCK, LANES), 1)

        def q_loop(qb, _):
            q0 = pl.multiple_of(qb * ATT_BLOCK, ATT_BLOCK)
            q = q_ref[pl.ds(q0, ATT_BLOCK), :]
            acc_s[...] = jnp.zeros_like(acc_s)
            run_s[...] = jnp.zeros_like(run_s)
            r_ref[pl.ds(q0, ATT_BLOCK), :] = jnp.zeros((ATT_BLOCK, LANES), F32)

            def k_loop(it, _):
                kb = qb - it
                k0 = pl.multiple_of(kb * ATT_BLOCK, ATT_BLOCK)
                z, mask, n = _att_scores(q, k_ref[pl.ds(k0, ATT_BLOCK), :], q0, k0, scale)
                run = run_s[...]
                w = jnp.where(mask, jnp.exp(z + n + _split_dot(n, upper) + run), 0.0)
                acc_s[...] += jnp.dot(w.astype(BF16), v_ref[pl.ds(k0, ATT_BLOCK), :], preferred_element_type=F32)
                r_ref[pl.ds(q0, ATT_BLOCK), :] = jnp.where(lane == kb, run, r_ref[pl.ds(q0, ATT_BLOCK), :])
                run_s[...] = run + jnp.sum(n, axis=1, keepdims=True)
                return 0

            lax.fori_loop(0, qb + 1, k_loop, 0)
            o_ref[pl.ds(q0, ATT_BLOCK), :] = acc_s[...].astype(o_ref.dtype)
            return 0

        lax.fori_loop(0, nq, q_loop, 0)

    return pl.pallas_call(
        body, name="attention_fwd", grid=(heads,),
        in_specs=[_head_spec(T, 0, heads), _head_spec(T, 1, heads), _head_spec(T, 2, heads)],
        out_specs=[pl.BlockSpec((T, ATT_HEAD_DIM), lambda h: (0, h)), pl.BlockSpec((None, T, LANES), lambda h: (h, 0, 0))],
        out_shape=[jax.ShapeDtypeStruct((T, D), BF16), jax.ShapeDtypeStruct((heads, T, LANES), F32)],
        scratch_shapes=[pltpu.VMEM((ATT_BLOCK, ATT_HEAD_DIM), F32), pltpu.VMEM((ATT_BLOCK, LANES), F32)],
        compiler_params=_params(("parallel",)))(qkv, qkv, qkv)


def _attention_bwd(qkv, do, rmat):
    _, T, D = qkv.shape
    heads = D // ATT_HEAD_DIM
    nq = T // ATT_BLOCK
    scale = 1.0 / math.sqrt(ATT_HEAD_DIM)

    def body(q_ref, k_ref, v_ref, do_ref, r_ref, dqkv_ref, dk_s, dv_s, dq_s, left_s):
        rr = lax.broadcasted_iota(jnp.int32, (ATT_BLOCK, ATT_BLOCK), 0)
        cc = lax.broadcasted_iota(jnp.int32, (ATT_BLOCK, ATT_BLOCK), 1)
        upper = jnp.where(rr > cc, 1.0, 0.0).astype(BF16)
        lower = jnp.where(rr < cc, 1.0, 0.0).astype(BF16)
        lane = lax.broadcasted_iota(jnp.int32, (ATT_BLOCK, LANES), 1)
        dk_s[...] = jnp.zeros_like(dk_s)
        dv_s[...] = jnp.zeros_like(dv_s)

        def q_loop(qb, _):
            q0 = pl.multiple_of(qb * ATT_BLOCK, ATT_BLOCK)
            q = q_ref[pl.ds(q0, ATT_BLOCK), :]
            dov = do_ref[pl.ds(q0, ATT_BLOCK), :]
            dq_s[...] = jnp.zeros_like(dq_s)
            left_s[...] = jnp.zeros_like(left_s)

            def k_loop(kb, _):
                k0 = pl.multiple_of(kb * ATT_BLOCK, ATT_BLOCK)
                kv = k_ref[pl.ds(k0, ATT_BLOCK), :]
                vv = v_ref[pl.ds(k0, ATT_BLOCK), :]
                z, mask, n = _att_scores(q, kv, q0, k0, scale)
                run = jnp.sum(jnp.where(lane == kb, r_ref[pl.ds(q0, ATT_BLOCK), :], 0.0), axis=1, keepdims=True)
                s = z + n
                w = jnp.where(mask, jnp.exp(s + _split_dot(n, upper) + run), 0.0)
                e = w * lax.dot_general(dov, vv, DIMS_NT, preferred_element_type=F32)
                left = left_s[...]
                before = left + _split_dot(e, lower)
                sig = jnp.exp(s)
                dz = (jnp.where(mask, e * (1.0 - sig) - before * sig, 0.0) * scale).astype(BF16)
                dq_s[...] += jnp.dot(dz, kv, preferred_element_type=F32)
                dk_s[pl.ds(k0, ATT_BLOCK), :] += lax.dot_general(dz, q, DIMS_TN, preferred_element_type=F32)
                dv_s[pl.ds(k0, ATT_BLOCK), :] += lax.dot_general(w.astype(BF16), dov, DIMS_TN, preferred_element_type=F32)
                left_s[...] = left + jnp.sum(e, axis=1, keepdims=True)
                return 0

            lax.fori_loop(0, qb + 1, k_loop, 0)
            dqkv_ref[0, pl.ds(q0, ATT_BLOCK), :] = dq_s[...].astype(dqkv_ref.dtype)
            return 0

        lax.fori_loop(0, nq, q_loop, 0)
        dqkv_ref[1, :, :] = dk_s[...].astype(dqkv_ref.dtype)
        dqkv_ref[2, :, :] = dv_s[...].astype(dqkv_ref.dtype)

    return pl.pallas_call(
        body, name="attention_bwd", grid=(heads,),
        in_specs=[_head_spec(T, 0, heads), _head_spec(T, 1, heads), _head_spec(T, 2, heads),
                  pl.BlockSpec((T, ATT_HEAD_DIM), lambda h: (0, h)), pl.BlockSpec((None, T, LANES), lambda h: (h, 0, 0))],
        out_specs=pl.BlockSpec((3, T, ATT_HEAD_DIM), lambda h: (0, 0, h)),
        out_shape=jax.ShapeDtypeStruct((3, T, D), BF16),
        scratch_shapes=[pltpu.VMEM((T, ATT_HEAD_DIM), F32), pltpu.VMEM((T, ATT_HEAD_DIM), F32),
                        pltpu.VMEM((ATT_BLOCK, ATT_HEAD_DIM), F32), pltpu.VMEM((ATT_BLOCK, LANES), F32)],
        compiler_params=_params(("parallel",)))(qkv, qkv, qkv, do, rmat)


def _block_diag_pairs(w):
    h = w.shape[0]
    wp = w.reshape(h // 2, 2, RG_HEAD_DIM, RG_HEAD_DIM)
    z = jnp.zeros_like(wp[:, 0])
    top = jnp.concatenate([wp[:, 0], z], axis=2)
    bot = jnp.concatenate([z, wp[:, 1]], axis=2)
    return jnp.concatenate([top, bot], axis=1)


def _diag_pairs(g):
    n = g.shape[0]
    a = g[:, :RG_HEAD_DIM, :RG_HEAD_DIM]
    b = g[:, RG_HEAD_DIM:, RG_HEAD_DIM:]
    return jnp.stack([a, b], axis=1).reshape(2 * n, RG_HEAD_DIM, RG_HEAD_DIM)


def _mlp_fwd(tag, h, w_up, layer, w_down):
    T, D = h.shape
    fb = w_up.shape[3]
    F = fb * N_DEV
    tm, tn, tk = _tile(T, 1024), _tile(fb, 1024), _tile(D, 512)
    nb = fb // tn

    def up_epilogue(u):
        r = jnp.maximum(u, 0.0)
        return u, r * r

    o_spec = pl.BlockSpec((tm, tn), lambda i, j, k: (i, j))
    u, act = _matmul(
        f"mlp_up_{tag}",
        [(h, pl.BlockSpec((tm, tk), lambda i, j, k: (i, k))),
         (w_up, pl.BlockSpec((None, None, tk, tn), lambda i, j, k: (j // nb, layer, k, j % nb)))],
        [(jax.ShapeDtypeStruct((T, F), BF16), o_spec), (jax.ShapeDtypeStruct((T, F), BF16), o_spec)],
        (T // tm, F // tn, D // tk), DIMS_NN, (tm, tn), up_epilogue)
    tn2, tk2 = _tile(D, 1024), _tile(fb, 512)
    nkb = fb // tk2
    (m,) = _matmul(
        f"mlp_down_{tag}",
        [(act, pl.BlockSpec((tm, tk2), lambda i, j, k: (i, k))),
         (w_down, pl.BlockSpec((None, None, tk2, tn2), lambda i, j, k: (k // nkb, layer, k % nkb, j)))],
        [(jax.ShapeDtypeStruct((T, D), F32), pl.BlockSpec((tm, tn2), lambda i, j, k: (i, j)))],
        (T // tm, D // tn2, F // tk2), DIMS_NN, (tm, tn2))
    return u, act, m


def _mlp_bwd(tag, h, u, act, dm, w_up, layer, w_down):
    T, D = h.shape
    fb = w_up.shape[3]
    F = fb * N_DEV
    dw_down = _mm_tn(f"mlp_down_dw_{tag}", act, dm, BF16)
    tm, tn, tk = _tile(T, 1024), _tile(fb, 1024), _tile(D, 512)
    nb = fb // tn
    o_spec = pl.BlockSpec((tm, tn), lambda i, j, k: (i, j))
    (du,) = _matmul(
        f"mlp_down_dx_{tag}",
        [(dm, pl.BlockSpec((tm, tk), lambda i, j, k: (i, k))),
         (w_down, pl.BlockSpec((None, None, tn, tk), lambda i, j, k: (j // nb, layer, j % nb, k))),
         (u, o_spec)],
        [(jax.ShapeDtypeStruct((T, F), BF16), o_spec)],
        (T // tm, F // tn, D // tk), DIMS_NT, (tm, tn),
        lambda r, uv: (r * (2.0 * jnp.maximum(uv.astype(F32), 0.0)),))
    dw_up = _mm_tn(f"mlp_up_dw_{tag}", h, du, BF16, out_blocks=N_DEV)
    tn2, tk2 = _tile(D, 1024), _tile(fb, 512)
    nkb = fb // tk2
    (dh,) = _matmul(
        f"mlp_up_dx_{tag}",
        [(du, pl.BlockSpec((tm, tk2), lambda i, j, k: (i, k))),
         (w_up, pl.BlockSpec((None, None, tn2, tk2), lambda i, j, k: (k // nkb, layer, j, k % nkb)))],
        [(jax.ShapeDtypeStruct((T, D), F32), pl.BlockSpec((tm, tn2), lambda i, j, k: (i, j)))],
        (T // tm, D // tn2, F // tk2), DIMS_NT, (tm, tn2))
    return dw_down, dw_up, dh


def _local_step(x, target, gains, w_in, conv_a, conv_b, conv_b_bias, rg_w_a, rg_b_a, rg_w_x, rg_b_x, rg_lambda,
                w_out, w_qkv, w_o, w_up, w_down):
    T, D = x.shape
    g = lambda l, i: gains[l, i][None, :]
    wa_p = _block_diag_pairs(rg_w_a).astype(BF16)
    wx_p = _block_diag_pairs(rg_w_x).astype(BF16)

    h0 = _norm_fwd("norm_in", x, g(0, 0))
    (proj,) = _mm_nn("w_in_fwd", h0, w_in, F32, out_seg=5)
    y_a = _mixer_a_fwd(proj, conv_a)
    y_b = _mixer_b_fwd(proj, conv_b, conv_b_bias, wa_p, rg_b_a, wx_p, rg_b_x, rg_lambda)
    y = jnp.stack([y_a, y_b], axis=0)
    (mix0,) = _mm_nn("w_out_fwd", y, w_out, F32, a_seg=2)
    x1, h1 = _resid_norm("resid_mix0", x, mix0, g(0, 1), g(0, 2))
    u0, act0, m0 = _mlp_fwd("l0", h1, w_up, 0, w_down)
    x2, h2 = _resid_norm("resid_mlp0", x1, m0, g(0, 3), g(1, 0))
    (qkv,) = _mm_nn("w_qkv_fwd", h2, w_qkv, BF16, out_seg=3)
    o, rmat = _attention_fwd(qkv)
    (mix1,) = _mm_nn("w_o_fwd", o, w_o, F32)
    x3, h3 = _resid_norm("resid_mix1", x2, mix1, g(1, 1), g(1, 2))
    u1, act1, m1 = _mlp_fwd("l1", h3, w_up, 1, w_down)
    dx4, sq = _final_loss("loss", x3, m1, g(1, 3), target)

    dm1, dg13 = _norm_bwd("norm_bwd_m1", m1, g(1, 3), dx4, None, BF16)
    dw_down1, dw_up1, dh3 = _mlp_bwd("l1", h3, u1, act1, dm1, w_up, 1, w_down)
    dx3, dg12 = _norm_bwd("norm_bwd_x3", x3, g(1, 2), dh3, dx4, F32)
    dmix1, dg11 = _norm_bwd("norm_bwd_mix1", mix1, g(1, 1), dx3, None, BF16)
    dw_o = _mm_tn("w_o_dw", o, dmix1, BF16)
    do = _mm_nt("w_o_dx", dmix1, w_o, BF16)
    dqkv = _attention_bwd(qkv, do, rmat)
    dw_qkv = _mm_tn("w_qkv_dw", h2, dqkv, BF16, b_seg=3)
    dh2 = _mm_nt("w_qkv_dx", dqkv, w_qkv, F32, a_seg=3)
    dx2, dg10 = _norm_bwd("norm_bwd_x2", x2, g(1, 0), dh2, dx3, F32)
    dm0, dg03 = _norm_bwd("norm_bwd_m0", m0, g(0, 3), dx2, None, BF16)
    dw_down0, dw_up0, dh1 = _mlp_bwd("l0", h1, u0, act0, dm0, w_up, 0, w_down)
    dx1, dg02 = _norm_bwd("norm_bwd_x1", x1, g(0, 2), dh1, dx2, F32)
    dmix0, dg01 = _norm_bwd("norm_bwd_mix0", mix0, g(0, 1), dx1, None, BF16)
    dw_out = _mm_tn("w_out_dw", y, dmix0, BF16, a_seg=2)
    dy = _mm_nt("w_out_dx", dmix0, w_out, F32, out_seg=2)
    dproj_a, dconv_a = _mixer_a_bwd(proj, conv_a, dy)
    dproj_b, sm_b, dwa_p, dwx_p = _mixer_b_bwd(proj, conv_b, conv_b_bias, wa_p, rg_b_a, wx_p, rg_b_x, rg_lambda, dy)
    dproj = jnp.concatenate([dproj_a, dproj_b], axis=0)
    dw_in = _mm_tn("w_in_dw", h0, dproj, BF16, b_seg=5)
    dh0 = _mm_nt("w_in_dx", dproj, w_in, F32, a_seg=5)
    dx0, dg00 = _norm_bwd("norm_bwd_x0", x, g(0, 0), dh0, dx1, F32)

    C = D // 2
    lanes_to_vec = lambda t, row: t[:, row, :].reshape(1, C)
    small = {
        "norm_gains": jnp.concatenate([dg00, dg01, dg02, dg03, dg10, dg11, dg12, dg13], axis=0).reshape(2, 4, D),
        "conv_a": jnp.transpose(dconv_a[:, :3, :], (1, 0, 2)).reshape(3, C),
        "conv_b": jnp.transpose(sm_b[:, :4, :], (1, 0, 2)).reshape(4, C),
        "conv_b_bias": lanes_to_vec(sm_b, _ROW_BIAS),
        "rg_w_a": _diag_pairs(dwa_p),
        "rg_b_a": lanes_to_vec(sm_b, _ROW_BA),
        "rg_w_x": _diag_pairs(dwx_p),
        "rg_b_x": lanes_to_vec(sm_b, _ROW_BX),
        "rg_lambda": lanes_to_vec(sm_b, _ROW_LAM),
    }
    big = {"w_in": dw_in, "w_out": dw_out, "w_qkv": dw_qkv, "w_o": dw_o,
           "w_up": (dw_up0, dw_up1), "w_down": (dw_down0, dw_down1)}
    return sq[0, 0], dx0, small, big


def _my_index():
    return 4 * lax.axis_index("x") + 2 * lax.axis_index("y") + lax.axis_index("c")


def _peers():
    x, y, c = lax.axis_index("x"), lax.axis_index("y"), lax.axis_index("c")
    out = []
    for k in range(1, N_DEV):
        px = x ^ ((k >> 2) & 1)
        py = y ^ ((k >> 1) & 1)
        pc = c ^ (k & 1)
        out.append(((px, py, pc), 4 * px + 2 * py + pc))
    return out


_ANY = pl.BlockSpec(memory_space=pl.ANY)


def _all_gather(name, shards):
    n = len(shards)

    def body(*refs):
        srcs, dsts = refs[:n], refs[n:2 * n]
        send_sems, recv_sems, local_sems = refs[2 * n:]
        me = _my_index()
        peers = _peers()
        copies = []
        for a in range(n):
            lc = pltpu.make_async_copy(srcs[a], dsts[a].at[me], local_sems.at[a])
            lc.start()
            copies.append(lc)
        remote = []
        for a in range(n):
            for k, (pos, _) in enumerate(peers):
                cp = pltpu.make_async_remote_copy(
                    src_ref=srcs[a], dst_ref=dsts[a].at[me], send_sem=send_sems.at[a, k], recv_sem=recv_sems.at[a, k],
                    device_id=pos, device_id_type=MESH)
                cp.start()
                remote.append(cp)
        for a in range(n):
            for k, (pos, idx) in enumerate(peers):
                pltpu.make_async_remote_copy(
                    src_ref=srcs[a], dst_ref=dsts[a].at[idx], send_sem=send_sems.at[a, k], recv_sem=recv_sems.at[a, k],
                    device_id=pos, device_id_type=MESH).wait_recv()
        for cp in remote:
            cp.wait_send()
        for lc in copies:
            lc.wait()

    return pl.pallas_call(
        body, name=name,
        in_specs=[_ANY] * n, out_specs=[_ANY] * n,
        out_shape=[jax.ShapeDtypeStruct((N_DEV,) + s.shape, s.dtype) for s in shards],
        scratch_shapes=[pltpu.SemaphoreType.DMA((n, N_DEV - 1)), pltpu.SemaphoreType.DMA((n, N_DEV - 1)),
                        pltpu.SemaphoreType.DMA((n,))],
    )(*shards)


def _exchange_blocks(name, pieces, lands):
    n, nl = len(pieces), len(lands)

    def body(*refs):
        srcs, dsts = refs[:n], refs[n:n + nl]
        send_sems, recv_sems, local_sems = refs[n + nl:]
        me = _my_index()
        peers = _peers()

        def land(a, slot):
            _, lid, lead = pieces[a]
            r = dsts[lid].at[slot]
            for i in lead:
                r = r.at[i]
            return r

        copies = []
        for a in range(n):
            lc = pltpu.make_async_copy(srcs[a].at[me], land(a, me), local_sems.at[a])
            lc.start()
            copies.append(lc)
        remote = []
        for a in range(n):
            for k, (pos, idx) in enumerate(peers):
                cp = pltpu.make_async_remote_copy(
                    src_ref=srcs[a].at[idx], dst_ref=land(a, me), send_sem=send_sems.at[a, k],
                    recv_sem=recv_sems.at[a, k], device_id=pos, device_id_type=MESH)
                cp.start()
                remote.append(cp)
        for a in range(n):
            for k, (pos, idx) in enumerate(peers):
                pltpu.make_async_remote_copy(
                    src_ref=srcs[a].at[idx], dst_ref=land(a, idx), send_sem=send_sems.at[a, k],
                    recv_sem=recv_sems.at[a, k], device_id=pos, device_id_type=MESH).wait_recv()
        for cp in remote:
            cp.wait_send()
        for lc in copies:
            lc.wait()

    return pl.pallas_call(
        body, name=name,
        in_specs=[_ANY] * n, out_specs=[_ANY] * nl,
        out_shape=lands,
        scratch_shapes=[pltpu.SemaphoreType.DMA((n, N_DEV - 1)), pltpu.SemaphoreType.DMA((n, N_DEV - 1)),
                        pltpu.SemaphoreType.DMA((n,))],
    )(*[p[0] for p in pieces])


def _adamw_math(w, g, m, v):
    m = ADAM_B1 * m + (1.0 - ADAM_B1) * g
    v = ADAM_B2 * v + (1.0 - ADAM_B2) * (g * g)
    m_hat = m / (1.0 - ADAM_B1 ** ADAM_STEP)
    v_hat = v / (1.0 - ADAM_B2 ** ADAM_STEP)
    delta = -ADAM_LR * (m_hat / (jnp.sqrt(v_hat) + ADAM_EPS) + ADAM_WD * w)
    return delta, m, v


def _sum_slots(ref):
    g = ref[0].astype(F32)
    for s in range(1, N_DEV):
        g = g + ref[s].astype(F32)
    return g


def _adamw_big(name, land, w, m, v):
    R, C = w.shape
    tr = _tile(R, max(LANES, (256 * 1024) // C))

    def body(l_ref, w_ref, m_ref, v_ref, g_ref, d_ref, nm_ref, nv_ref):
        g = _sum_slots(l_ref)
        d, nm, nv = _adamw_math(w_ref[...], g, m_ref[...], v_ref[...])
        g_ref[...] = g
        d_ref[...] = d
        nm_ref[...] = nm
        nv_ref[...] = nv

    row = pl.BlockSpec((tr, C), lambda i: (i, 0))
    return pl.pallas_call(
        body, name=name, grid=(R // tr,),
        in_specs=[pl.BlockSpec((N_DEV, tr, C), lambda i: (0, i, 0)), row, row, row],
        out_specs=[row] * 4, out_shape=[jax.ShapeDtypeStruct((R, C), F32)] * 4,
        compiler_params=_params(("parallel",)))(land, w, m, v)


def _sum8(name, slots):
    _, R, C = slots.shape

    def body(s_ref, o_ref):
        o_ref[...] = _sum_slots(s_ref)

    return pl.pallas_call(body, name=name, out_shape=jax.ShapeDtypeStruct((R, C), F32))(slots)


def _adamw_small(name, g, w, m, v):
    def body(g_ref, w_ref, m_ref, v_ref, d_ref, nm_ref, nv_ref):
        d, nm, nv = _adamw_math(w_ref[...], g_ref[...], m_ref[...], v_ref[...])
        d_ref[...] = d
        nm_ref[...] = nm
        nv_ref[...] = nv

    return pl.pallas_call(body, name=name, out_shape=[jax.ShapeDtypeStruct(w.shape, F32)] * 3)(g, w, m, v)


def _pack_rows(arrs):
    parts, spans, r0 = [], [], 0
    for a in arrs:
        flat = a.astype(F32).reshape(-1)
        rows = -(-flat.shape[0] // LANES)
        rows = -(-rows // SUBLANES) * SUBLANES
        flat = jnp.pad(flat, (0, rows * LANES - flat.shape[0]))
        parts.append(flat.reshape(rows, LANES))
        spans.append((r0, rows, a.shape))
        r0 += rows
    return jnp.concatenate(parts, axis=0), spans


def _unpack_rows(buf, span):
    r0, rows, shape = span
    n = math.prod(shape)
    return buf[..., r0:r0 + rows, :].reshape(buf.shape[:-2] + (rows * LANES,))[..., :n].reshape(buf.shape[:-2] + shape)


def _col_blocks(w, n_blocks):
    K, N = w.shape
    return jnp.transpose(w.reshape(K, n_blocks, N // n_blocks), (1, 0, 2))


def _from_col_blocks(wb):
    B, K, n = wb.shape
    return jnp.transpose(wb, (1, 0, 2)).reshape(K, B * n)


def kernel(x, norm_gains, hyb_w_in, hyb_conv_a, hyb_conv_b, hyb_conv_b_bias, hyb_rg_w_a, hyb_rg_b_a, hyb_rg_w_x, hyb_rg_b_x, hyb_rg_lambda, hyb_w_out, sb_w_qkv, sb_w_o, mlp_w_up, mlp_w_down, loss_target, m_norm_gains, m_hyb_w_in, m_hyb_conv_a, m_hyb_conv_b, m_hyb_conv_b_bias, m_hyb_rg_w_a, m_hyb_rg_b_a, m_hyb_rg_w_x, m_hyb_rg_b_x, m_hyb_rg_lambda, m_hyb_w_out, m_sb_w_qkv, m_sb_w_o, m_mlp_w_up, m_mlp_w_down, v_norm_gains, v_hyb_w_in, v_hyb_conv_a, v_hyb_conv_b, v_hyb_conv_b_bias, v_hyb_rg_w_a, v_hyb_rg_b_a, v_hyb_rg_w_x, v_hyb_rg_b_x, v_hyb_rg_lambda, v_hyb_w_out, v_sb_w_qkv, v_sb_w_o, v_mlp_w_up, v_mlp_w_down):
    T, D = x.shape[1], x.shape[2]
    me = _my_index()

    small_shards, small_spans = _pack_rows([norm_gains, hyb_conv_a[0], hyb_conv_b[0]])
    (small_all,) = _all_gather("gather_small", [small_shards])
    gains_b = _unpack_rows(small_all, small_spans[0])
    gains = jnp.transpose(gains_b, (1, 2, 0, 3)).reshape(2, 4, D)
    conv_a = _from_col_blocks(_unpack_rows(small_all, small_spans[1]))
    conv_b = _from_col_blocks(_unpack_rows(small_all, small_spans[2]))

    big_shards = [hyb_w_in[0].astype(BF16), hyb_w_out[0].astype(BF16), sb_w_qkv[0].astype(BF16),
                  sb_w_o[0].astype(BF16), mlp_w_up.astype(BF16), mlp_w_down.astype(BF16)]
    g_in, g_out, g_qkv, g_o, g_up, g_down = _all_gather("gather_weights", big_shards)
    w_in = _from_col_blocks(g_in)
    w_qkv = _from_col_blocks(g_qkv)
    w_out = g_out.reshape(D, D)
    w_o = g_o.reshape(D, D)

    sq, grad_x, small, big = _local_step(
        x[0], loss_target[0], gains, w_in, conv_a, conv_b, hyb_conv_b_bias, hyb_rg_w_a[0], hyb_rg_b_a, hyb_rg_w_x[0],
        hyb_rg_b_x, hyb_rg_lambda, w_out, w_qkv, w_o, g_up, g_down)
    loss = lax.psum(0.5 * sq / D, ("x", "y", "c"))

    rows_o = D // N_DEV
    pieces = [
        (_col_blocks(big["w_in"], N_DEV), 0, ()),
        (big["w_out"].reshape(N_DEV, rows_o, D), 1, ()),
        (_col_blocks(big["w_qkv"], N_DEV), 2, ()),
        (big["w_o"].reshape(N_DEV, rows_o, D), 3, ()),
        (big["w_up"][0], 4, (0,)), (big["w_up"][1], 4, (1,)),
        (big["w_down"][0].reshape((N_DEV,) + mlp_w_down.shape[1:]), 5, (0,)),
        (big["w_down"][1].reshape((N_DEV,) + mlp_w_down.shape[1:]), 5, (1,)),
    ]
    shard_shapes = [hyb_w_in.shape[1:], hyb_w_out.shape[1:], sb_w_qkv.shape[1:], sb_w_o.shape[1:],
                    mlp_w_up.shape, mlp_w_down.shape]
    lands = _exchange_blocks("exchange_grads", pieces,
                             [jax.ShapeDtypeStruct((N_DEV,) + tuple(s), BF16) for s in shard_shapes])

    names = ["norm_gains", "hyb_w_in", "hyb_conv_a", "hyb_conv_b", "hyb_conv_b_bias", "hyb_rg_w_a", "hyb_rg_b_a",
             "hyb_rg_w_x", "hyb_rg_b_x", "hyb_rg_lambda", "hyb_w_out", "sb_w_qkv", "sb_w_o", "mlp_w_up", "mlp_w_down"]
    params = dict(zip(names, [norm_gains, hyb_w_in, hyb_conv_a, hyb_conv_b, hyb_conv_b_bias, hyb_rg_w_a, hyb_rg_b_a,
                              hyb_rg_w_x, hyb_rg_b_x, hyb_rg_lambda, hyb_w_out, sb_w_qkv, sb_w_o, mlp_w_up, mlp_w_down]))
    moms = dict(zip(names, [m_norm_gains, m_hyb_w_in, m_hyb_conv_a, m_hyb_conv_b, m_hyb_conv_b_bias, m_hyb_rg_w_a,
                            m_hyb_rg_b_a, m_hyb_rg_w_x, m_hyb_rg_b_x, m_hyb_rg_lambda, m_hyb_w_out, m_sb_w_qkv,
                            m_sb_w_o, m_mlp_w_up, m_mlp_w_down]))
    vars_ = dict(zip(names, [v_norm_gains, v_hyb_w_in, v_hyb_conv_a, v_hyb_conv_b, v_hyb_conv_b_bias, v_hyb_rg_w_a,
                             v_hyb_rg_b_a, v_hyb_rg_w_x, v_hyb_rg_b_x, v_hyb_rg_lambda, v_hyb_w_out, v_sb_w_qkv,
                             v_sb_w_o, v_mlp_w_up, v_mlp_w_down]))
    grads, deltas, new_m, new_v = {}, {}, {}, {}

    big_names = ["hyb_w_in", "hyb_w_out", "sb_w_qkv", "sb_w_o", "mlp_w_up", "mlp_w_down"]
    for nm, land in zip(big_names, lands):
        shape = params[nm].shape
        cols = shape[-1]
        flat = lambda a: a.reshape(-1, cols)
        outs = _adamw_big(f"adamw_{nm}", land.reshape(N_DEV, -1, cols), flat(params[nm]), flat(moms[nm]), flat(vars_[nm]))
        grads[nm], deltas[nm], new_m[nm], new_v[nm] = [o.reshape(shape) for o in outs]

    small_names = ["norm_gains", "hyb_conv_a", "hyb_conv_b", "hyb_conv_b_bias", "hyb_rg_w_a", "hyb_rg_b_a",
                   "hyb_rg_w_x", "hyb_rg_b_x", "hyb_rg_lambda"]
    small_keys = ["norm_gains", "conv_a", "conv_b", "conv_b_bias", "rg_w_a", "rg_b_a", "rg_w_x", "rg_b_x", "rg_lambda"]
    sg_buf, sg_spans = _pack_rows([small[k] for k in small_keys])
    (sg_all,) = _all_gather("gather_small_grads", [sg_buf])
    sg_sum = _sum8("sum_small_grads", sg_all)
    full = {nm: _unpack_rows(sg_sum, sp) for nm, sp in zip(small_names, sg_spans)}
    cb = (D // 2) // N_DEV
    small_grads = {
        "norm_gains": lax.dynamic_slice_in_dim(full["norm_gains"], me * (D // N_DEV), D // N_DEV, axis=2),
        "hyb_conv_a": lax.dynamic_slice_in_dim(full["hyb_conv_a"], me * cb, cb, axis=1)[None],
        "hyb_conv_b": lax.dynamic_slice_in_dim(full["hyb_conv_b"], me * cb, cb, axis=1)[None],
        "hyb_conv_b_bias": full["hyb_conv_b_bias"],
        "hyb_rg_w_a": full["hyb_rg_w_a"][None],
        "hyb_rg_b_a": full["hyb_rg_b_a"],
        "hyb_rg_w_x": full["hyb_rg_w_x"][None],
        "hyb_rg_b_x": full["hyb_rg_b_x"],
        "hyb_rg_lambda": full["hyb_rg_lambda"],
    }
    pk = lambda d: _pack_rows([d[nm] for nm in small_names])
    g_buf, spans = pk(small_grads)
    w_buf, _ = pk(params)
    m_buf, _ = pk(moms)
    v_buf, _ = pk(vars_)
    d_buf, nm_buf, nv_buf = _adamw_small("adamw_small", g_buf, w_buf, m_buf, v_buf)
    for nm, sp in zip(small_names, spans):
        grads[nm] = small_grads[nm]
        deltas[nm], new_m[nm], new_v[nm] = _unpack_rows(d_buf, sp), _unpack_rows(nm_buf, sp), _unpack_rows(nv_buf, sp)

    return (loss, grad_x[None], *[grads[n] for n in names], *[deltas[n] for n in names],
            *[new_m[n] for n in names], *[new_v[n] for n in names])
```

```python
import functools
import math

import jax
import jax.numpy as jnp
from jax import lax
from jax.experimental import pallas as pl
from jax.experimental.pallas import tpu as pltpu

F32 = jnp.float32
BF16 = jnp.bfloat16

NORM_EPS = 1e-6
LRU_C = 8.0
ATT_HEAD_DIM = 128
RG_HEAD_DIM = 64
LANES = 128
SUBLANES = 8
N_DEV = 8
ADAM_LR = 0.001
ADAM_B1 = 0.9
ADAM_B2 = 0.999
ADAM_EPS = 1e-08
ADAM_WD = 0.01
ADAM_STEP = 10
VMEM_LIMIT = 56 * 1024 * 1024
MESH = pl.DeviceIdType.MESH


def _tile(n, pref):
    if n <= pref:
        return n
    t = (pref // LANES) * LANES
    while t > LANES and n % t:
        t -= LANES
    assert n % t == 0, (n, pref)
    return t


def _params(sem):
    return pltpu.CompilerParams(dimension_semantics=sem, vmem_limit_bytes=VMEM_LIMIT)


DIMS_NN = (((1,), (0,)), ((), ()))
DIMS_NT = (((1,), (1,)), ((), ()))
DIMS_TN = (((0,), (0,)), ((), ()))


def _matmul(name, ins, outs, grid, dims, acc_shape, epilogue=None):
    n_in, n_out, nk = len(ins), len(outs), grid[2]

    def body(*refs):
        a_ref, b_ref = refs[0], refs[1]
        extras = refs[2:n_in]
        out_refs = refs[n_in:n_in + n_out]
        acc = refs[n_in + n_out]
        k = pl.program_id(2)

        @pl.when(k == 0)
        def _():
            acc[...] = jnp.zeros_like(acc)

        acc[...] += lax.dot_general(a_ref[...], b_ref[...], dims, preferred_element_type=F32)

        @pl.when(k == nk - 1)
        def _():
            r = acc[...]
            res = epilogue(r, *[e[...] for e in extras]) if epilogue is not None else (r,)
            for o, v in zip(out_refs, res):
                o[...] = v.astype(o.dtype)

    res = pl.pallas_call(
        body, name=name, grid=grid,
        in_specs=[s for _, s in ins],
        out_specs=[s for _, s in outs],
        out_shape=[s for s, _ in outs],
        scratch_shapes=[pltpu.VMEM(acc_shape, F32)],
        compiler_params=_params(("parallel", "parallel", "arbitrary")),
    )(*[a for a, _ in ins])
    return res


def _mm_nn(name, a, b, out_dtype, *, a_seg=None, out_seg=None, tm=1024, tn=1024, tk=512, epilogue=None,
           extras=(), n_out=1, out_dtypes=None):
    if a_seg:
        _, M, ks = a.shape
        K = ks * a_seg
    else:
        M, K = a.shape
        ks = K
    N = b.shape[1]
    ns = N // out_seg if out_seg else N
    tm, tn, tk = _tile(M, tm), _tile(ns, tn), _tile(ks, tk)
    nks, nns = ks // tk, ns // tn
    grid = (M // tm, N // tn, K // tk)
    if a_seg:
        a_spec = pl.BlockSpec((None, tm, tk), lambda i, j, k: (k // nks, i, k % nks))
    else:
        a_spec = pl.BlockSpec((tm, tk), lambda i, j, k: (i, k))
    b_spec = pl.BlockSpec((tk, tn), lambda i, j, k: (k, j))
    if out_seg:
        o_spec = pl.BlockSpec((None, tm, tn), lambda i, j, k: (j // nns, i, j % nns))
        o_shape = (out_seg, M, ns)
    else:
        o_spec = pl.BlockSpec((tm, tn), lambda i, j, k: (i, j))
        o_shape = (M, N)
    dts = out_dtypes or [out_dtype] * n_out
    outs = [(jax.ShapeDtypeStruct(o_shape, dt), o_spec) for dt in dts]
    ins = [(a, a_spec), (b, b_spec)] + [(e, o_spec) for e in extras]
    return _matmul(name, ins, outs, grid, DIMS_NN, (tm, tn), epilogue)


def _mm_nt(name, a, b, out_dtype, *, a_seg=None, out_seg=None, tm=1024, tn=1024, tk=512, epilogue=None, extras=()):
    if a_seg:
        _, M, ks = a.shape
        K = ks * a_seg
    else:
        M, K = a.shape
        ks = K
    N = b.shape[0]
    ns = N // out_seg if out_seg else N
    tm, tn, tk = _tile(M, tm), _tile(ns, tn), _tile(ks, tk)
    nks, nns = ks // tk, ns // tn
    grid = (M // tm, N // tn, K // tk)
    if a_seg:
        a_spec = pl.BlockSpec((None, tm, tk), lambda i, j, k: (k // nks, i, k % nks))
    else:
        a_spec = pl.BlockSpec((tm, tk), lambda i, j, k: (i, k))
    b_spec = pl.BlockSpec((tn, tk), lambda i, j, k: (j, k))
    if out_seg:
        o_spec = pl.BlockSpec((None, tm, tn), lambda i, j, k: (j // nns, i, j % nns))
        o_shape = (out_seg, M, ns)
    else:
        o_spec = pl.BlockSpec((tm, tn), lambda i, j, k: (i, j))
        o_shape = (M, N)
    outs = [(jax.ShapeDtypeStruct(o_shape, out_dtype), o_spec)]
    ins = [(a, a_spec), (b, b_spec)] + [(e, o_spec) for e in extras]
    return _matmul(name, ins, outs, grid, DIMS_NT, (tm, tn), epilogue)[0]


def _mm_tn(name, a, b, out_dtype, *, a_seg=None, b_seg=None, out_blocks=None, tm=1024, tn=1024, tk=512):
    if a_seg:
        _, T, ms = a.shape
        M = ms * a_seg
    else:
        T, M = a.shape
        ms = M
    if b_seg:
        _, _, ns = b.shape
        N = ns * b_seg
    else:
        N = b.shape[1]
        ns = N
    nb_cols = N // out_blocks if out_blocks else N
    tm, tk = _tile(ms, tm), _tile(T, tk)
    tn = _tile(math.gcd(ns, nb_cols), tn)
    nms, nns, nbs = ms // tm, ns // tn, nb_cols // tn
    grid = (M // tm, N // tn, T // tk)
    if a_seg:
        a_spec = pl.BlockSpec((None, tk, tm), lambda i, j, k: (i // nms, k, i % nms))
    else:
        a_spec = pl.BlockSpec((tk, tm), lambda i, j, k: (k, i))
    if b_seg:
        b_spec = pl.BlockSpec((None, tk, tn), lambda i, j, k: (j // nns, k, j % nns))
    else:
        b_spec = pl.BlockSpec((tk, tn), lambda i, j, k: (k, j))
    if out_blocks:
        o_spec = pl.BlockSpec((None, tm, tn), lambda i, j, k: (j // nbs, i, j % nbs))
        o_shape = (out_blocks, M, nb_cols)
    else:
        o_spec = pl.BlockSpec((tm, tn), lambda i, j, k: (i, j))
        o_shape = (M, N)
    outs = [(jax.ShapeDtypeStruct(o_shape, out_dtype), o_spec)]
    return _matmul(name, [(a, a_spec), (b, b_spec)], outs, grid, DIMS_TN, (tm, tn))[0]


def _rms(x):
    return lax.rsqrt(jnp.mean(x * x, axis=-1, keepdims=True) + NORM_EPS)


def _row_tile(T):
    return _tile(T, 256)


def _norm_fwd(name, x, g):
    T, D = x.shape
    tr = _row_tile(T)

    def body(x_ref, g_ref, h_ref):
        xv = x_ref[...]
        h_ref[...] = (xv * _rms(xv) * g_ref[...]).astype(h_ref.dtype)

    row = pl.BlockSpec((tr, D), lambda i: (i, 0))
    vec = pl.BlockSpec((1, D), lambda i: (0, 0))
    return pl.pallas_call(body, name=name, grid=(T // tr,), in_specs=[row, vec], out_specs=row,
                          out_shape=jax.ShapeDtypeStruct((T, D), BF16), compiler_params=_params(("parallel",)))(x, g)


def _resid_norm(name, x, br, g_post, g_next):
    T, D = x.shape
    tr = _row_tile(T)

    def body(x_ref, br_ref, gp_ref, gn_ref, xn_ref, h_ref):
        b = br_ref[...]
        xn = x_ref[...] + b * _rms(b) * gp_ref[...]
        xn_ref[...] = xn
        h_ref[...] = (xn * _rms(xn) * gn_ref[...]).astype(h_ref.dtype)

    row = pl.BlockSpec((tr, D), lambda i: (i, 0))
    vec = pl.BlockSpec((1, D), lambda i: (0, 0))
    return pl.pallas_call(body, name=name, grid=(T // tr,), in_specs=[row, row, vec, vec], out_specs=[row, row],
                          out_shape=[jax.ShapeDtypeStruct((T, D), F32), jax.ShapeDtypeStruct((T, D), BF16)],
                          compiler_params=_params(("parallel",)))(x, br, g_post, g_next)


def _final_loss(name, x, br, g_post, target):
    T, D = x.shape
    tr = _row_tile(T)

    def body(x_ref, br_ref, gp_ref, t_ref, dy_ref, ls_ref):
        b = br_ref[...]
        err = x_ref[...] + b * _rms(b) * gp_ref[...] - t_ref[...]
        dy_ref[...] = err * (1.0 / D)

        @pl.when(pl.program_id(0) == 0)
        def _():
            ls_ref[...] = jnp.zeros_like(ls_ref)

        ls_ref[...] += jnp.sum(err * err)

    row = pl.BlockSpec((tr, D), lambda i: (i, 0))
    vec = pl.BlockSpec((1, D), lambda i: (0, 0))
    acc = pl.BlockSpec((SUBLANES, LANES), lambda i: (0, 0))
    return pl.pallas_call(body, name=name, grid=(T // tr,), in_specs=[row, row, vec, row], out_specs=[row, acc],
                          out_shape=[jax.ShapeDtypeStruct((T, D), F32), jax.ShapeDtypeStruct((SUBLANES, LANES), F32)],
                          compiler_params=_params(("arbitrary",)))(x, br, g_post, target)


def _norm_bwd(name, x, g, dy, add, out_dtype):
    T, D = x.shape
    tr = _row_tile(T)
    has_add = add is not None

    def body(*refs):
        if has_add:
            x_ref, g_ref, dy_ref, add_ref, dx_ref, dg_ref = refs
        else:
            x_ref, g_ref, dy_ref, dx_ref, dg_ref = refs
        xv = x_ref[...]
        r = _rms(xv)
        xhat = xv * r
        dyv = dy_ref[...].astype(F32)
        gdy = dyv * g_ref[...]
        dx = r * (gdy - xhat * jnp.mean(gdy * xhat, axis=-1, keepdims=True))
        if has_add:
            dx = dx + add_ref[...]
        dx_ref[...] = dx.astype(dx_ref.dtype)

        @pl.when(pl.program_id(0) == 0)
        def _():
            dg_ref[...] = jnp.zeros_like(dg_ref)

        dg_ref[...] += jnp.sum(dyv * xhat, axis=0, keepdims=True)

    row = pl.BlockSpec((tr, D), lambda i: (i, 0))
    vec = pl.BlockSpec((1, D), lambda i: (0, 0))
    ins = [x, g, dy] + ([add] if has_add else [])
    specs = [row, vec, row] + ([row] if has_add else [])
    return pl.pallas_call(body, name=name, grid=(T // tr,), in_specs=specs, out_specs=[row, vec],
                          out_shape=[jax.ShapeDtypeStruct((T, D), out_dtype), jax.ShapeDtypeStruct((1, D), F32)],
                          compiler_params=_params(("arbitrary",)))(*ins)


HALO = SUBLANES
TIME_CHUNK = 512


def _chunks(T):
    tc = min(TIME_CHUNK, T)
    assert T % tc == 0 and tc % SUBLANES == 0
    return [(t0, tc) for t0 in range(0, T, tc)]


def _log_sigmoid(x):
    return -(jnp.maximum(-x, 0.0) + jnp.log(1.0 + jnp.exp(-jnp.abs(x))))


def _one_minus_exp(x):
    series = -x * (1.0 + x * (0.5 + x * (1.0 / 6.0 + x * (1.0 / 24.0))))
    return jnp.where(x > -0.01, series, 1.0 - jnp.exp(x))


_GELU_C = math.sqrt(2.0 / math.pi)


def _gelu(x):
    return 0.5 * x * (1.0 + jnp.tanh(_GELU_C * (x + 0.044715 * x * x * x)))


def _gelu_grad(x):
    th = jnp.tanh(_GELU_C * (x + 0.044715 * x * x * x))
    return 0.5 * (1.0 + th) + 0.5 * x * (1.0 - th * th) * _GELU_C * (1.0 + 3.0 * 0.044715 * x * x)


def _tile_scan(a, b, reverse):
    rows = a.shape[0]
    pos = lax.broadcasted_iota(jnp.int32, a.shape, 0) & (SUBLANES - 1)
    for d in (1, 2, 4):
        if reverse:
            ok = pos < SUBLANES - d
            shift = rows - d
        else:
            ok = pos >= d
            shift = d
        a_sh = jnp.where(ok, pltpu.roll(a, shift, 0), 1.0)
        b_sh = jnp.where(ok, pltpu.roll(b, shift, 0), 0.0)
        b = a * b_sh + b
        a = a * a_sh
    return a, b


def _carry_scan(a_s, b_s, T, reverse):
    n = T // SUBLANES
    edge = 0 if reverse else SUBLANES - 1

    def step(j, carry):
        g = (n - 1 - j) if reverse else j
        r = pl.multiple_of(g * SUBLANES, SUBLANES)
        h = b_s[pl.ds(r, SUBLANES), :] + a_s[pl.ds(r, SUBLANES), :] * carry
        b_s[pl.ds(r, SUBLANES), :] = h
        return jnp.broadcast_to(h[edge:edge + 1, :], h.shape)

    lax.fori_loop(0, n, step, jnp.zeros((SUBLANES, a_s.shape[1]), F32))


def _seg_spec(T, seg, nblk):
    return pl.BlockSpec((None, T, LANES), lambda c: (seg, 0, c))


def _rows_to_tile(rows):
    idx = lax.broadcasted_iota(jnp.int32, (SUBLANES, LANES), 0)
    out = jnp.zeros((SUBLANES, LANES), F32)
    for k, r in enumerate(rows):
        out = jnp.where(idx == k, r, out)
    return out


def _mixer_a_fwd(proj, conv_a):
    _, T, C = proj.shape
    nblk = C // LANES
    chunks = _chunks(T)

    def body(bg_ref, cg_ref, ax_ref, w_ref, y_ref, p_s):
        p_s[pl.ds(0, HALO), :] = jnp.zeros((HALO, LANES), F32)
        for t0, tc in chunks:
            p_s[pl.ds(HALO + t0, tc), :] = cg_ref[pl.ds(t0, tc), :] * ax_ref[pl.ds(t0, tc), :]
        w = w_ref[...]
        for t0, tc in chunks:
            c = (w[2:3, :] * p_s[pl.ds(HALO + t0, tc), :] + w[1:2, :] * p_s[pl.ds(HALO + t0 - 1, tc), :]
                 + w[0:1, :] * p_s[pl.ds(HALO + t0 - 2, tc), :])
            y_ref[pl.ds(t0, tc), :] = (bg_ref[pl.ds(t0, tc), :] * c).astype(y_ref.dtype)

    return pl.pallas_call(
        body, name="mixer_a_fwd", grid=(nblk,),
        in_specs=[_seg_spec(T, 0, nblk), _seg_spec(T, 1, nblk), _seg_spec(T, 2, nblk),
                  pl.BlockSpec((3, LANES), lambda c: (0, c))],
        out_specs=pl.BlockSpec((T, LANES), lambda c: (0, c)),
        out_shape=jax.ShapeDtypeStruct((T, C), BF16),
        scratch_shapes=[pltpu.VMEM((T + HALO, LANES), F32)],
        compiler_params=_params(("parallel",)))(proj, proj, proj, conv_a)


def _mixer_a_bwd(proj, conv_a, dy):
    _, T, C = proj.shape
    nblk = C // LANES
    chunks = _chunks(T)

    def body(bg_ref, cg_ref, ax_ref, w_ref, dy_ref, dp_ref, dw_ref, p_s, dc_s):
        p_s[pl.ds(0, HALO), :] = jnp.zeros((HALO, LANES), F32)
        dc_s[pl.ds(T, HALO), :] = jnp.zeros((HALO, LANES), F32)
        for t0, tc in chunks:
            p_s[pl.ds(HALO + t0, tc), :] = cg_ref[pl.ds(t0, tc), :] * ax_ref[pl.ds(t0, tc), :]
        w = w_ref[...]
        for t0, tc in chunks:
            c = (w[2:3, :] * p_s[pl.ds(HALO + t0, tc), :] + w[1:2, :] * p_s[pl.ds(HALO + t0 - 1, tc), :]
                 + w[0:1, :] * p_s[pl.ds(HALO + t0 - 2, tc), :])
            dyv = dy_ref[pl.ds(t0, tc), :]
            dp_ref[0, pl.ds(t0, tc), :] = (dyv * c).astype(dp_ref.dtype)
            dc_s[pl.ds(t0, tc), :] = dyv * bg_ref[pl.ds(t0, tc), :]
        dw = [jnp.zeros((1, LANES), F32) for _ in range(3)]
        for t0, tc in chunks:
            dc = dc_s[pl.ds(t0, tc), :]
            dpv = w[2:3, :] * dc + w[1:2, :] * dc_s[pl.ds(t0 + 1, tc), :] + w[0:1, :] * dc_s[pl.ds(t0 + 2, tc), :]
            dp_ref[1, pl.ds(t0, tc), :] = (dpv * ax_ref[pl.ds(t0, tc), :]).astype(dp_ref.dtype)
            dp_ref[2, pl.ds(t0, tc), :] = (dpv * cg_ref[pl.ds(t0, tc), :]).astype(dp_ref.dtype)
            for k in range(3):
                dw[k] = dw[k] + jnp.sum(dc * p_s[pl.ds(HALO + t0 - (2 - k), tc), :], axis=0, keepdims=True)
        dw_ref[...] = _rows_to_tile(dw)

    return pl.pallas_call(
        body, name="mixer_a_bwd", grid=(nblk,),
        in_specs=[_seg_spec(T, 0, nblk), _seg_spec(T, 1, nblk), _seg_spec(T, 2, nblk),
                  pl.BlockSpec((3, LANES), lambda c: (0, c)), _seg_spec(T, 0, nblk)],
        out_specs=[pl.BlockSpec((3, T, LANES), lambda c: (0, 0, c)),
                   pl.BlockSpec((None, SUBLANES, LANES), lambda c: (c, 0, 0))],
        out_shape=[jax.ShapeDtypeStruct((3, T, C), BF16), jax.ShapeDtypeStruct((nblk, SUBLANES, LANES), F32)],
        scratch_shapes=[pltpu.VMEM((T + HALO, LANES), F32), pltpu.VMEM((T + HALO, LANES), F32)],
        compiler_params=_params(("parallel",)))(proj, proj, proj, conv_a, dy)


def _rg_gates(xr, wa, ba, wx, bx, ls):
    xb = xr.astype(BF16)
    r = jax.nn.sigmoid(jnp.dot(xb, wa, preferred_element_type=F32) + ba)
    i = jax.nn.sigmoid(jnp.dot(xb, wx, preferred_element_type=F32) + bx)
    log_a = LRU_C * r * ls
    a = jnp.exp(log_a)
    mult = jnp.sqrt(_one_minus_exp(2.0 * log_a))
    return r, i, a, mult


def _conv4(xh_s, cw, bias, t0, tc):
    return (cw[3:4, :] * xh_s[pl.ds(HALO + t0, tc), :] + cw[2:3, :] * xh_s[pl.ds(HALO + t0 - 1, tc), :]
            + cw[1:2, :] * xh_s[pl.ds(HALO + t0 - 2, tc), :] + cw[0:1, :] * xh_s[pl.ds(HALO + t0 - 3, tc), :] + bias)


def _mixer_b_specs(T, nblk):
    vec = pl.BlockSpec((1, LANES), lambda c: (0, c))
    mat = pl.BlockSpec((None, LANES, LANES), lambda c: (c, 0, 0))
    return [_seg_spec(T, 3, nblk), _seg_spec(T, 4, nblk), pl.BlockSpec((4, LANES), lambda c: (0, c)),
            vec, mat, vec, mat, vec, vec]


def _mixer_b_fwd(proj, conv_b, bias, wa, ba, wx, bx, lam):
    _, T, C = proj.shape
    nblk = C // LANES
    chunks = _chunks(T)

    def body(gate_ref, x_ref, cw_ref, cb_ref, wa_ref, ba_ref, wx_ref, bx_ref, lam_ref, y_ref, xh_s, a_s, b_s):
        xh_s[pl.ds(0, HALO), :] = jnp.zeros((HALO, LANES), F32)
        for t0, tc in chunks:
            xh_s[pl.ds(HALO + t0, tc), :] = x_ref[pl.ds(t0, tc), :]
        cw, bias_v = cw_ref[...], cb_ref[...]
        ls = _log_sigmoid(lam_ref[...])
        for t0, tc in chunks:
            xr = _conv4(xh_s, cw, bias_v, t0, tc)
            r, i, a, mult = _rg_gates(xr, wa_ref[...], ba_ref[...], wx_ref[...], bx_ref[...], ls)
            ac, hc = _tile_scan(a, mult * i * xr, reverse=False)
            a_s[pl.ds(t0, tc), :] = ac
            b_s[pl.ds(t0, tc), :] = hc
        _carry_scan(a_s, b_s, T, reverse=False)
        for t0, tc in chunks:
            y_ref[pl.ds(t0, tc), :] = (b_s[pl.ds(t0, tc), :] * _gelu(gate_ref[pl.ds(t0, tc), :])).astype(y_ref.dtype)

    return pl.pallas_call(
        body, name="mixer_b_fwd", grid=(nblk,), in_specs=_mixer_b_specs(T, nblk),
        out_specs=pl.BlockSpec((T, LANES), lambda c: (0, c)),
        out_shape=jax.ShapeDtypeStruct((T, C), BF16),
        scratch_shapes=[pltpu.VMEM((T + HALO, LANES), F32), pltpu.VMEM((T, LANES), F32), pltpu.VMEM((T, LANES), F32)],
        compiler_params=_params(("parallel",)))(proj, proj, conv_b, bias, wa, ba, wx, bx, lam)


_ROW_CONV, _ROW_BIAS, _ROW_BA, _ROW_BX, _ROW_LAM = 0, 4, 5, 6, 7


def _mixer_b_bwd(proj, conv_b, bias, wa, ba, wx, bx, lam, dy):
    _, T, C = proj.shape
    nblk = C // LANES
    chunks = _chunks(T)

    def body(gate_ref, x_ref, cw_ref, cb_ref, wa_ref, ba_ref, wx_ref, bx_ref, lam_ref, dy_ref,
             dp_ref, sm_ref, dwa_ref, dwx_ref, xh_s, xr_s, r_s, i_s, a_s, h_s, sa_s, sb_s, dx_s):
        zero_halo = jnp.zeros((HALO, LANES), F32)
        xh_s[pl.ds(0, HALO), :] = zero_halo
        h_s[pl.ds(0, HALO), :] = zero_halo
        a_s[pl.ds(T, HALO), :] = zero_halo
        dx_s[pl.ds(T, HALO), :] = zero_halo
        for t0, tc in chunks:
            xh_s[pl.ds(HALO + t0, tc), :] = x_ref[pl.ds(t0, tc), :]
        cw, bias_v = cw_ref[...], cb_ref[...]
        lam_v = lam_ref[...]
        ls = _log_sigmoid(lam_v)
        wa_v, wx_v, ba_v, bx_v = wa_ref[...], wx_ref[...], ba_ref[...], bx_ref[...]
        for t0, tc in chunks:
            xr = _conv4(xh_s, cw, bias_v, t0, tc)
            r, i, a, mult = _rg_gates(xr, wa_v, ba_v, wx_v, bx_v, ls)
            xr_s[pl.ds(t0, tc), :] = xr
            r_s[pl.ds(t0, tc), :] = r
            i_s[pl.ds(t0, tc), :] = i
            a_s[pl.ds(t0, tc), :] = a
            ac, hc = _tile_scan(a, mult * i * xr, reverse=False)
            sa_s[pl.ds(t0, tc), :] = ac
            sb_s[pl.ds(t0, tc), :] = hc
        _carry_scan(sa_s, sb_s, T, reverse=False)
        for t0, tc in chunks:
            h_s[pl.ds(HALO + t0, tc), :] = sb_s[pl.ds(t0, tc), :]
        for t0, tc in chunks:
            gv = gate_ref[pl.ds(t0, tc), :]
            dyv = dy_ref[pl.ds(t0, tc), :]
            dp_ref[0, pl.ds(t0, tc), :] = (dyv * h_s[pl.ds(HALO + t0, tc), :] * _gelu_grad(gv)).astype(dp_ref.dtype)
            ac, gc = _tile_scan(a_s[pl.ds(t0 + 1, tc), :], dyv * _gelu(gv), reverse=True)
            sa_s[pl.ds(t0, tc), :] = ac
            sb_s[pl.ds(t0, tc), :] = gc
        _carry_scan(sa_s, sb_s, T, reverse=True)
        acc = {k: jnp.zeros((1, LANES), F32) for k in ("bias", "ba", "bx", "lam")}
        dwa = jnp.zeros((LANES, LANES), F32)
        dwx = jnp.zeros((LANES, LANES), F32)
        for t0, tc in chunks:
            dht = sb_s[pl.ds(t0, tc), :]
            xr, r, i, a = xr_s[pl.ds(t0, tc), :], r_s[pl.ds(t0, tc), :], i_s[pl.ds(t0, tc), :], a_s[pl.ds(t0, tc), :]
            mult = jnp.sqrt(_one_minus_exp(2.0 * LRU_C * r * ls))
            da = dht * h_s[pl.ds(HALO + t0 - 1, tc), :]
            dmult = dht * i * xr
            di = dht * mult * xr
            dlog_a = da * a - dmult * a * a / mult
            dpa = dlog_a * (LRU_C * ls) * r * (1.0 - r)
            dpx = di * i * (1.0 - i)
            acc["lam"] = acc["lam"] + jnp.sum(dlog_a * r, axis=0, keepdims=True)
            acc["ba"] = acc["ba"] + jnp.sum(dpa, axis=0, keepdims=True)
            acc["bx"] = acc["bx"] + jnp.sum(dpx, axis=0, keepdims=True)
            xb, dpab, dpxb = xr.astype(BF16), dpa.astype(BF16), dpx.astype(BF16)
            dwa = dwa + lax.dot_general(xb, dpab, DIMS_TN, preferred_element_type=F32)
            dwx = dwx + lax.dot_general(xb, dpxb, DIMS_TN, preferred_element_type=F32)
            dxr = (dht * mult * i + lax.dot_general(dpab, wa_v, DIMS_NT, preferred_element_type=F32)
                   + lax.dot_general(dpxb, wx_v, DIMS_NT, preferred_element_type=F32))
            acc["bias"] = acc["bias"] + jnp.sum(dxr, axis=0, keepdims=True)
            dx_s[pl.ds(t0, tc), :] = dxr
        dcw = [jnp.zeros((1, LANES), F32) for _ in range(4)]
        for t0, tc in chunks:
            dxr = dx_s[pl.ds(t0, tc), :]
            dxin = (cw[3:4, :] * dxr + cw[2:3, :] * dx_s[pl.ds(t0 + 1, tc), :] + cw[1:2, :] * dx_s[pl.ds(t0 + 2, tc), :]
                    + cw[0:1, :] * dx_s[pl.ds(t0 + 3, tc), :])
            dp_ref[1, pl.ds(t0, tc), :] = dxin.astype(dp_ref.dtype)
            for k in range(4):
                dcw[k] = dcw[k] + jnp.sum(dxr * xh_s[pl.ds(HALO + t0 - (3 - k), tc), :], axis=0, keepdims=True)
        dlam = acc["lam"] * LRU_C * jax.nn.sigmoid(-lam_v)
        sm_ref[...] = _rows_to_tile(dcw + [acc["bias"], acc["ba"], acc["bx"], dlam])
        dwa_ref[...] = dwa
        dwx_ref[...] = dwx

    big = lambda halo: pltpu.VMEM((T + halo, LANES), F32)
    mat = pl.BlockSpec((None, LANES, LANES), lambda c: (c, 0, 0))
    return pl.pallas_call(
        body, name="mixer_b_bwd", grid=(nblk,),
        in_specs=_mixer_b_specs(T, nblk) + [_seg_spec(T, 1, nblk)],
        out_specs=[pl.BlockSpec((2, T, LANES), lambda c: (0, 0, c)),
                   pl.BlockSpec((None, SUBLANES, LANES), lambda c: (c, 0, 0)), mat, mat],
        out_shape=[jax.ShapeDtypeStruct((2, T, C), BF16), jax.ShapeDtypeStruct((nblk, SUBLANES, LANES), F32),
                   jax.ShapeDtypeStruct((nblk, LANES, LANES), F32), jax.ShapeDtypeStruct((nblk, LANES, LANES), F32)],
        scratch_shapes=[big(HALO), big(0), big(0), big(0), big(HALO), big(HALO), big(0), big(0), big(HALO)],
        compiler_params=_params(("parallel",)))(proj, proj, conv_b, bias, wa, ba, wx, bx, lam, dy)


ATT_BLOCK = 128
ATT_GROUP = 4
ATT_TILE = ATT_BLOCK * ATT_GROUP
ATT_UNDERFLOW = -110.0
ATT_UNVISITED = -1e30


def _split_dot(x, m):
    hi = x.astype(BF16)
    lo = (x - hi.astype(F32)).astype(BF16)
    return jnp.dot(hi, m, preferred_element_type=F32) + jnp.dot(lo, m, preferred_element_type=F32)


def _sub(x, j):
    return x[:, j * ATT_BLOCK:(j + 1) * ATT_BLOCK]


def _stack_rows(x):
    return jnp.concatenate([_sub(x, j) for j in range(ATT_GROUP)], axis=0)


def _unstack_rows(x, offsets):
    return jnp.concatenate([x[j * ATT_BLOCK:(j + 1) * ATT_BLOCK, :] + offsets[j] for j in range(ATT_GROUP)], axis=1)


def _att_tile(q, k_ref, q0, qb, it, scale):
    hi = (qb + 1 - ATT_GROUP * it) * ATT_BLOCK
    k0 = pl.multiple_of(jnp.maximum(hi - ATT_TILE, 0), ATT_BLOCK)
    kt = k_ref[pl.ds(k0, ATT_TILE), :]
    z = lax.dot_general(q, kt, DIMS_NT, preferred_element_type=F32) * scale
    key = k0 + lax.broadcasted_iota(jnp.int32, z.shape, 1)
    row = q0 + lax.broadcasted_iota(jnp.int32, z.shape, 0)
    mask = (key < row) & (key < hi)
    n = jnp.where(mask, -(jnp.maximum(z, 0.0) + jnp.log(1.0 + jnp.exp(-jnp.abs(z)))), 0.0)
    return k0, kt, z, mask, n


def _suffix_in_tile(n, upper, run):
    rs = [jnp.sum(_sub(n, j), axis=1, keepdims=True) for j in range(ATT_GROUP)]
    offs = [None] * ATT_GROUP
    offs[ATT_GROUP - 1] = run
    for j in range(ATT_GROUP - 2, -1, -1):
        offs[j] = offs[j + 1] + rs[j + 1]
    return _unstack_rows(_split_dot(_stack_rows(n), upper), offs), offs[0] + rs[0]


def _head_spec(T, seg, heads):
    return pl.BlockSpec((None, T, ATT_HEAD_DIM), lambda h: (seg, 0, h))


def _attention_fwd(qkv):
    _, T, D = qkv.shape
    heads = D // ATT_HEAD_DIM
    nq = T // ATT_BLOCK
    assert nq <= LANES
    scale = 1.0 / math.sqrt(ATT_HEAD_DIM)

    def body(q_ref, k_ref, v_ref, o_ref, r_ref, acc_s, run_s):
        rr = lax.broadcasted_iota(jnp.int32, (ATT_BLOCK, ATT_BLOCK), 0)
        cc = lax.broadcasted_iota(jnp.int32, (ATT_BLOCK, ATT_BLOCK), 1)
        upper = jnp.where(rr > cc, 1.0, 0.0).astype(BF16)
        lane = lax.broadcasted_iota(jnp.int32, (ATT_BLOCK, LANES), 1)

        def q_loop(qb, _):
            q0 = pl.multiple_of(qb * ATT_BLOCK, ATT_BLOCK)
            q = q_ref[pl.ds(q0, ATT_BLOCK), :]
            acc_s[...] = jnp.zeros_like(acc_s)
            run_s[...] = jnp.zeros_like(run_s)
            r_ref[pl.ds(q0, ATT_BLOCK), :] = jnp.full((ATT_BLOCK, LANES), ATT_UNVISITED, F32)
            n_tiles = (qb + ATT_GROUP) // ATT_GROUP

            def tile(carry):
                it, _ = carry
                k0, _, z, mask, n = _att_tile(q, k_ref, q0, qb, it, scale)
                run = run_s[...]
                suffix, run_next = _suffix_in_tile(n, upper, run)
                w = jnp.where(mask, jnp.exp(z + n + suffix), 0.0)
                acc_s[...] += jnp.dot(w.astype(BF16), v_ref[pl.ds(k0, ATT_TILE), :], preferred_element_type=F32)
                r_ref[pl.ds(q0, ATT_BLOCK), :] = jnp.where(lane == it, run, r_ref[pl.ds(q0, ATT_BLOCK), :])
                run_s[...] = run_next
                return it + 1, jnp.max(run_next) >= ATT_UNDERFLOW

            lax.while_loop(lambda c: (c[0] < n_tiles) & c[1], tile, (jnp.int32(0), jnp.bool_(True)))
            o_ref[pl.ds(q0, ATT_BLOCK), :] = acc_s[...].astype(o_ref.dtype)
            return 0

        lax.fori_loop(0, nq, q_loop, 0)

    return pl.pallas_call(
        body, name="attention_fwd", grid=(heads,),
        in_specs=[_head_spec(T, 0, heads), _head_spec(T, 1, heads), _head_spec(T, 2, heads)],
        out_specs=[pl.BlockSpec((T, ATT_HEAD_DIM), lambda h: (0, h)), pl.BlockSpec((None, T, LANES), lambda h: (h, 0, 0))],
        out_shape=[jax.ShapeDtypeStruct((T, D), BF16), jax.ShapeDtypeStruct((heads, T, LANES), F32)],
        scratch_shapes=[pltpu.VMEM((ATT_BLOCK, ATT_HEAD_DIM), F32), pltpu.VMEM((ATT_BLOCK, LANES), F32)],
        compiler_params=_params(("parallel",)))(qkv, qkv, qkv)


def _attention_bwd(qkv, do, rmat):
    _, T, D = qkv.shape
    heads = D // ATT_HEAD_DIM
    nq = T // ATT_BLOCK
    scale = 1.0 / math.sqrt(ATT_HEAD_DIM)

    def body(q_ref, k_ref, v_ref, do_ref, r_ref, dqkv_ref, dk_s, dv_s, dq_s, left_s):
        rr = lax.broadcasted_iota(jnp.int32, (ATT_BLOCK, ATT_BLOCK), 0)
        cc = lax.broadcasted_iota(jnp.int32, (ATT_BLOCK, ATT_BLOCK), 1)
        upper = jnp.where(rr > cc, 1.0, 0.0).astype(BF16)
        lower = jnp.where(rr < cc, 1.0, 0.0).astype(BF16)
        lane = lax.broadcasted_iota(jnp.int32, (ATT_BLOCK, LANES), 1)
        dk_s[...] = jnp.zeros_like(dk_s)
        dv_s[...] = jnp.zeros_like(dv_s)

        def q_loop(qb, _):
            q0 = pl.multiple_of(qb * ATT_BLOCK, ATT_BLOCK)
            q = q_ref[pl.ds(q0, ATT_BLOCK), :]
            dov = do_ref[pl.ds(q0, ATT_BLOCK), :]
            dq_s[...] = jnp.zeros_like(dq_s)
            left_s[...] = jnp.zeros_like(left_s)
            rm = r_ref[pl.ds(q0, ATT_BLOCK), :]
            n_tiles = (qb + ATT_GROUP) // ATT_GROUP
            seen = (jnp.max(rm, axis=0, keepdims=True) > 0.5 * ATT_UNVISITED) & (lane[0:1, :] < n_tiles)
            n_visited = jnp.sum(jnp.where(seen, 1.0, 0.0)).astype(jnp.int32)

            def tile(j, _):
                it = n_visited - 1 - j
                k0, kt, z, mask, n = _att_tile(q, k_ref, q0, qb, it, scale)
                vt = v_ref[pl.ds(k0, ATT_TILE), :]
                run = jnp.sum(jnp.where(lane == it, r_ref[pl.ds(q0, ATT_BLOCK), :], 0.0), axis=1, keepdims=True)
                suffix, _ = _suffix_in_tile(n, upper, run)
                s = z + n
                w = jnp.where(mask, jnp.exp(s + suffix), 0.0)
                e = w * lax.dot_general(dov, vt, DIMS_NT, preferred_element_type=F32)
                es = [jnp.sum(_sub(e, g), axis=1, keepdims=True) for g in range(ATT_GROUP)]
                pre = [left_s[...]]
                for g in range(ATT_GROUP):
                    pre.append(pre[g] + es[g])
                before = _unstack_rows(_split_dot(_stack_rows(e), lower), pre)
                sig = jnp.exp(s)
                dz = (jnp.where(mask, e * (1.0 - sig) - before * sig, 0.0) * scale).astype(BF16)
                dq_s[...] += jnp.dot(dz, kt, preferred_element_type=F32)
                dk_s[pl.ds(k0, ATT_TILE), :] += lax.dot_general(dz, q, DIMS_TN, preferred_element_type=F32)
                dv_s[pl.ds(k0, ATT_TILE), :] += lax.dot_general(w.astype(BF16), dov, DIMS_TN, preferred_element_type=F32)
                left_s[...] = pre[ATT_GROUP]
                return 0

            lax.fori_loop(0, n_visited, tile, 0)
            dqkv_ref[0, pl.ds(q0, ATT_BLOCK), :] = dq_s[...].astype(dqkv_ref.dtype)
            return 0

        lax.fori_loop(0, nq, q_loop, 0)
        dqkv_ref[1, :, :] = dk_s[...].astype(dqkv_ref.dtype)
        dqkv_ref[2, :, :] = dv_s[...].astype(dqkv_ref.dtype)

    return pl.pallas_call(
        body, name="attention_bwd", grid=(heads,),
        in_specs=[_head_spec(T, 0, heads), _head_spec(T, 1, heads), _head_spec(T, 2, heads),
                  pl.BlockSpec((T, ATT_HEAD_DIM), lambda h: (0, h)), pl.BlockSpec((None, T, LANES), lambda h: (h, 0, 0))],
        out_specs=pl.BlockSpec((3, T, ATT_HEAD_DIM), lambda h: (0, 0, h)),
        out_shape=jax.ShapeDtypeStruct((3, T, D), BF16),
        scratch_shapes=[pltpu.VMEM((T, ATT_HEAD_DIM), F32), pltpu.VMEM((T, ATT_HEAD_DIM), F32),
                        pltpu.VMEM((ATT_BLOCK, ATT_HEAD_DIM), F32), pltpu.VMEM((ATT_BLOCK, LANES), F32)],
        compiler_params=_params(("parallel",)))(qkv, qkv, qkv, do, rmat)


def _block_diag_pairs(w):
    h = w.shape[0]
    wp = w.reshape(h // 2, 2, RG_HEAD_DIM, RG_HEAD_DIM)
    z = jnp.zeros_like(wp[:, 0])
    top = jnp.concatenate([wp[:, 0], z], axis=2)
    bot = jnp.concatenate([z, wp[:, 1]], axis=2)
    return jnp.concatenate([top, bot], axis=1)


def _diag_pairs(g):
    n = g.shape[0]
    a = g[:, :RG_HEAD_DIM, :RG_HEAD_DIM]
    b = g[:, RG_HEAD_DIM:, RG_HEAD_DIM:]
    return jnp.stack([a, b], axis=1).reshape(2 * n, RG_HEAD_DIM, RG_HEAD_DIM)


def _mlp_fwd(tag, h, w_up, layer, w_down):
    T, D = h.shape
    fb = w_up.shape[3]
    F = fb * N_DEV
    tm, tn, tk = _tile(T, 1024), _tile(fb, 1024), _tile(D, 512)
    nb = fb // tn

    def up_epilogue(u):
        r = jnp.maximum(u, 0.0)
        return u, r * r

    o_spec = pl.BlockSpec((tm, tn), lambda i, j, k: (i, j))
    u, act = _matmul(
        f"mlp_up_{tag}",
        [(h, pl.BlockSpec((tm, tk), lambda i, j, k: (i, k))),
         (w_up, pl.BlockSpec((None, None, tk, tn), lambda i, j, k: (j // nb, layer, k, j % nb)))],
        [(jax.ShapeDtypeStruct((T, F), BF16), o_spec), (jax.ShapeDtypeStruct((T, F), BF16), o_spec)],
        (T // tm, F // tn, D // tk), DIMS_NN, (tm, tn), up_epilogue)
    tn2, tk2 = _tile(D, 1024), _tile(fb, 512)
    nkb = fb // tk2
    (m,) = _matmul(
        f"mlp_down_{tag}",
        [(act, pl.BlockSpec((tm, tk2), lambda i, j, k: (i, k))),
         (w_down, pl.BlockSpec((None, None, tk2, tn2), lambda i, j, k: (k // nkb, layer, k % nkb, j)))],
        [(jax.ShapeDtypeStruct((T, D), F32), pl.BlockSpec((tm, tn2), lambda i, j, k: (i, j)))],
        (T // tm, D // tn2, F // tk2), DIMS_NN, (tm, tn2))
    return u, act, m


def _mlp_bwd(tag, h, u, act, dm, w_up, layer, w_down):
    T, D = h.shape
    fb = w_up.shape[3]
    F = fb * N_DEV
    dw_down = _mm_tn(f"mlp_down_dw_{tag}", act, dm, BF16)
    tm, tn, tk = _tile(T, 1024), _tile(fb, 1024), _tile(D, 512)
    nb = fb // tn
    o_spec = pl.BlockSpec((tm, tn), lambda i, j, k: (i, j))
    (du,) = _matmul(
        f"mlp_down_dx_{tag}",
        [(dm, pl.BlockSpec((tm, tk), lambda i, j, k: (i, k))),
         (w_down, pl.BlockSpec((None, None, tn, tk), lambda i, j, k: (j // nb, layer, j % nb, k))),
         (u, o_spec)],
        [(jax.ShapeDtypeStruct((T, F), BF16), o_spec)],
        (T // tm, F // tn, D // tk), DIMS_NT, (tm, tn),
        lambda r, uv: (r * (2.0 * jnp.maximum(uv.astype(F32), 0.0)),))
    dw_up = _mm_tn(f"mlp_up_dw_{tag}", h, du, BF16, out_blocks=N_DEV)
    tn2, tk2 = _tile(D, 1024), _tile(fb, 512)
    nkb = fb // tk2
    (dh,) = _matmul(
        f"mlp_up_dx_{tag}",
        [(du, pl.BlockSpec((tm, tk2), lambda i, j, k: (i, k))),
         (w_up, pl.BlockSpec((None, None, tn2, tk2), lambda i, j, k: (k // nkb, layer, j, k % nkb)))],
        [(jax.ShapeDtypeStruct((T, D), F32), pl.BlockSpec((tm, tn2), lambda i, j, k: (i, j)))],
        (T // tm, D // tn2, F // tk2), DIMS_NT, (tm, tn2))
    return dw_down, dw_up, dh


def _local_step(x, target, gains, w_in, conv_a, conv_b, conv_b_bias, rg_w_a, rg_b_a, rg_w_x, rg_b_x, rg_lambda,
                w_out, w_qkv, w_o, w_up, w_down):
    T, D = x.shape
    g = lambda l, i: gains[l, i][None, :]
    wa_p = _block_diag_pairs(rg_w_a).astype(BF16)
    wx_p = _block_diag_pairs(rg_w_x).astype(BF16)

    h0 = _norm_fwd("norm_in", x, g(0, 0))
    (proj,) = _mm_nn("w_in_fwd", h0, w_in, F32, out_seg=5)
    y_a = _mixer_a_fwd(proj, conv_a)
    y_b = _mixer_b_fwd(proj, conv_b, conv_b_bias, wa_p, rg_b_a, wx_p, rg_b_x, rg_lambda)
    y = jnp.stack([y_a, y_b], axis=0)
    (mix0,) = _mm_nn("w_out_fwd", y, w_out, F32, a_seg=2)
    x1, h1 = _resid_norm("resid_mix0", x, mix0, g(0, 1), g(0, 2))
    u0, act0, m0 = _mlp_fwd("l0", h1, w_up, 0, w_down)
    x2, h2 = _resid_norm("resid_mlp0", x1, m0, g(0, 3), g(1, 0))
    (qkv,) = _mm_nn("w_qkv_fwd", h2, w_qkv, BF16, out_seg=3)
    o, rmat = _attention_fwd(qkv)
    (mix1,) = _mm_nn("w_o_fwd", o, w_o, F32)
    x3, h3 = _resid_norm("resid_mix1", x2, mix1, g(1, 1), g(1, 2))
    u1, act1, m1 = _mlp_fwd("l1", h3, w_up, 1, w_down)
    dx4, sq = _final_loss("loss", x3, m1, g(1, 3), target)

    dm1, dg13 = _norm_bwd("norm_bwd_m1", m1, g(1, 3), dx4, None, BF16)
    dw_down1, dw_up1, dh3 = _mlp_bwd("l1", h3, u1, act1, dm1, w_up, 1, w_down)
    dx3, dg12 = _norm_bwd("norm_bwd_x3", x3, g(1, 2), dh3, dx4, F32)
    dmix1, dg11 = _norm_bwd("norm_bwd_mix1", mix1, g(1, 1), dx3, None, BF16)
    dw_o = _mm_tn("w_o_dw", o, dmix1, BF16)
    do = _mm_nt("w_o_dx", dmix1, w_o, BF16)
    dqkv = _attention_bwd(qkv, do, rmat)
    dw_qkv = _mm_tn("w_qkv_dw", h2, dqkv, BF16, b_seg=3)
    dh2 = _mm_nt("w_qkv_dx", dqkv, w_qkv, F32, a_seg=3)
    dx2, dg10 = _norm_bwd("norm_bwd_x2", x2, g(1, 0), dh2, dx3, F32)
    dm0, dg03 = _norm_bwd("norm_bwd_m0", m0, g(0, 3), dx2, None, BF16)
    dw_down0, dw_up0, dh1 = _mlp_bwd("l0", h1, u0, act0, dm0, w_up, 0, w_down)
    dx1, dg02 = _norm_bwd("norm_bwd_x1", x1, g(0, 2), dh1, dx2, F32)
    dmix0, dg01 = _norm_bwd("norm_bwd_mix0", mix0, g(0, 1), dx1, None, BF16)
    dw_out = _mm_tn("w_out_dw", y, dmix0, BF16, a_seg=2)
    dy = _mm_nt("w_out_dx", dmix0, w_out, F32, out_seg=2)
    dproj_a, dconv_a = _mixer_a_bwd(proj, conv_a, dy)
    dproj_b, sm_b, dwa_p, dwx_p = _mixer_b_bwd(proj, conv_b, conv_b_bias, wa_p, rg_b_a, wx_p, rg_b_x, rg_lambda, dy)
    dproj = jnp.concatenate([dproj_a, dproj_b], axis=0)
    dw_in = _mm_tn("w_in_dw", h0, dproj, BF16, b_seg=5)
    dh0 = _mm_nt("w_in_dx", dproj, w_in, F32, a_seg=5)
    dx0, dg00 = _norm_bwd("norm_bwd_x0", x, g(0, 0), dh0, dx1, F32)

    C = D // 2
    lanes_to_vec = lambda t, row: t[:, row, :].reshape(1, C)
    small = {
        "norm_gains": jnp.concatenate([dg00, dg01, dg02, dg03, dg10, dg11, dg12, dg13], axis=0).reshape(2, 4, D),
        "conv_a": jnp.transpose(dconv_a[:, :3, :], (1, 0, 2)).reshape(3, C),
        "conv_b": jnp.transpose(sm_b[:, :4, :], (1, 0, 2)).reshape(4, C),
        "conv_b_bias": lanes_to_vec(sm_b, _ROW_BIAS),
        "rg_w_a": _diag_pairs(dwa_p),
        "rg_b_a": lanes_to_vec(sm_b, _ROW_BA),
        "rg_w_x": _diag_pairs(dwx_p),
        "rg_b_x": lanes_to_vec(sm_b, _ROW_BX),
        "rg_lambda": lanes_to_vec(sm_b, _ROW_LAM),
    }
    big = {"w_in": dw_in, "w_out": dw_out, "w_qkv": dw_qkv, "w_o": dw_o,
           "w_up": (dw_up0, dw_up1), "w_down": (dw_down0, dw_down1)}
    return sq[0, 0], dx0, small, big


def _my_index():
    return 4 * lax.axis_index("x") + 2 * lax.axis_index("y") + lax.axis_index("c")


def _peers():
    x, y, c = lax.axis_index("x"), lax.axis_index("y"), lax.axis_index("c")
    out = []
    for k in range(1, N_DEV):
        px = x ^ ((k >> 2) & 1)
        py = y ^ ((k >> 1) & 1)
        pc = c ^ (k & 1)
        out.append(((px, py, pc), 4 * px + 2 * py + pc))
    return out


_ANY = pl.BlockSpec(memory_space=pl.ANY)


def _all_gather(name, shards):
    n = len(shards)

    def body(*refs):
        srcs, dsts = refs[:n], refs[n:2 * n]
        send_sems, recv_sems, local_sems = refs[2 * n:]
        me = _my_index()
        peers = _peers()
        copies = []
        for a in range(n):
            lc = pltpu.make_async_copy(srcs[a], dsts[a].at[me], local_sems.at[a])
            lc.start()
            copies.append(lc)
        remote = []
        for a in range(n):
            for k, (pos, _) in enumerate(peers):
                cp = pltpu.make_async_remote_copy(
                    src_ref=srcs[a], dst_ref=dsts[a].at[me], send_sem=send_sems.at[a, k], recv_sem=recv_sems.at[a, k],
                    device_id=pos, device_id_type=MESH)
                cp.start()
                remote.append(cp)
        for a in range(n):
            for k, (pos, idx) in enumerate(peers):
                pltpu.make_async_remote_copy(
                    src_ref=srcs[a], dst_ref=dsts[a].at[idx], send_sem=send_sems.at[a, k], recv_sem=recv_sems.at[a, k],
                    device_id=pos, device_id_type=MESH).wait_recv()
        for cp in remote:
            cp.wait_send()
        for lc in copies:
            lc.wait()

    return pl.pallas_call(
        body, name=name,
        in_specs=[_ANY] * n, out_specs=[_ANY] * n,
        out_shape=[jax.ShapeDtypeStruct((N_DEV,) + s.shape, s.dtype) for s in shards],
        scratch_shapes=[pltpu.SemaphoreType.DMA((n, N_DEV - 1)), pltpu.SemaphoreType.DMA((n, N_DEV - 1)),
                        pltpu.SemaphoreType.DMA((n,))],
    )(*shards)


def _exchange_blocks(name, pieces, lands):
    n, nl = len(pieces), len(lands)

    def body(*refs):
        srcs, dsts = refs[:n], refs[n:n + nl]
        send_sems, recv_sems, local_sems = refs[n + nl:]
        me = _my_index()
        peers = _peers()

        def land(a, slot):
            _, lid, lead = pieces[a]
            r = dsts[lid].at[slot]
            for i in lead:
                r = r.at[i]
            return r

        copies = []
        for a in range(n):
            lc = pltpu.make_async_copy(srcs[a].at[me], land(a, me), local_sems.at[a])
            lc.start()
            copies.append(lc)
        remote = []
        for a in range(n):
            for k, (pos, idx) in enumerate(peers):
                cp = pltpu.make_async_remote_copy(
                    src_ref=srcs[a].at[idx], dst_ref=land(a, me), send_sem=send_sems.at[a, k],
                    recv_sem=recv_sems.at[a, k], device_id=pos, device_id_type=MESH)
                cp.start()
                remote.append(cp)
        for a in range(n):
            for k, (pos, idx) in enumerate(peers):
                pltpu.make_async_remote_copy(
                    src_ref=srcs[a].at[idx], dst_ref=land(a, idx), send_sem=send_sems.at[a, k],
                    recv_sem=recv_sems.at[a, k], device_id=pos, device_id_type=MESH).wait_recv()
        for cp in remote:
            cp.wait_send()
        for lc in copies:
            lc.wait()

    return pl.pallas_call(
        body, name=name,
        in_specs=[_ANY] * n, out_specs=[_ANY] * nl,
        out_shape=lands,
        scratch_shapes=[pltpu.SemaphoreType.DMA((n, N_DEV - 1)), pltpu.SemaphoreType.DMA((n, N_DEV - 1)),
                        pltpu.SemaphoreType.DMA((n,))],
    )(*[p[0] for p in pieces])


def _adamw_math(w, g, m, v):
    m = ADAM_B1 * m + (1.0 - ADAM_B1) * g
    v = ADAM_B2 * v + (1.0 - ADAM_B2) * (g * g)
    m_hat = m / (1.0 - ADAM_B1 ** ADAM_STEP)
    v_hat = v / (1.0 - ADAM_B2 ** ADAM_STEP)
    delta = -ADAM_LR * (m_hat / (jnp.sqrt(v_hat) + ADAM_EPS) + ADAM_WD * w)
    return delta, m, v


def _sum_slots(ref):
    g = ref[0].astype(F32)
    for s in range(1, N_DEV):
        g = g + ref[s].astype(F32)
    return g


def _adamw_big(name, land, w, m, v):
    R, C = w.shape
    tr = _tile(R, max(LANES, (256 * 1024) // C))

    def body(l_ref, w_ref, m_ref, v_ref, g_ref, d_ref, nm_ref, nv_ref):
        g = _sum_slots(l_ref)
        d, nm, nv = _adamw_math(w_ref[...], g, m_ref[...], v_ref[...])
        g_ref[...] = g
        d_ref[...] = d
        nm_ref[...] = nm
        nv_ref[...] = nv

    row = pl.BlockSpec((tr, C), lambda i: (i, 0))
    return pl.pallas_call(
        body, name=name, grid=(R // tr,),
        in_specs=[pl.BlockSpec((N_DEV, tr, C), lambda i: (0, i, 0)), row, row, row],
        out_specs=[row] * 4, out_shape=[jax.ShapeDtypeStruct((R, C), F32)] * 4,
        compiler_params=_params(("parallel",)))(land, w, m, v)


def _sum8(name, slots):
    _, R, C = slots.shape

    def body(s_ref, o_ref):
        o_ref[...] = _sum_slots(s_ref)

    return pl.pallas_call(body, name=name, out_shape=jax.ShapeDtypeStruct((R, C), F32))(slots)


def _adamw_small(name, g, w, m, v):
    def body(g_ref, w_ref, m_ref, v_ref, d_ref, nm_ref, nv_ref):
        d, nm, nv = _adamw_math(w_ref[...], g_ref[...], m_ref[...], v_ref[...])
        d_ref[...] = d
        nm_ref[...] = nm
        nv_ref[...] = nv

    return pl.pallas_call(body, name=name, out_shape=[jax.ShapeDtypeStruct(w.shape, F32)] * 3)(g, w, m, v)


def _pack_rows(arrs):
    parts, spans, r0 = [], [], 0
    for a in arrs:
        flat = a.astype(F32).reshape(-1)
        rows = -(-flat.shape[0] // LANES)
        rows = -(-rows // SUBLANES) * SUBLANES
        flat = jnp.pad(flat, (0, rows * LANES - flat.shape[0]))
        parts.append(flat.reshape(rows, LANES))
        spans.append((r0, rows, a.shape))
        r0 += rows
    return jnp.concatenate(parts, axis=0), spans


def _unpack_rows(buf, span):
    r0, rows, shape = span
    n = math.prod(shape)
    return buf[..., r0:r0 + rows, :].reshape(buf.shape[:-2] + (rows * LANES,))[..., :n].reshape(buf.shape[:-2] + shape)


def _col_blocks(w, n_blocks):
    K, N = w.shape
    return jnp.transpose(w.reshape(K, n_blocks, N // n_blocks), (1, 0, 2))


def _from_col_blocks(wb):
    B, K, n = wb.shape
    return jnp.transpose(wb, (1, 0, 2)).reshape(K, B * n)


def kernel(x, norm_gains, hyb_w_in, hyb_conv_a, hyb_conv_b, hyb_conv_b_bias, hyb_rg_w_a, hyb_rg_b_a, hyb_rg_w_x, hyb_rg_b_x, hyb_rg_lambda, hyb_w_out, sb_w_qkv, sb_w_o, mlp_w_up, mlp_w_down, loss_target, m_norm_gains, m_hyb_w_in, m_hyb_conv_a, m_hyb_conv_b, m_hyb_conv_b_bias, m_hyb_rg_w_a, m_hyb_rg_b_a, m_hyb_rg_w_x, m_hyb_rg_b_x, m_hyb_rg_lambda, m_hyb_w_out, m_sb_w_qkv, m_sb_w_o, m_mlp_w_up, m_mlp_w_down, v_norm_gains, v_hyb_w_in, v_hyb_conv_a, v_hyb_conv_b, v_hyb_conv_b_bias, v_hyb_rg_w_a, v_hyb_rg_b_a, v_hyb_rg_w_x, v_hyb_rg_b_x, v_hyb_rg_lambda, v_hyb_w_out, v_sb_w_qkv, v_sb_w_o, v_mlp_w_up, v_mlp_w_down):
    T, D = x.shape[1], x.shape[2]
    me = _my_index()

    small_shards, small_spans = _pack_rows([norm_gains, hyb_conv_a[0], hyb_conv_b[0]])
    (small_all,) = _all_gather("gather_small", [small_shards])
    gains_b = _unpack_rows(small_all, small_spans[0])
    gains = jnp.transpose(gains_b, (1, 2, 0, 3)).reshape(2, 4, D)
    conv_a = _from_col_blocks(_unpack_rows(small_all, small_spans[1]))
    conv_b = _from_col_blocks(_unpack_rows(small_all, small_spans[2]))

    big_shards = [hyb_w_in[0].astype(BF16), hyb_w_out[0].astype(BF16), sb_w_qkv[0].astype(BF16),
                  sb_w_o[0].astype(BF16), mlp_w_up.astype(BF16), mlp_w_down.astype(BF16)]
    g_in, g_out, g_qkv, g_o, g_up, g_down = _all_gather("gather_weights", big_shards)
    w_in = _from_col_blocks(g_in)
    w_qkv = _from_col_blocks(g_qkv)
    w_out = g_out.reshape(D, D)
    w_o = g_o.reshape(D, D)

    sq, grad_x, small, big = _local_step(
        x[0], loss_target[0], gains, w_in, conv_a, conv_b, hyb_conv_b_bias, hyb_rg_w_a[0], hyb_rg_b_a, hyb_rg_w_x[0],
        hyb_rg_b_x, hyb_rg_lambda, w_out, w_qkv, w_o, g_up, g_down)
    loss = lax.psum(0.5 * sq / D, ("x", "y", "c"))

    rows_o = D // N_DEV
    pieces = [
        (_col_blocks(big["w_in"], N_DEV), 0, ()),
        (big["w_out"].reshape(N_DEV, rows_o, D), 1, ()),
        (_col_blocks(big["w_qkv"], N_DEV), 2, ()),
        (big["w_o"].reshape(N_DEV, rows_o, D), 3, ()),
        (big["w_up"][0], 4, (0,)), (big["w_up"][1], 4, (1,)),
        (big["w_down"][0].reshape((N_DEV,) + mlp_w_down.shape[1:]), 5, (0,)),
        (big["w_down"][1].reshape((N_DEV,) + mlp_w_down.shape[1:]), 5, (1,)),
    ]
    shard_shapes = [hyb_w_in.shape[1:], hyb_w_out.shape[1:], sb_w_qkv.shape[1:], sb_w_o.shape[1:],
                    mlp_w_up.shape, mlp_w_down.shape]
    lands = _exchange_blocks("exchange_grads", pieces,
                             [jax.ShapeDtypeStruct((N_DEV,) + tuple(s), BF16) for s in shard_shapes])

    names = ["norm_gains", "hyb_w_in", "hyb_conv_a", "hyb_conv_b", "hyb_conv_b_bias", "hyb_rg_w_a", "hyb_rg_b_a",
             "hyb_rg_w_x", "hyb_rg_b_x", "hyb_rg_lambda", "hyb_w_out", "sb_w_qkv", "sb_w_o", "mlp_w_up", "mlp_w_down"]
    params = dict(zip(names, [norm_gains, hyb_w_in, hyb_conv_a, hyb_conv_b, hyb_conv_b_bias, hyb_rg_w_a, hyb_rg_b_a,
                              hyb_rg_w_x, hyb_rg_b_x, hyb_rg_lambda, hyb_w_out, sb_w_qkv, sb_w_o, mlp_w_up, mlp_w_down]))
    moms = dict(zip(names, [m_norm_gains, m_hyb_w_in, m_hyb_conv_a, m_hyb_conv_b, m_hyb_conv_b_bias, m_hyb_rg_w_a,
                            m_hyb_rg_b_a, m_hyb_rg_w_x, m_hyb_rg_b_x, m_hyb_rg_lambda, m_hyb_w_out, m_sb_w_qkv,
                            m_sb_w_o, m_mlp_w_up, m_mlp_w_down]))
    vars_ = dict(zip(names, [v_norm_gains, v_hyb_w_in, v_hyb_conv_a, v_hyb_conv_b, v_hyb_conv_b_bias, v_hyb_rg_w_a,
                             v_hyb_rg_b_a, v_hyb_rg_w_x, v_hyb_rg_b_x, v_hyb_rg_lambda, v_hyb_w_out, v_sb_w_qkv,
                             v_sb_w_o, v_mlp_w_up, v_mlp_w_down]))
    grads, deltas, new_m, new_v = {}, {}, {}, {}

    big_names = ["hyb_w_in", "hyb_w_out", "sb_w_qkv", "sb_w_o", "mlp_w_up", "mlp_w_down"]
    for nm, land in zip(big_names, lands):
        shape = params[nm].shape
        cols = shape[-1]
        flat = lambda a: a.reshape(-1, cols)
        outs = _adamw_big(f"adamw_{nm}", land.reshape(N_DEV, -1, cols), flat(params[nm]), flat(moms[nm]), flat(vars_[nm]))
        grads[nm], deltas[nm], new_m[nm], new_v[nm] = [o.reshape(shape) for o in outs]

    small_names = ["norm_gains", "hyb_conv_a", "hyb_conv_b", "hyb_conv_b_bias", "hyb_rg_w_a", "hyb_rg_b_a",
                   "hyb_rg_w_x", "hyb_rg_b_x", "hyb_rg_lambda"]
    small_keys = ["norm_gains", "conv_a", "conv_b", "conv_b_bias", "rg_w_a", "rg_b_a", "rg_w_x", "rg_b_x", "rg_lambda"]
    sg_buf, sg_spans = _pack_rows([small[k] for k in small_keys])
    (sg_all,) = _all_gather("gather_small_grads", [sg_buf])
    sg_sum = _sum8("sum_small_grads", sg_all)
    full = {nm: _unpack_rows(sg_sum, sp) for nm, sp in zip(small_names, sg_spans)}
    cb = (D // 2) // N_DEV
    small_grads = {
        "norm_gains": lax.dynamic_slice_in_dim(full["norm_gains"], me * (D // N_DEV), D // N_DEV, axis=2),
        "hyb_conv_a": lax.dynamic_slice_in_dim(full["hyb_conv_a"], me * cb, cb, axis=1)[None],
        "hyb_conv_b": lax.dynamic_slice_in_dim(full["hyb_conv_b"], me * cb, cb, axis=1)[None],
        "hyb_conv_b_bias": full["hyb_conv_b_bias"],
        "hyb_rg_w_a": full["hyb_rg_w_a"][None],
        "hyb_rg_b_a": full["hyb_rg_b_a"],
        "hyb_rg_w_x": full["hyb_rg_w_x"][None],
        "hyb_rg_b_x": full["hyb_rg_b_x"],
        "hyb_rg_lambda": full["hyb_rg_lambda"],
    }
    pk = lambda d: _pack_rows([d[nm] for nm in small_names])
    g_buf, spans = pk(small_grads)
    w_buf, _ = pk(params)
    m_buf, _ = pk(moms)
    v_buf, _ = pk(vars_)
    d_buf, nm_buf, nv_buf = _adamw_small("adamw_small", g_buf, w_buf, m_buf, v_buf)
    for nm, sp in zip(small_names, spans):
        grads[nm] = small_grads[nm]
        deltas[nm], new_m[nm], new_v[nm] = _unpack_rows(d_buf, sp), _unpack_rows(nm_buf, sp), _unpack_rows(nv_buf, sp)

    return (loss, grad_x[None], *[grads[n] for n in names], *[deltas[n] for n in names],
            *[new_m[n] for n in names], *[new_v[n] for n in names])
```

```python
import functools
import math

import jax
import jax.numpy as jnp
from jax import lax
from jax.experimental import pallas as pl
from jax.experimental.pallas import tpu as pltpu

F32 = jnp.float32
BF16 = jnp.bfloat16

NORM_EPS = 1e-6
LRU_C = 8.0
ATT_HEAD_DIM = 128
RG_HEAD_DIM = 64
LANES = 128
SUBLANES = 8
N_DEV = 8
ADAM_LR = 0.001
ADAM_B1 = 0.9
ADAM_B2 = 0.999
ADAM_EPS = 1e-08
ADAM_WD = 0.01
ADAM_STEP = 10
VMEM_LIMIT = 56 * 1024 * 1024
MESH = pl.DeviceIdType.MESH


def _tile(n, pref):
    if n <= pref:
        return n
    t = (pref // LANES) * LANES
    while t > LANES and n % t:
        t -= LANES
    assert n % t == 0, (n, pref)
    return t


def _params(sem):
    return pltpu.CompilerParams(dimension_semantics=sem, vmem_limit_bytes=VMEM_LIMIT)


DIMS_NN = (((1,), (0,)), ((), ()))
DIMS_NT = (((1,), (1,)), ((), ()))
DIMS_TN = (((0,), (0,)), ((), ()))


_ANY = pl.BlockSpec(memory_space=pl.ANY)


class _Job:
    def __init__(self, ins, outs, sems, start, mid, end, alias=None):
        self.ins, self.outs, self.sems = ins, outs, sems
        self.start, self.mid, self.end = start, mid, end
        self.alias = alias or {}


def _job_plumbing(jobs, n_in, n_out):
    j_ins = [a for jb in jobs for a in jb.ins]
    j_outs = [o for jb in jobs for o in jb.outs]
    j_sems = [s for jb in jobs for s in jb.sems]
    aliases, pi, po = {}, 0, 0
    for jb in jobs:
        for i_in, i_out in jb.alias.items():
            aliases[n_in + pi + i_in] = n_out + po + i_out
        pi += len(jb.ins)
        po += len(jb.outs)
    return j_ins, j_outs, j_sems, aliases


def _job_phase(jobs, which, jin, jout, jsem):
    pi = po = ps = 0
    for jb in jobs:
        getattr(jb, which)(jin[pi:pi + len(jb.ins)], jout[po:po + len(jb.outs)], jsem[ps:ps + len(jb.sems)])
        pi, po, ps = pi + len(jb.ins), po + len(jb.outs), ps + len(jb.sems)


def _run_jobs(name, jobs):
    j_ins, j_outs, j_sems, aliases = _job_plumbing(jobs, 0, 0)
    n_ji, n_jo = len(j_ins), len(j_outs)

    def body(*refs):
        jin, jout, jsem = refs[:n_ji], refs[n_ji:n_ji + n_jo], refs[n_ji + n_jo:]
        for which in ("start", "mid", "end"):
            _job_phase(jobs, which, jin, jout, jsem)

    return pl.pallas_call(body, name=name, in_specs=[_ANY] * n_ji, out_specs=[_ANY] * n_jo, out_shape=j_outs,
                          scratch_shapes=j_sems, input_output_aliases=aliases)(*j_ins)


def _matmul(name, ins, outs, grid, dims, acc_shape, epilogue=None, jobs=()):
    n_in, n_out, nk = len(ins), len(outs), grid[2]
    j_ins, j_outs, j_sems, aliases = _job_plumbing(jobs, n_in, n_out)
    n_ji, n_jo = len(j_ins), len(j_outs)
    total = grid[0] * grid[1] * grid[2]

    def body(*refs):
        a_ref, b_ref = refs[0], refs[1]
        extras = refs[2:n_in]
        jin = refs[n_in:n_in + n_ji]
        out_refs = refs[n_in + n_ji:n_in + n_ji + n_out]
        jout = refs[n_in + n_ji + n_out:n_in + n_ji + n_out + n_jo]
        acc = refs[n_in + n_ji + n_out + n_jo]
        jsem = refs[n_in + n_ji + n_out + n_jo + 1:]
        k = pl.program_id(2)
        step = (pl.program_id(0) * grid[1] + pl.program_id(1)) * grid[2] + k
        if jobs:
            pl.when(step == 0)(lambda: _job_phase(jobs, "start", jin, jout, jsem))
            pl.when(step == total // 2)(lambda: _job_phase(jobs, "mid", jin, jout, jsem))

        @pl.when(k == 0)
        def _():
            acc[...] = jnp.zeros_like(acc)

        acc[...] += lax.dot_general(a_ref[...], b_ref[...], dims, preferred_element_type=F32)

        @pl.when(k == nk - 1)
        def _():
            r = acc[...]
            res = epilogue(r, *[e[...] for e in extras]) if epilogue is not None else (r,)
            for o, v in zip(out_refs, res):
                o[...] = v.astype(o.dtype)

        if jobs:
            pl.when(step == total - 1)(lambda: _job_phase(jobs, "end", jin, jout, jsem))

    sem = ("arbitrary",) * 3 if jobs else ("parallel", "parallel", "arbitrary")
    res = pl.pallas_call(
        body, name=name, grid=grid,
        in_specs=[s for _, s in ins] + [_ANY] * n_ji,
        out_specs=[s for _, s in outs] + [_ANY] * n_jo,
        out_shape=[s for s, _ in outs] + j_outs,
        scratch_shapes=[pltpu.VMEM(acc_shape, F32)] + j_sems,
        input_output_aliases=aliases,
        compiler_params=_params(sem),
    )(*[a for a, _ in ins], *j_ins)
    return res


def _mm_nn(name, a, b, out_dtype, *, a_seg=None, out_seg=None, tm=1024, tn=1024, tk=512, epilogue=None,
           extras=(), n_out=1, out_dtypes=None, jobs=()):
    if a_seg:
        _, M, ks = a.shape
        K = ks * a_seg
    else:
        M, K = a.shape
        ks = K
    N = b.shape[1]
    ns = N // out_seg if out_seg else N
    tm, tn, tk = _tile(M, tm), _tile(ns, tn), _tile(ks, tk)
    nks, nns = ks // tk, ns // tn
    grid = (M // tm, N // tn, K // tk)
    if a_seg:
        a_spec = pl.BlockSpec((None, tm, tk), lambda i, j, k: (k // nks, i, k % nks))
    else:
        a_spec = pl.BlockSpec((tm, tk), lambda i, j, k: (i, k))
    b_spec = pl.BlockSpec((tk, tn), lambda i, j, k: (k, j))
    if out_seg:
        o_spec = pl.BlockSpec((None, tm, tn), lambda i, j, k: (j // nns, i, j % nns))
        o_shape = (out_seg, M, ns)
    else:
        o_spec = pl.BlockSpec((tm, tn), lambda i, j, k: (i, j))
        o_shape = (M, N)
    dts = out_dtypes or [out_dtype] * n_out
    outs = [(jax.ShapeDtypeStruct(o_shape, dt), o_spec) for dt in dts]
    ins = [(a, a_spec), (b, b_spec)] + [(e, o_spec) for e in extras]
    return _matmul(name, ins, outs, grid, DIMS_NN, (tm, tn), epilogue, jobs)


def _mm_nt(name, a, b, out_dtype, *, a_seg=None, out_seg=None, tm=1024, tn=1024, tk=512, epilogue=None, extras=(),
           jobs=()):
    if a_seg:
        _, M, ks = a.shape
        K = ks * a_seg
    else:
        M, K = a.shape
        ks = K
    N = b.shape[0]
    ns = N // out_seg if out_seg else N
    tm, tn, tk = _tile(M, tm), _tile(ns, tn), _tile(ks, tk)
    nks, nns = ks // tk, ns // tn
    grid = (M // tm, N // tn, K // tk)
    if a_seg:
        a_spec = pl.BlockSpec((None, tm, tk), lambda i, j, k: (k // nks, i, k % nks))
    else:
        a_spec = pl.BlockSpec((tm, tk), lambda i, j, k: (i, k))
    b_spec = pl.BlockSpec((tn, tk), lambda i, j, k: (j, k))
    if out_seg:
        o_spec = pl.BlockSpec((None, tm, tn), lambda i, j, k: (j // nns, i, j % nns))
        o_shape = (out_seg, M, ns)
    else:
        o_spec = pl.BlockSpec((tm, tn), lambda i, j, k: (i, j))
        o_shape = (M, N)
    outs = [(jax.ShapeDtypeStruct(o_shape, out_dtype), o_spec)]
    ins = [(a, a_spec), (b, b_spec)] + [(e, o_spec) for e in extras]
    return _matmul(name, ins, outs, grid, DIMS_NT, (tm, tn), epilogue, jobs)


def _mm_tn(name, a, b, out_dtype, *, a_seg=None, b_seg=None, out_blocks=None, tm=1024, tn=1024, tk=512, jobs=()):
    if a_seg:
        _, T, ms = a.shape
        M = ms * a_seg
    else:
        T, M = a.shape
        ms = M
    if b_seg:
        _, _, ns = b.shape
        N = ns * b_seg
    else:
        N = b.shape[1]
        ns = N
    nb_cols = N // out_blocks if out_blocks else N
    tm, tk = _tile(ms, tm), _tile(T, tk)
    tn = _tile(math.gcd(ns, nb_cols), tn)
    nms, nns, nbs = ms // tm, ns // tn, nb_cols // tn
    grid = (M // tm, N // tn, T // tk)
    if a_seg:
        a_spec = pl.BlockSpec((None, tk, tm), lambda i, j, k: (i // nms, k, i % nms))
    else:
        a_spec = pl.BlockSpec((tk, tm), lambda i, j, k: (k, i))
    if b_seg:
        b_spec = pl.BlockSpec((None, tk, tn), lambda i, j, k: (j // nns, k, j % nns))
    else:
        b_spec = pl.BlockSpec((tk, tn), lambda i, j, k: (k, j))
    if out_blocks:
        o_spec = pl.BlockSpec((None, tm, tn), lambda i, j, k: (j // nbs, i, j % nbs))
        o_shape = (out_blocks, M, nb_cols)
    else:
        o_spec = pl.BlockSpec((tm, tn), lambda i, j, k: (i, j))
        o_shape = (M, N)
    outs = [(jax.ShapeDtypeStruct(o_shape, out_dtype), o_spec)]
    return _matmul(name, [(a, a_spec), (b, b_spec)], outs, grid, DIMS_TN, (tm, tn), None, jobs)


def _rms(x):
    return lax.rsqrt(jnp.mean(x * x, axis=-1, keepdims=True) + NORM_EPS)


def _row_tile(T):
    return _tile(T, 256)


def _norm_fwd(name, x, g):
    T, D = x.shape
    tr = _row_tile(T)

    def body(x_ref, g_ref, h_ref):
        xv = x_ref[...]
        h_ref[...] = (xv * _rms(xv) * g_ref[...]).astype(h_ref.dtype)

    row = pl.BlockSpec((tr, D), lambda i: (i, 0))
    vec = pl.BlockSpec((1, D), lambda i: (0, 0))
    return pl.pallas_call(body, name=name, grid=(T // tr,), in_specs=[row, vec], out_specs=row,
                          out_shape=jax.ShapeDtypeStruct((T, D), BF16), compiler_params=_params(("parallel",)))(x, g)


def _resid_norm(name, x, br, g_post, g_next):
    T, D = x.shape
    tr = _row_tile(T)

    def body(x_ref, br_ref, gp_ref, gn_ref, xn_ref, h_ref):
        b = br_ref[...]
        xn = x_ref[...] + b * _rms(b) * gp_ref[...]
        xn_ref[...] = xn
        h_ref[...] = (xn * _rms(xn) * gn_ref[...]).astype(h_ref.dtype)

    row = pl.BlockSpec((tr, D), lambda i: (i, 0))
    vec = pl.BlockSpec((1, D), lambda i: (0, 0))
    return pl.pallas_call(body, name=name, grid=(T // tr,), in_specs=[row, row, vec, vec], out_specs=[row, row],
                          out_shape=[jax.ShapeDtypeStruct((T, D), F32), jax.ShapeDtypeStruct((T, D), BF16)],
                          compiler_params=_params(("parallel",)))(x, br, g_post, g_next)


def _final_loss(name, x, br, g_post, target):
    T, D = x.shape
    tr = _row_tile(T)

    def body(x_ref, br_ref, gp_ref, t_ref, dy_ref, ls_ref):
        b = br_ref[...]
        err = x_ref[...] + b * _rms(b) * gp_ref[...] - t_ref[...]
        dy_ref[...] = err * (1.0 / D)

        @pl.when(pl.program_id(0) == 0)
        def _():
            ls_ref[...] = jnp.zeros_like(ls_ref)

        ls_ref[...] += jnp.sum(err * err)

    row = pl.BlockSpec((tr, D), lambda i: (i, 0))
    vec = pl.BlockSpec((1, D), lambda i: (0, 0))
    acc = pl.BlockSpec((SUBLANES, LANES), lambda i: (0, 0))
    return pl.pallas_call(body, name=name, grid=(T // tr,), in_specs=[row, row, vec, row], out_specs=[row, acc],
                          out_shape=[jax.ShapeDtypeStruct((T, D), F32), jax.ShapeDtypeStruct((SUBLANES, LANES), F32)],
                          compiler_params=_params(("arbitrary",)))(x, br, g_post, target)


def _norm_bwd(name, x, g, dy, add, out_dtype):
    T, D = x.shape
    tr = _row_tile(T)
    has_add = add is not None

    def body(*refs):
        if has_add:
            x_ref, g_ref, dy_ref, add_ref, dx_ref, dg_ref = refs
        else:
            x_ref, g_ref, dy_ref, dx_ref, dg_ref = refs
        xv = x_ref[...]
        r = _rms(xv)
        xhat = xv * r
        dyv = dy_ref[...].astype(F32)
        gdy = dyv * g_ref[...]
        dx = r * (gdy - xhat * jnp.mean(gdy * xhat, axis=-1, keepdims=True))
        if has_add:
            dx = dx + add_ref[...]
        dx_ref[...] = dx.astype(dx_ref.dtype)

        @pl.when(pl.program_id(0) == 0)
        def _():
            dg_ref[...] = jnp.zeros_like(dg_ref)

        dg_ref[...] += jnp.sum(dyv * xhat, axis=0, keepdims=True)

    row = pl.BlockSpec((tr, D), lambda i: (i, 0))
    vec = pl.BlockSpec((1, D), lambda i: (0, 0))
    ins = [x, g, dy] + ([add] if has_add else [])
    specs = [row, vec, row] + ([row] if has_add else [])
    return pl.pallas_call(body, name=name, grid=(T // tr,), in_specs=specs, out_specs=[row, vec],
                          out_shape=[jax.ShapeDtypeStruct((T, D), out_dtype), jax.ShapeDtypeStruct((1, D), F32)],
                          compiler_params=_params(("arbitrary",)))(*ins)


HALO = SUBLANES
TIME_CHUNK = 512


def _chunks(T):
    tc = min(TIME_CHUNK, T)
    assert T % tc == 0 and tc % SUBLANES == 0
    return [(t0, tc) for t0 in range(0, T, tc)]


def _log_sigmoid(x):
    return -(jnp.maximum(-x, 0.0) + jnp.log(1.0 + jnp.exp(-jnp.abs(x))))


def _one_minus_exp(x):
    series = -x * (1.0 + x * (0.5 + x * (1.0 / 6.0 + x * (1.0 / 24.0))))
    return jnp.where(x > -0.01, series, 1.0 - jnp.exp(x))


_GELU_C = math.sqrt(2.0 / math.pi)


def _gelu(x):
    return 0.5 * x * (1.0 + jnp.tanh(_GELU_C * (x + 0.044715 * x * x * x)))


def _gelu_grad(x):
    th = jnp.tanh(_GELU_C * (x + 0.044715 * x * x * x))
    return 0.5 * (1.0 + th) + 0.5 * x * (1.0 - th * th) * _GELU_C * (1.0 + 3.0 * 0.044715 * x * x)


def _tile_scan(a, b, reverse):
    rows = a.shape[0]
    pos = lax.broadcasted_iota(jnp.int32, a.shape, 0) & (SUBLANES - 1)
    for d in (1, 2, 4):
        if reverse:
            ok = pos < SUBLANES - d
            shift = rows - d
        else:
            ok = pos >= d
            shift = d
        a_sh = jnp.where(ok, pltpu.roll(a, shift, 0), 1.0)
        b_sh = jnp.where(ok, pltpu.roll(b, shift, 0), 0.0)
        b = a * b_sh + b
        a = a * a_sh
    return a, b


def _carry_scan(a_s, b_s, T, reverse):
    n = T // SUBLANES
    edge = 0 if reverse else SUBLANES - 1

    def step(j, carry):
        g = (n - 1 - j) if reverse else j
        r = pl.multiple_of(g * SUBLANES, SUBLANES)
        h = b_s[pl.ds(r, SUBLANES), :] + a_s[pl.ds(r, SUBLANES), :] * carry
        b_s[pl.ds(r, SUBLANES), :] = h
        return jnp.broadcast_to(h[edge:edge + 1, :], h.shape)

    lax.fori_loop(0, n, step, jnp.zeros((SUBLANES, a_s.shape[1]), F32))


def _seg_spec(T, seg, nblk):
    return pl.BlockSpec((None, T, LANES), lambda c: (seg, 0, c))


def _rows_to_tile(rows):
    idx = lax.broadcasted_iota(jnp.int32, (SUBLANES, LANES), 0)
    out = jnp.zeros((SUBLANES, LANES), F32)
    for k, r in enumerate(rows):
        out = jnp.where(idx == k, r, out)
    return out


def _mixer_a_fwd(proj, conv_a):
    _, T, C = proj.shape
    nblk = C // LANES
    chunks = _chunks(T)

    def body(bg_ref, cg_ref, ax_ref, w_ref, y_ref, p_s):
        p_s[pl.ds(0, HALO), :] = jnp.zeros((HALO, LANES), F32)
        for t0, tc in chunks:
            p_s[pl.ds(HALO + t0, tc), :] = cg_ref[pl.ds(t0, tc), :] * ax_ref[pl.ds(t0, tc), :]
        w = w_ref[...]
        for t0, tc in chunks:
            c = (w[2:3, :] * p_s[pl.ds(HALO + t0, tc), :] + w[1:2, :] * p_s[pl.ds(HALO + t0 - 1, tc), :]
                 + w[0:1, :] * p_s[pl.ds(HALO + t0 - 2, tc), :])
            y_ref[pl.ds(t0, tc), :] = (bg_ref[pl.ds(t0, tc), :] * c).astype(y_ref.dtype)

    return pl.pallas_call(
        body, name="mixer_a_fwd", grid=(nblk,),
        in_specs=[_seg_spec(T, 0, nblk), _seg_spec(T, 1, nblk), _seg_spec(T, 2, nblk),
                  pl.BlockSpec((3, LANES), lambda c: (0, c))],
        out_specs=pl.BlockSpec((T, LANES), lambda c: (0, c)),
        out_shape=jax.ShapeDtypeStruct((T, C), BF16),
        scratch_shapes=[pltpu.VMEM((T + HALO, LANES), F32)],
        compiler_params=_params(("parallel",)))(proj, proj, proj, conv_a)


def _mixer_a_bwd(proj, conv_a, dy):
    _, T, C = proj.shape
    nblk = C // LANES
    chunks = _chunks(T)

    def body(bg_ref, cg_ref, ax_ref, w_ref, dy_ref, dp_ref, dw_ref, p_s, dc_s):
        p_s[pl.ds(0, HALO), :] = jnp.zeros((HALO, LANES), F32)
        dc_s[pl.ds(T, HALO), :] = jnp.zeros((HALO, LANES), F32)
        for t0, tc in chunks:
            p_s[pl.ds(HALO + t0, tc), :] = cg_ref[pl.ds(t0, tc), :] * ax_ref[pl.ds(t0, tc), :]
        w = w_ref[...]
        for t0, tc in chunks:
            c = (w[2:3, :] * p_s[pl.ds(HALO + t0, tc), :] + w[1:2, :] * p_s[pl.ds(HALO + t0 - 1, tc), :]
                 + w[0:1, :] * p_s[pl.ds(HALO + t0 - 2, tc), :])
            dyv = dy_ref[pl.ds(t0, tc), :]
            dp_ref[0, pl.ds(t0, tc), :] = (dyv * c).astype(dp_ref.dtype)
            dc_s[pl.ds(t0, tc), :] = dyv * bg_ref[pl.ds(t0, tc), :]
        dw = [jnp.zeros((1, LANES), F32) for _ in range(3)]
        for t0, tc in chunks:
            dc = dc_s[pl.ds(t0, tc), :]
            dpv = w[2:3, :] * dc + w[1:2, :] * dc_s[pl.ds(t0 + 1, tc), :] + w[0:1, :] * dc_s[pl.ds(t0 + 2, tc), :]
            dp_ref[1, pl.ds(t0, tc), :] = (dpv * ax_ref[pl.ds(t0, tc), :]).astype(dp_ref.dtype)
            dp_ref[2, pl.ds(t0, tc), :] = (dpv * cg_ref[pl.ds(t0, tc), :]).astype(dp_ref.dtype)
            for k in range(3):
                dw[k] = dw[k] + jnp.sum(dc * p_s[pl.ds(HALO + t0 - (2 - k), tc), :], axis=0, keepdims=True)
        dw_ref[...] = _rows_to_tile(dw)

    return pl.pallas_call(
        body, name="mixer_a_bwd", grid=(nblk,),
        in_specs=[_seg_spec(T, 0, nblk), _seg_spec(T, 1, nblk), _seg_spec(T, 2, nblk),
                  pl.BlockSpec((3, LANES), lambda c: (0, c)), _seg_spec(T, 0, nblk)],
        out_specs=[pl.BlockSpec((3, T, LANES), lambda c: (0, 0, c)),
                   pl.BlockSpec((None, SUBLANES, LANES), lambda c: (c, 0, 0))],
        out_shape=[jax.ShapeDtypeStruct((3, T, C), BF16), jax.ShapeDtypeStruct((nblk, SUBLANES, LANES), F32)],
        scratch_shapes=[pltpu.VMEM((T + HALO, LANES), F32), pltpu.VMEM((T + HALO, LANES), F32)],
        compiler_params=_params(("parallel",)))(proj, proj, proj, conv_a, dy)


def _rg_gates(xr, wa, ba, wx, bx, ls):
    xb = xr.astype(BF16)
    r = jax.nn.sigmoid(jnp.dot(xb, wa, preferred_element_type=F32) + ba)
    i = jax.nn.sigmoid(jnp.dot(xb, wx, preferred_element_type=F32) + bx)
    log_a = LRU_C * r * ls
    a = jnp.exp(log_a)
    mult = jnp.sqrt(_one_minus_exp(2.0 * log_a))
    return r, i, a, mult


def _conv4(xh_s, cw, bias, t0, tc):
    return (cw[3:4, :] * xh_s[pl.ds(HALO + t0, tc), :] + cw[2:3, :] * xh_s[pl.ds(HALO + t0 - 1, tc), :]
            + cw[1:2, :] * xh_s[pl.ds(HALO + t0 - 2, tc), :] + cw[0:1, :] * xh_s[pl.ds(HALO + t0 - 3, tc), :] + bias)


def _mixer_b_specs(T, nblk):
    vec = pl.BlockSpec((1, LANES), lambda c: (0, c))
    mat = pl.BlockSpec((None, LANES, LANES), lambda c: (c, 0, 0))
    return [_seg_spec(T, 3, nblk), _seg_spec(T, 4, nblk), pl.BlockSpec((4, LANES), lambda c: (0, c)),
            vec, mat, vec, mat, vec, vec]


def _mixer_b_fwd(proj, conv_b, bias, wa, ba, wx, bx, lam):
    _, T, C = proj.shape
    nblk = C // LANES
    chunks = _chunks(T)

    def body(gate_ref, x_ref, cw_ref, cb_ref, wa_ref, ba_ref, wx_ref, bx_ref, lam_ref, y_ref, xh_s, a_s, b_s):
        xh_s[pl.ds(0, HALO), :] = jnp.zeros((HALO, LANES), F32)
        for t0, tc in chunks:
            xh_s[pl.ds(HALO + t0, tc), :] = x_ref[pl.ds(t0, tc), :]
        cw, bias_v = cw_ref[...], cb_ref[...]
        ls = _log_sigmoid(lam_ref[...])
        for t0, tc in chunks:
            xr = _conv4(xh_s, cw, bias_v, t0, tc)
            r, i, a, mult = _rg_gates(xr, wa_ref[...], ba_ref[...], wx_ref[...], bx_ref[...], ls)
            ac, hc = _tile_scan(a, mult * i * xr, reverse=False)
            a_s[pl.ds(t0, tc), :] = ac
            b_s[pl.ds(t0, tc), :] = hc
        _carry_scan(a_s, b_s, T, reverse=False)
        for t0, tc in chunks:
            y_ref[pl.ds(t0, tc), :] = (b_s[pl.ds(t0, tc), :] * _gelu(gate_ref[pl.ds(t0, tc), :])).astype(y_ref.dtype)

    return pl.pallas_call(
        body, name="mixer_b_fwd", grid=(nblk,), in_specs=_mixer_b_specs(T, nblk),
        out_specs=pl.BlockSpec((T, LANES), lambda c: (0, c)),
        out_shape=jax.ShapeDtypeStruct((T, C), BF16),
        scratch_shapes=[pltpu.VMEM((T + HALO, LANES), F32), pltpu.VMEM((T, LANES), F32), pltpu.VMEM((T, LANES), F32)],
        compiler_params=_params(("parallel",)))(proj, proj, conv_b, bias, wa, ba, wx, bx, lam)


_ROW_CONV, _ROW_BIAS, _ROW_BA, _ROW_BX, _ROW_LAM = 0, 4, 5, 6, 7


def _mixer_b_bwd(proj, conv_b, bias, wa, ba, wx, bx, lam, dy):
    _, T, C = proj.shape
    nblk = C // LANES
    chunks = _chunks(T)

    def body(gate_ref, x_ref, cw_ref, cb_ref, wa_ref, ba_ref, wx_ref, bx_ref, lam_ref, dy_ref,
             dp_ref, sm_ref, dwa_ref, dwx_ref, xh_s, xr_s, r_s, i_s, a_s, h_s, sa_s, sb_s, dx_s):
        zero_halo = jnp.zeros((HALO, LANES), F32)
        xh_s[pl.ds(0, HALO), :] = zero_halo
        h_s[pl.ds(0, HALO), :] = zero_halo
        a_s[pl.ds(T, HALO), :] = zero_halo
        dx_s[pl.ds(T, HALO), :] = zero_halo
        for t0, tc in chunks:
            xh_s[pl.ds(HALO + t0, tc), :] = x_ref[pl.ds(t0, tc), :]
        cw, bias_v = cw_ref[...], cb_ref[...]
        lam_v = lam_ref[...]
        ls = _log_sigmoid(lam_v)
        wa_v, wx_v, ba_v, bx_v = wa_ref[...], wx_ref[...], ba_ref[...], bx_ref[...]
        for t0, tc in chunks:
            xr = _conv4(xh_s, cw, bias_v, t0, tc)
            r, i, a, mult = _rg_gates(xr, wa_v, ba_v, wx_v, bx_v, ls)
            xr_s[pl.ds(t0, tc), :] = xr
            r_s[pl.ds(t0, tc), :] = r
            i_s[pl.ds(t0, tc), :] = i
            a_s[pl.ds(t0, tc), :] = a
            ac, hc = _tile_scan(a, mult * i * xr, reverse=False)
            sa_s[pl.ds(t0, tc), :] = ac
            sb_s[pl.ds(t0, tc), :] = hc
        _carry_scan(sa_s, sb_s, T, reverse=False)
        for t0, tc in chunks:
            h_s[pl.ds(HALO + t0, tc), :] = sb_s[pl.ds(t0, tc), :]
        for t0, tc in chunks:
            gv = gate_ref[pl.ds(t0, tc), :]
            dyv = dy_ref[pl.ds(t0, tc), :]
            dp_ref[0, pl.ds(t0, tc), :] = (dyv * h_s[pl.ds(HALO + t0, tc), :] * _gelu_grad(gv)).astype(dp_ref.dtype)
            ac, gc = _tile_scan(a_s[pl.ds(t0 + 1, tc), :], dyv * _gelu(gv), reverse=True)
            sa_s[pl.ds(t0, tc), :] = ac
            sb_s[pl.ds(t0, tc), :] = gc
        _carry_scan(sa_s, sb_s, T, reverse=True)
        acc = {k: jnp.zeros((1, LANES), F32) for k in ("bias", "ba", "bx", "lam")}
        dwa = jnp.zeros((LANES, LANES), F32)
        dwx = jnp.zeros((LANES, LANES), F32)
        for t0, tc in chunks:
            dht = sb_s[pl.ds(t0, tc), :]
            xr, r, i, a = xr_s[pl.ds(t0, tc), :], r_s[pl.ds(t0, tc), :], i_s[pl.ds(t0, tc), :], a_s[pl.ds(t0, tc), :]
            mult = jnp.sqrt(_one_minus_exp(2.0 * LRU_C * r * ls))
            da = dht * h_s[pl.ds(HALO + t0 - 1, tc), :]
            dmult = dht * i * xr
            di = dht * mult * xr
            dlog_a = da * a - dmult * a * a / mult
            dpa = dlog_a * (LRU_C * ls) * r * (1.0 - r)
            dpx = di * i * (1.0 - i)
            acc["lam"] = acc["lam"] + jnp.sum(dlog_a * r, axis=0, keepdims=True)
            acc["ba"] = acc["ba"] + jnp.sum(dpa, axis=0, keepdims=True)
            acc["bx"] = acc["bx"] + jnp.sum(dpx, axis=0, keepdims=True)
            xb, dpab, dpxb = xr.astype(BF16), dpa.astype(BF16), dpx.astype(BF16)
            dwa = dwa + lax.dot_general(xb, dpab, DIMS_TN, preferred_element_type=F32)
            dwx = dwx + lax.dot_general(xb, dpxb, DIMS_TN, preferred_element_type=F32)
            dxr = (dht * mult * i + lax.dot_general(dpab, wa_v, DIMS_NT, preferred_element_type=F32)
                   + lax.dot_general(dpxb, wx_v, DIMS_NT, preferred_element_type=F32))
            acc["bias"] = acc["bias"] + jnp.sum(dxr, axis=0, keepdims=True)
            dx_s[pl.ds(t0, tc), :] = dxr
        dcw = [jnp.zeros((1, LANES), F32) for _ in range(4)]
        for t0, tc in chunks:
            dxr = dx_s[pl.ds(t0, tc), :]
            dxin = (cw[3:4, :] * dxr + cw[2:3, :] * dx_s[pl.ds(t0 + 1, tc), :] + cw[1:2, :] * dx_s[pl.ds(t0 + 2, tc), :]
                    + cw[0:1, :] * dx_s[pl.ds(t0 + 3, tc), :])
            dp_ref[1, pl.ds(t0, tc), :] = dxin.astype(dp_ref.dtype)
            for k in range(4):
                dcw[k] = dcw[k] + jnp.sum(dxr * xh_s[pl.ds(HALO + t0 - (3 - k), tc), :], axis=0, keepdims=True)
        dlam = acc["lam"] * LRU_C * jax.nn.sigmoid(-lam_v)
        sm_ref[...] = _rows_to_tile(dcw + [acc["bias"], acc["ba"], acc["bx"], dlam])
        dwa_ref[...] = dwa
        dwx_ref[...] = dwx

    big = lambda halo: pltpu.VMEM((T + halo, LANES), F32)
    mat = pl.BlockSpec((None, LANES, LANES), lambda c: (c, 0, 0))
    return pl.pallas_call(
        body, name="mixer_b_bwd", grid=(nblk,),
        in_specs=_mixer_b_specs(T, nblk) + [_seg_spec(T, 1, nblk)],
        out_specs=[pl.BlockSpec((2, T, LANES), lambda c: (0, 0, c)),
                   pl.BlockSpec((None, SUBLANES, LANES), lambda c: (c, 0, 0)), mat, mat],
        out_shape=[jax.ShapeDtypeStruct((2, T, C), BF16), jax.ShapeDtypeStruct((nblk, SUBLANES, LANES), F32),
                   jax.ShapeDtypeStruct((nblk, LANES, LANES), F32), jax.ShapeDtypeStruct((nblk, LANES, LANES), F32)],
        scratch_shapes=[big(HALO), big(0), big(0), big(0), big(HALO), big(HALO), big(0), big(0), big(HALO)],
        compiler_params=_params(("parallel",)))(proj, proj, conv_b, bias, wa, ba, wx, bx, lam, dy)


ATT_BLOCK = 128
ATT_GROUP = 4
ATT_TILE = ATT_BLOCK * ATT_GROUP
ATT_UNDERFLOW = -110.0
ATT_UNVISITED = -1e30


def _split_dot(x, m):
    hi = x.astype(BF16)
    lo = (x - hi.astype(F32)).astype(BF16)
    return jnp.dot(hi, m, preferred_element_type=F32) + jnp.dot(lo, m, preferred_element_type=F32)


def _sub(x, j):
    return x[:, j * ATT_BLOCK:(j + 1) * ATT_BLOCK]


def _stack_rows(x):
    return jnp.concatenate([_sub(x, j) for j in range(ATT_GROUP)], axis=0)


def _unstack_rows(x, offsets):
    return jnp.concatenate([x[j * ATT_BLOCK:(j + 1) * ATT_BLOCK, :] + offsets[j] for j in range(ATT_GROUP)], axis=1)


def _att_tile(q, k_ref, q0, qb, it, scale):
    hi = (qb + 1 - ATT_GROUP * it) * ATT_BLOCK
    k0 = pl.multiple_of(jnp.maximum(hi - ATT_TILE, 0), ATT_BLOCK)
    kt = k_ref[pl.ds(k0, ATT_TILE), :]
    z = lax.dot_general(q, kt, DIMS_NT, preferred_element_type=F32) * scale
    key = k0 + lax.broadcasted_iota(jnp.int32, z.shape, 1)
    row = q0 + lax.broadcasted_iota(jnp.int32, z.shape, 0)
    mask = (key < row) & (key < hi)
    n = jnp.where(mask, -(jnp.maximum(z, 0.0) + jnp.log(1.0 + jnp.exp(-jnp.abs(z)))), 0.0)
    return k0, kt, z, mask, n


def _suffix_in_tile(n, upper, run):
    rs = [jnp.sum(_sub(n, j), axis=1, keepdims=True) for j in range(ATT_GROUP)]
    offs = [None] * ATT_GROUP
    offs[ATT_GROUP - 1] = run
    for j in range(ATT_GROUP - 2, -1, -1):
        offs[j] = offs[j + 1] + rs[j + 1]
    return _unstack_rows(_split_dot(_stack_rows(n), upper), offs), offs[0] + rs[0]


def _head_spec(T, seg, heads):
    return pl.BlockSpec((None, T, ATT_HEAD_DIM), lambda h: (seg, 0, h))


def _attention_fwd(qkv):
    _, T, D = qkv.shape
    heads = D // ATT_HEAD_DIM
    nq = T // ATT_BLOCK
    assert nq <= LANES
    scale = 1.0 / math.sqrt(ATT_HEAD_DIM)

    def body(q_ref, k_ref, v_ref, o_ref, r_ref, acc_s, run_s):
        rr = lax.broadcasted_iota(jnp.int32, (ATT_BLOCK, ATT_BLOCK), 0)
        cc = lax.broadcasted_iota(jnp.int32, (ATT_BLOCK, ATT_BLOCK), 1)
        upper = jnp.where(rr > cc, 1.0, 0.0).astype(BF16)
        lane = lax.broadcasted_iota(jnp.int32, (ATT_BLOCK, LANES), 1)

        def q_loop(qb, _):
            q0 = pl.multiple_of(qb * ATT_BLOCK, ATT_BLOCK)
            q = q_ref[pl.ds(q0, ATT_BLOCK), :]
            acc_s[...] = jnp.zeros_like(acc_s)
            run_s[...] = jnp.zeros_like(run_s)
            r_ref[pl.ds(q0, ATT_BLOCK), :] = jnp.full((ATT_BLOCK, LANES), ATT_UNVISITED, F32)
            n_tiles = (qb + ATT_GROUP) // ATT_GROUP

            def tile(carry):
                it, _ = carry
                k0, _, z, mask, n = _att_tile(q, k_ref, q0, qb, it, scale)
                run = run_s[...]
                suffix, run_next = _suffix_in_tile(n, upper, run)
                w = jnp.where(mask, jnp.exp(z + n + suffix), 0.0)
                acc_s[...] += jnp.dot(w.astype(BF16), v_ref[pl.ds(k0, ATT_TILE), :], preferred_element_type=F32)
                r_ref[pl.ds(q0, ATT_BLOCK), :] = jnp.where(lane == it, run, r_ref[pl.ds(q0, ATT_BLOCK), :])
                run_s[...] = run_next
                return it + 1, jnp.max(run_next) >= ATT_UNDERFLOW

            lax.while_loop(lambda c: (c[0] < n_tiles) & c[1], tile, (jnp.int32(0), jnp.bool_(True)))
            o_ref[pl.ds(q0, ATT_BLOCK), :] = acc_s[...].astype(o_ref.dtype)
            return 0

        lax.fori_loop(0, nq, q_loop, 0)

    return pl.pallas_call(
        body, name="attention_fwd", grid=(heads,),
        in_specs=[_head_spec(T, 0, heads), _head_spec(T, 1, heads), _head_spec(T, 2, heads)],
        out_specs=[pl.BlockSpec((T, ATT_HEAD_DIM), lambda h: (0, h)), pl.BlockSpec((None, T, LANES), lambda h: (h, 0, 0))],
        out_shape=[jax.ShapeDtypeStruct((T, D), BF16), jax.ShapeDtypeStruct((heads, T, LANES), F32)],
        scratch_shapes=[pltpu.VMEM((ATT_BLOCK, ATT_HEAD_DIM), F32), pltpu.VMEM((ATT_BLOCK, LANES), F32)],
        compiler_params=_params(("parallel",)))(qkv, qkv, qkv)


def _attention_bwd(qkv, do, rmat):
    _, T, D = qkv.shape
    heads = D // ATT_HEAD_DIM
    nq = T // ATT_BLOCK
    scale = 1.0 / math.sqrt(ATT_HEAD_DIM)

    def body(q_ref, k_ref, v_ref, do_ref, r_ref, dqkv_ref, dk_s, dv_s, dq_s, left_s):
        rr = lax.broadcasted_iota(jnp.int32, (ATT_BLOCK, ATT_BLOCK), 0)
        cc = lax.broadcasted_iota(jnp.int32, (ATT_BLOCK, ATT_BLOCK), 1)
        upper = jnp.where(rr > cc, 1.0, 0.0).astype(BF16)
        lower = jnp.where(rr < cc, 1.0, 0.0).astype(BF16)
        lane = lax.broadcasted_iota(jnp.int32, (ATT_BLOCK, LANES), 1)
        dk_s[...] = jnp.zeros_like(dk_s)
        dv_s[...] = jnp.zeros_like(dv_s)

        def q_loop(qb, _):
            q0 = pl.multiple_of(qb * ATT_BLOCK, ATT_BLOCK)
            q = q_ref[pl.ds(q0, ATT_BLOCK), :]
            dov = do_ref[pl.ds(q0, ATT_BLOCK), :]
            dq_s[...] = jnp.zeros_like(dq_s)
            left_s[...] = jnp.zeros_like(left_s)
            rm = r_ref[pl.ds(q0, ATT_BLOCK), :]
            n_tiles = (qb + ATT_GROUP) // ATT_GROUP
            seen = (jnp.max(rm, axis=0, keepdims=True) > 0.5 * ATT_UNVISITED) & (lane[0:1, :] < n_tiles)
            n_visited = jnp.sum(jnp.where(seen, 1.0, 0.0)).astype(jnp.int32)

            def tile(j, _):
                it = n_visited - 1 - j
                k0, kt, z, mask, n = _att_tile(q, k_ref, q0, qb, it, scale)
                vt = v_ref[pl.ds(k0, ATT_TILE), :]
                run = jnp.sum(jnp.where(lane == it, r_ref[pl.ds(q0, ATT_BLOCK), :], 0.0), axis=1, keepdims=True)
                suffix, _ = _suffix_in_tile(n, upper, run)
                s = z + n
                w = jnp.where(mask, jnp.exp(s + suffix), 0.0)
                e = w * lax.dot_general(dov, vt, DIMS_NT, preferred_element_type=F32)
                es = [jnp.sum(_sub(e, g), axis=1, keepdims=True) for g in range(ATT_GROUP)]
                pre = [left_s[...]]
                for g in range(ATT_GROUP):
                    pre.append(pre[g] + es[g])
                before = _unstack_rows(_split_dot(_stack_rows(e), lower), pre)
                sig = jnp.exp(s)
                dz = (jnp.where(mask, e * (1.0 - sig) - before * sig, 0.0) * scale).astype(BF16)
                dq_s[...] += jnp.dot(dz, kt, preferred_element_type=F32)
                dk_s[pl.ds(k0, ATT_TILE), :] += lax.dot_general(dz, q, DIMS_TN, preferred_element_type=F32)
                dv_s[pl.ds(k0, ATT_TILE), :] += lax.dot_general(w.astype(BF16), dov, DIMS_TN, preferred_element_type=F32)
                left_s[...] = pre[ATT_GROUP]
                return 0

            lax.fori_loop(0, n_visited, tile, 0)
            dqkv_ref[0, pl.ds(q0, ATT_BLOCK), :] = dq_s[...].astype(dqkv_ref.dtype)
            return 0

        lax.fori_loop(0, nq, q_loop, 0)
        dqkv_ref[1, :, :] = dk_s[...].astype(dqkv_ref.dtype)
        dqkv_ref[2, :, :] = dv_s[...].astype(dqkv_ref.dtype)

    return pl.pallas_call(
        body, name="attention_bwd", grid=(heads,),
        in_specs=[_head_spec(T, 0, heads), _head_spec(T, 1, heads), _head_spec(T, 2, heads),
                  pl.BlockSpec((T, ATT_HEAD_DIM), lambda h: (0, h)), pl.BlockSpec((None, T, LANES), lambda h: (h, 0, 0))],
        out_specs=pl.BlockSpec((3, T, ATT_HEAD_DIM), lambda h: (0, 0, h)),
        out_shape=jax.ShapeDtypeStruct((3, T, D), BF16),
        scratch_shapes=[pltpu.VMEM((T, ATT_HEAD_DIM), F32), pltpu.VMEM((T, ATT_HEAD_DIM), F32),
                        pltpu.VMEM((ATT_BLOCK, ATT_HEAD_DIM), F32), pltpu.VMEM((ATT_BLOCK, LANES), F32)],
        compiler_params=_params(("parallel",)))(qkv, qkv, qkv, do, rmat)


def _block_diag_pairs(w):
    h = w.shape[0]
    wp = w.reshape(h // 2, 2, RG_HEAD_DIM, RG_HEAD_DIM)
    z = jnp.zeros_like(wp[:, 0])
    top = jnp.concatenate([wp[:, 0], z], axis=2)
    bot = jnp.concatenate([z, wp[:, 1]], axis=2)
    return jnp.concatenate([top, bot], axis=1)


def _diag_pairs(g):
    n = g.shape[0]
    a = g[:, :RG_HEAD_DIM, :RG_HEAD_DIM]
    b = g[:, RG_HEAD_DIM:, RG_HEAD_DIM:]
    return jnp.stack([a, b], axis=1).reshape(2 * n, RG_HEAD_DIM, RG_HEAD_DIM)


class _Weights:
    def __init__(self, full, shards=None, plan=None):
        self.full, self.shards, self.plan = dict(full), shards or {}, plan or {}

    def __getitem__(self, name):
        return self.full[name]

    def jobs(self, call):
        return [_gather_job(self.shards[n]) for n in self.plan.get(call, ())]

    def deliver(self, call, outs):
        for n, g in zip(self.plan.get(call, ()), outs):
            self.full[n] = _gathered_layout(n, g)


def _gathered_layout(name, g):
    if name in ("w_in", "w_qkv"):
        return _from_col_blocks(g)
    if name in ("w_out", "w_o"):
        return g.reshape(g.shape[0] * g.shape[1], g.shape[2])
    return g


class _Grads:
    def __init__(self, lands=None, plan=None):
        self.lands, self.plan = dict(lands) if lands else None, plan or {}
        self.ready, self.sent = {}, []

    def put(self, name, arr):
        self.ready[name] = arr

    def jobs(self, call):
        if self.lands is None:
            return []
        return [_exchange_job(self.ready[n], self.lands[n]) for n in self.plan.get(call, ())]

    def deliver(self, call, outs):
        for n, o in zip(self.plan.get(call, ()), outs):
            self.lands[n] = o
            self.sent.append(n)

    def flush(self, name):
        rest = [n for n in self.ready if n not in self.sent]
        if self.lands is None or not rest:
            return
        outs = _run_jobs(name, [_exchange_job(self.ready[n], self.lands[n]) for n in rest])
        for n, o in zip(rest, outs):
            self.lands[n] = o
            self.sent.append(n)


def _mlp_fwd(tag, h, wts, run):
    T, D = h.shape
    w_up = wts["up" + tag]
    fb = w_up.shape[2]
    F = fb * N_DEV
    tm, tn, tk = _tile(T, 1024), _tile(fb, 1024), _tile(D, 512)
    nb = fb // tn

    def up_epilogue(u):
        r = jnp.maximum(u, 0.0)
        return u, r * r

    o_spec = pl.BlockSpec((tm, tn), lambda i, j, k: (i, j))
    u, act = run(
        _matmul, f"mlp_up_l{tag}",
        [(h, pl.BlockSpec((tm, tk), lambda i, j, k: (i, k))),
         (w_up, pl.BlockSpec((None, tk, tn), lambda i, j, k: (j // nb, k, j % nb)))],
        [(jax.ShapeDtypeStruct((T, F), BF16), o_spec), (jax.ShapeDtypeStruct((T, F), BF16), o_spec)],
        (T // tm, F // tn, D // tk), DIMS_NN, (tm, tn), up_epilogue, n_main=2)
    w_down = wts["down" + tag]
    tn2, tk2 = _tile(D, 1024), _tile(fb, 512)
    nkb = fb // tk2
    m = run(
        _matmul, f"mlp_down_l{tag}",
        [(act, pl.BlockSpec((tm, tk2), lambda i, j, k: (i, k))),
         (w_down, pl.BlockSpec((None, tk2, tn2), lambda i, j, k: (k // nkb, k % nkb, j)))],
        [(jax.ShapeDtypeStruct((T, D), F32), pl.BlockSpec((tm, tn2), lambda i, j, k: (i, j)))],
        (T // tm, D // tn2, F // tk2), DIMS_NN, (tm, tn2), None)
    return u, act, m


def _mlp_bwd(tag, h, u, act, dm, wts, grads, run):
    T, D = h.shape
    w_up, w_down = wts["up" + tag], wts["down" + tag]
    fb = w_up.shape[2]
    F = fb * N_DEV
    grads.put("down" + tag, run(_mm_tn, f"mlp_down_dw_l{tag}", act, dm, BF16).reshape(N_DEV, fb, D))
    tm, tn, tk = _tile(T, 1024), _tile(fb, 1024), _tile(D, 512)
    nb = fb // tn
    o_spec = pl.BlockSpec((tm, tn), lambda i, j, k: (i, j))
    du = run(
        _matmul, f"mlp_down_dx_l{tag}",
        [(dm, pl.BlockSpec((tm, tk), lambda i, j, k: (i, k))),
         (w_down, pl.BlockSpec((None, tn, tk), lambda i, j, k: (j // nb, j % nb, k))),
         (u, o_spec)],
        [(jax.ShapeDtypeStruct((T, F), BF16), o_spec)],
        (T // tm, F // tn, D // tk), DIMS_NT, (tm, tn),
        lambda r, uv: (r * (2.0 * jnp.maximum(uv.astype(F32), 0.0)),))
    grads.put("up" + tag, run(_mm_tn, f"mlp_up_dw_l{tag}", h, du, BF16, out_blocks=N_DEV))
    tn2, tk2 = _tile(D, 1024), _tile(fb, 512)
    nkb = fb // tk2
    return run(
        _matmul, f"mlp_up_dx_l{tag}",
        [(du, pl.BlockSpec((tm, tk2), lambda i, j, k: (i, k))),
         (w_up, pl.BlockSpec((None, tn2, tk2), lambda i, j, k: (k // nkb, j, k % nkb)))],
        [(jax.ShapeDtypeStruct((T, D), F32), pl.BlockSpec((tm, tn2), lambda i, j, k: (i, j)))],
        (T // tm, D // tn2, F // tk2), DIMS_NT, (tm, tn2), None)


def _local_step(x, target, gains, conv_a, conv_b, conv_b_bias, rg_w_a, rg_b_a, rg_w_x, rg_b_x, rg_lambda, wts, grads):
    T, D = x.shape
    g = lambda l, i: gains[l, i][None, :]
    wa_p = _block_diag_pairs(rg_w_a).astype(BF16)
    wx_p = _block_diag_pairs(rg_w_x).astype(BF16)

    def run(fn, name, *args, n_main=1, **kw):
        jw, jg = wts.jobs(name), grads.jobs(name)
        res = fn(name, *args, jobs=jw + jg, **kw)
        main, jo = res[:n_main], res[n_main:]
        wts.deliver(name, jo[:len(jw)])
        grads.deliver(name, jo[len(jw):])
        return main[0] if n_main == 1 else main

    h0 = _norm_fwd("norm_in", x, g(0, 0))
    proj = run(_mm_nn, "w_in_fwd", h0, wts["w_in"], F32, out_seg=5)
    y_a = _mixer_a_fwd(proj, conv_a)
    y_b = _mixer_b_fwd(proj, conv_b, conv_b_bias, wa_p, rg_b_a, wx_p, rg_b_x, rg_lambda)
    y = jnp.stack([y_a, y_b], axis=0)
    mix0 = run(_mm_nn, "w_out_fwd", y, wts["w_out"], F32, a_seg=2)
    x1, h1 = _resid_norm("resid_mix0", x, mix0, g(0, 1), g(0, 2))
    u0, act0, m0 = _mlp_fwd("0", h1, wts, run)
    x2, h2 = _resid_norm("resid_mlp0", x1, m0, g(0, 3), g(1, 0))
    qkv = run(_mm_nn, "w_qkv_fwd", h2, wts["w_qkv"], BF16, out_seg=3)
    o, rmat = _attention_fwd(qkv)
    mix1 = run(_mm_nn, "w_o_fwd", o, wts["w_o"], F32)
    x3, h3 = _resid_norm("resid_mix1", x2, mix1, g(1, 1), g(1, 2))
    u1, act1, m1 = _mlp_fwd("1", h3, wts, run)
    dx4, sq = _final_loss("loss", x3, m1, g(1, 3), target)

    dm1, dg13 = _norm_bwd("norm_bwd_m1", m1, g(1, 3), dx4, None, BF16)
    dh3 = _mlp_bwd("1", h3, u1, act1, dm1, wts, grads, run)
    dx3, dg12 = _norm_bwd("norm_bwd_x3", x3, g(1, 2), dh3, dx4, F32)
    dmix1, dg11 = _norm_bwd("norm_bwd_mix1", mix1, g(1, 1), dx3, None, BF16)
    grads.put("w_o", run(_mm_tn, "w_o_dw", o, dmix1, BF16).reshape(N_DEV, D // N_DEV, D))
    do = run(_mm_nt, "w_o_dx", dmix1, wts["w_o"], BF16)
    dqkv = _attention_bwd(qkv, do, rmat)
    grads.put("w_qkv", _col_blocks(run(_mm_tn, "w_qkv_dw", h2, dqkv, BF16, b_seg=3), N_DEV))
    dh2 = run(_mm_nt, "w_qkv_dx", dqkv, wts["w_qkv"], F32, a_seg=3)
    dx2, dg10 = _norm_bwd("norm_bwd_x2", x2, g(1, 0), dh2, dx3, F32)
    dm0, dg03 = _norm_bwd("norm_bwd_m0", m0, g(0, 3), dx2, None, BF16)
    dh1 = _mlp_bwd("0", h1, u0, act0, dm0, wts, grads, run)
    dx1, dg02 = _norm_bwd("norm_bwd_x1", x1, g(0, 2), dh1, dx2, F32)
    dmix0, dg01 = _norm_bwd("norm_bwd_mix0", mix0, g(0, 1), dx1, None, BF16)
    grads.put("w_out", run(_mm_tn, "w_out_dw", y, dmix0, BF16, a_seg=2).reshape(N_DEV, D // N_DEV, D))
    dy = run(_mm_nt, "w_out_dx", dmix0, wts["w_out"], F32, out_seg=2)
    dproj_a, dconv_a = _mixer_a_bwd(proj, conv_a, dy)
    dproj_b, sm_b, dwa_p, dwx_p = _mixer_b_bwd(proj, conv_b, conv_b_bias, wa_p, rg_b_a, wx_p, rg_b_x, rg_lambda, dy)
    dproj = jnp.concatenate([dproj_a, dproj_b], axis=0)
    grads.put("w_in", _col_blocks(run(_mm_tn, "w_in_dw", h0, dproj, BF16, b_seg=5), N_DEV))
    dh0 = run(_mm_nt, "w_in_dx", dproj, wts["w_in"], F32, a_seg=5)
    dx0, dg00 = _norm_bwd("norm_bwd_x0", x, g(0, 0), dh0, dx1, F32)

    C = D // 2
    lanes_to_vec = lambda t, row: t[:, row, :].reshape(1, C)
    small = {
        "norm_gains": jnp.concatenate([dg00, dg01, dg02, dg03, dg10, dg11, dg12, dg13], axis=0).reshape(2, 4, D),
        "conv_a": jnp.transpose(dconv_a[:, :3, :], (1, 0, 2)).reshape(3, C),
        "conv_b": jnp.transpose(sm_b[:, :4, :], (1, 0, 2)).reshape(4, C),
        "conv_b_bias": lanes_to_vec(sm_b, _ROW_BIAS),
        "rg_w_a": _diag_pairs(dwa_p),
        "rg_b_a": lanes_to_vec(sm_b, _ROW_BA),
        "rg_w_x": _diag_pairs(dwx_p),
        "rg_b_x": lanes_to_vec(sm_b, _ROW_BX),
        "rg_lambda": lanes_to_vec(sm_b, _ROW_LAM),
    }
    return sq[0, 0], dx0, small


def _my_index():
    return 4 * lax.axis_index("x") + 2 * lax.axis_index("y") + lax.axis_index("c")


def _peers():
    x, y, c = lax.axis_index("x"), lax.axis_index("y"), lax.axis_index("c")
    out = []
    for k in range(1, N_DEV):
        px = x ^ ((k >> 2) & 1)
        py = y ^ ((k >> 1) & 1)
        pc = c ^ (k & 1)
        out.append(((px, py, pc), 4 * px + 2 * py + pc))
    return out


GATHER_FIRST = ("w_in", "w_out")
GATHER_PLAN = {"w_in_fwd": ("up0",), "mlp_up_l0": ("down0",), "mlp_down_l0": ("w_qkv", "w_o"),
               "w_qkv_fwd": ("up1",), "mlp_up_l1": ("down1",)}
EXCHANGE_PLAN = {}


def _all_gather(name, shards):
    n = len(shards)

    def body(*refs):
        srcs, dsts = refs[:n], refs[n:2 * n]
        send_sems, recv_sems, local_sems = refs[2 * n:]
        me = _my_index()
        peers = _peers()
        copies = []
        for a in range(n):
            lc = pltpu.make_async_copy(srcs[a], dsts[a].at[me], local_sems.at[a])
            lc.start()
            copies.append(lc)
        remote = []
        for a in range(n):
            for k, (pos, _) in enumerate(peers):
                cp = pltpu.make_async_remote_copy(
                    src_ref=srcs[a], dst_ref=dsts[a].at[me], send_sem=send_sems.at[a, k], recv_sem=recv_sems.at[a, k],
                    device_id=pos, device_id_type=MESH)
                cp.start()
                remote.append(cp)
        for a in range(n):
            for k, (pos, idx) in enumerate(peers):
                pltpu.make_async_remote_copy(
                    src_ref=srcs[a], dst_ref=dsts[a].at[idx], send_sem=send_sems.at[a, k], recv_sem=recv_sems.at[a, k],
                    device_id=pos, device_id_type=MESH).wait_recv()
        for cp in remote:
            cp.wait_send()
        for lc in copies:
            lc.wait()

    return pl.pallas_call(
        body, name=name,
        in_specs=[_ANY] * n, out_specs=[_ANY] * n,
        out_shape=[jax.ShapeDtypeStruct((N_DEV,) + s.shape, s.dtype) for s in shards],
        scratch_shapes=[pltpu.SemaphoreType.DMA((n, N_DEV - 1)), pltpu.SemaphoreType.DMA((n, N_DEV - 1)),
                        pltpu.SemaphoreType.DMA((n,))],
    )(*shards)


def _job_sems():
    return [pltpu.SemaphoreType.DMA((N_DEV - 1,)), pltpu.SemaphoreType.DMA((N_DEV - 1,)), pltpu.SemaphoreType.DMA((1,))]


def _gather_job(shard):
    def ctx():
        x, y, c = lax.axis_index("x"), lax.axis_index("y"), lax.axis_index("c")
        chips = [(1 - x, y), (x, 1 - y), (1 - x, 1 - y)]
        return x, y, c, chips

    def idx(px, py, pc):
        return 4 * px + 2 * py + pc

    def copy(src, out, sems, k, block, to):
        return pltpu.make_async_remote_copy(
            src_ref=out.at[block] if src is None else src, dst_ref=out.at[block], send_sem=sems[0].at[k],
            recv_sem=sems[1].at[k], device_id=to, device_id_type=MESH)

    def start(ins, outs, sems):
        x, y, c, chips = ctx()
        src, out = ins[0], outs[0]
        me = idx(x, y, c)
        pltpu.make_async_copy(src, out.at[me], sems[2].at[0]).start()
        copy(src, out, sems, 0, me, (x, y, 1 - c)).start()
        for j, (px, py) in enumerate(chips):
            copy(src, out, sems, 1 + j, me, (px, py, c)).start()

    def mid(ins, outs, sems):
        x, y, c, chips = ctx()
        out = outs[0]
        for j, (px, py) in enumerate(chips):
            copy(None, out, sems, 1 + j, idx(px, py, c), (x, y, c)).wait_recv()
            copy(None, out, sems, 4 + j, idx(px, py, c), (x, y, 1 - c)).start()

    def end(ins, outs, sems):
        x, y, c, chips = ctx()
        src, out = ins[0], outs[0]
        me = (x, y, c)
        copy(None, out, sems, 0, idx(x, y, 1 - c), me).wait_recv()
        for j, (px, py) in enumerate(chips):
            copy(None, out, sems, 4 + j, idx(px, py, 1 - c), me).wait_recv()
        for k in range(N_DEV - 1):
            copy(src, out, sems, k, idx(x, y, c), me).wait_send()
        pltpu.make_async_copy(src, out.at[idx(x, y, c)], sems[2].at[0]).wait()

    return _Job([shard], [jax.ShapeDtypeStruct((N_DEV,) + shard.shape, shard.dtype)], _job_sems(), start, mid, end)


def _exchange_job(src, land, lead=()):
    def slot(land_ref, s):
        r = land_ref.at[s]
        for i in lead:
            r = r.at[i]
        return r

    def start(ins, outs, sems):
        me = _my_index()
        pltpu.make_async_copy(ins[0].at[me], slot(outs[0], me), sems[2].at[0]).start()
        for k, (pos, idx) in enumerate(_peers()):
            pltpu.make_async_remote_copy(
                src_ref=ins[0].at[idx], dst_ref=slot(outs[0], me), send_sem=sems[0].at[k], recv_sem=sems[1].at[k],
                device_id=pos, device_id_type=MESH).start()

    def mid(ins, outs, sems):
        pass

    def end(ins, outs, sems):
        me = _my_index()
        for k, (pos, idx) in enumerate(_peers()):
            cp = pltpu.make_async_remote_copy(
                src_ref=ins[0].at[idx], dst_ref=slot(outs[0], idx), send_sem=sems[0].at[k], recv_sem=sems[1].at[k],
                device_id=pos, device_id_type=MESH)
            cp.wait_recv()
            cp.wait_send()
        pltpu.make_async_copy(ins[0].at[me], slot(outs[0], me), sems[2].at[0]).wait()

    return _Job([src, land], [jax.ShapeDtypeStruct(land.shape, land.dtype)], _job_sems(), start, mid, end, alias={1: 0})


def _adamw_math(w, g, m, v):
    m = ADAM_B1 * m + (1.0 - ADAM_B1) * g
    v = ADAM_B2 * v + (1.0 - ADAM_B2) * (g * g)
    m_hat = m / (1.0 - ADAM_B1 ** ADAM_STEP)
    v_hat = v / (1.0 - ADAM_B2 ** ADAM_STEP)
    delta = -ADAM_LR * (m_hat / (jnp.sqrt(v_hat) + ADAM_EPS) + ADAM_WD * w)
    return delta, m, v


def _sum_slots(ref):
    g = ref[0].astype(F32)
    for s in range(1, N_DEV):
        g = g + ref[s].astype(F32)
    return g


def _adamw_big(name, lands, w, m, v):
    L, R, C = w.shape
    assert len(lands) == L
    tr = _tile(R, max(LANES, (256 * 1024) // C))

    def body(*refs):
        l_refs = refs[:L]
        w_ref, m_ref, v_ref, g_ref, d_ref, nm_ref, nv_ref = refs[L:]
        for li in range(L):
            @pl.when(pl.program_id(0) == li)
            def _(li=li):
                g = _sum_slots(l_refs[li])
                d, nm, nv = _adamw_math(w_ref[...], g, m_ref[...], v_ref[...])
                g_ref[...] = g
                d_ref[...] = d
                nm_ref[...] = nm
                nv_ref[...] = nv

    def land_spec(li):
        return pl.BlockSpec((N_DEV, tr, C), lambda l, i: (0, jnp.where(l == li, i, 0), 0))

    row = pl.BlockSpec((None, tr, C), lambda l, i: (l, i, 0))
    return pl.pallas_call(
        body, name=name, grid=(L, R // tr),
        in_specs=[land_spec(li) for li in range(L)] + [row, row, row],
        out_specs=[row] * 4, out_shape=[jax.ShapeDtypeStruct((L, R, C), F32)] * 4,
        compiler_params=_params(("arbitrary", "arbitrary")))(*lands, w, m, v)


def _sum8(name, slots):
    _, R, C = slots.shape

    def body(s_ref, o_ref):
        o_ref[...] = _sum_slots(s_ref)

    return pl.pallas_call(body, name=name, out_shape=jax.ShapeDtypeStruct((R, C), F32))(slots)


def _adamw_small(name, g, w, m, v):
    def body(g_ref, w_ref, m_ref, v_ref, d_ref, nm_ref, nv_ref):
        d, nm, nv = _adamw_math(w_ref[...], g_ref[...], m_ref[...], v_ref[...])
        d_ref[...] = d
        nm_ref[...] = nm
        nv_ref[...] = nv

    return pl.pallas_call(body, name=name, out_shape=[jax.ShapeDtypeStruct(w.shape, F32)] * 3)(g, w, m, v)


def _pack_rows(arrs):
    parts, spans, r0 = [], [], 0
    for a in arrs:
        flat = a.astype(F32).reshape(-1)
        rows = -(-flat.shape[0] // LANES)
        rows = -(-rows // SUBLANES) * SUBLANES
        flat = jnp.pad(flat, (0, rows * LANES - flat.shape[0]))
        parts.append(flat.reshape(rows, LANES))
        spans.append((r0, rows, a.shape))
        r0 += rows
    return jnp.concatenate(parts, axis=0), spans


def _unpack_rows(buf, span):
    r0, rows, shape = span
    n = math.prod(shape)
    return buf[..., r0:r0 + rows, :].reshape(buf.shape[:-2] + (rows * LANES,))[..., :n].reshape(buf.shape[:-2] + shape)


def _col_blocks(w, n_blocks):
    K, N = w.shape
    return jnp.transpose(w.reshape(K, n_blocks, N // n_blocks), (1, 0, 2))


def _from_col_blocks(wb):
    B, K, n = wb.shape
    return jnp.transpose(wb, (1, 0, 2)).reshape(K, B * n)


def kernel(x, norm_gains, hyb_w_in, hyb_conv_a, hyb_conv_b, hyb_conv_b_bias, hyb_rg_w_a, hyb_rg_b_a, hyb_rg_w_x, hyb_rg_b_x, hyb_rg_lambda, hyb_w_out, sb_w_qkv, sb_w_o, mlp_w_up, mlp_w_down, loss_target, m_norm_gains, m_hyb_w_in, m_hyb_conv_a, m_hyb_conv_b, m_hyb_conv_b_bias, m_hyb_rg_w_a, m_hyb_rg_b_a, m_hyb_rg_w_x, m_hyb_rg_b_x, m_hyb_rg_lambda, m_hyb_w_out, m_sb_w_qkv, m_sb_w_o, m_mlp_w_up, m_mlp_w_down, v_norm_gains, v_hyb_w_in, v_hyb_conv_a, v_hyb_conv_b, v_hyb_conv_b_bias, v_hyb_rg_w_a, v_hyb_rg_b_a, v_hyb_rg_w_x, v_hyb_rg_b_x, v_hyb_rg_lambda, v_hyb_w_out, v_sb_w_qkv, v_sb_w_o, v_mlp_w_up, v_mlp_w_down):
    T, D = x.shape[1], x.shape[2]
    me = _my_index()

    small_shards, small_spans = _pack_rows([norm_gains, hyb_conv_a[0], hyb_conv_b[0]])
    (small_all,) = _all_gather("gather_small", [small_shards])
    gains_b = _unpack_rows(small_all, small_spans[0])
    gains = jnp.transpose(gains_b, (1, 2, 0, 3)).reshape(2, 4, D)
    conv_a = _from_col_blocks(_unpack_rows(small_all, small_spans[1]))
    conv_b = _from_col_blocks(_unpack_rows(small_all, small_spans[2]))

    shards = {"w_in": hyb_w_in[0], "w_out": hyb_w_out[0], "w_qkv": sb_w_qkv[0], "w_o": sb_w_o[0],
              "up0": mlp_w_up[0], "up1": mlp_w_up[1], "down0": mlp_w_down[0], "down1": mlp_w_down[1]}
    shards = {n: s.astype(BF16) for n, s in shards.items()}
    first = _run_jobs("gather_first", [_gather_job(shards[n]) for n in GATHER_FIRST])
    wts = _Weights({n: _gathered_layout(n, g) for n, g in zip(GATHER_FIRST, first)}, shards, GATHER_PLAN)
    grads_big = _Grads({n: lax.empty((N_DEV,) + s.shape, BF16) for n, s in shards.items()}, EXCHANGE_PLAN)

    sq, grad_x, small = _local_step(
        x[0], loss_target[0], gains, conv_a, conv_b, hyb_conv_b_bias, hyb_rg_w_a[0], hyb_rg_b_a, hyb_rg_w_x[0],
        hyb_rg_b_x, hyb_rg_lambda, wts, grads_big)
    loss = lax.psum(0.5 * sq / D, ("x", "y", "c"))

    grads_big.flush("exchange_grads")
    land = grads_big.lands

    names = ["norm_gains", "hyb_w_in", "hyb_conv_a", "hyb_conv_b", "hyb_conv_b_bias", "hyb_rg_w_a", "hyb_rg_b_a",
             "hyb_rg_w_x", "hyb_rg_b_x", "hyb_rg_lambda", "hyb_w_out", "sb_w_qkv", "sb_w_o", "mlp_w_up", "mlp_w_down"]
    params = dict(zip(names, [norm_gains, hyb_w_in, hyb_conv_a, hyb_conv_b, hyb_conv_b_bias, hyb_rg_w_a, hyb_rg_b_a,
                              hyb_rg_w_x, hyb_rg_b_x, hyb_rg_lambda, hyb_w_out, sb_w_qkv, sb_w_o, mlp_w_up, mlp_w_down]))
    moms = dict(zip(names, [m_norm_gains, m_hyb_w_in, m_hyb_conv_a, m_hyb_conv_b, m_hyb_conv_b_bias, m_hyb_rg_w_a,
                            m_hyb_rg_b_a, m_hyb_rg_w_x, m_hyb_rg_b_x, m_hyb_rg_lambda, m_hyb_w_out, m_sb_w_qkv,
                            m_sb_w_o, m_mlp_w_up, m_mlp_w_down]))
    vars_ = dict(zip(names, [v_norm_gains, v_hyb_w_in, v_hyb_conv_a, v_hyb_conv_b, v_hyb_conv_b_bias, v_hyb_rg_w_a,
                             v_hyb_rg_b_a, v_hyb_rg_w_x, v_hyb_rg_b_x, v_hyb_rg_lambda, v_hyb_w_out, v_sb_w_qkv,
                             v_sb_w_o, v_mlp_w_up, v_mlp_w_down]))
    grads, deltas, new_m, new_v = {}, {}, {}, {}

    big_lands = {"hyb_w_in": ["w_in"], "hyb_w_out": ["w_out"], "sb_w_qkv": ["w_qkv"], "sb_w_o": ["w_o"],
                 "mlp_w_up": ["up0", "up1"], "mlp_w_down": ["down0", "down1"]}
    for nm, keys in big_lands.items():
        outs = _adamw_big(f"adamw_{nm}", [land[k] for k in keys], params[nm], moms[nm], vars_[nm])
        grads[nm], deltas[nm], new_m[nm], new_v[nm] = outs

    small_names = ["norm_gains", "hyb_conv_a", "hyb_conv_b", "hyb_conv_b_bias", "hyb_rg_w_a", "hyb_rg_b_a",
                   "hyb_rg_w_x", "hyb_rg_b_x", "hyb_rg_lambda"]
    small_keys = ["norm_gains", "conv_a", "conv_b", "conv_b_bias", "rg_w_a", "rg_b_a", "rg_w_x", "rg_b_x", "rg_lambda"]
    sg_buf, sg_spans = _pack_rows([small[k] for k in small_keys])
    (sg_all,) = _all_gather("gather_small_grads", [sg_buf])
    sg_sum = _sum8("sum_small_grads", sg_all)
    full = {nm: _unpack_rows(sg_sum, sp) for nm, sp in zip(small_names, sg_spans)}
    cb = (D // 2) // N_DEV
    small_grads = {
        "norm_gains": lax.dynamic_slice_in_dim(full["norm_gains"], me * (D // N_DEV), D // N_DEV, axis=2),
        "hyb_conv_a": lax.dynamic_slice_in_dim(full["hyb_conv_a"], me * cb, cb, axis=1)[None],
        "hyb_conv_b": lax.dynamic_slice_in_dim(full["hyb_conv_b"], me * cb, cb, axis=1)[None],
        "hyb_conv_b_bias": full["hyb_conv_b_bias"],
        "hyb_rg_w_a": full["hyb_rg_w_a"][None],
        "hyb_rg_b_a": full["hyb_rg_b_a"],
        "hyb_rg_w_x": full["hyb_rg_w_x"][None],
        "hyb_rg_b_x": full["hyb_rg_b_x"],
        "hyb_rg_lambda": full["hyb_rg_lambda"],
    }
    pk = lambda d: _pack_rows([d[nm] for nm in small_names])
    g_buf, spans = pk(small_grads)
    w_buf, _ = pk(params)
    m_buf, _ = pk(moms)
    v_buf, _ = pk(vars_)
    d_buf, nm_buf, nv_buf = _adamw_small("adamw_small", g_buf, w_buf, m_buf, v_buf)
    for nm, sp in zip(small_names, spans):
        grads[nm] = small_grads[nm]
        deltas[nm], new_m[nm], new_v[nm] = _unpack_rows(d_buf, sp), _unpack_rows(nm_buf, sp), _unpack_rows(nv_buf, sp)

    return (loss, grad_x[None], *[grads[n] for n in names], *[deltas[n] for n in names],
            *[new_m[n] for n in names], *[new_v[n] for n in names])
```

```python
import functools
import math

import jax
import jax.numpy as jnp
from jax import lax
from jax.experimental import pallas as pl
from jax.experimental.pallas import tpu as pltpu

F32 = jnp.float32
BF16 = jnp.bfloat16

NORM_EPS = 1e-6
LRU_C = 8.0
ATT_HEAD_DIM = 128
RG_HEAD_DIM = 64
LANES = 128
SUBLANES = 8
N_DEV = 8
ADAM_LR = 0.001
ADAM_B1 = 0.9
ADAM_B2 = 0.999
ADAM_EPS = 1e-08
ADAM_WD = 0.01
ADAM_STEP = 10
VMEM_LIMIT = 56 * 1024 * 1024
MESH = pl.DeviceIdType.MESH


def _tile(n, pref):
    if n <= pref:
        return n
    t = (pref // LANES) * LANES
    while t > LANES and n % t:
        t -= LANES
    assert n % t == 0, (n, pref)
    return t


def _params(sem):
    return pltpu.CompilerParams(dimension_semantics=sem, vmem_limit_bytes=VMEM_LIMIT)


DIMS_NN = (((1,), (0,)), ((), ()))
DIMS_NT = (((1,), (1,)), ((), ()))
DIMS_TN = (((0,), (0,)), ((), ()))


_ANY = pl.BlockSpec(memory_space=pl.ANY)


class _Job:
    def __init__(self, ins, outs, sems, start, mid, end, alias=None):
        self.ins, self.outs, self.sems = ins, outs, sems
        self.start, self.mid, self.end = start, mid, end
        self.alias = alias or {}


def _job_plumbing(jobs, n_in, n_out):
    j_ins = [a for jb in jobs for a in jb.ins]
    j_outs = [o for jb in jobs for o in jb.outs]
    j_sems = [s for jb in jobs for s in jb.sems]
    aliases, pi, po = {}, 0, 0
    for jb in jobs:
        for i_in, i_out in jb.alias.items():
            aliases[n_in + pi + i_in] = n_out + po + i_out
        pi += len(jb.ins)
        po += len(jb.outs)
    return j_ins, j_outs, j_sems, aliases


def _job_phase(jobs, which, jin, jout, jsem):
    pi = po = ps = 0
    for jb in jobs:
        getattr(jb, which)(jin[pi:pi + len(jb.ins)], jout[po:po + len(jb.outs)], jsem[ps:ps + len(jb.sems)])
        pi, po, ps = pi + len(jb.ins), po + len(jb.outs), ps + len(jb.sems)


def _run_jobs(name, jobs):
    j_ins, j_outs, j_sems, aliases = _job_plumbing(jobs, 0, 0)
    n_ji, n_jo = len(j_ins), len(j_outs)

    def body(*refs):
        jin, jout, jsem = refs[:n_ji], refs[n_ji:n_ji + n_jo], refs[n_ji + n_jo:]
        for which in ("start", "mid", "end"):
            _job_phase(jobs, which, jin, jout, jsem)

    return pl.pallas_call(body, name=name, in_specs=[_ANY] * n_ji, out_specs=[_ANY] * n_jo, out_shape=j_outs,
                          scratch_shapes=j_sems, input_output_aliases=aliases)(*j_ins)


def _matmul(name, ins, outs, grid, dims, acc_shape, epilogue=None, jobs=()):
    n_in, n_out, nk = len(ins), len(outs), grid[2]
    j_ins, j_outs, j_sems, aliases = _job_plumbing(jobs, n_in, n_out)
    n_ji, n_jo = len(j_ins), len(j_outs)
    total = grid[0] * grid[1] * grid[2]

    def body(*refs):
        a_ref, b_ref = refs[0], refs[1]
        extras = refs[2:n_in]
        jin = refs[n_in:n_in + n_ji]
        out_refs = refs[n_in + n_ji:n_in + n_ji + n_out]
        jout = refs[n_in + n_ji + n_out:n_in + n_ji + n_out + n_jo]
        acc = refs[n_in + n_ji + n_out + n_jo]
        jsem = refs[n_in + n_ji + n_out + n_jo + 1:]
        k = pl.program_id(2)
        step = (pl.program_id(0) * grid[1] + pl.program_id(1)) * grid[2] + k
        if jobs:
            pl.when(step == 0)(lambda: _job_phase(jobs, "start", jin, jout, jsem))
            pl.when(step == (4 * total) // 5)(lambda: _job_phase(jobs, "mid", jin, jout, jsem))

        @pl.when(k == 0)
        def _():
            acc[...] = jnp.zeros_like(acc)

        acc[...] += lax.dot_general(a_ref[...], b_ref[...], dims, preferred_element_type=F32)

        @pl.when(k == nk - 1)
        def _():
            r = acc[...]
            res = epilogue(r, *[e[...] for e in extras]) if epilogue is not None else (r,)
            for o, v in zip(out_refs, res):
                o[...] = v.astype(o.dtype)

        if jobs:
            pl.when(step == total - 1)(lambda: _job_phase(jobs, "end", jin, jout, jsem))

    sem = ("arbitrary",) * 3 if jobs else ("parallel", "parallel", "arbitrary")
    res = pl.pallas_call(
        body, name=name, grid=grid,
        in_specs=[s for _, s in ins] + [_ANY] * n_ji,
        out_specs=[s for _, s in outs] + [_ANY] * n_jo,
        out_shape=[s for s, _ in outs] + j_outs,
        scratch_shapes=[pltpu.VMEM(acc_shape, F32)] + j_sems,
        input_output_aliases=aliases,
        compiler_params=_params(sem),
    )(*[a for a, _ in ins], *j_ins)
    return res


def _mm_nn(name, a, b, out_dtype, *, a_seg=None, out_seg=None, tm=1024, tn=1024, tk=512, epilogue=None,
           extras=(), n_out=1, out_dtypes=None, jobs=()):
    if a_seg:
        _, M, ks = a.shape
        K = ks * a_seg
    else:
        M, K = a.shape
        ks = K
    N = b.shape[1]
    ns = N // out_seg if out_seg else N
    tm, tn, tk = _tile(M, tm), _tile(ns, tn), _tile(ks, tk)
    nks, nns = ks // tk, ns // tn
    grid = (M // tm, N // tn, K // tk)
    if a_seg:
        a_spec = pl.BlockSpec((None, tm, tk), lambda i, j, k: (k // nks, i, k % nks))
    else:
        a_spec = pl.BlockSpec((tm, tk), lambda i, j, k: (i, k))
    b_spec = pl.BlockSpec((tk, tn), lambda i, j, k: (k, j))
    if out_seg:
        o_spec = pl.BlockSpec((None, tm, tn), lambda i, j, k: (j // nns, i, j % nns))
        o_shape = (out_seg, M, ns)
    else:
        o_spec = pl.BlockSpec((tm, tn), lambda i, j, k: (i, j))
        o_shape = (M, N)
    dts = out_dtypes or [out_dtype] * n_out
    outs = [(jax.ShapeDtypeStruct(o_shape, dt), o_spec) for dt in dts]
    ins = [(a, a_spec), (b, b_spec)] + [(e, o_spec) for e in extras]
    return _matmul(name, ins, outs, grid, DIMS_NN, (tm, tn), epilogue, jobs)


def _mm_nt(name, a, b, out_dtype, *, a_seg=None, out_seg=None, tm=1024, tn=1024, tk=512, epilogue=None, extras=(),
           jobs=()):
    if a_seg:
        _, M, ks = a.shape
        K = ks * a_seg
    else:
        M, K = a.shape
        ks = K
    N = b.shape[0]
    ns = N // out_seg if out_seg else N
    tm, tn, tk = _tile(M, tm), _tile(ns, tn), _tile(ks, tk)
    nks, nns = ks // tk, ns // tn
    grid = (M // tm, N // tn, K // tk)
    if a_seg:
        a_spec = pl.BlockSpec((None, tm, tk), lambda i, j, k: (k // nks, i, k % nks))
    else:
        a_spec = pl.BlockSpec((tm, tk), lambda i, j, k: (i, k))
    b_spec = pl.BlockSpec((tn, tk), lambda i, j, k: (j, k))
    if out_seg:
        o_spec = pl.BlockSpec((None, tm, tn), lambda i, j, k: (j // nns, i, j % nns))
        o_shape = (out_seg, M, ns)
    else:
        o_spec = pl.BlockSpec((tm, tn), lambda i, j, k: (i, j))
        o_shape = (M, N)
    outs = [(jax.ShapeDtypeStruct(o_shape, out_dtype), o_spec)]
    ins = [(a, a_spec), (b, b_spec)] + [(e, o_spec) for e in extras]
    return _matmul(name, ins, outs, grid, DIMS_NT, (tm, tn), epilogue, jobs)


def _mm_tn(name, a, b, out_dtype, *, a_seg=None, b_seg=None, out_blocks=None, tm=1024, tn=1024, tk=512, jobs=()):
    if a_seg:
        _, T, ms = a.shape
        M = ms * a_seg
    else:
        T, M = a.shape
        ms = M
    if b_seg:
        _, _, ns = b.shape
        N = ns * b_seg
    else:
        N = b.shape[1]
        ns = N
    nb_cols = N // out_blocks if out_blocks else N
    tm, tk = _tile(ms, tm), _tile(T, tk)
    tn = _tile(math.gcd(ns, nb_cols), tn)
    nms, nns, nbs = ms // tm, ns // tn, nb_cols // tn
    grid = (M // tm, N // tn, T // tk)
    if a_seg:
        a_spec = pl.BlockSpec((None, tk, tm), lambda i, j, k: (i // nms, k, i % nms))
    else:
        a_spec = pl.BlockSpec((tk, tm), lambda i, j, k: (k, i))
    if b_seg:
        b_spec = pl.BlockSpec((None, tk, tn), lambda i, j, k: (j // nns, k, j % nns))
    else:
        b_spec = pl.BlockSpec((tk, tn), lambda i, j, k: (k, j))
    if out_blocks:
        o_spec = pl.BlockSpec((None, tm, tn), lambda i, j, k: (j // nbs, i, j % nbs))
        o_shape = (out_blocks, M, nb_cols)
    else:
        o_spec = pl.BlockSpec((tm, tn), lambda i, j, k: (i, j))
        o_shape = (M, N)
    outs = [(jax.ShapeDtypeStruct(o_shape, out_dtype), o_spec)]
    return _matmul(name, [(a, a_spec), (b, b_spec)], outs, grid, DIMS_TN, (tm, tn), None, jobs)


def _rms(x):
    return lax.rsqrt(jnp.mean(x * x, axis=-1, keepdims=True) + NORM_EPS)


def _row_tile(T):
    return _tile(T, 256)


def _norm_fwd(name, x, g):
    T, D = x.shape
    tr = _row_tile(T)

    def body(x_ref, g_ref, h_ref):
        xv = x_ref[...]
        h_ref[...] = (xv * _rms(xv) * g_ref[...]).astype(h_ref.dtype)

    row = pl.BlockSpec((tr, D), lambda i: (i, 0))
    vec = pl.BlockSpec((1, D), lambda i: (0, 0))
    return pl.pallas_call(body, name=name, grid=(T // tr,), in_specs=[row, vec], out_specs=row,
                          out_shape=jax.ShapeDtypeStruct((T, D), BF16), compiler_params=_params(("parallel",)))(x, g)


def _resid_norm(name, x, br, g_post, g_next):
    T, D = x.shape
    tr = _row_tile(T)

    def body(x_ref, br_ref, gp_ref, gn_ref, xn_ref, h_ref):
        b = br_ref[...]
        xn = x_ref[...] + b * _rms(b) * gp_ref[...]
        xn_ref[...] = xn
        h_ref[...] = (xn * _rms(xn) * gn_ref[...]).astype(h_ref.dtype)

    row = pl.BlockSpec((tr, D), lambda i: (i, 0))
    vec = pl.BlockSpec((1, D), lambda i: (0, 0))
    return pl.pallas_call(body, name=name, grid=(T // tr,), in_specs=[row, row, vec, vec], out_specs=[row, row],
                          out_shape=[jax.ShapeDtypeStruct((T, D), F32), jax.ShapeDtypeStruct((T, D), BF16)],
                          compiler_params=_params(("parallel",)))(x, br, g_post, g_next)


def _final_loss(name, x, br, g_post, target):
    T, D = x.shape
    tr = _row_tile(T)

    def body(x_ref, br_ref, gp_ref, t_ref, dy_ref, ls_ref):
        b = br_ref[...]
        err = x_ref[...] + b * _rms(b) * gp_ref[...] - t_ref[...]
        dy_ref[...] = err * (1.0 / D)

        @pl.when(pl.program_id(0) == 0)
        def _():
            ls_ref[...] = jnp.zeros_like(ls_ref)

        ls_ref[...] += jnp.sum(err * err)

    row = pl.BlockSpec((tr, D), lambda i: (i, 0))
    vec = pl.BlockSpec((1, D), lambda i: (0, 0))
    acc = pl.BlockSpec((SUBLANES, LANES), lambda i: (0, 0))
    return pl.pallas_call(body, name=name, grid=(T // tr,), in_specs=[row, row, vec, row], out_specs=[row, acc],
                          out_shape=[jax.ShapeDtypeStruct((T, D), F32), jax.ShapeDtypeStruct((SUBLANES, LANES), F32)],
                          compiler_params=_params(("arbitrary",)))(x, br, g_post, target)


def _norm_bwd(name, x, g, dy, add, out_dtype):
    T, D = x.shape
    tr = _row_tile(T)
    has_add = add is not None

    def body(*refs):
        if has_add:
            x_ref, g_ref, dy_ref, add_ref, dx_ref, dg_ref = refs
        else:
            x_ref, g_ref, dy_ref, dx_ref, dg_ref = refs
        xv = x_ref[...]
        r = _rms(xv)
        xhat = xv * r
        dyv = dy_ref[...].astype(F32)
        gdy = dyv * g_ref[...]
        dx = r * (gdy - xhat * jnp.mean(gdy * xhat, axis=-1, keepdims=True))
        if has_add:
            dx = dx + add_ref[...]
        dx_ref[...] = dx.astype(dx_ref.dtype)

        @pl.when(pl.program_id(0) == 0)
        def _():
            dg_ref[...] = jnp.zeros_like(dg_ref)

        dg_ref[...] += jnp.sum(dyv * xhat, axis=0, keepdims=True)

    row = pl.BlockSpec((tr, D), lambda i: (i, 0))
    vec = pl.BlockSpec((1, D), lambda i: (0, 0))
    ins = [x, g, dy] + ([add] if has_add else [])
    specs = [row, vec, row] + ([row] if has_add else [])
    return pl.pallas_call(body, name=name, grid=(T // tr,), in_specs=specs, out_specs=[row, vec],
                          out_shape=[jax.ShapeDtypeStruct((T, D), out_dtype), jax.ShapeDtypeStruct((1, D), F32)],
                          compiler_params=_params(("arbitrary",)))(*ins)


HALO = SUBLANES
TIME_CHUNK = 512


def _chunks(T):
    tc = min(TIME_CHUNK, T)
    assert T % tc == 0 and tc % SUBLANES == 0
    return [(t0, tc) for t0 in range(0, T, tc)]


def _log_sigmoid(x):
    return -(jnp.maximum(-x, 0.0) + jnp.log(1.0 + jnp.exp(-jnp.abs(x))))


def _one_minus_exp(x):
    series = -x * (1.0 + x * (0.5 + x * (1.0 / 6.0 + x * (1.0 / 24.0))))
    return jnp.where(x > -0.01, series, 1.0 - jnp.exp(x))


_GELU_C = math.sqrt(2.0 / math.pi)


def _gelu(x):
    return 0.5 * x * (1.0 + jnp.tanh(_GELU_C * (x + 0.044715 * x * x * x)))


def _gelu_grad(x):
    th = jnp.tanh(_GELU_C * (x + 0.044715 * x * x * x))
    return 0.5 * (1.0 + th) + 0.5 * x * (1.0 - th * th) * _GELU_C * (1.0 + 3.0 * 0.044715 * x * x)


def _tile_scan(a, b, reverse):
    rows = a.shape[0]
    pos = lax.broadcasted_iota(jnp.int32, a.shape, 0) & (SUBLANES - 1)
    for d in (1, 2, 4):
        if reverse:
            ok = pos < SUBLANES - d
            shift = rows - d
        else:
            ok = pos >= d
            shift = d
        a_sh = jnp.where(ok, pltpu.roll(a, shift, 0), 1.0)
        b_sh = jnp.where(ok, pltpu.roll(b, shift, 0), 0.0)
        b = a * b_sh + b
        a = a * a_sh
    return a, b


def _carry_scan(a_s, b_s, T, reverse):
    n = T // SUBLANES
    edge = 0 if reverse else SUBLANES - 1

    def step(j, carry):
        g = (n - 1 - j) if reverse else j
        r = pl.multiple_of(g * SUBLANES, SUBLANES)
        h = b_s[pl.ds(r, SUBLANES), :] + a_s[pl.ds(r, SUBLANES), :] * carry
        b_s[pl.ds(r, SUBLANES), :] = h
        return jnp.broadcast_to(h[edge:edge + 1, :], h.shape)

    lax.fori_loop(0, n, step, jnp.zeros((SUBLANES, a_s.shape[1]), F32))


def _seg_spec(T, seg, nblk):
    return pl.BlockSpec((None, T, LANES), lambda c: (seg, 0, c))


def _rows_to_tile(rows):
    idx = lax.broadcasted_iota(jnp.int32, (SUBLANES, LANES), 0)
    out = jnp.zeros((SUBLANES, LANES), F32)
    for k, r in enumerate(rows):
        out = jnp.where(idx == k, r, out)
    return out


def _mixer_a_fwd(proj, conv_a):
    _, T, C = proj.shape
    nblk = C // LANES
    chunks = _chunks(T)

    def body(bg_ref, cg_ref, ax_ref, w_ref, y_ref, p_s):
        p_s[pl.ds(0, HALO), :] = jnp.zeros((HALO, LANES), F32)
        for t0, tc in chunks:
            p_s[pl.ds(HALO + t0, tc), :] = cg_ref[pl.ds(t0, tc), :] * ax_ref[pl.ds(t0, tc), :]
        w = w_ref[...]
        for t0, tc in chunks:
            c = (w[2:3, :] * p_s[pl.ds(HALO + t0, tc), :] + w[1:2, :] * p_s[pl.ds(HALO + t0 - 1, tc), :]
                 + w[0:1, :] * p_s[pl.ds(HALO + t0 - 2, tc), :])
            y_ref[pl.ds(t0, tc), :] = (bg_ref[pl.ds(t0, tc), :] * c).astype(y_ref.dtype)

    return pl.pallas_call(
        body, name="mixer_a_fwd", grid=(nblk,),
        in_specs=[_seg_spec(T, 0, nblk), _seg_spec(T, 1, nblk), _seg_spec(T, 2, nblk),
                  pl.BlockSpec((3, LANES), lambda c: (0, c))],
        out_specs=pl.BlockSpec((T, LANES), lambda c: (0, c)),
        out_shape=jax.ShapeDtypeStruct((T, C), BF16),
        scratch_shapes=[pltpu.VMEM((T + HALO, LANES), F32)],
        compiler_params=_params(("parallel",)))(proj, proj, proj, conv_a)


def _mixer_a_bwd(proj, conv_a, dy):
    _, T, C = proj.shape
    nblk = C // LANES
    chunks = _chunks(T)

    def body(bg_ref, cg_ref, ax_ref, w_ref, dy_ref, dp_ref, dw_ref, p_s, dc_s):
        p_s[pl.ds(0, HALO), :] = jnp.zeros((HALO, LANES), F32)
        dc_s[pl.ds(T, HALO), :] = jnp.zeros((HALO, LANES), F32)
        for t0, tc in chunks:
            p_s[pl.ds(HALO + t0, tc), :] = cg_ref[pl.ds(t0, tc), :] * ax_ref[pl.ds(t0, tc), :]
        w = w_ref[...]
        for t0, tc in chunks:
            c = (w[2:3, :] * p_s[pl.ds(HALO + t0, tc), :] + w[1:2, :] * p_s[pl.ds(HALO + t0 - 1, tc), :]
                 + w[0:1, :] * p_s[pl.ds(HALO + t0 - 2, tc), :])
            dyv = dy_ref[pl.ds(t0, tc), :]
            dp_ref[0, pl.ds(t0, tc), :] = (dyv * c).astype(dp_ref.dtype)
            dc_s[pl.ds(t0, tc), :] = dyv * bg_ref[pl.ds(t0, tc), :]
        dw = [jnp.zeros((1, LANES), F32) for _ in range(3)]
        for t0, tc in chunks:
            dc = dc_s[pl.ds(t0, tc), :]
            dpv = w[2:3, :] * dc + w[1:2, :] * dc_s[pl.ds(t0 + 1, tc), :] + w[0:1, :] * dc_s[pl.ds(t0 + 2, tc), :]
            dp_ref[1, pl.ds(t0, tc), :] = (dpv * ax_ref[pl.ds(t0, tc), :]).astype(dp_ref.dtype)
            dp_ref[2, pl.ds(t0, tc), :] = (dpv * cg_ref[pl.ds(t0, tc), :]).astype(dp_ref.dtype)
            for k in range(3):
                dw[k] = dw[k] + jnp.sum(dc * p_s[pl.ds(HALO + t0 - (2 - k), tc), :], axis=0, keepdims=True)
        dw_ref[...] = _rows_to_tile(dw)

    return pl.pallas_call(
        body, name="mixer_a_bwd", grid=(nblk,),
        in_specs=[_seg_spec(T, 0, nblk), _seg_spec(T, 1, nblk), _seg_spec(T, 2, nblk),
                  pl.BlockSpec((3, LANES), lambda c: (0, c)), _seg_spec(T, 0, nblk)],
        out_specs=[pl.BlockSpec((3, T, LANES), lambda c: (0, 0, c)),
                   pl.BlockSpec((None, SUBLANES, LANES), lambda c: (c, 0, 0))],
        out_shape=[jax.ShapeDtypeStruct((3, T, C), BF16), jax.ShapeDtypeStruct((nblk, SUBLANES, LANES), F32)],
        scratch_shapes=[pltpu.VMEM((T + HALO, LANES), F32), pltpu.VMEM((T + HALO, LANES), F32)],
        compiler_params=_params(("parallel",)))(proj, proj, proj, conv_a, dy)


def _rg_gates(xr, wa, ba, wx, bx, ls):
    xb = xr.astype(BF16)
    r = jax.nn.sigmoid(jnp.dot(xb, wa, preferred_element_type=F32) + ba)
    i = jax.nn.sigmoid(jnp.dot(xb, wx, preferred_element_type=F32) + bx)
    log_a = LRU_C * r * ls
    a = jnp.exp(log_a)
    mult = jnp.sqrt(_one_minus_exp(2.0 * log_a))
    return r, i, a, mult


def _conv4(xh_s, cw, bias, t0, tc):
    return (cw[3:4, :] * xh_s[pl.ds(HALO + t0, tc), :] + cw[2:3, :] * xh_s[pl.ds(HALO + t0 - 1, tc), :]
            + cw[1:2, :] * xh_s[pl.ds(HALO + t0 - 2, tc), :] + cw[0:1, :] * xh_s[pl.ds(HALO + t0 - 3, tc), :] + bias)


def _mixer_b_specs(T, nblk):
    vec = pl.BlockSpec((1, LANES), lambda c: (0, c))
    mat = pl.BlockSpec((None, LANES, LANES), lambda c: (c, 0, 0))
    return [_seg_spec(T, 3, nblk), _seg_spec(T, 4, nblk), pl.BlockSpec((4, LANES), lambda c: (0, c)),
            vec, mat, vec, mat, vec, vec]


def _mixer_b_fwd(proj, conv_b, bias, wa, ba, wx, bx, lam):
    _, T, C = proj.shape
    nblk = C // LANES
    chunks = _chunks(T)

    def body(gate_ref, x_ref, cw_ref, cb_ref, wa_ref, ba_ref, wx_ref, bx_ref, lam_ref, y_ref, xh_s, a_s, b_s):
        xh_s[pl.ds(0, HALO), :] = jnp.zeros((HALO, LANES), F32)
        for t0, tc in chunks:
            xh_s[pl.ds(HALO + t0, tc), :] = x_ref[pl.ds(t0, tc), :]
        cw, bias_v = cw_ref[...], cb_ref[...]
        ls = _log_sigmoid(lam_ref[...])
        for t0, tc in chunks:
            xr = _conv4(xh_s, cw, bias_v, t0, tc)
            r, i, a, mult = _rg_gates(xr, wa_ref[...], ba_ref[...], wx_ref[...], bx_ref[...], ls)
            ac, hc = _tile_scan(a, mult * i * xr, reverse=False)
            a_s[pl.ds(t0, tc), :] = ac
            b_s[pl.ds(t0, tc), :] = hc
        _carry_scan(a_s, b_s, T, reverse=False)
        for t0, tc in chunks:
            y_ref[pl.ds(t0, tc), :] = (b_s[pl.ds(t0, tc), :] * _gelu(gate_ref[pl.ds(t0, tc), :])).astype(y_ref.dtype)

    return pl.pallas_call(
        body, name="mixer_b_fwd", grid=(nblk,), in_specs=_mixer_b_specs(T, nblk),
        out_specs=pl.BlockSpec((T, LANES), lambda c: (0, c)),
        out_shape=jax.ShapeDtypeStruct((T, C), BF16),
        scratch_shapes=[pltpu.VMEM((T + HALO, LANES), F32), pltpu.VMEM((T, LANES), F32), pltpu.VMEM((T, LANES), F32)],
        compiler_params=_params(("parallel",)))(proj, proj, conv_b, bias, wa, ba, wx, bx, lam)


_ROW_CONV, _ROW_BIAS, _ROW_BA, _ROW_BX, _ROW_LAM = 0, 4, 5, 6, 7


def _mixer_b_bwd(proj, conv_b, bias, wa, ba, wx, bx, lam, dy):
    _, T, C = proj.shape
    nblk = C // LANES
    chunks = _chunks(T)

    def body(gate_ref, x_ref, cw_ref, cb_ref, wa_ref, ba_ref, wx_ref, bx_ref, lam_ref, dy_ref,
             dp_ref, sm_ref, dwa_ref, dwx_ref, xh_s, xr_s, r_s, i_s, a_s, h_s, sa_s, sb_s, dx_s):
        zero_halo = jnp.zeros((HALO, LANES), F32)
        xh_s[pl.ds(0, HALO), :] = zero_halo
        h_s[pl.ds(0, HALO), :] = zero_halo
        a_s[pl.ds(T, HALO), :] = zero_halo
        dx_s[pl.ds(T, HALO), :] = zero_halo
        for t0, tc in chunks:
            xh_s[pl.ds(HALO + t0, tc), :] = x_ref[pl.ds(t0, tc), :]
        cw, bias_v = cw_ref[...], cb_ref[...]
        lam_v = lam_ref[...]
        ls = _log_sigmoid(lam_v)
        wa_v, wx_v, ba_v, bx_v = wa_ref[...], wx_ref[...], ba_ref[...], bx_ref[...]
        for t0, tc in chunks:
            xr = _conv4(xh_s, cw, bias_v, t0, tc)
            r, i, a, mult = _rg_gates(xr, wa_v, ba_v, wx_v, bx_v, ls)
            xr_s[pl.ds(t0, tc), :] = xr
            r_s[pl.ds(t0, tc), :] = r
            i_s[pl.ds(t0, tc), :] = i
            a_s[pl.ds(t0, tc), :] = a
            ac, hc = _tile_scan(a, mult * i * xr, reverse=False)
            sa_s[pl.ds(t0, tc), :] = ac
            sb_s[pl.ds(t0, tc), :] = hc
        _carry_scan(sa_s, sb_s, T, reverse=False)
        for t0, tc in chunks:
            h_s[pl.ds(HALO + t0, tc), :] = sb_s[pl.ds(t0, tc), :]
        for t0, tc in chunks:
            gv = gate_ref[pl.ds(t0, tc), :]
            dyv = dy_ref[pl.ds(t0, tc), :]
            dp_ref[0, pl.ds(t0, tc), :] = (dyv * h_s[pl.ds(HALO + t0, tc), :] * _gelu_grad(gv)).astype(dp_ref.dtype)
            ac, gc = _tile_scan(a_s[pl.ds(t0 + 1, tc), :], dyv * _gelu(gv), reverse=True)
            sa_s[pl.ds(t0, tc), :] = ac
            sb_s[pl.ds(t0, tc), :] = gc
        _carry_scan(sa_s, sb_s, T, reverse=True)
        acc = {k: jnp.zeros((1, LANES), F32) for k in ("bias", "ba", "bx", "lam")}
        dwa = jnp.zeros((LANES, LANES), F32)
        dwx = jnp.zeros((LANES, LANES), F32)
        for t0, tc in chunks:
            dht = sb_s[pl.ds(t0, tc), :]
            xr, r, i, a = xr_s[pl.ds(t0, tc), :], r_s[pl.ds(t0, tc), :], i_s[pl.ds(t0, tc), :], a_s[pl.ds(t0, tc), :]
            mult = jnp.sqrt(_one_minus_exp(2.0 * LRU_C * r * ls))
            da = dht * h_s[pl.ds(HALO + t0 - 1, tc), :]
            dmult = dht * i * xr
            di = dht * mult * xr
            dlog_a = da * a - dmult * a * a / mult
            dpa = dlog_a * (LRU_C * ls) * r * (1.0 - r)
            dpx = di * i * (1.0 - i)
            acc["lam"] = acc["lam"] + jnp.sum(dlog_a * r, axis=0, keepdims=True)
            acc["ba"] = acc["ba"] + jnp.sum(dpa, axis=0, keepdims=True)
            acc["bx"] = acc["bx"] + jnp.sum(dpx, axis=0, keepdims=True)
            xb, dpab, dpxb = xr.astype(BF16), dpa.astype(BF16), dpx.astype(BF16)
            dwa = dwa + lax.dot_general(xb, dpab, DIMS_TN, preferred_element_type=F32)
            dwx = dwx + lax.dot_general(xb, dpxb, DIMS_TN, preferred_element_type=F32)
            dxr = (dht * mult * i + lax.dot_general(dpab, wa_v, DIMS_NT, preferred_element_type=F32)
                   + lax.dot_general(dpxb, wx_v, DIMS_NT, preferred_element_type=F32))
            acc["bias"] = acc["bias"] + jnp.sum(dxr, axis=0, keepdims=True)
            dx_s[pl.ds(t0, tc), :] = dxr
        dcw = [jnp.zeros((1, LANES), F32) for _ in range(4)]
        for t0, tc in chunks:
            dxr = dx_s[pl.ds(t0, tc), :]
            dxin = (cw[3:4, :] * dxr + cw[2:3, :] * dx_s[pl.ds(t0 + 1, tc), :] + cw[1:2, :] * dx_s[pl.ds(t0 + 2, tc), :]
                    + cw[0:1, :] * dx_s[pl.ds(t0 + 3, tc), :])
            dp_ref[1, pl.ds(t0, tc), :] = dxin.astype(dp_ref.dtype)
            for k in range(4):
                dcw[k] = dcw[k] + jnp.sum(dxr * xh_s[pl.ds(HALO + t0 - (3 - k), tc), :], axis=0, keepdims=True)
        dlam = acc["lam"] * LRU_C * jax.nn.sigmoid(-lam_v)
        sm_ref[...] = _rows_to_tile(dcw + [acc["bias"], acc["ba"], acc["bx"], dlam])
        dwa_ref[...] = dwa
        dwx_ref[...] = dwx

    big = lambda halo: pltpu.VMEM((T + halo, LANES), F32)
    mat = pl.BlockSpec((None, LANES, LANES), lambda c: (c, 0, 0))
    return pl.pallas_call(
        body, name="mixer_b_bwd", grid=(nblk,),
        in_specs=_mixer_b_specs(T, nblk) + [_seg_spec(T, 1, nblk)],
        out_specs=[pl.BlockSpec((2, T, LANES), lambda c: (0, 0, c)),
                   pl.BlockSpec((None, SUBLANES, LANES), lambda c: (c, 0, 0)), mat, mat],
        out_shape=[jax.ShapeDtypeStruct((2, T, C), BF16), jax.ShapeDtypeStruct((nblk, SUBLANES, LANES), F32),
                   jax.ShapeDtypeStruct((nblk, LANES, LANES), F32), jax.ShapeDtypeStruct((nblk, LANES, LANES), F32)],
        scratch_shapes=[big(HALO), big(0), big(0), big(0), big(HALO), big(HALO), big(0), big(0), big(HALO)],
        compiler_params=_params(("parallel",)))(proj, proj, conv_b, bias, wa, ba, wx, bx, lam, dy)


ATT_BLOCK = 128
ATT_GROUP = 4
ATT_TILE = ATT_BLOCK * ATT_GROUP
ATT_UNDERFLOW = -110.0
ATT_UNVISITED = -1e30


def _split_dot(x, m):
    hi = x.astype(BF16)
    lo = (x - hi.astype(F32)).astype(BF16)
    return jnp.dot(hi, m, preferred_element_type=F32) + jnp.dot(lo, m, preferred_element_type=F32)


def _sub(x, j):
    return x[:, j * ATT_BLOCK:(j + 1) * ATT_BLOCK]


def _stack_rows(x):
    return jnp.concatenate([_sub(x, j) for j in range(ATT_GROUP)], axis=0)


def _unstack_rows(x, offsets):
    return jnp.concatenate([x[j * ATT_BLOCK:(j + 1) * ATT_BLOCK, :] + offsets[j] for j in range(ATT_GROUP)], axis=1)


def _att_tile(q, k_ref, q0, qb, it, scale):
    hi = (qb + 1 - ATT_GROUP * it) * ATT_BLOCK
    k0 = pl.multiple_of(jnp.maximum(hi - ATT_TILE, 0), ATT_BLOCK)
    kt = k_ref[pl.ds(k0, ATT_TILE), :]
    z = lax.dot_general(q, kt, DIMS_NT, preferred_element_type=F32) * scale
    key = k0 + lax.broadcasted_iota(jnp.int32, z.shape, 1)
    row = q0 + lax.broadcasted_iota(jnp.int32, z.shape, 0)
    mask = (key < row) & (key < hi)
    n = jnp.where(mask, -(jnp.maximum(z, 0.0) + jnp.log(1.0 + jnp.exp(-jnp.abs(z)))), 0.0)
    return k0, kt, z, mask, n


def _suffix_in_tile(n, upper, run):
    rs = [jnp.sum(_sub(n, j), axis=1, keepdims=True) for j in range(ATT_GROUP)]
    offs = [None] * ATT_GROUP
    offs[ATT_GROUP - 1] = run
    for j in range(ATT_GROUP - 2, -1, -1):
        offs[j] = offs[j + 1] + rs[j + 1]
    return _unstack_rows(_split_dot(_stack_rows(n), upper), offs), offs[0] + rs[0]


def _head_spec(T, seg, heads):
    return pl.BlockSpec((None, T, ATT_HEAD_DIM), lambda h: (seg, 0, h))


def _attention_fwd(qkv):
    _, T, D = qkv.shape
    heads = D // ATT_HEAD_DIM
    nq = T // ATT_BLOCK
    assert nq <= LANES
    scale = 1.0 / math.sqrt(ATT_HEAD_DIM)

    def body(q_ref, k_ref, v_ref, o_ref, r_ref, acc_s, run_s):
        rr = lax.broadcasted_iota(jnp.int32, (ATT_BLOCK, ATT_BLOCK), 0)
        cc = lax.broadcasted_iota(jnp.int32, (ATT_BLOCK, ATT_BLOCK), 1)
        upper = jnp.where(rr > cc, 1.0, 0.0).astype(BF16)
        lane = lax.broadcasted_iota(jnp.int32, (ATT_BLOCK, LANES), 1)

        def q_loop(qb, _):
            q0 = pl.multiple_of(qb * ATT_BLOCK, ATT_BLOCK)
            q = q_ref[pl.ds(q0, ATT_BLOCK), :]
            acc_s[...] = jnp.zeros_like(acc_s)
            run_s[...] = jnp.zeros_like(run_s)
            r_ref[pl.ds(q0, ATT_BLOCK), :] = jnp.full((ATT_BLOCK, LANES), ATT_UNVISITED, F32)
            n_tiles = (qb + ATT_GROUP) // ATT_GROUP

            def tile(carry):
                it, _ = carry
                k0, _, z, mask, n = _att_tile(q, k_ref, q0, qb, it, scale)
                run = run_s[...]
                suffix, run_next = _suffix_in_tile(n, upper, run)
                w = jnp.where(mask, jnp.exp(z + n + suffix), 0.0)
                acc_s[...] += jnp.dot(w.astype(BF16), v_ref[pl.ds(k0, ATT_TILE), :], preferred_element_type=F32)
                r_ref[pl.ds(q0, ATT_BLOCK), :] = jnp.where(lane == it, run, r_ref[pl.ds(q0, ATT_BLOCK), :])
                run_s[...] = run_next
                return it + 1, jnp.max(run_next) >= ATT_UNDERFLOW

            lax.while_loop(lambda c: (c[0] < n_tiles) & c[1], tile, (jnp.int32(0), jnp.bool_(True)))
            o_ref[pl.ds(q0, ATT_BLOCK), :] = acc_s[...].astype(o_ref.dtype)
            return 0

        lax.fori_loop(0, nq, q_loop, 0)

    return pl.pallas_call(
        body, name="attention_fwd", grid=(heads,),
        in_specs=[_head_spec(T, 0, heads), _head_spec(T, 1, heads), _head_spec(T, 2, heads)],
        out_specs=[pl.BlockSpec((T, ATT_HEAD_DIM), lambda h: (0, h)), pl.BlockSpec((None, T, LANES), lambda h: (h, 0, 0))],
        out_shape=[jax.ShapeDtypeStruct((T, D), BF16), jax.ShapeDtypeStruct((heads, T, LANES), F32)],
        scratch_shapes=[pltpu.VMEM((ATT_BLOCK, ATT_HEAD_DIM), F32), pltpu.VMEM((ATT_BLOCK, LANES), F32)],
        compiler_params=_params(("parallel",)))(qkv, qkv, qkv)


def _attention_bwd(qkv, do, rmat):
    _, T, D = qkv.shape
    heads = D // ATT_HEAD_DIM
    nq = T // ATT_BLOCK
    scale = 1.0 / math.sqrt(ATT_HEAD_DIM)

    def body(q_ref, k_ref, v_ref, do_ref, r_ref, dqkv_ref, dk_s, dv_s, dq_s, left_s):
        rr = lax.broadcasted_iota(jnp.int32, (ATT_BLOCK, ATT_BLOCK), 0)
        cc = lax.broadcasted_iota(jnp.int32, (ATT_BLOCK, ATT_BLOCK), 1)
        upper = jnp.where(rr > cc, 1.0, 0.0).astype(BF16)
        lower = jnp.where(rr < cc, 1.0, 0.0).astype(BF16)
        lane = lax.broadcasted_iota(jnp.int32, (ATT_BLOCK, LANES), 1)
        dk_s[...] = jnp.zeros_like(dk_s)
        dv_s[...] = jnp.zeros_like(dv_s)

        def q_loop(qb, _):
            q0 = pl.multiple_of(qb * ATT_BLOCK, ATT_BLOCK)
            q = q_ref[pl.ds(q0, ATT_BLOCK), :]
            dov = do_ref[pl.ds(q0, ATT_BLOCK), :]
            dq_s[...] = jnp.zeros_like(dq_s)
            left_s[...] = jnp.zeros_like(left_s)
            rm = r_ref[pl.ds(q0, ATT_BLOCK), :]
            n_tiles = (qb + ATT_GROUP) // ATT_GROUP
            seen = (jnp.max(rm, axis=0, keepdims=True) > 0.5 * ATT_UNVISITED) & (lane[0:1, :] < n_tiles)
            n_visited = jnp.sum(jnp.where(seen, 1.0, 0.0)).astype(jnp.int32)

            def tile(j, _):
                it = n_visited - 1 - j
                k0, kt, z, mask, n = _att_tile(q, k_ref, q0, qb, it, scale)
                vt = v_ref[pl.ds(k0, ATT_TILE), :]
                run = jnp.sum(jnp.where(lane == it, r_ref[pl.ds(q0, ATT_BLOCK), :], 0.0), axis=1, keepdims=True)
                suffix, _ = _suffix_in_tile(n, upper, run)
                s = z + n
                w = jnp.where(mask, jnp.exp(s + suffix), 0.0)
                e = w * lax.dot_general(dov, vt, DIMS_NT, preferred_element_type=F32)
                es = [jnp.sum(_sub(e, g), axis=1, keepdims=True) for g in range(ATT_GROUP)]
                pre = [left_s[...]]
                for g in range(ATT_GROUP):
                    pre.append(pre[g] + es[g])
                before = _unstack_rows(_split_dot(_stack_rows(e), lower), pre)
                sig = jnp.exp(s)
                dz = (jnp.where(mask, e * (1.0 - sig) - before * sig, 0.0) * scale).astype(BF16)
                dq_s[...] += jnp.dot(dz, kt, preferred_element_type=F32)
                dk_s[pl.ds(k0, ATT_TILE), :] += lax.dot_general(dz, q, DIMS_TN, preferred_element_type=F32)
                dv_s[pl.ds(k0, ATT_TILE), :] += lax.dot_general(w.astype(BF16), dov, DIMS_TN, preferred_element_type=F32)
                left_s[...] = pre[ATT_GROUP]
                return 0

            lax.fori_loop(0, n_visited, tile, 0)
            dqkv_ref[0, pl.ds(q0, ATT_BLOCK), :] = dq_s[...].astype(dqkv_ref.dtype)
            return 0

        lax.fori_loop(0, nq, q_loop, 0)
        dqkv_ref[1, :, :] = dk_s[...].astype(dqkv_ref.dtype)
        dqkv_ref[2, :, :] = dv_s[...].astype(dqkv_ref.dtype)

    return pl.pallas_call(
        body, name="attention_bwd", grid=(heads,),
        in_specs=[_head_spec(T, 0, heads), _head_spec(T, 1, heads), _head_spec(T, 2, heads),
                  pl.BlockSpec((T, ATT_HEAD_DIM), lambda h: (0, h)), pl.BlockSpec((None, T, LANES), lambda h: (h, 0, 0))],
        out_specs=pl.BlockSpec((3, T, ATT_HEAD_DIM), lambda h: (0, 0, h)),
        out_shape=jax.ShapeDtypeStruct((3, T, D), BF16),
        scratch_shapes=[pltpu.VMEM((T, ATT_HEAD_DIM), F32), pltpu.VMEM((T, ATT_HEAD_DIM), F32),
                        pltpu.VMEM((ATT_BLOCK, ATT_HEAD_DIM), F32), pltpu.VMEM((ATT_BLOCK, LANES), F32)],
        compiler_params=_params(("parallel",)))(qkv, qkv, qkv, do, rmat)


def _block_diag_pairs(w):
    h = w.shape[0]
    wp = w.reshape(h // 2, 2, RG_HEAD_DIM, RG_HEAD_DIM)
    z = jnp.zeros_like(wp[:, 0])
    top = jnp.concatenate([wp[:, 0], z], axis=2)
    bot = jnp.concatenate([z, wp[:, 1]], axis=2)
    return jnp.concatenate([top, bot], axis=1)


def _diag_pairs(g):
    n = g.shape[0]
    a = g[:, :RG_HEAD_DIM, :RG_HEAD_DIM]
    b = g[:, RG_HEAD_DIM:, RG_HEAD_DIM:]
    return jnp.stack([a, b], axis=1).reshape(2 * n, RG_HEAD_DIM, RG_HEAD_DIM)


class _Weights:
    def __init__(self, full, shards=None, plan=None):
        self.full, self.shards, self.plan = dict(full), shards or {}, plan or {}

    def __getitem__(self, name):
        return self.full[name]

    def jobs(self, call):
        return [_gather_job(self.shards[n]) for n in self.plan.get(call, ())]

    def deliver(self, call, outs):
        for n, g in zip(self.plan.get(call, ()), outs):
            self.full[n] = _gathered_layout(n, g)


def _gathered_layout(name, g):
    if name in ("w_in", "w_qkv"):
        return _from_col_blocks(g)
    if name in ("w_out", "w_o"):
        return g.reshape(g.shape[0] * g.shape[1], g.shape[2])
    return g


class _Grads:
    def __init__(self, lands=None, plan=None):
        self.lands, self.plan = dict(lands) if lands else None, plan or {}
        self.ready, self.sent = {}, {}

    def put(self, name, arr):
        self.ready[name] = arr

    def jobs(self, call):
        if self.lands is None:
            return []
        return [_exchange_job(self.ready[n], self.lands[n], part, parts) for n, part, parts in self.plan.get(call, ())]

    def deliver(self, call, outs):
        for (n, part, parts), o in zip(self.plan.get(call, ()), outs):
            self.lands[n] = o
            self.sent.setdefault(n, set()).add((part, parts))

    def flush(self, name):
        if self.lands is None:
            return
        for n, done in self.sent.items():
            assert len(done) == next(iter(done))[1], (n, done)
        rest = [n for n in self.ready if n not in self.sent]
        if rest:
            outs = _run_jobs(name, [_exchange_job(self.ready[n], self.lands[n]) for n in rest])
            for n, o in zip(rest, outs):
                self.lands[n] = o


def _mlp_fwd(tag, h, wts, run):
    T, D = h.shape
    w_up = wts["up" + tag]
    fb = w_up.shape[2]
    F = fb * N_DEV
    tm, tn, tk = _tile(T, 1024), _tile(fb, 1024), _tile(D, 512)
    nb = fb // tn

    def up_epilogue(u):
        r = jnp.maximum(u, 0.0)
        return u, r * r

    o_spec = pl.BlockSpec((tm, tn), lambda i, j, k: (i, j))
    u, act = run(
        _matmul, f"mlp_up_l{tag}",
        [(h, pl.BlockSpec((tm, tk), lambda i, j, k: (i, k))),
         (w_up, pl.BlockSpec((None, tk, tn), lambda i, j, k: (j // nb, k, j % nb)))],
        [(jax.ShapeDtypeStruct((T, F), BF16), o_spec), (jax.ShapeDtypeStruct((T, F), BF16), o_spec)],
        (T // tm, F // tn, D // tk), DIMS_NN, (tm, tn), up_epilogue, n_main=2)
    w_down = wts["down" + tag]
    tn2, tk2 = _tile(D, 1024), _tile(fb, 512)
    nkb = fb // tk2
    m = run(
        _matmul, f"mlp_down_l{tag}",
        [(act, pl.BlockSpec((tm, tk2), lambda i, j, k: (i, k))),
         (w_down, pl.BlockSpec((None, tk2, tn2), lambda i, j, k: (k // nkb, k % nkb, j)))],
        [(jax.ShapeDtypeStruct((T, D), F32), pl.BlockSpec((tm, tn2), lambda i, j, k: (i, j)))],
        (T // tm, D // tn2, F // tk2), DIMS_NN, (tm, tn2), None)
    return u, act, m


def _mlp_bwd(tag, h, u, act, dm, wts, grads, run):
    T, D = h.shape
    w_up, w_down = wts["up" + tag], wts["down" + tag]
    fb = w_up.shape[2]
    F = fb * N_DEV
    grads.put("down" + tag, run(_mm_tn, f"mlp_down_dw_l{tag}", act, dm, BF16).reshape(N_DEV, fb, D))
    tm, tn, tk = _tile(T, 1024), _tile(fb, 1024), _tile(D, 512)
    nb = fb // tn
    o_spec = pl.BlockSpec((tm, tn), lambda i, j, k: (i, j))
    du = run(
        _matmul, f"mlp_down_dx_l{tag}",
        [(dm, pl.BlockSpec((tm, tk), lambda i, j, k: (i, k))),
         (w_down, pl.BlockSpec((None, tn, tk), lambda i, j, k: (j // nb, j % nb, k))),
         (u, o_spec)],
        [(jax.ShapeDtypeStruct((T, F), BF16), o_spec)],
        (T // tm, F // tn, D // tk), DIMS_NT, (tm, tn),
        lambda r, uv: (r * (2.0 * jnp.maximum(uv.astype(F32), 0.0)),))
    grads.put("up" + tag, run(_mm_tn, f"mlp_up_dw_l{tag}", h, du, BF16, out_blocks=N_DEV))
    tn2, tk2 = _tile(D, 1024), _tile(fb, 512)
    nkb = fb // tk2
    return run(
        _matmul, f"mlp_up_dx_l{tag}",
        [(du, pl.BlockSpec((tm, tk2), lambda i, j, k: (i, k))),
         (w_up, pl.BlockSpec((None, tn2, tk2), lambda i, j, k: (k // nkb, j, k % nkb)))],
        [(jax.ShapeDtypeStruct((T, D), F32), pl.BlockSpec((tm, tn2), lambda i, j, k: (i, j)))],
        (T // tm, D // tn2, F // tk2), DIMS_NT, (tm, tn2), None)


def _local_step(x, target, gains, conv_a, conv_b, conv_b_bias, rg_w_a, rg_b_a, rg_w_x, rg_b_x, rg_lambda, wts, grads):
    T, D = x.shape
    g = lambda l, i: gains[l, i][None, :]
    wa_p = _block_diag_pairs(rg_w_a).astype(BF16)
    wx_p = _block_diag_pairs(rg_w_x).astype(BF16)

    def run(fn, name, *args, n_main=1, **kw):
        jw, jg = wts.jobs(name), grads.jobs(name)
        res = fn(name, *args, jobs=jw + jg, **kw)
        main, jo = res[:n_main], res[n_main:]
        wts.deliver(name, jo[:len(jw)])
        grads.deliver(name, jo[len(jw):])
        return main[0] if n_main == 1 else main

    h0 = _norm_fwd("norm_in", x, g(0, 0))
    proj = run(_mm_nn, "w_in_fwd", h0, wts["w_in"], F32, out_seg=5)
    y_a = _mixer_a_fwd(proj, conv_a)
    y_b = _mixer_b_fwd(proj, conv_b, conv_b_bias, wa_p, rg_b_a, wx_p, rg_b_x, rg_lambda)
    y = jnp.stack([y_a, y_b], axis=0)
    mix0 = run(_mm_nn, "w_out_fwd", y, wts["w_out"], F32, a_seg=2)
    x1, h1 = _resid_norm("resid_mix0", x, mix0, g(0, 1), g(0, 2))
    u0, act0, m0 = _mlp_fwd("0", h1, wts, run)
    x2, h2 = _resid_norm("resid_mlp0", x1, m0, g(0, 3), g(1, 0))
    qkv = run(_mm_nn, "w_qkv_fwd", h2, wts["w_qkv"], BF16, out_seg=3)
    o, rmat = _attention_fwd(qkv)
    mix1 = run(_mm_nn, "w_o_fwd", o, wts["w_o"], F32)
    x3, h3 = _resid_norm("resid_mix1", x2, mix1, g(1, 1), g(1, 2))
    u1, act1, m1 = _mlp_fwd("1", h3, wts, run)
    dx4, sq = _final_loss("loss", x3, m1, g(1, 3), target)

    dm1, dg13 = _norm_bwd("norm_bwd_m1", m1, g(1, 3), dx4, None, BF16)
    dh3 = _mlp_bwd("1", h3, u1, act1, dm1, wts, grads, run)
    dx3, dg12 = _norm_bwd("norm_bwd_x3", x3, g(1, 2), dh3, dx4, F32)
    dmix1, dg11 = _norm_bwd("norm_bwd_mix1", mix1, g(1, 1), dx3, None, BF16)
    grads.put("w_o", run(_mm_tn, "w_o_dw", o, dmix1, BF16).reshape(N_DEV, D // N_DEV, D))
    do = run(_mm_nt, "w_o_dx", dmix1, wts["w_o"], BF16)
    dqkv = _attention_bwd(qkv, do, rmat)
    grads.put("w_qkv", _col_blocks(run(_mm_tn, "w_qkv_dw", h2, dqkv, BF16, b_seg=3), N_DEV))
    dh2 = run(_mm_nt, "w_qkv_dx", dqkv, wts["w_qkv"], F32, a_seg=3)
    dx2, dg10 = _norm_bwd("norm_bwd_x2", x2, g(1, 0), dh2, dx3, F32)
    dm0, dg03 = _norm_bwd("norm_bwd_m0", m0, g(0, 3), dx2, None, BF16)
    dh1 = _mlp_bwd("0", h1, u0, act0, dm0, wts, grads, run)
    dx1, dg02 = _norm_bwd("norm_bwd_x1", x1, g(0, 2), dh1, dx2, F32)
    dmix0, dg01 = _norm_bwd("norm_bwd_mix0", mix0, g(0, 1), dx1, None, BF16)
    grads.put("w_out", run(_mm_tn, "w_out_dw", y, dmix0, BF16, a_seg=2).reshape(N_DEV, D // N_DEV, D))
    dy = run(_mm_nt, "w_out_dx", dmix0, wts["w_out"], F32, out_seg=2)
    dproj_a, dconv_a = _mixer_a_bwd(proj, conv_a, dy)
    dproj_b, sm_b, dwa_p, dwx_p = _mixer_b_bwd(proj, conv_b, conv_b_bias, wa_p, rg_b_a, wx_p, rg_b_x, rg_lambda, dy)
    dproj = jnp.concatenate([dproj_a, dproj_b], axis=0)
    grads.put("w_in", _col_blocks(run(_mm_tn, "w_in_dw", h0, dproj, BF16, b_seg=5), N_DEV))
    dh0 = run(_mm_nt, "w_in_dx", dproj, wts["w_in"], F32, a_seg=5)
    dx0, dg00 = _norm_bwd("norm_bwd_x0", x, g(0, 0), dh0, dx1, F32)

    C = D // 2
    lanes_to_vec = lambda t, row: t[:, row, :].reshape(1, C)
    small = {
        "norm_gains": jnp.concatenate([dg00, dg01, dg02, dg03, dg10, dg11, dg12, dg13], axis=0).reshape(2, 4, D),
        "conv_a": jnp.transpose(dconv_a[:, :3, :], (1, 0, 2)).reshape(3, C),
        "conv_b": jnp.transpose(sm_b[:, :4, :], (1, 0, 2)).reshape(4, C),
        "conv_b_bias": lanes_to_vec(sm_b, _ROW_BIAS),
        "rg_w_a": _diag_pairs(dwa_p),
        "rg_b_a": lanes_to_vec(sm_b, _ROW_BA),
        "rg_w_x": _diag_pairs(dwx_p),
        "rg_b_x": lanes_to_vec(sm_b, _ROW_BX),
        "rg_lambda": lanes_to_vec(sm_b, _ROW_LAM),
    }
    return sq[0, 0], dx0, small


def _my_index():
    return 4 * lax.axis_index("x") + 2 * lax.axis_index("y") + lax.axis_index("c")


def _peers():
    x, y, c = lax.axis_index("x"), lax.axis_index("y"), lax.axis_index("c")
    out = []
    for k in range(1, N_DEV):
        px = x ^ ((k >> 2) & 1)
        py = y ^ ((k >> 1) & 1)
        pc = c ^ (k & 1)
        out.append(((px, py, pc), 4 * px + 2 * py + pc))
    return out


GATHER_FIRST = ("w_in", "w_out")
GATHER_PLAN = {"w_in_fwd": ("up0",), "mlp_up_l0": ("down0",), "mlp_down_l0": ("w_qkv", "w_o"),
               "w_qkv_fwd": ("up1",), "mlp_up_l1": ("down1",)}
EXCHANGE_PLAN = {
    "mlp_down_dx_l1": (("down1", 0, 2),), "mlp_up_dw_l1": (("down1", 1, 2),),
    "mlp_up_dx_l1": (("up1", 0, 2),), "w_qkv_dw": (("up1", 1, 2),),
    "w_qkv_dx": (("w_o", 0, 1),),
    "mlp_down_dw_l0": (("w_qkv", 0, 2),), "mlp_down_dx_l0": (("w_qkv", 1, 2),),
    "mlp_up_dw_l0": (("down0", 0, 2),), "mlp_up_dx_l0": (("down0", 1, 2),),
    "w_in_dw": (("up0", 0, 2),), "w_in_dx": (("up0", 1, 2),),
}


def _all_gather(name, shards):
    n = len(shards)

    def body(*refs):
        srcs, dsts = refs[:n], refs[n:2 * n]
        send_sems, recv_sems, local_sems = refs[2 * n:]
        me = _my_index()
        peers = _peers()
        copies = []
        for a in range(n):
            lc = pltpu.make_async_copy(srcs[a], dsts[a].at[me], local_sems.at[a])
            lc.start()
            copies.append(lc)
        remote = []
        for a in range(n):
            for k, (pos, _) in enumerate(peers):
                cp = pltpu.make_async_remote_copy(
                    src_ref=srcs[a], dst_ref=dsts[a].at[me], send_sem=send_sems.at[a, k], recv_sem=recv_sems.at[a, k],
                    device_id=pos, device_id_type=MESH)
                cp.start()
                remote.append(cp)
        for a in range(n):
            for k, (pos, idx) in enumerate(peers):
                pltpu.make_async_remote_copy(
                    src_ref=srcs[a], dst_ref=dsts[a].at[idx], send_sem=send_sems.at[a, k], recv_sem=recv_sems.at[a, k],
                    device_id=pos, device_id_type=MESH).wait_recv()
        for cp in remote:
            cp.wait_send()
        for lc in copies:
            lc.wait()

    return pl.pallas_call(
        body, name=name,
        in_specs=[_ANY] * n, out_specs=[_ANY] * n,
        out_shape=[jax.ShapeDtypeStruct((N_DEV,) + s.shape, s.dtype) for s in shards],
        scratch_shapes=[pltpu.SemaphoreType.DMA((n, N_DEV - 1)), pltpu.SemaphoreType.DMA((n, N_DEV - 1)),
                        pltpu.SemaphoreType.DMA((n,))],
    )(*shards)


def _job_sems():
    return [pltpu.SemaphoreType.DMA((N_DEV - 1,)), pltpu.SemaphoreType.DMA((N_DEV - 1,)), pltpu.SemaphoreType.DMA((1,))]


def _gather_job(shard):
    def ctx():
        x, y, c = lax.axis_index("x"), lax.axis_index("y"), lax.axis_index("c")
        chips = [(1 - x, y), (x, 1 - y), (1 - x, 1 - y)]
        return x, y, c, chips

    def idx(px, py, pc):
        return 4 * px + 2 * py + pc

    def copy(src, out, sems, k, block, to):
        return pltpu.make_async_remote_copy(
            src_ref=out.at[block] if src is None else src, dst_ref=out.at[block], send_sem=sems[0].at[k],
            recv_sem=sems[1].at[k], device_id=to, device_id_type=MESH)

    def start(ins, outs, sems):
        x, y, c, chips = ctx()
        src, out = ins[0], outs[0]
        me = idx(x, y, c)
        pltpu.make_async_copy(src, out.at[me], sems[2].at[0]).start()
        copy(src, out, sems, 0, me, (x, y, 1 - c)).start()
        for j, (px, py) in enumerate(chips):
            copy(src, out, sems, 1 + j, me, (px, py, c)).start()

    def mid(ins, outs, sems):
        x, y, c, chips = ctx()
        out = outs[0]
        for j, (px, py) in enumerate(chips):
            copy(None, out, sems, 1 + j, idx(px, py, c), (x, y, c)).wait_recv()
            copy(None, out, sems, 4 + j, idx(px, py, c), (x, y, 1 - c)).start()

    def end(ins, outs, sems):
        x, y, c, chips = ctx()
        src, out = ins[0], outs[0]
        me = (x, y, c)
        copy(None, out, sems, 0, idx(x, y, 1 - c), me).wait_recv()
        for j, (px, py) in enumerate(chips):
            copy(None, out, sems, 4 + j, idx(px, py, 1 - c), me).wait_recv()
        for k in range(N_DEV - 1):
            copy(src, out, sems, k, idx(x, y, c), me).wait_send()
        pltpu.make_async_copy(src, out.at[idx(x, y, c)], sems[2].at[0]).wait()

    return _Job([shard], [jax.ShapeDtypeStruct((N_DEV,) + shard.shape, shard.dtype)], _job_sems(), start, mid, end)


def _exchange_job(src, land, part=0, parts=1):
    rows = src.shape[1] // parts
    assert rows * parts == src.shape[1]

    def sl(ref, s):
        return ref.at[s, pl.ds(part * rows, rows)]

    def start(ins, outs, sems):
        me = _my_index()
        pltpu.make_async_copy(sl(ins[0], me), sl(outs[0], me), sems[2].at[0]).start()
        for k, (pos, idx) in enumerate(_peers()):
            pltpu.make_async_remote_copy(
                src_ref=sl(ins[0], idx), dst_ref=sl(outs[0], me), send_sem=sems[0].at[k], recv_sem=sems[1].at[k],
                device_id=pos, device_id_type=MESH).start()

    def mid(ins, outs, sems):
        pass

    def end(ins, outs, sems):
        me = _my_index()
        for k, (pos, idx) in enumerate(_peers()):
            cp = pltpu.make_async_remote_copy(
                src_ref=sl(ins[0], idx), dst_ref=sl(outs[0], idx), send_sem=sems[0].at[k], recv_sem=sems[1].at[k],
                device_id=pos, device_id_type=MESH)
            cp.wait_recv()
            cp.wait_send()
        pltpu.make_async_copy(sl(ins[0], me), sl(outs[0], me), sems[2].at[0]).wait()

    return _Job([src, land], [jax.ShapeDtypeStruct(land.shape, land.dtype)], _job_sems(), start, mid, end, alias={1: 0})


def _adamw_math(w, g, m, v):
    m = ADAM_B1 * m + (1.0 - ADAM_B1) * g
    v = ADAM_B2 * v + (1.0 - ADAM_B2) * (g * g)
    m_hat = m / (1.0 - ADAM_B1 ** ADAM_STEP)
    v_hat = v / (1.0 - ADAM_B2 ** ADAM_STEP)
    delta = -ADAM_LR * (m_hat / (jnp.sqrt(v_hat) + ADAM_EPS) + ADAM_WD * w)
    return delta, m, v


def _sum_slots(ref):
    g = ref[0].astype(F32)
    for s in range(1, N_DEV):
        g = g + ref[s].astype(F32)
    return g


def _adamw_big(name, lands, w, m, v):
    L, R, C = w.shape
    assert len(lands) == L
    tr = _tile(R, max(LANES, (256 * 1024) // C))

    def body(*refs):
        l_refs = refs[:L]
        w_ref, m_ref, v_ref, g_ref, d_ref, nm_ref, nv_ref = refs[L:]
        for li in range(L):
            @pl.when(pl.program_id(0) == li)
            def _(li=li):
                g = _sum_slots(l_refs[li])
                d, nm, nv = _adamw_math(w_ref[...], g, m_ref[...], v_ref[...])
                g_ref[...] = g
                d_ref[...] = d
                nm_ref[...] = nm
                nv_ref[...] = nv

    def land_spec(li):
        return pl.BlockSpec((N_DEV, tr, C), lambda l, i: (0, jnp.where(l == li, i, 0), 0))

    row = pl.BlockSpec((None, tr, C), lambda l, i: (l, i, 0))
    return pl.pallas_call(
        body, name=name, grid=(L, R // tr),
        in_specs=[land_spec(li) for li in range(L)] + [row, row, row],
        out_specs=[row] * 4, out_shape=[jax.ShapeDtypeStruct((L, R, C), F32)] * 4,
        compiler_params=_params(("arbitrary", "arbitrary")))(*lands, w, m, v)


def _sum8(name, slots):
    _, R, C = slots.shape

    def body(s_ref, o_ref):
        o_ref[...] = _sum_slots(s_ref)

    return pl.pallas_call(body, name=name, out_shape=jax.ShapeDtypeStruct((R, C), F32))(slots)


def _adamw_small(name, g, w, m, v):
    def body(g_ref, w_ref, m_ref, v_ref, d_ref, nm_ref, nv_ref):
        d, nm, nv = _adamw_math(w_ref[...], g_ref[...], m_ref[...], v_ref[...])
        d_ref[...] = d
        nm_ref[...] = nm
        nv_ref[...] = nv

    return pl.pallas_call(body, name=name, out_shape=[jax.ShapeDtypeStruct(w.shape, F32)] * 3)(g, w, m, v)


def _pack_rows(arrs):
    parts, spans, r0 = [], [], 0
    for a in arrs:
        flat = a.astype(F32).reshape(-1)
        rows = -(-flat.shape[0] // LANES)
        rows = -(-rows // SUBLANES) * SUBLANES
        flat = jnp.pad(flat, (0, rows * LANES - flat.shape[0]))
        parts.append(flat.reshape(rows, LANES))
        spans.append((r0, rows, a.shape))
        r0 += rows
    return jnp.concatenate(parts, axis=0), spans


def _unpack_rows(buf, span):
    r0, rows, shape = span
    n = math.prod(shape)
    return buf[..., r0:r0 + rows, :].reshape(buf.shape[:-2] + (rows * LANES,))[..., :n].reshape(buf.shape[:-2] + shape)


def _col_blocks(w, n_blocks):
    K, N = w.shape
    return jnp.transpose(w.reshape(K, n_blocks, N // n_blocks), (1, 0, 2))


def _from_col_blocks(wb):
    B, K, n = wb.shape
    return jnp.transpose(wb, (1, 0, 2)).reshape(K, B * n)


def kernel(x, norm_gains, hyb_w_in, hyb_conv_a, hyb_conv_b, hyb_conv_b_bias, hyb_rg_w_a, hyb_rg_b_a, hyb_rg_w_x, hyb_rg_b_x, hyb_rg_lambda, hyb_w_out, sb_w_qkv, sb_w_o, mlp_w_up, mlp_w_down, loss_target, m_norm_gains, m_hyb_w_in, m_hyb_conv_a, m_hyb_conv_b, m_hyb_conv_b_bias, m_hyb_rg_w_a, m_hyb_rg_b_a, m_hyb_rg_w_x, m_hyb_rg_b_x, m_hyb_rg_lambda, m_hyb_w_out, m_sb_w_qkv, m_sb_w_o, m_mlp_w_up, m_mlp_w_down, v_norm_gains, v_hyb_w_in, v_hyb_conv_a, v_hyb_conv_b, v_hyb_conv_b_bias, v_hyb_rg_w_a, v_hyb_rg_b_a, v_hyb_rg_w_x, v_hyb_rg_b_x, v_hyb_rg_lambda, v_hyb_w_out, v_sb_w_qkv, v_sb_w_o, v_mlp_w_up, v_mlp_w_down):
    T, D = x.shape[1], x.shape[2]
    me = _my_index()

    small_shards, small_spans = _pack_rows([norm_gains, hyb_conv_a[0], hyb_conv_b[0]])
    (small_all,) = _all_gather("gather_small", [small_shards])
    gains_b = _unpack_rows(small_all, small_spans[0])
    gains = jnp.transpose(gains_b, (1, 2, 0, 3)).reshape(2, 4, D)
    conv_a = _from_col_blocks(_unpack_rows(small_all, small_spans[1]))
    conv_b = _from_col_blocks(_unpack_rows(small_all, small_spans[2]))

    shards = {"w_in": hyb_w_in[0], "w_out": hyb_w_out[0], "w_qkv": sb_w_qkv[0], "w_o": sb_w_o[0],
              "up0": mlp_w_up[0], "up1": mlp_w_up[1], "down0": mlp_w_down[0], "down1": mlp_w_down[1]}
    shards = {n: s.astype(BF16) for n, s in shards.items()}
    first = _run_jobs("gather_first", [_gather_job(shards[n]) for n in GATHER_FIRST])
    wts = _Weights({n: _gathered_layout(n, g) for n, g in zip(GATHER_FIRST, first)}, shards, GATHER_PLAN)
    grads_big = _Grads({n: lax.empty((N_DEV,) + s.shape, BF16) for n, s in shards.items()}, EXCHANGE_PLAN)

    sq, grad_x, small = _local_step(
        x[0], loss_target[0], gains, conv_a, conv_b, hyb_conv_b_bias, hyb_rg_w_a[0], hyb_rg_b_a, hyb_rg_w_x[0],
        hyb_rg_b_x, hyb_rg_lambda, wts, grads_big)
    loss = lax.psum(0.5 * sq / D, ("x", "y", "c"))

    grads_big.flush("exchange_grads")
    land = grads_big.lands

    names = ["norm_gains", "hyb_w_in", "hyb_conv_a", "hyb_conv_b", "hyb_conv_b_bias", "hyb_rg_w_a", "hyb_rg_b_a",
             "hyb_rg_w_x", "hyb_rg_b_x", "hyb_rg_lambda", "hyb_w_out", "sb_w_qkv", "sb_w_o", "mlp_w_up", "mlp_w_down"]
    params = dict(zip(names, [norm_gains, hyb_w_in, hyb_conv_a, hyb_conv_b, hyb_conv_b_bias, hyb_rg_w_a, hyb_rg_b_a,
                              hyb_rg_w_x, hyb_rg_b_x, hyb_rg_lambda, hyb_w_out, sb_w_qkv, sb_w_o, mlp_w_up, mlp_w_down]))
    moms = dict(zip(names, [m_norm_gains, m_hyb_w_in, m_hyb_conv_a, m_hyb_conv_b, m_hyb_conv_b_bias, m_hyb_rg_w_a,
                            m_hyb_rg_b_a, m_hyb_rg_w_x, m_hyb_rg_b_x, m_hyb_rg_lambda, m_hyb_w_out, m_sb_w_qkv,
                            m_sb_w_o, m_mlp_w_up, m_mlp_w_down]))
    vars_ = dict(zip(names, [v_norm_gains, v_hyb_w_in, v_hyb_conv_a, v_hyb_conv_b, v_hyb_conv_b_bias, v_hyb_rg_w_a,
                             v_hyb_rg_b_a, v_hyb_rg_w_x, v_hyb_rg_b_x, v_hyb_rg_lambda, v_hyb_w_out, v_sb_w_qkv,
                             v_sb_w_o, v_mlp_w_up, v_mlp_w_down]))
    grads, deltas, new_m, new_v = {}, {}, {}, {}

    big_lands = {"hyb_w_in": ["w_in"], "hyb_w_out": ["w_out"], "sb_w_qkv": ["w_qkv"], "sb_w_o": ["w_o"],
                 "mlp_w_up": ["up0", "up1"], "mlp_w_down": ["down0", "down1"]}
    for nm, keys in big_lands.items():
        outs = _adamw_big(f"adamw_{nm}", [land[k] for k in keys], params[nm], moms[nm], vars_[nm])
        grads[nm], deltas[nm], new_m[nm], new_v[nm] = outs

    small_names = ["norm_gains", "hyb_conv_a", "hyb_conv_b", "hyb_conv_b_bias", "hyb_rg_w_a", "hyb_rg_b_a",
                   "hyb_rg_w_x", "hyb_rg_b_x", "hyb_rg_lambda"]
    small_keys = ["norm_gains", "conv_a", "conv_b", "conv_b_bias", "rg_w_a", "rg_b_a", "rg_w_x", "rg_b_x", "rg_lambda"]
    sg_buf, sg_spans = _pack_rows([small[k] for k in small_keys])
    (sg_all,) = _all_gather("gather_small_grads", [sg_buf])
    sg_sum = _sum8("sum_small_grads", sg_all)
    full = {nm: _unpack_rows(sg_sum, sp) for nm, sp in zip(small_names, sg_spans)}
    cb = (D // 2) // N_DEV
    small_grads = {
        "norm_gains": lax.dynamic_slice_in_dim(full["norm_gains"], me * (D // N_DEV), D // N_DEV, axis=2),
        "hyb_conv_a": lax.dynamic_slice_in_dim(full["hyb_conv_a"], me * cb, cb, axis=1)[None],
        "hyb_conv_b": lax.dynamic_slice_in_dim(full["hyb_conv_b"], me * cb, cb, axis=1)[None],
        "hyb_conv_b_bias": full["hyb_conv_b_bias"],
        "hyb_rg_w_a": full["hyb_rg_w_a"][None],
        "hyb_rg_b_a": full["hyb_rg_b_a"],
        "hyb_rg_w_x": full["hyb_rg_w_x"][None],
        "hyb_rg_b_x": full["hyb_rg_b_x"],
        "hyb_rg_lambda": full["hyb_rg_lambda"],
    }
    pk = lambda d: _pack_rows([d[nm] for nm in small_names])
    g_buf, spans = pk(small_grads)
    w_buf, _ = pk(params)
    m_buf, _ = pk(moms)
    v_buf, _ = pk(vars_)
    d_buf, nm_buf, nv_buf = _adamw_small("adamw_small", g_buf, w_buf, m_buf, v_buf)
    for nm, sp in zip(small_names, spans):
        grads[nm] = small_grads[nm]
        deltas[nm], new_m[nm], new_v[nm] = _unpack_rows(d_buf, sp), _unpack_rows(nm_buf, sp), _unpack_rows(nv_buf, sp)

    return (loss, grad_x[None], *[grads[n] for n in names], *[deltas[n] for n in names],
            *[new_m[n] for n in names], *[new_v[n] for n in names])
```

```python
import functools
import math

import jax
import jax.numpy as jnp
from jax import lax
from jax.experimental import pallas as pl
from jax.experimental.pallas import tpu as pltpu

F32 = jnp.float32
BF16 = jnp.bfloat16

NORM_EPS = 1e-6
LRU_C = 8.0
ATT_HEAD_DIM = 128
RG_HEAD_DIM = 64
LANES = 128
SUBLANES = 8
N_DEV = 8
ADAM_LR = 0.001
ADAM_B1 = 0.9
ADAM_B2 = 0.999
ADAM_EPS = 1e-08
ADAM_WD = 0.01
ADAM_STEP = 10
VMEM_LIMIT = 56 * 1024 * 1024
MM_TK = 2048
MESH = pl.DeviceIdType.MESH


def _tile(n, pref):
    if n <= pref:
        return n
    t = (pref // LANES) * LANES
    while t > LANES and n % t:
        t -= LANES
    assert n % t == 0, (n, pref)
    return t


def _params(sem):
    return pltpu.CompilerParams(dimension_semantics=sem, vmem_limit_bytes=VMEM_LIMIT)


DIMS_NN = (((1,), (0,)), ((), ()))
DIMS_NT = (((1,), (1,)), ((), ()))
DIMS_TN = (((0,), (0,)), ((), ()))


_ANY = pl.BlockSpec(memory_space=pl.ANY)


class _Job:
    def __init__(self, ins, outs, sems, start, mid, end, alias=None):
        self.ins, self.outs, self.sems = ins, outs, sems
        self.start, self.mid, self.end = start, mid, end
        self.alias = alias or {}


def _job_plumbing(jobs, n_in, n_out):
    j_ins = [a for jb in jobs for a in jb.ins]
    j_outs = [o for jb in jobs for o in jb.outs]
    j_sems = [s for jb in jobs for s in jb.sems]
    aliases, pi, po = {}, 0, 0
    for jb in jobs:
        for i_in, i_out in jb.alias.items():
            aliases[n_in + pi + i_in] = n_out + po + i_out
        pi += len(jb.ins)
        po += len(jb.outs)
    return j_ins, j_outs, j_sems, aliases


def _job_phase(jobs, which, jin, jout, jsem):
    pi = po = ps = 0
    for jb in jobs:
        getattr(jb, which)(jin[pi:pi + len(jb.ins)], jout[po:po + len(jb.outs)], jsem[ps:ps + len(jb.sems)])
        pi, po, ps = pi + len(jb.ins), po + len(jb.outs), ps + len(jb.sems)


def _run_jobs(name, jobs):
    j_ins, j_outs, j_sems, aliases = _job_plumbing(jobs, 0, 0)
    n_ji, n_jo = len(j_ins), len(j_outs)

    def body(*refs):
        jin, jout, jsem = refs[:n_ji], refs[n_ji:n_ji + n_jo], refs[n_ji + n_jo:]
        for which in ("start", "mid", "end"):
            _job_phase(jobs, which, jin, jout, jsem)

    return pl.pallas_call(body, name=name, in_specs=[_ANY] * n_ji, out_specs=[_ANY] * n_jo, out_shape=j_outs,
                          scratch_shapes=j_sems, input_output_aliases=aliases)(*j_ins)


def _matmul(name, ins, outs, grid, dims, acc_shape, epilogue=None, jobs=()):
    n_in, n_out, nk = len(ins), len(outs), grid[2]
    j_ins, j_outs, j_sems, aliases = _job_plumbing(jobs, n_in, n_out)
    n_ji, n_jo = len(j_ins), len(j_outs)
    total = grid[0] * grid[1] * grid[2]
    n_acc = 0 if nk == 1 else 1

    def body(*refs):
        a_ref, b_ref = refs[0], refs[1]
        extras = refs[2:n_in]
        jin = refs[n_in:n_in + n_ji]
        out_refs = refs[n_in + n_ji:n_in + n_ji + n_out]
        jout = refs[n_in + n_ji + n_out:n_in + n_ji + n_out + n_jo]
        jsem = refs[n_in + n_ji + n_out + n_jo + n_acc:]
        k = pl.program_id(2)
        step = (pl.program_id(0) * grid[1] + pl.program_id(1)) * grid[2] + k
        if jobs:
            pl.when(step == 0)(lambda: _job_phase(jobs, "start", jin, jout, jsem))
            pl.when(step == (4 * total) // 5)(lambda: _job_phase(jobs, "mid", jin, jout, jsem))

        def finish(r):
            res = epilogue(r, *[e[...] for e in extras]) if epilogue is not None else (r,)
            for o, v in zip(out_refs, res):
                o[...] = v.astype(o.dtype)

        prod = lax.dot_general(a_ref[...], b_ref[...], dims, preferred_element_type=F32)
        if nk == 1:
            finish(prod)
        else:
            acc = refs[n_in + n_ji + n_out + n_jo]

            @pl.when(k == 0)
            def _():
                acc[...] = prod

            @pl.when((k > 0) & (k < nk - 1))
            def _():
                acc[...] += prod

            @pl.when(k == nk - 1)
            def _():
                finish(acc[...] + prod)

        if jobs:
            pl.when(step == total - 1)(lambda: _job_phase(jobs, "end", jin, jout, jsem))

    sem = ("arbitrary",) * 3 if jobs else ("parallel", "parallel", "arbitrary")
    res = pl.pallas_call(
        body, name=name, grid=grid,
        in_specs=[s for _, s in ins] + [_ANY] * n_ji,
        out_specs=[s for _, s in outs] + [_ANY] * n_jo,
        out_shape=[s for s, _ in outs] + j_outs,
        scratch_shapes=[pltpu.VMEM(acc_shape, F32)] * n_acc + j_sems,
        input_output_aliases=aliases,
        compiler_params=_params(sem),
    )(*[a for a, _ in ins], *j_ins)
    return res


def _mm_nn(name, a, b, out_dtype, *, a_seg=None, out_seg=None, tm=1024, tn=1024, tk=MM_TK, epilogue=None,
           extras=(), n_out=1, out_dtypes=None, jobs=()):
    if a_seg:
        _, M, ks = a.shape
        K = ks * a_seg
    else:
        M, K = a.shape
        ks = K
    N = b.shape[1]
    ns = N // out_seg if out_seg else N
    tm, tn, tk = _tile(M, tm), _tile(ns, tn), _tile(ks, tk)
    nks, nns = ks // tk, ns // tn
    grid = (M // tm, N // tn, K // tk)
    if a_seg:
        a_spec = pl.BlockSpec((None, tm, tk), lambda i, j, k: (k // nks, i, k % nks))
    else:
        a_spec = pl.BlockSpec((tm, tk), lambda i, j, k: (i, k))
    b_spec = pl.BlockSpec((tk, tn), lambda i, j, k: (k, j))
    if out_seg:
        o_spec = pl.BlockSpec((None, tm, tn), lambda i, j, k: (j // nns, i, j % nns))
        o_shape = (out_seg, M, ns)
    else:
        o_spec = pl.BlockSpec((tm, tn), lambda i, j, k: (i, j))
        o_shape = (M, N)
    dts = out_dtypes or [out_dtype] * n_out
    outs = [(jax.ShapeDtypeStruct(o_shape, dt), o_spec) for dt in dts]
    ins = [(a, a_spec), (b, b_spec)] + [(e, o_spec) for e in extras]
    return _matmul(name, ins, outs, grid, DIMS_NN, (tm, tn), epilogue, jobs)


def _mm_nt(name, a, b, out_dtype, *, a_seg=None, out_seg=None, tm=1024, tn=1024, tk=MM_TK, epilogue=None, extras=(),
           jobs=()):
    if a_seg:
        _, M, ks = a.shape
        K = ks * a_seg
    else:
        M, K = a.shape
        ks = K
    N = b.shape[0]
    ns = N // out_seg if out_seg else N
    tm, tn, tk = _tile(M, tm), _tile(ns, tn), _tile(ks, tk)
    nks, nns = ks // tk, ns // tn
    grid = (M // tm, N // tn, K // tk)
    if a_seg:
        a_spec = pl.BlockSpec((None, tm, tk), lambda i, j, k: (k // nks, i, k % nks))
    else:
        a_spec = pl.BlockSpec((tm, tk), lambda i, j, k: (i, k))
    b_spec = pl.BlockSpec((tn, tk), lambda i, j, k: (j, k))
    if out_seg:
        o_spec = pl.BlockSpec((None, tm, tn), lambda i, j, k: (j // nns, i, j % nns))
        o_shape = (out_seg, M, ns)
    else:
        o_spec = pl.BlockSpec((tm, tn), lambda i, j, k: (i, j))
        o_shape = (M, N)
    outs = [(jax.ShapeDtypeStruct(o_shape, out_dtype), o_spec)]
    ins = [(a, a_spec), (b, b_spec)] + [(e, o_spec) for e in extras]
    return _matmul(name, ins, outs, grid, DIMS_NT, (tm, tn), epilogue, jobs)


def _mm_tn(name, a, b, out_dtype, *, a_seg=None, b_seg=None, out_blocks=None, tm=1024, tn=1024, tk=MM_TK, jobs=()):
    if a_seg:
        _, T, ms = a.shape
        M = ms * a_seg
    else:
        T, M = a.shape
        ms = M
    if b_seg:
        _, _, ns = b.shape
        N = ns * b_seg
    else:
        N = b.shape[1]
        ns = N
    nb_cols = N // out_blocks if out_blocks else N
    tm, tk = _tile(ms, tm), _tile(T, tk)
    tn = _tile(math.gcd(ns, nb_cols), tn)
    nms, nns, nbs = ms // tm, ns // tn, nb_cols // tn
    grid = (M // tm, N // tn, T // tk)
    if a_seg:
        a_spec = pl.BlockSpec((None, tk, tm), lambda i, j, k: (i // nms, k, i % nms))
    else:
        a_spec = pl.BlockSpec((tk, tm), lambda i, j, k: (k, i))
    if b_seg:
        b_spec = pl.BlockSpec((None, tk, tn), lambda i, j, k: (j // nns, k, j % nns))
    else:
        b_spec = pl.BlockSpec((tk, tn), lambda i, j, k: (k, j))
    if out_blocks:
        o_spec = pl.BlockSpec((None, tm, tn), lambda i, j, k: (j // nbs, i, j % nbs))
        o_shape = (out_blocks, M, nb_cols)
    else:
        o_spec = pl.BlockSpec((tm, tn), lambda i, j, k: (i, j))
        o_shape = (M, N)
    outs = [(jax.ShapeDtypeStruct(o_shape, out_dtype), o_spec)]
    return _matmul(name, [(a, a_spec), (b, b_spec)], outs, grid, DIMS_TN, (tm, tn), None, jobs)


def _rms(x):
    return lax.rsqrt(jnp.mean(x * x, axis=-1, keepdims=True) + NORM_EPS)


def _row_tile(T):
    return _tile(T, 256)


def _norm_fwd(name, x, g):
    T, D = x.shape
    tr = _row_tile(T)

    def body(x_ref, g_ref, h_ref):
        xv = x_ref[...]
        h_ref[...] = (xv * _rms(xv) * g_ref[...]).astype(h_ref.dtype)

    row = pl.BlockSpec((tr, D), lambda i: (i, 0))
    vec = pl.BlockSpec((1, D), lambda i: (0, 0))
    return pl.pallas_call(body, name=name, grid=(T // tr,), in_specs=[row, vec], out_specs=row,
                          out_shape=jax.ShapeDtypeStruct((T, D), BF16), compiler_params=_params(("parallel",)))(x, g)


def _resid_norm(name, x, br, g_post, g_next):
    T, D = x.shape
    tr = _row_tile(T)

    def body(x_ref, br_ref, gp_ref, gn_ref, xn_ref, h_ref):
        b = br_ref[...]
        xn = x_ref[...] + b * _rms(b) * gp_ref[...]
        xn_ref[...] = xn
        h_ref[...] = (xn * _rms(xn) * gn_ref[...]).astype(h_ref.dtype)

    row = pl.BlockSpec((tr, D), lambda i: (i, 0))
    vec = pl.BlockSpec((1, D), lambda i: (0, 0))
    return pl.pallas_call(body, name=name, grid=(T // tr,), in_specs=[row, row, vec, vec], out_specs=[row, row],
                          out_shape=[jax.ShapeDtypeStruct((T, D), F32), jax.ShapeDtypeStruct((T, D), BF16)],
                          compiler_params=_params(("parallel",)))(x, br, g_post, g_next)


def _final_loss(name, x, br, g_post, target):
    T, D = x.shape
    tr = _row_tile(T)

    def body(x_ref, br_ref, gp_ref, t_ref, dy_ref, ls_ref):
        b = br_ref[...]
        err = x_ref[...] + b * _rms(b) * gp_ref[...] - t_ref[...]
        dy_ref[...] = err * (1.0 / D)

        @pl.when(pl.program_id(0) == 0)
        def _():
            ls_ref[...] = jnp.zeros_like(ls_ref)

        ls_ref[...] += jnp.sum(err * err)

    row = pl.BlockSpec((tr, D), lambda i: (i, 0))
    vec = pl.BlockSpec((1, D), lambda i: (0, 0))
    acc = pl.BlockSpec((SUBLANES, LANES), lambda i: (0, 0))
    return pl.pallas_call(body, name=name, grid=(T // tr,), in_specs=[row, row, vec, row], out_specs=[row, acc],
                          out_shape=[jax.ShapeDtypeStruct((T, D), F32), jax.ShapeDtypeStruct((SUBLANES, LANES), F32)],
                          compiler_params=_params(("arbitrary",)))(x, br, g_post, target)


def _norm_bwd(name, x, g, dy, add, out_dtype):
    T, D = x.shape
    tr = _row_tile(T)
    has_add = add is not None

    def body(*refs):
        if has_add:
            x_ref, g_ref, dy_ref, add_ref, dx_ref, dg_ref = refs
        else:
            x_ref, g_ref, dy_ref, dx_ref, dg_ref = refs
        xv = x_ref[...]
        r = _rms(xv)
        xhat = xv * r
        dyv = dy_ref[...].astype(F32)
        gdy = dyv * g_ref[...]
        dx = r * (gdy - xhat * jnp.mean(gdy * xhat, axis=-1, keepdims=True))
        if has_add:
            dx = dx + add_ref[...]
        dx_ref[...] = dx.astype(dx_ref.dtype)

        @pl.when(pl.program_id(0) == 0)
        def _():
            dg_ref[...] = jnp.zeros_like(dg_ref)

        dg_ref[...] += jnp.sum(dyv * xhat, axis=0, keepdims=True)

    row = pl.BlockSpec((tr, D), lambda i: (i, 0))
    vec = pl.BlockSpec((1, D), lambda i: (0, 0))
    ins = [x, g, dy] + ([add] if has_add else [])
    specs = [row, vec, row] + ([row] if has_add else [])
    return pl.pallas_call(body, name=name, grid=(T // tr,), in_specs=specs, out_specs=[row, vec],
                          out_shape=[jax.ShapeDtypeStruct((T, D), out_dtype), jax.ShapeDtypeStruct((1, D), F32)],
                          compiler_params=_params(("arbitrary",)))(*ins)


HALO = SUBLANES
TIME_CHUNK = 512


def _chunks(T):
    tc = min(TIME_CHUNK, T)
    assert T % tc == 0 and tc % SUBLANES == 0
    return [(t0, tc) for t0 in range(0, T, tc)]


def _log_sigmoid(x):
    return -(jnp.maximum(-x, 0.0) + jnp.log(1.0 + jnp.exp(-jnp.abs(x))))


def _one_minus_exp(x):
    series = -x * (1.0 + x * (0.5 + x * (1.0 / 6.0 + x * (1.0 / 24.0))))
    return jnp.where(x > -0.01, series, 1.0 - jnp.exp(x))


_GELU_C = math.sqrt(2.0 / math.pi)


def _gelu(x):
    return 0.5 * x * (1.0 + jnp.tanh(_GELU_C * (x + 0.044715 * x * x * x)))


def _gelu_grad(x):
    th = jnp.tanh(_GELU_C * (x + 0.044715 * x * x * x))
    return 0.5 * (1.0 + th) + 0.5 * x * (1.0 - th * th) * _GELU_C * (1.0 + 3.0 * 0.044715 * x * x)


def _tile_scan(a, b, reverse):
    rows = a.shape[0]
    pos = lax.broadcasted_iota(jnp.int32, a.shape, 0) & (SUBLANES - 1)
    for d in (1, 2, 4):
        if reverse:
            ok = pos < SUBLANES - d
            shift = rows - d
        else:
            ok = pos >= d
            shift = d
        a_sh = jnp.where(ok, pltpu.roll(a, shift, 0), 1.0)
        b_sh = jnp.where(ok, pltpu.roll(b, shift, 0), 0.0)
        b = a * b_sh + b
        a = a * a_sh
    return a, b


def _carry_scan(a_s, b_s, T, reverse):
    n = T // SUBLANES
    edge = 0 if reverse else SUBLANES - 1

    def step(j, carry):
        g = (n - 1 - j) if reverse else j
        r = pl.multiple_of(g * SUBLANES, SUBLANES)
        h = b_s[pl.ds(r, SUBLANES), :] + a_s[pl.ds(r, SUBLANES), :] * carry
        b_s[pl.ds(r, SUBLANES), :] = h
        return jnp.broadcast_to(h[edge:edge + 1, :], h.shape)

    lax.fori_loop(0, n, step, jnp.zeros((SUBLANES, a_s.shape[1]), F32))


def _seg_spec(T, seg, nblk):
    return pl.BlockSpec((None, T, LANES), lambda c: (seg, 0, c))


def _rows_to_tile(rows):
    idx = lax.broadcasted_iota(jnp.int32, (SUBLANES, LANES), 0)
    out = jnp.zeros((SUBLANES, LANES), F32)
    for k, r in enumerate(rows):
        out = jnp.where(idx == k, r, out)
    return out


def _mixer_a_fwd(proj, conv_a):
    _, T, C = proj.shape
    nblk = C // LANES
    chunks = _chunks(T)

    def body(bg_ref, cg_ref, ax_ref, w_ref, y_ref, p_s):
        p_s[pl.ds(0, HALO), :] = jnp.zeros((HALO, LANES), F32)
        for t0, tc in chunks:
            p_s[pl.ds(HALO + t0, tc), :] = cg_ref[pl.ds(t0, tc), :] * ax_ref[pl.ds(t0, tc), :]
        w = w_ref[...]
        for t0, tc in chunks:
            c = (w[2:3, :] * p_s[pl.ds(HALO + t0, tc), :] + w[1:2, :] * p_s[pl.ds(HALO + t0 - 1, tc), :]
                 + w[0:1, :] * p_s[pl.ds(HALO + t0 - 2, tc), :])
            y_ref[pl.ds(t0, tc), :] = (bg_ref[pl.ds(t0, tc), :] * c).astype(y_ref.dtype)

    return pl.pallas_call(
        body, name="mixer_a_fwd", grid=(nblk,),
        in_specs=[_seg_spec(T, 0, nblk), _seg_spec(T, 1, nblk), _seg_spec(T, 2, nblk),
                  pl.BlockSpec((3, LANES), lambda c: (0, c))],
        out_specs=pl.BlockSpec((T, LANES), lambda c: (0, c)),
        out_shape=jax.ShapeDtypeStruct((T, C), BF16),
        scratch_shapes=[pltpu.VMEM((T + HALO, LANES), F32)],
        compiler_params=_params(("parallel",)))(proj, proj, proj, conv_a)


def _mixer_a_bwd(proj, conv_a, dy):
    _, T, C = proj.shape
    nblk = C // LANES
    chunks = _chunks(T)

    def body(bg_ref, cg_ref, ax_ref, w_ref, dy_ref, dp_ref, dw_ref, p_s, dc_s):
        p_s[pl.ds(0, HALO), :] = jnp.zeros((HALO, LANES), F32)
        dc_s[pl.ds(T, HALO), :] = jnp.zeros((HALO, LANES), F32)
        for t0, tc in chunks:
            p_s[pl.ds(HALO + t0, tc), :] = cg_ref[pl.ds(t0, tc), :] * ax_ref[pl.ds(t0, tc), :]
        w = w_ref[...]
        for t0, tc in chunks:
            c = (w[2:3, :] * p_s[pl.ds(HALO + t0, tc), :] + w[1:2, :] * p_s[pl.ds(HALO + t0 - 1, tc), :]
                 + w[0:1, :] * p_s[pl.ds(HALO + t0 - 2, tc), :])
            dyv = dy_ref[pl.ds(t0, tc), :]
            dp_ref[0, pl.ds(t0, tc), :] = (dyv * c).astype(dp_ref.dtype)
            dc_s[pl.ds(t0, tc), :] = dyv * bg_ref[pl.ds(t0, tc), :]
        dw = [jnp.zeros((1, LANES), F32) for _ in range(3)]
        for t0, tc in chunks:
            dc = dc_s[pl.ds(t0, tc), :]
            dpv = w[2:3, :] * dc + w[1:2, :] * dc_s[pl.ds(t0 + 1, tc), :] + w[0:1, :] * dc_s[pl.ds(t0 + 2, tc), :]
            dp_ref[1, pl.ds(t0, tc), :] = (dpv * ax_ref[pl.ds(t0, tc), :]).astype(dp_ref.dtype)
            dp_ref[2, pl.ds(t0, tc), :] = (dpv * cg_ref[pl.ds(t0, tc), :]).astype(dp_ref.dtype)
            for k in range(3):
                dw[k] = dw[k] + jnp.sum(dc * p_s[pl.ds(HALO + t0 - (2 - k), tc), :], axis=0, keepdims=True)
        dw_ref[...] = _rows_to_tile(dw)

    return pl.pallas_call(
        body, name="mixer_a_bwd", grid=(nblk,),
        in_specs=[_seg_spec(T, 0, nblk), _seg_spec(T, 1, nblk), _seg_spec(T, 2, nblk),
                  pl.BlockSpec((3, LANES), lambda c: (0, c)), _seg_spec(T, 0, nblk)],
        out_specs=[pl.BlockSpec((3, T, LANES), lambda c: (0, 0, c)),
                   pl.BlockSpec((None, SUBLANES, LANES), lambda c: (c, 0, 0))],
        out_shape=[jax.ShapeDtypeStruct((3, T, C), BF16), jax.ShapeDtypeStruct((nblk, SUBLANES, LANES), F32)],
        scratch_shapes=[pltpu.VMEM((T + HALO, LANES), F32), pltpu.VMEM((T + HALO, LANES), F32)],
        compiler_params=_params(("parallel",)))(proj, proj, proj, conv_a, dy)


def _rg_gates(xr, wa, ba, wx, bx, ls):
    xb = xr.astype(BF16)
    r = jax.nn.sigmoid(jnp.dot(xb, wa, preferred_element_type=F32) + ba)
    i = jax.nn.sigmoid(jnp.dot(xb, wx, preferred_element_type=F32) + bx)
    log_a = LRU_C * r * ls
    a = jnp.exp(log_a)
    mult = jnp.sqrt(_one_minus_exp(2.0 * log_a))
    return r, i, a, mult


def _conv4(xh_s, cw, bias, t0, tc):
    return (cw[3:4, :] * xh_s[pl.ds(HALO + t0, tc), :] + cw[2:3, :] * xh_s[pl.ds(HALO + t0 - 1, tc), :]
            + cw[1:2, :] * xh_s[pl.ds(HALO + t0 - 2, tc), :] + cw[0:1, :] * xh_s[pl.ds(HALO + t0 - 3, tc), :] + bias)


def _mixer_b_specs(T, nblk):
    vec = pl.BlockSpec((1, LANES), lambda c: (0, c))
    mat = pl.BlockSpec((None, LANES, LANES), lambda c: (c, 0, 0))
    return [_seg_spec(T, 3, nblk), _seg_spec(T, 4, nblk), pl.BlockSpec((4, LANES), lambda c: (0, c)),
            vec, mat, vec, mat, vec, vec]


def _mixer_b_fwd(proj, conv_b, bias, wa, ba, wx, bx, lam):
    _, T, C = proj.shape
    nblk = C // LANES
    chunks = _chunks(T)

    def body(gate_ref, x_ref, cw_ref, cb_ref, wa_ref, ba_ref, wx_ref, bx_ref, lam_ref, y_ref, xh_s, a_s, b_s):
        xh_s[pl.ds(0, HALO), :] = jnp.zeros((HALO, LANES), F32)
        for t0, tc in chunks:
            xh_s[pl.ds(HALO + t0, tc), :] = x_ref[pl.ds(t0, tc), :]
        cw, bias_v = cw_ref[...], cb_ref[...]
        ls = _log_sigmoid(lam_ref[...])
        for t0, tc in chunks:
            xr = _conv4(xh_s, cw, bias_v, t0, tc)
            r, i, a, mult = _rg_gates(xr, wa_ref[...], ba_ref[...], wx_ref[...], bx_ref[...], ls)
            ac, hc = _tile_scan(a, mult * i * xr, reverse=False)
            a_s[pl.ds(t0, tc), :] = ac
            b_s[pl.ds(t0, tc), :] = hc
        _carry_scan(a_s, b_s, T, reverse=False)
        for t0, tc in chunks:
            y_ref[pl.ds(t0, tc), :] = (b_s[pl.ds(t0, tc), :] * _gelu(gate_ref[pl.ds(t0, tc), :])).astype(y_ref.dtype)

    return pl.pallas_call(
        body, name="mixer_b_fwd", grid=(nblk,), in_specs=_mixer_b_specs(T, nblk),
        out_specs=pl.BlockSpec((T, LANES), lambda c: (0, c)),
        out_shape=jax.ShapeDtypeStruct((T, C), BF16),
        scratch_shapes=[pltpu.VMEM((T + HALO, LANES), F32), pltpu.VMEM((T, LANES), F32), pltpu.VMEM((T, LANES), F32)],
        compiler_params=_params(("parallel",)))(proj, proj, conv_b, bias, wa, ba, wx, bx, lam)


_ROW_CONV, _ROW_BIAS, _ROW_BA, _ROW_BX, _ROW_LAM = 0, 4, 5, 6, 7


def _mixer_b_bwd(proj, conv_b, bias, wa, ba, wx, bx, lam, dy):
    _, T, C = proj.shape
    nblk = C // LANES
    chunks = _chunks(T)

    def body(gate_ref, x_ref, cw_ref, cb_ref, wa_ref, ba_ref, wx_ref, bx_ref, lam_ref, dy_ref,
             dp_ref, sm_ref, dwa_ref, dwx_ref, xh_s, xr_s, r_s, i_s, a_s, h_s, sa_s, sb_s, dx_s):
        zero_halo = jnp.zeros((HALO, LANES), F32)
        xh_s[pl.ds(0, HALO), :] = zero_halo
        h_s[pl.ds(0, HALO), :] = zero_halo
        a_s[pl.ds(T, HALO), :] = zero_halo
        dx_s[pl.ds(T, HALO), :] = zero_halo
        for t0, tc in chunks:
            xh_s[pl.ds(HALO + t0, tc), :] = x_ref[pl.ds(t0, tc), :]
        cw, bias_v = cw_ref[...], cb_ref[...]
        lam_v = lam_ref[...]
        ls = _log_sigmoid(lam_v)
        wa_v, wx_v, ba_v, bx_v = wa_ref[...], wx_ref[...], ba_ref[...], bx_ref[...]
        for t0, tc in chunks:
            xr = _conv4(xh_s, cw, bias_v, t0, tc)
            r, i, a, mult = _rg_gates(xr, wa_v, ba_v, wx_v, bx_v, ls)
            xr_s[pl.ds(t0, tc), :] = xr
            r_s[pl.ds(t0, tc), :] = r
            i_s[pl.ds(t0, tc), :] = i
            a_s[pl.ds(t0, tc), :] = a
            ac, hc = _tile_scan(a, mult * i * xr, reverse=False)
            sa_s[pl.ds(t0, tc), :] = ac
            sb_s[pl.ds(t0, tc), :] = hc
        _carry_scan(sa_s, sb_s, T, reverse=False)
        for t0, tc in chunks:
            h_s[pl.ds(HALO + t0, tc), :] = sb_s[pl.ds(t0, tc), :]
        for t0, tc in chunks:
            gv = gate_ref[pl.ds(t0, tc), :]
            dyv = dy_ref[pl.ds(t0, tc), :]
            dp_ref[0, pl.ds(t0, tc), :] = (dyv * h_s[pl.ds(HALO + t0, tc), :] * _gelu_grad(gv)).astype(dp_ref.dtype)
            ac, gc = _tile_scan(a_s[pl.ds(t0 + 1, tc), :], dyv * _gelu(gv), reverse=True)
            sa_s[pl.ds(t0, tc), :] = ac
            sb_s[pl.ds(t0, tc), :] = gc
        _carry_scan(sa_s, sb_s, T, reverse=True)
        acc = {k: jnp.zeros((1, LANES), F32) for k in ("bias", "ba", "bx", "lam")}
        dwa = jnp.zeros((LANES, LANES), F32)
        dwx = jnp.zeros((LANES, LANES), F32)
        for t0, tc in chunks:
            dht = sb_s[pl.ds(t0, tc), :]
            xr, r, i, a = xr_s[pl.ds(t0, tc), :], r_s[pl.ds(t0, tc), :], i_s[pl.ds(t0, tc), :], a_s[pl.ds(t0, tc), :]
            mult = jnp.sqrt(_one_minus_exp(2.0 * LRU_C * r * ls))
            da = dht * h_s[pl.ds(HALO + t0 - 1, tc), :]
            dmult = dht * i * xr
            di = dht * mult * xr
            dlog_a = da * a - dmult * a * a / mult
            dpa = dlog_a * (LRU_C * ls) * r * (1.0 - r)
            dpx = di * i * (1.0 - i)
            acc["lam"] = acc["lam"] + jnp.sum(dlog_a * r, axis=0, keepdims=True)
            acc["ba"] = acc["ba"] + jnp.sum(dpa, axis=0, keepdims=True)
            acc["bx"] = acc["bx"] + jnp.sum(dpx, axis=0, keepdims=True)
            xb, dpab, dpxb = xr.astype(BF16), dpa.astype(BF16), dpx.astype(BF16)
            dwa = dwa + lax.dot_general(xb, dpab, DIMS_TN, preferred_element_type=F32)
            dwx = dwx + lax.dot_general(xb, dpxb, DIMS_TN, preferred_element_type=F32)
            dxr = (dht * mult * i + lax.dot_general(dpab, wa_v, DIMS_NT, preferred_element_type=F32)
                   + lax.dot_general(dpxb, wx_v, DIMS_NT, preferred_element_type=F32))
            acc["bias"] = acc["bias"] + jnp.sum(dxr, axis=0, keepdims=True)
            dx_s[pl.ds(t0, tc), :] = dxr
        dcw = [jnp.zeros((1, LANES), F32) for _ in range(4)]
        for t0, tc in chunks:
            dxr = dx_s[pl.ds(t0, tc), :]
            dxin = (cw[3:4, :] * dxr + cw[2:3, :] * dx_s[pl.ds(t0 + 1, tc), :] + cw[1:2, :] * dx_s[pl.ds(t0 + 2, tc), :]
                    + cw[0:1, :] * dx_s[pl.ds(t0 + 3, tc), :])
            dp_ref[1, pl.ds(t0, tc), :] = dxin.astype(dp_ref.dtype)
            for k in range(4):
                dcw[k] = dcw[k] + jnp.sum(dxr * xh_s[pl.ds(HALO + t0 - (3 - k), tc), :], axis=0, keepdims=True)
        dlam = acc["lam"] * LRU_C * jax.nn.sigmoid(-lam_v)
        sm_ref[...] = _rows_to_tile(dcw + [acc["bias"], acc["ba"], acc["bx"], dlam])
        dwa_ref[...] = dwa
        dwx_ref[...] = dwx

    big = lambda halo: pltpu.VMEM((T + halo, LANES), F32)
    mat = pl.BlockSpec((None, LANES, LANES), lambda c: (c, 0, 0))
    return pl.pallas_call(
        body, name="mixer_b_bwd", grid=(nblk,),
        in_specs=_mixer_b_specs(T, nblk) + [_seg_spec(T, 1, nblk)],
        out_specs=[pl.BlockSpec((2, T, LANES), lambda c: (0, 0, c)),
                   pl.BlockSpec((None, SUBLANES, LANES), lambda c: (c, 0, 0)), mat, mat],
        out_shape=[jax.ShapeDtypeStruct((2, T, C), BF16), jax.ShapeDtypeStruct((nblk, SUBLANES, LANES), F32),
                   jax.ShapeDtypeStruct((nblk, LANES, LANES), F32), jax.ShapeDtypeStruct((nblk, LANES, LANES), F32)],
        scratch_shapes=[big(HALO), big(0), big(0), big(0), big(HALO), big(HALO), big(0), big(0), big(HALO)],
        compiler_params=_params(("parallel",)))(proj, proj, conv_b, bias, wa, ba, wx, bx, lam, dy)


ATT_BLOCK = 128
ATT_GROUP = 4
ATT_TILE = ATT_BLOCK * ATT_GROUP
ATT_UNDERFLOW = -110.0
ATT_UNVISITED = -1e30


def _split_dot(x, m):
    hi = x.astype(BF16)
    lo = (x - hi.astype(F32)).astype(BF16)
    return jnp.dot(hi, m, preferred_element_type=F32) + jnp.dot(lo, m, preferred_element_type=F32)


def _sub(x, j):
    return x[:, j * ATT_BLOCK:(j + 1) * ATT_BLOCK]


def _stack_rows(x):
    return jnp.concatenate([_sub(x, j) for j in range(ATT_GROUP)], axis=0)


def _unstack_rows(x, offsets):
    return jnp.concatenate([x[j * ATT_BLOCK:(j + 1) * ATT_BLOCK, :] + offsets[j] for j in range(ATT_GROUP)], axis=1)


def _att_tile(q, k_ref, q0, qb, it, scale):
    hi = (qb + 1 - ATT_GROUP * it) * ATT_BLOCK
    k0 = pl.multiple_of(jnp.maximum(hi - ATT_TILE, 0), ATT_BLOCK)
    kt = k_ref[pl.ds(k0, ATT_TILE), :]
    z = lax.dot_general(q, kt, DIMS_NT, preferred_element_type=F32) * scale
    key = k0 + lax.broadcasted_iota(jnp.int32, z.shape, 1)
    row = q0 + lax.broadcasted_iota(jnp.int32, z.shape, 0)
    mask = (key < row) & (key < hi)
    n = jnp.where(mask, -(jnp.maximum(z, 0.0) + jnp.log(1.0 + jnp.exp(-jnp.abs(z)))), 0.0)
    return k0, kt, z, mask, n


def _suffix_in_tile(n, upper, run):
    rs = [jnp.sum(_sub(n, j), axis=1, keepdims=True) for j in range(ATT_GROUP)]
    offs = [None] * ATT_GROUP
    offs[ATT_GROUP - 1] = run
    for j in range(ATT_GROUP - 2, -1, -1):
        offs[j] = offs[j + 1] + rs[j + 1]
    return _unstack_rows(_split_dot(_stack_rows(n), upper), offs), offs[0] + rs[0]


def _head_spec(T, seg, heads):
    return pl.BlockSpec((None, T, ATT_HEAD_DIM), lambda h: (seg, 0, h))


def _attention_fwd(qkv):
    _, T, D = qkv.shape
    heads = D // ATT_HEAD_DIM
    nq = T // ATT_BLOCK
    assert nq <= LANES
    scale = 1.0 / math.sqrt(ATT_HEAD_DIM)

    def body(q_ref, k_ref, v_ref, o_ref, r_ref, acc_s, run_s):
        rr = lax.broadcasted_iota(jnp.int32, (ATT_BLOCK, ATT_BLOCK), 0)
        cc = lax.broadcasted_iota(jnp.int32, (ATT_BLOCK, ATT_BLOCK), 1)
        upper = jnp.where(rr > cc, 1.0, 0.0).astype(BF16)
        lane = lax.broadcasted_iota(jnp.int32, (ATT_BLOCK, LANES), 1)

        def q_loop(qb, _):
            q0 = pl.multiple_of(qb * ATT_BLOCK, ATT_BLOCK)
            q = q_ref[pl.ds(q0, ATT_BLOCK), :]
            acc_s[...] = jnp.zeros_like(acc_s)
            run_s[...] = jnp.zeros_like(run_s)
            r_ref[pl.ds(q0, ATT_BLOCK), :] = jnp.full((ATT_BLOCK, LANES), ATT_UNVISITED, F32)
            n_tiles = (qb + ATT_GROUP) // ATT_GROUP

            def tile(carry):
                it, _ = carry
                k0, _, z, mask, n = _att_tile(q, k_ref, q0, qb, it, scale)
                run = run_s[...]
                suffix, run_next = _suffix_in_tile(n, upper, run)
                w = jnp.where(mask, jnp.exp(z + n + suffix), 0.0)
                acc_s[...] += jnp.dot(w.astype(BF16), v_ref[pl.ds(k0, ATT_TILE), :], preferred_element_type=F32)
                r_ref[pl.ds(q0, ATT_BLOCK), :] = jnp.where(lane == it, run, r_ref[pl.ds(q0, ATT_BLOCK), :])
                run_s[...] = run_next
                return it + 1, jnp.max(run_next) >= ATT_UNDERFLOW

            lax.while_loop(lambda c: (c[0] < n_tiles) & c[1], tile, (jnp.int32(0), jnp.bool_(True)))
            o_ref[pl.ds(q0, ATT_BLOCK), :] = acc_s[...].astype(o_ref.dtype)
            return 0

        lax.fori_loop(0, nq, q_loop, 0)

    return pl.pallas_call(
        body, name="attention_fwd", grid=(heads,),
        in_specs=[_head_spec(T, 0, heads), _head_spec(T, 1, heads), _head_spec(T, 2, heads)],
        out_specs=[pl.BlockSpec((T, ATT_HEAD_DIM), lambda h: (0, h)), pl.BlockSpec((None, T, LANES), lambda h: (h, 0, 0))],
        out_shape=[jax.ShapeDtypeStruct((T, D), BF16), jax.ShapeDtypeStruct((heads, T, LANES), F32)],
        scratch_shapes=[pltpu.VMEM((ATT_BLOCK, ATT_HEAD_DIM), F32), pltpu.VMEM((ATT_BLOCK, LANES), F32)],
        compiler_params=_params(("parallel",)))(qkv, qkv, qkv)


def _attention_bwd(qkv, do, rmat):
    _, T, D = qkv.shape
    heads = D // ATT_HEAD_DIM
    nq = T // ATT_BLOCK
    scale = 1.0 / math.sqrt(ATT_HEAD_DIM)

    def body(q_ref, k_ref, v_ref, do_ref, r_ref, dqkv_ref, dk_s, dv_s, dq_s, left_s):
        rr = lax.broadcasted_iota(jnp.int32, (ATT_BLOCK, ATT_BLOCK), 0)
        cc = lax.broadcasted_iota(jnp.int32, (ATT_BLOCK, ATT_BLOCK), 1)
        upper = jnp.where(rr > cc, 1.0, 0.0).astype(BF16)
        lower = jnp.where(rr < cc, 1.0, 0.0).astype(BF16)
        lane = lax.broadcasted_iota(jnp.int32, (ATT_BLOCK, LANES), 1)
        dk_s[...] = jnp.zeros_like(dk_s)
        dv_s[...] = jnp.zeros_like(dv_s)

        def q_loop(qb, _):
            q0 = pl.multiple_of(qb * ATT_BLOCK, ATT_BLOCK)
            q = q_ref[pl.ds(q0, ATT_BLOCK), :]
            dov = do_ref[pl.ds(q0, ATT_BLOCK), :]
            dq_s[...] = jnp.zeros_like(dq_s)
            left_s[...] = jnp.zeros_like(left_s)
            rm = r_ref[pl.ds(q0, ATT_BLOCK), :]
            n_tiles = (qb + ATT_GROUP) // ATT_GROUP
            seen = (jnp.max(rm, axis=0, keepdims=True) > 0.5 * ATT_UNVISITED) & (lane[0:1, :] < n_tiles)
            n_visited = jnp.sum(jnp.where(seen, 1.0, 0.0)).astype(jnp.int32)

            def tile(j, _):
                it = n_visited - 1 - j
                k0, kt, z, mask, n = _att_tile(q, k_ref, q0, qb, it, scale)
                vt = v_ref[pl.ds(k0, ATT_TILE), :]
                run = jnp.sum(jnp.where(lane == it, r_ref[pl.ds(q0, ATT_BLOCK), :], 0.0), axis=1, keepdims=True)
                suffix, _ = _suffix_in_tile(n, upper, run)
                s = z + n
                w = jnp.where(mask, jnp.exp(s + suffix), 0.0)
                e = w * lax.dot_general(dov, vt, DIMS_NT, preferred_element_type=F32)
                es = [jnp.sum(_sub(e, g), axis=1, keepdims=True) for g in range(ATT_GROUP)]
                pre = [left_s[...]]
                for g in range(ATT_GROUP):
                    pre.append(pre[g] + es[g])
                before = _unstack_rows(_split_dot(_stack_rows(e), lower), pre)
                sig = jnp.exp(s)
                dz = (jnp.where(mask, e * (1.0 - sig) - before * sig, 0.0) * scale).astype(BF16)
                dq_s[...] += jnp.dot(dz, kt, preferred_element_type=F32)
                dk_s[pl.ds(k0, ATT_TILE), :] += lax.dot_general(dz, q, DIMS_TN, preferred_element_type=F32)
                dv_s[pl.ds(k0, ATT_TILE), :] += lax.dot_general(w.astype(BF16), dov, DIMS_TN, preferred_element_type=F32)
                left_s[...] = pre[ATT_GROUP]
                return 0

            lax.fori_loop(0, n_visited, tile, 0)
            dqkv_ref[0, pl.ds(q0, ATT_BLOCK), :] = dq_s[...].astype(dqkv_ref.dtype)
            return 0

        lax.fori_loop(0, nq, q_loop, 0)
        dqkv_ref[1, :, :] = dk_s[...].astype(dqkv_ref.dtype)
        dqkv_ref[2, :, :] = dv_s[...].astype(dqkv_ref.dtype)

    return pl.pallas_call(
        body, name="attention_bwd", grid=(heads,),
        in_specs=[_head_spec(T, 0, heads), _head_spec(T, 1, heads), _head_spec(T, 2, heads),
                  pl.BlockSpec((T, ATT_HEAD_DIM), lambda h: (0, h)), pl.BlockSpec((None, T, LANES), lambda h: (h, 0, 0))],
        out_specs=pl.BlockSpec((3, T, ATT_HEAD_DIM), lambda h: (0, 0, h)),
        out_shape=jax.ShapeDtypeStruct((3, T, D), BF16),
        scratch_shapes=[pltpu.VMEM((T, ATT_HEAD_DIM), F32), pltpu.VMEM((T, ATT_HEAD_DIM), F32),
                        pltpu.VMEM((ATT_BLOCK, ATT_HEAD_DIM), F32), pltpu.VMEM((ATT_BLOCK, LANES), F32)],
        compiler_params=_params(("parallel",)))(qkv, qkv, qkv, do, rmat)


def _block_diag_pairs(w):
    h = w.shape[0]
    wp = w.reshape(h // 2, 2, RG_HEAD_DIM, RG_HEAD_DIM)
    z = jnp.zeros_like(wp[:, 0])
    top = jnp.concatenate([wp[:, 0], z], axis=2)
    bot = jnp.concatenate([z, wp[:, 1]], axis=2)
    return jnp.concatenate([top, bot], axis=1)


def _diag_pairs(g):
    n = g.shape[0]
    a = g[:, :RG_HEAD_DIM, :RG_HEAD_DIM]
    b = g[:, RG_HEAD_DIM:, RG_HEAD_DIM:]
    return jnp.stack([a, b], axis=1).reshape(2 * n, RG_HEAD_DIM, RG_HEAD_DIM)


class _Weights:
    def __init__(self, full, shards=None, plan=None):
        self.full, self.shards, self.plan = dict(full), shards or {}, plan or {}

    def __getitem__(self, name):
        return self.full[name]

    def jobs(self, call):
        return [_gather_job(self.shards[n]) for n in self.plan.get(call, ())]

    def deliver(self, call, outs):
        for n, g in zip(self.plan.get(call, ()), outs):
            self.full[n] = _gathered_layout(n, g)


def _gathered_layout(name, g):
    if name in ("w_in", "w_qkv"):
        return _from_col_blocks(g)
    if name in ("w_out", "w_o"):
        return g.reshape(g.shape[0] * g.shape[1], g.shape[2])
    return g


class _Grads:
    def __init__(self, lands=None, plan=None):
        self.lands, self.plan = dict(lands) if lands else None, plan or {}
        self.ready, self.sent = {}, {}

    def put(self, name, arr):
        self.ready[name] = arr

    def jobs(self, call):
        if self.lands is None:
            return []
        return [_exchange_job(self.ready[n], self.lands[n], part, parts) for n, part, parts in self.plan.get(call, ())]

    def deliver(self, call, outs):
        for (n, part, parts), o in zip(self.plan.get(call, ()), outs):
            self.lands[n] = o
            self.sent.setdefault(n, set()).add((part, parts))

    def flush(self, name):
        if self.lands is None:
            return
        for n, done in self.sent.items():
            assert len(done) == next(iter(done))[1], (n, done)
        rest = [n for n in self.ready if n not in self.sent]
        if rest:
            outs = _run_jobs(name, [_exchange_job(self.ready[n], self.lands[n]) for n in rest])
            for n, o in zip(rest, outs):
                self.lands[n] = o


def _mlp_fwd(tag, h, wts, run):
    T, D = h.shape
    w_up = wts["up" + tag]
    fb = w_up.shape[2]
    F = fb * N_DEV
    tm, tn, tk = _tile(T, 1024), _tile(fb, 1024), _tile(D, MM_TK)
    nb = fb // tn

    def up_epilogue(u):
        r = jnp.maximum(u, 0.0)
        return u, r * r

    o_spec = pl.BlockSpec((tm, tn), lambda i, j, k: (i, j))
    u, act = run(
        _matmul, f"mlp_up_l{tag}",
        [(h, pl.BlockSpec((tm, tk), lambda i, j, k: (i, k))),
         (w_up, pl.BlockSpec((None, tk, tn), lambda i, j, k: (j // nb, k, j % nb)))],
        [(jax.ShapeDtypeStruct((T, F), BF16), o_spec), (jax.ShapeDtypeStruct((T, F), BF16), o_spec)],
        (T // tm, F // tn, D // tk), DIMS_NN, (tm, tn), up_epilogue, n_main=2)
    w_down = wts["down" + tag]
    tn2, tk2 = _tile(D, 1024), _tile(fb, MM_TK)
    nkb = fb // tk2
    m = run(
        _matmul, f"mlp_down_l{tag}",
        [(act, pl.BlockSpec((tm, tk2), lambda i, j, k: (i, k))),
         (w_down, pl.BlockSpec((None, tk2, tn2), lambda i, j, k: (k // nkb, k % nkb, j)))],
        [(jax.ShapeDtypeStruct((T, D), F32), pl.BlockSpec((tm, tn2), lambda i, j, k: (i, j)))],
        (T // tm, D // tn2, F // tk2), DIMS_NN, (tm, tn2), None)
    return u, act, m


def _mlp_bwd(tag, h, u, act, dm, wts, grads, run):
    T, D = h.shape
    w_up, w_down = wts["up" + tag], wts["down" + tag]
    fb = w_up.shape[2]
    F = fb * N_DEV
    grads.put("down" + tag, run(_mm_tn, f"mlp_down_dw_l{tag}", act, dm, BF16).reshape(N_DEV, fb, D))
    tm, tn, tk = _tile(T, 1024), _tile(fb, 1024), _tile(D, MM_TK)
    nb = fb // tn
    o_spec = pl.BlockSpec((tm, tn), lambda i, j, k: (i, j))
    du = run(
        _matmul, f"mlp_down_dx_l{tag}",
        [(dm, pl.BlockSpec((tm, tk), lambda i, j, k: (i, k))),
         (w_down, pl.BlockSpec((None, tn, tk), lambda i, j, k: (j // nb, j % nb, k))),
         (u, o_spec)],
        [(jax.ShapeDtypeStruct((T, F), BF16), o_spec)],
        (T // tm, F // tn, D // tk), DIMS_NT, (tm, tn),
        lambda r, uv: (r * (2.0 * jnp.maximum(uv.astype(F32), 0.0)),))
    grads.put("up" + tag, run(_mm_tn, f"mlp_up_dw_l{tag}", h, du, BF16, out_blocks=N_DEV))
    tn2, tk2 = _tile(D, 1024), _tile(fb, MM_TK)
    nkb = fb // tk2
    return run(
        _matmul, f"mlp_up_dx_l{tag}",
        [(du, pl.BlockSpec((tm, tk2), lambda i, j, k: (i, k))),
         (w_up, pl.BlockSpec((None, tn2, tk2), lambda i, j, k: (k // nkb, j, k % nkb)))],
        [(jax.ShapeDtypeStruct((T, D), F32), pl.BlockSpec((tm, tn2), lambda i, j, k: (i, j)))],
        (T // tm, D // tn2, F // tk2), DIMS_NT, (tm, tn2), None)


def _local_step(x, target, gains, conv_a, conv_b, conv_b_bias, rg_w_a, rg_b_a, rg_w_x, rg_b_x, rg_lambda, wts, grads):
    T, D = x.shape
    g = lambda l, i: gains[l, i][None, :]
    wa_p = _block_diag_pairs(rg_w_a).astype(BF16)
    wx_p = _block_diag_pairs(rg_w_x).astype(BF16)

    def run(fn, name, *args, n_main=1, **kw):
        jw, jg = wts.jobs(name), grads.jobs(name)
        res = fn(name, *args, jobs=jw + jg, **kw)
        main, jo = res[:n_main], res[n_main:]
        wts.deliver(name, jo[:len(jw)])
        grads.deliver(name, jo[len(jw):])
        return main[0] if n_main == 1 else main

    h0 = _norm_fwd("norm_in", x, g(0, 0))
    proj = run(_mm_nn, "w_in_fwd", h0, wts["w_in"], F32, out_seg=5)
    y_a = _mixer_a_fwd(proj, conv_a)
    y_b = _mixer_b_fwd(proj, conv_b, conv_b_bias, wa_p, rg_b_a, wx_p, rg_b_x, rg_lambda)
    y = jnp.stack([y_a, y_b], axis=0)
    mix0 = run(_mm_nn, "w_out_fwd", y, wts["w_out"], F32, a_seg=2)
    x1, h1 = _resid_norm("resid_mix0", x, mix0, g(0, 1), g(0, 2))
    u0, act0, m0 = _mlp_fwd("0", h1, wts, run)
    x2, h2 = _resid_norm("resid_mlp0", x1, m0, g(0, 3), g(1, 0))
    qkv = run(_mm_nn, "w_qkv_fwd", h2, wts["w_qkv"], BF16, out_seg=3)
    o, rmat = _attention_fwd(qkv)
    mix1 = run(_mm_nn, "w_o_fwd", o, wts["w_o"], F32)
    x3, h3 = _resid_norm("resid_mix1", x2, mix1, g(1, 1), g(1, 2))
    u1, act1, m1 = _mlp_fwd("1", h3, wts, run)
    dx4, sq = _final_loss("loss", x3, m1, g(1, 3), target)

    dm1, dg13 = _norm_bwd("norm_bwd_m1", m1, g(1, 3), dx4, None, BF16)
    dh3 = _mlp_bwd("1", h3, u1, act1, dm1, wts, grads, run)
    dx3, dg12 = _norm_bwd("norm_bwd_x3", x3, g(1, 2), dh3, dx4, F32)
    dmix1, dg11 = _norm_bwd("norm_bwd_mix1", mix1, g(1, 1), dx3, None, BF16)
    grads.put("w_o", run(_mm_tn, "w_o_dw", o, dmix1, BF16).reshape(N_DEV, D // N_DEV, D))
    do = run(_mm_nt, "w_o_dx", dmix1, wts["w_o"], BF16)
    dqkv = _attention_bwd(qkv, do, rmat)
    grads.put("w_qkv", _col_blocks(run(_mm_tn, "w_qkv_dw", h2, dqkv, BF16, b_seg=3), N_DEV))
    dh2 = run(_mm_nt, "w_qkv_dx", dqkv, wts["w_qkv"], F32, a_seg=3)
    dx2, dg10 = _norm_bwd("norm_bwd_x2", x2, g(1, 0), dh2, dx3, F32)
    dm0, dg03 = _norm_bwd("norm_bwd_m0", m0, g(0, 3), dx2, None, BF16)
    dh1 = _mlp_bwd("0", h1, u0, act0, dm0, wts, grads, run)
    dx1, dg02 = _norm_bwd("norm_bwd_x1", x1, g(0, 2), dh1, dx2, F32)
    dmix0, dg01 = _norm_bwd("norm_bwd_mix0", mix0, g(0, 1), dx1, None, BF16)
    grads.put("w_out", run(_mm_tn, "w_out_dw", y, dmix0, BF16, a_seg=2).reshape(N_DEV, D // N_DEV, D))
    dy = run(_mm_nt, "w_out_dx", dmix0, wts["w_out"], F32, out_seg=2)
    dproj_a, dconv_a = _mixer_a_bwd(proj, conv_a, dy)
    dproj_b, sm_b, dwa_p, dwx_p = _mixer_b_bwd(proj, conv_b, conv_b_bias, wa_p, rg_b_a, wx_p, rg_b_x, rg_lambda, dy)
    dproj = jnp.concatenate([dproj_a, dproj_b], axis=0)
    grads.put("w_in", _col_blocks(run(_mm_tn, "w_in_dw", h0, dproj, BF16, b_seg=5), N_DEV))
    dh0 = run(_mm_nt, "w_in_dx", dproj, wts["w_in"], F32, a_seg=5)
    dx0, dg00 = _norm_bwd("norm_bwd_x0", x, g(0, 0), dh0, dx1, F32)

    C = D // 2
    lanes_to_vec = lambda t, row: t[:, row, :].reshape(1, C)
    small = {
        "norm_gains": jnp.concatenate([dg00, dg01, dg02, dg03, dg10, dg11, dg12, dg13], axis=0).reshape(2, 4, D),
        "conv_a": jnp.transpose(dconv_a[:, :3, :], (1, 0, 2)).reshape(3, C),
        "conv_b": jnp.transpose(sm_b[:, :4, :], (1, 0, 2)).reshape(4, C),
        "conv_b_bias": lanes_to_vec(sm_b, _ROW_BIAS),
        "rg_w_a": _diag_pairs(dwa_p),
        "rg_b_a": lanes_to_vec(sm_b, _ROW_BA),
        "rg_w_x": _diag_pairs(dwx_p),
        "rg_b_x": lanes_to_vec(sm_b, _ROW_BX),
        "rg_lambda": lanes_to_vec(sm_b, _ROW_LAM),
    }
    return sq[0, 0], dx0, small


def _my_index():
    return 4 * lax.axis_index("x") + 2 * lax.axis_index("y") + lax.axis_index("c")


def _peers():
    x, y, c = lax.axis_index("x"), lax.axis_index("y"), lax.axis_index("c")
    out = []
    for k in range(1, N_DEV):
        px = x ^ ((k >> 2) & 1)
        py = y ^ ((k >> 1) & 1)
        pc = c ^ (k & 1)
        out.append(((px, py, pc), 4 * px + 2 * py + pc))
    return out


GATHER_FIRST = ("w_in", "w_out")
GATHER_PLAN = {"w_in_fwd": ("up0",), "mlp_up_l0": ("down0",), "mlp_down_l0": ("w_qkv", "w_o"),
               "w_qkv_fwd": ("up1",), "mlp_up_l1": ("down1",)}
EXCHANGE_PLAN = {
    "mlp_down_dx_l1": (("down1", 0, 2),), "mlp_up_dw_l1": (("down1", 1, 2),),
    "mlp_up_dx_l1": (("up1", 0, 2),), "w_qkv_dw": (("up1", 1, 2),),
    "w_qkv_dx": (("w_o", 0, 1),),
    "mlp_down_dw_l0": (("w_qkv", 0, 2),), "mlp_down_dx_l0": (("w_qkv", 1, 2),),
    "mlp_up_dw_l0": (("down0", 0, 2),), "mlp_up_dx_l0": (("down0", 1, 2),),
    "w_in_dw": (("up0", 0, 2),), "w_in_dx": (("up0", 1, 2),),
}


def _all_gather(name, shards):
    n = len(shards)

    def body(*refs):
        srcs, dsts = refs[:n], refs[n:2 * n]
        send_sems, recv_sems, local_sems = refs[2 * n:]
        me = _my_index()
        peers = _peers()
        copies = []
        for a in range(n):
            lc = pltpu.make_async_copy(srcs[a], dsts[a].at[me], local_sems.at[a])
            lc.start()
            copies.append(lc)
        remote = []
        for a in range(n):
            for k, (pos, _) in enumerate(peers):
                cp = pltpu.make_async_remote_copy(
                    src_ref=srcs[a], dst_ref=dsts[a].at[me], send_sem=send_sems.at[a, k], recv_sem=recv_sems.at[a, k],
                    device_id=pos, device_id_type=MESH)
                cp.start()
                remote.append(cp)
        for a in range(n):
            for k, (pos, idx) in enumerate(peers):
                pltpu.make_async_remote_copy(
                    src_ref=srcs[a], dst_ref=dsts[a].at[idx], send_sem=send_sems.at[a, k], recv_sem=recv_sems.at[a, k],
                    device_id=pos, device_id_type=MESH).wait_recv()
        for cp in remote:
            cp.wait_send()
        for lc in copies:
            lc.wait()

    return pl.pallas_call(
        body, name=name,
        in_specs=[_ANY] * n, out_specs=[_ANY] * n,
        out_shape=[jax.ShapeDtypeStruct((N_DEV,) + s.shape, s.dtype) for s in shards],
        scratch_shapes=[pltpu.SemaphoreType.DMA((n, N_DEV - 1)), pltpu.SemaphoreType.DMA((n, N_DEV - 1)),
                        pltpu.SemaphoreType.DMA((n,))],
    )(*shards)


def _job_sems():
    return [pltpu.SemaphoreType.DMA((N_DEV - 1,)), pltpu.SemaphoreType.DMA((N_DEV - 1,)), pltpu.SemaphoreType.DMA((1,))]


def _gather_job(shard):
    def ctx():
        x, y, c = lax.axis_index("x"), lax.axis_index("y"), lax.axis_index("c")
        chips = [(1 - x, y), (x, 1 - y), (1 - x, 1 - y)]
        return x, y, c, chips

    def idx(px, py, pc):
        return 4 * px + 2 * py + pc

    def copy(src, out, sems, k, block, to):
        return pltpu.make_async_remote_copy(
            src_ref=out.at[block] if src is None else src, dst_ref=out.at[block], send_sem=sems[0].at[k],
            recv_sem=sems[1].at[k], device_id=to, device_id_type=MESH)

    def start(ins, outs, sems):
        x, y, c, chips = ctx()
        src, out = ins[0], outs[0]
        me = idx(x, y, c)
        pltpu.make_async_copy(src, out.at[me], sems[2].at[0]).start()
        copy(src, out, sems, 0, me, (x, y, 1 - c)).start()
        for j, (px, py) in enumerate(chips):
            copy(src, out, sems, 1 + j, me, (px, py, c)).start()

    def mid(ins, outs, sems):
        x, y, c, chips = ctx()
        out = outs[0]
        for j, (px, py) in enumerate(chips):
            copy(None, out, sems, 1 + j, idx(px, py, c), (x, y, c)).wait_recv()
            copy(None, out, sems, 4 + j, idx(px, py, c), (x, y, 1 - c)).start()

    def end(ins, outs, sems):
        x, y, c, chips = ctx()
        src, out = ins[0], outs[0]
        me = (x, y, c)
        copy(None, out, sems, 0, idx(x, y, 1 - c), me).wait_recv()
        for j, (px, py) in enumerate(chips):
            copy(None, out, sems, 4 + j, idx(px, py, 1 - c), me).wait_recv()
        for k in range(N_DEV - 1):
            copy(src, out, sems, k, idx(x, y, c), me).wait_send()
        pltpu.make_async_copy(src, out.at[idx(x, y, c)], sems[2].at[0]).wait()

    return _Job([shard], [jax.ShapeDtypeStruct((N_DEV,) + shard.shape, shard.dtype)], _job_sems(), start, mid, end)


def _exchange_job(src, land, part=0, parts=1):
    rows = src.shape[1] // parts
    assert rows * parts == src.shape[1]

    def sl(ref, s):
        return ref.at[s, pl.ds(part * rows, rows)]

    def start(ins, outs, sems):
        me = _my_index()
        pltpu.make_async_copy(sl(ins[0], me), sl(outs[0], me), sems[2].at[0]).start()
        for k, (pos, idx) in enumerate(_peers()):
            pltpu.make_async_remote_copy(
                src_ref=sl(ins[0], idx), dst_ref=sl(outs[0], me), send_sem=sems[0].at[k], recv_sem=sems[1].at[k],
                device_id=pos, device_id_type=MESH).start()

    def mid(ins, outs, sems):
        pass

    def end(ins, outs, sems):
        me = _my_index()
        for k, (pos, idx) in enumerate(_peers()):
            cp = pltpu.make_async_remote_copy(
                src_ref=sl(ins[0], idx), dst_ref=sl(outs[0], idx), send_sem=sems[0].at[k], recv_sem=sems[1].at[k],
                device_id=pos, device_id_type=MESH)
            cp.wait_recv()
            cp.wait_send()
        pltpu.make_async_copy(sl(ins[0], me), sl(outs[0], me), sems[2].at[0]).wait()

    return _Job([src, land], [jax.ShapeDtypeStruct(land.shape, land.dtype)], _job_sems(), start, mid, end, alias={1: 0})


def _adamw_math(w, g, m, v):
    m = ADAM_B1 * m + (1.0 - ADAM_B1) * g
    v = ADAM_B2 * v + (1.0 - ADAM_B2) * (g * g)
    m_hat = m / (1.0 - ADAM_B1 ** ADAM_STEP)
    v_hat = v / (1.0 - ADAM_B2 ** ADAM_STEP)
    delta = -ADAM_LR * (m_hat / (jnp.sqrt(v_hat) + ADAM_EPS) + ADAM_WD * w)
    return delta, m, v


def _sum_slots(ref):
    g = ref[0].astype(F32)
    for s in range(1, N_DEV):
        g = g + ref[s].astype(F32)
    return g


def _adamw_big(name, lands, w, m, v):
    L, R, C = w.shape
    assert len(lands) == L
    tr = _tile(R, max(LANES, (256 * 1024) // C))

    def body(*refs):
        l_refs = refs[:L]
        w_ref, m_ref, v_ref, g_ref, d_ref, nm_ref, nv_ref = refs[L:]
        for li in range(L):
            @pl.when(pl.program_id(0) == li)
            def _(li=li):
                g = _sum_slots(l_refs[li])
                d, nm, nv = _adamw_math(w_ref[...], g, m_ref[...], v_ref[...])
                g_ref[...] = g
                d_ref[...] = d
                nm_ref[...] = nm
                nv_ref[...] = nv

    def land_spec(li):
        return pl.BlockSpec((N_DEV, tr, C), lambda l, i: (0, jnp.where(l == li, i, 0), 0))

    row = pl.BlockSpec((None, tr, C), lambda l, i: (l, i, 0))
    return pl.pallas_call(
        body, name=name, grid=(L, R // tr),
        in_specs=[land_spec(li) for li in range(L)] + [row, row, row],
        out_specs=[row] * 4, out_shape=[jax.ShapeDtypeStruct((L, R, C), F32)] * 4,
        compiler_params=_params(("arbitrary", "arbitrary")))(*lands, w, m, v)


def _sum8(name, slots):
    _, R, C = slots.shape

    def body(s_ref, o_ref):
        o_ref[...] = _sum_slots(s_ref)

    return pl.pallas_call(body, name=name, out_shape=jax.ShapeDtypeStruct((R, C), F32))(slots)


def _adamw_small(name, g, w, m, v):
    def body(g_ref, w_ref, m_ref, v_ref, d_ref, nm_ref, nv_ref):
        d, nm, nv = _adamw_math(w_ref[...], g_ref[...], m_ref[...], v_ref[...])
        d_ref[...] = d
        nm_ref[...] = nm
        nv_ref[...] = nv

    return pl.pallas_call(body, name=name, out_shape=[jax.ShapeDtypeStruct(w.shape, F32)] * 3)(g, w, m, v)


def _pack_rows(arrs):
    parts, spans, r0 = [], [], 0
    for a in arrs:
        flat = a.astype(F32).reshape(-1)
        rows = -(-flat.shape[0] // LANES)
        rows = -(-rows // SUBLANES) * SUBLANES
        flat = jnp.pad(flat, (0, rows * LANES - flat.shape[0]))
        parts.append(flat.reshape(rows, LANES))
        spans.append((r0, rows, a.shape))
        r0 += rows
    return jnp.concatenate(parts, axis=0), spans


def _unpack_rows(buf, span):
    r0, rows, shape = span
    n = math.prod(shape)
    return buf[..., r0:r0 + rows, :].reshape(buf.shape[:-2] + (rows * LANES,))[..., :n].reshape(buf.shape[:-2] + shape)


def _col_blocks(w, n_blocks):
    K, N = w.shape
    return jnp.transpose(w.reshape(K, n_blocks, N // n_blocks), (1, 0, 2))


def _from_col_blocks(wb):
    B, K, n = wb.shape
    return jnp.transpose(wb, (1, 0, 2)).reshape(K, B * n)


def kernel(x, norm_gains, hyb_w_in, hyb_conv_a, hyb_conv_b, hyb_conv_b_bias, hyb_rg_w_a, hyb_rg_b_a, hyb_rg_w_x, hyb_rg_b_x, hyb_rg_lambda, hyb_w_out, sb_w_qkv, sb_w_o, mlp_w_up, mlp_w_down, loss_target, m_norm_gains, m_hyb_w_in, m_hyb_conv_a, m_hyb_conv_b, m_hyb_conv_b_bias, m_hyb_rg_w_a, m_hyb_rg_b_a, m_hyb_rg_w_x, m_hyb_rg_b_x, m_hyb_rg_lambda, m_hyb_w_out, m_sb_w_qkv, m_sb_w_o, m_mlp_w_up, m_mlp_w_down, v_norm_gains, v_hyb_w_in, v_hyb_conv_a, v_hyb_conv_b, v_hyb_conv_b_bias, v_hyb_rg_w_a, v_hyb_rg_b_a, v_hyb_rg_w_x, v_hyb_rg_b_x, v_hyb_rg_lambda, v_hyb_w_out, v_sb_w_qkv, v_sb_w_o, v_mlp_w_up, v_mlp_w_down):
    T, D = x.shape[1], x.shape[2]
    me = _my_index()

    small_shards, small_spans = _pack_rows([norm_gains, hyb_conv_a[0], hyb_conv_b[0]])
    (small_all,) = _all_gather("gather_small", [small_shards])
    gains_b = _unpack_rows(small_all, small_spans[0])
    gains = jnp.transpose(gains_b, (1, 2, 0, 3)).reshape(2, 4, D)
    conv_a = _from_col_blocks(_unpack_rows(small_all, small_spans[1]))
    conv_b = _from_col_blocks(_unpack_rows(small_all, small_spans[2]))

    shards = {"w_in": hyb_w_in[0], "w_out": hyb_w_out[0], "w_qkv": sb_w_qkv[0], "w_o": sb_w_o[0],
              "up0": mlp_w_up[0], "up1": mlp_w_up[1], "down0": mlp_w_down[0], "down1": mlp_w_down[1]}
    shards = {n: s.astype(BF16) for n, s in shards.items()}
    first = _run_jobs("gather_first", [_gather_job(shards[n]) for n in GATHER_FIRST])
    wts = _Weights({n: _gathered_layout(n, g) for n, g in zip(GATHER_FIRST, first)}, shards, GATHER_PLAN)
    grads_big = _Grads({n: lax.empty((N_DEV,) + s.shape, BF16) for n, s in shards.items()}, EXCHANGE_PLAN)

    sq, grad_x, small = _local_step(
        x[0], loss_target[0], gains, conv_a, conv_b, hyb_conv_b_bias, hyb_rg_w_a[0], hyb_rg_b_a, hyb_rg_w_x[0],
        hyb_rg_b_x, hyb_rg_lambda, wts, grads_big)
    loss = lax.psum(0.5 * sq / D, ("x", "y", "c"))

    grads_big.flush("exchange_grads")
    land = grads_big.lands

    names = ["norm_gains", "hyb_w_in", "hyb_conv_a", "hyb_conv_b", "hyb_conv_b_bias", "hyb_rg_w_a", "hyb_rg_b_a",
             "hyb_rg_w_x", "hyb_rg_b_x", "hyb_rg_lambda", "hyb_w_out", "sb_w_qkv", "sb_w_o", "mlp_w_up", "mlp_w_down"]
    params = dict(zip(names, [norm_gains, hyb_w_in, hyb_conv_a, hyb_conv_b, hyb_conv_b_bias, hyb_rg_w_a, hyb_rg_b_a,
                              hyb_rg_w_x, hyb_rg_b_x, hyb_rg_lambda, hyb_w_out, sb_w_qkv, sb_w_o, mlp_w_up, mlp_w_down]))
    moms = dict(zip(names, [m_norm_gains, m_hyb_w_in, m_hyb_conv_a, m_hyb_conv_b, m_hyb_conv_b_bias, m_hyb_rg_w_a,
                            m_hyb_rg_b_a, m_hyb_rg_w_x, m_hyb_rg_b_x, m_hyb_rg_lambda, m_hyb_w_out, m_sb_w_qkv,
                            m_sb_w_o, m_mlp_w_up, m_mlp_w_down]))
    vars_ = dict(zip(names, [v_norm_gains, v_hyb_w_in, v_hyb_conv_a, v_hyb_conv_b, v_hyb_conv_b_bias, v_hyb_rg_w_a,
                             v_hyb_rg_b_a, v_hyb_rg_w_x, v_hyb_rg_b_x, v_hyb_rg_lambda, v_hyb_w_out, v_sb_w_qkv,
                             v_sb_w_o, v_mlp_w_up, v_mlp_w_down]))
    grads, deltas, new_m, new_v = {}, {}, {}, {}

    big_lands = {"hyb_w_in": ["w_in"], "hyb_w_out": ["w_out"], "sb_w_qkv": ["w_qkv"], "sb_w_o": ["w_o"],
                 "mlp_w_up": ["up0", "up1"], "mlp_w_down": ["down0", "down1"]}
    for nm, keys in big_lands.items():
        outs = _adamw_big(f"adamw_{nm}", [land[k] for k in keys], params[nm], moms[nm], vars_[nm])
        grads[nm], deltas[nm], new_m[nm], new_v[nm] = outs

    small_names = ["norm_gains", "hyb_conv_a", "hyb_conv_b", "hyb_conv_b_bias", "hyb_rg_w_a", "hyb_rg_b_a",
                   "hyb_rg_w_x", "hyb_rg_b_x", "hyb_rg_lambda"]
    small_keys = ["norm_gains", "conv_a", "conv_b", "conv_b_bias", "rg_w_a", "rg_b_a", "rg_w_x", "rg_b_x", "rg_lambda"]
    sg_buf, sg_spans = _pack_rows([small[k] for k in small_keys])
    (sg_all,) = _all_gather("gather_small_grads", [sg_buf])
    sg_sum = _sum8("sum_small_grads", sg_all)
    full = {nm: _unpack_rows(sg_sum, sp) for nm, sp in zip(small_names, sg_spans)}
    cb = (D // 2) // N_DEV
    small_grads = {
        "norm_gains": lax.dynamic_slice_in_dim(full["norm_gains"], me * (D // N_DEV), D // N_DEV, axis=2),
        "hyb_conv_a": lax.dynamic_slice_in_dim(full["hyb_conv_a"], me * cb, cb, axis=1)[None],
        "hyb_conv_b": lax.dynamic_slice_in_dim(full["hyb_conv_b"], me * cb, cb, axis=1)[None],
        "hyb_conv_b_bias": full["hyb_conv_b_bias"],
        "hyb_rg_w_a": full["hyb_rg_w_a"][None],
        "hyb_rg_b_a": full["hyb_rg_b_a"],
        "hyb_rg_w_x": full["hyb_rg_w_x"][None],
        "hyb_rg_b_x": full["hyb_rg_b_x"],
        "hyb_rg_lambda": full["hyb_rg_lambda"],
    }
    pk = lambda d: _pack_rows([d[nm] for nm in small_names])
    g_buf, spans = pk(small_grads)
    w_buf, _ = pk(params)
    m_buf, _ = pk(moms)
    v_buf, _ = pk(vars_)
    d_buf, nm_buf, nv_buf = _adamw_small("adamw_small", g_buf, w_buf, m_buf, v_buf)
    for nm, sp in zip(small_names, spans):
        grads[nm] = small_grads[nm]
        deltas[nm], new_m[nm], new_v[nm] = _unpack_rows(d_buf, sp), _unpack_rows(nm_buf, sp), _unpack_rows(nv_buf, sp)

    return (loss, grad_x[None], *[grads[n] for n in names], *[deltas[n] for n in names],
            *[new_m[n] for n in names], *[new_v[n] for n in names])
```

```python
import functools
import math

import jax
import jax.numpy as jnp
from jax import lax
from jax.experimental import pallas as pl
from jax.experimental.pallas import tpu as pltpu

F32 = jnp.float32
BF16 = jnp.bfloat16

NORM_EPS = 1e-6
LRU_C = 8.0
ATT_HEAD_DIM = 128
RG_HEAD_DIM = 64
LANES = 128
SUBLANES = 8
N_DEV = 8
ADAM_LR = 0.001
ADAM_B1 = 0.9
ADAM_B2 = 0.999
ADAM_EPS = 1e-08
ADAM_WD = 0.01
ADAM_STEP = 10
VMEM_LIMIT = 56 * 1024 * 1024
MM_TK = 2048
MESH = pl.DeviceIdType.MESH


def _tile(n, pref):
    if n <= pref:
        return n
    t = (pref // LANES) * LANES
    while t > LANES and n % t:
        t -= LANES
    assert n % t == 0, (n, pref)
    return t


def _params(sem):
    return pltpu.CompilerParams(dimension_semantics=sem, vmem_limit_bytes=VMEM_LIMIT)


DIMS_NN = (((1,), (0,)), ((), ()))
DIMS_NT = (((1,), (1,)), ((), ()))
DIMS_TN = (((0,), (0,)), ((), ()))


_ANY = pl.BlockSpec(memory_space=pl.ANY)


class _Job:
    def __init__(self, ins, outs, sems, start, mid, end, alias=None):
        self.ins, self.outs, self.sems = ins, outs, sems
        self.start, self.mid, self.end = start, mid, end
        self.alias = alias or {}


def _job_plumbing(jobs, n_in, n_out):
    j_ins = [a for jb in jobs for a in jb.ins]
    j_outs = [o for jb in jobs for o in jb.outs]
    j_sems = [s for jb in jobs for s in jb.sems]
    aliases, pi, po = {}, 0, 0
    for jb in jobs:
        for i_in, i_out in jb.alias.items():
            aliases[n_in + pi + i_in] = n_out + po + i_out
        pi += len(jb.ins)
        po += len(jb.outs)
    return j_ins, j_outs, j_sems, aliases


def _job_phase(jobs, which, jin, jout, jsem):
    pi = po = ps = 0
    for jb in jobs:
        getattr(jb, which)(jin[pi:pi + len(jb.ins)], jout[po:po + len(jb.outs)], jsem[ps:ps + len(jb.sems)])
        pi, po, ps = pi + len(jb.ins), po + len(jb.outs), ps + len(jb.sems)


def _run_jobs(name, jobs):
    j_ins, j_outs, j_sems, aliases = _job_plumbing(jobs, 0, 0)
    n_ji, n_jo = len(j_ins), len(j_outs)

    def body(*refs):
        jin, jout, jsem = refs[:n_ji], refs[n_ji:n_ji + n_jo], refs[n_ji + n_jo:]
        for which in ("start", "mid", "end"):
            _job_phase(jobs, which, jin, jout, jsem)

    return pl.pallas_call(body, name=name, in_specs=[_ANY] * n_ji, out_specs=[_ANY] * n_jo, out_shape=j_outs,
                          scratch_shapes=j_sems, input_output_aliases=aliases)(*j_ins)


def _carry_call(body, *, name, steps, in_specs, out_specs, out_shape, scratch_shapes, args, jobs=(), aliases=None):
    n_in, n_out, n_sc = len(in_specs), len(out_shape), len(scratch_shapes)
    j_ins, j_outs, j_sems, j_aliases = _job_plumbing(jobs, n_in, n_out)
    n_ji, n_jo = len(j_ins), len(j_outs)

    def wrapped(*refs):
        ins, jin = refs[:n_in], refs[n_in:n_in + n_ji]
        o0 = n_in + n_ji
        outs, jout = refs[o0:o0 + n_out], refs[o0 + n_out:o0 + n_out + n_jo]
        s0 = o0 + n_out + n_jo
        scratch, jsem = refs[s0:s0 + n_sc], refs[s0 + n_sc:]
        step = pl.program_id(0)
        if jobs:
            pl.when(step == 0)(lambda: _job_phase(jobs, "start", jin, jout, jsem))
            pl.when(step == (4 * steps) // 5)(lambda: _job_phase(jobs, "mid", jin, jout, jsem))
        body(*ins, *outs, *scratch)
        if jobs:
            pl.when(step == steps - 1)(lambda: _job_phase(jobs, "end", jin, jout, jsem))

    return pl.pallas_call(
        wrapped, name=name, grid=(steps,),
        in_specs=list(in_specs) + [_ANY] * n_ji, out_specs=list(out_specs) + [_ANY] * n_jo,
        out_shape=list(out_shape) + j_outs, scratch_shapes=list(scratch_shapes) + j_sems,
        input_output_aliases={**(aliases or {}), **j_aliases},
        compiler_params=_params(("arbitrary",) if jobs else ("parallel",)))(*args, *j_ins)


def _matmul(name, ins, outs, grid, dims, acc_shape, epilogue=None, jobs=()):
    n_in, n_out, nk = len(ins), len(outs), grid[2]
    j_ins, j_outs, j_sems, aliases = _job_plumbing(jobs, n_in, n_out)
    n_ji, n_jo = len(j_ins), len(j_outs)
    total = grid[0] * grid[1] * grid[2]
    n_acc = 0 if nk == 1 else 1

    def body(*refs):
        a_ref, b_ref = refs[0], refs[1]
        extras = refs[2:n_in]
        jin = refs[n_in:n_in + n_ji]
        out_refs = refs[n_in + n_ji:n_in + n_ji + n_out]
        jout = refs[n_in + n_ji + n_out:n_in + n_ji + n_out + n_jo]
        jsem = refs[n_in + n_ji + n_out + n_jo + n_acc:]
        k = pl.program_id(2)
        step = (pl.program_id(0) * grid[1] + pl.program_id(1)) * grid[2] + k
        if jobs:
            pl.when(step == 0)(lambda: _job_phase(jobs, "start", jin, jout, jsem))
            pl.when(step == (4 * total) // 5)(lambda: _job_phase(jobs, "mid", jin, jout, jsem))

        def finish(r):
            res = epilogue(r, *[e[...] for e in extras]) if epilogue is not None else (r,)
            for o, v in zip(out_refs, res):
                o[...] = v.astype(o.dtype)

        prod = lax.dot_general(a_ref[...], b_ref[...], dims, preferred_element_type=F32)
        if nk == 1:
            finish(prod)
        else:
            acc = refs[n_in + n_ji + n_out + n_jo]

            @pl.when(k == 0)
            def _():
                acc[...] = prod

            @pl.when((k > 0) & (k < nk - 1))
            def _():
                acc[...] += prod

            @pl.when(k == nk - 1)
            def _():
                finish(acc[...] + prod)

        if jobs:
            pl.when(step == total - 1)(lambda: _job_phase(jobs, "end", jin, jout, jsem))

    sem = ("arbitrary",) * 3 if jobs else ("parallel", "parallel", "arbitrary")
    res = pl.pallas_call(
        body, name=name, grid=grid,
        in_specs=[s for _, s in ins] + [_ANY] * n_ji,
        out_specs=[s for _, s in outs] + [_ANY] * n_jo,
        out_shape=[s for s, _ in outs] + j_outs,
        scratch_shapes=[pltpu.VMEM(acc_shape, F32)] * n_acc + j_sems,
        input_output_aliases=aliases,
        compiler_params=_params(sem),
    )(*[a for a, _ in ins], *j_ins)
    return res


def _mm_nn(name, a, b, out_dtype, *, a_seg=None, out_seg=None, tm=1024, tn=1024, tk=MM_TK, epilogue=None,
           extras=(), n_out=1, out_dtypes=None, jobs=()):
    if a_seg:
        _, M, ks = a.shape
        K = ks * a_seg
    else:
        M, K = a.shape
        ks = K
    N = b.shape[1]
    ns = N // out_seg if out_seg else N
    tm, tn, tk = _tile(M, tm), _tile(ns, tn), _tile(ks, tk)
    nks, nns = ks // tk, ns // tn
    grid = (M // tm, N // tn, K // tk)
    if a_seg:
        a_spec = pl.BlockSpec((None, tm, tk), lambda i, j, k: (k // nks, i, k % nks))
    else:
        a_spec = pl.BlockSpec((tm, tk), lambda i, j, k: (i, k))
    b_spec = pl.BlockSpec((tk, tn), lambda i, j, k: (k, j))
    if out_seg:
        o_spec = pl.BlockSpec((None, tm, tn), lambda i, j, k: (j // nns, i, j % nns))
        o_shape = (out_seg, M, ns)
    else:
        o_spec = pl.BlockSpec((tm, tn), lambda i, j, k: (i, j))
        o_shape = (M, N)
    dts = out_dtypes or [out_dtype] * n_out
    outs = [(jax.ShapeDtypeStruct(o_shape, dt), o_spec) for dt in dts]
    ins = [(a, a_spec), (b, b_spec)] + [(e, o_spec) for e in extras]
    return _matmul(name, ins, outs, grid, DIMS_NN, (tm, tn), epilogue, jobs)


def _mm_nt(name, a, b, out_dtype, *, a_seg=None, out_seg=None, tm=1024, tn=1024, tk=MM_TK, epilogue=None, extras=(),
           jobs=()):
    if a_seg:
        _, M, ks = a.shape
        K = ks * a_seg
    else:
        M, K = a.shape
        ks = K
    N = b.shape[0]
    ns = N // out_seg if out_seg else N
    tm, tn, tk = _tile(M, tm), _tile(ns, tn), _tile(ks, tk)
    nks, nns = ks // tk, ns // tn
    grid = (M // tm, N // tn, K // tk)
    if a_seg:
        a_spec = pl.BlockSpec((None, tm, tk), lambda i, j, k: (k // nks, i, k % nks))
    else:
        a_spec = pl.BlockSpec((tm, tk), lambda i, j, k: (i, k))
    b_spec = pl.BlockSpec((tn, tk), lambda i, j, k: (j, k))
    if out_seg:
        o_spec = pl.BlockSpec((None, tm, tn), lambda i, j, k: (j // nns, i, j % nns))
        o_shape = (out_seg, M, ns)
    else:
        o_spec = pl.BlockSpec((tm, tn), lambda i, j, k: (i, j))
        o_shape = (M, N)
    outs = [(jax.ShapeDtypeStruct(o_shape, out_dtype), o_spec)]
    ins = [(a, a_spec), (b, b_spec)] + [(e, o_spec) for e in extras]
    return _matmul(name, ins, outs, grid, DIMS_NT, (tm, tn), epilogue, jobs)


def _mm_tn(name, a, b, out_dtype, *, a_seg=None, b_seg=None, out_blocks=None, tm=1024, tn=1024, tk=MM_TK, jobs=()):
    if a_seg:
        _, T, ms = a.shape
        M = ms * a_seg
    else:
        T, M = a.shape
        ms = M
    if b_seg:
        _, _, ns = b.shape
        N = ns * b_seg
    else:
        N = b.shape[1]
        ns = N
    nb_cols = N // out_blocks if out_blocks else N
    tm, tk = _tile(ms, tm), _tile(T, tk)
    tn = _tile(math.gcd(ns, nb_cols), tn)
    nms, nns, nbs = ms // tm, ns // tn, nb_cols // tn
    grid = (M // tm, N // tn, T // tk)
    if a_seg:
        a_spec = pl.BlockSpec((None, tk, tm), lambda i, j, k: (i // nms, k, i % nms))
    else:
        a_spec = pl.BlockSpec((tk, tm), lambda i, j, k: (k, i))
    if b_seg:
        b_spec = pl.BlockSpec((None, tk, tn), lambda i, j, k: (j // nns, k, j % nns))
    else:
        b_spec = pl.BlockSpec((tk, tn), lambda i, j, k: (k, j))
    if out_blocks:
        o_spec = pl.BlockSpec((None, tm, tn), lambda i, j, k: (j // nbs, i, j % nbs))
        o_shape = (out_blocks, M, nb_cols)
    else:
        o_spec = pl.BlockSpec((tm, tn), lambda i, j, k: (i, j))
        o_shape = (M, N)
    outs = [(jax.ShapeDtypeStruct(o_shape, out_dtype), o_spec)]
    return _matmul(name, [(a, a_spec), (b, b_spec)], outs, grid, DIMS_TN, (tm, tn), None, jobs)


def _rms(x):
    return lax.rsqrt(jnp.mean(x * x, axis=-1, keepdims=True) + NORM_EPS)


def _row_tile(T):
    return _tile(T, 256)


def _norm_fwd(name, x, g):
    T, D = x.shape
    tr = _row_tile(T)

    def body(x_ref, g_ref, h_ref):
        xv = x_ref[...]
        h_ref[...] = (xv * _rms(xv) * g_ref[...]).astype(h_ref.dtype)

    row = pl.BlockSpec((tr, D), lambda i: (i, 0))
    vec = pl.BlockSpec((1, D), lambda i: (0, 0))
    return pl.pallas_call(body, name=name, grid=(T // tr,), in_specs=[row, vec], out_specs=row,
                          out_shape=jax.ShapeDtypeStruct((T, D), BF16), compiler_params=_params(("parallel",)))(x, g)


def _resid_norm(name, x, br, g_post, g_next):
    T, D = x.shape
    tr = _row_tile(T)

    def body(x_ref, br_ref, gp_ref, gn_ref, xn_ref, h_ref):
        b = br_ref[...]
        xn = x_ref[...] + b * _rms(b) * gp_ref[...]
        xn_ref[...] = xn
        h_ref[...] = (xn * _rms(xn) * gn_ref[...]).astype(h_ref.dtype)

    row = pl.BlockSpec((tr, D), lambda i: (i, 0))
    vec = pl.BlockSpec((1, D), lambda i: (0, 0))
    return pl.pallas_call(body, name=name, grid=(T // tr,), in_specs=[row, row, vec, vec], out_specs=[row, row],
                          out_shape=[jax.ShapeDtypeStruct((T, D), F32), jax.ShapeDtypeStruct((T, D), BF16)],
                          compiler_params=_params(("parallel",)))(x, br, g_post, g_next)


def _final_loss(name, x, br, g_post, target):
    T, D = x.shape
    tr = _row_tile(T)

    def body(x_ref, br_ref, gp_ref, t_ref, dy_ref, ls_ref):
        b = br_ref[...]
        err = x_ref[...] + b * _rms(b) * gp_ref[...] - t_ref[...]
        dy_ref[...] = err * (1.0 / D)

        @pl.when(pl.program_id(0) == 0)
        def _():
            ls_ref[...] = jnp.zeros_like(ls_ref)

        ls_ref[...] += jnp.sum(err * err)

    row = pl.BlockSpec((tr, D), lambda i: (i, 0))
    vec = pl.BlockSpec((1, D), lambda i: (0, 0))
    acc = pl.BlockSpec((SUBLANES, LANES), lambda i: (0, 0))
    return pl.pallas_call(body, name=name, grid=(T // tr,), in_specs=[row, row, vec, row], out_specs=[row, acc],
                          out_shape=[jax.ShapeDtypeStruct((T, D), F32), jax.ShapeDtypeStruct((SUBLANES, LANES), F32)],
                          compiler_params=_params(("arbitrary",)))(x, br, g_post, target)


def _norm_bwd(name, x, g, dy, add, out_dtype):
    T, D = x.shape
    tr = _row_tile(T)
    has_add = add is not None

    def body(*refs):
        if has_add:
            x_ref, g_ref, dy_ref, add_ref, dx_ref, dg_ref = refs
        else:
            x_ref, g_ref, dy_ref, dx_ref, dg_ref = refs
        xv = x_ref[...]
        r = _rms(xv)
        xhat = xv * r
        dyv = dy_ref[...].astype(F32)
        gdy = dyv * g_ref[...]
        dx = r * (gdy - xhat * jnp.mean(gdy * xhat, axis=-1, keepdims=True))
        if has_add:
            dx = dx + add_ref[...]
        dx_ref[...] = dx.astype(dx_ref.dtype)

        @pl.when(pl.program_id(0) == 0)
        def _():
            dg_ref[...] = jnp.zeros_like(dg_ref)

        dg_ref[...] += jnp.sum(dyv * xhat, axis=0, keepdims=True)

    row = pl.BlockSpec((tr, D), lambda i: (i, 0))
    vec = pl.BlockSpec((1, D), lambda i: (0, 0))
    ins = [x, g, dy] + ([add] if has_add else [])
    specs = [row, vec, row] + ([row] if has_add else [])
    return pl.pallas_call(body, name=name, grid=(T // tr,), in_specs=specs, out_specs=[row, vec],
                          out_shape=[jax.ShapeDtypeStruct((T, D), out_dtype), jax.ShapeDtypeStruct((1, D), F32)],
                          compiler_params=_params(("arbitrary",)))(*ins)


HALO = SUBLANES
TIME_CHUNK = 512


def _chunks(T):
    tc = min(TIME_CHUNK, T)
    assert T % tc == 0 and tc % SUBLANES == 0
    return [(t0, tc) for t0 in range(0, T, tc)]


def _log_sigmoid(x):
    return -(jnp.maximum(-x, 0.0) + jnp.log(1.0 + jnp.exp(-jnp.abs(x))))


def _one_minus_exp(x):
    series = -x * (1.0 + x * (0.5 + x * (1.0 / 6.0 + x * (1.0 / 24.0))))
    return jnp.where(x > -0.01, series, 1.0 - jnp.exp(x))


_GELU_C = math.sqrt(2.0 / math.pi)


def _gelu(x):
    return 0.5 * x * (1.0 + jnp.tanh(_GELU_C * (x + 0.044715 * x * x * x)))


def _gelu_grad(x):
    th = jnp.tanh(_GELU_C * (x + 0.044715 * x * x * x))
    return 0.5 * (1.0 + th) + 0.5 * x * (1.0 - th * th) * _GELU_C * (1.0 + 3.0 * 0.044715 * x * x)


def _tile_scan(a, b, reverse):
    rows = a.shape[0]
    pos = lax.broadcasted_iota(jnp.int32, a.shape, 0) & (SUBLANES - 1)
    for d in (1, 2, 4):
        if reverse:
            ok = pos < SUBLANES - d
            shift = rows - d
        else:
            ok = pos >= d
            shift = d
        a_sh = jnp.where(ok, pltpu.roll(a, shift, 0), 1.0)
        b_sh = jnp.where(ok, pltpu.roll(b, shift, 0), 0.0)
        b = a * b_sh + b
        a = a * a_sh
    return a, b


def _carry_scan(a_s, b_s, T, reverse):
    n = T // SUBLANES
    edge = 0 if reverse else SUBLANES - 1

    def step(j, carry):
        g = (n - 1 - j) if reverse else j
        r = pl.multiple_of(g * SUBLANES, SUBLANES)
        h = b_s[pl.ds(r, SUBLANES), :] + a_s[pl.ds(r, SUBLANES), :] * carry
        b_s[pl.ds(r, SUBLANES), :] = h
        return jnp.broadcast_to(h[edge:edge + 1, :], h.shape)

    lax.fori_loop(0, n, step, jnp.zeros((SUBLANES, a_s.shape[1]), F32))


def _seg_spec(T, seg, nblk):
    return pl.BlockSpec((None, T, LANES), lambda c: (seg, 0, c))


def _rows_to_tile(rows):
    idx = lax.broadcasted_iota(jnp.int32, (SUBLANES, LANES), 0)
    out = jnp.zeros((SUBLANES, LANES), F32)
    for k, r in enumerate(rows):
        out = jnp.where(idx == k, r, out)
    return out


def _mixer_a_fwd(proj, conv_a):
    _, T, C = proj.shape
    nblk = C // LANES
    chunks = _chunks(T)

    def body(bg_ref, cg_ref, ax_ref, w_ref, y_ref, p_s):
        p_s[pl.ds(0, HALO), :] = jnp.zeros((HALO, LANES), F32)
        for t0, tc in chunks:
            p_s[pl.ds(HALO + t0, tc), :] = cg_ref[pl.ds(t0, tc), :] * ax_ref[pl.ds(t0, tc), :]
        w = w_ref[...]
        for t0, tc in chunks:
            c = (w[2:3, :] * p_s[pl.ds(HALO + t0, tc), :] + w[1:2, :] * p_s[pl.ds(HALO + t0 - 1, tc), :]
                 + w[0:1, :] * p_s[pl.ds(HALO + t0 - 2, tc), :])
            y_ref[pl.ds(t0, tc), :] = (bg_ref[pl.ds(t0, tc), :] * c).astype(y_ref.dtype)

    return pl.pallas_call(
        body, name="mixer_a_fwd", grid=(nblk,),
        in_specs=[_seg_spec(T, 0, nblk), _seg_spec(T, 1, nblk), _seg_spec(T, 2, nblk),
                  pl.BlockSpec((3, LANES), lambda c: (0, c))],
        out_specs=_seg_spec(T, 0, nblk),
        out_shape=jax.ShapeDtypeStruct((2, T, C), BF16),
        scratch_shapes=[pltpu.VMEM((T + HALO, LANES), F32)],
        compiler_params=_params(("parallel",)))(proj, proj, proj, conv_a)


def _mixer_a_bwd(proj, conv_a, dy):
    _, T, C = proj.shape
    nblk = C // LANES
    chunks = _chunks(T)

    def body(bg_ref, cg_ref, ax_ref, w_ref, dy_ref, dp_ref, dw_ref, p_s, dc_s):
        p_s[pl.ds(0, HALO), :] = jnp.zeros((HALO, LANES), F32)
        dc_s[pl.ds(T, HALO), :] = jnp.zeros((HALO, LANES), F32)
        for t0, tc in chunks:
            p_s[pl.ds(HALO + t0, tc), :] = cg_ref[pl.ds(t0, tc), :] * ax_ref[pl.ds(t0, tc), :]
        w = w_ref[...]
        for t0, tc in chunks:
            c = (w[2:3, :] * p_s[pl.ds(HALO + t0, tc), :] + w[1:2, :] * p_s[pl.ds(HALO + t0 - 1, tc), :]
                 + w[0:1, :] * p_s[pl.ds(HALO + t0 - 2, tc), :])
            dyv = dy_ref[pl.ds(t0, tc), :]
            dp_ref[0, pl.ds(t0, tc), :] = (dyv * c).astype(dp_ref.dtype)
            dc_s[pl.ds(t0, tc), :] = dyv * bg_ref[pl.ds(t0, tc), :]
        dw = [jnp.zeros((1, LANES), F32) for _ in range(3)]
        for t0, tc in chunks:
            dc = dc_s[pl.ds(t0, tc), :]
            dpv = w[2:3, :] * dc + w[1:2, :] * dc_s[pl.ds(t0 + 1, tc), :] + w[0:1, :] * dc_s[pl.ds(t0 + 2, tc), :]
            dp_ref[1, pl.ds(t0, tc), :] = (dpv * ax_ref[pl.ds(t0, tc), :]).astype(dp_ref.dtype)
            dp_ref[2, pl.ds(t0, tc), :] = (dpv * cg_ref[pl.ds(t0, tc), :]).astype(dp_ref.dtype)
            for k in range(3):
                dw[k] = dw[k] + jnp.sum(dc * p_s[pl.ds(HALO + t0 - (2 - k), tc), :], axis=0, keepdims=True)
        dw_ref[...] = _rows_to_tile(dw)

    return pl.pallas_call(
        body, name="mixer_a_bwd", grid=(nblk,),
        in_specs=[_seg_spec(T, 0, nblk), _seg_spec(T, 1, nblk), _seg_spec(T, 2, nblk),
                  pl.BlockSpec((3, LANES), lambda c: (0, c)), _seg_spec(T, 0, nblk)],
        out_specs=[pl.BlockSpec((3, T, LANES), lambda c: (0, 0, c)),
                   pl.BlockSpec((None, SUBLANES, LANES), lambda c: (c, 0, 0))],
        out_shape=[jax.ShapeDtypeStruct((6, T, C), BF16), jax.ShapeDtypeStruct((nblk, SUBLANES, LANES), F32)],
        scratch_shapes=[pltpu.VMEM((T + HALO, LANES), F32), pltpu.VMEM((T + HALO, LANES), F32)],
        compiler_params=_params(("parallel",)))(proj, proj, proj, conv_a, dy)


def _rg_gates(xr, wa, ba, wx, bx, ls):
    xb = xr.astype(BF16)
    r = jax.nn.sigmoid(jnp.dot(xb, wa, preferred_element_type=F32) + ba)
    i = jax.nn.sigmoid(jnp.dot(xb, wx, preferred_element_type=F32) + bx)
    log_a = LRU_C * r * ls
    a = jnp.exp(log_a)
    mult = jnp.sqrt(_one_minus_exp(2.0 * log_a))
    return r, i, a, mult


def _conv4(xh_s, cw, bias, t0, tc):
    return (cw[3:4, :] * xh_s[pl.ds(HALO + t0, tc), :] + cw[2:3, :] * xh_s[pl.ds(HALO + t0 - 1, tc), :]
            + cw[1:2, :] * xh_s[pl.ds(HALO + t0 - 2, tc), :] + cw[0:1, :] * xh_s[pl.ds(HALO + t0 - 3, tc), :] + bias)


def _mixer_b_specs(T, nblk):
    vec = pl.BlockSpec((1, LANES), lambda c: (0, c))
    mat = pl.BlockSpec((None, LANES, LANES), lambda c: (c, 0, 0))
    return [_seg_spec(T, 3, nblk), _seg_spec(T, 4, nblk), pl.BlockSpec((4, LANES), lambda c: (0, c)),
            vec, mat, vec, mat, vec, vec]


def _mixer_b_fwd(proj, conv_b, bias, wa, ba, wx, bx, lam, y):
    _, T, C = proj.shape
    nblk = C // LANES
    chunks = _chunks(T)

    def body(gate_ref, x_ref, cw_ref, cb_ref, wa_ref, ba_ref, wx_ref, bx_ref, lam_ref, y_in, y_ref, xh_s, a_s, b_s):
        xh_s[pl.ds(0, HALO), :] = jnp.zeros((HALO, LANES), F32)
        for t0, tc in chunks:
            xh_s[pl.ds(HALO + t0, tc), :] = x_ref[pl.ds(t0, tc), :]
        cw, bias_v = cw_ref[...], cb_ref[...]
        ls = _log_sigmoid(lam_ref[...])
        for t0, tc in chunks:
            xr = _conv4(xh_s, cw, bias_v, t0, tc)
            r, i, a, mult = _rg_gates(xr, wa_ref[...], ba_ref[...], wx_ref[...], bx_ref[...], ls)
            ac, hc = _tile_scan(a, mult * i * xr, reverse=False)
            a_s[pl.ds(t0, tc), :] = ac
            b_s[pl.ds(t0, tc), :] = hc
        _carry_scan(a_s, b_s, T, reverse=False)
        for t0, tc in chunks:
            y_ref[pl.ds(t0, tc), :] = (b_s[pl.ds(t0, tc), :] * _gelu(gate_ref[pl.ds(t0, tc), :])).astype(y_ref.dtype)

    return pl.pallas_call(
        body, name="mixer_b_fwd", grid=(nblk,), in_specs=_mixer_b_specs(T, nblk) + [_ANY],
        out_specs=_seg_spec(T, 1, nblk),
        out_shape=jax.ShapeDtypeStruct(y.shape, y.dtype),
        scratch_shapes=[pltpu.VMEM((T + HALO, LANES), F32), pltpu.VMEM((T, LANES), F32), pltpu.VMEM((T, LANES), F32)],
        input_output_aliases={9: 0},
        compiler_params=_params(("parallel",)))(proj, proj, conv_b, bias, wa, ba, wx, bx, lam, y)


_ROW_CONV, _ROW_BIAS, _ROW_BA, _ROW_BX, _ROW_LAM = 0, 4, 5, 6, 7


def _mixer_b_bwd(name, proj, conv_b, bias, wa, ba, wx, bx, lam, dy, dproj, jobs=()):
    _, T, C = proj.shape
    nblk = C // LANES
    chunks = _chunks(T)

    def body(gate_ref, x_ref, cw_ref, cb_ref, wa_ref, ba_ref, wx_ref, bx_ref, lam_ref, dy_ref, dp_in,
             dp_ref, sm_ref, dwa_ref, dwx_ref, xh_s, xr_s, r_s, i_s, a_s, h_s, sa_s, sb_s, dx_s):
        zero_halo = jnp.zeros((HALO, LANES), F32)
        xh_s[pl.ds(0, HALO), :] = zero_halo
        h_s[pl.ds(0, HALO), :] = zero_halo
        a_s[pl.ds(T, HALO), :] = zero_halo
        dx_s[pl.ds(T, HALO), :] = zero_halo
        for t0, tc in chunks:
            xh_s[pl.ds(HALO + t0, tc), :] = x_ref[pl.ds(t0, tc), :]
        cw, bias_v = cw_ref[...], cb_ref[...]
        lam_v = lam_ref[...]
        ls = _log_sigmoid(lam_v)
        wa_v, wx_v, ba_v, bx_v = wa_ref[...], wx_ref[...], ba_ref[...], bx_ref[...]
        for t0, tc in chunks:
            xr = _conv4(xh_s, cw, bias_v, t0, tc)
            r, i, a, mult = _rg_gates(xr, wa_v, ba_v, wx_v, bx_v, ls)
            xr_s[pl.ds(t0, tc), :] = xr
            r_s[pl.ds(t0, tc), :] = r
            i_s[pl.ds(t0, tc), :] = i
            a_s[pl.ds(t0, tc), :] = a
            ac, hc = _tile_scan(a, mult * i * xr, reverse=False)
            sa_s[pl.ds(t0, tc), :] = ac
            sb_s[pl.ds(t0, tc), :] = hc
        _carry_scan(sa_s, sb_s, T, reverse=False)
        for t0, tc in chunks:
            h_s[pl.ds(HALO + t0, tc), :] = sb_s[pl.ds(t0, tc), :]
        for t0, tc in chunks:
            gv = gate_ref[pl.ds(t0, tc), :]
            dyv = dy_ref[pl.ds(t0, tc), :]
            dp_ref[0, pl.ds(t0, tc), :] = (dyv * h_s[pl.ds(HALO + t0, tc), :] * _gelu_grad(gv)).astype(dp_ref.dtype)
            ac, gc = _tile_scan(a_s[pl.ds(t0 + 1, tc), :], dyv * _gelu(gv), reverse=True)
            sa_s[pl.ds(t0, tc), :] = ac
            sb_s[pl.ds(t0, tc), :] = gc
        _carry_scan(sa_s, sb_s, T, reverse=True)
        acc = {k: jnp.zeros((1, LANES), F32) for k in ("bias", "ba", "bx", "lam")}
        dwa = jnp.zeros((LANES, LANES), F32)
        dwx = jnp.zeros((LANES, LANES), F32)
        for t0, tc in chunks:
            dht = sb_s[pl.ds(t0, tc), :]
            xr, r, i, a = xr_s[pl.ds(t0, tc), :], r_s[pl.ds(t0, tc), :], i_s[pl.ds(t0, tc), :], a_s[pl.ds(t0, tc), :]
            mult = jnp.sqrt(_one_minus_exp(2.0 * LRU_C * r * ls))
            da = dht * h_s[pl.ds(HALO + t0 - 1, tc), :]
            dmult = dht * i * xr
            di = dht * mult * xr
            dlog_a = da * a - dmult * a * a / mult
            dpa = dlog_a * (LRU_C * ls) * r * (1.0 - r)
            dpx = di * i * (1.0 - i)
            acc["lam"] = acc["lam"] + jnp.sum(dlog_a * r, axis=0, keepdims=True)
            acc["ba"] = acc["ba"] + jnp.sum(dpa, axis=0, keepdims=True)
            acc["bx"] = acc["bx"] + jnp.sum(dpx, axis=0, keepdims=True)
            xb, dpab, dpxb = xr.astype(BF16), dpa.astype(BF16), dpx.astype(BF16)
            dwa = dwa + lax.dot_general(xb, dpab, DIMS_TN, preferred_element_type=F32)
            dwx = dwx + lax.dot_general(xb, dpxb, DIMS_TN, preferred_element_type=F32)
            dxr = (dht * mult * i + lax.dot_general(dpab, wa_v, DIMS_NT, preferred_element_type=F32)
                   + lax.dot_general(dpxb, wx_v, DIMS_NT, preferred_element_type=F32))
            acc["bias"] = acc["bias"] + jnp.sum(dxr, axis=0, keepdims=True)
            dx_s[pl.ds(t0, tc), :] = dxr
        dcw = [jnp.zeros((1, LANES), F32) for _ in range(4)]
        for t0, tc in chunks:
            dxr = dx_s[pl.ds(t0, tc), :]
            dxin = (cw[3:4, :] * dxr + cw[2:3, :] * dx_s[pl.ds(t0 + 1, tc), :] + cw[1:2, :] * dx_s[pl.ds(t0 + 2, tc), :]
                    + cw[0:1, :] * dx_s[pl.ds(t0 + 3, tc), :])
            dp_ref[1, pl.ds(t0, tc), :] = dxin.astype(dp_ref.dtype)
            for k in range(4):
                dcw[k] = dcw[k] + jnp.sum(dxr * xh_s[pl.ds(HALO + t0 - (3 - k), tc), :], axis=0, keepdims=True)
        dlam = acc["lam"] * LRU_C * jax.nn.sigmoid(-lam_v)
        sm_ref[...] = _rows_to_tile(dcw + [acc["bias"], acc["ba"], acc["bx"], dlam])
        dwa_ref[...] = dwa
        dwx_ref[...] = dwx

    big = lambda halo: pltpu.VMEM((T + halo, LANES), F32)
    mat = pl.BlockSpec((None, LANES, LANES), lambda c: (c, 0, 0))
    return _carry_call(
        body, name=name, steps=nblk,
        in_specs=_mixer_b_specs(T, nblk) + [_seg_spec(T, 1, nblk), _ANY],
        out_specs=[pl.BlockSpec((3, T, LANES), lambda c: (1, 0, c)),
                   pl.BlockSpec((None, SUBLANES, LANES), lambda c: (c, 0, 0)), mat, mat],
        out_shape=[jax.ShapeDtypeStruct(dproj.shape, dproj.dtype), jax.ShapeDtypeStruct((nblk, SUBLANES, LANES), F32),
                   jax.ShapeDtypeStruct((nblk, LANES, LANES), F32), jax.ShapeDtypeStruct((nblk, LANES, LANES), F32)],
        scratch_shapes=[big(HALO), big(0), big(0), big(0), big(HALO), big(HALO), big(0), big(0), big(HALO)],
        args=(proj, proj, conv_b, bias, wa, ba, wx, bx, lam, dy, dproj), jobs=jobs, aliases={10: 0})


ATT_BLOCK = 128
ATT_GROUP = 3
ATT_TILE = ATT_BLOCK * ATT_GROUP
ATT_UNDERFLOW = -110.0
ATT_UNVISITED = -1e30


def _split_dot(x, m):
    hi = x.astype(BF16)
    lo = (x - hi.astype(F32)).astype(BF16)
    return jnp.dot(hi, m, preferred_element_type=F32) + jnp.dot(lo, m, preferred_element_type=F32)


def _sub(x, j):
    return x[:, j * ATT_BLOCK:(j + 1) * ATT_BLOCK]


def _stack_rows(x):
    return jnp.concatenate([_sub(x, j) for j in range(ATT_GROUP)], axis=0)


def _unstack_rows(x, offsets):
    return jnp.concatenate([x[j * ATT_BLOCK:(j + 1) * ATT_BLOCK, :] + offsets[j] for j in range(ATT_GROUP)], axis=1)


def _att_tile(q, k_ref, q0, qb, it, scale):
    hi = (qb + 1 - ATT_GROUP * it) * ATT_BLOCK
    k0 = pl.multiple_of(jnp.maximum(hi - ATT_TILE, 0), ATT_BLOCK)
    kt = k_ref[pl.ds(k0, ATT_TILE), :]
    z = lax.dot_general(q, kt, DIMS_NT, preferred_element_type=F32) * scale
    key = k0 + lax.broadcasted_iota(jnp.int32, z.shape, 1)
    row = q0 + lax.broadcasted_iota(jnp.int32, z.shape, 0)
    mask = (key < row) & (key < hi)
    n = jnp.where(mask, -(jnp.maximum(z, 0.0) + jnp.log(1.0 + jnp.exp(-jnp.abs(z)))), 0.0)
    return k0, kt, z, mask, n


def _suffix_in_tile(n, upper, run):
    rs = [jnp.sum(_sub(n, j), axis=1, keepdims=True) for j in range(ATT_GROUP)]
    offs = [None] * ATT_GROUP
    offs[ATT_GROUP - 1] = run
    for j in range(ATT_GROUP - 2, -1, -1):
        offs[j] = offs[j + 1] + rs[j + 1]
    return _unstack_rows(_split_dot(_stack_rows(n), upper), offs), offs[0] + rs[0]


def _head_spec(T, seg, heads):
    return pl.BlockSpec((None, T, ATT_HEAD_DIM), lambda h: (seg, 0, h))


def _attention_fwd(name, qkv, jobs=()):
    _, T, D = qkv.shape
    heads = D // ATT_HEAD_DIM
    nq = T // ATT_BLOCK
    assert nq <= LANES and T >= ATT_TILE
    scale = 1.0 / math.sqrt(ATT_HEAD_DIM)

    def body(q_ref, k_ref, v_ref, o_ref, r_ref, acc_s, run_s):
        rr = lax.broadcasted_iota(jnp.int32, (ATT_BLOCK, ATT_BLOCK), 0)
        cc = lax.broadcasted_iota(jnp.int32, (ATT_BLOCK, ATT_BLOCK), 1)
        upper = jnp.where(rr > cc, 1.0, 0.0).astype(BF16)
        lane = lax.broadcasted_iota(jnp.int32, (ATT_BLOCK, LANES), 1)

        def q_loop(qb, _):
            q0 = pl.multiple_of(qb * ATT_BLOCK, ATT_BLOCK)
            q = q_ref[pl.ds(q0, ATT_BLOCK), :]
            acc_s[...] = jnp.zeros_like(acc_s)
            run_s[...] = jnp.zeros_like(run_s)
            r_ref[pl.ds(q0, ATT_BLOCK), :] = jnp.full((ATT_BLOCK, LANES), ATT_UNVISITED, F32)
            n_tiles = (qb + ATT_GROUP) // ATT_GROUP

            def tile(carry):
                it, _ = carry
                k0, _, z, mask, n = _att_tile(q, k_ref, q0, qb, it, scale)
                run = run_s[...]
                suffix, run_next = _suffix_in_tile(n, upper, run)
                w = jnp.where(mask, jnp.exp(z + n + suffix), 0.0)
                acc_s[...] += jnp.dot(w.astype(BF16), v_ref[pl.ds(k0, ATT_TILE), :], preferred_element_type=F32)
                r_ref[pl.ds(q0, ATT_BLOCK), :] = jnp.where(lane == it, run, r_ref[pl.ds(q0, ATT_BLOCK), :])
                run_s[...] = run_next
                return it + 1, jnp.max(run_next) >= ATT_UNDERFLOW

            lax.while_loop(lambda c: (c[0] < n_tiles) & c[1], tile, (jnp.int32(0), jnp.bool_(True)))
            o_ref[pl.ds(q0, ATT_BLOCK), :] = acc_s[...].astype(o_ref.dtype)
            return 0

        lax.fori_loop(0, nq, q_loop, 0)

    return _carry_call(
        body, name=name, steps=heads,
        in_specs=[_head_spec(T, 0, heads), _head_spec(T, 1, heads), _head_spec(T, 2, heads)],
        out_specs=[pl.BlockSpec((T, ATT_HEAD_DIM), lambda h: (0, h)), pl.BlockSpec((None, T, LANES), lambda h: (h, 0, 0))],
        out_shape=[jax.ShapeDtypeStruct((T, D), BF16), jax.ShapeDtypeStruct((heads, T, LANES), F32)],
        scratch_shapes=[pltpu.VMEM((ATT_BLOCK, ATT_HEAD_DIM), F32), pltpu.VMEM((ATT_BLOCK, LANES), F32)],
        args=(qkv, qkv, qkv), jobs=jobs)


def _attention_bwd(name, qkv, do, rmat, jobs=()):
    _, T, D = qkv.shape
    heads = D // ATT_HEAD_DIM
    nq = T // ATT_BLOCK
    scale = 1.0 / math.sqrt(ATT_HEAD_DIM)

    def body(q_ref, k_ref, v_ref, do_ref, r_ref, dqkv_ref, dk_s, dv_s, dq_s, left_s):
        rr = lax.broadcasted_iota(jnp.int32, (ATT_BLOCK, ATT_BLOCK), 0)
        cc = lax.broadcasted_iota(jnp.int32, (ATT_BLOCK, ATT_BLOCK), 1)
        upper = jnp.where(rr > cc, 1.0, 0.0).astype(BF16)
        lower = jnp.where(rr < cc, 1.0, 0.0).astype(BF16)
        lane = lax.broadcasted_iota(jnp.int32, (ATT_BLOCK, LANES), 1)
        dk_s[...] = jnp.zeros_like(dk_s)
        dv_s[...] = jnp.zeros_like(dv_s)

        def q_loop(qb, _):
            q0 = pl.multiple_of(qb * ATT_BLOCK, ATT_BLOCK)
            q = q_ref[pl.ds(q0, ATT_BLOCK), :]
            dov = do_ref[pl.ds(q0, ATT_BLOCK), :]
            dq_s[...] = jnp.zeros_like(dq_s)
            left_s[...] = jnp.zeros_like(left_s)
            rm = r_ref[pl.ds(q0, ATT_BLOCK), :]
            n_tiles = (qb + ATT_GROUP) // ATT_GROUP
            seen = (jnp.max(rm, axis=0, keepdims=True) > 0.5 * ATT_UNVISITED) & (lane[0:1, :] < n_tiles)
            n_visited = jnp.sum(jnp.where(seen, 1.0, 0.0)).astype(jnp.int32)

            def tile(j, _):
                it = n_visited - 1 - j
                k0, kt, z, mask, n = _att_tile(q, k_ref, q0, qb, it, scale)
                vt = v_ref[pl.ds(k0, ATT_TILE), :]
                run = jnp.sum(jnp.where(lane == it, r_ref[pl.ds(q0, ATT_BLOCK), :], 0.0), axis=1, keepdims=True)
                suffix, _ = _suffix_in_tile(n, upper, run)
                s = z + n
                w = jnp.where(mask, jnp.exp(s + suffix), 0.0)
                e = w * lax.dot_general(dov, vt, DIMS_NT, preferred_element_type=F32)
                es = [jnp.sum(_sub(e, g), axis=1, keepdims=True) for g in range(ATT_GROUP)]
                pre = [left_s[...]]
                for g in range(ATT_GROUP):
                    pre.append(pre[g] + es[g])
                before = _unstack_rows(_split_dot(_stack_rows(e), lower), pre)
                sig = jnp.exp(s)
                dz = (jnp.where(mask, e * (1.0 - sig) - before * sig, 0.0) * scale).astype(BF16)
                dq_s[...] += jnp.dot(dz, kt, preferred_element_type=F32)
                dk_s[pl.ds(k0, ATT_TILE), :] += lax.dot_general(dz, q, DIMS_TN, preferred_element_type=F32)
                dv_s[pl.ds(k0, ATT_TILE), :] += lax.dot_general(w.astype(BF16), dov, DIMS_TN, preferred_element_type=F32)
                left_s[...] = pre[ATT_GROUP]
                return 0

            lax.fori_loop(0, n_visited, tile, 0)
            dqkv_ref[0, pl.ds(q0, ATT_BLOCK), :] = dq_s[...].astype(dqkv_ref.dtype)
            return 0

        lax.fori_loop(0, nq, q_loop, 0)
        dqkv_ref[1, :, :] = dk_s[...].astype(dqkv_ref.dtype)
        dqkv_ref[2, :, :] = dv_s[...].astype(dqkv_ref.dtype)

    return _carry_call(
        body, name=name, steps=heads,
        in_specs=[_head_spec(T, 0, heads), _head_spec(T, 1, heads), _head_spec(T, 2, heads),
                  pl.BlockSpec((T, ATT_HEAD_DIM), lambda h: (0, h)), pl.BlockSpec((None, T, LANES), lambda h: (h, 0, 0))],
        out_specs=[pl.BlockSpec((3, T, ATT_HEAD_DIM), lambda h: (0, 0, h))],
        out_shape=[jax.ShapeDtypeStruct((3, T, D), BF16)],
        scratch_shapes=[pltpu.VMEM((T, ATT_HEAD_DIM), F32), pltpu.VMEM((T, ATT_HEAD_DIM), F32),
                        pltpu.VMEM((ATT_BLOCK, ATT_HEAD_DIM), F32), pltpu.VMEM((ATT_BLOCK, LANES), F32)],
        args=(qkv, qkv, qkv, do, rmat), jobs=jobs)


def _block_diag_pairs(w):
    h = w.shape[0]
    wp = w.reshape(h // 2, 2, RG_HEAD_DIM, RG_HEAD_DIM)
    z = jnp.zeros_like(wp[:, 0])
    top = jnp.concatenate([wp[:, 0], z], axis=2)
    bot = jnp.concatenate([z, wp[:, 1]], axis=2)
    return jnp.concatenate([top, bot], axis=1)


def _diag_pairs(g):
    n = g.shape[0]
    a = g[:, :RG_HEAD_DIM, :RG_HEAD_DIM]
    b = g[:, RG_HEAD_DIM:, RG_HEAD_DIM:]
    return jnp.stack([a, b], axis=1).reshape(2 * n, RG_HEAD_DIM, RG_HEAD_DIM)


class _Weights:
    def __init__(self, full, shards=None, plan=None):
        self.full, self.shards, self.plan = dict(full), shards or {}, plan or {}

    def __getitem__(self, name):
        return self.full[name]

    def jobs(self, call):
        return [_gather_job(self.shards[n]) for n in self.plan.get(call, ())]

    def deliver(self, call, outs):
        for n, g in zip(self.plan.get(call, ()), outs):
            self.full[n] = _gathered_layout(n, g)


def _gathered_layout(name, g):
    if name in ("w_in", "w_qkv"):
        return _from_col_blocks(g)
    if name in ("w_out", "w_o"):
        return g.reshape(g.shape[0] * g.shape[1], g.shape[2])
    return g


class _Grads:
    def __init__(self, lands=None, plan=None):
        self.lands, self.plan = dict(lands) if lands else None, plan or {}
        self.ready, self.sent = {}, {}

    def put(self, name, arr):
        self.ready[name] = arr

    def jobs(self, call):
        if self.lands is None:
            return []
        return [_exchange_job(self.ready[n], self.lands[n], part, parts) for n, part, parts in self.plan.get(call, ())]

    def deliver(self, call, outs):
        for (n, part, parts), o in zip(self.plan.get(call, ()), outs):
            self.lands[n] = o
            self.sent.setdefault(n, set()).add((part, parts))

    def flush(self, name):
        if self.lands is None:
            return
        rest = []
        for n in self.ready:
            done = self.sent.get(n, set())
            parts = next(iter(done))[1] if done else 1
            left = [p for p in range(parts) if (p, parts) not in done]
            assert len(left) <= 1, (n, done)
            rest += [(n, p, parts) for p in left]
        if rest:
            outs = _run_jobs(name, [_exchange_job(self.ready[n], self.lands[n], p, parts) for n, p, parts in rest])
            for (n, _, _), o in zip(rest, outs):
                self.lands[n] = o


def _mlp_fwd(tag, h, wts, run):
    T, D = h.shape
    w_up = wts["up" + tag]
    fb = w_up.shape[2]
    F = fb * N_DEV
    tm, tn, tk = _tile(T, 1024), _tile(fb, 1024), _tile(D, MM_TK)
    nb = fb // tn

    def up_epilogue(u):
        r = jnp.maximum(u, 0.0)
        return u, r * r

    o_spec = pl.BlockSpec((tm, tn), lambda i, j, k: (i, j))
    u, act = run(
        _matmul, f"mlp_up_l{tag}",
        [(h, pl.BlockSpec((tm, tk), lambda i, j, k: (i, k))),
         (w_up, pl.BlockSpec((None, tk, tn), lambda i, j, k: (j // nb, k, j % nb)))],
        [(jax.ShapeDtypeStruct((T, F), BF16), o_spec), (jax.ShapeDtypeStruct((T, F), BF16), o_spec)],
        (T // tm, F // tn, D // tk), DIMS_NN, (tm, tn), up_epilogue, n_main=2)
    w_down = wts["down" + tag]
    tn2, tk2 = _tile(D, 1024), _tile(fb, MM_TK)
    nkb = fb // tk2
    m = run(
        _matmul, f"mlp_down_l{tag}",
        [(act, pl.BlockSpec((tm, tk2), lambda i, j, k: (i, k))),
         (w_down, pl.BlockSpec((None, tk2, tn2), lambda i, j, k: (k // nkb, k % nkb, j)))],
        [(jax.ShapeDtypeStruct((T, D), F32), pl.BlockSpec((tm, tn2), lambda i, j, k: (i, j)))],
        (T // tm, D // tn2, F // tk2), DIMS_NN, (tm, tn2), None)
    return u, act, m


def _mlp_bwd(tag, h, u, act, dm, wts, grads, run):
    T, D = h.shape
    w_up, w_down = wts["up" + tag], wts["down" + tag]
    fb = w_up.shape[2]
    F = fb * N_DEV
    grads.put("down" + tag, run(_mm_tn, f"mlp_down_dw_l{tag}", act, dm, BF16).reshape(N_DEV, fb, D))
    tm, tn, tk = _tile(T, 1024), _tile(fb, 1024), _tile(D, MM_TK)
    nb = fb // tn
    o_spec = pl.BlockSpec((tm, tn), lambda i, j, k: (i, j))
    du = run(
        _matmul, f"mlp_down_dx_l{tag}",
        [(dm, pl.BlockSpec((tm, tk), lambda i, j, k: (i, k))),
         (w_down, pl.BlockSpec((None, tn, tk), lambda i, j, k: (j // nb, j % nb, k))),
         (u, o_spec)],
        [(jax.ShapeDtypeStruct((T, F), BF16), o_spec)],
        (T // tm, F // tn, D // tk), DIMS_NT, (tm, tn),
        lambda r, uv: (r * (2.0 * jnp.maximum(uv.astype(F32), 0.0)),))
    grads.put("up" + tag, run(_mm_tn, f"mlp_up_dw_l{tag}", h, du, BF16, out_blocks=N_DEV))
    tn2, tk2 = _tile(D, 1024), _tile(fb, MM_TK)
    nkb = fb // tk2
    return run(
        _matmul, f"mlp_up_dx_l{tag}",
        [(du, pl.BlockSpec((tm, tk2), lambda i, j, k: (i, k))),
         (w_up, pl.BlockSpec((None, tn2, tk2), lambda i, j, k: (k // nkb, j, k % nkb)))],
        [(jax.ShapeDtypeStruct((T, D), F32), pl.BlockSpec((tm, tn2), lambda i, j, k: (i, j)))],
        (T // tm, D // tn2, F // tk2), DIMS_NT, (tm, tn2), None)


def _local_step(x, target, gains, conv_a, conv_b, conv_b_bias, rg_w_a, rg_b_a, rg_w_x, rg_b_x, rg_lambda, wts, grads):
    T, D = x.shape
    g = lambda l, i: gains[l, i][None, :]
    wa_p = _block_diag_pairs(rg_w_a).astype(BF16)
    wx_p = _block_diag_pairs(rg_w_x).astype(BF16)

    def run(fn, name, *args, n_main=1, **kw):
        jw, jg = wts.jobs(name), grads.jobs(name)
        res = fn(name, *args, jobs=jw + jg, **kw)
        main, jo = res[:n_main], res[n_main:]
        wts.deliver(name, jo[:len(jw)])
        grads.deliver(name, jo[len(jw):])
        return main[0] if n_main == 1 else main

    h0 = _norm_fwd("norm_in", x, g(0, 0))
    proj = run(_mm_nn, "w_in_fwd", h0, wts["w_in"], F32, out_seg=5)
    y = _mixer_b_fwd(proj, conv_b, conv_b_bias, wa_p, rg_b_a, wx_p, rg_b_x, rg_lambda, _mixer_a_fwd(proj, conv_a))
    mix0 = run(_mm_nn, "w_out_fwd", y, wts["w_out"], F32, a_seg=2)
    x1, h1 = _resid_norm("resid_mix0", x, mix0, g(0, 1), g(0, 2))
    u0, act0, m0 = _mlp_fwd("0", h1, wts, run)
    x2, h2 = _resid_norm("resid_mlp0", x1, m0, g(0, 3), g(1, 0))
    qkv = run(_mm_nn, "w_qkv_fwd", h2, wts["w_qkv"], BF16, out_seg=3)
    o, rmat = run(_attention_fwd, "attention_fwd", qkv, n_main=2)
    mix1 = run(_mm_nn, "w_o_fwd", o, wts["w_o"], F32)
    x3, h3 = _resid_norm("resid_mix1", x2, mix1, g(1, 1), g(1, 2))
    u1, act1, m1 = _mlp_fwd("1", h3, wts, run)
    dx4, sq = _final_loss("loss", x3, m1, g(1, 3), target)

    dm1, dg13 = _norm_bwd("norm_bwd_m1", m1, g(1, 3), dx4, None, BF16)
    dh3 = _mlp_bwd("1", h3, u1, act1, dm1, wts, grads, run)
    dx3, dg12 = _norm_bwd("norm_bwd_x3", x3, g(1, 2), dh3, dx4, F32)
    dmix1, dg11 = _norm_bwd("norm_bwd_mix1", mix1, g(1, 1), dx3, None, BF16)
    grads.put("w_o", run(_mm_tn, "w_o_dw", o, dmix1, BF16).reshape(N_DEV, D // N_DEV, D))
    do = run(_mm_nt, "w_o_dx", dmix1, wts["w_o"], BF16)
    dqkv = run(_attention_bwd, "attention_bwd", qkv, do, rmat)
    grads.put("w_qkv", _col_blocks(run(_mm_tn, "w_qkv_dw", h2, dqkv, BF16, b_seg=3), N_DEV))
    dh2 = run(_mm_nt, "w_qkv_dx", dqkv, wts["w_qkv"], F32, a_seg=3)
    dx2, dg10 = _norm_bwd("norm_bwd_x2", x2, g(1, 0), dh2, dx3, F32)
    dm0, dg03 = _norm_bwd("norm_bwd_m0", m0, g(0, 3), dx2, None, BF16)
    dh1 = _mlp_bwd("0", h1, u0, act0, dm0, wts, grads, run)
    dx1, dg02 = _norm_bwd("norm_bwd_x1", x1, g(0, 2), dh1, dx2, F32)
    dmix0, dg01 = _norm_bwd("norm_bwd_mix0", mix0, g(0, 1), dx1, None, BF16)
    grads.put("w_out", run(_mm_tn, "w_out_dw", y, dmix0, BF16, a_seg=2).reshape(N_DEV, D // N_DEV, D))
    dy = run(_mm_nt, "w_out_dx", dmix0, wts["w_out"], F32, out_seg=2)
    dproj_a, dconv_a = _mixer_a_bwd(proj, conv_a, dy)
    dproj, sm_b, dwa_p, dwx_p = run(_mixer_b_bwd, "mixer_b_bwd", proj, conv_b, conv_b_bias, wa_p, rg_b_a, wx_p, rg_b_x,
                                    rg_lambda, dy, dproj_a, n_main=4)
    grads.put("w_in", _col_blocks(run(_mm_tn, "w_in_dw", h0, dproj, BF16, b_seg=5), N_DEV))
    dh0 = run(_mm_nt, "w_in_dx", dproj, wts["w_in"], F32, a_seg=5)
    dx0, dg00 = _norm_bwd("norm_bwd_x0", x, g(0, 0), dh0, dx1, F32)

    C = D // 2
    lanes_to_vec = lambda t, row: t[:, row, :].reshape(1, C)
    small = {
        "norm_gains": jnp.concatenate([dg00, dg01, dg02, dg03, dg10, dg11, dg12, dg13], axis=0).reshape(2, 4, D),
        "conv_a": jnp.transpose(dconv_a[:, :3, :], (1, 0, 2)).reshape(3, C),
        "conv_b": jnp.transpose(sm_b[:, :4, :], (1, 0, 2)).reshape(4, C),
        "conv_b_bias": lanes_to_vec(sm_b, _ROW_BIAS),
        "rg_w_a": _diag_pairs(dwa_p),
        "rg_b_a": lanes_to_vec(sm_b, _ROW_BA),
        "rg_w_x": _diag_pairs(dwx_p),
        "rg_b_x": lanes_to_vec(sm_b, _ROW_BX),
        "rg_lambda": lanes_to_vec(sm_b, _ROW_LAM),
    }
    return sq[0, 0], dx0, small


def _my_index():
    return 4 * lax.axis_index("x") + 2 * lax.axis_index("y") + lax.axis_index("c")


def _peers():
    x, y, c = lax.axis_index("x"), lax.axis_index("y"), lax.axis_index("c")
    out = []
    for k in range(1, N_DEV):
        px = x ^ ((k >> 2) & 1)
        py = y ^ ((k >> 1) & 1)
        pc = c ^ (k & 1)
        out.append(((px, py, pc), 4 * px + 2 * py + pc))
    return out


GATHER_FIRST = ("w_in", "w_out")
GATHER_PLAN = {"w_in_fwd": ("up0",), "mlp_up_l0": ("down0",), "mlp_down_l0": ("w_qkv", "w_o"),
               "attention_fwd": ("up1", "down1")}
EXCHANGE_PLAN = {
    "mlp_down_dx_l1": (("down1", 0, 2),), "mlp_up_dw_l1": (("down1", 1, 2),),
    "mlp_up_dx_l1": (("up1", 0, 2),), "attention_bwd": (("up1", 1, 2), ("w_o", 0, 1)),
    "w_qkv_dx": (("w_qkv", 0, 2),), "mlp_down_dw_l0": (("w_qkv", 1, 2),),
    "mlp_down_dx_l0": (("down0", 0, 2),), "mlp_up_dw_l0": (("down0", 1, 2),),
    "mlp_up_dx_l0": (("up0", 0, 2),), "mixer_b_bwd": (("up0", 1, 2),),
    "w_in_dw": (("w_out", 0, 1),), "w_in_dx": (("w_in", 0, 2),),
}


def _all_gather(name, shards):
    n = len(shards)

    def body(*refs):
        srcs, dsts = refs[:n], refs[n:2 * n]
        send_sems, recv_sems, local_sems = refs[2 * n:]
        me = _my_index()
        peers = _peers()
        copies = []
        for a in range(n):
            lc = pltpu.make_async_copy(srcs[a], dsts[a].at[me], local_sems.at[a])
            lc.start()
            copies.append(lc)
        remote = []
        for a in range(n):
            for k, (pos, _) in enumerate(peers):
                cp = pltpu.make_async_remote_copy(
                    src_ref=srcs[a], dst_ref=dsts[a].at[me], send_sem=send_sems.at[a, k], recv_sem=recv_sems.at[a, k],
                    device_id=pos, device_id_type=MESH)
                cp.start()
                remote.append(cp)
        for a in range(n):
            for k, (pos, idx) in enumerate(peers):
                pltpu.make_async_remote_copy(
                    src_ref=srcs[a], dst_ref=dsts[a].at[idx], send_sem=send_sems.at[a, k], recv_sem=recv_sems.at[a, k],
                    device_id=pos, device_id_type=MESH).wait_recv()
        for cp in remote:
            cp.wait_send()
        for lc in copies:
            lc.wait()

    return pl.pallas_call(
        body, name=name,
        in_specs=[_ANY] * n, out_specs=[_ANY] * n,
        out_shape=[jax.ShapeDtypeStruct((N_DEV,) + s.shape, s.dtype) for s in shards],
        scratch_shapes=[pltpu.SemaphoreType.DMA((n, N_DEV - 1)), pltpu.SemaphoreType.DMA((n, N_DEV - 1)),
                        pltpu.SemaphoreType.DMA((n,))],
    )(*shards)


def _job_sems():
    return [pltpu.SemaphoreType.DMA((N_DEV - 1,)), pltpu.SemaphoreType.DMA((N_DEV - 1,)), pltpu.SemaphoreType.DMA((1,))]


def _gather_job(shard):
    def ctx():
        x, y, c = lax.axis_index("x"), lax.axis_index("y"), lax.axis_index("c")
        chips = [(1 - x, y), (x, 1 - y), (1 - x, 1 - y)]
        return x, y, c, chips

    def idx(px, py, pc):
        return 4 * px + 2 * py + pc

    def copy(src, out, sems, k, block, to):
        return pltpu.make_async_remote_copy(
            src_ref=out.at[block] if src is None else src, dst_ref=out.at[block], send_sem=sems[0].at[k],
            recv_sem=sems[1].at[k], device_id=to, device_id_type=MESH)

    def start(ins, outs, sems):
        x, y, c, chips = ctx()
        src, out = ins[0], outs[0]
        me = idx(x, y, c)
        pltpu.make_async_copy(src, out.at[me], sems[2].at[0]).start()
        copy(src, out, sems, 0, me, (x, y, 1 - c)).start()
        for j, (px, py) in enumerate(chips):
            copy(src, out, sems, 1 + j, me, (px, py, c)).start()

    def mid(ins, outs, sems):
        x, y, c, chips = ctx()
        out = outs[0]
        for j, (px, py) in enumerate(chips):
            copy(None, out, sems, 1 + j, idx(px, py, c), (x, y, c)).wait_recv()
            copy(None, out, sems, 4 + j, idx(px, py, c), (x, y, 1 - c)).start()

    def end(ins, outs, sems):
        x, y, c, chips = ctx()
        src, out = ins[0], outs[0]
        me = (x, y, c)
        copy(None, out, sems, 0, idx(x, y, 1 - c), me).wait_recv()
        for j, (px, py) in enumerate(chips):
            copy(None, out, sems, 4 + j, idx(px, py, 1 - c), me).wait_recv()
        for k in range(N_DEV - 1):
            copy(src, out, sems, k, idx(x, y, c), me).wait_send()
        pltpu.make_async_copy(src, out.at[idx(x, y, c)], sems[2].at[0]).wait()

    return _Job([shard], [jax.ShapeDtypeStruct((N_DEV,) + shard.shape, shard.dtype)], _job_sems(), start, mid, end)


def _exchange_job(src, land, part=0, parts=1):
    rows = src.shape[1] // parts
    assert rows * parts == src.shape[1]

    def sl(ref, s):
        return ref.at[s, pl.ds(part * rows, rows)]

    def start(ins, outs, sems):
        me = _my_index()
        pltpu.make_async_copy(sl(ins[0], me), sl(outs[0], me), sems[2].at[0]).start()
        for k, (pos, idx) in enumerate(_peers()):
            pltpu.make_async_remote_copy(
                src_ref=sl(ins[0], idx), dst_ref=sl(outs[0], me), send_sem=sems[0].at[k], recv_sem=sems[1].at[k],
                device_id=pos, device_id_type=MESH).start()

    def mid(ins, outs, sems):
        pass

    def end(ins, outs, sems):
        me = _my_index()
        for k, (pos, idx) in enumerate(_peers()):
            cp = pltpu.make_async_remote_copy(
                src_ref=sl(ins[0], idx), dst_ref=sl(outs[0], idx), send_sem=sems[0].at[k], recv_sem=sems[1].at[k],
                device_id=pos, device_id_type=MESH)
            cp.wait_recv()
            cp.wait_send()
        pltpu.make_async_copy(sl(ins[0], me), sl(outs[0], me), sems[2].at[0]).wait()

    return _Job([src, land], [jax.ShapeDtypeStruct(land.shape, land.dtype)], _job_sems(), start, mid, end, alias={1: 0})


def _adamw_math(w, g, m, v):
    m = ADAM_B1 * m + (1.0 - ADAM_B1) * g
    v = ADAM_B2 * v + (1.0 - ADAM_B2) * (g * g)
    m_hat = m / (1.0 - ADAM_B1 ** ADAM_STEP)
    v_hat = v / (1.0 - ADAM_B2 ** ADAM_STEP)
    delta = -ADAM_LR * (m_hat / (jnp.sqrt(v_hat) + ADAM_EPS) + ADAM_WD * w)
    return delta, m, v


def _sum_slots(ref):
    g = ref[0].astype(F32)
    for s in range(1, N_DEV):
        g = g + ref[s].astype(F32)
    return g


def _adamw_big(name, lands, w, m, v):
    L, R, C = w.shape
    assert len(lands) == L
    tr = _tile(R, max(LANES, (256 * 1024) // C))

    def body(*refs):
        l_refs = refs[:L]
        w_ref, m_ref, v_ref, g_ref, d_ref, nm_ref, nv_ref = refs[L:]
        for li in range(L):
            @pl.when(pl.program_id(0) == li)
            def _(li=li):
                g = _sum_slots(l_refs[li])
                d, nm, nv = _adamw_math(w_ref[...], g, m_ref[...], v_ref[...])
                g_ref[...] = g
                d_ref[...] = d
                nm_ref[...] = nm
                nv_ref[...] = nv

    def land_spec(li):
        return pl.BlockSpec((N_DEV, tr, C), lambda l, i: (0, jnp.where(l == li, i, 0), 0))

    row = pl.BlockSpec((None, tr, C), lambda l, i: (l, i, 0))
    return pl.pallas_call(
        body, name=name, grid=(L, R // tr),
        in_specs=[land_spec(li) for li in range(L)] + [row, row, row],
        out_specs=[row] * 4, out_shape=[jax.ShapeDtypeStruct((L, R, C), F32)] * 4,
        compiler_params=_params(("arbitrary", "arbitrary")))(*lands, w, m, v)


def _sum8(name, slots):
    _, R, C = slots.shape

    def body(s_ref, o_ref):
        o_ref[...] = _sum_slots(s_ref)

    return pl.pallas_call(body, name=name, out_shape=jax.ShapeDtypeStruct((R, C), F32))(slots)


def _adamw_small(name, g, w, m, v):
    def body(g_ref, w_ref, m_ref, v_ref, d_ref, nm_ref, nv_ref):
        d, nm, nv = _adamw_math(w_ref[...], g_ref[...], m_ref[...], v_ref[...])
        d_ref[...] = d
        nm_ref[...] = nm
        nv_ref[...] = nv

    return pl.pallas_call(body, name=name, out_shape=[jax.ShapeDtypeStruct(w.shape, F32)] * 3)(g, w, m, v)


def _pack_rows(arrs):
    parts, spans, r0 = [], [], 0
    for a in arrs:
        flat = a.astype(F32).reshape(-1)
        rows = -(-flat.shape[0] // LANES)
        rows = -(-rows // SUBLANES) * SUBLANES
        flat = jnp.pad(flat, (0, rows * LANES - flat.shape[0]))
        parts.append(flat.reshape(rows, LANES))
        spans.append((r0, rows, a.shape))
        r0 += rows
    return jnp.concatenate(parts, axis=0), spans


def _unpack_rows(buf, span):
    r0, rows, shape = span
    n = math.prod(shape)
    return buf[..., r0:r0 + rows, :].reshape(buf.shape[:-2] + (rows * LANES,))[..., :n].reshape(buf.shape[:-2] + shape)


def _col_blocks(w, n_blocks):
    K, N = w.shape
    return jnp.transpose(w.reshape(K, n_blocks, N // n_blocks), (1, 0, 2))


def _from_col_blocks(wb):
    B, K, n = wb.shape
    return jnp.transpose(wb, (1, 0, 2)).reshape(K, B * n)


def kernel(x, norm_gains, hyb_w_in, hyb_conv_a, hyb_conv_b, hyb_conv_b_bias, hyb_rg_w_a, hyb_rg_b_a, hyb_rg_w_x, hyb_rg_b_x, hyb_rg_lambda, hyb_w_out, sb_w_qkv, sb_w_o, mlp_w_up, mlp_w_down, loss_target, m_norm_gains, m_hyb_w_in, m_hyb_conv_a, m_hyb_conv_b, m_hyb_conv_b_bias, m_hyb_rg_w_a, m_hyb_rg_b_a, m_hyb_rg_w_x, m_hyb_rg_b_x, m_hyb_rg_lambda, m_hyb_w_out, m_sb_w_qkv, m_sb_w_o, m_mlp_w_up, m_mlp_w_down, v_norm_gains, v_hyb_w_in, v_hyb_conv_a, v_hyb_conv_b, v_hyb_conv_b_bias, v_hyb_rg_w_a, v_hyb_rg_b_a, v_hyb_rg_w_x, v_hyb_rg_b_x, v_hyb_rg_lambda, v_hyb_w_out, v_sb_w_qkv, v_sb_w_o, v_mlp_w_up, v_mlp_w_down):
    T, D = x.shape[1], x.shape[2]
    me = _my_index()

    small_shards, small_spans = _pack_rows([norm_gains, hyb_conv_a[0], hyb_conv_b[0]])
    (small_all,) = _all_gather("gather_small", [small_shards])
    gains_b = _unpack_rows(small_all, small_spans[0])
    gains = jnp.transpose(gains_b, (1, 2, 0, 3)).reshape(2, 4, D)
    conv_a = _from_col_blocks(_unpack_rows(small_all, small_spans[1]))
    conv_b = _from_col_blocks(_unpack_rows(small_all, small_spans[2]))

    shards = {"w_in": hyb_w_in[0], "w_out": hyb_w_out[0], "w_qkv": sb_w_qkv[0], "w_o": sb_w_o[0],
              "up0": mlp_w_up[0], "up1": mlp_w_up[1], "down0": mlp_w_down[0], "down1": mlp_w_down[1]}
    shards = {n: s.astype(BF16) for n, s in shards.items()}
    first = _run_jobs("gather_first", [_gather_job(shards[n]) for n in GATHER_FIRST])
    wts = _Weights({n: _gathered_layout(n, g) for n, g in zip(GATHER_FIRST, first)}, shards, GATHER_PLAN)
    grads_big = _Grads({n: lax.empty((N_DEV,) + s.shape, BF16) for n, s in shards.items()}, EXCHANGE_PLAN)

    sq, grad_x, small = _local_step(
        x[0], loss_target[0], gains, conv_a, conv_b, hyb_conv_b_bias, hyb_rg_w_a[0], hyb_rg_b_a, hyb_rg_w_x[0],
        hyb_rg_b_x, hyb_rg_lambda, wts, grads_big)
    loss = lax.psum(0.5 * sq / D, ("x", "y", "c"))

    grads_big.flush("exchange_grads")
    land = grads_big.lands

    names = ["norm_gains", "hyb_w_in", "hyb_conv_a", "hyb_conv_b", "hyb_conv_b_bias", "hyb_rg_w_a", "hyb_rg_b_a",
             "hyb_rg_w_x", "hyb_rg_b_x", "hyb_rg_lambda", "hyb_w_out", "sb_w_qkv", "sb_w_o", "mlp_w_up", "mlp_w_down"]
    params = dict(zip(names, [norm_gains, hyb_w_in, hyb_conv_a, hyb_conv_b, hyb_conv_b_bias, hyb_rg_w_a, hyb_rg_b_a,
                              hyb_rg_w_x, hyb_rg_b_x, hyb_rg_lambda, hyb_w_out, sb_w_qkv, sb_w_o, mlp_w_up, mlp_w_down]))
    moms = dict(zip(names, [m_norm_gains, m_hyb_w_in, m_hyb_conv_a, m_hyb_conv_b, m_hyb_conv_b_bias, m_hyb_rg_w_a,
                            m_hyb_rg_b_a, m_hyb_rg_w_x, m_hyb_rg_b_x, m_hyb_rg_lambda, m_hyb_w_out, m_sb_w_qkv,
                            m_sb_w_o, m_mlp_w_up, m_mlp_w_down]))
    vars_ = dict(zip(names, [v_norm_gains, v_hyb_w_in, v_hyb_conv_a, v_hyb_conv_b, v_hyb_conv_b_bias, v_hyb_rg_w_a,
                             v_hyb_rg_b_a, v_hyb_rg_w_x, v_hyb_rg_b_x, v_hyb_rg_lambda, v_hyb_w_out, v_sb_w_qkv,
                             v_sb_w_o, v_mlp_w_up, v_mlp_w_down]))
    grads, deltas, new_m, new_v = {}, {}, {}, {}

    big_lands = {"hyb_w_in": ["w_in"], "hyb_w_out": ["w_out"], "sb_w_qkv": ["w_qkv"], "sb_w_o": ["w_o"],
                 "mlp_w_up": ["up0", "up1"], "mlp_w_down": ["down0", "down1"]}
    for nm, keys in big_lands.items():
        outs = _adamw_big(f"adamw_{nm}", [land[k] for k in keys], params[nm], moms[nm], vars_[nm])
        grads[nm], deltas[nm], new_m[nm], new_v[nm] = outs

    small_names = ["norm_gains", "hyb_conv_a", "hyb_conv_b", "hyb_conv_b_bias", "hyb_rg_w_a", "hyb_rg_b_a",
                   "hyb_rg_w_x", "hyb_rg_b_x", "hyb_rg_lambda"]
    small_keys = ["norm_gains", "conv_a", "conv_b", "conv_b_bias", "rg_w_a", "rg_b_a", "rg_w_x", "rg_b_x", "rg_lambda"]
    sg_buf, sg_spans = _pack_rows([small[k] for k in small_keys])
    (sg_all,) = _all_gather("gather_small_grads", [sg_buf])
    sg_sum = _sum8("sum_small_grads", sg_all)
    full = {nm: _unpack_rows(sg_sum, sp) for nm, sp in zip(small_names, sg_spans)}
    cb = (D // 2) // N_DEV
    small_grads = {
        "norm_gains": lax.dynamic_slice_in_dim(full["norm_gains"], me * (D // N_DEV), D // N_DEV, axis=2),
        "hyb_conv_a": lax.dynamic_slice_in_dim(full["hyb_conv_a"], me * cb, cb, axis=1)[None],
        "hyb_conv_b": lax.dynamic_slice_in_dim(full["hyb_conv_b"], me * cb, cb, axis=1)[None],
        "hyb_conv_b_bias": full["hyb_conv_b_bias"],
        "hyb_rg_w_a": full["hyb_rg_w_a"][None],
        "hyb_rg_b_a": full["hyb_rg_b_a"],
        "hyb_rg_w_x": full["hyb_rg_w_x"][None],
        "hyb_rg_b_x": full["hyb_rg_b_x"],
        "hyb_rg_lambda": full["hyb_rg_lambda"],
    }
    pk = lambda d: _pack_rows([d[nm] for nm in small_names])
    g_buf, spans = pk(small_grads)
    w_buf, _ = pk(params)
    m_buf, _ = pk(moms)
    v_buf, _ = pk(vars_)
    d_buf, nm_buf, nv_buf = _adamw_small("adamw_small", g_buf, w_buf, m_buf, v_buf)
    for nm, sp in zip(small_names, spans):
        grads[nm] = small_grads[nm]
        deltas[nm], new_m[nm], new_v[nm] = _unpack_rows(d_buf, sp), _unpack_rows(nm_buf, sp), _unpack_rows(nv_buf, sp)

    return (loss, grad_x[None], *[grads[n] for n in names], *[deltas[n] for n in names],
            *[new_m[n] for n in names], *[new_v[n] for n in names])
```

```python
import functools
import math

import jax
import jax.numpy as jnp
from jax import lax
from jax.experimental import pallas as pl
from jax.experimental.pallas import tpu as pltpu

F32 = jnp.float32
BF16 = jnp.bfloat16

NORM_EPS = 1e-6
LRU_C = 8.0
ATT_HEAD_DIM = 128
RG_HEAD_DIM = 64
LANES = 128
SUBLANES = 8
N_DEV = 8
ADAM_LR = 0.001
ADAM_B1 = 0.9
ADAM_B2 = 0.999
ADAM_EPS = 1e-08
ADAM_WD = 0.01
ADAM_STEP = 10
VMEM_LIMIT = 56 * 1024 * 1024
MM_TK = 2048
MESH = pl.DeviceIdType.MESH


def _tile(n, pref):
    if n <= pref:
        return n
    t = (pref // LANES) * LANES
    while t > LANES and n % t:
        t -= LANES
    assert n % t == 0, (n, pref)
    return t


def _params(sem):
    return pltpu.CompilerParams(dimension_semantics=sem, vmem_limit_bytes=VMEM_LIMIT)


DIMS_NN = (((1,), (0,)), ((), ()))
DIMS_NT = (((1,), (1,)), ((), ()))
DIMS_TN = (((0,), (0,)), ((), ()))


_ANY = pl.BlockSpec(memory_space=pl.ANY)


class _Job:
    def __init__(self, ins, outs, sems, start, mid, end, alias=None):
        self.ins, self.outs, self.sems = ins, outs, sems
        self.start, self.mid, self.end = start, mid, end
        self.alias = alias or {}


def _job_plumbing(jobs, n_in, n_out):
    j_ins = [a for jb in jobs for a in jb.ins]
    j_outs = [o for jb in jobs for o in jb.outs]
    j_sems = [s for jb in jobs for s in jb.sems]
    aliases, pi, po = {}, 0, 0
    for jb in jobs:
        for i_in, i_out in jb.alias.items():
            aliases[n_in + pi + i_in] = n_out + po + i_out
        pi += len(jb.ins)
        po += len(jb.outs)
    return j_ins, j_outs, j_sems, aliases


def _job_phase(jobs, which, jin, jout, jsem):
    pi = po = ps = 0
    for jb in jobs:
        getattr(jb, which)(jin[pi:pi + len(jb.ins)], jout[po:po + len(jb.outs)], jsem[ps:ps + len(jb.sems)])
        pi, po, ps = pi + len(jb.ins), po + len(jb.outs), ps + len(jb.sems)


def _run_jobs(name, jobs):
    j_ins, j_outs, j_sems, aliases = _job_plumbing(jobs, 0, 0)
    n_ji, n_jo = len(j_ins), len(j_outs)

    def body(*refs):
        jin, jout, jsem = refs[:n_ji], refs[n_ji:n_ji + n_jo], refs[n_ji + n_jo:]
        for which in ("start", "mid", "end"):
            _job_phase(jobs, which, jin, jout, jsem)

    return pl.pallas_call(body, name=name, in_specs=[_ANY] * n_ji, out_specs=[_ANY] * n_jo, out_shape=j_outs,
                          scratch_shapes=j_sems, input_output_aliases=aliases)(*j_ins)


def _carry_call(body, *, name, steps, in_specs, out_specs, out_shape, scratch_shapes, args, jobs=(), aliases=None):
    n_in, n_out, n_sc = len(in_specs), len(out_shape), len(scratch_shapes)
    j_ins, j_outs, j_sems, j_aliases = _job_plumbing(jobs, n_in, n_out)
    n_ji, n_jo = len(j_ins), len(j_outs)

    def wrapped(*refs):
        ins, jin = refs[:n_in], refs[n_in:n_in + n_ji]
        o0 = n_in + n_ji
        outs, jout = refs[o0:o0 + n_out], refs[o0 + n_out:o0 + n_out + n_jo]
        s0 = o0 + n_out + n_jo
        scratch, jsem = refs[s0:s0 + n_sc], refs[s0 + n_sc:]
        step = pl.program_id(0)
        if jobs:
            pl.when(step == 0)(lambda: _job_phase(jobs, "start", jin, jout, jsem))
            pl.when(step == (4 * steps) // 5)(lambda: _job_phase(jobs, "mid", jin, jout, jsem))
        body(*ins, *outs, *scratch)
        if jobs:
            pl.when(step == steps - 1)(lambda: _job_phase(jobs, "end", jin, jout, jsem))

    return pl.pallas_call(
        wrapped, name=name, grid=(steps,),
        in_specs=list(in_specs) + [_ANY] * n_ji, out_specs=list(out_specs) + [_ANY] * n_jo,
        out_shape=list(out_shape) + j_outs, scratch_shapes=list(scratch_shapes) + j_sems,
        input_output_aliases={**(aliases or {}), **j_aliases},
        compiler_params=_params(("arbitrary",) if jobs else ("parallel",)))(*args, *j_ins)


def _matmul(name, ins, outs, grid, dims, acc_shape, epilogue=None, jobs=()):
    n_in, n_out, nk = len(ins), len(outs), grid[2]
    j_ins, j_outs, j_sems, aliases = _job_plumbing(jobs, n_in, n_out)
    n_ji, n_jo = len(j_ins), len(j_outs)
    total = grid[0] * grid[1] * grid[2]
    n_acc = 0 if nk == 1 else 1

    def body(*refs):
        a_ref, b_ref = refs[0], refs[1]
        extras = refs[2:n_in]
        jin = refs[n_in:n_in + n_ji]
        out_refs = refs[n_in + n_ji:n_in + n_ji + n_out]
        jout = refs[n_in + n_ji + n_out:n_in + n_ji + n_out + n_jo]
        jsem = refs[n_in + n_ji + n_out + n_jo + n_acc:]
        k = pl.program_id(2)
        step = (pl.program_id(0) * grid[1] + pl.program_id(1)) * grid[2] + k
        if jobs:
            pl.when(step == 0)(lambda: _job_phase(jobs, "start", jin, jout, jsem))
            pl.when(step == (4 * total) // 5)(lambda: _job_phase(jobs, "mid", jin, jout, jsem))

        def finish(r):
            res = epilogue(r, *[e[...] for e in extras]) if epilogue is not None else (r,)
            for o, v in zip(out_refs, res):
                o[...] = v.astype(o.dtype)

        prod = lax.dot_general(a_ref[...], b_ref[...], dims, preferred_element_type=F32)
        if nk == 1:
            finish(prod)
        else:
            acc = refs[n_in + n_ji + n_out + n_jo]

            @pl.when(k == 0)
            def _():
                acc[...] = prod

            @pl.when((k > 0) & (k < nk - 1))
            def _():
                acc[...] += prod

            @pl.when(k == nk - 1)
            def _():
                finish(acc[...] + prod)

        if jobs:
            pl.when(step == total - 1)(lambda: _job_phase(jobs, "end", jin, jout, jsem))

    sem = ("arbitrary",) * 3 if jobs else ("parallel", "parallel", "arbitrary")
    res = pl.pallas_call(
        body, name=name, grid=grid,
        in_specs=[s for _, s in ins] + [_ANY] * n_ji,
        out_specs=[s for _, s in outs] + [_ANY] * n_jo,
        out_shape=[s for s, _ in outs] + j_outs,
        scratch_shapes=[pltpu.VMEM(acc_shape, F32)] * n_acc + j_sems,
        input_output_aliases=aliases,
        compiler_params=_params(sem),
    )(*[a for a, _ in ins], *j_ins)
    return res


def _mm_nn(name, a, b, out_dtype, *, a_seg=None, out_seg=None, tm=1024, tn=1024, tk=MM_TK, epilogue=None,
           extras=(), n_out=1, out_dtypes=None, jobs=()):
    if a_seg:
        _, M, ks = a.shape
        K = ks * a_seg
    else:
        M, K = a.shape
        ks = K
    N = b.shape[1]
    ns = N // out_seg if out_seg else N
    tm, tn, tk = _tile(M, tm), _tile(ns, tn), _tile(ks, tk)
    nks, nns = ks // tk, ns // tn
    grid = (M // tm, N // tn, K // tk)
    if a_seg:
        a_spec = pl.BlockSpec((None, tm, tk), lambda i, j, k: (k // nks, i, k % nks))
    else:
        a_spec = pl.BlockSpec((tm, tk), lambda i, j, k: (i, k))
    b_spec = pl.BlockSpec((tk, tn), lambda i, j, k: (k, j))
    if out_seg:
        o_spec = pl.BlockSpec((None, tm, tn), lambda i, j, k: (j // nns, i, j % nns))
        o_shape = (out_seg, M, ns)
    else:
        o_spec = pl.BlockSpec((tm, tn), lambda i, j, k: (i, j))
        o_shape = (M, N)
    dts = out_dtypes or [out_dtype] * n_out
    outs = [(jax.ShapeDtypeStruct(o_shape, dt), o_spec) for dt in dts]
    ins = [(a, a_spec), (b, b_spec)] + [(e, o_spec) for e in extras]
    return _matmul(name, ins, outs, grid, DIMS_NN, (tm, tn), epilogue, jobs)


def _mm_nt(name, a, b, out_dtype, *, a_seg=None, out_seg=None, tm=1024, tn=1024, tk=MM_TK, epilogue=None, extras=(),
           jobs=()):
    if a_seg:
        _, M, ks = a.shape
        K = ks * a_seg
    else:
        M, K = a.shape
        ks = K
    N = b.shape[0]
    ns = N // out_seg if out_seg else N
    tm, tn, tk = _tile(M, tm), _tile(ns, tn), _tile(ks, tk)
    nks, nns = ks // tk, ns // tn
    grid = (M // tm, N // tn, K // tk)
    if a_seg:
        a_spec = pl.BlockSpec((None, tm, tk), lambda i, j, k: (k // nks, i, k % nks))
    else:
        a_spec = pl.BlockSpec((tm, tk), lambda i, j, k: (i, k))
    b_spec = pl.BlockSpec((tn, tk), lambda i, j, k: (j, k))
    if out_seg:
        o_spec = pl.BlockSpec((None, tm, tn), lambda i, j, k: (j // nns, i, j % nns))
        o_shape = (out_seg, M, ns)
    else:
        o_spec = pl.BlockSpec((tm, tn), lambda i, j, k: (i, j))
        o_shape = (M, N)
    outs = [(jax.ShapeDtypeStruct(o_shape, out_dtype), o_spec)]
    ins = [(a, a_spec), (b, b_spec)] + [(e, o_spec) for e in extras]
    return _matmul(name, ins, outs, grid, DIMS_NT, (tm, tn), epilogue, jobs)


def _mm_tn(name, a, b, out_dtype, *, a_seg=None, b_seg=None, out_blocks=None, tm=1024, tn=1024, tk=MM_TK, jobs=()):
    if a_seg:
        _, T, ms = a.shape
        M = ms * a_seg
    else:
        T, M = a.shape
        ms = M
    if b_seg:
        _, _, ns = b.shape
        N = ns * b_seg
    else:
        N = b.shape[1]
        ns = N
    nb_cols = N // out_blocks if out_blocks else N
    tm, tk = _tile(ms, tm), _tile(T, tk)
    tn = _tile(math.gcd(ns, nb_cols), tn)
    nms, nns, nbs = ms // tm, ns // tn, nb_cols // tn
    grid = (M // tm, N // tn, T // tk)
    if a_seg:
        a_spec = pl.BlockSpec((None, tk, tm), lambda i, j, k: (i // nms, k, i % nms))
    else:
        a_spec = pl.BlockSpec((tk, tm), lambda i, j, k: (k, i))
    if b_seg:
        b_spec = pl.BlockSpec((None, tk, tn), lambda i, j, k: (j // nns, k, j % nns))
    else:
        b_spec = pl.BlockSpec((tk, tn), lambda i, j, k: (k, j))
    if out_blocks:
        o_spec = pl.BlockSpec((None, tm, tn), lambda i, j, k: (j // nbs, i, j % nbs))
        o_shape = (out_blocks, M, nb_cols)
    else:
        o_spec = pl.BlockSpec((tm, tn), lambda i, j, k: (i, j))
        o_shape = (M, N)
    outs = [(jax.ShapeDtypeStruct(o_shape, out_dtype), o_spec)]
    return _matmul(name, [(a, a_spec), (b, b_spec)], outs, grid, DIMS_TN, (tm, tn), None, jobs)


def _rms(x):
    return lax.rsqrt(jnp.mean(x * x, axis=-1, keepdims=True) + NORM_EPS)


def _row_tile(T):
    return _tile(T, 256)


def _norm_fwd(name, x, g):
    T, D = x.shape
    tr = _row_tile(T)

    def body(x_ref, g_ref, h_ref):
        xv = x_ref[...]
        h_ref[...] = (xv * _rms(xv) * g_ref[...]).astype(h_ref.dtype)

    row = pl.BlockSpec((tr, D), lambda i: (i, 0))
    vec = pl.BlockSpec((1, D), lambda i: (0, 0))
    return pl.pallas_call(body, name=name, grid=(T // tr,), in_specs=[row, vec], out_specs=row,
                          out_shape=jax.ShapeDtypeStruct((T, D), BF16), compiler_params=_params(("parallel",)))(x, g)


def _resid_norm(name, x, br, g_post, g_next):
    T, D = x.shape
    tr = _row_tile(T)

    def body(x_ref, br_ref, gp_ref, gn_ref, xn_ref, h_ref):
        b = br_ref[...]
        xn = x_ref[...] + b * _rms(b) * gp_ref[...]
        xn_ref[...] = xn
        h_ref[...] = (xn * _rms(xn) * gn_ref[...]).astype(h_ref.dtype)

    row = pl.BlockSpec((tr, D), lambda i: (i, 0))
    vec = pl.BlockSpec((1, D), lambda i: (0, 0))
    return pl.pallas_call(body, name=name, grid=(T // tr,), in_specs=[row, row, vec, vec], out_specs=[row, row],
                          out_shape=[jax.ShapeDtypeStruct((T, D), F32), jax.ShapeDtypeStruct((T, D), BF16)],
                          compiler_params=_params(("parallel",)))(x, br, g_post, g_next)


def _final_loss(name, x, br, g_post, target):
    T, D = x.shape
    tr = _row_tile(T)

    def body(x_ref, br_ref, gp_ref, t_ref, dy_ref, ls_ref):
        b = br_ref[...]
        err = x_ref[...] + b * _rms(b) * gp_ref[...] - t_ref[...]
        dy_ref[...] = err * (1.0 / D)

        @pl.when(pl.program_id(0) == 0)
        def _():
            ls_ref[...] = jnp.zeros_like(ls_ref)

        ls_ref[...] += jnp.sum(err * err)

    row = pl.BlockSpec((tr, D), lambda i: (i, 0))
    vec = pl.BlockSpec((1, D), lambda i: (0, 0))
    acc = pl.BlockSpec((SUBLANES, LANES), lambda i: (0, 0))
    return pl.pallas_call(body, name=name, grid=(T // tr,), in_specs=[row, row, vec, row], out_specs=[row, acc],
                          out_shape=[jax.ShapeDtypeStruct((T, D), F32), jax.ShapeDtypeStruct((SUBLANES, LANES), F32)],
                          compiler_params=_params(("arbitrary",)))(x, br, g_post, target)


def _norm_bwd(name, x, g, dy, add, out_dtype, jobs=()):
    T, D = x.shape
    tr = _row_tile(T)
    has_add = add is not None

    def body(*refs):
        if has_add:
            x_ref, g_ref, dy_ref, add_ref, dx_ref, dg_ref = refs
        else:
            x_ref, g_ref, dy_ref, dx_ref, dg_ref = refs
        xv = x_ref[...]
        r = _rms(xv)
        xhat = xv * r
        dyv = dy_ref[...].astype(F32)
        gdy = dyv * g_ref[...]
        dx = r * (gdy - xhat * jnp.mean(gdy * xhat, axis=-1, keepdims=True))
        if has_add:
            dx = dx + add_ref[...]
        dx_ref[...] = dx.astype(dx_ref.dtype)

        @pl.when(pl.program_id(0) == 0)
        def _():
            dg_ref[...] = jnp.zeros_like(dg_ref)

        dg_ref[...] += jnp.sum(dyv * xhat, axis=0, keepdims=True)

    row = pl.BlockSpec((tr, D), lambda i: (i, 0))
    vec = pl.BlockSpec((1, D), lambda i: (0, 0))
    ins = [x, g, dy] + ([add] if has_add else [])
    specs = [row, vec, row] + ([row] if has_add else [])
    if not jobs:
        return pl.pallas_call(body, name=name, grid=(T // tr,), in_specs=specs, out_specs=[row, vec],
                              out_shape=[jax.ShapeDtypeStruct((T, D), out_dtype), jax.ShapeDtypeStruct((1, D), F32)],
                              compiler_params=_params(("arbitrary",)))(*ins)
    return _carry_call(body, name=name, steps=T // tr, in_specs=specs, out_specs=[row, vec],
                       out_shape=[jax.ShapeDtypeStruct((T, D), out_dtype), jax.ShapeDtypeStruct((1, D), F32)],
                       scratch_shapes=[], args=ins, jobs=jobs)


HALO = SUBLANES
TIME_CHUNK = 512


def _chunks(T):
    tc = min(TIME_CHUNK, T)
    assert T % tc == 0 and tc % SUBLANES == 0
    return [(t0, tc) for t0 in range(0, T, tc)]


def _log_sigmoid(x):
    return -(jnp.maximum(-x, 0.0) + jnp.log(1.0 + jnp.exp(-jnp.abs(x))))


def _one_minus_exp(x):
    series = -x * (1.0 + x * (0.5 + x * (1.0 / 6.0 + x * (1.0 / 24.0))))
    return jnp.where(x > -0.01, series, 1.0 - jnp.exp(x))


_GELU_C = math.sqrt(2.0 / math.pi)


def _gelu(x):
    return 0.5 * x * (1.0 + jnp.tanh(_GELU_C * (x + 0.044715 * x * x * x)))


def _gelu_grad(x):
    th = jnp.tanh(_GELU_C * (x + 0.044715 * x * x * x))
    return 0.5 * (1.0 + th) + 0.5 * x * (1.0 - th * th) * _GELU_C * (1.0 + 3.0 * 0.044715 * x * x)


def _tile_scan(a, b, reverse):
    rows = a.shape[0]
    pos = lax.broadcasted_iota(jnp.int32, a.shape, 0) & (SUBLANES - 1)
    for d in (1, 2, 4):
        if reverse:
            ok = pos < SUBLANES - d
            shift = rows - d
        else:
            ok = pos >= d
            shift = d
        a_sh = jnp.where(ok, pltpu.roll(a, shift, 0), 1.0)
        b_sh = jnp.where(ok, pltpu.roll(b, shift, 0), 0.0)
        b = a * b_sh + b
        a = a * a_sh
    return a, b


def _carry_scan(a_s, b_s, T, reverse):
    n = T // SUBLANES
    edge = 0 if reverse else SUBLANES - 1

    def step(j, carry):
        g = (n - 1 - j) if reverse else j
        r = pl.multiple_of(g * SUBLANES, SUBLANES)
        h = b_s[pl.ds(r, SUBLANES), :] + a_s[pl.ds(r, SUBLANES), :] * carry
        b_s[pl.ds(r, SUBLANES), :] = h
        return jnp.broadcast_to(h[edge:edge + 1, :], h.shape)

    lax.fori_loop(0, n, step, jnp.zeros((SUBLANES, a_s.shape[1]), F32))


def _seg_spec(T, seg, nblk):
    return pl.BlockSpec((None, T, LANES), lambda c: (seg, 0, c))


def _rows_to_tile(rows):
    idx = lax.broadcasted_iota(jnp.int32, (SUBLANES, LANES), 0)
    out = jnp.zeros((SUBLANES, LANES), F32)
    for k, r in enumerate(rows):
        out = jnp.where(idx == k, r, out)
    return out


def _mixer_a_fwd(proj, conv_a):
    _, T, C = proj.shape
    nblk = C // LANES
    chunks = _chunks(T)

    def body(bg_ref, cg_ref, ax_ref, w_ref, y_ref, p_s):
        p_s[pl.ds(0, HALO), :] = jnp.zeros((HALO, LANES), F32)
        for t0, tc in chunks:
            p_s[pl.ds(HALO + t0, tc), :] = cg_ref[pl.ds(t0, tc), :] * ax_ref[pl.ds(t0, tc), :]
        w = w_ref[...]
        for t0, tc in chunks:
            c = (w[2:3, :] * p_s[pl.ds(HALO + t0, tc), :] + w[1:2, :] * p_s[pl.ds(HALO + t0 - 1, tc), :]
                 + w[0:1, :] * p_s[pl.ds(HALO + t0 - 2, tc), :])
            y_ref[pl.ds(t0, tc), :] = (bg_ref[pl.ds(t0, tc), :] * c).astype(y_ref.dtype)

    return pl.pallas_call(
        body, name="mixer_a_fwd", grid=(nblk,),
        in_specs=[_seg_spec(T, 0, nblk), _seg_spec(T, 1, nblk), _seg_spec(T, 2, nblk),
                  pl.BlockSpec((3, LANES), lambda c: (0, c))],
        out_specs=_seg_spec(T, 0, nblk),
        out_shape=jax.ShapeDtypeStruct((2, T, C), BF16),
        scratch_shapes=[pltpu.VMEM((T + HALO, LANES), F32)],
        compiler_params=_params(("parallel",)))(proj, proj, proj, conv_a)


def _mixer_a_bwd(proj, conv_a, dy):
    _, T, C = proj.shape
    nblk = C // LANES
    chunks = _chunks(T)

    def body(bg_ref, cg_ref, ax_ref, w_ref, dy_ref, dp_ref, dw_ref, p_s, dc_s):
        p_s[pl.ds(0, HALO), :] = jnp.zeros((HALO, LANES), F32)
        dc_s[pl.ds(T, HALO), :] = jnp.zeros((HALO, LANES), F32)
        for t0, tc in chunks:
            p_s[pl.ds(HALO + t0, tc), :] = cg_ref[pl.ds(t0, tc), :] * ax_ref[pl.ds(t0, tc), :]
        w = w_ref[...]
        for t0, tc in chunks:
            c = (w[2:3, :] * p_s[pl.ds(HALO + t0, tc), :] + w[1:2, :] * p_s[pl.ds(HALO + t0 - 1, tc), :]
                 + w[0:1, :] * p_s[pl.ds(HALO + t0 - 2, tc), :])
            dyv = dy_ref[pl.ds(t0, tc), :]
            dp_ref[0, pl.ds(t0, tc), :] = (dyv * c).astype(dp_ref.dtype)
            dc_s[pl.ds(t0, tc), :] = dyv * bg_ref[pl.ds(t0, tc), :]
        dw = [jnp.zeros((1, LANES), F32) for _ in range(3)]
        for t0, tc in chunks:
            dc = dc_s[pl.ds(t0, tc), :]
            dpv = w[2:3, :] * dc + w[1:2, :] * dc_s[pl.ds(t0 + 1, tc), :] + w[0:1, :] * dc_s[pl.ds(t0 + 2, tc), :]
            dp_ref[1, pl.ds(t0, tc), :] = (dpv * ax_ref[pl.ds(t0, tc), :]).astype(dp_ref.dtype)
            dp_ref[2, pl.ds(t0, tc), :] = (dpv * cg_ref[pl.ds(t0, tc), :]).astype(dp_ref.dtype)
            for k in range(3):
                dw[k] = dw[k] + jnp.sum(dc * p_s[pl.ds(HALO + t0 - (2 - k), tc), :], axis=0, keepdims=True)
        dw_ref[...] = _rows_to_tile(dw)

    return pl.pallas_call(
        body, name="mixer_a_bwd", grid=(nblk,),
        in_specs=[_seg_spec(T, 0, nblk), _seg_spec(T, 1, nblk), _seg_spec(T, 2, nblk),
                  pl.BlockSpec((3, LANES), lambda c: (0, c)), _seg_spec(T, 0, nblk)],
        out_specs=[pl.BlockSpec((3, T, LANES), lambda c: (0, 0, c)),
                   pl.BlockSpec((None, SUBLANES, LANES), lambda c: (c, 0, 0))],
        out_shape=[jax.ShapeDtypeStruct((6, T, C), BF16), jax.ShapeDtypeStruct((nblk, SUBLANES, LANES), F32)],
        scratch_shapes=[pltpu.VMEM((T + HALO, LANES), F32), pltpu.VMEM((T + HALO, LANES), F32)],
        compiler_params=_params(("parallel",)))(proj, proj, proj, conv_a, dy)


def _rg_gates(xr, wa, ba, wx, bx, ls):
    xb = xr.astype(BF16)
    r = jax.nn.sigmoid(jnp.dot(xb, wa, preferred_element_type=F32) + ba)
    i = jax.nn.sigmoid(jnp.dot(xb, wx, preferred_element_type=F32) + bx)
    log_a = LRU_C * r * ls
    a = jnp.exp(log_a)
    mult = jnp.sqrt(_one_minus_exp(2.0 * log_a))
    return r, i, a, mult


def _conv4(xh_s, cw, bias, t0, tc):
    return (cw[3:4, :] * xh_s[pl.ds(HALO + t0, tc), :] + cw[2:3, :] * xh_s[pl.ds(HALO + t0 - 1, tc), :]
            + cw[1:2, :] * xh_s[pl.ds(HALO + t0 - 2, tc), :] + cw[0:1, :] * xh_s[pl.ds(HALO + t0 - 3, tc), :] + bias)


def _mixer_b_specs(T, nblk):
    vec = pl.BlockSpec((1, LANES), lambda c: (0, c))
    mat = pl.BlockSpec((None, LANES, LANES), lambda c: (c, 0, 0))
    return [_seg_spec(T, 3, nblk), _seg_spec(T, 4, nblk), pl.BlockSpec((4, LANES), lambda c: (0, c)),
            vec, mat, vec, mat, vec, vec]


def _mixer_b_fwd(name, proj, conv_b, bias, wa, ba, wx, bx, lam, y, jobs=()):
    _, T, C = proj.shape
    nblk = C // LANES
    chunks = _chunks(T)

    def body(gate_ref, x_ref, cw_ref, cb_ref, wa_ref, ba_ref, wx_ref, bx_ref, lam_ref, y_in, y_ref, xh_s, a_s, b_s):
        xh_s[pl.ds(0, HALO), :] = jnp.zeros((HALO, LANES), F32)
        for t0, tc in chunks:
            xh_s[pl.ds(HALO + t0, tc), :] = x_ref[pl.ds(t0, tc), :]
        cw, bias_v = cw_ref[...], cb_ref[...]
        ls = _log_sigmoid(lam_ref[...])
        for t0, tc in chunks:
            xr = _conv4(xh_s, cw, bias_v, t0, tc)
            r, i, a, mult = _rg_gates(xr, wa_ref[...], ba_ref[...], wx_ref[...], bx_ref[...], ls)
            ac, hc = _tile_scan(a, mult * i * xr, reverse=False)
            a_s[pl.ds(t0, tc), :] = ac
            b_s[pl.ds(t0, tc), :] = hc
        _carry_scan(a_s, b_s, T, reverse=False)
        for t0, tc in chunks:
            y_ref[pl.ds(t0, tc), :] = (b_s[pl.ds(t0, tc), :] * _gelu(gate_ref[pl.ds(t0, tc), :])).astype(y_ref.dtype)

    return _carry_call(
        body, name=name, steps=nblk, in_specs=_mixer_b_specs(T, nblk) + [_ANY],
        out_specs=[_seg_spec(T, 1, nblk)], out_shape=[jax.ShapeDtypeStruct(y.shape, y.dtype)],
        scratch_shapes=[pltpu.VMEM((T + HALO, LANES), F32), pltpu.VMEM((T, LANES), F32), pltpu.VMEM((T, LANES), F32)],
        args=(proj, proj, conv_b, bias, wa, ba, wx, bx, lam, y), jobs=jobs, aliases={9: 0})


_ROW_CONV, _ROW_BIAS, _ROW_BA, _ROW_BX, _ROW_LAM = 0, 4, 5, 6, 7


def _mixer_b_bwd(name, proj, conv_b, bias, wa, ba, wx, bx, lam, dy, dproj, jobs=()):
    _, T, C = proj.shape
    nblk = C // LANES
    chunks = _chunks(T)

    def body(gate_ref, x_ref, cw_ref, cb_ref, wa_ref, ba_ref, wx_ref, bx_ref, lam_ref, dy_ref, dp_in,
             dp_ref, sm_ref, dwa_ref, dwx_ref, xh_s, xr_s, r_s, i_s, a_s, h_s, sa_s, sb_s, dx_s):
        zero_halo = jnp.zeros((HALO, LANES), F32)
        xh_s[pl.ds(0, HALO), :] = zero_halo
        h_s[pl.ds(0, HALO), :] = zero_halo
        a_s[pl.ds(T, HALO), :] = zero_halo
        dx_s[pl.ds(T, HALO), :] = zero_halo
        for t0, tc in chunks:
            xh_s[pl.ds(HALO + t0, tc), :] = x_ref[pl.ds(t0, tc), :]
        cw, bias_v = cw_ref[...], cb_ref[...]
        lam_v = lam_ref[...]
        ls = _log_sigmoid(lam_v)
        wa_v, wx_v, ba_v, bx_v = wa_ref[...], wx_ref[...], ba_ref[...], bx_ref[...]
        for t0, tc in chunks:
            xr = _conv4(xh_s, cw, bias_v, t0, tc)
            r, i, a, mult = _rg_gates(xr, wa_v, ba_v, wx_v, bx_v, ls)
            xr_s[pl.ds(t0, tc), :] = xr
            r_s[pl.ds(t0, tc), :] = r
            i_s[pl.ds(t0, tc), :] = i
            a_s[pl.ds(t0, tc), :] = a
            ac, hc = _tile_scan(a, mult * i * xr, reverse=False)
            sa_s[pl.ds(t0, tc), :] = ac
            sb_s[pl.ds(t0, tc), :] = hc
        _carry_scan(sa_s, sb_s, T, reverse=False)
        for t0, tc in chunks:
            h_s[pl.ds(HALO + t0, tc), :] = sb_s[pl.ds(t0, tc), :]
        for t0, tc in chunks:
            gv = gate_ref[pl.ds(t0, tc), :]
            dyv = dy_ref[pl.ds(t0, tc), :]
            dp_ref[0, pl.ds(t0, tc), :] = (dyv * h_s[pl.ds(HALO + t0, tc), :] * _gelu_grad(gv)).astype(dp_ref.dtype)
            ac, gc = _tile_scan(a_s[pl.ds(t0 + 1, tc), :], dyv * _gelu(gv), reverse=True)
            sa_s[pl.ds(t0, tc), :] = ac
            sb_s[pl.ds(t0, tc), :] = gc
        _carry_scan(sa_s, sb_s, T, reverse=True)
        acc = {k: jnp.zeros((1, LANES), F32) for k in ("bias", "ba", "bx", "lam")}
        dwa = jnp.zeros((LANES, LANES), F32)
        dwx = jnp.zeros((LANES, LANES), F32)
        for t0, tc in chunks:
            dht = sb_s[pl.ds(t0, tc), :]
            xr, r, i, a = xr_s[pl.ds(t0, tc), :], r_s[pl.ds(t0, tc), :], i_s[pl.ds(t0, tc), :], a_s[pl.ds(t0, tc), :]
            mult = jnp.sqrt(_one_minus_exp(2.0 * LRU_C * r * ls))
            da = dht * h_s[pl.ds(HALO + t0 - 1, tc), :]
            dmult = dht * i * xr
            di = dht * mult * xr
            dlog_a = da * a - dmult * a * a / mult
            dpa = dlog_a * (LRU_C * ls) * r * (1.0 - r)
            dpx = di * i * (1.0 - i)
            acc["lam"] = acc["lam"] + jnp.sum(dlog_a * r, axis=0, keepdims=True)
            acc["ba"] = acc["ba"] + jnp.sum(dpa, axis=0, keepdims=True)
            acc["bx"] = acc["bx"] + jnp.sum(dpx, axis=0, keepdims=True)
            xb, dpab, dpxb = xr.astype(BF16), dpa.astype(BF16), dpx.astype(BF16)
            dwa = dwa + lax.dot_general(xb, dpab, DIMS_TN, preferred_element_type=F32)
            dwx = dwx + lax.dot_general(xb, dpxb, DIMS_TN, preferred_element_type=F32)
            dxr = (dht * mult * i + lax.dot_general(dpab, wa_v, DIMS_NT, preferred_element_type=F32)
                   + lax.dot_general(dpxb, wx_v, DIMS_NT, preferred_element_type=F32))
            acc["bias"] = acc["bias"] + jnp.sum(dxr, axis=0, keepdims=True)
            dx_s[pl.ds(t0, tc), :] = dxr
        dcw = [jnp.zeros((1, LANES), F32) for _ in range(4)]
        for t0, tc in chunks:
            dxr = dx_s[pl.ds(t0, tc), :]
            dxin = (cw[3:4, :] * dxr + cw[2:3, :] * dx_s[pl.ds(t0 + 1, tc), :] + cw[1:2, :] * dx_s[pl.ds(t0 + 2, tc), :]
                    + cw[0:1, :] * dx_s[pl.ds(t0 + 3, tc), :])
            dp_ref[1, pl.ds(t0, tc), :] = dxin.astype(dp_ref.dtype)
            for k in range(4):
                dcw[k] = dcw[k] + jnp.sum(dxr * xh_s[pl.ds(HALO + t0 - (3 - k), tc), :], axis=0, keepdims=True)
        dlam = acc["lam"] * LRU_C * jax.nn.sigmoid(-lam_v)
        sm_ref[...] = _rows_to_tile(dcw + [acc["bias"], acc["ba"], acc["bx"], dlam])
        dwa_ref[...] = dwa
        dwx_ref[...] = dwx

    big = lambda halo: pltpu.VMEM((T + halo, LANES), F32)
    mat = pl.BlockSpec((None, LANES, LANES), lambda c: (c, 0, 0))
    return _carry_call(
        body, name=name, steps=nblk,
        in_specs=_mixer_b_specs(T, nblk) + [_seg_spec(T, 1, nblk), _ANY],
        out_specs=[pl.BlockSpec((3, T, LANES), lambda c: (1, 0, c)),
                   pl.BlockSpec((None, SUBLANES, LANES), lambda c: (c, 0, 0)), mat, mat],
        out_shape=[jax.ShapeDtypeStruct(dproj.shape, dproj.dtype), jax.ShapeDtypeStruct((nblk, SUBLANES, LANES), F32),
                   jax.ShapeDtypeStruct((nblk, LANES, LANES), F32), jax.ShapeDtypeStruct((nblk, LANES, LANES), F32)],
        scratch_shapes=[big(HALO), big(0), big(0), big(0), big(HALO), big(HALO), big(0), big(0), big(HALO)],
        args=(proj, proj, conv_b, bias, wa, ba, wx, bx, lam, dy, dproj), jobs=jobs, aliases={10: 0})


ATT_BLOCK = 128
ATT_GROUP = 3
ATT_TILE = ATT_BLOCK * ATT_GROUP
ATT_UNDERFLOW = -110.0
ATT_UNVISITED = -1e30


def _split_dot(x, m):
    hi = x.astype(BF16)
    lo = (x - hi.astype(F32)).astype(BF16)
    return jnp.dot(hi, m, preferred_element_type=F32) + jnp.dot(lo, m, preferred_element_type=F32)


def _sub(x, j):
    return x[:, j * ATT_BLOCK:(j + 1) * ATT_BLOCK]


def _stack_rows(x):
    return jnp.concatenate([_sub(x, j) for j in range(ATT_GROUP)], axis=0)


def _unstack_rows(x, offsets):
    return jnp.concatenate([x[j * ATT_BLOCK:(j + 1) * ATT_BLOCK, :] + offsets[j] for j in range(ATT_GROUP)], axis=1)


def _att_tile(q, k_ref, q0, qb, it, scale):
    hi = (qb + 1 - ATT_GROUP * it) * ATT_BLOCK
    k0 = pl.multiple_of(jnp.maximum(hi - ATT_TILE, 0), ATT_BLOCK)
    kt = k_ref[pl.ds(k0, ATT_TILE), :]
    z = lax.dot_general(q, kt, DIMS_NT, preferred_element_type=F32) * scale
    key = k0 + lax.broadcasted_iota(jnp.int32, z.shape, 1)
    row = q0 + lax.broadcasted_iota(jnp.int32, z.shape, 0)
    mask = (key < row) & (key < hi)
    n = jnp.where(mask, -(jnp.maximum(z, 0.0) + jnp.log(1.0 + jnp.exp(-jnp.abs(z)))), 0.0)
    return k0, kt, z, mask, n


def _suffix_in_tile(n, upper, run):
    rs = [jnp.sum(_sub(n, j), axis=1, keepdims=True) for j in range(ATT_GROUP)]
    offs = [None] * ATT_GROUP
    offs[ATT_GROUP - 1] = run
    for j in range(ATT_GROUP - 2, -1, -1):
        offs[j] = offs[j + 1] + rs[j + 1]
    return _unstack_rows(_split_dot(_stack_rows(n), upper), offs), offs[0] + rs[0]


def _head_spec(T, seg, heads):
    return pl.BlockSpec((None, T, ATT_HEAD_DIM), lambda h: (seg, 0, h))


def _attention_fwd(name, qkv, jobs=()):
    _, T, D = qkv.shape
    heads = D // ATT_HEAD_DIM
    nq = T // ATT_BLOCK
    assert nq <= LANES and T >= ATT_TILE
    scale = 1.0 / math.sqrt(ATT_HEAD_DIM)

    def body(q_ref, k_ref, v_ref, o_ref, r_ref, acc_s, run_s):
        rr = lax.broadcasted_iota(jnp.int32, (ATT_BLOCK, ATT_BLOCK), 0)
        cc = lax.broadcasted_iota(jnp.int32, (ATT_BLOCK, ATT_BLOCK), 1)
        upper = jnp.where(rr > cc, 1.0, 0.0).astype(BF16)
        lane = lax.broadcasted_iota(jnp.int32, (ATT_BLOCK, LANES), 1)

        def q_loop(qb, _):
            q0 = pl.multiple_of(qb * ATT_BLOCK, ATT_BLOCK)
            q = q_ref[pl.ds(q0, ATT_BLOCK), :]
            acc_s[...] = jnp.zeros_like(acc_s)
            run_s[...] = jnp.zeros_like(run_s)
            r_ref[pl.ds(q0, ATT_BLOCK), :] = jnp.full((ATT_BLOCK, LANES), ATT_UNVISITED, F32)
            n_tiles = (qb + ATT_GROUP) // ATT_GROUP

            def tile(carry):
                it, _ = carry
                k0, _, z, mask, n = _att_tile(q, k_ref, q0, qb, it, scale)
                run = run_s[...]
                suffix, run_next = _suffix_in_tile(n, upper, run)
                w = jnp.where(mask, jnp.exp(z + n + suffix), 0.0)
                acc_s[...] += jnp.dot(w.astype(BF16), v_ref[pl.ds(k0, ATT_TILE), :], preferred_element_type=F32)
                r_ref[pl.ds(q0, ATT_BLOCK), :] = jnp.where(lane == it, run, r_ref[pl.ds(q0, ATT_BLOCK), :])
                run_s[...] = run_next
                return it + 1, jnp.max(run_next) >= ATT_UNDERFLOW

            lax.while_loop(lambda c: (c[0] < n_tiles) & c[1], tile, (jnp.int32(0), jnp.bool_(True)))
            o_ref[pl.ds(q0, ATT_BLOCK), :] = acc_s[...].astype(o_ref.dtype)
            return 0

        lax.fori_loop(0, nq, q_loop, 0)

    return _carry_call(
        body, name=name, steps=heads,
        in_specs=[_head_spec(T, 0, heads), _head_spec(T, 1, heads), _head_spec(T, 2, heads)],
        out_specs=[pl.BlockSpec((T, ATT_HEAD_DIM), lambda h: (0, h)), pl.BlockSpec((None, T, LANES), lambda h: (h, 0, 0))],
        out_shape=[jax.ShapeDtypeStruct((T, D), BF16), jax.ShapeDtypeStruct((heads, T, LANES), F32)],
        scratch_shapes=[pltpu.VMEM((ATT_BLOCK, ATT_HEAD_DIM), F32), pltpu.VMEM((ATT_BLOCK, LANES), F32)],
        args=(qkv, qkv, qkv), jobs=jobs)


def _attention_bwd(name, qkv, do, rmat, jobs=()):
    _, T, D = qkv.shape
    heads = D // ATT_HEAD_DIM
    nq = T // ATT_BLOCK
    scale = 1.0 / math.sqrt(ATT_HEAD_DIM)

    def body(q_ref, k_ref, v_ref, do_ref, r_ref, dqkv_ref, dk_s, dv_s, dq_s, left_s):
        rr = lax.broadcasted_iota(jnp.int32, (ATT_BLOCK, ATT_BLOCK), 0)
        cc = lax.broadcasted_iota(jnp.int32, (ATT_BLOCK, ATT_BLOCK), 1)
        upper = jnp.where(rr > cc, 1.0, 0.0).astype(BF16)
        lower = jnp.where(rr < cc, 1.0, 0.0).astype(BF16)
        lane = lax.broadcasted_iota(jnp.int32, (ATT_BLOCK, LANES), 1)
        dk_s[...] = jnp.zeros_like(dk_s)
        dv_s[...] = jnp.zeros_like(dv_s)

        def q_loop(qb, _):
            q0 = pl.multiple_of(qb * ATT_BLOCK, ATT_BLOCK)
            q = q_ref[pl.ds(q0, ATT_BLOCK), :]
            dov = do_ref[pl.ds(q0, ATT_BLOCK), :]
            dq_s[...] = jnp.zeros_like(dq_s)
            left_s[...] = jnp.zeros_like(left_s)
            rm = r_ref[pl.ds(q0, ATT_BLOCK), :]
            n_tiles = (qb + ATT_GROUP) // ATT_GROUP
            seen = (jnp.max(rm, axis=0, keepdims=True) > 0.5 * ATT_UNVISITED) & (lane[0:1, :] < n_tiles)
            n_visited = jnp.sum(jnp.where(seen, 1.0, 0.0)).astype(jnp.int32)

            def tile(j, _):
                it = n_visited - 1 - j
                k0, kt, z, mask, n = _att_tile(q, k_ref, q0, qb, it, scale)
                vt = v_ref[pl.ds(k0, ATT_TILE), :]
                run = jnp.sum(jnp.where(lane == it, r_ref[pl.ds(q0, ATT_BLOCK), :], 0.0), axis=1, keepdims=True)
                suffix, _ = _suffix_in_tile(n, upper, run)
                s = z + n
                w = jnp.where(mask, jnp.exp(s + suffix), 0.0)
                e = w * lax.dot_general(dov, vt, DIMS_NT, preferred_element_type=F32)
                es = [jnp.sum(_sub(e, g), axis=1, keepdims=True) for g in range(ATT_GROUP)]
                pre = [left_s[...]]
                for g in range(ATT_GROUP):
                    pre.append(pre[g] + es[g])
                before = _unstack_rows(_split_dot(_stack_rows(e), lower), pre)
                sig = jnp.exp(s)
                dz = (jnp.where(mask, e * (1.0 - sig) - before * sig, 0.0) * scale).astype(BF16)
                dq_s[...] += jnp.dot(dz, kt, preferred_element_type=F32)
                dk_s[pl.ds(k0, ATT_TILE), :] += lax.dot_general(dz, q, DIMS_TN, preferred_element_type=F32)
                dv_s[pl.ds(k0, ATT_TILE), :] += lax.dot_general(w.astype(BF16), dov, DIMS_TN, preferred_element_type=F32)
                left_s[...] = pre[ATT_GROUP]
                return 0

            lax.fori_loop(0, n_visited, tile, 0)
            dqkv_ref[0, pl.ds(q0, ATT_BLOCK), :] = dq_s[...].astype(dqkv_ref.dtype)
            return 0

        lax.fori_loop(0, nq, q_loop, 0)
        dqkv_ref[1, :, :] = dk_s[...].astype(dqkv_ref.dtype)
        dqkv_ref[2, :, :] = dv_s[...].astype(dqkv_ref.dtype)

    return _carry_call(
        body, name=name, steps=heads,
        in_specs=[_head_spec(T, 0, heads), _head_spec(T, 1, heads), _head_spec(T, 2, heads),
                  pl.BlockSpec((T, ATT_HEAD_DIM), lambda h: (0, h)), pl.BlockSpec((None, T, LANES), lambda h: (h, 0, 0))],
        out_specs=[pl.BlockSpec((3, T, ATT_HEAD_DIM), lambda h: (0, 0, h))],
        out_shape=[jax.ShapeDtypeStruct((3, T, D), BF16)],
        scratch_shapes=[pltpu.VMEM((T, ATT_HEAD_DIM), F32), pltpu.VMEM((T, ATT_HEAD_DIM), F32),
                        pltpu.VMEM((ATT_BLOCK, ATT_HEAD_DIM), F32), pltpu.VMEM((ATT_BLOCK, LANES), F32)],
        args=(qkv, qkv, qkv, do, rmat), jobs=jobs)


def _block_diag_pairs(w):
    h = w.shape[0]
    wp = w.reshape(h // 2, 2, RG_HEAD_DIM, RG_HEAD_DIM)
    z = jnp.zeros_like(wp[:, 0])
    top = jnp.concatenate([wp[:, 0], z], axis=2)
    bot = jnp.concatenate([z, wp[:, 1]], axis=2)
    return jnp.concatenate([top, bot], axis=1)


def _diag_pairs(g):
    n = g.shape[0]
    a = g[:, :RG_HEAD_DIM, :RG_HEAD_DIM]
    b = g[:, RG_HEAD_DIM:, RG_HEAD_DIM:]
    return jnp.stack([a, b], axis=1).reshape(2 * n, RG_HEAD_DIM, RG_HEAD_DIM)


class _Weights:
    def __init__(self, full, shards=None, plan=None):
        self.full, self.shards, self.plan = dict(full), shards or {}, plan or {}
        self.partial, self.rows = {}, {}

    def __getitem__(self, name):
        return self.full[name]

    def jobs(self, call):
        return [_gather_job(self.shards[n], self.partial.get(n), lo, hi, parts)
                for n, lo, hi, parts in self.plan.get(call, ())]

    def deliver(self, call, outs):
        for (n, lo, hi, parts), g in zip(self.plan.get(call, ()), outs):
            self.partial[n] = g
            self.rows[n] = self.rows.get(n, 0) + hi - lo
            if self.rows[n] == parts:
                self.full[n] = _gathered_layout(n, g)


def _gathered_layout(name, g):
    if name in ("w_in", "w_qkv"):
        return _from_col_blocks(g)
    if name in ("w_out", "w_o"):
        return g.reshape(g.shape[0] * g.shape[1], g.shape[2])
    return g


class _Grads:
    def __init__(self, lands=None, plan=None):
        self.lands, self.plan = dict(lands) if lands else None, plan or {}
        self.ready, self.sent = {}, {}

    def put(self, name, arr):
        self.ready[name] = arr

    def jobs(self, call):
        if self.lands is None:
            return []
        return [_exchange_job(self.ready[n], self.lands[n], lo, hi, parts) for n, lo, hi, parts in self.plan.get(call, ())]

    def deliver(self, call, outs):
        for (n, lo, hi, parts), o in zip(self.plan.get(call, ()), outs):
            assert self.sent.get(n, (0, parts)) == (lo, parts), (call, n)
            self.lands[n] = o
            self.sent[n] = (hi, parts)

    def flush(self, name):
        if self.lands is None:
            return
        rest = []
        for n in self.ready:
            lo, parts = self.sent.get(n, (0, 1))
            if lo < parts:
                rest.append((n, lo, parts, parts))
        if rest:
            outs = _run_jobs(name, [_exchange_job(self.ready[n], self.lands[n], lo, hi, parts) for n, lo, hi, parts in rest])
            for (n, _, hi, parts), o in zip(rest, outs):
                self.lands[n] = o
                self.sent[n] = (hi, parts)


def _mlp_fwd(tag, h, wts, run):
    T, D = h.shape
    w_up = wts["up" + tag]
    fb = w_up.shape[2]
    F = fb * N_DEV
    tm, tn, tk = _tile(T, 1024), _tile(fb, 1024), _tile(D, MM_TK)
    nb = fb // tn

    def up_epilogue(u):
        r = jnp.maximum(u, 0.0)
        return u, r * r

    o_spec = pl.BlockSpec((tm, tn), lambda i, j, k: (i, j))
    u, act = run(
        _matmul, f"mlp_up_l{tag}",
        [(h, pl.BlockSpec((tm, tk), lambda i, j, k: (i, k))),
         (w_up, pl.BlockSpec((None, tk, tn), lambda i, j, k: (j // nb, k, j % nb)))],
        [(jax.ShapeDtypeStruct((T, F), BF16), o_spec), (jax.ShapeDtypeStruct((T, F), BF16), o_spec)],
        (T // tm, F // tn, D // tk), DIMS_NN, (tm, tn), up_epilogue, n_main=2)
    w_down = wts["down" + tag].reshape(F, D)
    m = run(_mm_nn, f"mlp_down_l{tag}", act, w_down, F32)
    return u, act, m


def _mlp_bwd(tag, h, u, act, dm, wts, grads, run):
    T, D = h.shape
    w_up, w_down = wts["up" + tag], wts["down" + tag]
    fb = w_up.shape[2]
    F = fb * N_DEV
    grads.put("down" + tag, run(_mm_tn, f"mlp_down_dw_l{tag}", act, dm, BF16).reshape(N_DEV, fb, D))
    tm, tn, tk = _tile(T, 1024), _tile(fb, 1024), _tile(D, MM_TK)
    nb = fb // tn
    o_spec = pl.BlockSpec((tm, tn), lambda i, j, k: (i, j))
    du = run(
        _matmul, f"mlp_down_dx_l{tag}",
        [(dm, pl.BlockSpec((tm, tk), lambda i, j, k: (i, k))),
         (w_down, pl.BlockSpec((None, tn, tk), lambda i, j, k: (j // nb, j % nb, k))),
         (u, o_spec)],
        [(jax.ShapeDtypeStruct((T, F), BF16), o_spec)],
        (T // tm, F // tn, D // tk), DIMS_NT, (tm, tn),
        lambda r, uv: (r * (2.0 * jnp.maximum(uv.astype(F32), 0.0)),))
    grads.put("up" + tag, run(_mm_tn, f"mlp_up_dw_l{tag}", h, du, BF16, out_blocks=N_DEV))
    tn2, tk2 = _tile(D, 1024), _tile(fb, MM_TK)
    nkb = fb // tk2
    return run(
        _matmul, f"mlp_up_dx_l{tag}",
        [(du, pl.BlockSpec((tm, tk2), lambda i, j, k: (i, k))),
         (w_up, pl.BlockSpec((None, tn2, tk2), lambda i, j, k: (k // nkb, j, k % nkb)))],
        [(jax.ShapeDtypeStruct((T, D), F32), pl.BlockSpec((tm, tn2), lambda i, j, k: (i, j)))],
        (T // tm, D // tn2, F // tk2), DIMS_NT, (tm, tn2), None)


def _local_step(x, target, gains, conv_a, conv_b, conv_b_bias, rg_w_a, rg_b_a, rg_w_x, rg_b_x, rg_lambda, wts, grads):
    T, D = x.shape
    g = lambda l, i: gains[l, i][None, :]
    wa_p = _block_diag_pairs(rg_w_a).astype(BF16)
    wx_p = _block_diag_pairs(rg_w_x).astype(BF16)

    def run(fn, name, *args, n_main=1, **kw):
        jw, jg = wts.jobs(name), grads.jobs(name)
        res = fn(name, *args, jobs=jw + jg, **kw)
        main, jo = res[:n_main], res[n_main:]
        wts.deliver(name, jo[:len(jw)])
        grads.deliver(name, jo[len(jw):])
        return main[0] if n_main == 1 else main

    h0 = _norm_fwd("norm_in", x, g(0, 0))
    proj = run(_mm_nn, "w_in_fwd", h0, wts["w_in"], F32, out_seg=5)
    y = run(_mixer_b_fwd, "mixer_b_fwd", proj, conv_b, conv_b_bias, wa_p, rg_b_a, wx_p, rg_b_x, rg_lambda,
            _mixer_a_fwd(proj, conv_a))
    mix0 = run(_mm_nn, "w_out_fwd", y, wts["w_out"], F32, a_seg=2)
    x1, h1 = _resid_norm("resid_mix0", x, mix0, g(0, 1), g(0, 2))
    u0, act0, m0 = _mlp_fwd("0", h1, wts, run)
    x2, h2 = _resid_norm("resid_mlp0", x1, m0, g(0, 3), g(1, 0))
    qkv = run(_mm_nn, "w_qkv_fwd", h2, wts["w_qkv"], BF16, out_seg=3)
    o, rmat = run(_attention_fwd, "attention_fwd", qkv, n_main=2)
    mix1 = run(_mm_nn, "w_o_fwd", o, wts["w_o"], F32)
    x3, h3 = _resid_norm("resid_mix1", x2, mix1, g(1, 1), g(1, 2))
    u1, act1, m1 = _mlp_fwd("1", h3, wts, run)
    dx4, sq = _final_loss("loss", x3, m1, g(1, 3), target)

    dm1, dg13 = _norm_bwd("norm_bwd_m1", m1, g(1, 3), dx4, None, BF16)
    dh3 = _mlp_bwd("1", h3, u1, act1, dm1, wts, grads, run)
    dx3, dg12 = _norm_bwd("norm_bwd_x3", x3, g(1, 2), dh3, dx4, F32)
    dmix1, dg11 = _norm_bwd("norm_bwd_mix1", mix1, g(1, 1), dx3, None, BF16)
    grads.put("w_o", run(_mm_tn, "w_o_dw", o, dmix1, BF16).reshape(N_DEV, D // N_DEV, D))
    do = run(_mm_nt, "w_o_dx", dmix1, wts["w_o"], BF16)
    dqkv = run(_attention_bwd, "attention_bwd", qkv, do, rmat)
    grads.put("w_qkv", _col_blocks(run(_mm_tn, "w_qkv_dw", h2, dqkv, BF16, b_seg=3), N_DEV))
    dh2 = run(_mm_nt, "w_qkv_dx", dqkv, wts["w_qkv"], F32, a_seg=3)
    dx2, dg10 = _norm_bwd("norm_bwd_x2", x2, g(1, 0), dh2, dx3, F32)
    dm0, dg03 = _norm_bwd("norm_bwd_m0", m0, g(0, 3), dx2, None, BF16)
    dh1 = _mlp_bwd("0", h1, u0, act0, dm0, wts, grads, run)
    dx1, dg02 = _norm_bwd("norm_bwd_x1", x1, g(0, 2), dh1, dx2, F32)
    dmix0, dg01 = _norm_bwd("norm_bwd_mix0", mix0, g(0, 1), dx1, None, BF16)
    grads.put("w_out", run(_mm_tn, "w_out_dw", y, dmix0, BF16, a_seg=2).reshape(N_DEV, D // N_DEV, D))
    dy = run(_mm_nt, "w_out_dx", dmix0, wts["w_out"], F32, out_seg=2)
    dproj_a, dconv_a = _mixer_a_bwd(proj, conv_a, dy)
    dproj, sm_b, dwa_p, dwx_p = run(_mixer_b_bwd, "mixer_b_bwd", proj, conv_b, conv_b_bias, wa_p, rg_b_a, wx_p, rg_b_x,
                                    rg_lambda, dy, dproj_a, n_main=4)
    grads.put("w_in", _col_blocks(run(_mm_tn, "w_in_dw", h0, dproj, BF16, b_seg=5), N_DEV))
    dh0 = run(_mm_nt, "w_in_dx", dproj, wts["w_in"], F32, a_seg=5)
    dx0, dg00 = run(_norm_bwd, "norm_bwd_x0", x, g(0, 0), dh0, dx1, F32, n_main=2)

    C = D // 2
    lanes_to_vec = lambda t, row: t[:, row, :].reshape(1, C)
    small = {
        "norm_gains": jnp.concatenate([dg00, dg01, dg02, dg03, dg10, dg11, dg12, dg13], axis=0).reshape(2, 4, D),
        "conv_a": jnp.transpose(dconv_a[:, :3, :], (1, 0, 2)).reshape(3, C),
        "conv_b": jnp.transpose(sm_b[:, :4, :], (1, 0, 2)).reshape(4, C),
        "conv_b_bias": lanes_to_vec(sm_b, _ROW_BIAS),
        "rg_w_a": _diag_pairs(dwa_p),
        "rg_b_a": lanes_to_vec(sm_b, _ROW_BA),
        "rg_w_x": _diag_pairs(dwx_p),
        "rg_b_x": lanes_to_vec(sm_b, _ROW_BX),
        "rg_lambda": lanes_to_vec(sm_b, _ROW_LAM),
    }
    return sq[0, 0], dx0, small


def _my_index():
    return 4 * lax.axis_index("x") + 2 * lax.axis_index("y") + lax.axis_index("c")


def _peers():
    x, y, c = lax.axis_index("x"), lax.axis_index("y"), lax.axis_index("c")
    out = []
    for k in range(1, N_DEV):
        px = x ^ ((k >> 2) & 1)
        py = y ^ ((k >> 1) & 1)
        pc = c ^ (k & 1)
        out.append(((px, py, pc), 4 * px + 2 * py + pc))
    return out


GATHER_FIRST = ("w_in",)
GATHER_PLAN = {
    "w_in_fwd": (("w_out", 0, 1, 1), ("up0", 0, 1, 4)),
    "mixer_b_fwd": (("up0", 1, 3, 4),),
    "w_out_fwd": (("up0", 3, 4, 4),),
    "mlp_up_l0": (("down0", 0, 1, 1),),
    "mlp_down_l0": (("w_qkv", 0, 1, 1), ("w_o", 0, 1, 1)),
    "w_qkv_fwd": (("up1", 0, 2, 4),),
    "attention_fwd": (("up1", 2, 4, 4), ("down1", 0, 1, 1)),
}
EXCHANGE_PLAN = {
    "mlp_down_dx_l1": (("down1", 0, 3, 8),), "mlp_up_dw_l1": (("down1", 3, 6, 8),),
    "mlp_up_dx_l1": (("down1", 6, 8, 8), ("up1", 0, 1, 8)),
    "w_o_dw": (("up1", 1, 2, 8),), "w_o_dx": (("up1", 2, 3, 8),),
    "attention_bwd": (("up1", 3, 8, 8), ("w_o", 0, 1, 1)),
    "w_qkv_dx": (("w_qkv", 0, 1, 2),), "mlp_down_dw_l0": (("w_qkv", 1, 2, 2),),
    "mlp_down_dx_l0": (("down0", 0, 3, 8),), "mlp_up_dw_l0": (("down0", 3, 6, 8),),
    "mlp_up_dx_l0": (("down0", 6, 8, 8), ("up0", 0, 1, 8)),
    "w_out_dw": (("up0", 1, 2, 8),), "w_out_dx": (("up0", 2, 3, 8),),
    "mixer_b_bwd": (("up0", 3, 7, 8),),
    "w_in_dw": (("up0", 7, 8, 8), ("w_out", 0, 1, 2)),
    "w_in_dx": (("w_out", 1, 2, 2), ("w_in", 0, 1, 4)),
    "norm_bwd_x0": (("w_in", 1, 2, 4),),
    "adamw_mlp_w_down": (("w_in", 2, 3, 4),), "adamw_mlp_w_up": (("w_in", 3, 4, 4),),
}


def _all_gather(name, shards):
    n = len(shards)

    def body(*refs):
        srcs, dsts = refs[:n], refs[n:2 * n]
        send_sems, recv_sems, local_sems = refs[2 * n:]
        me = _my_index()
        peers = _peers()
        copies = []
        for a in range(n):
            lc = pltpu.make_async_copy(srcs[a], dsts[a].at[me], local_sems.at[a])
            lc.start()
            copies.append(lc)
        remote = []
        for a in range(n):
            for k, (pos, _) in enumerate(peers):
                cp = pltpu.make_async_remote_copy(
                    src_ref=srcs[a], dst_ref=dsts[a].at[me], send_sem=send_sems.at[a, k], recv_sem=recv_sems.at[a, k],
                    device_id=pos, device_id_type=MESH)
                cp.start()
                remote.append(cp)
        for a in range(n):
            for k, (pos, idx) in enumerate(peers):
                pltpu.make_async_remote_copy(
                    src_ref=srcs[a], dst_ref=dsts[a].at[idx], send_sem=send_sems.at[a, k], recv_sem=recv_sems.at[a, k],
                    device_id=pos, device_id_type=MESH).wait_recv()
        for cp in remote:
            cp.wait_send()
        for lc in copies:
            lc.wait()

    return pl.pallas_call(
        body, name=name,
        in_specs=[_ANY] * n, out_specs=[_ANY] * n,
        out_shape=[jax.ShapeDtypeStruct((N_DEV,) + s.shape, s.dtype) for s in shards],
        scratch_shapes=[pltpu.SemaphoreType.DMA((n, N_DEV - 1)), pltpu.SemaphoreType.DMA((n, N_DEV - 1)),
                        pltpu.SemaphoreType.DMA((n,))],
    )(*shards)


def _job_sems():
    return [pltpu.SemaphoreType.DMA((N_DEV - 1,)), pltpu.SemaphoreType.DMA((N_DEV - 1,)), pltpu.SemaphoreType.DMA((1,))]


def _gather_job(shard, prev=None, lo=0, hi=1, parts=1):
    n = shard.shape[0] // parts
    assert n * parts == shard.shape[0]
    rows = pl.ds(lo * n, (hi - lo) * n)

    def ctx():
        x, y, c = lax.axis_index("x"), lax.axis_index("y"), lax.axis_index("c")
        chips = [(1 - x, y), (x, 1 - y), (1 - x, 1 - y)]
        return x, y, c, chips

    def idx(px, py, pc):
        return 4 * px + 2 * py + pc

    def copy(src, out, sems, k, block, to):
        return pltpu.make_async_remote_copy(
            src_ref=out.at[block, rows] if src is None else src.at[rows], dst_ref=out.at[block, rows],
            send_sem=sems[0].at[k], recv_sem=sems[1].at[k], device_id=to, device_id_type=MESH)

    def start(ins, outs, sems):
        x, y, c, chips = ctx()
        src, out = ins[0], outs[0]
        me = idx(x, y, c)
        pltpu.make_async_copy(src.at[rows], out.at[me, rows], sems[2].at[0]).start()
        copy(src, out, sems, 0, me, (x, y, 1 - c)).start()
        for j, (px, py) in enumerate(chips):
            copy(src, out, sems, 1 + j, me, (px, py, c)).start()

    def mid(ins, outs, sems):
        x, y, c, chips = ctx()
        out = outs[0]
        for j, (px, py) in enumerate(chips):
            copy(None, out, sems, 1 + j, idx(px, py, c), (x, y, c)).wait_recv()
            copy(None, out, sems, 4 + j, idx(px, py, c), (x, y, 1 - c)).start()

    def end(ins, outs, sems):
        x, y, c, chips = ctx()
        src, out = ins[0], outs[0]
        me = (x, y, c)
        copy(None, out, sems, 0, idx(x, y, 1 - c), me).wait_recv()
        for j, (px, py) in enumerate(chips):
            copy(None, out, sems, 4 + j, idx(px, py, 1 - c), me).wait_recv()
        for k in range(N_DEV - 1):
            copy(src, out, sems, k, idx(x, y, c), me).wait_send()
        pltpu.make_async_copy(src.at[rows], out.at[idx(x, y, c), rows], sems[2].at[0]).wait()

    out_shape = jax.ShapeDtypeStruct((N_DEV,) + shard.shape, shard.dtype)
    if prev is None:
        return _Job([shard], [out_shape], _job_sems(), start, mid, end)
    return _Job([shard, prev], [out_shape], _job_sems(), start, mid, end, alias={1: 0})


def _exchange_job(src, land, lo=0, hi=1, parts=1):
    n = src.shape[1] // parts
    assert n * parts == src.shape[1]

    def sl(ref, s):
        return ref.at[s, pl.ds(lo * n, (hi - lo) * n)]

    def start(ins, outs, sems):
        me = _my_index()
        pltpu.make_async_copy(sl(ins[0], me), sl(outs[0], me), sems[2].at[0]).start()
        for k, (pos, idx) in enumerate(_peers()):
            pltpu.make_async_remote_copy(
                src_ref=sl(ins[0], idx), dst_ref=sl(outs[0], me), send_sem=sems[0].at[k], recv_sem=sems[1].at[k],
                device_id=pos, device_id_type=MESH).start()

    def mid(ins, outs, sems):
        pass

    def end(ins, outs, sems):
        me = _my_index()
        for k, (pos, idx) in enumerate(_peers()):
            cp = pltpu.make_async_remote_copy(
                src_ref=sl(ins[0], idx), dst_ref=sl(outs[0], idx), send_sem=sems[0].at[k], recv_sem=sems[1].at[k],
                device_id=pos, device_id_type=MESH)
            cp.wait_recv()
            cp.wait_send()
        pltpu.make_async_copy(sl(ins[0], me), sl(outs[0], me), sems[2].at[0]).wait()

    return _Job([src, land], [jax.ShapeDtypeStruct(land.shape, land.dtype)], _job_sems(), start, mid, end, alias={1: 0})


def _adamw_math(w, g, m, v):
    m = ADAM_B1 * m + (1.0 - ADAM_B1) * g
    v = ADAM_B2 * v + (1.0 - ADAM_B2) * (g * g)
    m_hat = m / (1.0 - ADAM_B1 ** ADAM_STEP)
    v_hat = v / (1.0 - ADAM_B2 ** ADAM_STEP)
    delta = -ADAM_LR * (m_hat / (jnp.sqrt(v_hat) + ADAM_EPS) + ADAM_WD * w)
    return delta, m, v


def _sum_slots(ref):
    g = ref[0].astype(F32)
    for s in range(1, N_DEV):
        g = g + ref[s].astype(F32)
    return g


def _adamw_big(name, lands, w, m, v, jobs=()):
    L, R, C = w.shape
    assert len(lands) == L
    tr = _tile(R, max(LANES, (256 * 1024) // C))
    nr = R // tr

    def body(*refs):
        l_refs = refs[:L]
        w_ref, m_ref, v_ref, g_ref, d_ref, nm_ref, nv_ref = refs[L:]
        for li in range(L):
            @pl.when(pl.program_id(0) // nr == li)
            def _(li=li):
                g = _sum_slots(l_refs[li])
                d, nm, nv = _adamw_math(w_ref[...], g, m_ref[...], v_ref[...])
                g_ref[...] = g
                d_ref[...] = d
                nm_ref[...] = nm
                nv_ref[...] = nv

    def land_spec(li):
        return pl.BlockSpec((N_DEV, tr, C), lambda s: (0, jnp.where(s // nr == li, s % nr, 0), 0))

    row = pl.BlockSpec((None, tr, C), lambda s: (s // nr, s % nr, 0))
    return _carry_call(
        body, name=name, steps=L * nr, in_specs=[land_spec(li) for li in range(L)] + [row, row, row],
        out_specs=[row] * 4, out_shape=[jax.ShapeDtypeStruct((L, R, C), F32)] * 4, scratch_shapes=[],
        args=(*lands, w, m, v), jobs=jobs)


def _sum8(name, slots):
    _, R, C = slots.shape

    def body(s_ref, o_ref):
        o_ref[...] = _sum_slots(s_ref)

    return pl.pallas_call(body, name=name, out_shape=jax.ShapeDtypeStruct((R, C), F32))(slots)


def _adamw_small(name, g, w, m, v):
    def body(g_ref, w_ref, m_ref, v_ref, d_ref, nm_ref, nv_ref):
        d, nm, nv = _adamw_math(w_ref[...], g_ref[...], m_ref[...], v_ref[...])
        d_ref[...] = d
        nm_ref[...] = nm
        nv_ref[...] = nv

    return pl.pallas_call(body, name=name, out_shape=[jax.ShapeDtypeStruct(w.shape, F32)] * 3)(g, w, m, v)


def _pack_rows(arrs):
    parts, spans, r0 = [], [], 0
    for a in arrs:
        flat = a.astype(F32).reshape(-1)
        rows = -(-flat.shape[0] // LANES)
        rows = -(-rows // SUBLANES) * SUBLANES
        flat = jnp.pad(flat, (0, rows * LANES - flat.shape[0]))
        parts.append(flat.reshape(rows, LANES))
        spans.append((r0, rows, a.shape))
        r0 += rows
    return jnp.concatenate(parts, axis=0), spans


def _unpack_rows(buf, span):
    r0, rows, shape = span
    n = math.prod(shape)
    return buf[..., r0:r0 + rows, :].reshape(buf.shape[:-2] + (rows * LANES,))[..., :n].reshape(buf.shape[:-2] + shape)


def _col_blocks(w, n_blocks):
    K, N = w.shape
    return jnp.transpose(w.reshape(K, n_blocks, N // n_blocks), (1, 0, 2))


def _from_col_blocks(wb):
    B, K, n = wb.shape
    return jnp.transpose(wb, (1, 0, 2)).reshape(K, B * n)


def kernel(x, norm_gains, hyb_w_in, hyb_conv_a, hyb_conv_b, hyb_conv_b_bias, hyb_rg_w_a, hyb_rg_b_a, hyb_rg_w_x, hyb_rg_b_x, hyb_rg_lambda, hyb_w_out, sb_w_qkv, sb_w_o, mlp_w_up, mlp_w_down, loss_target, m_norm_gains, m_hyb_w_in, m_hyb_conv_a, m_hyb_conv_b, m_hyb_conv_b_bias, m_hyb_rg_w_a, m_hyb_rg_b_a, m_hyb_rg_w_x, m_hyb_rg_b_x, m_hyb_rg_lambda, m_hyb_w_out, m_sb_w_qkv, m_sb_w_o, m_mlp_w_up, m_mlp_w_down, v_norm_gains, v_hyb_w_in, v_hyb_conv_a, v_hyb_conv_b, v_hyb_conv_b_bias, v_hyb_rg_w_a, v_hyb_rg_b_a, v_hyb_rg_w_x, v_hyb_rg_b_x, v_hyb_rg_lambda, v_hyb_w_out, v_sb_w_qkv, v_sb_w_o, v_mlp_w_up, v_mlp_w_down):
    T, D = x.shape[1], x.shape[2]
    me = _my_index()

    small_shards, small_spans = _pack_rows([norm_gains, hyb_conv_a[0], hyb_conv_b[0]])
    (small_all,) = _all_gather("gather_small", [small_shards])
    gains_b = _unpack_rows(small_all, small_spans[0])
    gains = jnp.transpose(gains_b, (1, 2, 0, 3)).reshape(2, 4, D)
    conv_a = _from_col_blocks(_unpack_rows(small_all, small_spans[1]))
    conv_b = _from_col_blocks(_unpack_rows(small_all, small_spans[2]))

    shards = {"w_in": hyb_w_in[0], "w_out": hyb_w_out[0], "w_qkv": sb_w_qkv[0], "w_o": sb_w_o[0],
              "up0": mlp_w_up[0], "up1": mlp_w_up[1], "down0": mlp_w_down[0], "down1": mlp_w_down[1]}
    shards = {n: s.astype(BF16) for n, s in shards.items()}
    first = _run_jobs("gather_first", [_gather_job(shards[n]) for n in GATHER_FIRST])
    wts = _Weights({n: _gathered_layout(n, g) for n, g in zip(GATHER_FIRST, first)}, shards, GATHER_PLAN)
    assert not set(GATHER_FIRST) & {e[0] for es in GATHER_PLAN.values() for e in es}
    grads_big = _Grads({n: lax.empty((N_DEV,) + s.shape, BF16) for n, s in shards.items()}, EXCHANGE_PLAN)

    sq, grad_x, small = _local_step(
        x[0], loss_target[0], gains, conv_a, conv_b, hyb_conv_b_bias, hyb_rg_w_a[0], hyb_rg_b_a, hyb_rg_w_x[0],
        hyb_rg_b_x, hyb_rg_lambda, wts, grads_big)
    loss = lax.psum(0.5 * sq / D, ("x", "y", "c"))


    names = ["norm_gains", "hyb_w_in", "hyb_conv_a", "hyb_conv_b", "hyb_conv_b_bias", "hyb_rg_w_a", "hyb_rg_b_a",
             "hyb_rg_w_x", "hyb_rg_b_x", "hyb_rg_lambda", "hyb_w_out", "sb_w_qkv", "sb_w_o", "mlp_w_up", "mlp_w_down"]
    params = dict(zip(names, [norm_gains, hyb_w_in, hyb_conv_a, hyb_conv_b, hyb_conv_b_bias, hyb_rg_w_a, hyb_rg_b_a,
                              hyb_rg_w_x, hyb_rg_b_x, hyb_rg_lambda, hyb_w_out, sb_w_qkv, sb_w_o, mlp_w_up, mlp_w_down]))
    moms = dict(zip(names, [m_norm_gains, m_hyb_w_in, m_hyb_conv_a, m_hyb_conv_b, m_hyb_conv_b_bias, m_hyb_rg_w_a,
                            m_hyb_rg_b_a, m_hyb_rg_w_x, m_hyb_rg_b_x, m_hyb_rg_lambda, m_hyb_w_out, m_sb_w_qkv,
                            m_sb_w_o, m_mlp_w_up, m_mlp_w_down]))
    vars_ = dict(zip(names, [v_norm_gains, v_hyb_w_in, v_hyb_conv_a, v_hyb_conv_b, v_hyb_conv_b_bias, v_hyb_rg_w_a,
                             v_hyb_rg_b_a, v_hyb_rg_w_x, v_hyb_rg_b_x, v_hyb_rg_lambda, v_hyb_w_out, v_sb_w_qkv,
                             v_sb_w_o, v_mlp_w_up, v_mlp_w_down]))
    grads, deltas, new_m, new_v = {}, {}, {}, {}

    big_lands = {"mlp_w_down": ["down0", "down1"], "mlp_w_up": ["up0", "up1"], "sb_w_qkv": ["w_qkv"], "sb_w_o": ["w_o"],
                 "hyb_w_out": ["w_out"], "hyb_w_in": ["w_in"]}
    for nm, keys in big_lands.items():
        call = f"adamw_{nm}"
        if nm == "hyb_w_in":
            grads_big.flush("exchange_grads")
        jobs = grads_big.jobs(call)
        assert not {k for k in keys} & {e[0] for e in EXCHANGE_PLAN.get(call, ())}
        outs = _adamw_big(call, [grads_big.lands[k] for k in keys], params[nm], moms[nm], vars_[nm], jobs=jobs)
        grads[nm], deltas[nm], new_m[nm], new_v[nm] = outs[:4]
        grads_big.deliver(call, outs[4:])

    small_names = ["norm_gains", "hyb_conv_a", "hyb_conv_b", "hyb_conv_b_bias", "hyb_rg_w_a", "hyb_rg_b_a",
                   "hyb_rg_w_x", "hyb_rg_b_x", "hyb_rg_lambda"]
    small_keys = ["norm_gains", "conv_a", "conv_b", "conv_b_bias", "rg_w_a", "rg_b_a", "rg_w_x", "rg_b_x", "rg_lambda"]
    sg_buf, sg_spans = _pack_rows([small[k] for k in small_keys])
    (sg_all,) = _all_gather("gather_small_grads", [sg_buf])
    sg_sum = _sum8("sum_small_grads", sg_all)
    full = {nm: _unpack_rows(sg_sum, sp) for nm, sp in zip(small_names, sg_spans)}
    cb = (D // 2) // N_DEV
    small_grads = {
        "norm_gains": lax.dynamic_slice_in_dim(full["norm_gains"], me * (D // N_DEV), D // N_DEV, axis=2),
        "hyb_conv_a": lax.dynamic_slice_in_dim(full["hyb_conv_a"], me * cb, cb, axis=1)[None],
        "hyb_conv_b": lax.dynamic_slice_in_dim(full["hyb_conv_b"], me * cb, cb, axis=1)[None],
        "hyb_conv_b_bias": full["hyb_conv_b_bias"],
        "hyb_rg_w_a": full["hyb_rg_w_a"][None],
        "hyb_rg_b_a": full["hyb_rg_b_a"],
        "hyb_rg_w_x": full["hyb_rg_w_x"][None],
        "hyb_rg_b_x": full["hyb_rg_b_x"],
        "hyb_rg_lambda": full["hyb_rg_lambda"],
    }
    pk = lambda d: _pack_rows([d[nm] for nm in small_names])
    g_buf, spans = pk(small_grads)
    w_buf, _ = pk(params)
    m_buf, _ = pk(moms)
    v_buf, _ = pk(vars_)
    d_buf, nm_buf, nv_buf = _adamw_small("adamw_small", g_buf, w_buf, m_buf, v_buf)
    for nm, sp in zip(small_names, spans):
        grads[nm] = small_grads[nm]
        deltas[nm], new_m[nm], new_v[nm] = _unpack_rows(d_buf, sp), _unpack_rows(nm_buf, sp), _unpack_rows(nv_buf, sp)

    return (loss, grad_x[None], *[grads[n] for n in names], *[deltas[n] for n in names],
            *[new_m[n] for n in names], *[new_v[n] for n in names])
```

```python
import functools
import math

import jax
import jax.numpy as jnp
from jax import lax
from jax.experimental import pallas as pl
from jax.experimental.pallas import tpu as pltpu

F32 = jnp.float32
BF16 = jnp.bfloat16

NORM_EPS = 1e-6
LRU_C = 8.0
ATT_HEAD_DIM = 128
RG_HEAD_DIM = 64
LANES = 128
SUBLANES = 8
N_DEV = 8
ADAM_LR = 0.001
ADAM_B1 = 0.9
ADAM_B2 = 0.999
ADAM_EPS = 1e-08
ADAM_WD = 0.01
ADAM_STEP = 10
VMEM_LIMIT = 56 * 1024 * 1024
MM_TK = 2048
MESH = pl.DeviceIdType.MESH


def _tile(n, pref):
    if n <= pref:
        return n
    t = (pref // LANES) * LANES
    while t > LANES and n % t:
        t -= LANES
    assert n % t == 0, (n, pref)
    return t


def _params(sem):
    return pltpu.CompilerParams(dimension_semantics=sem, vmem_limit_bytes=VMEM_LIMIT)


DIMS_NN = (((1,), (0,)), ((), ()))
DIMS_NT = (((1,), (1,)), ((), ()))
DIMS_TN = (((0,), (0,)), ((), ()))


_ANY = pl.BlockSpec(memory_space=pl.ANY)


class _Job:
    def __init__(self, ins, outs, sems, start, mid, end, alias=None):
        self.ins, self.outs, self.sems = ins, outs, sems
        self.start, self.mid, self.end = start, mid, end
        self.alias = alias or {}


def _job_plumbing(jobs, n_in, n_out):
    j_ins = [a for jb in jobs for a in jb.ins]
    j_outs = [o for jb in jobs for o in jb.outs]
    j_sems = [s for jb in jobs for s in jb.sems]
    aliases, pi, po = {}, 0, 0
    for jb in jobs:
        for i_in, i_out in jb.alias.items():
            aliases[n_in + pi + i_in] = n_out + po + i_out
        pi += len(jb.ins)
        po += len(jb.outs)
    return j_ins, j_outs, j_sems, aliases


def _job_phase(jobs, which, jin, jout, jsem):
    pi = po = ps = 0
    for jb in jobs:
        getattr(jb, which)(jin[pi:pi + len(jb.ins)], jout[po:po + len(jb.outs)], jsem[ps:ps + len(jb.sems)])
        pi, po, ps = pi + len(jb.ins), po + len(jb.outs), ps + len(jb.sems)


def _run_jobs(name, jobs):
    j_ins, j_outs, j_sems, aliases = _job_plumbing(jobs, 0, 0)
    n_ji, n_jo = len(j_ins), len(j_outs)

    def body(*refs):
        jin, jout, jsem = refs[:n_ji], refs[n_ji:n_ji + n_jo], refs[n_ji + n_jo:]
        for which in ("start", "mid", "end"):
            _job_phase(jobs, which, jin, jout, jsem)

    return pl.pallas_call(body, name=name, in_specs=[_ANY] * n_ji, out_specs=[_ANY] * n_jo, out_shape=j_outs,
                          scratch_shapes=j_sems, input_output_aliases=aliases)(*j_ins)


def _carry_call(body, *, name, steps, in_specs, out_specs, out_shape, scratch_shapes, args, jobs=(), aliases=None):
    n_in, n_out, n_sc = len(in_specs), len(out_shape), len(scratch_shapes)
    j_ins, j_outs, j_sems, j_aliases = _job_plumbing(jobs, n_in, n_out)
    n_ji, n_jo = len(j_ins), len(j_outs)

    def wrapped(*refs):
        ins, jin = refs[:n_in], refs[n_in:n_in + n_ji]
        o0 = n_in + n_ji
        outs, jout = refs[o0:o0 + n_out], refs[o0 + n_out:o0 + n_out + n_jo]
        s0 = o0 + n_out + n_jo
        scratch, jsem = refs[s0:s0 + n_sc], refs[s0 + n_sc:]
        step = pl.program_id(0)
        if jobs:
            pl.when(step == 0)(lambda: _job_phase(jobs, "start", jin, jout, jsem))
            pl.when(step == (4 * steps) // 5)(lambda: _job_phase(jobs, "mid", jin, jout, jsem))
        body(*ins, *outs, *scratch)
        if jobs:
            pl.when(step == steps - 1)(lambda: _job_phase(jobs, "end", jin, jout, jsem))

    return pl.pallas_call(
        wrapped, name=name, grid=(steps,),
        in_specs=list(in_specs) + [_ANY] * n_ji, out_specs=list(out_specs) + [_ANY] * n_jo,
        out_shape=list(out_shape) + j_outs, scratch_shapes=list(scratch_shapes) + j_sems,
        input_output_aliases={**(aliases or {}), **j_aliases},
        compiler_params=_params(("arbitrary",) if jobs else ("parallel",)))(*args, *j_ins)


def _matmul(name, ins, outs, grid, dims, acc_shape, epilogue=None, jobs=()):
    n_in, n_out, nk = len(ins), len(outs), grid[2]
    j_ins, j_outs, j_sems, aliases = _job_plumbing(jobs, n_in, n_out)
    n_ji, n_jo = len(j_ins), len(j_outs)
    total = grid[0] * grid[1] * grid[2]
    n_acc = 0 if nk == 1 else 1

    def body(*refs):
        a_ref, b_ref = refs[0], refs[1]
        extras = refs[2:n_in]
        jin = refs[n_in:n_in + n_ji]
        out_refs = refs[n_in + n_ji:n_in + n_ji + n_out]
        jout = refs[n_in + n_ji + n_out:n_in + n_ji + n_out + n_jo]
        jsem = refs[n_in + n_ji + n_out + n_jo + n_acc:]
        k = pl.program_id(2)
        step = (pl.program_id(0) * grid[1] + pl.program_id(1)) * grid[2] + k
        if jobs:
            pl.when(step == 0)(lambda: _job_phase(jobs, "start", jin, jout, jsem))
            pl.when(step == (4 * total) // 5)(lambda: _job_phase(jobs, "mid", jin, jout, jsem))

        def finish(r):
            res = epilogue(r, *[e[...] for e in extras]) if epilogue is not None else (r,)
            for o, v in zip(out_refs, res):
                o[...] = v.astype(o.dtype)

        if len(b_ref.shape) == 2:
            prod = lax.dot_general(a_ref[...], b_ref[...], dims, preferred_element_type=F32)
        else:
            kb = a_ref.shape[1] // b_ref.shape[0]
            prod = sum(lax.dot_general(a_ref[:, g * kb:(g + 1) * kb], b_ref[g], dims, preferred_element_type=F32)
                       for g in range(b_ref.shape[0]))
        if nk == 1:
            finish(prod)
        else:
            acc = refs[n_in + n_ji + n_out + n_jo]

            @pl.when(k == 0)
            def _():
                acc[...] = prod

            @pl.when((k > 0) & (k < nk - 1))
            def _():
                acc[...] += prod

            @pl.when(k == nk - 1)
            def _():
                finish(acc[...] + prod)

        if jobs:
            pl.when(step == total - 1)(lambda: _job_phase(jobs, "end", jin, jout, jsem))

    sem = ("arbitrary",) * 3 if jobs else ("parallel", "parallel", "arbitrary")
    res = pl.pallas_call(
        body, name=name, grid=grid,
        in_specs=[s for _, s in ins] + [_ANY] * n_ji,
        out_specs=[s for _, s in outs] + [_ANY] * n_jo,
        out_shape=[s for s, _ in outs] + j_outs,
        scratch_shapes=[pltpu.VMEM(acc_shape, F32)] * n_acc + j_sems,
        input_output_aliases=aliases,
        compiler_params=_params(sem),
    )(*[a for a, _ in ins], *j_ins)
    return res


def _mm_nn(name, a, b, out_dtype, *, a_seg=None, out_seg=None, tm=1024, tn=1024, tk=MM_TK, epilogue=None,
           extras=(), n_out=1, out_dtypes=None, jobs=()):
    if a_seg:
        _, M, ks = a.shape
        K = ks * a_seg
    else:
        M, K = a.shape
        ks = K
    N = b.shape[1]
    ns = N // out_seg if out_seg else N
    tm, tn, tk = _tile(M, tm), _tile(ns, tn), _tile(ks, tk)
    nks, nns = ks // tk, ns // tn
    grid = (M // tm, N // tn, K // tk)
    if a_seg:
        a_spec = pl.BlockSpec((None, tm, tk), lambda i, j, k: (k // nks, i, k % nks))
    else:
        a_spec = pl.BlockSpec((tm, tk), lambda i, j, k: (i, k))
    b_spec = pl.BlockSpec((tk, tn), lambda i, j, k: (k, j))
    if out_seg:
        o_spec = pl.BlockSpec((None, tm, tn), lambda i, j, k: (j // nns, i, j % nns))
        o_shape = (out_seg, M, ns)
    else:
        o_spec = pl.BlockSpec((tm, tn), lambda i, j, k: (i, j))
        o_shape = (M, N)
    dts = out_dtypes or [out_dtype] * n_out
    outs = [(jax.ShapeDtypeStruct(o_shape, dt), o_spec) for dt in dts]
    ins = [(a, a_spec), (b, b_spec)] + [(e, o_spec) for e in extras]
    return _matmul(name, ins, outs, grid, DIMS_NN, (tm, tn), epilogue, jobs)


def _mm_nt(name, a, b, out_dtype, *, a_seg=None, out_seg=None, tm=1024, tn=1024, tk=MM_TK, epilogue=None, extras=(),
           jobs=()):
    if a_seg:
        _, M, ks = a.shape
        K = ks * a_seg
    else:
        M, K = a.shape
        ks = K
    N = b.shape[0]
    ns = N // out_seg if out_seg else N
    tm, tn, tk = _tile(M, tm), _tile(ns, tn), _tile(ks, tk)
    nks, nns = ks // tk, ns // tn
    grid = (M // tm, N // tn, K // tk)
    if a_seg:
        a_spec = pl.BlockSpec((None, tm, tk), lambda i, j, k: (k // nks, i, k % nks))
    else:
        a_spec = pl.BlockSpec((tm, tk), lambda i, j, k: (i, k))
    b_spec = pl.BlockSpec((tn, tk), lambda i, j, k: (j, k))
    if out_seg:
        o_spec = pl.BlockSpec((None, tm, tn), lambda i, j, k: (j // nns, i, j % nns))
        o_shape = (out_seg, M, ns)
    else:
        o_spec = pl.BlockSpec((tm, tn), lambda i, j, k: (i, j))
        o_shape = (M, N)
    outs = [(jax.ShapeDtypeStruct(o_shape, out_dtype), o_spec)]
    ins = [(a, a_spec), (b, b_spec)] + [(e, o_spec) for e in extras]
    return _matmul(name, ins, outs, grid, DIMS_NT, (tm, tn), epilogue, jobs)


def _mm_tn(name, a, b, out_dtype, *, a_seg=None, b_seg=None, out_blocks=None, tm=1024, tn=1024, tk=MM_TK, jobs=()):
    if a_seg:
        _, T, ms = a.shape
        M = ms * a_seg
    else:
        T, M = a.shape
        ms = M
    if b_seg:
        _, _, ns = b.shape
        N = ns * b_seg
    else:
        N = b.shape[1]
        ns = N
    nb_cols = N // out_blocks if out_blocks else N
    tm, tk = _tile(ms, tm), _tile(T, tk)
    tn = _tile(math.gcd(ns, nb_cols), tn)
    nms, nns, nbs = ms // tm, ns // tn, nb_cols // tn
    grid = (M // tm, N // tn, T // tk)
    if a_seg:
        a_spec = pl.BlockSpec((None, tk, tm), lambda i, j, k: (i // nms, k, i % nms))
    else:
        a_spec = pl.BlockSpec((tk, tm), lambda i, j, k: (k, i))
    if b_seg:
        b_spec = pl.BlockSpec((None, tk, tn), lambda i, j, k: (j // nns, k, j % nns))
    else:
        b_spec = pl.BlockSpec((tk, tn), lambda i, j, k: (k, j))
    if out_blocks:
        o_spec = pl.BlockSpec((None, tm, tn), lambda i, j, k: (j // nbs, i, j % nbs))
        o_shape = (out_blocks, M, nb_cols)
    else:
        o_spec = pl.BlockSpec((tm, tn), lambda i, j, k: (i, j))
        o_shape = (M, N)
    outs = [(jax.ShapeDtypeStruct(o_shape, out_dtype), o_spec)]
    return _matmul(name, [(a, a_spec), (b, b_spec)], outs, grid, DIMS_TN, (tm, tn), None, jobs)


def _rms(x):
    return lax.rsqrt(jnp.mean(x * x, axis=-1, keepdims=True) + NORM_EPS)


def _row_tile(T):
    return _tile(T, 256)


def _norm_fwd(name, x, g):
    T, D = x.shape
    tr = _row_tile(T)

    def body(x_ref, g_ref, h_ref):
        xv = x_ref[...]
        h_ref[...] = (xv * _rms(xv) * g_ref[...]).astype(h_ref.dtype)

    row = pl.BlockSpec((tr, D), lambda i: (i, 0))
    vec = pl.BlockSpec((1, D), lambda i: (0, 0))
    return pl.pallas_call(body, name=name, grid=(T // tr,), in_specs=[row, vec], out_specs=row,
                          out_shape=jax.ShapeDtypeStruct((T, D), BF16), compiler_params=_params(("parallel",)))(x, g)


def _resid_norm(name, x, br, g_post, g_next):
    T, D = x.shape
    tr = _row_tile(T)

    def body(x_ref, br_ref, gp_ref, gn_ref, xn_ref, h_ref):
        b = br_ref[...]
        xn = x_ref[...] + b * _rms(b) * gp_ref[...]
        xn_ref[...] = xn
        h_ref[...] = (xn * _rms(xn) * gn_ref[...]).astype(h_ref.dtype)

    row = pl.BlockSpec((tr, D), lambda i: (i, 0))
    vec = pl.BlockSpec((1, D), lambda i: (0, 0))
    return pl.pallas_call(body, name=name, grid=(T // tr,), in_specs=[row, row, vec, vec], out_specs=[row, row],
                          out_shape=[jax.ShapeDtypeStruct((T, D), F32), jax.ShapeDtypeStruct((T, D), BF16)],
                          compiler_params=_params(("parallel",)))(x, br, g_post, g_next)


def _final_loss(name, x, br, g_post, target):
    T, D = x.shape
    tr = _row_tile(T)

    def body(x_ref, br_ref, gp_ref, t_ref, dy_ref, ls_ref):
        b = br_ref[...]
        err = x_ref[...] + b * _rms(b) * gp_ref[...] - t_ref[...]
        dy_ref[...] = err * (1.0 / D)

        @pl.when(pl.program_id(0) == 0)
        def _():
            ls_ref[...] = jnp.zeros_like(ls_ref)

        ls_ref[...] += jnp.sum(err * err)

    row = pl.BlockSpec((tr, D), lambda i: (i, 0))
    vec = pl.BlockSpec((1, D), lambda i: (0, 0))
    acc = pl.BlockSpec((SUBLANES, LANES), lambda i: (0, 0))
    return pl.pallas_call(body, name=name, grid=(T // tr,), in_specs=[row, row, vec, row], out_specs=[row, acc],
                          out_shape=[jax.ShapeDtypeStruct((T, D), F32), jax.ShapeDtypeStruct((SUBLANES, LANES), F32)],
                          compiler_params=_params(("arbitrary",)))(x, br, g_post, target)


def _norm_bwd(name, x, g, dy, add, out_dtype, jobs=()):
    T, D = x.shape
    tr = _row_tile(T)
    has_add = add is not None

    def body(*refs):
        if has_add:
            x_ref, g_ref, dy_ref, add_ref, dx_ref, dg_ref = refs
        else:
            x_ref, g_ref, dy_ref, dx_ref, dg_ref = refs
        xv = x_ref[...]
        r = _rms(xv)
        xhat = xv * r
        dyv = dy_ref[...].astype(F32)
        gdy = dyv * g_ref[...]
        dx = r * (gdy - xhat * jnp.mean(gdy * xhat, axis=-1, keepdims=True))
        if has_add:
            dx = dx + add_ref[...]
        dx_ref[...] = dx.astype(dx_ref.dtype)

        @pl.when(pl.program_id(0) == 0)
        def _():
            dg_ref[...] = jnp.zeros_like(dg_ref)

        dg_ref[...] += jnp.sum(dyv * xhat, axis=0, keepdims=True)

    row = pl.BlockSpec((tr, D), lambda i: (i, 0))
    vec = pl.BlockSpec((1, D), lambda i: (0, 0))
    ins = [x, g, dy] + ([add] if has_add else [])
    specs = [row, vec, row] + ([row] if has_add else [])
    if not jobs:
        return pl.pallas_call(body, name=name, grid=(T // tr,), in_specs=specs, out_specs=[row, vec],
                              out_shape=[jax.ShapeDtypeStruct((T, D), out_dtype), jax.ShapeDtypeStruct((1, D), F32)],
                              compiler_params=_params(("arbitrary",)))(*ins)
    return _carry_call(body, name=name, steps=T // tr, in_specs=specs, out_specs=[row, vec],
                       out_shape=[jax.ShapeDtypeStruct((T, D), out_dtype), jax.ShapeDtypeStruct((1, D), F32)],
                       scratch_shapes=[], args=ins, jobs=jobs)


HALO = SUBLANES
TIME_CHUNK = 512


def _chunks(T):
    tc = min(TIME_CHUNK, T)
    assert T % tc == 0 and tc % SUBLANES == 0
    return [(t0, tc) for t0 in range(0, T, tc)]


def _log_sigmoid(x):
    return -(jnp.maximum(-x, 0.0) + jnp.log(1.0 + jnp.exp(-jnp.abs(x))))


def _one_minus_exp(x):
    series = -x * (1.0 + x * (0.5 + x * (1.0 / 6.0 + x * (1.0 / 24.0))))
    return jnp.where(x > -0.01, series, 1.0 - jnp.exp(x))


_GELU_C = math.sqrt(2.0 / math.pi)


def _gelu(x):
    return 0.5 * x * (1.0 + jnp.tanh(_GELU_C * (x + 0.044715 * x * x * x)))


def _gelu_grad(x):
    th = jnp.tanh(_GELU_C * (x + 0.044715 * x * x * x))
    return 0.5 * (1.0 + th) + 0.5 * x * (1.0 - th * th) * _GELU_C * (1.0 + 3.0 * 0.044715 * x * x)


def _tile_scan(a, b, reverse):
    rows = a.shape[0]
    pos = lax.broadcasted_iota(jnp.int32, a.shape, 0) & (SUBLANES - 1)
    for d in (1, 2, 4):
        if reverse:
            ok = pos < SUBLANES - d
            shift = rows - d
        else:
            ok = pos >= d
            shift = d
        a_sh = jnp.where(ok, pltpu.roll(a, shift, 0), 1.0)
        b_sh = jnp.where(ok, pltpu.roll(b, shift, 0), 0.0)
        b = a * b_sh + b
        a = a * a_sh
    return a, b


def _carry_scan(a_s, b_s, T, reverse):
    n = T // SUBLANES
    edge = 0 if reverse else SUBLANES - 1

    def step(j, carry):
        g = (n - 1 - j) if reverse else j
        r = pl.multiple_of(g * SUBLANES, SUBLANES)
        h = b_s[pl.ds(r, SUBLANES), :] + a_s[pl.ds(r, SUBLANES), :] * carry
        b_s[pl.ds(r, SUBLANES), :] = h
        return jnp.broadcast_to(h[edge:edge + 1, :], h.shape)

    lax.fori_loop(0, n, step, jnp.zeros((SUBLANES, a_s.shape[1]), F32))


def _seg_spec(T, seg, nblk):
    return pl.BlockSpec((None, T, LANES), lambda c: (seg, 0, c))


def _rows_to_tile(rows):
    idx = lax.broadcasted_iota(jnp.int32, (SUBLANES, LANES), 0)
    out = jnp.zeros((SUBLANES, LANES), F32)
    for k, r in enumerate(rows):
        out = jnp.where(idx == k, r, out)
    return out


def _mixer_a_fwd(proj, conv_a):
    _, T, C = proj.shape
    nblk = C // LANES
    chunks = _chunks(T)

    def body(bg_ref, cg_ref, ax_ref, w_ref, y_ref, p_s):
        p_s[pl.ds(0, HALO), :] = jnp.zeros((HALO, LANES), F32)
        for t0, tc in chunks:
            p_s[pl.ds(HALO + t0, tc), :] = cg_ref[pl.ds(t0, tc), :] * ax_ref[pl.ds(t0, tc), :]
        w = w_ref[...]
        for t0, tc in chunks:
            c = (w[2:3, :] * p_s[pl.ds(HALO + t0, tc), :] + w[1:2, :] * p_s[pl.ds(HALO + t0 - 1, tc), :]
                 + w[0:1, :] * p_s[pl.ds(HALO + t0 - 2, tc), :])
            y_ref[pl.ds(t0, tc), :] = (bg_ref[pl.ds(t0, tc), :] * c).astype(y_ref.dtype)

    return pl.pallas_call(
        body, name="mixer_a_fwd", grid=(nblk,),
        in_specs=[_seg_spec(T, 0, nblk), _seg_spec(T, 1, nblk), _seg_spec(T, 2, nblk),
                  pl.BlockSpec((3, LANES), lambda c: (0, c))],
        out_specs=_seg_spec(T, 0, nblk),
        out_shape=jax.ShapeDtypeStruct((2, T, C), BF16),
        scratch_shapes=[pltpu.VMEM((T + HALO, LANES), F32)],
        compiler_params=_params(("parallel",)))(proj, proj, proj, conv_a)


def _mixer_a_bwd(proj, conv_a, dy):
    _, T, C = proj.shape
    nblk = C // LANES
    chunks = _chunks(T)

    def body(bg_ref, cg_ref, ax_ref, w_ref, dy_ref, dp_ref, dw_ref, p_s, dc_s):
        p_s[pl.ds(0, HALO), :] = jnp.zeros((HALO, LANES), F32)
        dc_s[pl.ds(T, HALO), :] = jnp.zeros((HALO, LANES), F32)
        for t0, tc in chunks:
            p_s[pl.ds(HALO + t0, tc), :] = cg_ref[pl.ds(t0, tc), :] * ax_ref[pl.ds(t0, tc), :]
        w = w_ref[...]
        for t0, tc in chunks:
            c = (w[2:3, :] * p_s[pl.ds(HALO + t0, tc), :] + w[1:2, :] * p_s[pl.ds(HALO + t0 - 1, tc), :]
                 + w[0:1, :] * p_s[pl.ds(HALO + t0 - 2, tc), :])
            dyv = dy_ref[pl.ds(t0, tc), :]
            dp_ref[0, pl.ds(t0, tc), :] = (dyv * c).astype(dp_ref.dtype)
            dc_s[pl.ds(t0, tc), :] = dyv * bg_ref[pl.ds(t0, tc), :]
        dw = [jnp.zeros((1, LANES), F32) for _ in range(3)]
        for t0, tc in chunks:
            dc = dc_s[pl.ds(t0, tc), :]
            dpv = w[2:3, :] * dc + w[1:2, :] * dc_s[pl.ds(t0 + 1, tc), :] + w[0:1, :] * dc_s[pl.ds(t0 + 2, tc), :]
            dp_ref[1, pl.ds(t0, tc), :] = (dpv * ax_ref[pl.ds(t0, tc), :]).astype(dp_ref.dtype)
            dp_ref[2, pl.ds(t0, tc), :] = (dpv * cg_ref[pl.ds(t0, tc), :]).astype(dp_ref.dtype)
            for k in range(3):
                dw[k] = dw[k] + jnp.sum(dc * p_s[pl.ds(HALO + t0 - (2 - k), tc), :], axis=0, keepdims=True)
        dw_ref[...] = _rows_to_tile(dw)

    return pl.pallas_call(
        body, name="mixer_a_bwd", grid=(nblk,),
        in_specs=[_seg_spec(T, 0, nblk), _seg_spec(T, 1, nblk), _seg_spec(T, 2, nblk),
                  pl.BlockSpec((3, LANES), lambda c: (0, c)), _seg_spec(T, 0, nblk)],
        out_specs=[pl.BlockSpec((3, T, LANES), lambda c: (0, 0, c)),
                   pl.BlockSpec((None, SUBLANES, LANES), lambda c: (c, 0, 0))],
        out_shape=[jax.ShapeDtypeStruct((6, T, C), BF16), jax.ShapeDtypeStruct((nblk, SUBLANES, LANES), F32)],
        scratch_shapes=[pltpu.VMEM((T + HALO, LANES), F32), pltpu.VMEM((T + HALO, LANES), F32)],
        compiler_params=_params(("parallel",)))(proj, proj, proj, conv_a, dy)


def _rg_gates(xr, wa, ba, wx, bx, ls):
    xb = xr.astype(BF16)
    r = jax.nn.sigmoid(jnp.dot(xb, wa, preferred_element_type=F32) + ba)
    i = jax.nn.sigmoid(jnp.dot(xb, wx, preferred_element_type=F32) + bx)
    log_a = LRU_C * r * ls
    a = jnp.exp(log_a)
    mult = jnp.sqrt(_one_minus_exp(2.0 * log_a))
    return r, i, a, mult


def _conv4(xh_s, cw, bias, t0, tc):
    return (cw[3:4, :] * xh_s[pl.ds(HALO + t0, tc), :] + cw[2:3, :] * xh_s[pl.ds(HALO + t0 - 1, tc), :]
            + cw[1:2, :] * xh_s[pl.ds(HALO + t0 - 2, tc), :] + cw[0:1, :] * xh_s[pl.ds(HALO + t0 - 3, tc), :] + bias)


def _mixer_b_specs(T, nblk):
    vec = pl.BlockSpec((1, LANES), lambda c: (0, c))
    mat = pl.BlockSpec((None, LANES, LANES), lambda c: (c, 0, 0))
    return [_seg_spec(T, 3, nblk), _seg_spec(T, 4, nblk), pl.BlockSpec((4, LANES), lambda c: (0, c)),
            vec, mat, vec, mat, vec, vec]


def _mixer_b_fwd(name, proj, conv_b, bias, wa, ba, wx, bx, lam, y, jobs=()):
    _, T, C = proj.shape
    nblk = C // LANES
    chunks = _chunks(T)

    def body(gate_ref, x_ref, cw_ref, cb_ref, wa_ref, ba_ref, wx_ref, bx_ref, lam_ref, y_in, y_ref, xh_s, a_s, b_s):
        xh_s[pl.ds(0, HALO), :] = jnp.zeros((HALO, LANES), F32)
        for t0, tc in chunks:
            xh_s[pl.ds(HALO + t0, tc), :] = x_ref[pl.ds(t0, tc), :]
        cw, bias_v = cw_ref[...], cb_ref[...]
        ls = _log_sigmoid(lam_ref[...])
        for t0, tc in chunks:
            xr = _conv4(xh_s, cw, bias_v, t0, tc)
            r, i, a, mult = _rg_gates(xr, wa_ref[...], ba_ref[...], wx_ref[...], bx_ref[...], ls)
            ac, hc = _tile_scan(a, mult * i * xr, reverse=False)
            a_s[pl.ds(t0, tc), :] = ac
            b_s[pl.ds(t0, tc), :] = hc
        _carry_scan(a_s, b_s, T, reverse=False)
        for t0, tc in chunks:
            y_ref[pl.ds(t0, tc), :] = (b_s[pl.ds(t0, tc), :] * _gelu(gate_ref[pl.ds(t0, tc), :])).astype(y_ref.dtype)

    return _carry_call(
        body, name=name, steps=nblk, in_specs=_mixer_b_specs(T, nblk) + [_ANY],
        out_specs=[_seg_spec(T, 1, nblk)], out_shape=[jax.ShapeDtypeStruct(y.shape, y.dtype)],
        scratch_shapes=[pltpu.VMEM((T + HALO, LANES), F32), pltpu.VMEM((T, LANES), F32), pltpu.VMEM((T, LANES), F32)],
        args=(proj, proj, conv_b, bias, wa, ba, wx, bx, lam, y), jobs=jobs, aliases={9: 0})


_ROW_CONV, _ROW_BIAS, _ROW_BA, _ROW_BX, _ROW_LAM = 0, 4, 5, 6, 7


def _mixer_b_bwd(name, proj, conv_b, bias, wa, ba, wx, bx, lam, dy, dproj, jobs=()):
    _, T, C = proj.shape
    nblk = C // LANES
    chunks = _chunks(T)

    def body(gate_ref, x_ref, cw_ref, cb_ref, wa_ref, ba_ref, wx_ref, bx_ref, lam_ref, dy_ref, dp_in,
             dp_ref, sm_ref, dwa_ref, dwx_ref, xh_s, xr_s, r_s, i_s, a_s, h_s, sa_s, sb_s, dx_s):
        zero_halo = jnp.zeros((HALO, LANES), F32)
        xh_s[pl.ds(0, HALO), :] = zero_halo
        h_s[pl.ds(0, HALO), :] = zero_halo
        a_s[pl.ds(T, HALO), :] = zero_halo
        dx_s[pl.ds(T, HALO), :] = zero_halo
        for t0, tc in chunks:
            xh_s[pl.ds(HALO + t0, tc), :] = x_ref[pl.ds(t0, tc), :]
        cw, bias_v = cw_ref[...], cb_ref[...]
        lam_v = lam_ref[...]
        ls = _log_sigmoid(lam_v)
        wa_v, wx_v, ba_v, bx_v = wa_ref[...], wx_ref[...], ba_ref[...], bx_ref[...]
        for t0, tc in chunks:
            xr = _conv4(xh_s, cw, bias_v, t0, tc)
            r, i, a, mult = _rg_gates(xr, wa_v, ba_v, wx_v, bx_v, ls)
            xr_s[pl.ds(t0, tc), :] = xr
            r_s[pl.ds(t0, tc), :] = r
            i_s[pl.ds(t0, tc), :] = i
            a_s[pl.ds(t0, tc), :] = a
            ac, hc = _tile_scan(a, mult * i * xr, reverse=False)
            sa_s[pl.ds(t0, tc), :] = ac
            sb_s[pl.ds(t0, tc), :] = hc
        _carry_scan(sa_s, sb_s, T, reverse=False)
        for t0, tc in chunks:
            h_s[pl.ds(HALO + t0, tc), :] = sb_s[pl.ds(t0, tc), :]
        for t0, tc in chunks:
            gv = gate_ref[pl.ds(t0, tc), :]
            dyv = dy_ref[pl.ds(t0, tc), :]
            dp_ref[0, pl.ds(t0, tc), :] = (dyv * h_s[pl.ds(HALO + t0, tc), :] * _gelu_grad(gv)).astype(dp_ref.dtype)
            ac, gc = _tile_scan(a_s[pl.ds(t0 + 1, tc), :], dyv * _gelu(gv), reverse=True)
            sa_s[pl.ds(t0, tc), :] = ac
            sb_s[pl.ds(t0, tc), :] = gc
        _carry_scan(sa_s, sb_s, T, reverse=True)
        acc = {k: jnp.zeros((1, LANES), F32) for k in ("bias", "ba", "bx", "lam")}
        dwa = jnp.zeros((LANES, LANES), F32)
        dwx = jnp.zeros((LANES, LANES), F32)
        for t0, tc in chunks:
            dht = sb_s[pl.ds(t0, tc), :]
            xr, r, i, a = xr_s[pl.ds(t0, tc), :], r_s[pl.ds(t0, tc), :], i_s[pl.ds(t0, tc), :], a_s[pl.ds(t0, tc), :]
            mult = jnp.sqrt(_one_minus_exp(2.0 * LRU_C * r * ls))
            da = dht * h_s[pl.ds(HALO + t0 - 1, tc), :]
            dmult = dht * i * xr
            di = dht * mult * xr
            dlog_a = da * a - dmult * a * a / mult
            dpa = dlog_a * (LRU_C * ls) * r * (1.0 - r)
            dpx = di * i * (1.0 - i)
            acc["lam"] = acc["lam"] + jnp.sum(dlog_a * r, axis=0, keepdims=True)
            acc["ba"] = acc["ba"] + jnp.sum(dpa, axis=0, keepdims=True)
            acc["bx"] = acc["bx"] + jnp.sum(dpx, axis=0, keepdims=True)
            xb, dpab, dpxb = xr.astype(BF16), dpa.astype(BF16), dpx.astype(BF16)
            dwa = dwa + lax.dot_general(xb, dpab, DIMS_TN, preferred_element_type=F32)
            dwx = dwx + lax.dot_general(xb, dpxb, DIMS_TN, preferred_element_type=F32)
            dxr = (dht * mult * i + lax.dot_general(dpab, wa_v, DIMS_NT, preferred_element_type=F32)
                   + lax.dot_general(dpxb, wx_v, DIMS_NT, preferred_element_type=F32))
            acc["bias"] = acc["bias"] + jnp.sum(dxr, axis=0, keepdims=True)
            dx_s[pl.ds(t0, tc), :] = dxr
        dcw = [jnp.zeros((1, LANES), F32) for _ in range(4)]
        for t0, tc in chunks:
            dxr = dx_s[pl.ds(t0, tc), :]
            dxin = (cw[3:4, :] * dxr + cw[2:3, :] * dx_s[pl.ds(t0 + 1, tc), :] + cw[1:2, :] * dx_s[pl.ds(t0 + 2, tc), :]
                    + cw[0:1, :] * dx_s[pl.ds(t0 + 3, tc), :])
            dp_ref[1, pl.ds(t0, tc), :] = dxin.astype(dp_ref.dtype)
            for k in range(4):
                dcw[k] = dcw[k] + jnp.sum(dxr * xh_s[pl.ds(HALO + t0 - (3 - k), tc), :], axis=0, keepdims=True)
        dlam = acc["lam"] * LRU_C * jax.nn.sigmoid(-lam_v)
        sm_ref[...] = _rows_to_tile(dcw + [acc["bias"], acc["ba"], acc["bx"], dlam])
        dwa_ref[...] = dwa
        dwx_ref[...] = dwx

    big = lambda halo: pltpu.VMEM((T + halo, LANES), F32)
    mat = pl.BlockSpec((None, LANES, LANES), lambda c: (c, 0, 0))
    return _carry_call(
        body, name=name, steps=nblk,
        in_specs=_mixer_b_specs(T, nblk) + [_seg_spec(T, 1, nblk), _ANY],
        out_specs=[pl.BlockSpec((3, T, LANES), lambda c: (1, 0, c)),
                   pl.BlockSpec((None, SUBLANES, LANES), lambda c: (c, 0, 0)), mat, mat],
        out_shape=[jax.ShapeDtypeStruct(dproj.shape, dproj.dtype), jax.ShapeDtypeStruct((nblk, SUBLANES, LANES), F32),
                   jax.ShapeDtypeStruct((nblk, LANES, LANES), F32), jax.ShapeDtypeStruct((nblk, LANES, LANES), F32)],
        scratch_shapes=[big(HALO), big(0), big(0), big(0), big(HALO), big(HALO), big(0), big(0), big(HALO)],
        args=(proj, proj, conv_b, bias, wa, ba, wx, bx, lam, dy, dproj), jobs=jobs, aliases={10: 0})


ATT_BLOCK = 128
ATT_GROUP = 3
ATT_TILE = ATT_BLOCK * ATT_GROUP
ATT_UNDERFLOW = -110.0
ATT_UNVISITED = -1e30


def _split_dot(x, m):
    hi = x.astype(BF16)
    lo = (x - hi.astype(F32)).astype(BF16)
    return jnp.dot(hi, m, preferred_element_type=F32) + jnp.dot(lo, m, preferred_element_type=F32)


def _sub(x, j):
    return x[:, j * ATT_BLOCK:(j + 1) * ATT_BLOCK]


def _stack_rows(x):
    return jnp.concatenate([_sub(x, j) for j in range(ATT_GROUP)], axis=0)


def _unstack_rows(x, offsets):
    return jnp.concatenate([x[j * ATT_BLOCK:(j + 1) * ATT_BLOCK, :] + offsets[j] for j in range(ATT_GROUP)], axis=1)


def _att_tile(q, k_ref, q0, qb, it, scale):
    hi = (qb + 1 - ATT_GROUP * it) * ATT_BLOCK
    k0 = pl.multiple_of(jnp.maximum(hi - ATT_TILE, 0), ATT_BLOCK)
    kt = k_ref[pl.ds(k0, ATT_TILE), :]
    z = lax.dot_general(q, kt, DIMS_NT, preferred_element_type=F32) * scale
    key = k0 + lax.broadcasted_iota(jnp.int32, z.shape, 1)
    row = q0 + lax.broadcasted_iota(jnp.int32, z.shape, 0)
    mask = (key < row) & (key < hi)
    n = jnp.where(mask, -(jnp.maximum(z, 0.0) + jnp.log(1.0 + jnp.exp(-jnp.abs(z)))), 0.0)
    return k0, kt, z, mask, n


def _suffix_in_tile(n, upper, run):
    rs = [jnp.sum(_sub(n, j), axis=1, keepdims=True) for j in range(ATT_GROUP)]
    offs = [None] * ATT_GROUP
    offs[ATT_GROUP - 1] = run
    for j in range(ATT_GROUP - 2, -1, -1):
        offs[j] = offs[j + 1] + rs[j + 1]
    return _unstack_rows(_split_dot(_stack_rows(n), upper), offs), offs[0] + rs[0]


def _head_spec(T, seg, heads):
    return pl.BlockSpec((None, T, ATT_HEAD_DIM), lambda h: (seg, 0, h))


def _attention_fwd(name, qkv, jobs=()):
    _, T, D = qkv.shape
    heads = D // ATT_HEAD_DIM
    nq = T // ATT_BLOCK
    assert nq <= LANES and T >= ATT_TILE and nq % 2 == 0
    scale = 1.0 / math.sqrt(ATT_HEAD_DIM)

    def body(q_ref, k_ref, v_ref, o_ref, r_ref, acc_s, run_s):
        rr = lax.broadcasted_iota(jnp.int32, (ATT_BLOCK, ATT_BLOCK), 0)
        cc = lax.broadcasted_iota(jnp.int32, (ATT_BLOCK, ATT_BLOCK), 1)
        upper = jnp.where(rr > cc, 1.0, 0.0).astype(BF16)
        lane = lax.broadcasted_iota(jnp.int32, (ATT_BLOCK, LANES), 1)

        def tile(slot, qb, q0, q, it, first):
            k0, _, z, mask, n = _att_tile(q, k_ref, q0, qb, it, scale)
            run = jnp.zeros((ATT_BLOCK, LANES), F32) if first else run_s[slot]
            suffix, run_next = _suffix_in_tile(n, upper, run)
            w = jnp.where(mask, jnp.exp(z + n + suffix), 0.0)
            pv = jnp.dot(w.astype(BF16), v_ref[pl.ds(k0, ATT_TILE), :], preferred_element_type=F32)
            if first:
                acc_s[slot] = pv
            else:
                acc_s[slot] += pv
                r_ref[pl.ds(q0, ATT_BLOCK), :] = jnp.where(lane == it, run, r_ref[pl.ds(q0, ATT_BLOCK), :])
            run_s[slot] = run_next
            return jnp.max(run_next) >= ATT_UNDERFLOW

        def pair_loop(p, _):
            blocks = []
            for slot in range(2):
                qb = 2 * p + slot
                q0 = pl.multiple_of(qb * ATT_BLOCK, ATT_BLOCK)
                r_ref[pl.ds(q0, ATT_BLOCK), :] = jnp.where(lane == 0, 0.0, ATT_UNVISITED)
                blocks.append((qb, q0, q_ref[pl.ds(q0, ATT_BLOCK), :]))
            go = [tile(slot, *blocks[slot], 0, True) for slot in range(2)]
            for slot in range(2):
                qb, q0, q = blocks[slot]
                n_tiles = (qb + ATT_GROUP) // ATT_GROUP
                lax.while_loop(lambda c: (c[0] < n_tiles) & c[1],
                               lambda c: (c[0] + 1, tile(slot, qb, q0, q, c[0], False)), (jnp.int32(1), go[slot]))
                o_ref[pl.ds(q0, ATT_BLOCK), :] = acc_s[slot].astype(o_ref.dtype)
            return 0

        lax.fori_loop(0, nq // 2, pair_loop, 0)

    return _carry_call(
        body, name=name, steps=heads,
        in_specs=[_head_spec(T, 0, heads), _head_spec(T, 1, heads), _head_spec(T, 2, heads)],
        out_specs=[pl.BlockSpec((T, ATT_HEAD_DIM), lambda h: (0, h)), pl.BlockSpec((None, T, LANES), lambda h: (h, 0, 0))],
        out_shape=[jax.ShapeDtypeStruct((T, D), BF16), jax.ShapeDtypeStruct((heads, T, LANES), F32)],
        scratch_shapes=[pltpu.VMEM((2, ATT_BLOCK, ATT_HEAD_DIM), F32), pltpu.VMEM((2, ATT_BLOCK, LANES), F32)],
        args=(qkv, qkv, qkv), jobs=jobs)


def _attention_bwd(name, qkv, do, rmat, jobs=()):
    _, T, D = qkv.shape
    heads = D // ATT_HEAD_DIM
    nq = T // ATT_BLOCK
    scale = 1.0 / math.sqrt(ATT_HEAD_DIM)

    def body(q_ref, k_ref, v_ref, do_ref, r_ref, dqkv_ref, dk_s, dv_s, dq_s, left_s):
        rr = lax.broadcasted_iota(jnp.int32, (ATT_BLOCK, ATT_BLOCK), 0)
        cc = lax.broadcasted_iota(jnp.int32, (ATT_BLOCK, ATT_BLOCK), 1)
        upper = jnp.where(rr > cc, 1.0, 0.0).astype(BF16)
        lower = jnp.where(rr < cc, 1.0, 0.0).astype(BF16)
        lane = lax.broadcasted_iota(jnp.int32, (ATT_BLOCK, LANES), 1)
        dk_s[...] = jnp.zeros_like(dk_s)
        dv_s[...] = jnp.zeros_like(dv_s)

        def tile(slot, qb, q0, q, dov, it, first):
            k0, kt, z, mask, n = _att_tile(q, k_ref, q0, qb, it, scale)
            vt = v_ref[pl.ds(k0, ATT_TILE), :]
            run = jnp.sum(jnp.where(lane == it, r_ref[pl.ds(q0, ATT_BLOCK), :], 0.0), axis=1, keepdims=True)
            suffix, _ = _suffix_in_tile(n, upper, run)
            s = z + n
            w = jnp.where(mask, jnp.exp(s + suffix), 0.0)
            e = w * lax.dot_general(dov, vt, DIMS_NT, preferred_element_type=F32)
            es = [jnp.sum(_sub(e, g), axis=1, keepdims=True) for g in range(ATT_GROUP)]
            pre = [jnp.zeros((ATT_BLOCK, LANES), F32) if first else left_s[slot]]
            for g in range(ATT_GROUP):
                pre.append(pre[g] + es[g])
            before = _unstack_rows(_split_dot(_stack_rows(e), lower), pre)
            sig = jnp.exp(s)
            dz = (jnp.where(mask, e * (1.0 - sig) - before * sig, 0.0) * scale).astype(BF16)
            dq = jnp.dot(dz, kt, preferred_element_type=F32)
            if first:
                dq_s[slot] = dq
            else:
                dq_s[slot] += dq
            dk_s[pl.ds(k0, ATT_TILE), :] += lax.dot_general(dz, q, DIMS_TN, preferred_element_type=F32)
            dv_s[pl.ds(k0, ATT_TILE), :] += lax.dot_general(w.astype(BF16), dov, DIMS_TN, preferred_element_type=F32)
            left_s[slot] = pre[ATT_GROUP]

        def pair_loop(p, _):
            blocks = []
            for slot in range(2):
                qb = 2 * p + slot
                q0 = pl.multiple_of(qb * ATT_BLOCK, ATT_BLOCK)
                n_tiles = (qb + ATT_GROUP) // ATT_GROUP
                seen = ((jnp.max(r_ref[pl.ds(q0, ATT_BLOCK), :], axis=0, keepdims=True) > 0.5 * ATT_UNVISITED)
                        & (lane[0:1, :] < n_tiles))
                n_visited = jnp.maximum(jnp.sum(jnp.where(seen, 1.0, 0.0)).astype(jnp.int32), 1)
                blocks.append((qb, q0, q_ref[pl.ds(q0, ATT_BLOCK), :], do_ref[pl.ds(q0, ATT_BLOCK), :], n_visited))
            for slot in range(2):
                qb, q0, q, dov, n_visited = blocks[slot]
                tile(slot, qb, q0, q, dov, n_visited - 1, True)
            for slot in range(2):
                qb, q0, q, dov, n_visited = blocks[slot]
                lax.fori_loop(1, n_visited, lambda j, c: (tile(slot, qb, q0, q, dov, n_visited - 1 - j, False), c)[1], 0)
                dqkv_ref[0, pl.ds(q0, ATT_BLOCK), :] = dq_s[slot].astype(dqkv_ref.dtype)
            return 0

        lax.fori_loop(0, nq // 2, pair_loop, 0)
        dqkv_ref[1, :, :] = dk_s[...].astype(dqkv_ref.dtype)
        dqkv_ref[2, :, :] = dv_s[...].astype(dqkv_ref.dtype)

    return _carry_call(
        body, name=name, steps=heads,
        in_specs=[_head_spec(T, 0, heads), _head_spec(T, 1, heads), _head_spec(T, 2, heads),
                  pl.BlockSpec((T, ATT_HEAD_DIM), lambda h: (0, h)), pl.BlockSpec((None, T, LANES), lambda h: (h, 0, 0))],
        out_specs=[pl.BlockSpec((3, T, ATT_HEAD_DIM), lambda h: (0, 0, h))],
        out_shape=[jax.ShapeDtypeStruct((3, T, D), BF16)],
        scratch_shapes=[pltpu.VMEM((T, ATT_HEAD_DIM), F32), pltpu.VMEM((T, ATT_HEAD_DIM), F32),
                        pltpu.VMEM((2, ATT_BLOCK, ATT_HEAD_DIM), F32), pltpu.VMEM((2, ATT_BLOCK, LANES), F32)],
        args=(qkv, qkv, qkv, do, rmat), jobs=jobs)


def _block_diag_pairs(w):
    h = w.shape[0]
    wp = w.reshape(h // 2, 2, RG_HEAD_DIM, RG_HEAD_DIM)
    z = jnp.zeros_like(wp[:, 0])
    top = jnp.concatenate([wp[:, 0], z], axis=2)
    bot = jnp.concatenate([z, wp[:, 1]], axis=2)
    return jnp.concatenate([top, bot], axis=1)


def _diag_pairs(g):
    n = g.shape[0]
    a = g[:, :RG_HEAD_DIM, :RG_HEAD_DIM]
    b = g[:, RG_HEAD_DIM:, RG_HEAD_DIM:]
    return jnp.stack([a, b], axis=1).reshape(2 * n, RG_HEAD_DIM, RG_HEAD_DIM)


class _Weights:
    def __init__(self, full, shards=None, plan=None):
        self.full, self.shards, self.plan = dict(full), shards or {}, plan or {}
        self.partial, self.rows = {}, {}

    def __getitem__(self, name):
        return self.full[name]

    def jobs(self, call):
        return [_gather_job(self.shards[n], self.partial.get(n), lo, hi, parts)
                for n, lo, hi, parts in self.plan.get(call, ())]

    def deliver(self, call, outs):
        for (n, lo, hi, parts), g in zip(self.plan.get(call, ()), outs):
            self.partial[n] = g
            self.rows[n] = self.rows.get(n, 0) + hi - lo
            if self.rows[n] == parts:
                self.full[n] = _gathered_layout(n, g)


def _gathered_layout(name, g):
    if name in ("w_in", "w_qkv", "w_out", "w_o"):
        return g.reshape(g.shape[0] * g.shape[1], g.shape[2])
    return g


class _Grads:
    def __init__(self, lands=None, plan=None):
        self.lands, self.plan = dict(lands) if lands else None, plan or {}
        self.ready, self.sent = {}, {}

    def put(self, name, arr):
        self.ready[name] = arr

    def jobs(self, call):
        if self.lands is None:
            return []
        return [_exchange_job(self.ready[n], self.lands[n], lo, hi, parts) for n, lo, hi, parts in self.plan.get(call, ())]

    def deliver(self, call, outs):
        for (n, lo, hi, parts), o in zip(self.plan.get(call, ()), outs):
            assert self.sent.get(n, (0, parts)) == (lo, parts), (call, n)
            self.lands[n] = o
            self.sent[n] = (hi, parts)

    def flush(self, name):
        if self.lands is None:
            return
        rest = []
        for n in self.ready:
            lo, parts = self.sent.get(n, (0, 1))
            if lo < parts:
                rest.append((n, lo, parts, parts))
        if rest:
            outs = _run_jobs(name, [_exchange_job(self.ready[n], self.lands[n], lo, hi, parts) for n, lo, hi, parts in rest])
            for (n, _, hi, parts), o in zip(rest, outs):
                self.lands[n] = o
                self.sent[n] = (hi, parts)


def _mlp_fwd(tag, h, wts, run):
    T, D = h.shape
    w_up = wts["up" + tag]
    fb = w_up.shape[2]
    F = fb * N_DEV
    tm, tn, tk = _tile(T, 1024), _tile(fb, 1024), _tile(D, MM_TK)
    nb = fb // tn

    def up_epilogue(u):
        r = jnp.maximum(u, 0.0)
        return u, r * r

    o_spec = pl.BlockSpec((tm, tn), lambda i, j, k: (i, j))
    u, act = run(
        _matmul, f"mlp_up_l{tag}",
        [(h, pl.BlockSpec((tm, tk), lambda i, j, k: (i, k))),
         (w_up, pl.BlockSpec((None, tk, tn), lambda i, j, k: (j // nb, k, j % nb)))],
        [(jax.ShapeDtypeStruct((T, F), BF16), o_spec), (jax.ShapeDtypeStruct((T, F), BF16), o_spec)],
        (T // tm, F // tn, D // tk), DIMS_NN, (tm, tn), up_epilogue, n_main=2)
    w_down = wts["down" + tag].reshape(F, D)
    m = run(_mm_nn, f"mlp_down_l{tag}", act, w_down, F32)
    return u, act, m


def _mlp_bwd(tag, h, u, act, dm, wts, grads, run):
    T, D = h.shape
    w_up, w_down = wts["up" + tag], wts["down" + tag]
    fb = w_up.shape[2]
    F = fb * N_DEV
    grads.put("down" + tag, run(_mm_tn, f"mlp_down_dw_l{tag}", act, dm, BF16).reshape(N_DEV, fb, D))
    tm, tn, tk = _tile(T, 1024), _tile(fb, 1024), _tile(D, MM_TK)
    nb = fb // tn
    o_spec = pl.BlockSpec((tm, tn), lambda i, j, k: (i, j))
    du = run(
        _matmul, f"mlp_down_dx_l{tag}",
        [(dm, pl.BlockSpec((tm, tk), lambda i, j, k: (i, k))),
         (w_down, pl.BlockSpec((None, tn, tk), lambda i, j, k: (j // nb, j % nb, k))),
         (u, o_spec)],
        [(jax.ShapeDtypeStruct((T, F), BF16), o_spec)],
        (T // tm, F // tn, D // tk), DIMS_NT, (tm, tn),
        lambda r, uv: (r * (2.0 * jnp.maximum(uv.astype(F32), 0.0)),))
    grads.put("up" + tag, run(_mm_tn, f"mlp_up_dw_l{tag}", h, du, BF16, out_blocks=N_DEV))
    tn2 = _tile(D, 1024)
    pair = 2 if MM_TK >= 2 * fb else 1
    return run(
        _matmul, f"mlp_up_dx_l{tag}",
        [(du, pl.BlockSpec((tm, pair * fb), lambda i, j, k: (i, k))),
         (w_up, pl.BlockSpec((pair, tn2, fb), lambda i, j, k: (k, j, 0)))],
        [(jax.ShapeDtypeStruct((T, D), F32), pl.BlockSpec((tm, tn2), lambda i, j, k: (i, j)))],
        (T // tm, D // tn2, N_DEV // pair), DIMS_NT, (tm, tn2), None)


def _local_step(x, target, gains, conv_a, conv_b, conv_b_bias, rg_w_a, rg_b_a, rg_w_x, rg_b_x, rg_lambda, wts, grads):
    T, D = x.shape
    g = lambda l, i: gains[l, i][None, :]
    wa_p = _block_diag_pairs(rg_w_a).astype(BF16)
    wx_p = _block_diag_pairs(rg_w_x).astype(BF16)

    def run(fn, name, *args, n_main=1, **kw):
        jw, jg = wts.jobs(name), grads.jobs(name)
        res = fn(name, *args, jobs=jw + jg, **kw)
        main, jo = res[:n_main], res[n_main:]
        wts.deliver(name, jo[:len(jw)])
        grads.deliver(name, jo[len(jw):])
        return main[0] if n_main == 1 else main

    h0 = _norm_fwd("norm_in", x, g(0, 0))
    proj = run(_mm_nt, "w_in_fwd", h0, wts["w_in"], F32, out_seg=5)
    y = run(_mixer_b_fwd, "mixer_b_fwd", proj, conv_b, conv_b_bias, wa_p, rg_b_a, wx_p, rg_b_x, rg_lambda,
            _mixer_a_fwd(proj, conv_a))
    mix0 = run(_mm_nn, "w_out_fwd", y, wts["w_out"], F32, a_seg=2)
    x1, h1 = _resid_norm("resid_mix0", x, mix0, g(0, 1), g(0, 2))
    u0, act0, m0 = _mlp_fwd("0", h1, wts, run)
    x2, h2 = _resid_norm("resid_mlp0", x1, m0, g(0, 3), g(1, 0))
    qkv = run(_mm_nt, "w_qkv_fwd", h2, wts["w_qkv"], BF16, out_seg=3)
    o, rmat = run(_attention_fwd, "attention_fwd", qkv, n_main=2)
    mix1 = run(_mm_nn, "w_o_fwd", o, wts["w_o"], F32)
    x3, h3 = _resid_norm("resid_mix1", x2, mix1, g(1, 1), g(1, 2))
    u1, act1, m1 = _mlp_fwd("1", h3, wts, run)
    dx4, sq = _final_loss("loss", x3, m1, g(1, 3), target)

    dm1, dg13 = _norm_bwd("norm_bwd_m1", m1, g(1, 3), dx4, None, BF16)
    dh3 = _mlp_bwd("1", h3, u1, act1, dm1, wts, grads, run)
    dx3, dg12 = _norm_bwd("norm_bwd_x3", x3, g(1, 2), dh3, dx4, F32)
    dmix1, dg11 = _norm_bwd("norm_bwd_mix1", mix1, g(1, 1), dx3, None, BF16)
    grads.put("w_o", run(_mm_tn, "w_o_dw", o, dmix1, BF16).reshape(N_DEV, D // N_DEV, D))
    do = run(_mm_nt, "w_o_dx", dmix1, wts["w_o"], BF16)
    dqkv = run(_attention_bwd, "attention_bwd", qkv, do, rmat)
    grads.put("w_qkv", run(_mm_tn, "w_qkv_dw", dqkv, h2, BF16, a_seg=3).reshape(N_DEV, 3 * D // N_DEV, D))
    dh2 = run(_mm_nn, "w_qkv_dx", dqkv, wts["w_qkv"], F32, a_seg=3)
    dx2, dg10 = _norm_bwd("norm_bwd_x2", x2, g(1, 0), dh2, dx3, F32)
    dm0, dg03 = _norm_bwd("norm_bwd_m0", m0, g(0, 3), dx2, None, BF16)
    dh1 = _mlp_bwd("0", h1, u0, act0, dm0, wts, grads, run)
    dx1, dg02 = _norm_bwd("norm_bwd_x1", x1, g(0, 2), dh1, dx2, F32)
    dmix0, dg01 = _norm_bwd("norm_bwd_mix0", mix0, g(0, 1), dx1, None, BF16)
    grads.put("w_out", run(_mm_tn, "w_out_dw", y, dmix0, BF16, a_seg=2).reshape(N_DEV, D // N_DEV, D))
    dy = run(_mm_nt, "w_out_dx", dmix0, wts["w_out"], F32, out_seg=2)
    dproj_a, dconv_a = _mixer_a_bwd(proj, conv_a, dy)
    dproj, sm_b, dwa_p, dwx_p = run(_mixer_b_bwd, "mixer_b_bwd", proj, conv_b, conv_b_bias, wa_p, rg_b_a, wx_p, rg_b_x,
                                    rg_lambda, dy, dproj_a, n_main=4)
    grads.put("w_in", run(_mm_tn, "w_in_dw", dproj, h0, BF16, a_seg=5).reshape(N_DEV, 5 * D // (2 * N_DEV), D))
    dh0 = run(_mm_nn, "w_in_dx", dproj, wts["w_in"], F32, a_seg=5)
    dx0, dg00 = run(_norm_bwd, "norm_bwd_x0", x, g(0, 0), dh0, dx1, F32, n_main=2)

    C = D // 2
    lanes_to_vec = lambda t, row: t[:, row, :].reshape(1, C)
    small = {
        "norm_gains": jnp.concatenate([dg00, dg01, dg02, dg03, dg10, dg11, dg12, dg13], axis=0).reshape(2, 4, D),
        "conv_a": jnp.transpose(dconv_a[:, :3, :], (1, 0, 2)).reshape(3, C),
        "conv_b": jnp.transpose(sm_b[:, :4, :], (1, 0, 2)).reshape(4, C),
        "conv_b_bias": lanes_to_vec(sm_b, _ROW_BIAS),
        "rg_w_a": _diag_pairs(dwa_p),
        "rg_b_a": lanes_to_vec(sm_b, _ROW_BA),
        "rg_w_x": _diag_pairs(dwx_p),
        "rg_b_x": lanes_to_vec(sm_b, _ROW_BX),
        "rg_lambda": lanes_to_vec(sm_b, _ROW_LAM),
    }
    return sq[0, 0], dx0, small


def _my_index():
    return 4 * lax.axis_index("x") + 2 * lax.axis_index("y") + lax.axis_index("c")


def _peers():
    x, y, c = lax.axis_index("x"), lax.axis_index("y"), lax.axis_index("c")
    out = []
    for k in range(1, N_DEV):
        px = x ^ ((k >> 2) & 1)
        py = y ^ ((k >> 1) & 1)
        pc = c ^ (k & 1)
        out.append(((px, py, pc), 4 * px + 2 * py + pc))
    return out


GATHER_FIRST = ("w_in",)
GATHER_PLAN = {
    "w_in_fwd": (("w_out", 0, 1, 1), ("up0", 0, 1, 4)),
    "mixer_b_fwd": (("up0", 1, 3, 4),),
    "w_out_fwd": (("up0", 3, 4, 4),),
    "mlp_up_l0": (("down0", 0, 1, 1),),
    "mlp_down_l0": (("w_qkv", 0, 1, 1),),
    "w_qkv_fwd": (("w_o", 0, 1, 1), ("up1", 0, 2, 4)),
    "attention_fwd": (("up1", 2, 4, 4), ("down1", 0, 1, 1)),
}
EXCHANGE_PLAN = {
    "mlp_down_dx_l1": (("down1", 0, 3, 8),), "mlp_up_dw_l1": (("down1", 3, 6, 8),),
    "mlp_up_dx_l1": (("down1", 6, 8, 8), ("up1", 0, 1, 8)),
    "w_o_dw": (("up1", 1, 2, 8),), "w_o_dx": (("up1", 2, 3, 8),),
    "attention_bwd": (("up1", 3, 8, 8), ("w_o", 0, 1, 1)),
    "w_qkv_dx": (("w_qkv", 0, 1, 2),), "mlp_down_dw_l0": (("w_qkv", 1, 2, 2),),
    "mlp_down_dx_l0": (("down0", 0, 3, 8),), "mlp_up_dw_l0": (("down0", 3, 6, 8),),
    "mlp_up_dx_l0": (("down0", 6, 8, 8), ("up0", 0, 1, 8)),
    "w_out_dw": (("up0", 1, 2, 8),), "w_out_dx": (("up0", 2, 3, 8),),
    "mixer_b_bwd": (("up0", 3, 7, 8),),
    "w_in_dw": (("up0", 7, 8, 8), ("w_out", 0, 1, 2)),
    "w_in_dx": (("w_out", 1, 2, 2), ("w_in", 0, 1, 4)),
    "norm_bwd_x0": (("w_in", 1, 2, 4),),
    "adamw_mlp_w_down": (("w_in", 2, 3, 4),), "adamw_mlp_w_up": (("w_in", 3, 4, 4),),
}


def _all_gather(name, shards):
    n = len(shards)

    def body(*refs):
        srcs, dsts = refs[:n], refs[n:2 * n]
        send_sems, recv_sems, local_sems = refs[2 * n:]
        me = _my_index()
        peers = _peers()
        copies = []
        for a in range(n):
            lc = pltpu.make_async_copy(srcs[a], dsts[a].at[me], local_sems.at[a])
            lc.start()
            copies.append(lc)
        remote = []
        for a in range(n):
            for k, (pos, _) in enumerate(peers):
                cp = pltpu.make_async_remote_copy(
                    src_ref=srcs[a], dst_ref=dsts[a].at[me], send_sem=send_sems.at[a, k], recv_sem=recv_sems.at[a, k],
                    device_id=pos, device_id_type=MESH)
                cp.start()
                remote.append(cp)
        for a in range(n):
            for k, (pos, idx) in enumerate(peers):
                pltpu.make_async_remote_copy(
                    src_ref=srcs[a], dst_ref=dsts[a].at[idx], send_sem=send_sems.at[a, k], recv_sem=recv_sems.at[a, k],
                    device_id=pos, device_id_type=MESH).wait_recv()
        for cp in remote:
            cp.wait_send()
        for lc in copies:
            lc.wait()

    return pl.pallas_call(
        body, name=name,
        in_specs=[_ANY] * n, out_specs=[_ANY] * n,
        out_shape=[jax.ShapeDtypeStruct((N_DEV,) + s.shape, s.dtype) for s in shards],
        scratch_shapes=[pltpu.SemaphoreType.DMA((n, N_DEV - 1)), pltpu.SemaphoreType.DMA((n, N_DEV - 1)),
                        pltpu.SemaphoreType.DMA((n,))],
    )(*shards)


def _job_sems():
    return [pltpu.SemaphoreType.DMA((N_DEV - 1,)), pltpu.SemaphoreType.DMA((N_DEV - 1,)), pltpu.SemaphoreType.DMA((1,))]


def _gather_job(shard, prev=None, lo=0, hi=1, parts=1):
    n = shard.shape[0] // parts
    assert n * parts == shard.shape[0]
    rows = pl.ds(lo * n, (hi - lo) * n)

    def ctx():
        x, y, c = lax.axis_index("x"), lax.axis_index("y"), lax.axis_index("c")
        chips = [(1 - x, y), (x, 1 - y), (1 - x, 1 - y)]
        return x, y, c, chips

    def idx(px, py, pc):
        return 4 * px + 2 * py + pc

    def copy(src, out, sems, k, block, to):
        return pltpu.make_async_remote_copy(
            src_ref=out.at[block, rows] if src is None else src.at[rows], dst_ref=out.at[block, rows],
            send_sem=sems[0].at[k], recv_sem=sems[1].at[k], device_id=to, device_id_type=MESH)

    def start(ins, outs, sems):
        x, y, c, chips = ctx()
        src, out = ins[0], outs[0]
        me = idx(x, y, c)
        pltpu.make_async_copy(src.at[rows], out.at[me, rows], sems[2].at[0]).start()
        copy(src, out, sems, 0, me, (x, y, 1 - c)).start()
        for j, (px, py) in enumerate(chips):
            copy(src, out, sems, 1 + j, me, (px, py, c)).start()

    def mid(ins, outs, sems):
        x, y, c, chips = ctx()
        out = outs[0]
        for j, (px, py) in enumerate(chips):
            copy(None, out, sems, 1 + j, idx(px, py, c), (x, y, c)).wait_recv()
            copy(None, out, sems, 4 + j, idx(px, py, c), (x, y, 1 - c)).start()

    def end(ins, outs, sems):
        x, y, c, chips = ctx()
        src, out = ins[0], outs[0]
        me = (x, y, c)
        copy(None, out, sems, 0, idx(x, y, 1 - c), me).wait_recv()
        for j, (px, py) in enumerate(chips):
            copy(None, out, sems, 4 + j, idx(px, py, 1 - c), me).wait_recv()
        for k in range(N_DEV - 1):
            copy(src, out, sems, k, idx(x, y, c), me).wait_send()
        pltpu.make_async_copy(src.at[rows], out.at[idx(x, y, c), rows], sems[2].at[0]).wait()

    out_shape = jax.ShapeDtypeStruct((N_DEV,) + shard.shape, shard.dtype)
    if prev is None:
        return _Job([shard], [out_shape], _job_sems(), start, mid, end)
    return _Job([shard, prev], [out_shape], _job_sems(), start, mid, end, alias={1: 0})


def _exchange_job(src, land, lo=0, hi=1, parts=1):
    n = src.shape[1] // parts
    assert n * parts == src.shape[1]

    def sl(ref, s):
        return ref.at[s, pl.ds(lo * n, (hi - lo) * n)]

    def start(ins, outs, sems):
        me = _my_index()
        pltpu.make_async_copy(sl(ins[0], me), sl(outs[0], me), sems[2].at[0]).start()
        for k, (pos, idx) in enumerate(_peers()):
            pltpu.make_async_remote_copy(
                src_ref=sl(ins[0], idx), dst_ref=sl(outs[0], me), send_sem=sems[0].at[k], recv_sem=sems[1].at[k],
                device_id=pos, device_id_type=MESH).start()

    def mid(ins, outs, sems):
        pass

    def end(ins, outs, sems):
        me = _my_index()
        for k, (pos, idx) in enumerate(_peers()):
            cp = pltpu.make_async_remote_copy(
                src_ref=sl(ins[0], idx), dst_ref=sl(outs[0], idx), send_sem=sems[0].at[k], recv_sem=sems[1].at[k],
                device_id=pos, device_id_type=MESH)
            cp.wait_recv()
            cp.wait_send()
        pltpu.make_async_copy(sl(ins[0], me), sl(outs[0], me), sems[2].at[0]).wait()

    return _Job([src, land], [jax.ShapeDtypeStruct(land.shape, land.dtype)], _job_sems(), start, mid, end, alias={1: 0})


def _adamw_math(w, g, m, v):
    m = ADAM_B1 * m + (1.0 - ADAM_B1) * g
    v = ADAM_B2 * v + (1.0 - ADAM_B2) * (g * g)
    m_hat = m / (1.0 - ADAM_B1 ** ADAM_STEP)
    v_hat = v / (1.0 - ADAM_B2 ** ADAM_STEP)
    delta = -ADAM_LR * (m_hat / (jnp.sqrt(v_hat) + ADAM_EPS) + ADAM_WD * w)
    return delta, m, v


def _sum_slots(ref):
    g = ref[0].astype(F32)
    for s in range(1, N_DEV):
        g = g + ref[s].astype(F32)
    return g


def _adamw_big(name, lands, w, m, v, jobs=(), transposed=False):
    L, R, C = w.shape
    assert len(lands) == L
    tr = _tile(R, max(LANES, (256 * 1024) // C))
    nr = R // tr

    def body(*refs):
        l_refs = refs[:L]
        w_ref, m_ref, v_ref, g_ref, d_ref, nm_ref, nv_ref = refs[L:]
        for li in range(L):
            @pl.when(pl.program_id(0) // nr == li)
            def _(li=li):
                g = _sum_slots(l_refs[li])
                if transposed:
                    g = g.T
                d, nm, nv = _adamw_math(w_ref[...], g, m_ref[...], v_ref[...])
                g_ref[...] = g
                d_ref[...] = d
                nm_ref[...] = nm
                nv_ref[...] = nv

    def land_spec(li):
        if transposed:
            return pl.BlockSpec((N_DEV, C, tr), lambda s: (0, 0, jnp.where(s // nr == li, s % nr, 0)))
        return pl.BlockSpec((N_DEV, tr, C), lambda s: (0, jnp.where(s // nr == li, s % nr, 0), 0))

    row = pl.BlockSpec((None, tr, C), lambda s: (s // nr, s % nr, 0))
    return _carry_call(
        body, name=name, steps=L * nr, in_specs=[land_spec(li) for li in range(L)] + [row, row, row],
        out_specs=[row] * 4, out_shape=[jax.ShapeDtypeStruct((L, R, C), F32)] * 4, scratch_shapes=[],
        args=(*lands, w, m, v), jobs=jobs)


def _sum8(name, slots):
    _, R, C = slots.shape

    def body(s_ref, o_ref):
        o_ref[...] = _sum_slots(s_ref)

    return pl.pallas_call(body, name=name, out_shape=jax.ShapeDtypeStruct((R, C), F32))(slots)


def _adamw_small(name, g, w, m, v):
    def body(g_ref, w_ref, m_ref, v_ref, d_ref, nm_ref, nv_ref):
        d, nm, nv = _adamw_math(w_ref[...], g_ref[...], m_ref[...], v_ref[...])
        d_ref[...] = d
        nm_ref[...] = nm
        nv_ref[...] = nv

    return pl.pallas_call(body, name=name, out_shape=[jax.ShapeDtypeStruct(w.shape, F32)] * 3)(g, w, m, v)


def _pack_rows(arrs):
    parts, spans, r0 = [], [], 0
    for a in arrs:
        flat = a.astype(F32).reshape(-1)
        rows = -(-flat.shape[0] // LANES)
        rows = -(-rows // SUBLANES) * SUBLANES
        flat = jnp.pad(flat, (0, rows * LANES - flat.shape[0]))
        parts.append(flat.reshape(rows, LANES))
        spans.append((r0, rows, a.shape))
        r0 += rows
    return jnp.concatenate(parts, axis=0), spans


def _unpack_rows(buf, span):
    r0, rows, shape = span
    n = math.prod(shape)
    return buf[..., r0:r0 + rows, :].reshape(buf.shape[:-2] + (rows * LANES,))[..., :n].reshape(buf.shape[:-2] + shape)


def _col_blocks(w, n_blocks):
    K, N = w.shape
    return jnp.transpose(w.reshape(K, n_blocks, N // n_blocks), (1, 0, 2))


def _from_col_blocks(wb):
    B, K, n = wb.shape
    return jnp.transpose(wb, (1, 0, 2)).reshape(K, B * n)


def kernel(x, norm_gains, hyb_w_in, hyb_conv_a, hyb_conv_b, hyb_conv_b_bias, hyb_rg_w_a, hyb_rg_b_a, hyb_rg_w_x, hyb_rg_b_x, hyb_rg_lambda, hyb_w_out, sb_w_qkv, sb_w_o, mlp_w_up, mlp_w_down, loss_target, m_norm_gains, m_hyb_w_in, m_hyb_conv_a, m_hyb_conv_b, m_hyb_conv_b_bias, m_hyb_rg_w_a, m_hyb_rg_b_a, m_hyb_rg_w_x, m_hyb_rg_b_x, m_hyb_rg_lambda, m_hyb_w_out, m_sb_w_qkv, m_sb_w_o, m_mlp_w_up, m_mlp_w_down, v_norm_gains, v_hyb_w_in, v_hyb_conv_a, v_hyb_conv_b, v_hyb_conv_b_bias, v_hyb_rg_w_a, v_hyb_rg_b_a, v_hyb_rg_w_x, v_hyb_rg_b_x, v_hyb_rg_lambda, v_hyb_w_out, v_sb_w_qkv, v_sb_w_o, v_mlp_w_up, v_mlp_w_down):
    T, D = x.shape[1], x.shape[2]
    me = _my_index()

    small_shards, small_spans = _pack_rows([norm_gains, hyb_conv_a[0], hyb_conv_b[0]])
    (small_all,) = _all_gather("gather_small", [small_shards])
    gains_b = _unpack_rows(small_all, small_spans[0])
    gains = jnp.transpose(gains_b, (1, 2, 0, 3)).reshape(2, 4, D)
    conv_a = _from_col_blocks(_unpack_rows(small_all, small_spans[1]))
    conv_b = _from_col_blocks(_unpack_rows(small_all, small_spans[2]))

    shards = {"w_in": hyb_w_in[0].T, "w_out": hyb_w_out[0], "w_qkv": sb_w_qkv[0].T, "w_o": sb_w_o[0],
              "up0": mlp_w_up[0], "up1": mlp_w_up[1], "down0": mlp_w_down[0], "down1": mlp_w_down[1]}
    shards = {n: s.astype(BF16) for n, s in shards.items()}
    first = _run_jobs("gather_first", [_gather_job(shards[n]) for n in GATHER_FIRST])
    wts = _Weights({n: _gathered_layout(n, g) for n, g in zip(GATHER_FIRST, first)}, shards, GATHER_PLAN)
    assert not set(GATHER_FIRST) & {e[0] for es in GATHER_PLAN.values() for e in es}
    grads_big = _Grads({n: lax.empty((N_DEV,) + s.shape, BF16) for n, s in shards.items()}, EXCHANGE_PLAN)

    sq, grad_x, small = _local_step(
        x[0], loss_target[0], gains, conv_a, conv_b, hyb_conv_b_bias, hyb_rg_w_a[0], hyb_rg_b_a, hyb_rg_w_x[0],
        hyb_rg_b_x, hyb_rg_lambda, wts, grads_big)
    loss = lax.psum(0.5 * sq / D, ("x", "y", "c"))


    names = ["norm_gains", "hyb_w_in", "hyb_conv_a", "hyb_conv_b", "hyb_conv_b_bias", "hyb_rg_w_a", "hyb_rg_b_a",
             "hyb_rg_w_x", "hyb_rg_b_x", "hyb_rg_lambda", "hyb_w_out", "sb_w_qkv", "sb_w_o", "mlp_w_up", "mlp_w_down"]
    params = dict(zip(names, [norm_gains, hyb_w_in, hyb_conv_a, hyb_conv_b, hyb_conv_b_bias, hyb_rg_w_a, hyb_rg_b_a,
                              hyb_rg_w_x, hyb_rg_b_x, hyb_rg_lambda, hyb_w_out, sb_w_qkv, sb_w_o, mlp_w_up, mlp_w_down]))
    moms = dict(zip(names, [m_norm_gains, m_hyb_w_in, m_hyb_conv_a, m_hyb_conv_b, m_hyb_conv_b_bias, m_hyb_rg_w_a,
                            m_hyb_rg_b_a, m_hyb_rg_w_x, m_hyb_rg_b_x, m_hyb_rg_lambda, m_hyb_w_out, m_sb_w_qkv,
                            m_sb_w_o, m_mlp_w_up, m_mlp_w_down]))
    vars_ = dict(zip(names, [v_norm_gains, v_hyb_w_in, v_hyb_conv_a, v_hyb_conv_b, v_hyb_conv_b_bias, v_hyb_rg_w_a,
                             v_hyb_rg_b_a, v_hyb_rg_w_x, v_hyb_rg_b_x, v_hyb_rg_lambda, v_hyb_w_out, v_sb_w_qkv,
                             v_sb_w_o, v_mlp_w_up, v_mlp_w_down]))
    grads, deltas, new_m, new_v = {}, {}, {}, {}

    big_lands = {"mlp_w_down": ["down0", "down1"], "mlp_w_up": ["up0", "up1"], "sb_w_qkv": ["w_qkv"], "sb_w_o": ["w_o"],
                 "hyb_w_out": ["w_out"], "hyb_w_in": ["w_in"]}
    for nm, keys in big_lands.items():
        call = f"adamw_{nm}"
        if nm == "hyb_w_in":
            grads_big.flush("exchange_grads")
        jobs = grads_big.jobs(call)
        assert not {k for k in keys} & {e[0] for e in EXCHANGE_PLAN.get(call, ())}
        outs = _adamw_big(call, [grads_big.lands[k] for k in keys], params[nm], moms[nm], vars_[nm], jobs=jobs,
                          transposed=nm in ("hyb_w_in", "sb_w_qkv"))
        grads[nm], deltas[nm], new_m[nm], new_v[nm] = outs[:4]
        grads_big.deliver(call, outs[4:])

    small_names = ["norm_gains", "hyb_conv_a", "hyb_conv_b", "hyb_conv_b_bias", "hyb_rg_w_a", "hyb_rg_b_a",
                   "hyb_rg_w_x", "hyb_rg_b_x", "hyb_rg_lambda"]
    small_keys = ["norm_gains", "conv_a", "conv_b", "conv_b_bias", "rg_w_a", "rg_b_a", "rg_w_x", "rg_b_x", "rg_lambda"]
    sg_buf, sg_spans = _pack_rows([small[k] for k in small_keys])
    (sg_all,) = _all_gather("gather_small_grads", [sg_buf])
    sg_sum = _sum8("sum_small_grads", sg_all)
    full = {nm: _unpack_rows(sg_sum, sp) for nm, sp in zip(small_names, sg_spans)}
    cb = (D // 2) // N_DEV
    small_grads = {
        "norm_gains": lax.dynamic_slice_in_dim(full["norm_gains"], me * (D // N_DEV), D // N_DEV, axis=2),
        "hyb_conv_a": lax.dynamic_slice_in_dim(full["hyb_conv_a"], me * cb, cb, axis=1)[None],
        "hyb_conv_b": lax.dynamic_slice_in_dim(full["hyb_conv_b"], me * cb, cb, axis=1)[None],
        "hyb_conv_b_bias": full["hyb_conv_b_bias"],
        "hyb_rg_w_a": full["hyb_rg_w_a"][None],
        "hyb_rg_b_a": full["hyb_rg_b_a"],
        "hyb_rg_w_x": full["hyb_rg_w_x"][None],
        "hyb_rg_b_x": full["hyb_rg_b_x"],
        "hyb_rg_lambda": full["hyb_rg_lambda"],
    }
    pk = lambda d: _pack_rows([d[nm] for nm in small_names])
    g_buf, spans = pk(small_grads)
    w_buf, _ = pk(params)
    m_buf, _ = pk(moms)
    v_buf, _ = pk(vars_)
    d_buf, nm_buf, nv_buf = _adamw_small("adamw_small", g_buf, w_buf, m_buf, v_buf)
    for nm, sp in zip(small_names, spans):
        grads[nm] = small_grads[nm]
        deltas[nm], new_m[nm], new_v[nm] = _unpack_rows(d_buf, sp), _unpack_rows(nm_buf, sp), _unpack_rows(nv_buf, sp)

    return (loss, grad_x[None], *[grads[n] for n in names], *[deltas[n] for n in names],
            *[new_m[n] for n in names], *[new_v[n] for n in names])
```

```python
import functools
import math

import jax
import jax.numpy as jnp
from jax import lax
from jax.experimental import pallas as pl
from jax.experimental.pallas import tpu as pltpu

F32 = jnp.float32
BF16 = jnp.bfloat16

NORM_EPS = 1e-6
LRU_C = 8.0
ATT_HEAD_DIM = 128
RG_HEAD_DIM = 64
LANES = 128
SUBLANES = 8
N_DEV = 8
ADAM_LR = 0.001
ADAM_B1 = 0.9
ADAM_B2 = 0.999
ADAM_EPS = 1e-08
ADAM_WD = 0.01
ADAM_STEP = 10
VMEM_LIMIT = 56 * 1024 * 1024
MM_TK = 2048
MESH = pl.DeviceIdType.MESH


def _tile(n, pref):
    if n <= pref:
        return n
    t = (pref // LANES) * LANES
    while t > LANES and n % t:
        t -= LANES
    assert n % t == 0, (n, pref)
    return t


def _params(sem):
    return pltpu.CompilerParams(dimension_semantics=sem, vmem_limit_bytes=VMEM_LIMIT)


DIMS_NN = (((1,), (0,)), ((), ()))
DIMS_NT = (((1,), (1,)), ((), ()))
DIMS_TN = (((0,), (0,)), ((), ()))


_ANY = pl.BlockSpec(memory_space=pl.ANY)


class _Job:
    def __init__(self, ins, outs, sems, start, mid, end, alias=None):
        self.ins, self.outs, self.sems = ins, outs, sems
        self.start, self.mid, self.end = start, mid, end
        self.alias = alias or {}


def _job_plumbing(jobs, n_in, n_out):
    j_ins = [a for jb in jobs for a in jb.ins]
    j_outs = [o for jb in jobs for o in jb.outs]
    j_sems = [s for jb in jobs for s in jb.sems]
    aliases, pi, po = {}, 0, 0
    for jb in jobs:
        for i_in, i_out in jb.alias.items():
            aliases[n_in + pi + i_in] = n_out + po + i_out
        pi += len(jb.ins)
        po += len(jb.outs)
    return j_ins, j_outs, j_sems, aliases


def _job_phase(jobs, which, jin, jout, jsem):
    pi = po = ps = 0
    for jb in jobs:
        getattr(jb, which)(jin[pi:pi + len(jb.ins)], jout[po:po + len(jb.outs)], jsem[ps:ps + len(jb.sems)])
        pi, po, ps = pi + len(jb.ins), po + len(jb.outs), ps + len(jb.sems)


def _run_jobs(name, jobs):
    j_ins, j_outs, j_sems, aliases = _job_plumbing(jobs, 0, 0)
    n_ji, n_jo = len(j_ins), len(j_outs)

    def body(*refs):
        jin, jout, jsem = refs[:n_ji], refs[n_ji:n_ji + n_jo], refs[n_ji + n_jo:]
        for which in ("start", "mid", "end"):
            _job_phase(jobs, which, jin, jout, jsem)

    return pl.pallas_call(body, name=name, in_specs=[_ANY] * n_ji, out_specs=[_ANY] * n_jo, out_shape=j_outs,
                          scratch_shapes=j_sems, input_output_aliases=aliases)(*j_ins)


def _carry_call(body, *, name, steps, in_specs, out_specs, out_shape, scratch_shapes, args, jobs=(), aliases=None):
    n_in, n_out, n_sc = len(in_specs), len(out_shape), len(scratch_shapes)
    j_ins, j_outs, j_sems, j_aliases = _job_plumbing(jobs, n_in, n_out)
    n_ji, n_jo = len(j_ins), len(j_outs)

    def wrapped(*refs):
        ins, jin = refs[:n_in], refs[n_in:n_in + n_ji]
        o0 = n_in + n_ji
        outs, jout = refs[o0:o0 + n_out], refs[o0 + n_out:o0 + n_out + n_jo]
        s0 = o0 + n_out + n_jo
        scratch, jsem = refs[s0:s0 + n_sc], refs[s0 + n_sc:]
        step = pl.program_id(0)
        if jobs:
            pl.when(step == 0)(lambda: _job_phase(jobs, "start", jin, jout, jsem))
            pl.when(step == (4 * steps) // 5)(lambda: _job_phase(jobs, "mid", jin, jout, jsem))
        body(*ins, *outs, *scratch)
        if jobs:
            pl.when(step == steps - 1)(lambda: _job_phase(jobs, "end", jin, jout, jsem))

    return pl.pallas_call(
        wrapped, name=name, grid=(steps,),
        in_specs=list(in_specs) + [_ANY] * n_ji, out_specs=list(out_specs) + [_ANY] * n_jo,
        out_shape=list(out_shape) + j_outs, scratch_shapes=list(scratch_shapes) + j_sems,
        input_output_aliases={**(aliases or {}), **j_aliases},
        compiler_params=_params(("arbitrary",) if jobs else ("parallel",)))(*args, *j_ins)


def _matmul(name, ins, outs, grid, dims, acc_shape, epilogue=None, jobs=()):
    n_in, n_out, nk = len(ins), len(outs), grid[2]
    j_ins, j_outs, j_sems, aliases = _job_plumbing(jobs, n_in, n_out)
    n_ji, n_jo = len(j_ins), len(j_outs)
    total = grid[0] * grid[1] * grid[2]
    n_acc = 0 if nk == 1 else 1

    def body(*refs):
        a_ref, b_ref = refs[0], refs[1]
        extras = refs[2:n_in]
        jin = refs[n_in:n_in + n_ji]
        out_refs = refs[n_in + n_ji:n_in + n_ji + n_out]
        jout = refs[n_in + n_ji + n_out:n_in + n_ji + n_out + n_jo]
        jsem = refs[n_in + n_ji + n_out + n_jo + n_acc:]
        k = pl.program_id(2)
        step = (pl.program_id(0) * grid[1] + pl.program_id(1)) * grid[2] + k
        if jobs:
            pl.when(step == 0)(lambda: _job_phase(jobs, "start", jin, jout, jsem))
            pl.when(step == (4 * total) // 5)(lambda: _job_phase(jobs, "mid", jin, jout, jsem))

        def finish(r):
            res = epilogue(r, *[e[...] for e in extras]) if epilogue is not None else (r,)
            for o, v in zip(out_refs, res):
                o[...] = v.astype(o.dtype)

        if len(b_ref.shape) == 2:
            prod = lax.dot_general(a_ref[...], b_ref[...], dims, preferred_element_type=F32)
        else:
            kb = a_ref.shape[1] // b_ref.shape[0]
            prod = sum(lax.dot_general(a_ref[:, g * kb:(g + 1) * kb], b_ref[g], dims, preferred_element_type=F32)
                       for g in range(b_ref.shape[0]))
        if nk == 1:
            finish(prod)
        else:
            acc = refs[n_in + n_ji + n_out + n_jo]

            @pl.when(k == 0)
            def _():
                acc[...] = prod

            @pl.when((k > 0) & (k < nk - 1))
            def _():
                acc[...] += prod

            @pl.when(k == nk - 1)
            def _():
                finish(acc[...] + prod)

        if jobs:
            pl.when(step == total - 1)(lambda: _job_phase(jobs, "end", jin, jout, jsem))

    sem = ("arbitrary",) * 3 if jobs else ("parallel", "parallel", "arbitrary")
    res = pl.pallas_call(
        body, name=name, grid=grid,
        in_specs=[s for _, s in ins] + [_ANY] * n_ji,
        out_specs=[s for _, s in outs] + [_ANY] * n_jo,
        out_shape=[s for s, _ in outs] + j_outs,
        scratch_shapes=[pltpu.VMEM(acc_shape, F32)] * n_acc + j_sems,
        input_output_aliases=aliases,
        compiler_params=_params(sem),
    )(*[a for a, _ in ins], *j_ins)
    return res


def _mm_nn(name, a, b, out_dtype, *, a_seg=None, out_seg=None, tm=1024, tn=1024, tk=MM_TK, epilogue=None,
           extras=(), n_out=1, out_dtypes=None, jobs=()):
    if a_seg:
        _, M, ks = a.shape
        K = ks * a_seg
    else:
        M, K = a.shape
        ks = K
    N = b.shape[1]
    ns = N // out_seg if out_seg else N
    tm, tn, tk = _tile(M, tm), _tile(ns, tn), _tile(ks, tk)
    nks, nns = ks // tk, ns // tn
    grid = (M // tm, N // tn, K // tk)
    if a_seg:
        a_spec = pl.BlockSpec((None, tm, tk), lambda i, j, k: (k // nks, i, k % nks))
    else:
        a_spec = pl.BlockSpec((tm, tk), lambda i, j, k: (i, k))
    b_spec = pl.BlockSpec((tk, tn), lambda i, j, k: (k, j))
    if out_seg:
        o_spec = pl.BlockSpec((None, tm, tn), lambda i, j, k: (j // nns, i, j % nns))
        o_shape = (out_seg, M, ns)
    else:
        o_spec = pl.BlockSpec((tm, tn), lambda i, j, k: (i, j))
        o_shape = (M, N)
    dts = out_dtypes or [out_dtype] * n_out
    outs = [(jax.ShapeDtypeStruct(o_shape, dt), o_spec) for dt in dts]
    ins = [(a, a_spec), (b, b_spec)] + [(e, o_spec) for e in extras]
    return _matmul(name, ins, outs, grid, DIMS_NN, (tm, tn), epilogue, jobs)


def _mm_nt(name, a, b, out_dtype, *, a_seg=None, out_seg=None, tm=1024, tn=1024, tk=MM_TK, epilogue=None, extras=(),
           jobs=()):
    if a_seg:
        _, M, ks = a.shape
        K = ks * a_seg
    else:
        M, K = a.shape
        ks = K
    N = b.shape[0]
    ns = N // out_seg if out_seg else N
    tm, tn, tk = _tile(M, tm), _tile(ns, tn), _tile(ks, tk)
    nks, nns = ks // tk, ns // tn
    grid = (M // tm, N // tn, K // tk)
    if a_seg:
        a_spec = pl.BlockSpec((None, tm, tk), lambda i, j, k: (k // nks, i, k % nks))
    else:
        a_spec = pl.BlockSpec((tm, tk), lambda i, j, k: (i, k))
    b_spec = pl.BlockSpec((tn, tk), lambda i, j, k: (j, k))
    if out_seg:
        o_spec = pl.BlockSpec((None, tm, tn), lambda i, j, k: (j // nns, i, j % nns))
        o_shape = (out_seg, M, ns)
    else:
        o_spec = pl.BlockSpec((tm, tn), lambda i, j, k: (i, j))
        o_shape = (M, N)
    outs = [(jax.ShapeDtypeStruct(o_shape, out_dtype), o_spec)]
    ins = [(a, a_spec), (b, b_spec)] + [(e, o_spec) for e in extras]
    return _matmul(name, ins, outs, grid, DIMS_NT, (tm, tn), epilogue, jobs)


def _mm_tn(name, a, b, out_dtype, *, a_seg=None, b_seg=None, out_blocks=None, tm=1024, tn=1024, tk=MM_TK, jobs=()):
    if a_seg:
        _, T, ms = a.shape
        M = ms * a_seg
    else:
        T, M = a.shape
        ms = M
    if b_seg:
        _, _, ns = b.shape
        N = ns * b_seg
    else:
        N = b.shape[1]
        ns = N
    nb_cols = N // out_blocks if out_blocks else N
    tm, tk = _tile(ms, tm), _tile(T, tk)
    tn = _tile(math.gcd(ns, nb_cols), tn)
    nms, nns, nbs = ms // tm, ns // tn, nb_cols // tn
    grid = (M // tm, N // tn, T // tk)
    if a_seg:
        a_spec = pl.BlockSpec((None, tk, tm), lambda i, j, k: (i // nms, k, i % nms))
    else:
        a_spec = pl.BlockSpec((tk, tm), lambda i, j, k: (k, i))
    if b_seg:
        b_spec = pl.BlockSpec((None, tk, tn), lambda i, j, k: (j // nns, k, j % nns))
    else:
        b_spec = pl.BlockSpec((tk, tn), lambda i, j, k: (k, j))
    if out_blocks:
        o_spec = pl.BlockSpec((None, tm, tn), lambda i, j, k: (j // nbs, i, j % nbs))
        o_shape = (out_blocks, M, nb_cols)
    else:
        o_spec = pl.BlockSpec((tm, tn), lambda i, j, k: (i, j))
        o_shape = (M, N)
    outs = [(jax.ShapeDtypeStruct(o_shape, out_dtype), o_spec)]
    return _matmul(name, [(a, a_spec), (b, b_spec)], outs, grid, DIMS_TN, (tm, tn), None, jobs)


def _rms(x):
    return lax.rsqrt(jnp.mean(x * x, axis=-1, keepdims=True) + NORM_EPS)


def _row_tile(T):
    return _tile(T, 256)


def _norm_fwd(name, x, g, jobs=()):
    T, D = x.shape
    tr = _row_tile(T)

    def body(x_ref, g_ref, h_ref):
        xv = x_ref[...]
        h_ref[...] = (xv * _rms(xv) * g_ref[...]).astype(h_ref.dtype)

    row = pl.BlockSpec((tr, D), lambda i: (i, 0))
    vec = pl.BlockSpec((1, D), lambda i: (0, 0))
    return _carry_call(body, name=name, steps=T // tr, in_specs=[row, vec], out_specs=[row],
                       out_shape=[jax.ShapeDtypeStruct((T, D), BF16)], scratch_shapes=[], args=(x, g), jobs=jobs)


def _resid_norm(name, x, br, g_post, g_next, jobs=()):
    T, D = x.shape
    tr = _row_tile(T)

    def body(x_ref, br_ref, gp_ref, gn_ref, xn_ref, h_ref):
        b = br_ref[...]
        xn = x_ref[...] + b * _rms(b) * gp_ref[...]
        xn_ref[...] = xn
        h_ref[...] = (xn * _rms(xn) * gn_ref[...]).astype(h_ref.dtype)

    row = pl.BlockSpec((tr, D), lambda i: (i, 0))
    vec = pl.BlockSpec((1, D), lambda i: (0, 0))
    return _carry_call(body, name=name, steps=T // tr, in_specs=[row, row, vec, vec], out_specs=[row, row],
                       out_shape=[jax.ShapeDtypeStruct((T, D), F32), jax.ShapeDtypeStruct((T, D), BF16)],
                       scratch_shapes=[], args=(x, br, g_post, g_next), jobs=jobs)


def _final_loss(name, x, br, g_post, target):
    T, D = x.shape
    tr = _row_tile(T)

    def body(x_ref, br_ref, gp_ref, t_ref, dy_ref, ls_ref):
        b = br_ref[...]
        err = x_ref[...] + b * _rms(b) * gp_ref[...] - t_ref[...]
        dy_ref[...] = err * (1.0 / D)

        @pl.when(pl.program_id(0) == 0)
        def _():
            ls_ref[...] = jnp.zeros_like(ls_ref)

        ls_ref[...] += jnp.sum(err * err)

    row = pl.BlockSpec((tr, D), lambda i: (i, 0))
    vec = pl.BlockSpec((1, D), lambda i: (0, 0))
    acc = pl.BlockSpec((SUBLANES, LANES), lambda i: (0, 0))
    return pl.pallas_call(body, name=name, grid=(T // tr,), in_specs=[row, row, vec, row], out_specs=[row, acc],
                          out_shape=[jax.ShapeDtypeStruct((T, D), F32), jax.ShapeDtypeStruct((SUBLANES, LANES), F32)],
                          compiler_params=_params(("arbitrary",)))(x, br, g_post, target)


def _norm_bwd(name, x, g, dy, add, out_dtype, jobs=()):
    T, D = x.shape
    tr = _row_tile(T)
    has_add = add is not None

    def body(*refs):
        if has_add:
            x_ref, g_ref, dy_ref, add_ref, dx_ref, dg_ref = refs
        else:
            x_ref, g_ref, dy_ref, dx_ref, dg_ref = refs
        xv = x_ref[...]
        r = _rms(xv)
        xhat = xv * r
        dyv = dy_ref[...].astype(F32)
        gdy = dyv * g_ref[...]
        dx = r * (gdy - xhat * jnp.mean(gdy * xhat, axis=-1, keepdims=True))
        if has_add:
            dx = dx + add_ref[...]
        dx_ref[...] = dx.astype(dx_ref.dtype)

        @pl.when(pl.program_id(0) == 0)
        def _():
            dg_ref[...] = jnp.zeros_like(dg_ref)

        dg_ref[...] += jnp.sum(dyv * xhat, axis=0, keepdims=True)

    row = pl.BlockSpec((tr, D), lambda i: (i, 0))
    vec = pl.BlockSpec((1, D), lambda i: (0, 0))
    ins = [x, g, dy] + ([add] if has_add else [])
    specs = [row, vec, row] + ([row] if has_add else [])
    if not jobs:
        return pl.pallas_call(body, name=name, grid=(T // tr,), in_specs=specs, out_specs=[row, vec],
                              out_shape=[jax.ShapeDtypeStruct((T, D), out_dtype), jax.ShapeDtypeStruct((1, D), F32)],
                              compiler_params=_params(("arbitrary",)))(*ins)
    return _carry_call(body, name=name, steps=T // tr, in_specs=specs, out_specs=[row, vec],
                       out_shape=[jax.ShapeDtypeStruct((T, D), out_dtype), jax.ShapeDtypeStruct((1, D), F32)],
                       scratch_shapes=[], args=ins, jobs=jobs)


HALO = SUBLANES
TIME_CHUNK = 512


def _chunks(T):
    tc = min(TIME_CHUNK, T)
    assert T % tc == 0 and tc % SUBLANES == 0
    return [(t0, tc) for t0 in range(0, T, tc)]


def _log_sigmoid(x):
    return -(jnp.maximum(-x, 0.0) + jnp.log(1.0 + jnp.exp(-jnp.abs(x))))


def _one_minus_exp(x):
    series = -x * (1.0 + x * (0.5 + x * (1.0 / 6.0 + x * (1.0 / 24.0))))
    return jnp.where(x > -0.01, series, 1.0 - jnp.exp(x))


_GELU_C = math.sqrt(2.0 / math.pi)


def _gelu(x):
    return 0.5 * x * (1.0 + jnp.tanh(_GELU_C * (x + 0.044715 * x * x * x)))


def _gelu_grad(x):
    th = jnp.tanh(_GELU_C * (x + 0.044715 * x * x * x))
    return 0.5 * (1.0 + th) + 0.5 * x * (1.0 - th * th) * _GELU_C * (1.0 + 3.0 * 0.044715 * x * x)


def _tile_scan(a, b, reverse):
    rows = a.shape[0]
    pos = lax.broadcasted_iota(jnp.int32, a.shape, 0) & (SUBLANES - 1)
    for d in (1, 2, 4):
        if reverse:
            ok = pos < SUBLANES - d
            shift = rows - d
        else:
            ok = pos >= d
            shift = d
        a_sh = jnp.where(ok, pltpu.roll(a, shift, 0), 1.0)
        b_sh = jnp.where(ok, pltpu.roll(b, shift, 0), 0.0)
        b = a * b_sh + b
        a = a * a_sh
    return a, b


def _carry_scan(a_s, b_s, T, reverse):
    n = T // SUBLANES
    edge = 0 if reverse else SUBLANES - 1

    def step(j, carry):
        g = (n - 1 - j) if reverse else j
        r = pl.multiple_of(g * SUBLANES, SUBLANES)
        h = b_s[pl.ds(r, SUBLANES), :] + a_s[pl.ds(r, SUBLANES), :] * carry
        b_s[pl.ds(r, SUBLANES), :] = h
        return jnp.broadcast_to(h[edge:edge + 1, :], h.shape)

    lax.fori_loop(0, n, step, jnp.zeros((SUBLANES, a_s.shape[1]), F32))


def _seg_spec(T, seg, nblk):
    return pl.BlockSpec((None, T, LANES), lambda c: (seg, 0, c))


def _rows_to_tile(rows):
    idx = lax.broadcasted_iota(jnp.int32, (SUBLANES, LANES), 0)
    out = jnp.zeros((SUBLANES, LANES), F32)
    for k, r in enumerate(rows):
        out = jnp.where(idx == k, r, out)
    return out


def _mixer_a_fwd(proj, conv_a):
    _, T, C = proj.shape
    nblk = C // LANES
    chunks = _chunks(T)

    def body(bg_ref, cg_ref, ax_ref, w_ref, y_ref, p_s):
        p_s[pl.ds(0, HALO), :] = jnp.zeros((HALO, LANES), F32)
        for t0, tc in chunks:
            p_s[pl.ds(HALO + t0, tc), :] = cg_ref[pl.ds(t0, tc), :] * ax_ref[pl.ds(t0, tc), :]
        w = w_ref[...]
        for t0, tc in chunks:
            c = (w[2:3, :] * p_s[pl.ds(HALO + t0, tc), :] + w[1:2, :] * p_s[pl.ds(HALO + t0 - 1, tc), :]
                 + w[0:1, :] * p_s[pl.ds(HALO + t0 - 2, tc), :])
            y_ref[pl.ds(t0, tc), :] = (bg_ref[pl.ds(t0, tc), :] * c).astype(y_ref.dtype)

    return pl.pallas_call(
        body, name="mixer_a_fwd", grid=(nblk,),
        in_specs=[_seg_spec(T, 0, nblk), _seg_spec(T, 1, nblk), _seg_spec(T, 2, nblk),
                  pl.BlockSpec((3, LANES), lambda c: (0, c))],
        out_specs=_seg_spec(T, 0, nblk),
        out_shape=jax.ShapeDtypeStruct((2, T, C), BF16),
        scratch_shapes=[pltpu.VMEM((T + HALO, LANES), F32)],
        compiler_params=_params(("parallel",)))(proj, proj, proj, conv_a)


def _mixer_a_bwd(proj, conv_a, dy):
    _, T, C = proj.shape
    nblk = C // LANES
    chunks = _chunks(T)

    def body(bg_ref, cg_ref, ax_ref, w_ref, dy_ref, dp_ref, dw_ref, p_s, dc_s):
        p_s[pl.ds(0, HALO), :] = jnp.zeros((HALO, LANES), F32)
        dc_s[pl.ds(T, HALO), :] = jnp.zeros((HALO, LANES), F32)
        for t0, tc in chunks:
            p_s[pl.ds(HALO + t0, tc), :] = cg_ref[pl.ds(t0, tc), :] * ax_ref[pl.ds(t0, tc), :]
        w = w_ref[...]
        for t0, tc in chunks:
            c = (w[2:3, :] * p_s[pl.ds(HALO + t0, tc), :] + w[1:2, :] * p_s[pl.ds(HALO + t0 - 1, tc), :]
                 + w[0:1, :] * p_s[pl.ds(HALO + t0 - 2, tc), :])
            dyv = dy_ref[pl.ds(t0, tc), :]
            dp_ref[0, pl.ds(t0, tc), :] = (dyv * c).astype(dp_ref.dtype)
            dc_s[pl.ds(t0, tc), :] = dyv * bg_ref[pl.ds(t0, tc), :]
        dw = [jnp.zeros((1, LANES), F32) for _ in range(3)]
        for t0, tc in chunks:
            dc = dc_s[pl.ds(t0, tc), :]
            dpv = w[2:3, :] * dc + w[1:2, :] * dc_s[pl.ds(t0 + 1, tc), :] + w[0:1, :] * dc_s[pl.ds(t0 + 2, tc), :]
            dp_ref[1, pl.ds(t0, tc), :] = (dpv * ax_ref[pl.ds(t0, tc), :]).astype(dp_ref.dtype)
            dp_ref[2, pl.ds(t0, tc), :] = (dpv * cg_ref[pl.ds(t0, tc), :]).astype(dp_ref.dtype)
            for k in range(3):
                dw[k] = dw[k] + jnp.sum(dc * p_s[pl.ds(HALO + t0 - (2 - k), tc), :], axis=0, keepdims=True)
        dw_ref[...] = _rows_to_tile(dw)

    return pl.pallas_call(
        body, name="mixer_a_bwd", grid=(nblk,),
        in_specs=[_seg_spec(T, 0, nblk), _seg_spec(T, 1, nblk), _seg_spec(T, 2, nblk),
                  pl.BlockSpec((3, LANES), lambda c: (0, c)), _seg_spec(T, 0, nblk)],
        out_specs=[pl.BlockSpec((3, T, LANES), lambda c: (0, 0, c)),
                   pl.BlockSpec((None, SUBLANES, LANES), lambda c: (c, 0, 0))],
        out_shape=[jax.ShapeDtypeStruct((6, T, C), BF16), jax.ShapeDtypeStruct((nblk, SUBLANES, LANES), F32)],
        scratch_shapes=[pltpu.VMEM((T + HALO, LANES), F32), pltpu.VMEM((T + HALO, LANES), F32)],
        compiler_params=_params(("parallel",)))(proj, proj, proj, conv_a, dy)


def _rg_gates(xr, wa, ba, wx, bx, ls):
    xb = xr.astype(BF16)
    r = jax.nn.sigmoid(jnp.dot(xb, wa, preferred_element_type=F32) + ba)
    i = jax.nn.sigmoid(jnp.dot(xb, wx, preferred_element_type=F32) + bx)
    log_a = LRU_C * r * ls
    a = jnp.exp(log_a)
    mult = jnp.sqrt(_one_minus_exp(2.0 * log_a))
    return r, i, a, mult


def _conv4(xh_s, cw, bias, t0, tc):
    return (cw[3:4, :] * xh_s[pl.ds(HALO + t0, tc), :] + cw[2:3, :] * xh_s[pl.ds(HALO + t0 - 1, tc), :]
            + cw[1:2, :] * xh_s[pl.ds(HALO + t0 - 2, tc), :] + cw[0:1, :] * xh_s[pl.ds(HALO + t0 - 3, tc), :] + bias)


def _mixer_b_specs(T, nblk):
    vec = pl.BlockSpec((1, LANES), lambda c: (0, c))
    mat = pl.BlockSpec((None, LANES, LANES), lambda c: (c, 0, 0))
    return [_seg_spec(T, 3, nblk), _seg_spec(T, 4, nblk), pl.BlockSpec((4, LANES), lambda c: (0, c)),
            vec, mat, vec, mat, vec, vec]


def _mixer_b_fwd(name, proj, conv_b, bias, wa, ba, wx, bx, lam, y, jobs=()):
    _, T, C = proj.shape
    nblk = C // LANES
    chunks = _chunks(T)

    def body(gate_ref, x_ref, cw_ref, cb_ref, wa_ref, ba_ref, wx_ref, bx_ref, lam_ref, y_in, y_ref, xh_s, a_s, b_s):
        xh_s[pl.ds(0, HALO), :] = jnp.zeros((HALO, LANES), F32)
        for t0, tc in chunks:
            xh_s[pl.ds(HALO + t0, tc), :] = x_ref[pl.ds(t0, tc), :]
        cw, bias_v = cw_ref[...], cb_ref[...]
        ls = _log_sigmoid(lam_ref[...])
        for t0, tc in chunks:
            xr = _conv4(xh_s, cw, bias_v, t0, tc)
            r, i, a, mult = _rg_gates(xr, wa_ref[...], ba_ref[...], wx_ref[...], bx_ref[...], ls)
            ac, hc = _tile_scan(a, mult * i * xr, reverse=False)
            a_s[pl.ds(t0, tc), :] = ac
            b_s[pl.ds(t0, tc), :] = hc
        _carry_scan(a_s, b_s, T, reverse=False)
        for t0, tc in chunks:
            y_ref[pl.ds(t0, tc), :] = (b_s[pl.ds(t0, tc), :] * _gelu(gate_ref[pl.ds(t0, tc), :])).astype(y_ref.dtype)

    return _carry_call(
        body, name=name, steps=nblk, in_specs=_mixer_b_specs(T, nblk) + [_ANY],
        out_specs=[_seg_spec(T, 1, nblk)], out_shape=[jax.ShapeDtypeStruct(y.shape, y.dtype)],
        scratch_shapes=[pltpu.VMEM((T + HALO, LANES), F32), pltpu.VMEM((T, LANES), F32), pltpu.VMEM((T, LANES), F32)],
        args=(proj, proj, conv_b, bias, wa, ba, wx, bx, lam, y), jobs=jobs, aliases={9: 0})


_ROW_CONV, _ROW_BIAS, _ROW_BA, _ROW_BX, _ROW_LAM = 0, 4, 5, 6, 7


def _mixer_b_bwd(name, proj, conv_b, bias, wa, ba, wx, bx, lam, dy, dproj, jobs=()):
    _, T, C = proj.shape
    nblk = C // LANES
    chunks = _chunks(T)

    def body(gate_ref, x_ref, cw_ref, cb_ref, wa_ref, ba_ref, wx_ref, bx_ref, lam_ref, dy_ref, dp_in,
             dp_ref, sm_ref, dwa_ref, dwx_ref, xh_s, xr_s, r_s, i_s, a_s, h_s, sa_s, sb_s, dx_s):
        zero_halo = jnp.zeros((HALO, LANES), F32)
        xh_s[pl.ds(0, HALO), :] = zero_halo
        h_s[pl.ds(0, HALO), :] = zero_halo
        a_s[pl.ds(T, HALO), :] = zero_halo
        dx_s[pl.ds(T, HALO), :] = zero_halo
        for t0, tc in chunks:
            xh_s[pl.ds(HALO + t0, tc), :] = x_ref[pl.ds(t0, tc), :]
        cw, bias_v = cw_ref[...], cb_ref[...]
        lam_v = lam_ref[...]
        ls = _log_sigmoid(lam_v)
        wa_v, wx_v, ba_v, bx_v = wa_ref[...], wx_ref[...], ba_ref[...], bx_ref[...]
        for t0, tc in chunks:
            xr = _conv4(xh_s, cw, bias_v, t0, tc)
            r, i, a, mult = _rg_gates(xr, wa_v, ba_v, wx_v, bx_v, ls)
            xr_s[pl.ds(t0, tc), :] = xr
            r_s[pl.ds(t0, tc), :] = r
            i_s[pl.ds(t0, tc), :] = i
            a_s[pl.ds(t0, tc), :] = a
            ac, hc = _tile_scan(a, mult * i * xr, reverse=False)
            sa_s[pl.ds(t0, tc), :] = ac
            sb_s[pl.ds(t0, tc), :] = hc
        _carry_scan(sa_s, sb_s, T, reverse=False)
        for t0, tc in chunks:
            h_s[pl.ds(HALO + t0, tc), :] = sb_s[pl.ds(t0, tc), :]
        for t0, tc in chunks:
            gv = gate_ref[pl.ds(t0, tc), :]
            dyv = dy_ref[pl.ds(t0, tc), :]
            dp_ref[0, pl.ds(t0, tc), :] = (dyv * h_s[pl.ds(HALO + t0, tc), :] * _gelu_grad(gv)).astype(dp_ref.dtype)
            ac, gc = _tile_scan(a_s[pl.ds(t0 + 1, tc), :], dyv * _gelu(gv), reverse=True)
            sa_s[pl.ds(t0, tc), :] = ac
            sb_s[pl.ds(t0, tc), :] = gc
        _carry_scan(sa_s, sb_s, T, reverse=True)
        acc = {k: jnp.zeros((1, LANES), F32) for k in ("bias", "ba", "bx", "lam")}
        dwa = jnp.zeros((LANES, LANES), F32)
        dwx = jnp.zeros((LANES, LANES), F32)
        for t0, tc in chunks:
            dht = sb_s[pl.ds(t0, tc), :]
            xr, r, i, a = xr_s[pl.ds(t0, tc), :], r_s[pl.ds(t0, tc), :], i_s[pl.ds(t0, tc), :], a_s[pl.ds(t0, tc), :]
            mult = jnp.sqrt(_one_minus_exp(2.0 * LRU_C * r * ls))
            da = dht * h_s[pl.ds(HALO + t0 - 1, tc), :]
            dmult = dht * i * xr
            di = dht * mult * xr
            dlog_a = da * a - dmult * a * a / mult
            dpa = dlog_a * (LRU_C * ls) * r * (1.0 - r)
            dpx = di * i * (1.0 - i)
            acc["lam"] = acc["lam"] + jnp.sum(dlog_a * r, axis=0, keepdims=True)
            acc["ba"] = acc["ba"] + jnp.sum(dpa, axis=0, keepdims=True)
            acc["bx"] = acc["bx"] + jnp.sum(dpx, axis=0, keepdims=True)
            xb, dpab, dpxb = xr.astype(BF16), dpa.astype(BF16), dpx.astype(BF16)
            dwa = dwa + lax.dot_general(xb, dpab, DIMS_TN, preferred_element_type=F32)
            dwx = dwx + lax.dot_general(xb, dpxb, DIMS_TN, preferred_element_type=F32)
            dxr = (dht * mult * i + lax.dot_general(dpab, wa_v, DIMS_NT, preferred_element_type=F32)
                   + lax.dot_general(dpxb, wx_v, DIMS_NT, preferred_element_type=F32))
            acc["bias"] = acc["bias"] + jnp.sum(dxr, axis=0, keepdims=True)
            dx_s[pl.ds(t0, tc), :] = dxr
        dcw = [jnp.zeros((1, LANES), F32) for _ in range(4)]
        for t0, tc in chunks:
            dxr = dx_s[pl.ds(t0, tc), :]
            dxin = (cw[3:4, :] * dxr + cw[2:3, :] * dx_s[pl.ds(t0 + 1, tc), :] + cw[1:2, :] * dx_s[pl.ds(t0 + 2, tc), :]
                    + cw[0:1, :] * dx_s[pl.ds(t0 + 3, tc), :])
            dp_ref[1, pl.ds(t0, tc), :] = dxin.astype(dp_ref.dtype)
            for k in range(4):
                dcw[k] = dcw[k] + jnp.sum(dxr * xh_s[pl.ds(HALO + t0 - (3 - k), tc), :], axis=0, keepdims=True)
        dlam = acc["lam"] * LRU_C * jax.nn.sigmoid(-lam_v)
        sm_ref[...] = _rows_to_tile(dcw + [acc["bias"], acc["ba"], acc["bx"], dlam])
        dwa_ref[...] = dwa
        dwx_ref[...] = dwx

    big = lambda halo: pltpu.VMEM((T + halo, LANES), F32)
    mat = pl.BlockSpec((None, LANES, LANES), lambda c: (c, 0, 0))
    return _carry_call(
        body, name=name, steps=nblk,
        in_specs=_mixer_b_specs(T, nblk) + [_seg_spec(T, 1, nblk), _ANY],
        out_specs=[pl.BlockSpec((3, T, LANES), lambda c: (1, 0, c)),
                   pl.BlockSpec((None, SUBLANES, LANES), lambda c: (c, 0, 0)), mat, mat],
        out_shape=[jax.ShapeDtypeStruct(dproj.shape, dproj.dtype), jax.ShapeDtypeStruct((nblk, SUBLANES, LANES), F32),
                   jax.ShapeDtypeStruct((nblk, LANES, LANES), F32), jax.ShapeDtypeStruct((nblk, LANES, LANES), F32)],
        scratch_shapes=[big(HALO), big(0), big(0), big(0), big(HALO), big(HALO), big(0), big(0), big(HALO)],
        args=(proj, proj, conv_b, bias, wa, ba, wx, bx, lam, dy, dproj), jobs=jobs, aliases={10: 0})


ATT_BLOCK = 128
ATT_GROUP = 3
ATT_TILE = ATT_BLOCK * ATT_GROUP
ATT_UNDERFLOW = -110.0
ATT_UNVISITED = -1e30


def _split_dot(x, m):
    hi = x.astype(BF16)
    lo = (x - hi.astype(F32)).astype(BF16)
    return jnp.dot(hi, m, preferred_element_type=F32) + jnp.dot(lo, m, preferred_element_type=F32)


def _sub(x, j):
    return x[:, j * ATT_BLOCK:(j + 1) * ATT_BLOCK]


def _stack_rows(x):
    return jnp.concatenate([_sub(x, j) for j in range(ATT_GROUP)], axis=0)


def _unstack_rows(x, offsets):
    return jnp.concatenate([x[j * ATT_BLOCK:(j + 1) * ATT_BLOCK, :] + offsets[j] for j in range(ATT_GROUP)], axis=1)


def _att_tile(q, k_ref, q0, qb, it, scale):
    hi = (qb + 1 - ATT_GROUP * it) * ATT_BLOCK
    k0 = pl.multiple_of(jnp.maximum(hi - ATT_TILE, 0), ATT_BLOCK)
    kt = k_ref[pl.ds(k0, ATT_TILE), :]
    z = lax.dot_general(q, kt, DIMS_NT, preferred_element_type=F32) * scale
    key = k0 + lax.broadcasted_iota(jnp.int32, z.shape, 1)
    row = q0 + lax.broadcasted_iota(jnp.int32, z.shape, 0)
    mask = (key < row) & (key < hi)
    n = jnp.where(mask, -(jnp.maximum(z, 0.0) + jnp.log(1.0 + jnp.exp(-jnp.abs(z)))), 0.0)
    return k0, kt, z, mask, n


def _suffix_in_tile(n, upper, run):
    rs = [jnp.sum(_sub(n, j), axis=1, keepdims=True) for j in range(ATT_GROUP)]
    offs = [None] * ATT_GROUP
    offs[ATT_GROUP - 1] = run
    for j in range(ATT_GROUP - 2, -1, -1):
        offs[j] = offs[j + 1] + rs[j + 1]
    return _unstack_rows(_split_dot(_stack_rows(n), upper), offs), offs[0] + rs[0]


def _head_spec(T, seg, heads):
    return pl.BlockSpec((None, T, ATT_HEAD_DIM), lambda h: (seg, 0, h))


def _attention_fwd(name, qkv, jobs=()):
    _, T, D = qkv.shape
    heads = D // ATT_HEAD_DIM
    nq = T // ATT_BLOCK
    assert nq <= LANES and T >= ATT_TILE and nq % 2 == 0
    scale = 1.0 / math.sqrt(ATT_HEAD_DIM)

    def body(q_ref, k_ref, v_ref, o_ref, r_ref, acc_s, run_s):
        rr = lax.broadcasted_iota(jnp.int32, (ATT_BLOCK, ATT_BLOCK), 0)
        cc = lax.broadcasted_iota(jnp.int32, (ATT_BLOCK, ATT_BLOCK), 1)
        upper = jnp.where(rr > cc, 1.0, 0.0).astype(BF16)
        lane = lax.broadcasted_iota(jnp.int32, (ATT_BLOCK, LANES), 1)

        def tile(slot, qb, q0, q, it, first):
            k0, _, z, mask, n = _att_tile(q, k_ref, q0, qb, it, scale)
            run = jnp.zeros((ATT_BLOCK, LANES), F32) if first else run_s[slot]
            suffix, run_next = _suffix_in_tile(n, upper, run)
            w = jnp.where(mask, jnp.exp(z + n + suffix), 0.0)
            pv = jnp.dot(w.astype(BF16), v_ref[pl.ds(k0, ATT_TILE), :], preferred_element_type=F32)
            if first:
                acc_s[slot] = pv
            else:
                acc_s[slot] += pv
                r_ref[pl.ds(q0, ATT_BLOCK), :] = jnp.where(lane == it, run, r_ref[pl.ds(q0, ATT_BLOCK), :])
            run_s[slot] = run_next
            return jnp.max(run_next) >= ATT_UNDERFLOW

        def pair_loop(p, _):
            blocks = []
            for slot in range(2):
                qb = 2 * p + slot
                q0 = pl.multiple_of(qb * ATT_BLOCK, ATT_BLOCK)
                r_ref[pl.ds(q0, ATT_BLOCK), :] = jnp.where(lane == 0, 0.0, ATT_UNVISITED)
                blocks.append((qb, q0, q_ref[pl.ds(q0, ATT_BLOCK), :]))
            go = [tile(slot, *blocks[slot], 0, True) for slot in range(2)]
            for slot in range(2):
                qb, q0, q = blocks[slot]
                n_tiles = (qb + ATT_GROUP) // ATT_GROUP
                lax.while_loop(lambda c: (c[0] < n_tiles) & c[1],
                               lambda c: (c[0] + 1, tile(slot, qb, q0, q, c[0], False)), (jnp.int32(1), go[slot]))
                o_ref[pl.ds(q0, ATT_BLOCK), :] = acc_s[slot].astype(o_ref.dtype)
            return 0

        lax.fori_loop(0, nq // 2, pair_loop, 0)

    return _carry_call(
        body, name=name, steps=heads,
        in_specs=[_head_spec(T, 0, heads), _head_spec(T, 1, heads), _head_spec(T, 2, heads)],
        out_specs=[pl.BlockSpec((T, ATT_HEAD_DIM), lambda h: (0, h)), pl.BlockSpec((None, T, LANES), lambda h: (h, 0, 0))],
        out_shape=[jax.ShapeDtypeStruct((T, D), BF16), jax.ShapeDtypeStruct((heads, T, LANES), F32)],
        scratch_shapes=[pltpu.VMEM((2, ATT_BLOCK, ATT_HEAD_DIM), F32), pltpu.VMEM((2, ATT_BLOCK, LANES), F32)],
        args=(qkv, qkv, qkv), jobs=jobs)


def _attention_bwd(name, qkv, do, rmat, jobs=()):
    _, T, D = qkv.shape
    heads = D // ATT_HEAD_DIM
    nq = T // ATT_BLOCK
    scale = 1.0 / math.sqrt(ATT_HEAD_DIM)

    def body(q_ref, k_ref, v_ref, do_ref, r_ref, dqkv_ref, dk_s, dv_s, dq_s, left_s):
        rr = lax.broadcasted_iota(jnp.int32, (ATT_BLOCK, ATT_BLOCK), 0)
        cc = lax.broadcasted_iota(jnp.int32, (ATT_BLOCK, ATT_BLOCK), 1)
        upper = jnp.where(rr > cc, 1.0, 0.0).astype(BF16)
        lower = jnp.where(rr < cc, 1.0, 0.0).astype(BF16)
        lane = lax.broadcasted_iota(jnp.int32, (ATT_BLOCK, LANES), 1)
        dk_s[...] = jnp.zeros_like(dk_s)
        dv_s[...] = jnp.zeros_like(dv_s)

        def tile(slot, qb, q0, q, dov, it, first):
            k0, kt, z, mask, n = _att_tile(q, k_ref, q0, qb, it, scale)
            vt = v_ref[pl.ds(k0, ATT_TILE), :]
            run = jnp.sum(jnp.where(lane == it, r_ref[pl.ds(q0, ATT_BLOCK), :], 0.0), axis=1, keepdims=True)
            suffix, _ = _suffix_in_tile(n, upper, run)
            s = z + n
            w = jnp.where(mask, jnp.exp(s + suffix), 0.0)
            e = w * lax.dot_general(dov, vt, DIMS_NT, preferred_element_type=F32)
            es = [jnp.sum(_sub(e, g), axis=1, keepdims=True) for g in range(ATT_GROUP)]
            pre = [jnp.zeros((ATT_BLOCK, LANES), F32) if first else left_s[slot]]
            for g in range(ATT_GROUP):
                pre.append(pre[g] + es[g])
            before = _unstack_rows(_split_dot(_stack_rows(e), lower), pre)
            sig = jnp.exp(s)
            dz = (jnp.where(mask, e * (1.0 - sig) - before * sig, 0.0) * scale).astype(BF16)
            dq = jnp.dot(dz, kt, preferred_element_type=F32)
            if first:
                dq_s[slot] = dq
            else:
                dq_s[slot] += dq
            dk_s[pl.ds(k0, ATT_TILE), :] += lax.dot_general(dz, q, DIMS_TN, preferred_element_type=F32)
            dv_s[pl.ds(k0, ATT_TILE), :] += lax.dot_general(w.astype(BF16), dov, DIMS_TN, preferred_element_type=F32)
            left_s[slot] = pre[ATT_GROUP]

        def pair_loop(p, _):
            blocks = []
            for slot in range(2):
                qb = 2 * p + slot
                q0 = pl.multiple_of(qb * ATT_BLOCK, ATT_BLOCK)
                n_tiles = (qb + ATT_GROUP) // ATT_GROUP
                seen = ((jnp.max(r_ref[pl.ds(q0, ATT_BLOCK), :], axis=0, keepdims=True) > 0.5 * ATT_UNVISITED)
                        & (lane[0:1, :] < n_tiles))
                n_visited = jnp.maximum(jnp.sum(jnp.where(seen, 1.0, 0.0)).astype(jnp.int32), 1)
                blocks.append((qb, q0, q_ref[pl.ds(q0, ATT_BLOCK), :], do_ref[pl.ds(q0, ATT_BLOCK), :], n_visited))
            for slot in range(2):
                qb, q0, q, dov, n_visited = blocks[slot]
                tile(slot, qb, q0, q, dov, n_visited - 1, True)
            for slot in range(2):
                qb, q0, q, dov, n_visited = blocks[slot]
                lax.fori_loop(1, n_visited, lambda j, c: (tile(slot, qb, q0, q, dov, n_visited - 1 - j, False), c)[1], 0)
                dqkv_ref[0, pl.ds(q0, ATT_BLOCK), :] = dq_s[slot].astype(dqkv_ref.dtype)
            return 0

        lax.fori_loop(0, nq // 2, pair_loop, 0)
        dqkv_ref[1, :, :] = dk_s[...].astype(dqkv_ref.dtype)
        dqkv_ref[2, :, :] = dv_s[...].astype(dqkv_ref.dtype)

    return _carry_call(
        body, name=name, steps=heads,
        in_specs=[_head_spec(T, 0, heads), _head_spec(T, 1, heads), _head_spec(T, 2, heads),
                  pl.BlockSpec((T, ATT_HEAD_DIM), lambda h: (0, h)), pl.BlockSpec((None, T, LANES), lambda h: (h, 0, 0))],
        out_specs=[pl.BlockSpec((3, T, ATT_HEAD_DIM), lambda h: (0, 0, h))],
        out_shape=[jax.ShapeDtypeStruct((3, T, D), BF16)],
        scratch_shapes=[pltpu.VMEM((T, ATT_HEAD_DIM), F32), pltpu.VMEM((T, ATT_HEAD_DIM), F32),
                        pltpu.VMEM((2, ATT_BLOCK, ATT_HEAD_DIM), F32), pltpu.VMEM((2, ATT_BLOCK, LANES), F32)],
        args=(qkv, qkv, qkv, do, rmat), jobs=jobs)


def _block_diag_pairs(w):
    h = w.shape[0]
    wp = w.reshape(h // 2, 2, RG_HEAD_DIM, RG_HEAD_DIM)
    z = jnp.zeros_like(wp[:, 0])
    top = jnp.concatenate([wp[:, 0], z], axis=2)
    bot = jnp.concatenate([z, wp[:, 1]], axis=2)
    return jnp.concatenate([top, bot], axis=1)


def _diag_pairs(g):
    n = g.shape[0]
    a = g[:, :RG_HEAD_DIM, :RG_HEAD_DIM]
    b = g[:, RG_HEAD_DIM:, RG_HEAD_DIM:]
    return jnp.stack([a, b], axis=1).reshape(2 * n, RG_HEAD_DIM, RG_HEAD_DIM)


class _Weights:
    def __init__(self, full, shards=None, plan=None):
        self.full, self.shards, self.plan = dict(full), shards or {}, plan or {}
        self.partial, self.rows = {}, {}

    def __getitem__(self, name):
        return self.full[name]

    def jobs(self, call):
        return [_gather_job(self.shards[n], self.partial.get(n), lo, hi, parts)
                for n, lo, hi, parts in self.plan.get(call, ())]

    def deliver(self, call, outs):
        for (n, lo, hi, parts), g in zip(self.plan.get(call, ()), outs):
            self.partial[n] = g
            self.rows[n] = self.rows.get(n, 0) + hi - lo
            if self.rows[n] == parts:
                self.full[n] = _gathered_layout(n, g)


def _gathered_layout(name, g):
    if name in ("w_in", "w_qkv", "w_out", "w_o"):
        return g.reshape(g.shape[0] * g.shape[1], g.shape[2])
    return g


class _Grads:
    def __init__(self, lands=None, plan=None):
        self.lands, self.plan = dict(lands) if lands else None, plan or {}
        self.ready, self.sent = {}, {}

    def put(self, name, arr):
        self.ready[name] = arr

    def jobs(self, call):
        if self.lands is None:
            return []
        return [_exchange_job(self.ready[n], self.lands[n], lo, hi, parts) for n, lo, hi, parts in self.plan.get(call, ())]

    def deliver(self, call, outs):
        for (n, lo, hi, parts), o in zip(self.plan.get(call, ()), outs):
            assert self.sent.get(n, (0, parts)) == (lo, parts), (call, n)
            self.lands[n] = o
            self.sent[n] = (hi, parts)

    def flush(self, name):
        if self.lands is None:
            return
        rest = []
        for n in self.ready:
            lo, parts = self.sent.get(n, (0, 1))
            if lo < parts:
                rest.append((n, lo, parts, parts))
        if rest:
            outs = _run_jobs(name, [_exchange_job(self.ready[n], self.lands[n], lo, hi, parts) for n, lo, hi, parts in rest])
            for (n, _, hi, parts), o in zip(rest, outs):
                self.lands[n] = o
                self.sent[n] = (hi, parts)


def _mlp_fwd(tag, h, wts, run):
    T, D = h.shape
    w_up = wts["up" + tag]
    fb = w_up.shape[2]
    F = fb * N_DEV
    tm, tn, tk = _tile(T, 1024), _tile(fb, 1024), _tile(D, MM_TK)
    nb = fb // tn

    def up_epilogue(u):
        r = jnp.maximum(u, 0.0)
        return u, r * r

    o_spec = pl.BlockSpec((tm, tn), lambda i, j, k: (i, j))
    u, act = run(
        _matmul, f"mlp_up_l{tag}",
        [(h, pl.BlockSpec((tm, tk), lambda i, j, k: (i, k))),
         (w_up, pl.BlockSpec((None, tk, tn), lambda i, j, k: (j // nb, k, j % nb)))],
        [(jax.ShapeDtypeStruct((T, F), BF16), o_spec), (jax.ShapeDtypeStruct((T, F), BF16), o_spec)],
        (T // tm, F // tn, D // tk), DIMS_NN, (tm, tn), up_epilogue, n_main=2)
    w_down = wts["down" + tag].reshape(F, D)
    m = run(_mm_nn, f"mlp_down_l{tag}", act, w_down, F32)
    return u, act, m


def _mlp_bwd(tag, h, u, act, dm, wts, grads, run):
    T, D = h.shape
    w_up, w_down = wts["up" + tag], wts["down" + tag]
    fb = w_up.shape[2]
    F = fb * N_DEV
    grads.put("down" + tag, run(_mm_tn, f"mlp_down_dw_l{tag}", act, dm, BF16).reshape(N_DEV, fb, D))
    tm, tn, tk = _tile(T, 1024), _tile(fb, 1024), _tile(D, MM_TK)
    nb = fb // tn
    o_spec = pl.BlockSpec((tm, tn), lambda i, j, k: (i, j))
    du = run(
        _matmul, f"mlp_down_dx_l{tag}",
        [(dm, pl.BlockSpec((tm, tk), lambda i, j, k: (i, k))),
         (w_down, pl.BlockSpec((None, tn, tk), lambda i, j, k: (j // nb, j % nb, k))),
         (u, o_spec)],
        [(jax.ShapeDtypeStruct((T, F), BF16), o_spec)],
        (T // tm, F // tn, D // tk), DIMS_NT, (tm, tn),
        lambda r, uv: (r * (2.0 * jnp.maximum(uv.astype(F32), 0.0)),))
    grads.put("up" + tag, run(_mm_tn, f"mlp_up_dw_l{tag}", h, du, BF16, out_blocks=N_DEV))
    tn2 = _tile(D, 1024)
    pair = 2 if MM_TK >= 2 * fb else 1
    return run(
        _matmul, f"mlp_up_dx_l{tag}",
        [(du, pl.BlockSpec((tm, pair * fb), lambda i, j, k: (i, k))),
         (w_up, pl.BlockSpec((pair, tn2, fb), lambda i, j, k: (k, j, 0)))],
        [(jax.ShapeDtypeStruct((T, D), F32), pl.BlockSpec((tm, tn2), lambda i, j, k: (i, j)))],
        (T // tm, D // tn2, N_DEV // pair), DIMS_NT, (tm, tn2), None)


def _local_step(x, target, gains, conv_a, conv_b, conv_b_bias, rg_w_a, rg_b_a, rg_w_x, rg_b_x, rg_lambda, wts, grads):
    T, D = x.shape
    g = lambda l, i: gains[l, i][None, :]
    wa_p = _block_diag_pairs(rg_w_a).astype(BF16)
    wx_p = _block_diag_pairs(rg_w_x).astype(BF16)

    def run(fn, name, *args, n_main=1, **kw):
        jw, jg = wts.jobs(name), grads.jobs(name)
        res = fn(name, *args, jobs=jw + jg, **kw)
        main, jo = res[:n_main], res[n_main:]
        wts.deliver(name, jo[:len(jw)])
        grads.deliver(name, jo[len(jw):])
        return main[0] if n_main == 1 else main

    h0 = run(_norm_fwd, "norm_in", x, g(0, 0))
    proj = run(_mm_nt, "w_in_fwd", h0, wts["w_in"], F32, out_seg=5)
    y = run(_mixer_b_fwd, "mixer_b_fwd", proj, conv_b, conv_b_bias, wa_p, rg_b_a, wx_p, rg_b_x, rg_lambda,
            _mixer_a_fwd(proj, conv_a))
    mix0 = run(_mm_nn, "w_out_fwd", y, wts["w_out"], F32, a_seg=2)
    x1, h1 = run(_resid_norm, "resid_mix0", x, mix0, g(0, 1), g(0, 2), n_main=2)
    u0, act0, m0 = _mlp_fwd("0", h1, wts, run)
    x2, h2 = run(_resid_norm, "resid_mlp0", x1, m0, g(0, 3), g(1, 0), n_main=2)
    qkv = run(_mm_nt, "w_qkv_fwd", h2, wts["w_qkv"], BF16, out_seg=3)
    o, rmat = run(_attention_fwd, "attention_fwd", qkv, n_main=2)
    mix1 = run(_mm_nn, "w_o_fwd", o, wts["w_o"], F32)
    x3, h3 = run(_resid_norm, "resid_mix1", x2, mix1, g(1, 1), g(1, 2), n_main=2)
    u1, act1, m1 = _mlp_fwd("1", h3, wts, run)
    dx4, sq = _final_loss("loss", x3, m1, g(1, 3), target)

    dm1, dg13 = _norm_bwd("norm_bwd_m1", m1, g(1, 3), dx4, None, BF16)
    dh3 = _mlp_bwd("1", h3, u1, act1, dm1, wts, grads, run)
    dx3, dg12 = _norm_bwd("norm_bwd_x3", x3, g(1, 2), dh3, dx4, F32)
    dmix1, dg11 = _norm_bwd("norm_bwd_mix1", mix1, g(1, 1), dx3, None, BF16)
    grads.put("w_o", run(_mm_tn, "w_o_dw", o, dmix1, BF16).reshape(N_DEV, D // N_DEV, D))
    do = run(_mm_nt, "w_o_dx", dmix1, wts["w_o"], BF16)
    dqkv = run(_attention_bwd, "attention_bwd", qkv, do, rmat)
    grads.put("w_qkv", run(_mm_tn, "w_qkv_dw", dqkv, h2, BF16, a_seg=3).reshape(N_DEV, 3 * D // N_DEV, D))
    dh2 = run(_mm_nn, "w_qkv_dx", dqkv, wts["w_qkv"], F32, a_seg=3)
    dx2, dg10 = _norm_bwd("norm_bwd_x2", x2, g(1, 0), dh2, dx3, F32)
    dm0, dg03 = _norm_bwd("norm_bwd_m0", m0, g(0, 3), dx2, None, BF16)
    dh1 = _mlp_bwd("0", h1, u0, act0, dm0, wts, grads, run)
    dx1, dg02 = _norm_bwd("norm_bwd_x1", x1, g(0, 2), dh1, dx2, F32)
    dmix0, dg01 = _norm_bwd("norm_bwd_mix0", mix0, g(0, 1), dx1, None, BF16)
    grads.put("w_out", run(_mm_tn, "w_out_dw", y, dmix0, BF16, a_seg=2).reshape(N_DEV, D // N_DEV, D))
    dy = run(_mm_nt, "w_out_dx", dmix0, wts["w_out"], F32, out_seg=2)
    dproj_a, dconv_a = _mixer_a_bwd(proj, conv_a, dy)
    dproj, sm_b, dwa_p, dwx_p = run(_mixer_b_bwd, "mixer_b_bwd", proj, conv_b, conv_b_bias, wa_p, rg_b_a, wx_p, rg_b_x,
                                    rg_lambda, dy, dproj_a, n_main=4)
    grads.put("w_in", run(_mm_tn, "w_in_dw", dproj, h0, BF16, a_seg=5).reshape(N_DEV, 5 * D // (2 * N_DEV), D))
    dh0 = run(_mm_nn, "w_in_dx", dproj, wts["w_in"], F32, a_seg=5)
    dx0, dg00 = run(_norm_bwd, "norm_bwd_x0", x, g(0, 0), dh0, dx1, F32, n_main=2)

    C = D // 2
    lanes_to_vec = lambda t, row: t[:, row, :].reshape(1, C)
    small = {
        "norm_gains": jnp.concatenate([dg00, dg01, dg02, dg03, dg10, dg11, dg12, dg13], axis=0).reshape(2, 4, D),
        "conv_a": jnp.transpose(dconv_a[:, :3, :], (1, 0, 2)).reshape(3, C),
        "conv_b": jnp.transpose(sm_b[:, :4, :], (1, 0, 2)).reshape(4, C),
        "conv_b_bias": lanes_to_vec(sm_b, _ROW_BIAS),
        "rg_w_a": _diag_pairs(dwa_p),
        "rg_b_a": lanes_to_vec(sm_b, _ROW_BA),
        "rg_w_x": _diag_pairs(dwx_p),
        "rg_b_x": lanes_to_vec(sm_b, _ROW_BX),
        "rg_lambda": lanes_to_vec(sm_b, _ROW_LAM),
    }
    return sq[0, 0], dx0, small


def _my_index():
    return 4 * lax.axis_index("x") + 2 * lax.axis_index("y") + lax.axis_index("c")


def _peers():
    x, y, c = lax.axis_index("x"), lax.axis_index("y"), lax.axis_index("c")
    out = []
    for k in range(1, N_DEV):
        px = x ^ ((k >> 2) & 1)
        py = y ^ ((k >> 1) & 1)
        pc = c ^ (k & 1)
        out.append(((px, py, pc), 4 * px + 2 * py + pc))
    return out


GATHER_FIRST = ("w_in",)
GATHER_PLAN = {
    "norm_in": (("w_out", 0, 1, 2),),
    "w_in_fwd": (("w_out", 1, 2, 2), ("up0", 0, 2, 4)),
    "mixer_b_fwd": (("up0", 2, 4, 4),),
    "w_out_fwd": (("down0", 0, 1, 4),),
    "resid_mix0": (("down0", 1, 2, 4),),
    "mlp_up_l0": (("down0", 2, 4, 4),),
    "mlp_down_l0": (("w_qkv", 0, 1, 1),),
    "resid_mlp0": (("up1", 0, 1, 4),),
    "w_qkv_fwd": (("w_o", 0, 1, 1), ("up1", 1, 2, 4)),
    "attention_fwd": (("up1", 2, 4, 4), ("down1", 0, 2, 4)),
    "mlp_up_l1": (("down1", 2, 4, 4),),
}
EXCHANGE_PLAN = {
    "mlp_down_dx_l1": (("down1", 0, 3, 8),), "mlp_up_dw_l1": (("down1", 3, 6, 8),),
    "mlp_up_dx_l1": (("down1", 6, 8, 8), ("up1", 0, 1, 8)),
    "w_o_dw": (("up1", 1, 2, 8),), "w_o_dx": (("up1", 2, 3, 8),),
    "attention_bwd": (("up1", 3, 8, 8), ("w_o", 0, 1, 1)),
    "w_qkv_dx": (("w_qkv", 0, 1, 2),), "mlp_down_dw_l0": (("w_qkv", 1, 2, 2),),
    "mlp_down_dx_l0": (("down0", 0, 3, 8),), "mlp_up_dw_l0": (("down0", 3, 6, 8),),
    "mlp_up_dx_l0": (("down0", 6, 8, 8), ("up0", 0, 1, 8)),
    "w_out_dw": (("up0", 1, 2, 8),), "w_out_dx": (("up0", 2, 3, 8),),
    "mixer_b_bwd": (("up0", 3, 7, 8),),
    "w_in_dw": (("up0", 7, 8, 8), ("w_out", 0, 1, 2)),
    "w_in_dx": (("w_out", 1, 2, 2), ("w_in", 0, 1, 4)),
    "norm_bwd_x0": (("w_in", 1, 2, 4),),
    "adamw_mlp_w_down": (("w_in", 2, 3, 4),), "adamw_mlp_w_up": (("w_in", 3, 4, 4),),
}


def _all_gather(name, shards):
    n = len(shards)

    def body(*refs):
        srcs, dsts = refs[:n], refs[n:2 * n]
        send_sems, recv_sems, local_sems = refs[2 * n:]
        me = _my_index()
        peers = _peers()
        copies = []
        for a in range(n):
            lc = pltpu.make_async_copy(srcs[a], dsts[a].at[me], local_sems.at[a])
            lc.start()
            copies.append(lc)
        remote = []
        for a in range(n):
            for k, (pos, _) in enumerate(peers):
                cp = pltpu.make_async_remote_copy(
                    src_ref=srcs[a], dst_ref=dsts[a].at[me], send_sem=send_sems.at[a, k], recv_sem=recv_sems.at[a, k],
                    device_id=pos, device_id_type=MESH)
                cp.start()
                remote.append(cp)
        for a in range(n):
            for k, (pos, idx) in enumerate(peers):
                pltpu.make_async_remote_copy(
                    src_ref=srcs[a], dst_ref=dsts[a].at[idx], send_sem=send_sems.at[a, k], recv_sem=recv_sems.at[a, k],
                    device_id=pos, device_id_type=MESH).wait_recv()
        for cp in remote:
            cp.wait_send()
        for lc in copies:
            lc.wait()

    return pl.pallas_call(
        body, name=name,
        in_specs=[_ANY] * n, out_specs=[_ANY] * n,
        out_shape=[jax.ShapeDtypeStruct((N_DEV,) + s.shape, s.dtype) for s in shards],
        scratch_shapes=[pltpu.SemaphoreType.DMA((n, N_DEV - 1)), pltpu.SemaphoreType.DMA((n, N_DEV - 1)),
                        pltpu.SemaphoreType.DMA((n,))],
    )(*shards)


def _job_sems():
    return [pltpu.SemaphoreType.DMA((N_DEV - 1,)), pltpu.SemaphoreType.DMA((N_DEV - 1,)), pltpu.SemaphoreType.DMA((1,))]


def _gather_job(shard, prev=None, lo=0, hi=1, parts=1):
    n = shard.shape[0] // parts
    assert n * parts == shard.shape[0]
    rows = pl.ds(lo * n, (hi - lo) * n)

    def ctx():
        x, y, c = lax.axis_index("x"), lax.axis_index("y"), lax.axis_index("c")
        chips = [(1 - x, y), (x, 1 - y), (1 - x, 1 - y)]
        return x, y, c, chips

    def idx(px, py, pc):
        return 4 * px + 2 * py + pc

    def copy(src, out, sems, k, block, to):
        return pltpu.make_async_remote_copy(
            src_ref=out.at[block, rows] if src is None else src.at[rows], dst_ref=out.at[block, rows],
            send_sem=sems[0].at[k], recv_sem=sems[1].at[k], device_id=to, device_id_type=MESH)

    def start(ins, outs, sems):
        x, y, c, chips = ctx()
        src, out = ins[0], outs[0]
        me = idx(x, y, c)
        pltpu.make_async_copy(src.at[rows], out.at[me, rows], sems[2].at[0]).start()
        copy(src, out, sems, 0, me, (x, y, 1 - c)).start()
        for j, (px, py) in enumerate(chips):
            copy(src, out, sems, 1 + j, me, (px, py, c)).start()

    def mid(ins, outs, sems):
        x, y, c, chips = ctx()
        out = outs[0]
        for j, (px, py) in enumerate(chips):
            copy(None, out, sems, 1 + j, idx(px, py, c), (x, y, c)).wait_recv()
            copy(None, out, sems, 4 + j, idx(px, py, c), (x, y, 1 - c)).start()

    def end(ins, outs, sems):
        x, y, c, chips = ctx()
        src, out = ins[0], outs[0]
        me = (x, y, c)
        copy(None, out, sems, 0, idx(x, y, 1 - c), me).wait_recv()
        for j, (px, py) in enumerate(chips):
            copy(None, out, sems, 4 + j, idx(px, py, 1 - c), me).wait_recv()
        for k in range(N_DEV - 1):
            copy(src, out, sems, k, idx(x, y, c), me).wait_send()
        pltpu.make_async_copy(src.at[rows], out.at[idx(x, y, c), rows], sems[2].at[0]).wait()

    out_shape = jax.ShapeDtypeStruct((N_DEV,) + shard.shape, shard.dtype)
    if prev is None:
        return _Job([shard], [out_shape], _job_sems(), start, mid, end)
    return _Job([shard, prev], [out_shape], _job_sems(), start, mid, end, alias={1: 0})


def _exchange_job(src, land, lo=0, hi=1, parts=1):
    n = src.shape[1] // parts
    assert n * parts == src.shape[1]

    def sl(ref, s):
        return ref.at[s, pl.ds(lo * n, (hi - lo) * n)]

    def start(ins, outs, sems):
        me = _my_index()
        pltpu.make_async_copy(sl(ins[0], me), sl(outs[0], me), sems[2].at[0]).start()
        for k, (pos, idx) in enumerate(_peers()):
            pltpu.make_async_remote_copy(
                src_ref=sl(ins[0], idx), dst_ref=sl(outs[0], me), send_sem=sems[0].at[k], recv_sem=sems[1].at[k],
                device_id=pos, device_id_type=MESH).start()

    def mid(ins, outs, sems):
        pass

    def end(ins, outs, sems):
        me = _my_index()
        for k, (pos, idx) in enumerate(_peers()):
            cp = pltpu.make_async_remote_copy(
                src_ref=sl(ins[0], idx), dst_ref=sl(outs[0], idx), send_sem=sems[0].at[k], recv_sem=sems[1].at[k],
                device_id=pos, device_id_type=MESH)
            cp.wait_recv()
            cp.wait_send()
        pltpu.make_async_copy(sl(ins[0], me), sl(outs[0], me), sems[2].at[0]).wait()

    return _Job([src, land], [jax.ShapeDtypeStruct(land.shape, land.dtype)], _job_sems(), start, mid, end, alias={1: 0})


def _adamw_math(w, g, m, v):
    m = ADAM_B1 * m + (1.0 - ADAM_B1) * g
    v = ADAM_B2 * v + (1.0 - ADAM_B2) * (g * g)
    m_hat = m / (1.0 - ADAM_B1 ** ADAM_STEP)
    v_hat = v / (1.0 - ADAM_B2 ** ADAM_STEP)
    delta = -ADAM_LR * (m_hat / (jnp.sqrt(v_hat) + ADAM_EPS) + ADAM_WD * w)
    return delta, m, v


def _sum_slots(ref):
    g = ref[0].astype(F32)
    for s in range(1, N_DEV):
        g = g + ref[s].astype(F32)
    return g


def _adamw_big(name, lands, w, m, v, jobs=(), transposed=False):
    L, R, C = w.shape
    assert len(lands) == L
    tr = _tile(R, max(LANES, (256 * 1024) // C))
    nr = R // tr

    def body(*refs):
        l_refs = refs[:L]
        w_ref, m_ref, v_ref, g_ref, d_ref, nm_ref, nv_ref = refs[L:]
        for li in range(L):
            @pl.when(pl.program_id(0) // nr == li)
            def _(li=li):
                g = _sum_slots(l_refs[li])
                if transposed:
                    g = g.T
                d, nm, nv = _adamw_math(w_ref[...], g, m_ref[...], v_ref[...])
                g_ref[...] = g
                d_ref[...] = d
                nm_ref[...] = nm
                nv_ref[...] = nv

    def land_spec(li):
        if transposed:
            return pl.BlockSpec((N_DEV, C, tr), lambda s: (0, 0, jnp.where(s // nr == li, s % nr, 0)))
        return pl.BlockSpec((N_DEV, tr, C), lambda s: (0, jnp.where(s // nr == li, s % nr, 0), 0))

    row = pl.BlockSpec((None, tr, C), lambda s: (s // nr, s % nr, 0))
    return _carry_call(
        body, name=name, steps=L * nr, in_specs=[land_spec(li) for li in range(L)] + [row, row, row],
        out_specs=[row] * 4, out_shape=[jax.ShapeDtypeStruct((L, R, C), F32)] * 4, scratch_shapes=[],
        args=(*lands, w, m, v), jobs=jobs)


def _sum8(name, slots):
    _, R, C = slots.shape

    def body(s_ref, o_ref):
        o_ref[...] = _sum_slots(s_ref)

    return pl.pallas_call(body, name=name, out_shape=jax.ShapeDtypeStruct((R, C), F32))(slots)


def _adamw_small(name, g, w, m, v):
    def body(g_ref, w_ref, m_ref, v_ref, d_ref, nm_ref, nv_ref):
        d, nm, nv = _adamw_math(w_ref[...], g_ref[...], m_ref[...], v_ref[...])
        d_ref[...] = d
        nm_ref[...] = nm
        nv_ref[...] = nv

    return pl.pallas_call(body, name=name, out_shape=[jax.ShapeDtypeStruct(w.shape, F32)] * 3)(g, w, m, v)


def _pack_rows(arrs):
    parts, spans, r0 = [], [], 0
    for a in arrs:
        flat = a.astype(F32).reshape(-1)
        rows = -(-flat.shape[0] // LANES)
        rows = -(-rows // SUBLANES) * SUBLANES
        flat = jnp.pad(flat, (0, rows * LANES - flat.shape[0]))
        parts.append(flat.reshape(rows, LANES))
        spans.append((r0, rows, a.shape))
        r0 += rows
    return jnp.concatenate(parts, axis=0), spans


def _unpack_rows(buf, span):
    r0, rows, shape = span
    n = math.prod(shape)
    return buf[..., r0:r0 + rows, :].reshape(buf.shape[:-2] + (rows * LANES,))[..., :n].reshape(buf.shape[:-2] + shape)


def _col_blocks(w, n_blocks):
    K, N = w.shape
    return jnp.transpose(w.reshape(K, n_blocks, N // n_blocks), (1, 0, 2))


def _from_col_blocks(wb):
    B, K, n = wb.shape
    return jnp.transpose(wb, (1, 0, 2)).reshape(K, B * n)


def kernel(x, norm_gains, hyb_w_in, hyb_conv_a, hyb_conv_b, hyb_conv_b_bias, hyb_rg_w_a, hyb_rg_b_a, hyb_rg_w_x, hyb_rg_b_x, hyb_rg_lambda, hyb_w_out, sb_w_qkv, sb_w_o, mlp_w_up, mlp_w_down, loss_target, m_norm_gains, m_hyb_w_in, m_hyb_conv_a, m_hyb_conv_b, m_hyb_conv_b_bias, m_hyb_rg_w_a, m_hyb_rg_b_a, m_hyb_rg_w_x, m_hyb_rg_b_x, m_hyb_rg_lambda, m_hyb_w_out, m_sb_w_qkv, m_sb_w_o, m_mlp_w_up, m_mlp_w_down, v_norm_gains, v_hyb_w_in, v_hyb_conv_a, v_hyb_conv_b, v_hyb_conv_b_bias, v_hyb_rg_w_a, v_hyb_rg_b_a, v_hyb_rg_w_x, v_hyb_rg_b_x, v_hyb_rg_lambda, v_hyb_w_out, v_sb_w_qkv, v_sb_w_o, v_mlp_w_up, v_mlp_w_down):
    T, D = x.shape[1], x.shape[2]
    me = _my_index()

    small_shards, small_spans = _pack_rows([norm_gains, hyb_conv_a[0], hyb_conv_b[0]])
    (small_all,) = _all_gather("gather_small", [small_shards])
    gains_b = _unpack_rows(small_all, small_spans[0])
    gains = jnp.transpose(gains_b, (1, 2, 0, 3)).reshape(2, 4, D)
    conv_a = _from_col_blocks(_unpack_rows(small_all, small_spans[1]))
    conv_b = _from_col_blocks(_unpack_rows(small_all, small_spans[2]))

    shards = {"w_in": hyb_w_in[0].T, "w_out": hyb_w_out[0], "w_qkv": sb_w_qkv[0].T, "w_o": sb_w_o[0],
              "up0": mlp_w_up[0], "up1": mlp_w_up[1], "down0": mlp_w_down[0], "down1": mlp_w_down[1]}
    shards = {n: s.astype(BF16) for n, s in shards.items()}
    first = _run_jobs("gather_first", [_gather_job(shards[n]) for n in GATHER_FIRST])
    wts = _Weights({n: _gathered_layout(n, g) for n, g in zip(GATHER_FIRST, first)}, shards, GATHER_PLAN)
    assert not set(GATHER_FIRST) & {e[0] for es in GATHER_PLAN.values() for e in es}
    grads_big = _Grads({n: lax.empty((N_DEV,) + s.shape, BF16) for n, s in shards.items()}, EXCHANGE_PLAN)

    sq, grad_x, small = _local_step(
        x[0], loss_target[0], gains, conv_a, conv_b, hyb_conv_b_bias, hyb_rg_w_a[0], hyb_rg_b_a, hyb_rg_w_x[0],
        hyb_rg_b_x, hyb_rg_lambda, wts, grads_big)
    loss = lax.psum(0.5 * sq / D, ("x", "y", "c"))


    names = ["norm_gains", "hyb_w_in", "hyb_conv_a", "hyb_conv_b", "hyb_conv_b_bias", "hyb_rg_w_a", "hyb_rg_b_a",
             "hyb_rg_w_x", "hyb_rg_b_x", "hyb_rg_lambda", "hyb_w_out", "sb_w_qkv", "sb_w_o", "mlp_w_up", "mlp_w_down"]
    params = dict(zip(names, [norm_gains, hyb_w_in, hyb_conv_a, hyb_conv_b, hyb_conv_b_bias, hyb_rg_w_a, hyb_rg_b_a,
                              hyb_rg_w_x, hyb_rg_b_x, hyb_rg_lambda, hyb_w_out, sb_w_qkv, sb_w_o, mlp_w_up, mlp_w_down]))
    moms = dict(zip(names, [m_norm_gains, m_hyb_w_in, m_hyb_conv_a, m_hyb_conv_b, m_hyb_conv_b_bias, m_hyb_rg_w_a,
                            m_hyb_rg_b_a, m_hyb_rg_w_x, m_hyb_rg_b_x, m_hyb_rg_lambda, m_hyb_w_out, m_sb_w_qkv,
                            m_sb_w_o, m_mlp_w_up, m_mlp_w_down]))
    vars_ = dict(zip(names, [v_norm_gains, v_hyb_w_in, v_hyb_conv_a, v_hyb_conv_b, v_hyb_conv_b_bias, v_hyb_rg_w_a,
                             v_hyb_rg_b_a, v_hyb_rg_w_x, v_hyb_rg_b_x, v_hyb_rg_lambda, v_hyb_w_out, v_sb_w_qkv,
                             v_sb_w_o, v_mlp_w_up, v_mlp_w_down]))
    grads, deltas, new_m, new_v = {}, {}, {}, {}

    big_lands = {"mlp_w_down": ["down0", "down1"], "mlp_w_up": ["up0", "up1"], "sb_w_qkv": ["w_qkv"], "sb_w_o": ["w_o"],
                 "hyb_w_out": ["w_out"], "hyb_w_in": ["w_in"]}
    for nm, keys in big_lands.items():
        call = f"adamw_{nm}"
        if nm == "hyb_w_in":
            grads_big.flush("exchange_grads")
        jobs = grads_big.jobs(call)
        assert not {k for k in keys} & {e[0] for e in EXCHANGE_PLAN.get(call, ())}
        outs = _adamw_big(call, [grads_big.lands[k] for k in keys], params[nm], moms[nm], vars_[nm], jobs=jobs,
                          transposed=nm in ("hyb_w_in", "sb_w_qkv"))
        grads[nm], deltas[nm], new_m[nm], new_v[nm] = outs[:4]
        grads_big.deliver(call, outs[4:])

    small_names = ["norm_gains", "hyb_conv_a", "hyb_conv_b", "hyb_conv_b_bias", "hyb_rg_w_a", "hyb_rg_b_a",
                   "hyb_rg_w_x", "hyb_rg_b_x", "hyb_rg_lambda"]
    small_keys = ["norm_gains", "conv_a", "conv_b", "conv_b_bias", "rg_w_a", "rg_b_a", "rg_w_x", "rg_b_x", "rg_lambda"]
    sg_buf, sg_spans = _pack_rows([small[k] for k in small_keys])
    (sg_all,) = _all_gather("gather_small_grads", [sg_buf])
    sg_sum = _sum8("sum_small_grads", sg_all)
    full = {nm: _unpack_rows(sg_sum, sp) for nm, sp in zip(small_names, sg_spans)}
    cb = (D // 2) // N_DEV
    small_grads = {
        "norm_gains": lax.dynamic_slice_in_dim(full["norm_gains"], me * (D // N_DEV), D // N_DEV, axis=2),
        "hyb_conv_a": lax.dynamic_slice_in_dim(full["hyb_conv_a"], me * cb, cb, axis=1)[None],
        "hyb_conv_b": lax.dynamic_slice_in_dim(full["hyb_conv_b"], me * cb, cb, axis=1)[None],
        "hyb_conv_b_bias": full["hyb_conv_b_bias"],
        "hyb_rg_w_a": full["hyb_rg_w_a"][None],
        "hyb_rg_b_a": full["hyb_rg_b_a"],
        "hyb_rg_w_x": full["hyb_rg_w_x"][None],
        "hyb_rg_b_x": full["hyb_rg_b_x"],
        "hyb_rg_lambda": full["hyb_rg_lambda"],
    }
    pk = lambda d: _pack_rows([d[nm] for nm in small_names])
    g_buf, spans = pk(small_grads)
    w_buf, _ = pk(params)
    m_buf, _ = pk(moms)
    v_buf, _ = pk(vars_)
    d_buf, nm_buf, nv_buf = _adamw_small("adamw_small", g_buf, w_buf, m_buf, v_buf)
    for nm, sp in zip(small_names, spans):
        grads[nm] = small_grads[nm]
        deltas[nm], new_m[nm], new_v[nm] = _unpack_rows(d_buf, sp), _unpack_rows(nm_buf, sp), _unpack_rows(nv_buf, sp)

    return (loss, grad_x[None], *[grads[n] for n in names], *[deltas[n] for n in names],
            *[new_m[n] for n in names], *[new_v[n] for n in names])
```

```python
import functools
import math

import jax
import jax.numpy as jnp
from jax import lax
from jax.experimental import pallas as pl
from jax.experimental.pallas import tpu as pltpu

F32 = jnp.float32
BF16 = jnp.bfloat16

NORM_EPS = 1e-6
LRU_C = 8.0
ATT_HEAD_DIM = 128
RG_HEAD_DIM = 64
LANES = 128
SUBLANES = 8
N_DEV = 8
ADAM_LR = 0.001
ADAM_B1 = 0.9
ADAM_B2 = 0.999
ADAM_EPS = 1e-08
ADAM_WD = 0.01
ADAM_STEP = 10
VMEM_LIMIT = 56 * 1024 * 1024
MM_TK = 2048
MESH = pl.DeviceIdType.MESH


def _tile(n, pref):
    if n <= pref:
        return n
    t = (pref // LANES) * LANES
    while t > LANES and n % t:
        t -= LANES
    assert n % t == 0, (n, pref)
    return t


def _params(sem):
    return pltpu.CompilerParams(dimension_semantics=sem, vmem_limit_bytes=VMEM_LIMIT)


DIMS_NN = (((1,), (0,)), ((), ()))
DIMS_NT = (((1,), (1,)), ((), ()))
DIMS_TN = (((0,), (0,)), ((), ()))


_ANY = pl.BlockSpec(memory_space=pl.ANY)


class _Job:
    def __init__(self, ins, outs, sems, start, mid, end, alias=None):
        self.ins, self.outs, self.sems = ins, outs, sems
        self.start, self.mid, self.end = start, mid, end
        self.alias = alias or {}


def _job_plumbing(jobs, n_in, n_out):
    j_ins = [a for jb in jobs for a in jb.ins]
    j_outs = [o for jb in jobs for o in jb.outs]
    j_sems = [s for jb in jobs for s in jb.sems]
    aliases, pi, po = {}, 0, 0
    for jb in jobs:
        for i_in, i_out in jb.alias.items():
            aliases[n_in + pi + i_in] = n_out + po + i_out
        pi += len(jb.ins)
        po += len(jb.outs)
    return j_ins, j_outs, j_sems, aliases


def _job_phase(jobs, which, jin, jout, jsem):
    pi = po = ps = 0
    for jb in jobs:
        getattr(jb, which)(jin[pi:pi + len(jb.ins)], jout[po:po + len(jb.outs)], jsem[ps:ps + len(jb.sems)])
        pi, po, ps = pi + len(jb.ins), po + len(jb.outs), ps + len(jb.sems)


def _run_jobs(name, jobs):
    j_ins, j_outs, j_sems, aliases = _job_plumbing(jobs, 0, 0)
    n_ji, n_jo = len(j_ins), len(j_outs)

    def body(*refs):
        jin, jout, jsem = refs[:n_ji], refs[n_ji:n_ji + n_jo], refs[n_ji + n_jo:]
        for which in ("start", "mid", "end"):
            _job_phase(jobs, which, jin, jout, jsem)

    return pl.pallas_call(body, name=name, in_specs=[_ANY] * n_ji, out_specs=[_ANY] * n_jo, out_shape=j_outs,
                          scratch_shapes=j_sems, input_output_aliases=aliases)(*j_ins)


def _carry_call(body, *, name, steps, in_specs, out_specs, out_shape, scratch_shapes, args, jobs=(), aliases=None):
    n_in, n_out, n_sc = len(in_specs), len(out_shape), len(scratch_shapes)
    j_ins, j_outs, j_sems, j_aliases = _job_plumbing(jobs, n_in, n_out)
    n_ji, n_jo = len(j_ins), len(j_outs)

    def wrapped(*refs):
        ins, jin = refs[:n_in], refs[n_in:n_in + n_ji]
        o0 = n_in + n_ji
        outs, jout = refs[o0:o0 + n_out], refs[o0 + n_out:o0 + n_out + n_jo]
        s0 = o0 + n_out + n_jo
        scratch, jsem = refs[s0:s0 + n_sc], refs[s0 + n_sc:]
        step = pl.program_id(0)
        if jobs:
            pl.when(step == 0)(lambda: _job_phase(jobs, "start", jin, jout, jsem))
            pl.when(step == (4 * steps) // 5)(lambda: _job_phase(jobs, "mid", jin, jout, jsem))
        body(*ins, *outs, *scratch)
        if jobs:
            pl.when(step == steps - 1)(lambda: _job_phase(jobs, "end", jin, jout, jsem))

    return pl.pallas_call(
        wrapped, name=name, grid=(steps,),
        in_specs=list(in_specs) + [_ANY] * n_ji, out_specs=list(out_specs) + [_ANY] * n_jo,
        out_shape=list(out_shape) + j_outs, scratch_shapes=list(scratch_shapes) + j_sems,
        input_output_aliases={**(aliases or {}), **j_aliases},
        compiler_params=_params(("arbitrary",) if jobs else ("parallel",)))(*args, *j_ins)


def _matmul(name, ins, outs, grid, dims, acc_shape, epilogue=None, jobs=()):
    n_in, n_out, nk = len(ins), len(outs), grid[2]
    j_ins, j_outs, j_sems, aliases = _job_plumbing(jobs, n_in, n_out)
    n_ji, n_jo = len(j_ins), len(j_outs)
    total = grid[0] * grid[1] * grid[2]
    n_acc = 0 if nk == 1 else 1

    def body(*refs):
        a_ref, b_ref = refs[0], refs[1]
        extras = refs[2:n_in]
        jin = refs[n_in:n_in + n_ji]
        out_refs = refs[n_in + n_ji:n_in + n_ji + n_out]
        jout = refs[n_in + n_ji + n_out:n_in + n_ji + n_out + n_jo]
        jsem = refs[n_in + n_ji + n_out + n_jo + n_acc:]
        k = pl.program_id(2)
        step = (pl.program_id(0) * grid[1] + pl.program_id(1)) * grid[2] + k
        if jobs:
            pl.when(step == 0)(lambda: _job_phase(jobs, "start", jin, jout, jsem))
            pl.when(step == (4 * total) // 5)(lambda: _job_phase(jobs, "mid", jin, jout, jsem))

        def finish(r):
            res = epilogue(r, *[e[...] for e in extras]) if epilogue is not None else (r,)
            for o, v in zip(out_refs, res):
                o[...] = v.astype(o.dtype)

        if len(b_ref.shape) == 2:
            prod = lax.dot_general(a_ref[...], b_ref[...], dims, preferred_element_type=F32)
        else:
            kb = a_ref.shape[1] // b_ref.shape[0]
            prod = sum(lax.dot_general(a_ref[:, g * kb:(g + 1) * kb], b_ref[g], dims, preferred_element_type=F32)
                       for g in range(b_ref.shape[0]))
        if nk == 1:
            finish(prod)
        else:
            acc = refs[n_in + n_ji + n_out + n_jo]

            @pl.when(k == 0)
            def _():
                acc[...] = prod

            @pl.when((k > 0) & (k < nk - 1))
            def _():
                acc[...] += prod

            @pl.when(k == nk - 1)
            def _():
                finish(acc[...] + prod)

        if jobs:
            pl.when(step == total - 1)(lambda: _job_phase(jobs, "end", jin, jout, jsem))

    sem = ("arbitrary",) * 3 if jobs else ("parallel", "parallel", "arbitrary")
    res = pl.pallas_call(
        body, name=name, grid=grid,
        in_specs=[s for _, s in ins] + [_ANY] * n_ji,
        out_specs=[s for _, s in outs] + [_ANY] * n_jo,
        out_shape=[s for s, _ in outs] + j_outs,
        scratch_shapes=[pltpu.VMEM(acc_shape, F32)] * n_acc + j_sems,
        input_output_aliases=aliases,
        compiler_params=_params(sem),
    )(*[a for a, _ in ins], *j_ins)
    return res


def _mm_nn(name, a, b, out_dtype, *, a_seg=None, out_seg=None, tm=1024, tn=1024, tk=MM_TK, epilogue=None,
           extras=(), n_out=1, out_dtypes=None, jobs=()):
    if a_seg:
        _, M, ks = a.shape
        K = ks * a_seg
    else:
        M, K = a.shape
        ks = K
    N = b.shape[1]
    ns = N // out_seg if out_seg else N
    tm, tn, tk = _tile(M, tm), _tile(ns, tn), _tile(ks, tk)
    nks, nns = ks // tk, ns // tn
    grid = (M // tm, N // tn, K // tk)
    if a_seg:
        a_spec = pl.BlockSpec((None, tm, tk), lambda i, j, k: (k // nks, i, k % nks))
    else:
        a_spec = pl.BlockSpec((tm, tk), lambda i, j, k: (i, k))
    b_spec = pl.BlockSpec((tk, tn), lambda i, j, k: (k, j))
    if out_seg:
        o_spec = pl.BlockSpec((None, tm, tn), lambda i, j, k: (j // nns, i, j % nns))
        o_shape = (out_seg, M, ns)
    else:
        o_spec = pl.BlockSpec((tm, tn), lambda i, j, k: (i, j))
        o_shape = (M, N)
    dts = out_dtypes or [out_dtype] * n_out
    outs = [(jax.ShapeDtypeStruct(o_shape, dt), o_spec) for dt in dts]
    ins = [(a, a_spec), (b, b_spec)] + [(e, o_spec) for e in extras]
    return _matmul(name, ins, outs, grid, DIMS_NN, (tm, tn), epilogue, jobs)


def _mm_nt(name, a, b, out_dtype, *, a_seg=None, out_seg=None, tm=1024, tn=1024, tk=MM_TK, epilogue=None, extras=(),
           jobs=()):
    if a_seg:
        _, M, ks = a.shape
        K = ks * a_seg
    else:
        M, K = a.shape
        ks = K
    N = b.shape[0]
    ns = N // out_seg if out_seg else N
    tm, tn, tk = _tile(M, tm), _tile(ns, tn), _tile(ks, tk)
    nks, nns = ks // tk, ns // tn
    grid = (M // tm, N // tn, K // tk)
    if a_seg:
        a_spec = pl.BlockSpec((None, tm, tk), lambda i, j, k: (k // nks, i, k % nks))
    else:
        a_spec = pl.BlockSpec((tm, tk), lambda i, j, k: (i, k))
    b_spec = pl.BlockSpec((tn, tk), lambda i, j, k: (j, k))
    if out_seg:
        o_spec = pl.BlockSpec((None, tm, tn), lambda i, j, k: (j // nns, i, j % nns))
        o_shape = (out_seg, M, ns)
    else:
        o_spec = pl.BlockSpec((tm, tn), lambda i, j, k: (i, j))
        o_shape = (M, N)
    outs = [(jax.ShapeDtypeStruct(o_shape, out_dtype), o_spec)]
    ins = [(a, a_spec), (b, b_spec)] + [(e, o_spec) for e in extras]
    return _matmul(name, ins, outs, grid, DIMS_NT, (tm, tn), epilogue, jobs)


def _mm_tn(name, a, b, out_dtype, *, a_seg=None, b_seg=None, out_blocks=None, tm=1024, tn=1024, tk=MM_TK, jobs=()):
    if a_seg:
        _, T, ms = a.shape
        M = ms * a_seg
    else:
        T, M = a.shape
        ms = M
    if b_seg:
        _, _, ns = b.shape
        N = ns * b_seg
    else:
        N = b.shape[1]
        ns = N
    nb_cols = N // out_blocks if out_blocks else N
    tm, tk = _tile(ms, tm), _tile(T, tk)
    tn = _tile(math.gcd(ns, nb_cols), tn)
    nms, nns, nbs = ms // tm, ns // tn, nb_cols // tn
    grid = (M // tm, N // tn, T // tk)
    if a_seg:
        a_spec = pl.BlockSpec((None, tk, tm), lambda i, j, k: (i // nms, k, i % nms))
    else:
        a_spec = pl.BlockSpec((tk, tm), lambda i, j, k: (k, i))
    if b_seg:
        b_spec = pl.BlockSpec((None, tk, tn), lambda i, j, k: (j // nns, k, j % nns))
    else:
        b_spec = pl.BlockSpec((tk, tn), lambda i, j, k: (k, j))
    if out_blocks:
        o_spec = pl.BlockSpec((None, tm, tn), lambda i, j, k: (j // nbs, i, j % nbs))
        o_shape = (out_blocks, M, nb_cols)
    else:
        o_spec = pl.BlockSpec((tm, tn), lambda i, j, k: (i, j))
        o_shape = (M, N)
    outs = [(jax.ShapeDtypeStruct(o_shape, out_dtype), o_spec)]
    return _matmul(name, [(a, a_spec), (b, b_spec)], outs, grid, DIMS_TN, (tm, tn), None, jobs)


def _rms(x):
    return lax.rsqrt(jnp.mean(x * x, axis=-1, keepdims=True) + NORM_EPS)


def _row_tile(T):
    return _tile(T, 256)


def _norm_fwd(name, x, g, jobs=()):
    T, D = x.shape
    tr = _row_tile(T)

    def body(x_ref, g_ref, h_ref):
        xv = x_ref[...]
        h_ref[...] = (xv * _rms(xv) * g_ref[...]).astype(h_ref.dtype)

    row = pl.BlockSpec((tr, D), lambda i: (i, 0))
    vec = pl.BlockSpec((1, D), lambda i: (0, 0))
    return _carry_call(body, name=name, steps=T // tr, in_specs=[row, vec], out_specs=[row],
                       out_shape=[jax.ShapeDtypeStruct((T, D), BF16)], scratch_shapes=[], args=(x, g), jobs=jobs)


def _resid_norm(name, x, br, g_post, g_next, jobs=()):
    T, D = x.shape
    tr = _row_tile(T)

    def body(x_ref, br_ref, gp_ref, gn_ref, xn_ref, h_ref):
        b = br_ref[...].astype(F32)
        xn = x_ref[...] + b * _rms(b) * gp_ref[...]
        xn_ref[...] = xn
        h_ref[...] = (xn * _rms(xn) * gn_ref[...]).astype(h_ref.dtype)

    row = pl.BlockSpec((tr, D), lambda i: (i, 0))
    vec = pl.BlockSpec((1, D), lambda i: (0, 0))
    return _carry_call(body, name=name, steps=T // tr, in_specs=[row, row, vec, vec], out_specs=[row, row],
                       out_shape=[jax.ShapeDtypeStruct((T, D), F32), jax.ShapeDtypeStruct((T, D), BF16)],
                       scratch_shapes=[], args=(x, br, g_post, g_next), jobs=jobs)


def _final_loss(name, x, br, g_post, target):
    T, D = x.shape
    tr = _row_tile(T)

    def body(x_ref, br_ref, gp_ref, t_ref, dy_ref, ls_ref):
        b = br_ref[...].astype(F32)
        err = x_ref[...] + b * _rms(b) * gp_ref[...] - t_ref[...]
        dy_ref[...] = err * (1.0 / D)

        @pl.when(pl.program_id(0) == 0)
        def _():
            ls_ref[...] = jnp.zeros_like(ls_ref)

        ls_ref[...] += jnp.sum(err * err)

    row = pl.BlockSpec((tr, D), lambda i: (i, 0))
    vec = pl.BlockSpec((1, D), lambda i: (0, 0))
    acc = pl.BlockSpec((SUBLANES, LANES), lambda i: (0, 0))
    return pl.pallas_call(body, name=name, grid=(T // tr,), in_specs=[row, row, vec, row], out_specs=[row, acc],
                          out_shape=[jax.ShapeDtypeStruct((T, D), F32), jax.ShapeDtypeStruct((SUBLANES, LANES), F32)],
                          compiler_params=_params(("arbitrary",)))(x, br, g_post, target)


def _norm_bwd(name, x, g, dy, add, out_dtype, jobs=()):
    T, D = x.shape
    tr = _row_tile(T)
    has_add = add is not None

    def body(*refs):
        if has_add:
            x_ref, g_ref, dy_ref, add_ref, dx_ref, dg_ref = refs
        else:
            x_ref, g_ref, dy_ref, dx_ref, dg_ref = refs
        xv = x_ref[...].astype(F32)
        r = _rms(xv)
        xhat = xv * r
        dyv = dy_ref[...].astype(F32)
        gdy = dyv * g_ref[...]
        dx = r * (gdy - xhat * jnp.mean(gdy * xhat, axis=-1, keepdims=True))
        if has_add:
            dx = dx + add_ref[...]
        dx_ref[...] = dx.astype(dx_ref.dtype)

        @pl.when(pl.program_id(0) == 0)
        def _():
            dg_ref[...] = jnp.zeros_like(dg_ref)

        dg_ref[...] += jnp.sum(dyv * xhat, axis=0, keepdims=True)

    row = pl.BlockSpec((tr, D), lambda i: (i, 0))
    vec = pl.BlockSpec((1, D), lambda i: (0, 0))
    ins = [x, g, dy] + ([add] if has_add else [])
    specs = [row, vec, row] + ([row] if has_add else [])
    if not jobs:
        return pl.pallas_call(body, name=name, grid=(T // tr,), in_specs=specs, out_specs=[row, vec],
                              out_shape=[jax.ShapeDtypeStruct((T, D), out_dtype), jax.ShapeDtypeStruct((1, D), F32)],
                              compiler_params=_params(("arbitrary",)))(*ins)
    return _carry_call(body, name=name, steps=T // tr, in_specs=specs, out_specs=[row, vec],
                       out_shape=[jax.ShapeDtypeStruct((T, D), out_dtype), jax.ShapeDtypeStruct((1, D), F32)],
                       scratch_shapes=[], args=ins, jobs=jobs)


HALO = SUBLANES
TIME_CHUNK = 512


def _chunks(T):
    tc = min(TIME_CHUNK, T)
    assert T % tc == 0 and tc % SUBLANES == 0
    return [(t0, tc) for t0 in range(0, T, tc)]


def _log_sigmoid(x):
    return -(jnp.maximum(-x, 0.0) + jnp.log(1.0 + jnp.exp(-jnp.abs(x))))


def _one_minus_exp(x):
    series = -x * (1.0 + x * (0.5 + x * (1.0 / 6.0 + x * (1.0 / 24.0))))
    return jnp.where(x > -0.01, series, 1.0 - jnp.exp(x))


_GELU_C = math.sqrt(2.0 / math.pi)


def _gelu(x):
    return 0.5 * x * (1.0 + jnp.tanh(_GELU_C * (x + 0.044715 * x * x * x)))


def _gelu_grad(x):
    th = jnp.tanh(_GELU_C * (x + 0.044715 * x * x * x))
    return 0.5 * (1.0 + th) + 0.5 * x * (1.0 - th * th) * _GELU_C * (1.0 + 3.0 * 0.044715 * x * x)


def _tile_scan(a, b, reverse):
    rows = a.shape[0]
    pos = lax.broadcasted_iota(jnp.int32, a.shape, 0) & (SUBLANES - 1)
    for d in (1, 2, 4):
        if reverse:
            ok = pos < SUBLANES - d
            shift = rows - d
        else:
            ok = pos >= d
            shift = d
        a_sh = jnp.where(ok, pltpu.roll(a, shift, 0), 1.0)
        b_sh = jnp.where(ok, pltpu.roll(b, shift, 0), 0.0)
        b = a * b_sh + b
        a = a * a_sh
    return a, b


def _carry_scan(a_s, b_s, T, reverse):
    n = T // SUBLANES
    edge = 0 if reverse else SUBLANES - 1

    def step(j, carry):
        g = (n - 1 - j) if reverse else j
        r = pl.multiple_of(g * SUBLANES, SUBLANES)
        h = b_s[pl.ds(r, SUBLANES), :] + a_s[pl.ds(r, SUBLANES), :] * carry
        b_s[pl.ds(r, SUBLANES), :] = h
        return jnp.broadcast_to(h[edge:edge + 1, :], h.shape)

    lax.fori_loop(0, n, step, jnp.zeros((SUBLANES, a_s.shape[1]), F32))


def _seg_spec(T, seg, nblk):
    return pl.BlockSpec((None, T, LANES), lambda c: (seg, 0, c))


def _rows_to_tile(rows):
    idx = lax.broadcasted_iota(jnp.int32, (SUBLANES, LANES), 0)
    out = jnp.zeros((SUBLANES, LANES), F32)
    for k, r in enumerate(rows):
        out = jnp.where(idx == k, r, out)
    return out


def _mixer_a_fwd(proj, conv_a):
    _, T, C = proj.shape
    nblk = C // LANES
    chunks = _chunks(T)

    def body(bg_ref, cg_ref, ax_ref, w_ref, y_ref, p_s):
        p_s[pl.ds(0, HALO), :] = jnp.zeros((HALO, LANES), F32)
        for t0, tc in chunks:
            p_s[pl.ds(HALO + t0, tc), :] = cg_ref[pl.ds(t0, tc), :] * ax_ref[pl.ds(t0, tc), :]
        w = w_ref[...]
        for t0, tc in chunks:
            c = (w[2:3, :] * p_s[pl.ds(HALO + t0, tc), :] + w[1:2, :] * p_s[pl.ds(HALO + t0 - 1, tc), :]
                 + w[0:1, :] * p_s[pl.ds(HALO + t0 - 2, tc), :])
            y_ref[pl.ds(t0, tc), :] = (bg_ref[pl.ds(t0, tc), :] * c).astype(y_ref.dtype)

    return pl.pallas_call(
        body, name="mixer_a_fwd", grid=(nblk,),
        in_specs=[_seg_spec(T, 0, nblk), _seg_spec(T, 1, nblk), _seg_spec(T, 2, nblk),
                  pl.BlockSpec((3, LANES), lambda c: (0, c))],
        out_specs=_seg_spec(T, 0, nblk),
        out_shape=jax.ShapeDtypeStruct((2, T, C), BF16),
        scratch_shapes=[pltpu.VMEM((T + HALO, LANES), F32)],
        compiler_params=_params(("parallel",)))(proj, proj, proj, conv_a)


def _mixer_a_bwd(proj, conv_a, dy):
    _, T, C = proj.shape
    nblk = C // LANES
    chunks = _chunks(T)

    def body(bg_ref, cg_ref, ax_ref, w_ref, dy_ref, dp_ref, dw_ref, p_s, dc_s):
        p_s[pl.ds(0, HALO), :] = jnp.zeros((HALO, LANES), F32)
        dc_s[pl.ds(T, HALO), :] = jnp.zeros((HALO, LANES), F32)
        for t0, tc in chunks:
            p_s[pl.ds(HALO + t0, tc), :] = cg_ref[pl.ds(t0, tc), :] * ax_ref[pl.ds(t0, tc), :]
        w = w_ref[...]
        for t0, tc in chunks:
            c = (w[2:3, :] * p_s[pl.ds(HALO + t0, tc), :] + w[1:2, :] * p_s[pl.ds(HALO + t0 - 1, tc), :]
                 + w[0:1, :] * p_s[pl.ds(HALO + t0 - 2, tc), :])
            dyv = dy_ref[pl.ds(t0, tc), :]
            dp_ref[0, pl.ds(t0, tc), :] = (dyv * c).astype(dp_ref.dtype)
            dc_s[pl.ds(t0, tc), :] = dyv * bg_ref[pl.ds(t0, tc), :]
        dw = [jnp.zeros((1, LANES), F32) for _ in range(3)]
        for t0, tc in chunks:
            dc = dc_s[pl.ds(t0, tc), :]
            dpv = w[2:3, :] * dc + w[1:2, :] * dc_s[pl.ds(t0 + 1, tc), :] + w[0:1, :] * dc_s[pl.ds(t0 + 2, tc), :]
            dp_ref[1, pl.ds(t0, tc), :] = (dpv * ax_ref[pl.ds(t0, tc), :]).astype(dp_ref.dtype)
            dp_ref[2, pl.ds(t0, tc), :] = (dpv * cg_ref[pl.ds(t0, tc), :]).astype(dp_ref.dtype)
            for k in range(3):
                dw[k] = dw[k] + jnp.sum(dc * p_s[pl.ds(HALO + t0 - (2 - k), tc), :], axis=0, keepdims=True)
        dw_ref[...] = _rows_to_tile(dw)

    return pl.pallas_call(
        body, name="mixer_a_bwd", grid=(nblk,),
        in_specs=[_seg_spec(T, 0, nblk), _seg_spec(T, 1, nblk), _seg_spec(T, 2, nblk),
                  pl.BlockSpec((3, LANES), lambda c: (0, c)), _seg_spec(T, 0, nblk)],
        out_specs=[pl.BlockSpec((3, T, LANES), lambda c: (0, 0, c)),
                   pl.BlockSpec((None, SUBLANES, LANES), lambda c: (c, 0, 0))],
        out_shape=[jax.ShapeDtypeStruct((6, T, C), BF16), jax.ShapeDtypeStruct((nblk, SUBLANES, LANES), F32)],
        scratch_shapes=[pltpu.VMEM((T + HALO, LANES), F32), pltpu.VMEM((T + HALO, LANES), F32)],
        compiler_params=_params(("parallel",)))(proj, proj, proj, conv_a, dy)


def _rg_gates(xr, wa, ba, wx, bx, ls):
    xb = xr.astype(BF16)
    r = jax.nn.sigmoid(jnp.dot(xb, wa, preferred_element_type=F32) + ba)
    i = jax.nn.sigmoid(jnp.dot(xb, wx, preferred_element_type=F32) + bx)
    log_a = LRU_C * r * ls
    a = jnp.exp(log_a)
    mult = jnp.sqrt(_one_minus_exp(2.0 * log_a))
    return r, i, a, mult


def _conv4(xh_s, cw, bias, t0, tc):
    return (cw[3:4, :] * xh_s[pl.ds(HALO + t0, tc), :] + cw[2:3, :] * xh_s[pl.ds(HALO + t0 - 1, tc), :]
            + cw[1:2, :] * xh_s[pl.ds(HALO + t0 - 2, tc), :] + cw[0:1, :] * xh_s[pl.ds(HALO + t0 - 3, tc), :] + bias)


def _mixer_b_specs(T, nblk):
    vec = pl.BlockSpec((1, LANES), lambda c: (0, c))
    mat = pl.BlockSpec((None, LANES, LANES), lambda c: (c, 0, 0))
    return [_seg_spec(T, 3, nblk), _seg_spec(T, 4, nblk), pl.BlockSpec((4, LANES), lambda c: (0, c)),
            vec, mat, vec, mat, vec, vec]


def _mixer_b_fwd(name, proj, conv_b, bias, wa, ba, wx, bx, lam, y, jobs=()):
    _, T, C = proj.shape
    nblk = C // LANES
    chunks = _chunks(T)

    def body(gate_ref, x_ref, cw_ref, cb_ref, wa_ref, ba_ref, wx_ref, bx_ref, lam_ref, y_in, y_ref, xh_s, a_s, b_s):
        xh_s[pl.ds(0, HALO), :] = jnp.zeros((HALO, LANES), F32)
        for t0, tc in chunks:
            xh_s[pl.ds(HALO + t0, tc), :] = x_ref[pl.ds(t0, tc), :]
        cw, bias_v = cw_ref[...], cb_ref[...]
        ls = _log_sigmoid(lam_ref[...])
        for t0, tc in chunks:
            xr = _conv4(xh_s, cw, bias_v, t0, tc)
            r, i, a, mult = _rg_gates(xr, wa_ref[...], ba_ref[...], wx_ref[...], bx_ref[...], ls)
            ac, hc = _tile_scan(a, mult * i * xr, reverse=False)
            a_s[pl.ds(t0, tc), :] = ac
            b_s[pl.ds(t0, tc), :] = hc
        _carry_scan(a_s, b_s, T, reverse=False)
        for t0, tc in chunks:
            y_ref[pl.ds(t0, tc), :] = (b_s[pl.ds(t0, tc), :] * _gelu(gate_ref[pl.ds(t0, tc), :])).astype(y_ref.dtype)

    return _carry_call(
        body, name=name, steps=nblk, in_specs=_mixer_b_specs(T, nblk) + [_ANY],
        out_specs=[_seg_spec(T, 1, nblk)], out_shape=[jax.ShapeDtypeStruct(y.shape, y.dtype)],
        scratch_shapes=[pltpu.VMEM((T + HALO, LANES), F32), pltpu.VMEM((T, LANES), F32), pltpu.VMEM((T, LANES), F32)],
        args=(proj, proj, conv_b, bias, wa, ba, wx, bx, lam, y), jobs=jobs, aliases={9: 0})


_ROW_CONV, _ROW_BIAS, _ROW_BA, _ROW_BX, _ROW_LAM = 0, 4, 5, 6, 7


def _mixer_b_bwd(name, proj, conv_b, bias, wa, ba, wx, bx, lam, dy, dproj, jobs=()):
    _, T, C = proj.shape
    nblk = C // LANES
    chunks = _chunks(T)

    def body(gate_ref, x_ref, cw_ref, cb_ref, wa_ref, ba_ref, wx_ref, bx_ref, lam_ref, dy_ref, dp_in,
             dp_ref, sm_ref, dwa_ref, dwx_ref, xh_s, xr_s, r_s, i_s, a_s, h_s, sa_s, sb_s, dx_s):
        zero_halo = jnp.zeros((HALO, LANES), F32)
        xh_s[pl.ds(0, HALO), :] = zero_halo
        h_s[pl.ds(0, HALO), :] = zero_halo
        a_s[pl.ds(T, HALO), :] = zero_halo
        dx_s[pl.ds(T, HALO), :] = zero_halo
        for t0, tc in chunks:
            xh_s[pl.ds(HALO + t0, tc), :] = x_ref[pl.ds(t0, tc), :]
        cw, bias_v = cw_ref[...], cb_ref[...]
        lam_v = lam_ref[...]
        ls = _log_sigmoid(lam_v)
        wa_v, wx_v, ba_v, bx_v = wa_ref[...], wx_ref[...], ba_ref[...], bx_ref[...]
        for t0, tc in chunks:
            xr = _conv4(xh_s, cw, bias_v, t0, tc)
            r, i, a, mult = _rg_gates(xr, wa_v, ba_v, wx_v, bx_v, ls)
            xr_s[pl.ds(t0, tc), :] = xr
            r_s[pl.ds(t0, tc), :] = r
            i_s[pl.ds(t0, tc), :] = i
            a_s[pl.ds(t0, tc), :] = a
            ac, hc = _tile_scan(a, mult * i * xr, reverse=False)
            sa_s[pl.ds(t0, tc), :] = ac
            sb_s[pl.ds(t0, tc), :] = hc
        _carry_scan(sa_s, sb_s, T, reverse=False)
        for t0, tc in chunks:
            h_s[pl.ds(HALO + t0, tc), :] = sb_s[pl.ds(t0, tc), :]
        for t0, tc in chunks:
            gv = gate_ref[pl.ds(t0, tc), :]
            dyv = dy_ref[pl.ds(t0, tc), :]
            dp_ref[0, pl.ds(t0, tc), :] = (dyv * h_s[pl.ds(HALO + t0, tc), :] * _gelu_grad(gv)).astype(dp_ref.dtype)
            ac, gc = _tile_scan(a_s[pl.ds(t0 + 1, tc), :], dyv * _gelu(gv), reverse=True)
            sa_s[pl.ds(t0, tc), :] = ac
            sb_s[pl.ds(t0, tc), :] = gc
        _carry_scan(sa_s, sb_s, T, reverse=True)
        acc = {k: jnp.zeros((1, LANES), F32) for k in ("bias", "ba", "bx", "lam")}
        dwa = jnp.zeros((LANES, LANES), F32)
        dwx = jnp.zeros((LANES, LANES), F32)
        for t0, tc in chunks:
            dht = sb_s[pl.ds(t0, tc), :]
            xr, r, i, a = xr_s[pl.ds(t0, tc), :], r_s[pl.ds(t0, tc), :], i_s[pl.ds(t0, tc), :], a_s[pl.ds(t0, tc), :]
            mult = jnp.sqrt(_one_minus_exp(2.0 * LRU_C * r * ls))
            da = dht * h_s[pl.ds(HALO + t0 - 1, tc), :]
            dmult = dht * i * xr
            di = dht * mult * xr
            dlog_a = da * a - dmult * a * a / mult
            dpa = dlog_a * (LRU_C * ls) * r * (1.0 - r)
            dpx = di * i * (1.0 - i)
            acc["lam"] = acc["lam"] + jnp.sum(dlog_a * r, axis=0, keepdims=True)
            acc["ba"] = acc["ba"] + jnp.sum(dpa, axis=0, keepdims=True)
            acc["bx"] = acc["bx"] + jnp.sum(dpx, axis=0, keepdims=True)
            xb, dpab, dpxb = xr.astype(BF16), dpa.astype(BF16), dpx.astype(BF16)
            dwa = dwa + lax.dot_general(xb, dpab, DIMS_TN, preferred_element_type=F32)
            dwx = dwx + lax.dot_general(xb, dpxb, DIMS_TN, preferred_element_type=F32)
            dxr = (dht * mult * i + lax.dot_general(dpab, wa_v, DIMS_NT, preferred_element_type=F32)
                   + lax.dot_general(dpxb, wx_v, DIMS_NT, preferred_element_type=F32))
            acc["bias"] = acc["bias"] + jnp.sum(dxr, axis=0, keepdims=True)
            dx_s[pl.ds(t0, tc), :] = dxr
        dcw = [jnp.zeros((1, LANES), F32) for _ in range(4)]
        for t0, tc in chunks:
            dxr = dx_s[pl.ds(t0, tc), :]
            dxin = (cw[3:4, :] * dxr + cw[2:3, :] * dx_s[pl.ds(t0 + 1, tc), :] + cw[1:2, :] * dx_s[pl.ds(t0 + 2, tc), :]
                    + cw[0:1, :] * dx_s[pl.ds(t0 + 3, tc), :])
            dp_ref[1, pl.ds(t0, tc), :] = dxin.astype(dp_ref.dtype)
            for k in range(4):
                dcw[k] = dcw[k] + jnp.sum(dxr * xh_s[pl.ds(HALO + t0 - (3 - k), tc), :], axis=0, keepdims=True)
        dlam = acc["lam"] * LRU_C * jax.nn.sigmoid(-lam_v)
        sm_ref[...] = _rows_to_tile(dcw + [acc["bias"], acc["ba"], acc["bx"], dlam])
        dwa_ref[...] = dwa
        dwx_ref[...] = dwx

    big = lambda halo: pltpu.VMEM((T + halo, LANES), F32)
    mat = pl.BlockSpec((None, LANES, LANES), lambda c: (c, 0, 0))
    return _carry_call(
        body, name=name, steps=nblk,
        in_specs=_mixer_b_specs(T, nblk) + [_seg_spec(T, 1, nblk), _ANY],
        out_specs=[pl.BlockSpec((3, T, LANES), lambda c: (1, 0, c)),
                   pl.BlockSpec((None, SUBLANES, LANES), lambda c: (c, 0, 0)), mat, mat],
        out_shape=[jax.ShapeDtypeStruct(dproj.shape, dproj.dtype), jax.ShapeDtypeStruct((nblk, SUBLANES, LANES), F32),
                   jax.ShapeDtypeStruct((nblk, LANES, LANES), F32), jax.ShapeDtypeStruct((nblk, LANES, LANES), F32)],
        scratch_shapes=[big(HALO), big(0), big(0), big(0), big(HALO), big(HALO), big(0), big(0), big(HALO)],
        args=(proj, proj, conv_b, bias, wa, ba, wx, bx, lam, dy, dproj), jobs=jobs, aliases={10: 0})


ATT_BLOCK = 128
ATT_GROUP = 3
ATT_TILE = ATT_BLOCK * ATT_GROUP
ATT_UNDERFLOW = -110.0
ATT_UNVISITED = -1e30


def _split_dot(x, m):
    hi = x.astype(BF16)
    lo = (x - hi.astype(F32)).astype(BF16)
    return jnp.dot(hi, m, preferred_element_type=F32) + jnp.dot(lo, m, preferred_element_type=F32)


def _sub(x, j):
    return x[:, j * ATT_BLOCK:(j + 1) * ATT_BLOCK]


def _stack_rows(x):
    return jnp.concatenate([_sub(x, j) for j in range(ATT_GROUP)], axis=0)


def _unstack_rows(x, offsets):
    return jnp.concatenate([x[j * ATT_BLOCK:(j + 1) * ATT_BLOCK, :] + offsets[j] for j in range(ATT_GROUP)], axis=1)


def _att_tile(q, k_ref, q0, qb, it, scale):
    hi = (qb + 1 - ATT_GROUP * it) * ATT_BLOCK
    k0 = pl.multiple_of(jnp.maximum(hi - ATT_TILE, 0), ATT_BLOCK)
    kt = k_ref[pl.ds(k0, ATT_TILE), :]
    z = lax.dot_general(q, kt, DIMS_NT, preferred_element_type=F32) * scale
    key = k0 + lax.broadcasted_iota(jnp.int32, z.shape, 1)
    row = q0 + lax.broadcasted_iota(jnp.int32, z.shape, 0)
    mask = (key < row) & (key < hi)
    n = jnp.where(mask, -(jnp.maximum(z, 0.0) + jnp.log(1.0 + jnp.exp(-jnp.abs(z)))), 0.0)
    return k0, kt, z, mask, n


def _suffix_in_tile(n, upper, run):
    rs = [jnp.sum(_sub(n, j), axis=1, keepdims=True) for j in range(ATT_GROUP)]
    offs = [None] * ATT_GROUP
    offs[ATT_GROUP - 1] = run
    for j in range(ATT_GROUP - 2, -1, -1):
        offs[j] = offs[j + 1] + rs[j + 1]
    return _unstack_rows(_split_dot(_stack_rows(n), upper), offs), offs[0] + rs[0]


def _head_spec(T, seg, heads):
    return pl.BlockSpec((None, T, ATT_HEAD_DIM), lambda h: (seg, 0, h))


def _attention_fwd(name, qkv, jobs=()):
    _, T, D = qkv.shape
    heads = D // ATT_HEAD_DIM
    nq = T // ATT_BLOCK
    assert nq <= LANES and T >= ATT_TILE and nq % 2 == 0
    scale = 1.0 / math.sqrt(ATT_HEAD_DIM)

    def body(q_ref, k_ref, v_ref, o_ref, r_ref, acc_s, run_s):
        rr = lax.broadcasted_iota(jnp.int32, (ATT_BLOCK, ATT_BLOCK), 0)
        cc = lax.broadcasted_iota(jnp.int32, (ATT_BLOCK, ATT_BLOCK), 1)
        upper = jnp.where(rr > cc, 1.0, 0.0).astype(BF16)
        lane = lax.broadcasted_iota(jnp.int32, (ATT_BLOCK, LANES), 1)

        def tile(slot, qb, q0, q, it, first):
            k0, _, z, mask, n = _att_tile(q, k_ref, q0, qb, it, scale)
            run = jnp.zeros((ATT_BLOCK, LANES), F32) if first else run_s[slot]
            suffix, run_next = _suffix_in_tile(n, upper, run)
            w = jnp.where(mask, jnp.exp(z + n + suffix), 0.0)
            pv = jnp.dot(w.astype(BF16), v_ref[pl.ds(k0, ATT_TILE), :], preferred_element_type=F32)
            if first:
                acc_s[slot] = pv
            else:
                acc_s[slot] += pv
                r_ref[pl.ds(q0, ATT_BLOCK), :] = jnp.where(lane == it, run, r_ref[pl.ds(q0, ATT_BLOCK), :])
            run_s[slot] = run_next
            return jnp.max(run_next) >= ATT_UNDERFLOW

        def pair_loop(p, _):
            blocks = []
            for slot in range(2):
                qb = 2 * p + slot
                q0 = pl.multiple_of(qb * ATT_BLOCK, ATT_BLOCK)
                r_ref[pl.ds(q0, ATT_BLOCK), :] = jnp.where(lane == 0, 0.0, ATT_UNVISITED)
                blocks.append((qb, q0, q_ref[pl.ds(q0, ATT_BLOCK), :]))
            go = [tile(slot, *blocks[slot], 0, True) for slot in range(2)]
            for slot in range(2):
                qb, q0, q = blocks[slot]
                n_tiles = (qb + ATT_GROUP) // ATT_GROUP
                lax.while_loop(lambda c: (c[0] < n_tiles) & c[1],
                               lambda c: (c[0] + 1, tile(slot, qb, q0, q, c[0], False)), (jnp.int32(1), go[slot]))
                o_ref[pl.ds(q0, ATT_BLOCK), :] = acc_s[slot].astype(o_ref.dtype)
            return 0

        lax.fori_loop(0, nq // 2, pair_loop, 0)

    return _carry_call(
        body, name=name, steps=heads,
        in_specs=[_head_spec(T, 0, heads), _head_spec(T, 1, heads), _head_spec(T, 2, heads)],
        out_specs=[pl.BlockSpec((T, ATT_HEAD_DIM), lambda h: (0, h)), pl.BlockSpec((None, T, LANES), lambda h: (h, 0, 0))],
        out_shape=[jax.ShapeDtypeStruct((T, D), BF16), jax.ShapeDtypeStruct((heads, T, LANES), F32)],
        scratch_shapes=[pltpu.VMEM((2, ATT_BLOCK, ATT_HEAD_DIM), F32), pltpu.VMEM((2, ATT_BLOCK, LANES), F32)],
        args=(qkv, qkv, qkv), jobs=jobs)


def _attention_bwd(name, qkv, do, rmat, jobs=()):
    _, T, D = qkv.shape
    heads = D // ATT_HEAD_DIM
    nq = T // ATT_BLOCK
    scale = 1.0 / math.sqrt(ATT_HEAD_DIM)

    def body(q_ref, k_ref, v_ref, do_ref, r_ref, dqkv_ref, dk_s, dv_s, dq_s, left_s):
        rr = lax.broadcasted_iota(jnp.int32, (ATT_BLOCK, ATT_BLOCK), 0)
        cc = lax.broadcasted_iota(jnp.int32, (ATT_BLOCK, ATT_BLOCK), 1)
        upper = jnp.where(rr > cc, 1.0, 0.0).astype(BF16)
        lower = jnp.where(rr < cc, 1.0, 0.0).astype(BF16)
        lane = lax.broadcasted_iota(jnp.int32, (ATT_BLOCK, LANES), 1)
        dk_s[...] = jnp.zeros_like(dk_s)
        dv_s[...] = jnp.zeros_like(dv_s)

        def tile(slot, qb, q0, q, dov, it, first):
            k0, kt, z, mask, n = _att_tile(q, k_ref, q0, qb, it, scale)
            vt = v_ref[pl.ds(k0, ATT_TILE), :]
            run = jnp.sum(jnp.where(lane == it, r_ref[pl.ds(q0, ATT_BLOCK), :], 0.0), axis=1, keepdims=True)
            suffix, _ = _suffix_in_tile(n, upper, run)
            s = z + n
            w = jnp.where(mask, jnp.exp(s + suffix), 0.0)
            e = w * lax.dot_general(dov, vt, DIMS_NT, preferred_element_type=F32)
            es = [jnp.sum(_sub(e, g), axis=1, keepdims=True) for g in range(ATT_GROUP)]
            pre = [jnp.zeros((ATT_BLOCK, LANES), F32) if first else left_s[slot]]
            for g in range(ATT_GROUP):
                pre.append(pre[g] + es[g])
            before = _unstack_rows(_split_dot(_stack_rows(e), lower), pre)
            sig = jnp.exp(s)
            dz = (jnp.where(mask, e * (1.0 - sig) - before * sig, 0.0) * scale).astype(BF16)
            dq = jnp.dot(dz, kt, preferred_element_type=F32)
            if first:
                dq_s[slot] = dq
            else:
                dq_s[slot] += dq
            dk_s[pl.ds(k0, ATT_TILE), :] += lax.dot_general(dz, q, DIMS_TN, preferred_element_type=F32)
            dv_s[pl.ds(k0, ATT_TILE), :] += lax.dot_general(w.astype(BF16), dov, DIMS_TN, preferred_element_type=F32)
            left_s[slot] = pre[ATT_GROUP]

        def pair_loop(p, _):
            blocks = []
            for slot in range(2):
                qb = 2 * p + slot
                q0 = pl.multiple_of(qb * ATT_BLOCK, ATT_BLOCK)
                n_tiles = (qb + ATT_GROUP) // ATT_GROUP
                seen = ((jnp.max(r_ref[pl.ds(q0, ATT_BLOCK), :], axis=0, keepdims=True) > 0.5 * ATT_UNVISITED)
                        & (lane[0:1, :] < n_tiles))
                n_visited = jnp.maximum(jnp.sum(jnp.where(seen, 1.0, 0.0)).astype(jnp.int32), 1)
                blocks.append((qb, q0, q_ref[pl.ds(q0, ATT_BLOCK), :], do_ref[pl.ds(q0, ATT_BLOCK), :], n_visited))
            for slot in range(2):
                qb, q0, q, dov, n_visited = blocks[slot]
                tile(slot, qb, q0, q, dov, n_visited - 1, True)
            for slot in range(2):
                qb, q0, q, dov, n_visited = blocks[slot]
                lax.fori_loop(1, n_visited, lambda j, c: (tile(slot, qb, q0, q, dov, n_visited - 1 - j, False), c)[1], 0)
                dqkv_ref[0, pl.ds(q0, ATT_BLOCK), :] = dq_s[slot].astype(dqkv_ref.dtype)
            return 0

        lax.fori_loop(0, nq // 2, pair_loop, 0)
        dqkv_ref[1, :, :] = dk_s[...].astype(dqkv_ref.dtype)
        dqkv_ref[2, :, :] = dv_s[...].astype(dqkv_ref.dtype)

    return _carry_call(
        body, name=name, steps=heads,
        in_specs=[_head_spec(T, 0, heads), _head_spec(T, 1, heads), _head_spec(T, 2, heads),
                  pl.BlockSpec((T, ATT_HEAD_DIM), lambda h: (0, h)), pl.BlockSpec((None, T, LANES), lambda h: (h, 0, 0))],
        out_specs=[pl.BlockSpec((3, T, ATT_HEAD_DIM), lambda h: (0, 0, h))],
        out_shape=[jax.ShapeDtypeStruct((3, T, D), BF16)],
        scratch_shapes=[pltpu.VMEM((T, ATT_HEAD_DIM), F32), pltpu.VMEM((T, ATT_HEAD_DIM), F32),
                        pltpu.VMEM((2, ATT_BLOCK, ATT_HEAD_DIM), F32), pltpu.VMEM((2, ATT_BLOCK, LANES), F32)],
        args=(qkv, qkv, qkv, do, rmat), jobs=jobs)


def _block_diag_pairs(w):
    h = w.shape[0]
    wp = w.reshape(h // 2, 2, RG_HEAD_DIM, RG_HEAD_DIM)
    z = jnp.zeros_like(wp[:, 0])
    top = jnp.concatenate([wp[:, 0], z], axis=2)
    bot = jnp.concatenate([z, wp[:, 1]], axis=2)
    return jnp.concatenate([top, bot], axis=1)


def _diag_pairs(g):
    n = g.shape[0]
    a = g[:, :RG_HEAD_DIM, :RG_HEAD_DIM]
    b = g[:, RG_HEAD_DIM:, RG_HEAD_DIM:]
    return jnp.stack([a, b], axis=1).reshape(2 * n, RG_HEAD_DIM, RG_HEAD_DIM)


class _Weights:
    def __init__(self, full, shards=None, plan=None):
        self.full, self.shards, self.plan = dict(full), shards or {}, plan or {}
        self.partial, self.rows = {}, {}

    def __getitem__(self, name):
        return self.full[name]

    def jobs(self, call):
        return [_gather_job(self.shards[n], self.partial.get(n), lo, hi, parts)
                for n, lo, hi, parts in self.plan.get(call, ())]

    def deliver(self, call, outs):
        for (n, lo, hi, parts), g in zip(self.plan.get(call, ()), outs):
            self.partial[n] = g
            self.rows[n] = self.rows.get(n, 0) + hi - lo
            if self.rows[n] == parts:
                self.full[n] = _gathered_layout(n, g)


def _gathered_layout(name, g):
    if name in ("w_in", "w_qkv", "w_out", "w_o"):
        return g.reshape(g.shape[0] * g.shape[1], g.shape[2])
    return g


class _Grads:
    def __init__(self, lands=None, plan=None):
        self.lands, self.plan = dict(lands) if lands else None, plan or {}
        self.ready, self.sent = {}, {}

    def put(self, name, arr):
        self.ready[name] = arr

    def jobs(self, call):
        if self.lands is None:
            return []
        return [_exchange_job(self.ready[n], self.lands[n], lo, hi, parts) for n, lo, hi, parts in self.plan.get(call, ())]

    def deliver(self, call, outs):
        for (n, lo, hi, parts), o in zip(self.plan.get(call, ()), outs):
            assert self.sent.get(n, (0, parts)) == (lo, parts), (call, n)
            self.lands[n] = o
            self.sent[n] = (hi, parts)

    def flush(self, name):
        if self.lands is None:
            return
        rest = []
        for n in self.ready:
            lo, parts = self.sent.get(n, (0, 1))
            if lo < parts:
                rest.append((n, lo, parts, parts))
        if rest:
            outs = _run_jobs(name, [_exchange_job(self.ready[n], self.lands[n], lo, hi, parts) for n, lo, hi, parts in rest])
            for (n, _, hi, parts), o in zip(rest, outs):
                self.lands[n] = o
                self.sent[n] = (hi, parts)


def _mlp_fwd(tag, h, wts, run):
    T, D = h.shape
    w_up = wts["up" + tag]
    fb = w_up.shape[2]
    F = fb * N_DEV
    tm, tn, tk = _tile(T, 1024), _tile(fb, 1024), _tile(D, MM_TK)
    nb = fb // tn

    def up_epilogue(u):
        r = jnp.maximum(u, 0.0)
        return u, r * r

    o_spec = pl.BlockSpec((tm, tn), lambda i, j, k: (i, j))
    u, act = run(
        _matmul, f"mlp_up_l{tag}",
        [(h, pl.BlockSpec((tm, tk), lambda i, j, k: (i, k))),
         (w_up, pl.BlockSpec((None, tk, tn), lambda i, j, k: (j // nb, k, j % nb)))],
        [(jax.ShapeDtypeStruct((T, F), BF16), o_spec), (jax.ShapeDtypeStruct((T, F), BF16), o_spec)],
        (T // tm, F // tn, D // tk), DIMS_NN, (tm, tn), up_epilogue, n_main=2)
    w_down = wts["down" + tag].reshape(F, D)
    m = run(_mm_nn, f"mlp_down_l{tag}", act, w_down, BF16)
    return u, act, m


def _mlp_bwd(tag, h, u, act, dm, wts, grads, run):
    T, D = h.shape
    w_up, w_down = wts["up" + tag], wts["down" + tag]
    fb = w_up.shape[2]
    F = fb * N_DEV
    grads.put("down" + tag, run(_mm_tn, f"mlp_down_dw_l{tag}", act, dm, BF16).reshape(N_DEV, fb, D))
    tm, tn, tk = _tile(T, 1024), _tile(fb, 1024), _tile(D, MM_TK)
    nb = fb // tn
    o_spec = pl.BlockSpec((tm, tn), lambda i, j, k: (i, j))
    du = run(
        _matmul, f"mlp_down_dx_l{tag}",
        [(dm, pl.BlockSpec((tm, tk), lambda i, j, k: (i, k))),
         (w_down, pl.BlockSpec((None, tn, tk), lambda i, j, k: (j // nb, j % nb, k))),
         (u, o_spec)],
        [(jax.ShapeDtypeStruct((T, F), BF16), o_spec)],
        (T // tm, F // tn, D // tk), DIMS_NT, (tm, tn),
        lambda r, uv: (r * (2.0 * jnp.maximum(uv.astype(F32), 0.0)),))
    grads.put("up" + tag, run(_mm_tn, f"mlp_up_dw_l{tag}", h, du, BF16, out_blocks=N_DEV))
    tn2 = _tile(D, 1024)
    pair = 2 if MM_TK >= 2 * fb else 1
    return run(
        _matmul, f"mlp_up_dx_l{tag}",
        [(du, pl.BlockSpec((tm, pair * fb), lambda i, j, k: (i, k))),
         (w_up, pl.BlockSpec((pair, tn2, fb), lambda i, j, k: (k, j, 0)))],
        [(jax.ShapeDtypeStruct((T, D), BF16), pl.BlockSpec((tm, tn2), lambda i, j, k: (i, j)))],
        (T // tm, D // tn2, N_DEV // pair), DIMS_NT, (tm, tn2), None)


def _local_step(x, target, gains, conv_a, conv_b, conv_b_bias, rg_w_a, rg_b_a, rg_w_x, rg_b_x, rg_lambda, wts, grads):
    T, D = x.shape
    g = lambda l, i: gains[l, i][None, :]
    wa_p = _block_diag_pairs(rg_w_a).astype(BF16)
    wx_p = _block_diag_pairs(rg_w_x).astype(BF16)

    def run(fn, name, *args, n_main=1, **kw):
        jw, jg = wts.jobs(name), grads.jobs(name)
        res = fn(name, *args, jobs=jw + jg, **kw)
        main, jo = res[:n_main], res[n_main:]
        wts.deliver(name, jo[:len(jw)])
        grads.deliver(name, jo[len(jw):])
        return main[0] if n_main == 1 else main

    h0 = run(_norm_fwd, "norm_in", x, g(0, 0))
    proj = run(_mm_nt, "w_in_fwd", h0, wts["w_in"], F32, out_seg=5)
    y = run(_mixer_b_fwd, "mixer_b_fwd", proj, conv_b, conv_b_bias, wa_p, rg_b_a, wx_p, rg_b_x, rg_lambda,
            _mixer_a_fwd(proj, conv_a))
    mix0 = run(_mm_nn, "w_out_fwd", y, wts["w_out"], BF16, a_seg=2)
    x1, h1 = run(_resid_norm, "resid_mix0", x, mix0, g(0, 1), g(0, 2), n_main=2)
    u0, act0, m0 = _mlp_fwd("0", h1, wts, run)
    x2, h2 = run(_resid_norm, "resid_mlp0", x1, m0, g(0, 3), g(1, 0), n_main=2)
    qkv = run(_mm_nt, "w_qkv_fwd", h2, wts["w_qkv"], BF16, out_seg=3)
    o, rmat = run(_attention_fwd, "attention_fwd", qkv, n_main=2)
    mix1 = run(_mm_nn, "w_o_fwd", o, wts["w_o"], BF16)
    x3, h3 = run(_resid_norm, "resid_mix1", x2, mix1, g(1, 1), g(1, 2), n_main=2)
    u1, act1, m1 = _mlp_fwd("1", h3, wts, run)
    dx4, sq = _final_loss("loss", x3, m1, g(1, 3), target)

    dm1, dg13 = _norm_bwd("norm_bwd_m1", m1, g(1, 3), dx4, None, BF16)
    dh3 = _mlp_bwd("1", h3, u1, act1, dm1, wts, grads, run)
    dx3, dg12 = _norm_bwd("norm_bwd_x3", x3, g(1, 2), dh3, dx4, F32)
    dmix1, dg11 = _norm_bwd("norm_bwd_mix1", mix1, g(1, 1), dx3, None, BF16)
    grads.put("w_o", run(_mm_tn, "w_o_dw", o, dmix1, BF16).reshape(N_DEV, D // N_DEV, D))
    do = run(_mm_nt, "w_o_dx", dmix1, wts["w_o"], BF16)
    dqkv = run(_attention_bwd, "attention_bwd", qkv, do, rmat)
    grads.put("w_qkv", run(_mm_tn, "w_qkv_dw", dqkv, h2, BF16, a_seg=3).reshape(N_DEV, 3 * D // N_DEV, D))
    dh2 = run(_mm_nn, "w_qkv_dx", dqkv, wts["w_qkv"], BF16, a_seg=3)
    dx2, dg10 = _norm_bwd("norm_bwd_x2", x2, g(1, 0), dh2, dx3, F32)
    dm0, dg03 = _norm_bwd("norm_bwd_m0", m0, g(0, 3), dx2, None, BF16)
    dh1 = _mlp_bwd("0", h1, u0, act0, dm0, wts, grads, run)
    dx1, dg02 = _norm_bwd("norm_bwd_x1", x1, g(0, 2), dh1, dx2, F32)
    dmix0, dg01 = _norm_bwd("norm_bwd_mix0", mix0, g(0, 1), dx1, None, BF16)
    grads.put("w_out", run(_mm_tn, "w_out_dw", y, dmix0, BF16, a_seg=2).reshape(N_DEV, D // N_DEV, D))
    dy = run(_mm_nt, "w_out_dx", dmix0, wts["w_out"], F32, out_seg=2)
    dproj_a, dconv_a = _mixer_a_bwd(proj, conv_a, dy)
    dproj, sm_b, dwa_p, dwx_p = run(_mixer_b_bwd, "mixer_b_bwd", proj, conv_b, conv_b_bias, wa_p, rg_b_a, wx_p, rg_b_x,
                                    rg_lambda, dy, dproj_a, n_main=4)
    grads.put("w_in", run(_mm_tn, "w_in_dw", dproj, h0, BF16, a_seg=5).reshape(N_DEV, 5 * D // (2 * N_DEV), D))
    dh0 = run(_mm_nn, "w_in_dx", dproj, wts["w_in"], BF16, a_seg=5)
    dx0, dg00 = run(_norm_bwd, "norm_bwd_x0", x, g(0, 0), dh0, dx1, F32, n_main=2)

    C = D // 2
    lanes_to_vec = lambda t, row: t[:, row, :].reshape(1, C)
    small = {
        "norm_gains": jnp.concatenate([dg00, dg01, dg02, dg03, dg10, dg11, dg12, dg13], axis=0).reshape(2, 4, D),
        "conv_a": jnp.transpose(dconv_a[:, :3, :], (1, 0, 2)).reshape(3, C),
        "conv_b": jnp.transpose(sm_b[:, :4, :], (1, 0, 2)).reshape(4, C),
        "conv_b_bias": lanes_to_vec(sm_b, _ROW_BIAS),
        "rg_w_a": _diag_pairs(dwa_p),
        "rg_b_a": lanes_to_vec(sm_b, _ROW_BA),
        "rg_w_x": _diag_pairs(dwx_p),
        "rg_b_x": lanes_to_vec(sm_b, _ROW_BX),
        "rg_lambda": lanes_to_vec(sm_b, _ROW_LAM),
    }
    return sq[0, 0], dx0, small


def _my_index():
    return 4 * lax.axis_index("x") + 2 * lax.axis_index("y") + lax.axis_index("c")


def _peers():
    x, y, c = lax.axis_index("x"), lax.axis_index("y"), lax.axis_index("c")
    out = []
    for k in range(1, N_DEV):
        px = x ^ ((k >> 2) & 1)
        py = y ^ ((k >> 1) & 1)
        pc = c ^ (k & 1)
        out.append(((px, py, pc), 4 * px + 2 * py + pc))
    return out


GATHER_FIRST = ("w_in",)
GATHER_PLAN = {
    "norm_in": (("w_out", 0, 1, 2),),
    "w_in_fwd": (("w_out", 1, 2, 2), ("up0", 0, 2, 4)),
    "mixer_b_fwd": (("up0", 2, 4, 4),),
    "w_out_fwd": (("down0", 0, 1, 4),),
    "resid_mix0": (("down0", 1, 2, 4),),
    "mlp_up_l0": (("down0", 2, 4, 4),),
    "mlp_down_l0": (("w_qkv", 0, 1, 1),),
    "resid_mlp0": (("up1", 0, 1, 4),),
    "w_qkv_fwd": (("w_o", 0, 1, 1), ("up1", 1, 2, 4)),
    "attention_fwd": (("up1", 2, 4, 4), ("down1", 0, 2, 4)),
    "mlp_up_l1": (("down1", 2, 4, 4),),
}
EXCHANGE_PLAN = {
    "mlp_down_dx_l1": (("down1", 0, 3, 8),), "mlp_up_dw_l1": (("down1", 3, 6, 8),),
    "mlp_up_dx_l1": (("down1", 6, 8, 8), ("up1", 0, 1, 8)),
    "w_o_dw": (("up1", 1, 2, 8),), "w_o_dx": (("up1", 2, 3, 8),),
    "attention_bwd": (("up1", 3, 8, 8), ("w_o", 0, 1, 1)),
    "w_qkv_dx": (("w_qkv", 0, 1, 2),), "mlp_down_dw_l0": (("w_qkv", 1, 2, 2),),
    "mlp_down_dx_l0": (("down0", 0, 3, 8),), "mlp_up_dw_l0": (("down0", 3, 6, 8),),
    "mlp_up_dx_l0": (("down0", 6, 8, 8), ("up0", 0, 1, 8)),
    "w_out_dw": (("up0", 1, 2, 8),), "w_out_dx": (("up0", 2, 3, 8),),
    "mixer_b_bwd": (("up0", 3, 7, 8),),
    "w_in_dw": (("up0", 7, 8, 8), ("w_out", 0, 1, 2)),
    "w_in_dx": (("w_out", 1, 2, 2), ("w_in", 0, 1, 4)),
    "norm_bwd_x0": (("w_in", 1, 2, 4),),
    "adamw_mlp_w_down": (("w_in", 2, 3, 4),), "adamw_mlp_w_up": (("w_in", 3, 4, 4),),
}


def _all_gather(name, shards):
    n = len(shards)

    def body(*refs):
        srcs, dsts = refs[:n], refs[n:2 * n]
        send_sems, recv_sems, local_sems = refs[2 * n:]
        me = _my_index()
        peers = _peers()
        copies = []
        for a in range(n):
            lc = pltpu.make_async_copy(srcs[a], dsts[a].at[me], local_sems.at[a])
            lc.start()
            copies.append(lc)
        remote = []
        for a in range(n):
            for k, (pos, _) in enumerate(peers):
                cp = pltpu.make_async_remote_copy(
                    src_ref=srcs[a], dst_ref=dsts[a].at[me], send_sem=send_sems.at[a, k], recv_sem=recv_sems.at[a, k],
                    device_id=pos, device_id_type=MESH)
                cp.start()
                remote.append(cp)
        for a in range(n):
            for k, (pos, idx) in enumerate(peers):
                pltpu.make_async_remote_copy(
                    src_ref=srcs[a], dst_ref=dsts[a].at[idx], send_sem=send_sems.at[a, k], recv_sem=recv_sems.at[a, k],
                    device_id=pos, device_id_type=MESH).wait_recv()
        for cp in remote:
            cp.wait_send()
        for lc in copies:
            lc.wait()

    return pl.pallas_call(
        body, name=name,
        in_specs=[_ANY] * n, out_specs=[_ANY] * n,
        out_shape=[jax.ShapeDtypeStruct((N_DEV,) + s.shape, s.dtype) for s in shards],
        scratch_shapes=[pltpu.SemaphoreType.DMA((n, N_DEV - 1)), pltpu.SemaphoreType.DMA((n, N_DEV - 1)),
                        pltpu.SemaphoreType.DMA((n,))],
    )(*shards)


def _job_sems():
    return [pltpu.SemaphoreType.DMA((N_DEV - 1,)), pltpu.SemaphoreType.DMA((N_DEV - 1,)), pltpu.SemaphoreType.DMA((1,))]


def _gather_job(shard, prev=None, lo=0, hi=1, parts=1):
    n = shard.shape[0] // parts
    assert n * parts == shard.shape[0]
    rows = pl.ds(lo * n, (hi - lo) * n)

    def ctx():
        x, y, c = lax.axis_index("x"), lax.axis_index("y"), lax.axis_index("c")
        chips = [(1 - x, y), (x, 1 - y), (1 - x, 1 - y)]
        return x, y, c, chips

    def idx(px, py, pc):
        return 4 * px + 2 * py + pc

    def copy(src, out, sems, k, block, to):
        return pltpu.make_async_remote_copy(
            src_ref=out.at[block, rows] if src is None else src.at[rows], dst_ref=out.at[block, rows],
            send_sem=sems[0].at[k], recv_sem=sems[1].at[k], device_id=to, device_id_type=MESH)

    def start(ins, outs, sems):
        x, y, c, chips = ctx()
        src, out = ins[0], outs[0]
        me = idx(x, y, c)
        pltpu.make_async_copy(src.at[rows], out.at[me, rows], sems[2].at[0]).start()
        copy(src, out, sems, 0, me, (x, y, 1 - c)).start()
        for j, (px, py) in enumerate(chips):
            copy(src, out, sems, 1 + j, me, (px, py, c)).start()

    def mid(ins, outs, sems):
        x, y, c, chips = ctx()
        out = outs[0]
        for j, (px, py) in enumerate(chips):
            copy(None, out, sems, 1 + j, idx(px, py, c), (x, y, c)).wait_recv()
            copy(None, out, sems, 4 + j, idx(px, py, c), (x, y, 1 - c)).start()

    def end(ins, outs, sems):
        x, y, c, chips = ctx()
        src, out = ins[0], outs[0]
        me = (x, y, c)
        copy(None, out, sems, 0, idx(x, y, 1 - c), me).wait_recv()
        for j, (px, py) in enumerate(chips):
            copy(None, out, sems, 4 + j, idx(px, py, 1 - c), me).wait_recv()
        for k in range(N_DEV - 1):
            copy(src, out, sems, k, idx(x, y, c), me).wait_send()
        pltpu.make_async_copy(src.at[rows], out.at[idx(x, y, c), rows], sems[2].at[0]).wait()

    out_shape = jax.ShapeDtypeStruct((N_DEV,) + shard.shape, shard.dtype)
    if prev is None:
        return _Job([shard], [out_shape], _job_sems(), start, mid, end)
    return _Job([shard, prev], [out_shape], _job_sems(), start, mid, end, alias={1: 0})


def _exchange_job(src, land, lo=0, hi=1, parts=1):
    n = src.shape[1] // parts
    assert n * parts == src.shape[1]

    def sl(ref, s):
        return ref.at[s, pl.ds(lo * n, (hi - lo) * n)]

    def start(ins, outs, sems):
        me = _my_index()
        pltpu.make_async_copy(sl(ins[0], me), sl(outs[0], me), sems[2].at[0]).start()
        for k, (pos, idx) in enumerate(_peers()):
            pltpu.make_async_remote_copy(
                src_ref=sl(ins[0], idx), dst_ref=sl(outs[0], me), send_sem=sems[0].at[k], recv_sem=sems[1].at[k],
                device_id=pos, device_id_type=MESH).start()

    def mid(ins, outs, sems):
        pass

    def end(ins, outs, sems):
        me = _my_index()
        for k, (pos, idx) in enumerate(_peers()):
            cp = pltpu.make_async_remote_copy(
                src_ref=sl(ins[0], idx), dst_ref=sl(outs[0], idx), send_sem=sems[0].at[k], recv_sem=sems[1].at[k],
                device_id=pos, device_id_type=MESH)
            cp.wait_recv()
            cp.wait_send()
        pltpu.make_async_copy(sl(ins[0], me), sl(outs[0], me), sems[2].at[0]).wait()

    return _Job([src, land], [jax.ShapeDtypeStruct(land.shape, land.dtype)], _job_sems(), start, mid, end, alias={1: 0})


def _adamw_math(w, g, m, v):
    m = ADAM_B1 * m + (1.0 - ADAM_B1) * g
    v = ADAM_B2 * v + (1.0 - ADAM_B2) * (g * g)
    m_hat = m / (1.0 - ADAM_B1 ** ADAM_STEP)
    v_hat = v / (1.0 - ADAM_B2 ** ADAM_STEP)
    delta = -ADAM_LR * (m_hat / (jnp.sqrt(v_hat) + ADAM_EPS) + ADAM_WD * w)
    return delta, m, v


def _sum_slots(ref):
    g = ref[0].astype(F32)
    for s in range(1, N_DEV):
        g = g + ref[s].astype(F32)
    return g


def _adamw_big(name, lands, w, m, v, jobs=(), transposed=False):
    L, R, C = w.shape
    assert len(lands) == L
    tr = _tile(R, max(LANES, (256 * 1024) // C))
    nr = R // tr

    def body(*refs):
        l_refs = refs[:L]
        w_ref, m_ref, v_ref, g_ref, d_ref, nm_ref, nv_ref = refs[L:]
        for li in range(L):
            @pl.when(pl.program_id(0) // nr == li)
            def _(li=li):
                g = _sum_slots(l_refs[li])
                if transposed:
                    g = g.T
                d, nm, nv = _adamw_math(w_ref[...], g, m_ref[...], v_ref[...])
                g_ref[...] = g
                d_ref[...] = d
                nm_ref[...] = nm
                nv_ref[...] = nv

    def land_spec(li):
        if transposed:
            return pl.BlockSpec((N_DEV, C, tr), lambda s: (0, 0, jnp.where(s // nr == li, s % nr, 0)))
        return pl.BlockSpec((N_DEV, tr, C), lambda s: (0, jnp.where(s // nr == li, s % nr, 0), 0))

    row = pl.BlockSpec((None, tr, C), lambda s: (s // nr, s % nr, 0))
    return _carry_call(
        body, name=name, steps=L * nr, in_specs=[land_spec(li) for li in range(L)] + [row, row, row],
        out_specs=[row] * 4, out_shape=[jax.ShapeDtypeStruct((L, R, C), F32)] * 4, scratch_shapes=[],
        args=(*lands, w, m, v), jobs=jobs)


def _sum8(name, slots):
    _, R, C = slots.shape

    def body(s_ref, o_ref):
        o_ref[...] = _sum_slots(s_ref)

    return pl.pallas_call(body, name=name, out_shape=jax.ShapeDtypeStruct((R, C), F32))(slots)


def _adamw_small(name, g, w, m, v):
    def body(g_ref, w_ref, m_ref, v_ref, d_ref, nm_ref, nv_ref):
        d, nm, nv = _adamw_math(w_ref[...], g_ref[...], m_ref[...], v_ref[...])
        d_ref[...] = d
        nm_ref[...] = nm
        nv_ref[...] = nv

    return pl.pallas_call(body, name=name, out_shape=[jax.ShapeDtypeStruct(w.shape, F32)] * 3)(g, w, m, v)


def _pack_rows(arrs):
    parts, spans, r0 = [], [], 0
    for a in arrs:
        flat = a.astype(F32).reshape(-1)
        rows = -(-flat.shape[0] // LANES)
        rows = -(-rows // SUBLANES) * SUBLANES
        flat = jnp.pad(flat, (0, rows * LANES - flat.shape[0]))
        parts.append(flat.reshape(rows, LANES))
        spans.append((r0, rows, a.shape))
        r0 += rows
    return jnp.concatenate(parts, axis=0), spans


def _unpack_rows(buf, span):
    r0, rows, shape = span
    n = math.prod(shape)
    return buf[..., r0:r0 + rows, :].reshape(buf.shape[:-2] + (rows * LANES,))[..., :n].reshape(buf.shape[:-2] + shape)


def _col_blocks(w, n_blocks):
    K, N = w.shape
    return jnp.transpose(w.reshape(K, n_blocks, N // n_blocks), (1, 0, 2))


def _from_col_blocks(wb):
    B, K, n = wb.shape
    return jnp.transpose(wb, (1, 0, 2)).reshape(K, B * n)


def kernel(x, norm_gains, hyb_w_in, hyb_conv_a, hyb_conv_b, hyb_conv_b_bias, hyb_rg_w_a, hyb_rg_b_a, hyb_rg_w_x, hyb_rg_b_x, hyb_rg_lambda, hyb_w_out, sb_w_qkv, sb_w_o, mlp_w_up, mlp_w_down, loss_target, m_norm_gains, m_hyb_w_in, m_hyb_conv_a, m_hyb_conv_b, m_hyb_conv_b_bias, m_hyb_rg_w_a, m_hyb_rg_b_a, m_hyb_rg_w_x, m_hyb_rg_b_x, m_hyb_rg_lambda, m_hyb_w_out, m_sb_w_qkv, m_sb_w_o, m_mlp_w_up, m_mlp_w_down, v_norm_gains, v_hyb_w_in, v_hyb_conv_a, v_hyb_conv_b, v_hyb_conv_b_bias, v_hyb_rg_w_a, v_hyb_rg_b_a, v_hyb_rg_w_x, v_hyb_rg_b_x, v_hyb_rg_lambda, v_hyb_w_out, v_sb_w_qkv, v_sb_w_o, v_mlp_w_up, v_mlp_w_down):
    T, D = x.shape[1], x.shape[2]
    me = _my_index()

    small_shards, small_spans = _pack_rows([norm_gains, hyb_conv_a[0], hyb_conv_b[0]])
    (small_all,) = _all_gather("gather_small", [small_shards])
    gains_b = _unpack_rows(small_all, small_spans[0])
    gains = jnp.transpose(gains_b, (1, 2, 0, 3)).reshape(2, 4, D)
    conv_a = _from_col_blocks(_unpack_rows(small_all, small_spans[1]))
    conv_b = _from_col_blocks(_unpack_rows(small_all, small_spans[2]))

    shards = {"w_in": hyb_w_in[0].T, "w_out": hyb_w_out[0], "w_qkv": sb_w_qkv[0].T, "w_o": sb_w_o[0],
              "up0": mlp_w_up[0], "up1": mlp_w_up[1], "down0": mlp_w_down[0], "down1": mlp_w_down[1]}
    shards = {n: s.astype(BF16) for n, s in shards.items()}
    first = _run_jobs("gather_first", [_gather_job(shards[n]) for n in GATHER_FIRST])
    wts = _Weights({n: _gathered_layout(n, g) for n, g in zip(GATHER_FIRST, first)}, shards, GATHER_PLAN)
    assert not set(GATHER_FIRST) & {e[0] for es in GATHER_PLAN.values() for e in es}
    grads_big = _Grads({n: lax.empty((N_DEV,) + s.shape, BF16) for n, s in shards.items()}, EXCHANGE_PLAN)

    sq, grad_x, small = _local_step(
        x[0], loss_target[0], gains, conv_a, conv_b, hyb_conv_b_bias, hyb_rg_w_a[0], hyb_rg_b_a, hyb_rg_w_x[0],
        hyb_rg_b_x, hyb_rg_lambda, wts, grads_big)


    names = ["norm_gains", "hyb_w_in", "hyb_conv_a", "hyb_conv_b", "hyb_conv_b_bias", "hyb_rg_w_a", "hyb_rg_b_a",
             "hyb_rg_w_x", "hyb_rg_b_x", "hyb_rg_lambda", "hyb_w_out", "sb_w_qkv", "sb_w_o", "mlp_w_up", "mlp_w_down"]
    params = dict(zip(names, [norm_gains, hyb_w_in, hyb_conv_a, hyb_conv_b, hyb_conv_b_bias, hyb_rg_w_a, hyb_rg_b_a,
                              hyb_rg_w_x, hyb_rg_b_x, hyb_rg_lambda, hyb_w_out, sb_w_qkv, sb_w_o, mlp_w_up, mlp_w_down]))
    moms = dict(zip(names, [m_norm_gains, m_hyb_w_in, m_hyb_conv_a, m_hyb_conv_b, m_hyb_conv_b_bias, m_hyb_rg_w_a,
                            m_hyb_rg_b_a, m_hyb_rg_w_x, m_hyb_rg_b_x, m_hyb_rg_lambda, m_hyb_w_out, m_sb_w_qkv,
                            m_sb_w_o, m_mlp_w_up, m_mlp_w_down]))
    vars_ = dict(zip(names, [v_norm_gains, v_hyb_w_in, v_hyb_conv_a, v_hyb_conv_b, v_hyb_conv_b_bias, v_hyb_rg_w_a,
                             v_hyb_rg_b_a, v_hyb_rg_w_x, v_hyb_rg_b_x, v_hyb_rg_lambda, v_hyb_w_out, v_sb_w_qkv,
                             v_sb_w_o, v_mlp_w_up, v_mlp_w_down]))
    grads, deltas, new_m, new_v = {}, {}, {}, {}

    big_lands = {"mlp_w_down": ["down0", "down1"], "mlp_w_up": ["up0", "up1"], "sb_w_qkv": ["w_qkv"], "sb_w_o": ["w_o"],
                 "hyb_w_out": ["w_out"], "hyb_w_in": ["w_in"]}
    for nm, keys in big_lands.items():
        call = f"adamw_{nm}"
        if nm == "hyb_w_in":
            grads_big.flush("exchange_grads")
        jobs = grads_big.jobs(call)
        assert not {k for k in keys} & {e[0] for e in EXCHANGE_PLAN.get(call, ())}
        outs = _adamw_big(call, [grads_big.lands[k] for k in keys], params[nm], moms[nm], vars_[nm], jobs=jobs,
                          transposed=nm in ("hyb_w_in", "sb_w_qkv"))
        grads[nm], deltas[nm], new_m[nm], new_v[nm] = outs[:4]
        grads_big.deliver(call, outs[4:])

    small_names = ["norm_gains", "hyb_conv_a", "hyb_conv_b", "hyb_conv_b_bias", "hyb_rg_w_a", "hyb_rg_b_a",
                   "hyb_rg_w_x", "hyb_rg_b_x", "hyb_rg_lambda"]
    small_keys = ["norm_gains", "conv_a", "conv_b", "conv_b_bias", "rg_w_a", "rg_b_a", "rg_w_x", "rg_b_x", "rg_lambda"]
    sg_buf, sg_spans = _pack_rows([small[k] for k in small_keys] + [sq.reshape(1)])
    (sg_all,) = _all_gather("gather_small_grads", [sg_buf])
    sg_sum = _sum8("sum_small_grads", sg_all)
    full = {nm: _unpack_rows(sg_sum, sp) for nm, sp in zip(small_names, sg_spans)}
    loss = _unpack_rows(sg_sum, sg_spans[-1])[0] * (0.5 / D)
    cb = (D // 2) // N_DEV
    small_grads = {
        "norm_gains": lax.dynamic_slice_in_dim(full["norm_gains"], me * (D // N_DEV), D // N_DEV, axis=2),
        "hyb_conv_a": lax.dynamic_slice_in_dim(full["hyb_conv_a"], me * cb, cb, axis=1)[None],
        "hyb_conv_b": lax.dynamic_slice_in_dim(full["hyb_conv_b"], me * cb, cb, axis=1)[None],
        "hyb_conv_b_bias": full["hyb_conv_b_bias"],
        "hyb_rg_w_a": full["hyb_rg_w_a"][None],
        "hyb_rg_b_a": full["hyb_rg_b_a"],
        "hyb_rg_w_x": full["hyb_rg_w_x"][None],
        "hyb_rg_b_x": full["hyb_rg_b_x"],
        "hyb_rg_lambda": full["hyb_rg_lambda"],
    }
    pk = lambda d: _pack_rows([d[nm] for nm in small_names])
    g_buf, spans = pk(small_grads)
    w_buf, _ = pk(params)
    m_buf, _ = pk(moms)
    v_buf, _ = pk(vars_)
    d_buf, nm_buf, nv_buf = _adamw_small("adamw_small", g_buf, w_buf, m_buf, v_buf)
    for nm, sp in zip(small_names, spans):
        grads[nm] = small_grads[nm]
        deltas[nm], new_m[nm], new_v[nm] = _unpack_rows(d_buf, sp), _unpack_rows(nm_buf, sp), _unpack_rows(nv_buf, sp)

    return (loss, grad_x[None], *[grads[n] for n in names], *[deltas[n] for n in names],
            *[new_m[n] for n in names], *[new_v[n] for n in names])
```

```python
import functools
import math

import jax
import jax.numpy as jnp
from jax import lax
from jax.experimental import pallas as pl
from jax.experimental.pallas import tpu as pltpu

F32 = jnp.float32
BF16 = jnp.bfloat16

NORM_EPS = 1e-6
LRU_C = 8.0
ATT_HEAD_DIM = 128
RG_HEAD_DIM = 64
LANES = 128
SUBLANES = 8
N_DEV = 8
ADAM_LR = 0.001
ADAM_B1 = 0.9
ADAM_B2 = 0.999
ADAM_EPS = 1e-08
ADAM_WD = 0.01
ADAM_STEP = 10
VMEM_LIMIT = 56 * 1024 * 1024
MM_TK = 2048
MM_TM = 2048
MM_TK_TOKENS = 4096
MESH = pl.DeviceIdType.MESH


def _tile(n, pref):
    if n <= pref:
        return n
    t = (pref // LANES) * LANES
    while t > LANES and n % t:
        t -= LANES
    assert n % t == 0, (n, pref)
    return t


def _params(sem):
    return pltpu.CompilerParams(dimension_semantics=sem, vmem_limit_bytes=VMEM_LIMIT)


DIMS_NN = (((1,), (0,)), ((), ()))
DIMS_NT = (((1,), (1,)), ((), ()))
DIMS_TN = (((0,), (0,)), ((), ()))


_ANY = pl.BlockSpec(memory_space=pl.ANY)


class _Job:
    def __init__(self, ins, outs, sems, start, mid, end, alias=None):
        self.ins, self.outs, self.sems = ins, outs, sems
        self.start, self.mid, self.end = start, mid, end
        self.alias = alias or {}


def _job_plumbing(jobs, n_in, n_out):
    j_ins = [a for jb in jobs for a in jb.ins]
    j_outs = [o for jb in jobs for o in jb.outs]
    j_sems = [s for jb in jobs for s in jb.sems]
    aliases, pi, po = {}, 0, 0
    for jb in jobs:
        for i_in, i_out in jb.alias.items():
            aliases[n_in + pi + i_in] = n_out + po + i_out
        pi += len(jb.ins)
        po += len(jb.outs)
    return j_ins, j_outs, j_sems, aliases


def _job_phase(jobs, which, jin, jout, jsem):
    pi = po = ps = 0
    for jb in jobs:
        getattr(jb, which)(jin[pi:pi + len(jb.ins)], jout[po:po + len(jb.outs)], jsem[ps:ps + len(jb.sems)])
        pi, po, ps = pi + len(jb.ins), po + len(jb.outs), ps + len(jb.sems)


def _run_jobs(name, jobs):
    j_ins, j_outs, j_sems, aliases = _job_plumbing(jobs, 0, 0)
    n_ji, n_jo = len(j_ins), len(j_outs)

    def body(*refs):
        jin, jout, jsem = refs[:n_ji], refs[n_ji:n_ji + n_jo], refs[n_ji + n_jo:]
        for which in ("start", "mid", "end"):
            _job_phase(jobs, which, jin, jout, jsem)

    return pl.pallas_call(body, name=name, in_specs=[_ANY] * n_ji, out_specs=[_ANY] * n_jo, out_shape=j_outs,
                          scratch_shapes=j_sems, input_output_aliases=aliases)(*j_ins)


def _carry_call(body, *, name, steps, in_specs, out_specs, out_shape, scratch_shapes, args, jobs=(), aliases=None):
    n_in, n_out, n_sc = len(in_specs), len(out_shape), len(scratch_shapes)
    j_ins, j_outs, j_sems, j_aliases = _job_plumbing(jobs, n_in, n_out)
    n_ji, n_jo = len(j_ins), len(j_outs)

    def wrapped(*refs):
        ins, jin = refs[:n_in], refs[n_in:n_in + n_ji]
        o0 = n_in + n_ji
        outs, jout = refs[o0:o0 + n_out], refs[o0 + n_out:o0 + n_out + n_jo]
        s0 = o0 + n_out + n_jo
        scratch, jsem = refs[s0:s0 + n_sc], refs[s0 + n_sc:]
        step = pl.program_id(0)
        if jobs:
            pl.when(step == 0)(lambda: _job_phase(jobs, "start", jin, jout, jsem))
            pl.when(step == (4 * steps) // 5)(lambda: _job_phase(jobs, "mid", jin, jout, jsem))
        body(*ins, *outs, *scratch)
        if jobs:
            pl.when(step == steps - 1)(lambda: _job_phase(jobs, "end", jin, jout, jsem))

    return pl.pallas_call(
        wrapped, name=name, grid=(steps,),
        in_specs=list(in_specs) + [_ANY] * n_ji, out_specs=list(out_specs) + [_ANY] * n_jo,
        out_shape=list(out_shape) + j_outs, scratch_shapes=list(scratch_shapes) + j_sems,
        input_output_aliases={**(aliases or {}), **j_aliases},
        compiler_params=_params(("arbitrary",) if jobs else ("parallel",)))(*args, *j_ins)


def _matmul(name, ins, outs, grid, dims, acc_shape, epilogue=None, jobs=()):
    n_in, n_out, nk = len(ins), len(outs), grid[2]
    j_ins, j_outs, j_sems, aliases = _job_plumbing(jobs, n_in, n_out)
    n_ji, n_jo = len(j_ins), len(j_outs)
    total = grid[0] * grid[1] * grid[2]
    n_acc = 0 if nk == 1 else 1

    def body(*refs):
        a_ref, b_ref = refs[0], refs[1]
        extras = refs[2:n_in]
        jin = refs[n_in:n_in + n_ji]
        out_refs = refs[n_in + n_ji:n_in + n_ji + n_out]
        jout = refs[n_in + n_ji + n_out:n_in + n_ji + n_out + n_jo]
        jsem = refs[n_in + n_ji + n_out + n_jo + n_acc:]
        k = pl.program_id(2)
        step = (pl.program_id(0) * grid[1] + pl.program_id(1)) * grid[2] + k
        if jobs:
            pl.when(step == 0)(lambda: _job_phase(jobs, "start", jin, jout, jsem))
            pl.when(step == (4 * total) // 5)(lambda: _job_phase(jobs, "mid", jin, jout, jsem))

        def finish(r):
            res = epilogue(r, *[e[...] for e in extras]) if epilogue is not None else (r,)
            for o, v in zip(out_refs, res):
                o[...] = v.astype(o.dtype)

        if len(b_ref.shape) == 2:
            prod = lax.dot_general(a_ref[...], b_ref[...], dims, preferred_element_type=F32)
        else:
            kb = a_ref.shape[1] // b_ref.shape[0]
            prod = sum(lax.dot_general(a_ref[:, g * kb:(g + 1) * kb], b_ref[g], dims, preferred_element_type=F32)
                       for g in range(b_ref.shape[0]))
        if nk == 1:
            finish(prod)
        else:
            acc = refs[n_in + n_ji + n_out + n_jo]

            @pl.when(k == 0)
            def _():
                acc[...] = prod

            @pl.when((k > 0) & (k < nk - 1))
            def _():
                acc[...] += prod

            @pl.when(k == nk - 1)
            def _():
                finish(acc[...] + prod)

        if jobs:
            pl.when(step == total - 1)(lambda: _job_phase(jobs, "end", jin, jout, jsem))

    sem = ("arbitrary",) * 3 if jobs else ("parallel", "parallel", "arbitrary")
    res = pl.pallas_call(
        body, name=name, grid=grid,
        in_specs=[s for _, s in ins] + [_ANY] * n_ji,
        out_specs=[s for _, s in outs] + [_ANY] * n_jo,
        out_shape=[s for s, _ in outs] + j_outs,
        scratch_shapes=[pltpu.VMEM(acc_shape, F32)] * n_acc + j_sems,
        input_output_aliases=aliases,
        compiler_params=_params(sem),
    )(*[a for a, _ in ins], *j_ins)
    return res


def _mm_nn(name, a, b, out_dtype, *, a_seg=None, out_seg=None, tm=MM_TM, tn=1024, tk=MM_TK, epilogue=None,
           extras=(), n_out=1, out_dtypes=None, jobs=()):
    if a_seg:
        _, M, ks = a.shape
        K = ks * a_seg
    else:
        M, K = a.shape
        ks = K
    N = b.shape[1]
    ns = N // out_seg if out_seg else N
    tm, tn, tk = _tile(M, tm), _tile(ns, tn), _tile(ks, tk)
    nks, nns = ks // tk, ns // tn
    grid = (M // tm, N // tn, K // tk)
    if a_seg:
        a_spec = pl.BlockSpec((None, tm, tk), lambda i, j, k: (k // nks, i, k % nks))
    else:
        a_spec = pl.BlockSpec((tm, tk), lambda i, j, k: (i, k))
    b_spec = pl.BlockSpec((tk, tn), lambda i, j, k: (k, j))
    if out_seg:
        o_spec = pl.BlockSpec((None, tm, tn), lambda i, j, k: (j // nns, i, j % nns))
        o_shape = (out_seg, M, ns)
    else:
        o_spec = pl.BlockSpec((tm, tn), lambda i, j, k: (i, j))
        o_shape = (M, N)
    dts = out_dtypes or [out_dtype] * n_out
    outs = [(jax.ShapeDtypeStruct(o_shape, dt), o_spec) for dt in dts]
    ins = [(a, a_spec), (b, b_spec)] + [(e, o_spec) for e in extras]
    return _matmul(name, ins, outs, grid, DIMS_NN, (tm, tn), epilogue, jobs)


def _mm_nt(name, a, b, out_dtype, *, a_seg=None, out_seg=None, tm=MM_TM, tn=1024, tk=MM_TK, epilogue=None, extras=(),
           jobs=()):
    if a_seg:
        _, M, ks = a.shape
        K = ks * a_seg
    else:
        M, K = a.shape
        ks = K
    N = b.shape[0]
    ns = N // out_seg if out_seg else N
    tm, tn, tk = _tile(M, tm), _tile(ns, tn), _tile(ks, tk)
    nks, nns = ks // tk, ns // tn
    grid = (M // tm, N // tn, K // tk)
    if a_seg:
        a_spec = pl.BlockSpec((None, tm, tk), lambda i, j, k: (k // nks, i, k % nks))
    else:
        a_spec = pl.BlockSpec((tm, tk), lambda i, j, k: (i, k))
    b_spec = pl.BlockSpec((tn, tk), lambda i, j, k: (j, k))
    if out_seg:
        o_spec = pl.BlockSpec((None, tm, tn), lambda i, j, k: (j // nns, i, j % nns))
        o_shape = (out_seg, M, ns)
    else:
        o_spec = pl.BlockSpec((tm, tn), lambda i, j, k: (i, j))
        o_shape = (M, N)
    outs = [(jax.ShapeDtypeStruct(o_shape, out_dtype), o_spec)]
    ins = [(a, a_spec), (b, b_spec)] + [(e, o_spec) for e in extras]
    return _matmul(name, ins, outs, grid, DIMS_NT, (tm, tn), epilogue, jobs)


def _mm_tn(name, a, b, out_dtype, *, a_seg=None, b_seg=None, out_blocks=None, tm=1024, tn=1024, tk=MM_TK_TOKENS,
           jobs=()):
    if a_seg:
        _, T, ms = a.shape
        M = ms * a_seg
    else:
        T, M = a.shape
        ms = M
    if b_seg:
        _, _, ns = b.shape
        N = ns * b_seg
    else:
        N = b.shape[1]
        ns = N
    nb_cols = N // out_blocks if out_blocks else N
    tm, tk = _tile(ms, tm), _tile(T, tk)
    tn = _tile(math.gcd(ns, nb_cols), tn)
    nms, nns, nbs = ms // tm, ns // tn, nb_cols // tn
    grid = (M // tm, N // tn, T // tk)
    if a_seg:
        a_spec = pl.BlockSpec((None, tk, tm), lambda i, j, k: (i // nms, k, i % nms))
    else:
        a_spec = pl.BlockSpec((tk, tm), lambda i, j, k: (k, i))
    if b_seg:
        b_spec = pl.BlockSpec((None, tk, tn), lambda i, j, k: (j // nns, k, j % nns))
    else:
        b_spec = pl.BlockSpec((tk, tn), lambda i, j, k: (k, j))
    if out_blocks:
        o_spec = pl.BlockSpec((None, tm, tn), lambda i, j, k: (j // nbs, i, j % nbs))
        o_shape = (out_blocks, M, nb_cols)
    else:
        o_spec = pl.BlockSpec((tm, tn), lambda i, j, k: (i, j))
        o_shape = (M, N)
    outs = [(jax.ShapeDtypeStruct(o_shape, out_dtype), o_spec)]
    return _matmul(name, [(a, a_spec), (b, b_spec)], outs, grid, DIMS_TN, (tm, tn), None, jobs)


def _rms(x):
    return lax.rsqrt(jnp.mean(x * x, axis=-1, keepdims=True) + NORM_EPS)


def _row_tile(T):
    return _tile(T, 256)


def _norm_fwd(name, x, g, jobs=()):
    T, D = x.shape
    tr = _row_tile(T)

    def body(x_ref, g_ref, h_ref):
        xv = x_ref[...]
        h_ref[...] = (xv * _rms(xv) * g_ref[...]).astype(h_ref.dtype)

    row = pl.BlockSpec((tr, D), lambda i: (i, 0))
    vec = pl.BlockSpec((1, D), lambda i: (0, 0))
    return _carry_call(body, name=name, steps=T // tr, in_specs=[row, vec], out_specs=[row],
                       out_shape=[jax.ShapeDtypeStruct((T, D), BF16)], scratch_shapes=[], args=(x, g), jobs=jobs)


def _resid_norm(name, x, br, g_post, g_next, jobs=()):
    T, D = x.shape
    tr = _row_tile(T)

    def body(x_ref, br_ref, gp_ref, gn_ref, xn_ref, h_ref):
        b = br_ref[...].astype(F32)
        xn = x_ref[...] + b * _rms(b) * gp_ref[...]
        xn_ref[...] = xn
        h_ref[...] = (xn * _rms(xn) * gn_ref[...]).astype(h_ref.dtype)

    row = pl.BlockSpec((tr, D), lambda i: (i, 0))
    vec = pl.BlockSpec((1, D), lambda i: (0, 0))
    return _carry_call(body, name=name, steps=T // tr, in_specs=[row, row, vec, vec], out_specs=[row, row],
                       out_shape=[jax.ShapeDtypeStruct((T, D), F32), jax.ShapeDtypeStruct((T, D), BF16)],
                       scratch_shapes=[], args=(x, br, g_post, g_next), jobs=jobs)


def _final_loss(name, x, br, g_post, target):
    T, D = x.shape
    tr = _row_tile(T)

    def body(x_ref, br_ref, gp_ref, t_ref, dy_ref, ls_ref):
        b = br_ref[...].astype(F32)
        err = x_ref[...] + b * _rms(b) * gp_ref[...] - t_ref[...]
        dy_ref[...] = err * (1.0 / D)

        @pl.when(pl.program_id(0) == 0)
        def _():
            ls_ref[...] = jnp.zeros_like(ls_ref)

        ls_ref[...] += jnp.sum(err * err)

    row = pl.BlockSpec((tr, D), lambda i: (i, 0))
    vec = pl.BlockSpec((1, D), lambda i: (0, 0))
    acc = pl.BlockSpec((SUBLANES, LANES), lambda i: (0, 0))
    return pl.pallas_call(body, name=name, grid=(T // tr,), in_specs=[row, row, vec, row], out_specs=[row, acc],
                          out_shape=[jax.ShapeDtypeStruct((T, D), F32), jax.ShapeDtypeStruct((SUBLANES, LANES), F32)],
                          compiler_params=_params(("arbitrary",)))(x, br, g_post, target)


def _norm_bwd(name, x, g, dy, add, out_dtype, jobs=()):
    T, D = x.shape
    tr = _row_tile(T)
    has_add = add is not None

    def body(*refs):
        if has_add:
            x_ref, g_ref, dy_ref, add_ref, dx_ref, dg_ref = refs
        else:
            x_ref, g_ref, dy_ref, dx_ref, dg_ref = refs
        xv = x_ref[...].astype(F32)
        r = _rms(xv)
        xhat = xv * r
        dyv = dy_ref[...].astype(F32)
        gdy = dyv * g_ref[...]
        dx = r * (gdy - xhat * jnp.mean(gdy * xhat, axis=-1, keepdims=True))
        if has_add:
            dx = dx + add_ref[...]
        dx_ref[...] = dx.astype(dx_ref.dtype)

        @pl.when(pl.program_id(0) == 0)
        def _():
            dg_ref[...] = jnp.zeros_like(dg_ref)

        dg_ref[...] += jnp.sum(dyv * xhat, axis=0, keepdims=True)

    row = pl.BlockSpec((tr, D), lambda i: (i, 0))
    vec = pl.BlockSpec((1, D), lambda i: (0, 0))
    ins = [x, g, dy] + ([add] if has_add else [])
    specs = [row, vec, row] + ([row] if has_add else [])
    if not jobs:
        return pl.pallas_call(body, name=name, grid=(T // tr,), in_specs=specs, out_specs=[row, vec],
                              out_shape=[jax.ShapeDtypeStruct((T, D), out_dtype), jax.ShapeDtypeStruct((1, D), F32)],
                              compiler_params=_params(("arbitrary",)))(*ins)
    return _carry_call(body, name=name, steps=T // tr, in_specs=specs, out_specs=[row, vec],
                       out_shape=[jax.ShapeDtypeStruct((T, D), out_dtype), jax.ShapeDtypeStruct((1, D), F32)],
                       scratch_shapes=[], args=ins, jobs=jobs)


HALO = SUBLANES
TIME_CHUNK = 512


def _chunks(T):
    tc = min(TIME_CHUNK, T)
    assert T % tc == 0 and tc % SUBLANES == 0
    return [(t0, tc) for t0 in range(0, T, tc)]


def _log_sigmoid(x):
    return -(jnp.maximum(-x, 0.0) + jnp.log(1.0 + jnp.exp(-jnp.abs(x))))


def _one_minus_exp(x):
    series = -x * (1.0 + x * (0.5 + x * (1.0 / 6.0 + x * (1.0 / 24.0))))
    return jnp.where(x > -0.01, series, 1.0 - jnp.exp(x))


_GELU_C = math.sqrt(2.0 / math.pi)


def _gelu(x):
    return 0.5 * x * (1.0 + jnp.tanh(_GELU_C * (x + 0.044715 * x * x * x)))


def _gelu_grad(x):
    th = jnp.tanh(_GELU_C * (x + 0.044715 * x * x * x))
    return 0.5 * (1.0 + th) + 0.5 * x * (1.0 - th * th) * _GELU_C * (1.0 + 3.0 * 0.044715 * x * x)


def _tile_scan(a, b, reverse):
    rows = a.shape[0]
    pos = lax.broadcasted_iota(jnp.int32, a.shape, 0) & (SUBLANES - 1)
    for d in (1, 2, 4):
        if reverse:
            ok = pos < SUBLANES - d
            shift = rows - d
        else:
            ok = pos >= d
            shift = d
        a_sh = jnp.where(ok, pltpu.roll(a, shift, 0), 1.0)
        b_sh = jnp.where(ok, pltpu.roll(b, shift, 0), 0.0)
        b = a * b_sh + b
        a = a * a_sh
    return a, b


def _carry_scan(a_s, b_s, T, reverse):
    n = T // SUBLANES
    edge = 0 if reverse else SUBLANES - 1

    def step(j, carry):
        g = (n - 1 - j) if reverse else j
        r = pl.multiple_of(g * SUBLANES, SUBLANES)
        h = b_s[pl.ds(r, SUBLANES), :] + a_s[pl.ds(r, SUBLANES), :] * carry
        b_s[pl.ds(r, SUBLANES), :] = h
        return jnp.broadcast_to(h[edge:edge + 1, :], h.shape)

    lax.fori_loop(0, n, step, jnp.zeros((SUBLANES, a_s.shape[1]), F32))


def _seg_spec(T, seg, nblk):
    return pl.BlockSpec((None, T, LANES), lambda c: (seg, 0, c))


def _rows_to_tile(rows):
    idx = lax.broadcasted_iota(jnp.int32, (SUBLANES, LANES), 0)
    out = jnp.zeros((SUBLANES, LANES), F32)
    for k, r in enumerate(rows):
        out = jnp.where(idx == k, r, out)
    return out


def _mixer_a_fwd(proj, conv_a):
    _, T, C = proj.shape
    nblk = C // LANES
    chunks = _chunks(T)

    def body(bg_ref, cg_ref, ax_ref, w_ref, y_ref, p_s):
        p_s[pl.ds(0, HALO), :] = jnp.zeros((HALO, LANES), F32)
        for t0, tc in chunks:
            p_s[pl.ds(HALO + t0, tc), :] = cg_ref[pl.ds(t0, tc), :] * ax_ref[pl.ds(t0, tc), :]
        w = w_ref[...]
        for t0, tc in chunks:
            c = (w[2:3, :] * p_s[pl.ds(HALO + t0, tc), :] + w[1:2, :] * p_s[pl.ds(HALO + t0 - 1, tc), :]
                 + w[0:1, :] * p_s[pl.ds(HALO + t0 - 2, tc), :])
            y_ref[pl.ds(t0, tc), :] = (bg_ref[pl.ds(t0, tc), :] * c).astype(y_ref.dtype)

    return pl.pallas_call(
        body, name="mixer_a_fwd", grid=(nblk,),
        in_specs=[_seg_spec(T, 0, nblk), _seg_spec(T, 1, nblk), _seg_spec(T, 2, nblk),
                  pl.BlockSpec((3, LANES), lambda c: (0, c))],
        out_specs=_seg_spec(T, 0, nblk),
        out_shape=jax.ShapeDtypeStruct((2, T, C), BF16),
        scratch_shapes=[pltpu.VMEM((T + HALO, LANES), F32)],
        compiler_params=_params(("parallel",)))(proj, proj, proj, conv_a)


def _mixer_a_bwd(proj, conv_a, dy):
    _, T, C = proj.shape
    nblk = C // LANES
    chunks = _chunks(T)

    def body(bg_ref, cg_ref, ax_ref, w_ref, dy_ref, dp_ref, dw_ref, p_s, dc_s):
        p_s[pl.ds(0, HALO), :] = jnp.zeros((HALO, LANES), F32)
        dc_s[pl.ds(T, HALO), :] = jnp.zeros((HALO, LANES), F32)
        for t0, tc in chunks:
            p_s[pl.ds(HALO + t0, tc), :] = cg_ref[pl.ds(t0, tc), :] * ax_ref[pl.ds(t0, tc), :]
        w = w_ref[...]
        for t0, tc in chunks:
            c = (w[2:3, :] * p_s[pl.ds(HALO + t0, tc), :] + w[1:2, :] * p_s[pl.ds(HALO + t0 - 1, tc), :]
                 + w[0:1, :] * p_s[pl.ds(HALO + t0 - 2, tc), :])
            dyv = dy_ref[pl.ds(t0, tc), :]
            dp_ref[0, pl.ds(t0, tc), :] = (dyv * c).astype(dp_ref.dtype)
            dc_s[pl.ds(t0, tc), :] = dyv * bg_ref[pl.ds(t0, tc), :]
        dw = [jnp.zeros((1, LANES), F32) for _ in range(3)]
        for t0, tc in chunks:
            dc = dc_s[pl.ds(t0, tc), :]
            dpv = w[2:3, :] * dc + w[1:2, :] * dc_s[pl.ds(t0 + 1, tc), :] + w[0:1, :] * dc_s[pl.ds(t0 + 2, tc), :]
            dp_ref[1, pl.ds(t0, tc), :] = (dpv * ax_ref[pl.ds(t0, tc), :]).astype(dp_ref.dtype)
            dp_ref[2, pl.ds(t0, tc), :] = (dpv * cg_ref[pl.ds(t0, tc), :]).astype(dp_ref.dtype)
            for k in range(3):
                dw[k] = dw[k] + jnp.sum(dc * p_s[pl.ds(HALO + t0 - (2 - k), tc), :], axis=0, keepdims=True)
        dw_ref[...] = _rows_to_tile(dw)

    return pl.pallas_call(
        body, name="mixer_a_bwd", grid=(nblk,),
        in_specs=[_seg_spec(T, 0, nblk), _seg_spec(T, 1, nblk), _seg_spec(T, 2, nblk),
                  pl.BlockSpec((3, LANES), lambda c: (0, c)), _seg_spec(T, 0, nblk)],
        out_specs=[pl.BlockSpec((3, T, LANES), lambda c: (0, 0, c)),
                   pl.BlockSpec((None, SUBLANES, LANES), lambda c: (c, 0, 0))],
        out_shape=[jax.ShapeDtypeStruct((6, T, C), BF16), jax.ShapeDtypeStruct((nblk, SUBLANES, LANES), F32)],
        scratch_shapes=[pltpu.VMEM((T + HALO, LANES), F32), pltpu.VMEM((T + HALO, LANES), F32)],
        compiler_params=_params(("parallel",)))(proj, proj, proj, conv_a, dy)


def _rg_gates(xr, wa, ba, wx, bx, ls):
    xb = xr.astype(BF16)
    r = jax.nn.sigmoid(jnp.dot(xb, wa, preferred_element_type=F32) + ba)
    i = jax.nn.sigmoid(jnp.dot(xb, wx, preferred_element_type=F32) + bx)
    log_a = LRU_C * r * ls
    a = jnp.exp(log_a)
    mult = jnp.sqrt(_one_minus_exp(2.0 * log_a))
    return r, i, a, mult


def _conv4(xh_s, cw, bias, t0, tc):
    return (cw[3:4, :] * xh_s[pl.ds(HALO + t0, tc), :] + cw[2:3, :] * xh_s[pl.ds(HALO + t0 - 1, tc), :]
            + cw[1:2, :] * xh_s[pl.ds(HALO + t0 - 2, tc), :] + cw[0:1, :] * xh_s[pl.ds(HALO + t0 - 3, tc), :] + bias)


def _mixer_b_specs(T, nblk):
    vec = pl.BlockSpec((1, LANES), lambda c: (0, c))
    mat = pl.BlockSpec((None, LANES, LANES), lambda c: (c, 0, 0))
    return [_seg_spec(T, 3, nblk), _seg_spec(T, 4, nblk), pl.BlockSpec((4, LANES), lambda c: (0, c)),
            vec, mat, vec, mat, vec, vec]


def _mixer_b_fwd(name, proj, conv_b, bias, wa, ba, wx, bx, lam, y, jobs=()):
    _, T, C = proj.shape
    nblk = C // LANES
    chunks = _chunks(T)

    def body(gate_ref, x_ref, cw_ref, cb_ref, wa_ref, ba_ref, wx_ref, bx_ref, lam_ref, y_in, y_ref, xh_s, a_s, b_s):
        xh_s[pl.ds(0, HALO), :] = jnp.zeros((HALO, LANES), F32)
        for t0, tc in chunks:
            xh_s[pl.ds(HALO + t0, tc), :] = x_ref[pl.ds(t0, tc), :]
        cw, bias_v = cw_ref[...], cb_ref[...]
        ls = _log_sigmoid(lam_ref[...])
        for t0, tc in chunks:
            xr = _conv4(xh_s, cw, bias_v, t0, tc)
            r, i, a, mult = _rg_gates(xr, wa_ref[...], ba_ref[...], wx_ref[...], bx_ref[...], ls)
            ac, hc = _tile_scan(a, mult * i * xr, reverse=False)
            a_s[pl.ds(t0, tc), :] = ac
            b_s[pl.ds(t0, tc), :] = hc
        _carry_scan(a_s, b_s, T, reverse=False)
        for t0, tc in chunks:
            y_ref[pl.ds(t0, tc), :] = (b_s[pl.ds(t0, tc), :] * _gelu(gate_ref[pl.ds(t0, tc), :])).astype(y_ref.dtype)

    return _carry_call(
        body, name=name, steps=nblk, in_specs=_mixer_b_specs(T, nblk) + [_ANY],
        out_specs=[_seg_spec(T, 1, nblk)], out_shape=[jax.ShapeDtypeStruct(y.shape, y.dtype)],
        scratch_shapes=[pltpu.VMEM((T + HALO, LANES), F32), pltpu.VMEM((T, LANES), F32), pltpu.VMEM((T, LANES), F32)],
        args=(proj, proj, conv_b, bias, wa, ba, wx, bx, lam, y), jobs=jobs, aliases={9: 0})


_ROW_CONV, _ROW_BIAS, _ROW_BA, _ROW_BX, _ROW_LAM = 0, 4, 5, 6, 7


def _mixer_b_bwd(name, proj, conv_b, bias, wa, ba, wx, bx, lam, dy, dproj, jobs=()):
    _, T, C = proj.shape
    nblk = C // LANES
    chunks = _chunks(T)

    def body(gate_ref, x_ref, cw_ref, cb_ref, wa_ref, ba_ref, wx_ref, bx_ref, lam_ref, dy_ref, dp_in,
             dp_ref, sm_ref, dwa_ref, dwx_ref, xh_s, xr_s, r_s, i_s, a_s, h_s, sa_s, sb_s, dx_s):
        zero_halo = jnp.zeros((HALO, LANES), F32)
        xh_s[pl.ds(0, HALO), :] = zero_halo
        h_s[pl.ds(0, HALO), :] = zero_halo
        a_s[pl.ds(T, HALO), :] = zero_halo
        dx_s[pl.ds(T, HALO), :] = zero_halo
        for t0, tc in chunks:
            xh_s[pl.ds(HALO + t0, tc), :] = x_ref[pl.ds(t0, tc), :]
        cw, bias_v = cw_ref[...], cb_ref[...]
        lam_v = lam_ref[...]
        ls = _log_sigmoid(lam_v)
        wa_v, wx_v, ba_v, bx_v = wa_ref[...], wx_ref[...], ba_ref[...], bx_ref[...]
        for t0, tc in chunks:
            xr = _conv4(xh_s, cw, bias_v, t0, tc)
            r, i, a, mult = _rg_gates(xr, wa_v, ba_v, wx_v, bx_v, ls)
            xr_s[pl.ds(t0, tc), :] = xr
            r_s[pl.ds(t0, tc), :] = r
            i_s[pl.ds(t0, tc), :] = i
            a_s[pl.ds(t0, tc), :] = a
            ac, hc = _tile_scan(a, mult * i * xr, reverse=False)
            sa_s[pl.ds(t0, tc), :] = ac
            sb_s[pl.ds(t0, tc), :] = hc
        _carry_scan(sa_s, sb_s, T, reverse=False)
        for t0, tc in chunks:
            h_s[pl.ds(HALO + t0, tc), :] = sb_s[pl.ds(t0, tc), :]
        for t0, tc in chunks:
            gv = gate_ref[pl.ds(t0, tc), :]
            dyv = dy_ref[pl.ds(t0, tc), :]
            dp_ref[0, pl.ds(t0, tc), :] = (dyv * h_s[pl.ds(HALO + t0, tc), :] * _gelu_grad(gv)).astype(dp_ref.dtype)
            ac, gc = _tile_scan(a_s[pl.ds(t0 + 1, tc), :], dyv * _gelu(gv), reverse=True)
            sa_s[pl.ds(t0, tc), :] = ac
            sb_s[pl.ds(t0, tc), :] = gc
        _carry_scan(sa_s, sb_s, T, reverse=True)
        acc = {k: jnp.zeros((1, LANES), F32) for k in ("bias", "ba", "bx", "lam")}
        dwa = jnp.zeros((LANES, LANES), F32)
        dwx = jnp.zeros((LANES, LANES), F32)
        for t0, tc in chunks:
            dht = sb_s[pl.ds(t0, tc), :]
            xr, r, i, a = xr_s[pl.ds(t0, tc), :], r_s[pl.ds(t0, tc), :], i_s[pl.ds(t0, tc), :], a_s[pl.ds(t0, tc), :]
            mult = jnp.sqrt(_one_minus_exp(2.0 * LRU_C * r * ls))
            da = dht * h_s[pl.ds(HALO + t0 - 1, tc), :]
            dmult = dht * i * xr
            di = dht * mult * xr
            dlog_a = da * a - dmult * a * a / mult
            dpa = dlog_a * (LRU_C * ls) * r * (1.0 - r)
            dpx = di * i * (1.0 - i)
            acc["lam"] = acc["lam"] + jnp.sum(dlog_a * r, axis=0, keepdims=True)
            acc["ba"] = acc["ba"] + jnp.sum(dpa, axis=0, keepdims=True)
            acc["bx"] = acc["bx"] + jnp.sum(dpx, axis=0, keepdims=True)
            xb, dpab, dpxb = xr.astype(BF16), dpa.astype(BF16), dpx.astype(BF16)
            dwa = dwa + lax.dot_general(xb, dpab, DIMS_TN, preferred_element_type=F32)
            dwx = dwx + lax.dot_general(xb, dpxb, DIMS_TN, preferred_element_type=F32)
            dxr = (dht * mult * i + lax.dot_general(dpab, wa_v, DIMS_NT, preferred_element_type=F32)
                   + lax.dot_general(dpxb, wx_v, DIMS_NT, preferred_element_type=F32))
            acc["bias"] = acc["bias"] + jnp.sum(dxr, axis=0, keepdims=True)
            dx_s[pl.ds(t0, tc), :] = dxr
        dcw = [jnp.zeros((1, LANES), F32) for _ in range(4)]
        for t0, tc in chunks:
            dxr = dx_s[pl.ds(t0, tc), :]
            dxin = (cw[3:4, :] * dxr + cw[2:3, :] * dx_s[pl.ds(t0 + 1, tc), :] + cw[1:2, :] * dx_s[pl.ds(t0 + 2, tc), :]
                    + cw[0:1, :] * dx_s[pl.ds(t0 + 3, tc), :])
            dp_ref[1, pl.ds(t0, tc), :] = dxin.astype(dp_ref.dtype)
            for k in range(4):
                dcw[k] = dcw[k] + jnp.sum(dxr * xh_s[pl.ds(HALO + t0 - (3 - k), tc), :], axis=0, keepdims=True)
        dlam = acc["lam"] * LRU_C * jax.nn.sigmoid(-lam_v)
        sm_ref[...] = _rows_to_tile(dcw + [acc["bias"], acc["ba"], acc["bx"], dlam])
        dwa_ref[...] = dwa
        dwx_ref[...] = dwx

    big = lambda halo: pltpu.VMEM((T + halo, LANES), F32)
    mat = pl.BlockSpec((None, LANES, LANES), lambda c: (c, 0, 0))
    return _carry_call(
        body, name=name, steps=nblk,
        in_specs=_mixer_b_specs(T, nblk) + [_seg_spec(T, 1, nblk), _ANY],
        out_specs=[pl.BlockSpec((3, T, LANES), lambda c: (1, 0, c)),
                   pl.BlockSpec((None, SUBLANES, LANES), lambda c: (c, 0, 0)), mat, mat],
        out_shape=[jax.ShapeDtypeStruct(dproj.shape, dproj.dtype), jax.ShapeDtypeStruct((nblk, SUBLANES, LANES), F32),
                   jax.ShapeDtypeStruct((nblk, LANES, LANES), F32), jax.ShapeDtypeStruct((nblk, LANES, LANES), F32)],
        scratch_shapes=[big(HALO), big(0), big(0), big(0), big(HALO), big(HALO), big(0), big(0), big(HALO)],
        args=(proj, proj, conv_b, bias, wa, ba, wx, bx, lam, dy, dproj), jobs=jobs, aliases={10: 0})


ATT_BLOCK = 128
ATT_GROUP = 3
ATT_TILE = ATT_BLOCK * ATT_GROUP
ATT_UNDERFLOW = -110.0
ATT_UNVISITED = -1e30


def _split_dot(x, m):
    hi = x.astype(BF16)
    lo = (x - hi.astype(F32)).astype(BF16)
    return jnp.dot(hi, m, preferred_element_type=F32) + jnp.dot(lo, m, preferred_element_type=F32)


def _sub(x, j):
    return x[:, j * ATT_BLOCK:(j + 1) * ATT_BLOCK]


def _stack_rows(x):
    return jnp.concatenate([_sub(x, j) for j in range(ATT_GROUP)], axis=0)


def _unstack_rows(x, offsets):
    return jnp.concatenate([x[j * ATT_BLOCK:(j + 1) * ATT_BLOCK, :] + offsets[j] for j in range(ATT_GROUP)], axis=1)


def _att_tile(q, k_ref, q0, qb, it, scale):
    hi = (qb + 1 - ATT_GROUP * it) * ATT_BLOCK
    k0 = pl.multiple_of(jnp.maximum(hi - ATT_TILE, 0), ATT_BLOCK)
    kt = k_ref[pl.ds(k0, ATT_TILE), :]
    z = lax.dot_general(q, kt, DIMS_NT, preferred_element_type=F32) * scale
    key = k0 + lax.broadcasted_iota(jnp.int32, z.shape, 1)
    row = q0 + lax.broadcasted_iota(jnp.int32, z.shape, 0)
    mask = (key < row) & (key < hi)
    n = jnp.where(mask, -(jnp.maximum(z, 0.0) + jnp.log(1.0 + jnp.exp(-jnp.abs(z)))), 0.0)
    return k0, kt, z, mask, n


def _suffix_in_tile(n, upper, run):
    rs = [jnp.sum(_sub(n, j), axis=1, keepdims=True) for j in range(ATT_GROUP)]
    offs = [None] * ATT_GROUP
    offs[ATT_GROUP - 1] = run
    for j in range(ATT_GROUP - 2, -1, -1):
        offs[j] = offs[j + 1] + rs[j + 1]
    return _unstack_rows(_split_dot(_stack_rows(n), upper), offs), offs[0] + rs[0]


def _head_spec(T, seg, heads):
    return pl.BlockSpec((None, T, ATT_HEAD_DIM), lambda h: (seg, 0, h))


def _attention_fwd(name, qkv, jobs=()):
    _, T, D = qkv.shape
    heads = D // ATT_HEAD_DIM
    nq = T // ATT_BLOCK
    assert nq <= LANES and T >= ATT_TILE and nq % 2 == 0
    scale = 1.0 / math.sqrt(ATT_HEAD_DIM)

    def body(q_ref, k_ref, v_ref, o_ref, r_ref, acc_s, run_s):
        rr = lax.broadcasted_iota(jnp.int32, (ATT_BLOCK, ATT_BLOCK), 0)
        cc = lax.broadcasted_iota(jnp.int32, (ATT_BLOCK, ATT_BLOCK), 1)
        upper = jnp.where(rr > cc, 1.0, 0.0).astype(BF16)
        lane = lax.broadcasted_iota(jnp.int32, (ATT_BLOCK, LANES), 1)

        def tile(slot, qb, q0, q, it, first):
            k0, _, z, mask, n = _att_tile(q, k_ref, q0, qb, it, scale)
            run = jnp.zeros((ATT_BLOCK, LANES), F32) if first else run_s[slot]
            suffix, run_next = _suffix_in_tile(n, upper, run)
            w = jnp.where(mask, jnp.exp(z + n + suffix), 0.0)
            pv = jnp.dot(w.astype(BF16), v_ref[pl.ds(k0, ATT_TILE), :], preferred_element_type=F32)
            if first:
                acc_s[slot] = pv
            else:
                acc_s[slot] += pv
                r_ref[pl.ds(q0, ATT_BLOCK), :] = jnp.where(lane == it, run, r_ref[pl.ds(q0, ATT_BLOCK), :])
            run_s[slot] = run_next
            return jnp.max(run_next) >= ATT_UNDERFLOW

        def pair_loop(p, _):
            blocks = []
            for slot in range(2):
                qb = 2 * p + slot
                q0 = pl.multiple_of(qb * ATT_BLOCK, ATT_BLOCK)
                r_ref[pl.ds(q0, ATT_BLOCK), :] = jnp.where(lane == 0, 0.0, ATT_UNVISITED)
                blocks.append((qb, q0, q_ref[pl.ds(q0, ATT_BLOCK), :]))
            go = [tile(slot, *blocks[slot], 0, True) for slot in range(2)]
            for slot in range(2):
                qb, q0, q = blocks[slot]
                n_tiles = (qb + ATT_GROUP) // ATT_GROUP
                lax.while_loop(lambda c: (c[0] < n_tiles) & c[1],
                               lambda c: (c[0] + 1, tile(slot, qb, q0, q, c[0], False)), (jnp.int32(1), go[slot]))
                o_ref[pl.ds(q0, ATT_BLOCK), :] = acc_s[slot].astype(o_ref.dtype)
            return 0

        lax.fori_loop(0, nq // 2, pair_loop, 0)

    return _carry_call(
        body, name=name, steps=heads,
        in_specs=[_head_spec(T, 0, heads), _head_spec(T, 1, heads), _head_spec(T, 2, heads)],
        out_specs=[pl.BlockSpec((T, ATT_HEAD_DIM), lambda h: (0, h)), pl.BlockSpec((None, T, LANES), lambda h: (h, 0, 0))],
        out_shape=[jax.ShapeDtypeStruct((T, D), BF16), jax.ShapeDtypeStruct((heads, T, LANES), F32)],
        scratch_shapes=[pltpu.VMEM((2, ATT_BLOCK, ATT_HEAD_DIM), F32), pltpu.VMEM((2, ATT_BLOCK, LANES), F32)],
        args=(qkv, qkv, qkv), jobs=jobs)


def _attention_bwd(name, qkv, do, rmat, jobs=()):
    _, T, D = qkv.shape
    heads = D // ATT_HEAD_DIM
    nq = T // ATT_BLOCK
    scale = 1.0 / math.sqrt(ATT_HEAD_DIM)

    def body(q_ref, k_ref, v_ref, do_ref, r_ref, dqkv_ref, dk_s, dv_s, dq_s, left_s):
        rr = lax.broadcasted_iota(jnp.int32, (ATT_BLOCK, ATT_BLOCK), 0)
        cc = lax.broadcasted_iota(jnp.int32, (ATT_BLOCK, ATT_BLOCK), 1)
        upper = jnp.where(rr > cc, 1.0, 0.0).astype(BF16)
        lower = jnp.where(rr < cc, 1.0, 0.0).astype(BF16)
        lane = lax.broadcasted_iota(jnp.int32, (ATT_BLOCK, LANES), 1)
        dk_s[...] = jnp.zeros_like(dk_s)
        dv_s[...] = jnp.zeros_like(dv_s)

        def tile(slot, qb, q0, q, dov, it, first):
            k0, kt, z, mask, n = _att_tile(q, k_ref, q0, qb, it, scale)
            vt = v_ref[pl.ds(k0, ATT_TILE), :]
            run = jnp.sum(jnp.where(lane == it, r_ref[pl.ds(q0, ATT_BLOCK), :], 0.0), axis=1, keepdims=True)
            suffix, _ = _suffix_in_tile(n, upper, run)
            s = z + n
            w = jnp.where(mask, jnp.exp(s + suffix), 0.0)
            e = w * lax.dot_general(dov, vt, DIMS_NT, preferred_element_type=F32)
            es = [jnp.sum(_sub(e, g), axis=1, keepdims=True) for g in range(ATT_GROUP)]
            pre = [jnp.zeros((ATT_BLOCK, LANES), F32) if first else left_s[slot]]
            for g in range(ATT_GROUP):
                pre.append(pre[g] + es[g])
            before = _unstack_rows(_split_dot(_stack_rows(e), lower), pre)
            sig = jnp.exp(s)
            dz = (jnp.where(mask, e * (1.0 - sig) - before * sig, 0.0) * scale).astype(BF16)
            dq = jnp.dot(dz, kt, preferred_element_type=F32)
            if first:
                dq_s[slot] = dq
            else:
                dq_s[slot] += dq
            dk_s[pl.ds(k0, ATT_TILE), :] += lax.dot_general(dz, q, DIMS_TN, preferred_element_type=F32)
            dv_s[pl.ds(k0, ATT_TILE), :] += lax.dot_general(w.astype(BF16), dov, DIMS_TN, preferred_element_type=F32)
            left_s[slot] = pre[ATT_GROUP]

        def pair_loop(p, _):
            blocks = []
            for slot in range(2):
                qb = 2 * p + slot
                q0 = pl.multiple_of(qb * ATT_BLOCK, ATT_BLOCK)
                n_tiles = (qb + ATT_GROUP) // ATT_GROUP
                seen = ((jnp.max(r_ref[pl.ds(q0, ATT_BLOCK), :], axis=0, keepdims=True) > 0.5 * ATT_UNVISITED)
                        & (lane[0:1, :] < n_tiles))
                n_visited = jnp.maximum(jnp.sum(jnp.where(seen, 1.0, 0.0)).astype(jnp.int32), 1)
                blocks.append((qb, q0, q_ref[pl.ds(q0, ATT_BLOCK), :], do_ref[pl.ds(q0, ATT_BLOCK), :], n_visited))
            for slot in range(2):
                qb, q0, q, dov, n_visited = blocks[slot]
                tile(slot, qb, q0, q, dov, n_visited - 1, True)
            for slot in range(2):
                qb, q0, q, dov, n_visited = blocks[slot]
                lax.fori_loop(1, n_visited, lambda j, c: (tile(slot, qb, q0, q, dov, n_visited - 1 - j, False), c)[1], 0)
                dqkv_ref[0, pl.ds(q0, ATT_BLOCK), :] = dq_s[slot].astype(dqkv_ref.dtype)
            return 0

        lax.fori_loop(0, nq // 2, pair_loop, 0)
        dqkv_ref[1, :, :] = dk_s[...].astype(dqkv_ref.dtype)
        dqkv_ref[2, :, :] = dv_s[...].astype(dqkv_ref.dtype)

    return _carry_call(
        body, name=name, steps=heads,
        in_specs=[_head_spec(T, 0, heads), _head_spec(T, 1, heads), _head_spec(T, 2, heads),
                  pl.BlockSpec((T, ATT_HEAD_DIM), lambda h: (0, h)), pl.BlockSpec((None, T, LANES), lambda h: (h, 0, 0))],
        out_specs=[pl.BlockSpec((3, T, ATT_HEAD_DIM), lambda h: (0, 0, h))],
        out_shape=[jax.ShapeDtypeStruct((3, T, D), BF16)],
        scratch_shapes=[pltpu.VMEM((T, ATT_HEAD_DIM), F32), pltpu.VMEM((T, ATT_HEAD_DIM), F32),
                        pltpu.VMEM((2, ATT_BLOCK, ATT_HEAD_DIM), F32), pltpu.VMEM((2, ATT_BLOCK, LANES), F32)],
        args=(qkv, qkv, qkv, do, rmat), jobs=jobs)


def _block_diag_pairs(w):
    h = w.shape[0]
    wp = w.reshape(h // 2, 2, RG_HEAD_DIM, RG_HEAD_DIM)
    z = jnp.zeros_like(wp[:, 0])
    top = jnp.concatenate([wp[:, 0], z], axis=2)
    bot = jnp.concatenate([z, wp[:, 1]], axis=2)
    return jnp.concatenate([top, bot], axis=1)


def _diag_pairs(g):
    n = g.shape[0]
    a = g[:, :RG_HEAD_DIM, :RG_HEAD_DIM]
    b = g[:, RG_HEAD_DIM:, RG_HEAD_DIM:]
    return jnp.stack([a, b], axis=1).reshape(2 * n, RG_HEAD_DIM, RG_HEAD_DIM)


class _Weights:
    def __init__(self, full, shards=None, plan=None):
        self.full, self.shards, self.plan = dict(full), shards or {}, plan or {}
        self.partial, self.rows = {}, {}

    def __getitem__(self, name):
        return self.full[name]

    def jobs(self, call):
        return [_gather_job(self.shards[n], self.partial.get(n), lo, hi, parts)
                for n, lo, hi, parts in self.plan.get(call, ())]

    def deliver(self, call, outs):
        for (n, lo, hi, parts), g in zip(self.plan.get(call, ()), outs):
            self.partial[n] = g
            self.rows[n] = self.rows.get(n, 0) + hi - lo
            if self.rows[n] == parts:
                self.full[n] = _gathered_layout(n, g)


def _gathered_layout(name, g):
    if name in ("w_in", "w_qkv", "w_out", "w_o"):
        return g.reshape(g.shape[0] * g.shape[1], g.shape[2])
    return g


class _Grads:
    def __init__(self, lands=None, plan=None):
        self.lands, self.plan = dict(lands) if lands else None, plan or {}
        self.ready, self.sent = {}, {}

    def put(self, name, arr):
        self.ready[name] = arr

    def jobs(self, call):
        if self.lands is None:
            return []
        return [_exchange_job(self.ready[n], self.lands[n], lo, hi, parts) for n, lo, hi, parts in self.plan.get(call, ())]

    def deliver(self, call, outs):
        for (n, lo, hi, parts), o in zip(self.plan.get(call, ()), outs):
            assert self.sent.get(n, (0, parts)) == (lo, parts), (call, n)
            self.lands[n] = o
            self.sent[n] = (hi, parts)

    def flush(self, name):
        if self.lands is None:
            return
        rest = []
        for n in self.ready:
            lo, parts = self.sent.get(n, (0, 1))
            if lo < parts:
                rest.append((n, lo, parts, parts))
        if rest:
            outs = _run_jobs(name, [_exchange_job(self.ready[n], self.lands[n], lo, hi, parts) for n, lo, hi, parts in rest])
            for (n, _, hi, parts), o in zip(rest, outs):
                self.lands[n] = o
                self.sent[n] = (hi, parts)


def _mlp_fwd(tag, h, wts, run):
    T, D = h.shape
    w_up = wts["up" + tag]
    fb = w_up.shape[2]
    F = fb * N_DEV
    tm, tn, tk = _tile(T, MM_TM), _tile(fb, 1024), _tile(D, MM_TK)
    nb = fb // tn

    def up_epilogue(u):
        r = jnp.maximum(u, 0.0)
        return u, r * r

    o_spec = pl.BlockSpec((tm, tn), lambda i, j, k: (i, j))
    u, act = run(
        _matmul, f"mlp_up_l{tag}",
        [(h, pl.BlockSpec((tm, tk), lambda i, j, k: (i, k))),
         (w_up, pl.BlockSpec((None, tk, tn), lambda i, j, k: (j // nb, k, j % nb)))],
        [(jax.ShapeDtypeStruct((T, F), BF16), o_spec), (jax.ShapeDtypeStruct((T, F), BF16), o_spec)],
        (T // tm, F // tn, D // tk), DIMS_NN, (tm, tn), up_epilogue, n_main=2)
    w_down = wts["down" + tag].reshape(F, D)
    m = run(_mm_nn, f"mlp_down_l{tag}", act, w_down, BF16)
    return u, act, m


def _mlp_bwd(tag, h, u, act, dm, wts, grads, run):
    T, D = h.shape
    w_up, w_down = wts["up" + tag], wts["down" + tag]
    fb = w_up.shape[2]
    F = fb * N_DEV
    grads.put("down" + tag, run(_mm_tn, f"mlp_down_dw_l{tag}", act, dm, BF16).reshape(N_DEV, fb, D))
    tm, tn, tk = _tile(T, MM_TM), _tile(fb, 1024), _tile(D, MM_TK)
    nb = fb // tn
    o_spec = pl.BlockSpec((tm, tn), lambda i, j, k: (i, j))
    du = run(
        _matmul, f"mlp_down_dx_l{tag}",
        [(dm, pl.BlockSpec((tm, tk), lambda i, j, k: (i, k))),
         (w_down, pl.BlockSpec((None, tn, tk), lambda i, j, k: (j // nb, j % nb, k))),
         (u, o_spec)],
        [(jax.ShapeDtypeStruct((T, F), BF16), o_spec)],
        (T // tm, F // tn, D // tk), DIMS_NT, (tm, tn),
        lambda r, uv: (r * (2.0 * jnp.maximum(uv.astype(F32), 0.0)),))
    grads.put("up" + tag, run(_mm_tn, f"mlp_up_dw_l{tag}", h, du, BF16, out_blocks=N_DEV))
    tn2 = _tile(D, 1024)
    pair = 2 if MM_TK >= 2 * fb else 1
    return run(
        _matmul, f"mlp_up_dx_l{tag}",
        [(du, pl.BlockSpec((tm, pair * fb), lambda i, j, k: (i, k))),
         (w_up, pl.BlockSpec((pair, tn2, fb), lambda i, j, k: (k, j, 0)))],
        [(jax.ShapeDtypeStruct((T, D), BF16), pl.BlockSpec((tm, tn2), lambda i, j, k: (i, j)))],
        (T // tm, D // tn2, N_DEV // pair), DIMS_NT, (tm, tn2), None)


def _local_step(x, target, gains, conv_a, conv_b, conv_b_bias, rg_w_a, rg_b_a, rg_w_x, rg_b_x, rg_lambda, wts, grads):
    T, D = x.shape
    g = lambda l, i: gains[l, i][None, :]
    wa_p = _block_diag_pairs(rg_w_a).astype(BF16)
    wx_p = _block_diag_pairs(rg_w_x).astype(BF16)

    def run(fn, name, *args, n_main=1, **kw):
        jw, jg = wts.jobs(name), grads.jobs(name)
        res = fn(name, *args, jobs=jw + jg, **kw)
        main, jo = res[:n_main], res[n_main:]
        wts.deliver(name, jo[:len(jw)])
        grads.deliver(name, jo[len(jw):])
        return main[0] if n_main == 1 else main

    h0 = run(_norm_fwd, "norm_in", x, g(0, 0))
    proj = run(_mm_nt, "w_in_fwd", h0, wts["w_in"], F32, out_seg=5)
    y = run(_mixer_b_fwd, "mixer_b_fwd", proj, conv_b, conv_b_bias, wa_p, rg_b_a, wx_p, rg_b_x, rg_lambda,
            _mixer_a_fwd(proj, conv_a))
    mix0 = run(_mm_nn, "w_out_fwd", y, wts["w_out"], BF16, a_seg=2)
    x1, h1 = run(_resid_norm, "resid_mix0", x, mix0, g(0, 1), g(0, 2), n_main=2)
    u0, act0, m0 = _mlp_fwd("0", h1, wts, run)
    x2, h2 = run(_resid_norm, "resid_mlp0", x1, m0, g(0, 3), g(1, 0), n_main=2)
    qkv = run(_mm_nt, "w_qkv_fwd", h2, wts["w_qkv"], BF16, out_seg=3)
    o, rmat = run(_attention_fwd, "attention_fwd", qkv, n_main=2)
    mix1 = run(_mm_nn, "w_o_fwd", o, wts["w_o"], BF16)
    x3, h3 = run(_resid_norm, "resid_mix1", x2, mix1, g(1, 1), g(1, 2), n_main=2)
    u1, act1, m1 = _mlp_fwd("1", h3, wts, run)
    dx4, sq = _final_loss("loss", x3, m1, g(1, 3), target)

    dm1, dg13 = _norm_bwd("norm_bwd_m1", m1, g(1, 3), dx4, None, BF16)
    dh3 = _mlp_bwd("1", h3, u1, act1, dm1, wts, grads, run)
    dx3, dg12 = _norm_bwd("norm_bwd_x3", x3, g(1, 2), dh3, dx4, F32)
    dmix1, dg11 = _norm_bwd("norm_bwd_mix1", mix1, g(1, 1), dx3, None, BF16)
    grads.put("w_o", run(_mm_tn, "w_o_dw", o, dmix1, BF16).reshape(N_DEV, D // N_DEV, D))
    do = run(_mm_nt, "w_o_dx", dmix1, wts["w_o"], BF16)
    dqkv = run(_attention_bwd, "attention_bwd", qkv, do, rmat)
    grads.put("w_qkv", run(_mm_tn, "w_qkv_dw", dqkv, h2, BF16, a_seg=3).reshape(N_DEV, 3 * D // N_DEV, D))
    dh2 = run(_mm_nn, "w_qkv_dx", dqkv, wts["w_qkv"], BF16, a_seg=3)
    dx2, dg10 = _norm_bwd("norm_bwd_x2", x2, g(1, 0), dh2, dx3, F32)
    dm0, dg03 = _norm_bwd("norm_bwd_m0", m0, g(0, 3), dx2, None, BF16)
    dh1 = _mlp_bwd("0", h1, u0, act0, dm0, wts, grads, run)
    dx1, dg02 = _norm_bwd("norm_bwd_x1", x1, g(0, 2), dh1, dx2, F32)
    dmix0, dg01 = _norm_bwd("norm_bwd_mix0", mix0, g(0, 1), dx1, None, BF16)
    grads.put("w_out", run(_mm_tn, "w_out_dw", y, dmix0, BF16, a_seg=2).reshape(N_DEV, D // N_DEV, D))
    dy = run(_mm_nt, "w_out_dx", dmix0, wts["w_out"], F32, out_seg=2)
    dproj_a, dconv_a = _mixer_a_bwd(proj, conv_a, dy)
    dproj, sm_b, dwa_p, dwx_p = run(_mixer_b_bwd, "mixer_b_bwd", proj, conv_b, conv_b_bias, wa_p, rg_b_a, wx_p, rg_b_x,
                                    rg_lambda, dy, dproj_a, n_main=4)
    grads.put("w_in", run(_mm_tn, "w_in_dw", dproj, h0, BF16, a_seg=5).reshape(N_DEV, 5 * D // (2 * N_DEV), D))
    dh0 = run(_mm_nn, "w_in_dx", dproj, wts["w_in"], BF16, a_seg=5)
    dx0, dg00 = run(_norm_bwd, "norm_bwd_x0", x, g(0, 0), dh0, dx1, F32, n_main=2)

    C = D // 2
    lanes_to_vec = lambda t, row: t[:, row, :].reshape(1, C)
    small = {
        "norm_gains": jnp.concatenate([dg00, dg01, dg02, dg03, dg10, dg11, dg12, dg13], axis=0).reshape(2, 4, D),
        "conv_a": jnp.transpose(dconv_a[:, :3, :], (1, 0, 2)).reshape(3, C),
        "conv_b": jnp.transpose(sm_b[:, :4, :], (1, 0, 2)).reshape(4, C),
        "conv_b_bias": lanes_to_vec(sm_b, _ROW_BIAS),
        "rg_w_a": _diag_pairs(dwa_p),
        "rg_b_a": lanes_to_vec(sm_b, _ROW_BA),
        "rg_w_x": _diag_pairs(dwx_p),
        "rg_b_x": lanes_to_vec(sm_b, _ROW_BX),
        "rg_lambda": lanes_to_vec(sm_b, _ROW_LAM),
    }
    return sq[0, 0], dx0, small


def _my_index():
    return 4 * lax.axis_index("x") + 2 * lax.axis_index("y") + lax.axis_index("c")


def _peers():
    x, y, c = lax.axis_index("x"), lax.axis_index("y"), lax.axis_index("c")
    out = []
    for k in range(1, N_DEV):
        px = x ^ ((k >> 2) & 1)
        py = y ^ ((k >> 1) & 1)
        pc = c ^ (k & 1)
        out.append(((px, py, pc), 4 * px + 2 * py + pc))
    return out


GATHER_FIRST = ()
GATHER_PLAN = {
    "norm_in": (("w_in", 0, 1, 1),),
    "w_in_fwd": (("w_out", 0, 1, 1), ("up0", 0, 1, 4)),
    "mixer_b_fwd": (("up0", 1, 3, 4),),
    "w_out_fwd": (("up0", 3, 4, 4),),
    "resid_mix0": (("down0", 0, 1, 4),),
    "mlp_up_l0": (("down0", 1, 4, 4),),
    "mlp_down_l0": (("w_qkv", 0, 1, 1),),
    "resid_mlp0": (("up1", 0, 1, 4),),
    "w_qkv_fwd": (("w_o", 0, 1, 1), ("up1", 1, 2, 4)),
    "attention_fwd": (("up1", 2, 4, 4), ("down1", 0, 2, 4)),
    "mlp_up_l1": (("down1", 2, 4, 4),),
}
EXCHANGE_PLAN = {
    "mlp_down_dx_l1": (("down1", 0, 3, 8),), "mlp_up_dw_l1": (("down1", 3, 6, 8),),
    "mlp_up_dx_l1": (("down1", 6, 8, 8), ("up1", 0, 1, 8)),
    "w_o_dw": (("up1", 1, 2, 8),), "w_o_dx": (("up1", 2, 3, 8),),
    "attention_bwd": (("up1", 3, 8, 8), ("w_o", 0, 1, 1)),
    "w_qkv_dx": (("w_qkv", 0, 1, 2),), "mlp_down_dw_l0": (("w_qkv", 1, 2, 2),),
    "mlp_down_dx_l0": (("down0", 0, 3, 8),), "mlp_up_dw_l0": (("down0", 3, 6, 8),),
    "mlp_up_dx_l0": (("down0", 6, 8, 8), ("up0", 0, 1, 8)),
    "w_out_dw": (("up0", 1, 2, 8),), "w_out_dx": (("up0", 2, 3, 8),),
    "mixer_b_bwd": (("up0", 3, 7, 8),),
    "w_in_dw": (("up0", 7, 8, 8), ("w_out", 0, 1, 2)),
    "w_in_dx": (("w_out", 1, 2, 2), ("w_in", 0, 1, 4)),
    "norm_bwd_x0": (("w_in", 1, 2, 4),),
    "adamw_mlp_w_down": (("w_in", 2, 3, 4),), "adamw_mlp_w_up": (("w_in", 3, 4, 4),),
}


def _all_gather(name, shards):
    n = len(shards)

    def body(*refs):
        srcs, dsts = refs[:n], refs[n:2 * n]
        send_sems, recv_sems, local_sems = refs[2 * n:]
        me = _my_index()
        peers = _peers()
        copies = []
        for a in range(n):
            lc = pltpu.make_async_copy(srcs[a], dsts[a].at[me], local_sems.at[a])
            lc.start()
            copies.append(lc)
        remote = []
        for a in range(n):
            for k, (pos, _) in enumerate(peers):
                cp = pltpu.make_async_remote_copy(
                    src_ref=srcs[a], dst_ref=dsts[a].at[me], send_sem=send_sems.at[a, k], recv_sem=recv_sems.at[a, k],
                    device_id=pos, device_id_type=MESH)
                cp.start()
                remote.append(cp)
        for a in range(n):
            for k, (pos, idx) in enumerate(peers):
                pltpu.make_async_remote_copy(
                    src_ref=srcs[a], dst_ref=dsts[a].at[idx], send_sem=send_sems.at[a, k], recv_sem=recv_sems.at[a, k],
                    device_id=pos, device_id_type=MESH).wait_recv()
        for cp in remote:
            cp.wait_send()
        for lc in copies:
            lc.wait()

    return pl.pallas_call(
        body, name=name,
        in_specs=[_ANY] * n, out_specs=[_ANY] * n,
        out_shape=[jax.ShapeDtypeStruct((N_DEV,) + s.shape, s.dtype) for s in shards],
        scratch_shapes=[pltpu.SemaphoreType.DMA((n, N_DEV - 1)), pltpu.SemaphoreType.DMA((n, N_DEV - 1)),
                        pltpu.SemaphoreType.DMA((n,))],
    )(*shards)


def _job_sems():
    return [pltpu.SemaphoreType.DMA((N_DEV - 1,)), pltpu.SemaphoreType.DMA((N_DEV - 1,)), pltpu.SemaphoreType.DMA((1,))]


def _gather_job(shard, prev=None, lo=0, hi=1, parts=1):
    n = shard.shape[0] // parts
    assert n * parts == shard.shape[0]
    rows = pl.ds(lo * n, (hi - lo) * n)

    def ctx():
        x, y, c = lax.axis_index("x"), lax.axis_index("y"), lax.axis_index("c")
        chips = [(1 - x, y), (x, 1 - y), (1 - x, 1 - y)]
        return x, y, c, chips

    def idx(px, py, pc):
        return 4 * px + 2 * py + pc

    def copy(src, out, sems, k, block, to):
        return pltpu.make_async_remote_copy(
            src_ref=out.at[block, rows] if src is None else src.at[rows], dst_ref=out.at[block, rows],
            send_sem=sems[0].at[k], recv_sem=sems[1].at[k], device_id=to, device_id_type=MESH)

    def start(ins, outs, sems):
        x, y, c, chips = ctx()
        src, out = ins[0], outs[0]
        me = idx(x, y, c)
        pltpu.make_async_copy(src.at[rows], out.at[me, rows], sems[2].at[0]).start()
        copy(src, out, sems, 0, me, (x, y, 1 - c)).start()
        for j, (px, py) in enumerate(chips):
            copy(src, out, sems, 1 + j, me, (px, py, c)).start()

    def mid(ins, outs, sems):
        x, y, c, chips = ctx()
        out = outs[0]
        for j, (px, py) in enumerate(chips):
            copy(None, out, sems, 1 + j, idx(px, py, c), (x, y, c)).wait_recv()
            copy(None, out, sems, 4 + j, idx(px, py, c), (x, y, 1 - c)).start()

    def end(ins, outs, sems):
        x, y, c, chips = ctx()
        src, out = ins[0], outs[0]
        me = (x, y, c)
        copy(None, out, sems, 0, idx(x, y, 1 - c), me).wait_recv()
        for j, (px, py) in enumerate(chips):
            copy(None, out, sems, 4 + j, idx(px, py, 1 - c), me).wait_recv()
        for k in range(N_DEV - 1):
            copy(src, out, sems, k, idx(x, y, c), me).wait_send()
        pltpu.make_async_copy(src.at[rows], out.at[idx(x, y, c), rows], sems[2].at[0]).wait()

    out_shape = jax.ShapeDtypeStruct((N_DEV,) + shard.shape, shard.dtype)
    if prev is None:
        return _Job([shard], [out_shape], _job_sems(), start, mid, end)
    return _Job([shard, prev], [out_shape], _job_sems(), start, mid, end, alias={1: 0})


def _exchange_job(src, land, lo=0, hi=1, parts=1):
    n = src.shape[1] // parts
    assert n * parts == src.shape[1]

    def sl(ref, s):
        return ref.at[s, pl.ds(lo * n, (hi - lo) * n)]

    def start(ins, outs, sems):
        me = _my_index()
        pltpu.make_async_copy(sl(ins[0], me), sl(outs[0], me), sems[2].at[0]).start()
        for k, (pos, idx) in enumerate(_peers()):
            pltpu.make_async_remote_copy(
                src_ref=sl(ins[0], idx), dst_ref=sl(outs[0], me), send_sem=sems[0].at[k], recv_sem=sems[1].at[k],
                device_id=pos, device_id_type=MESH).start()

    def mid(ins, outs, sems):
        pass

    def end(ins, outs, sems):
        me = _my_index()
        for k, (pos, idx) in enumerate(_peers()):
            cp = pltpu.make_async_remote_copy(
                src_ref=sl(ins[0], idx), dst_ref=sl(outs[0], idx), send_sem=sems[0].at[k], recv_sem=sems[1].at[k],
                device_id=pos, device_id_type=MESH)
            cp.wait_recv()
            cp.wait_send()
        pltpu.make_async_copy(sl(ins[0], me), sl(outs[0], me), sems[2].at[0]).wait()

    return _Job([src, land], [jax.ShapeDtypeStruct(land.shape, land.dtype)], _job_sems(), start, mid, end, alias={1: 0})


def _adamw_math(w, g, m, v):
    m = ADAM_B1 * m + (1.0 - ADAM_B1) * g
    v = ADAM_B2 * v + (1.0 - ADAM_B2) * (g * g)
    m_hat = m / (1.0 - ADAM_B1 ** ADAM_STEP)
    v_hat = v / (1.0 - ADAM_B2 ** ADAM_STEP)
    delta = -ADAM_LR * (m_hat / (jnp.sqrt(v_hat) + ADAM_EPS) + ADAM_WD * w)
    return delta, m, v


def _sum_slots(ref):
    g = ref[0].astype(F32)
    for s in range(1, N_DEV):
        g = g + ref[s].astype(F32)
    return g


def _adamw_big(name, lands, w, m, v, jobs=(), transposed=False):
    L, R, C = w.shape
    assert len(lands) == L
    tr = _tile(R, max(LANES, (256 * 1024) // C))
    nr = R // tr

    def body(*refs):
        l_refs = refs[:L]
        w_ref, m_ref, v_ref, g_ref, d_ref, nm_ref, nv_ref = refs[L:]
        for li in range(L):
            @pl.when(pl.program_id(0) // nr == li)
            def _(li=li):
                g = _sum_slots(l_refs[li])
                if transposed:
                    g = g.T
                d, nm, nv = _adamw_math(w_ref[...], g, m_ref[...], v_ref[...])
                g_ref[...] = g
                d_ref[...] = d
                nm_ref[...] = nm
                nv_ref[...] = nv

    def land_spec(li):
        if transposed:
            return pl.BlockSpec((N_DEV, C, tr), lambda s: (0, 0, jnp.where(s // nr == li, s % nr, 0)))
        return pl.BlockSpec((N_DEV, tr, C), lambda s: (0, jnp.where(s // nr == li, s % nr, 0), 0))

    row = pl.BlockSpec((None, tr, C), lambda s: (s // nr, s % nr, 0))
    return _carry_call(
        body, name=name, steps=L * nr, in_specs=[land_spec(li) for li in range(L)] + [row, row, row],
        out_specs=[row] * 4, out_shape=[jax.ShapeDtypeStruct((L, R, C), F32)] * 4, scratch_shapes=[],
        args=(*lands, w, m, v), jobs=jobs)


def _sum8(name, slots):
    _, R, C = slots.shape

    def body(s_ref, o_ref):
        o_ref[...] = _sum_slots(s_ref)

    return pl.pallas_call(body, name=name, out_shape=jax.ShapeDtypeStruct((R, C), F32))(slots)


def _adamw_small(name, g, w, m, v):
    def body(g_ref, w_ref, m_ref, v_ref, d_ref, nm_ref, nv_ref):
        d, nm, nv = _adamw_math(w_ref[...], g_ref[...], m_ref[...], v_ref[...])
        d_ref[...] = d
        nm_ref[...] = nm
        nv_ref[...] = nv

    return pl.pallas_call(body, name=name, out_shape=[jax.ShapeDtypeStruct(w.shape, F32)] * 3)(g, w, m, v)


def _pack_rows(arrs):
    parts, spans, r0 = [], [], 0
    for a in arrs:
        flat = a.astype(F32).reshape(-1)
        rows = -(-flat.shape[0] // LANES)
        rows = -(-rows // SUBLANES) * SUBLANES
        flat = jnp.pad(flat, (0, rows * LANES - flat.shape[0]))
        parts.append(flat.reshape(rows, LANES))
        spans.append((r0, rows, a.shape))
        r0 += rows
    return jnp.concatenate(parts, axis=0), spans


def _unpack_rows(buf, span):
    r0, rows, shape = span
    n = math.prod(shape)
    return buf[..., r0:r0 + rows, :].reshape(buf.shape[:-2] + (rows * LANES,))[..., :n].reshape(buf.shape[:-2] + shape)


def _col_blocks(w, n_blocks):
    K, N = w.shape
    return jnp.transpose(w.reshape(K, n_blocks, N // n_blocks), (1, 0, 2))


def _from_col_blocks(wb):
    B, K, n = wb.shape
    return jnp.transpose(wb, (1, 0, 2)).reshape(K, B * n)


def kernel(x, norm_gains, hyb_w_in, hyb_conv_a, hyb_conv_b, hyb_conv_b_bias, hyb_rg_w_a, hyb_rg_b_a, hyb_rg_w_x, hyb_rg_b_x, hyb_rg_lambda, hyb_w_out, sb_w_qkv, sb_w_o, mlp_w_up, mlp_w_down, loss_target, m_norm_gains, m_hyb_w_in, m_hyb_conv_a, m_hyb_conv_b, m_hyb_conv_b_bias, m_hyb_rg_w_a, m_hyb_rg_b_a, m_hyb_rg_w_x, m_hyb_rg_b_x, m_hyb_rg_lambda, m_hyb_w_out, m_sb_w_qkv, m_sb_w_o, m_mlp_w_up, m_mlp_w_down, v_norm_gains, v_hyb_w_in, v_hyb_conv_a, v_hyb_conv_b, v_hyb_conv_b_bias, v_hyb_rg_w_a, v_hyb_rg_b_a, v_hyb_rg_w_x, v_hyb_rg_b_x, v_hyb_rg_lambda, v_hyb_w_out, v_sb_w_qkv, v_sb_w_o, v_mlp_w_up, v_mlp_w_down):
    T, D = x.shape[1], x.shape[2]
    me = _my_index()

    small_shards, small_spans = _pack_rows([norm_gains, hyb_conv_a[0], hyb_conv_b[0]])
    (small_all,) = _all_gather("gather_small", [small_shards])
    gains_b = _unpack_rows(small_all, small_spans[0])
    gains = jnp.transpose(gains_b, (1, 2, 0, 3)).reshape(2, 4, D)
    conv_a = _from_col_blocks(_unpack_rows(small_all, small_spans[1]))
    conv_b = _from_col_blocks(_unpack_rows(small_all, small_spans[2]))

    shards = {"w_in": hyb_w_in[0].T, "w_out": hyb_w_out[0], "w_qkv": sb_w_qkv[0].T, "w_o": sb_w_o[0],
              "up0": mlp_w_up[0], "up1": mlp_w_up[1], "down0": mlp_w_down[0], "down1": mlp_w_down[1]}
    shards = {n: s.astype(BF16) for n, s in shards.items()}
    first = _run_jobs("gather_first", [_gather_job(shards[n]) for n in GATHER_FIRST]) if GATHER_FIRST else []
    wts = _Weights({n: _gathered_layout(n, g) for n, g in zip(GATHER_FIRST, first)}, shards, GATHER_PLAN)
    assert not set(GATHER_FIRST) & {e[0] for es in GATHER_PLAN.values() for e in es}
    grads_big = _Grads({n: lax.empty((N_DEV,) + s.shape, BF16) for n, s in shards.items()}, EXCHANGE_PLAN)

    sq, grad_x, small = _local_step(
        x[0], loss_target[0], gains, conv_a, conv_b, hyb_conv_b_bias, hyb_rg_w_a[0], hyb_rg_b_a, hyb_rg_w_x[0],
        hyb_rg_b_x, hyb_rg_lambda, wts, grads_big)


    names = ["norm_gains", "hyb_w_in", "hyb_conv_a", "hyb_conv_b", "hyb_conv_b_bias", "hyb_rg_w_a", "hyb_rg_b_a",
             "hyb_rg_w_x", "hyb_rg_b_x", "hyb_rg_lambda", "hyb_w_out", "sb_w_qkv", "sb_w_o", "mlp_w_up", "mlp_w_down"]
    params = dict(zip(names, [norm_gains, hyb_w_in, hyb_conv_a, hyb_conv_b, hyb_conv_b_bias, hyb_rg_w_a, hyb_rg_b_a,
                              hyb_rg_w_x, hyb_rg_b_x, hyb_rg_lambda, hyb_w_out, sb_w_qkv, sb_w_o, mlp_w_up, mlp_w_down]))
    moms = dict(zip(names, [m_norm_gains, m_hyb_w_in, m_hyb_conv_a, m_hyb_conv_b, m_hyb_conv_b_bias, m_hyb_rg_w_a,
                            m_hyb_rg_b_a, m_hyb_rg_w_x, m_hyb_rg_b_x, m_hyb_rg_lambda, m_hyb_w_out, m_sb_w_qkv,
                            m_sb_w_o, m_mlp_w_up, m_mlp_w_down]))
    vars_ = dict(zip(names, [v_norm_gains, v_hyb_w_in, v_hyb_conv_a, v_hyb_conv_b, v_hyb_conv_b_bias, v_hyb_rg_w_a,
                             v_hyb_rg_b_a, v_hyb_rg_w_x, v_hyb_rg_b_x, v_hyb_rg_lambda, v_hyb_w_out, v_sb_w_qkv,
                             v_sb_w_o, v_mlp_w_up, v_mlp_w_down]))
    grads, deltas, new_m, new_v = {}, {}, {}, {}

    big_lands = {"mlp_w_down": ["down0", "down1"], "mlp_w_up": ["up0", "up1"], "sb_w_qkv": ["w_qkv"], "sb_w_o": ["w_o"],
                 "hyb_w_out": ["w_out"], "hyb_w_in": ["w_in"]}
    for nm, keys in big_lands.items():
        call = f"adamw_{nm}"
        if nm == "hyb_w_in":
            grads_big.flush("exchange_grads")
        jobs = grads_big.jobs(call)
        assert not {k for k in keys} & {e[0] for e in EXCHANGE_PLAN.get(call, ())}
        outs = _adamw_big(call, [grads_big.lands[k] for k in keys], params[nm], moms[nm], vars_[nm], jobs=jobs,
                          transposed=nm in ("hyb_w_in", "sb_w_qkv"))
        grads[nm], deltas[nm], new_m[nm], new_v[nm] = outs[:4]
        grads_big.deliver(call, outs[4:])

    small_names = ["norm_gains", "hyb_conv_a", "hyb_conv_b", "hyb_conv_b_bias", "hyb_rg_w_a", "hyb_rg_b_a",
                   "hyb_rg_w_x", "hyb_rg_b_x", "hyb_rg_lambda"]
    small_keys = ["norm_gains", "conv_a", "conv_b", "conv_b_bias", "rg_w_a", "rg_b_a", "rg_w_x", "rg_b_x", "rg_lambda"]
    sg_buf, sg_spans = _pack_rows([small[k] for k in small_keys] + [sq.reshape(1)])
    (sg_all,) = _all_gather("gather_small_grads", [sg_buf])
    sg_sum = _sum8("sum_small_grads", sg_all)
    full = {nm: _unpack_rows(sg_sum, sp) for nm, sp in zip(small_names, sg_spans)}
    loss = _unpack_rows(sg_sum, sg_spans[-1])[0] * (0.5 / D)
    cb = (D // 2) // N_DEV
    small_grads = {
        "norm_gains": lax.dynamic_slice_in_dim(full["norm_gains"], me * (D // N_DEV), D // N_DEV, axis=2),
        "hyb_conv_a": lax.dynamic_slice_in_dim(full["hyb_conv_a"], me * cb, cb, axis=1)[None],
        "hyb_conv_b": lax.dynamic_slice_in_dim(full["hyb_conv_b"], me * cb, cb, axis=1)[None],
        "hyb_conv_b_bias": full["hyb_conv_b_bias"],
        "hyb_rg_w_a": full["hyb_rg_w_a"][None],
        "hyb_rg_b_a": full["hyb_rg_b_a"],
        "hyb_rg_w_x": full["hyb_rg_w_x"][None],
        "hyb_rg_b_x": full["hyb_rg_b_x"],
        "hyb_rg_lambda": full["hyb_rg_lambda"],
    }
    pk = lambda d: _pack_rows([d[nm] for nm in small_names])
    g_buf, spans = pk(small_grads)
    w_buf, _ = pk(params)
    m_buf, _ = pk(moms)
    v_buf, _ = pk(vars_)
    d_buf, nm_buf, nv_buf = _adamw_small("adamw_small", g_buf, w_buf, m_buf, v_buf)
    for nm, sp in zip(small_names, spans):
        grads[nm] = small_grads[nm]
        deltas[nm], new_m[nm], new_v[nm] = _unpack_rows(d_buf, sp), _unpack_rows(nm_buf, sp), _unpack_rows(nv_buf, sp)

    return (loss, grad_x[None], *[grads[n] for n in names], *[deltas[n] for n in names],
            *[new_m[n] for n in names], *[new_v[n] for n in names])
```

```python
import functools
import math

import jax
import jax.numpy as jnp
from jax import lax
from jax.experimental import pallas as pl
from jax.experimental.pallas import tpu as pltpu

F32 = jnp.float32
BF16 = jnp.bfloat16

NORM_EPS = 1e-6
LRU_C = 8.0
ATT_HEAD_DIM = 128
RG_HEAD_DIM = 64
LANES = 128
SUBLANES = 8
N_DEV = 8
ADAM_LR = 0.001
ADAM_B1 = 0.9
ADAM_B2 = 0.999
ADAM_EPS = 1e-08
ADAM_WD = 0.01
ADAM_STEP = 10
VMEM_LIMIT = 56 * 1024 * 1024
MM_TK = 2048
MM_TM = 2048
MM_TK_TOKENS = 4096
MESH = pl.DeviceIdType.MESH


def _tile(n, pref):
    if n <= pref:
        return n
    t = (pref // LANES) * LANES
    while t > LANES and n % t:
        t -= LANES
    assert n % t == 0, (n, pref)
    return t


def _params(sem):
    return pltpu.CompilerParams(dimension_semantics=sem, vmem_limit_bytes=VMEM_LIMIT)


DIMS_NN = (((1,), (0,)), ((), ()))
DIMS_NT = (((1,), (1,)), ((), ()))
DIMS_TN = (((0,), (0,)), ((), ()))


_ANY = pl.BlockSpec(memory_space=pl.ANY)


class _Job:
    def __init__(self, ins, outs, sems, start, mid, end, alias=None):
        self.ins, self.outs, self.sems = ins, outs, sems
        self.start, self.mid, self.end = start, mid, end
        self.alias = alias or {}


def _job_plumbing(jobs, n_in, n_out):
    j_ins = [a for jb in jobs for a in jb.ins]
    j_outs = [o for jb in jobs for o in jb.outs]
    j_sems = [s for jb in jobs for s in jb.sems]
    aliases, pi, po = {}, 0, 0
    for jb in jobs:
        for i_in, i_out in jb.alias.items():
            aliases[n_in + pi + i_in] = n_out + po + i_out
        pi += len(jb.ins)
        po += len(jb.outs)
    return j_ins, j_outs, j_sems, aliases


def _job_phase(jobs, which, jin, jout, jsem):
    pi = po = ps = 0
    for jb in jobs:
        getattr(jb, which)(jin[pi:pi + len(jb.ins)], jout[po:po + len(jb.outs)], jsem[ps:ps + len(jb.sems)])
        pi, po, ps = pi + len(jb.ins), po + len(jb.outs), ps + len(jb.sems)


def _run_jobs(name, jobs):
    j_ins, j_outs, j_sems, aliases = _job_plumbing(jobs, 0, 0)
    n_ji, n_jo = len(j_ins), len(j_outs)

    def body(*refs):
        jin, jout, jsem = refs[:n_ji], refs[n_ji:n_ji + n_jo], refs[n_ji + n_jo:]
        for which in ("start", "mid", "end"):
            _job_phase(jobs, which, jin, jout, jsem)

    return pl.pallas_call(body, name=name, in_specs=[_ANY] * n_ji, out_specs=[_ANY] * n_jo, out_shape=j_outs,
                          scratch_shapes=j_sems, input_output_aliases=aliases)(*j_ins)


def _carry_call(body, *, name, steps, in_specs, out_specs, out_shape, scratch_shapes, args, jobs=(), aliases=None):
    n_in, n_out, n_sc = len(in_specs), len(out_shape), len(scratch_shapes)
    j_ins, j_outs, j_sems, j_aliases = _job_plumbing(jobs, n_in, n_out)
    n_ji, n_jo = len(j_ins), len(j_outs)

    def wrapped(*refs):
        ins, jin = refs[:n_in], refs[n_in:n_in + n_ji]
        o0 = n_in + n_ji
        outs, jout = refs[o0:o0 + n_out], refs[o0 + n_out:o0 + n_out + n_jo]
        s0 = o0 + n_out + n_jo
        scratch, jsem = refs[s0:s0 + n_sc], refs[s0 + n_sc:]
        step = pl.program_id(0)
        if jobs:
            pl.when(step == 0)(lambda: _job_phase(jobs, "start", jin, jout, jsem))
            pl.when(step == (4 * steps) // 5)(lambda: _job_phase(jobs, "mid", jin, jout, jsem))
        body(*ins, *outs, *scratch)
        if jobs:
            pl.when(step == steps - 1)(lambda: _job_phase(jobs, "end", jin, jout, jsem))

    return pl.pallas_call(
        wrapped, name=name, grid=(steps,),
        in_specs=list(in_specs) + [_ANY] * n_ji, out_specs=list(out_specs) + [_ANY] * n_jo,
        out_shape=list(out_shape) + j_outs, scratch_shapes=list(scratch_shapes) + j_sems,
        input_output_aliases={**(aliases or {}), **j_aliases},
        compiler_params=_params(("arbitrary",) if jobs else ("parallel",)))(*args, *j_ins)


def _matmul(name, ins, outs, grid, dims, acc_shape, epilogue=None, jobs=()):
    n_in, n_out, nk = len(ins), len(outs), grid[2]
    j_ins, j_outs, j_sems, aliases = _job_plumbing(jobs, n_in, n_out)
    n_ji, n_jo = len(j_ins), len(j_outs)
    total = grid[0] * grid[1] * grid[2]
    n_acc = 0 if nk == 1 else 1

    def body(*refs):
        a_ref, b_ref = refs[0], refs[1]
        extras = refs[2:n_in]
        jin = refs[n_in:n_in + n_ji]
        out_refs = refs[n_in + n_ji:n_in + n_ji + n_out]
        jout = refs[n_in + n_ji + n_out:n_in + n_ji + n_out + n_jo]
        jsem = refs[n_in + n_ji + n_out + n_jo + n_acc:]
        k = pl.program_id(2)
        step = (pl.program_id(0) * grid[1] + pl.program_id(1)) * grid[2] + k
        if jobs:
            pl.when(step == 0)(lambda: _job_phase(jobs, "start", jin, jout, jsem))
            pl.when(step == (4 * total) // 5)(lambda: _job_phase(jobs, "mid", jin, jout, jsem))

        def finish(r):
            res = epilogue(r, *[e[...] for e in extras]) if epilogue is not None else (r,)
            for o, v in zip(out_refs, res):
                o[...] = v.astype(o.dtype)

        if len(b_ref.shape) == 2:
            prod = lax.dot_general(a_ref[...], b_ref[...], dims, preferred_element_type=F32)
        else:
            kb = a_ref.shape[1] // b_ref.shape[0]
            prod = sum(lax.dot_general(a_ref[:, g * kb:(g + 1) * kb], b_ref[g], dims, preferred_element_type=F32)
                       for g in range(b_ref.shape[0]))
        if nk == 1:
            finish(prod)
        else:
            acc = refs[n_in + n_ji + n_out + n_jo]

            @pl.when(k == 0)
            def _():
                acc[...] = prod

            @pl.when((k > 0) & (k < nk - 1))
            def _():
                acc[...] += prod

            @pl.when(k == nk - 1)
            def _():
                finish(acc[...] + prod)

        if jobs:
            pl.when(step == total - 1)(lambda: _job_phase(jobs, "end", jin, jout, jsem))

    sem = ("arbitrary",) * 3 if jobs else ("parallel", "parallel", "arbitrary")
    res = pl.pallas_call(
        body, name=name, grid=grid,
        in_specs=[s for _, s in ins] + [_ANY] * n_ji,
        out_specs=[s for _, s in outs] + [_ANY] * n_jo,
        out_shape=[s for s, _ in outs] + j_outs,
        scratch_shapes=[pltpu.VMEM(acc_shape, F32)] * n_acc + j_sems,
        input_output_aliases=aliases,
        compiler_params=_params(sem),
    )(*[a for a, _ in ins], *j_ins)
    return res


def _mm_nn(name, a, b, out_dtype, *, a_seg=None, out_seg=None, tm=MM_TM, tn=1024, tk=MM_TK, epilogue=None,
           extras=(), n_out=1, out_dtypes=None, jobs=()):
    if a_seg:
        _, M, ks = a.shape
        K = ks * a_seg
    else:
        M, K = a.shape
        ks = K
    N = b.shape[1]
    ns = N // out_seg if out_seg else N
    tm, tn, tk = _tile(M, tm), _tile(ns, tn), _tile(ks, tk)
    nks, nns = ks // tk, ns // tn
    grid = (M // tm, N // tn, K // tk)
    if a_seg:
        a_spec = pl.BlockSpec((None, tm, tk), lambda i, j, k: (k // nks, i, k % nks))
    else:
        a_spec = pl.BlockSpec((tm, tk), lambda i, j, k: (i, k))
    b_spec = pl.BlockSpec((tk, tn), lambda i, j, k: (k, j))
    if out_seg:
        o_spec = pl.BlockSpec((None, tm, tn), lambda i, j, k: (j // nns, i, j % nns))
        o_shape = (out_seg, M, ns)
    else:
        o_spec = pl.BlockSpec((tm, tn), lambda i, j, k: (i, j))
        o_shape = (M, N)
    dts = out_dtypes or [out_dtype] * n_out
    outs = [(jax.ShapeDtypeStruct(o_shape, dt), o_spec) for dt in dts]
    ins = [(a, a_spec), (b, b_spec)] + [(e, o_spec) for e in extras]
    return _matmul(name, ins, outs, grid, DIMS_NN, (tm, tn), epilogue, jobs)


def _mm_nt(name, a, b, out_dtype, *, a_seg=None, out_seg=None, tm=MM_TM, tn=1024, tk=MM_TK, epilogue=None, extras=(),
           jobs=()):
    if a_seg:
        _, M, ks = a.shape
        K = ks * a_seg
    else:
        M, K = a.shape
        ks = K
    N = b.shape[0]
    ns = N // out_seg if out_seg else N
    tm, tn, tk = _tile(M, tm), _tile(ns, tn), _tile(ks, tk)
    nks, nns = ks // tk, ns // tn
    grid = (M // tm, N // tn, K // tk)
    if a_seg:
        a_spec = pl.BlockSpec((None, tm, tk), lambda i, j, k: (k // nks, i, k % nks))
    else:
        a_spec = pl.BlockSpec((tm, tk), lambda i, j, k: (i, k))
    b_spec = pl.BlockSpec((tn, tk), lambda i, j, k: (j, k))
    if out_seg:
        o_spec = pl.BlockSpec((None, tm, tn), lambda i, j, k: (j // nns, i, j % nns))
        o_shape = (out_seg, M, ns)
    else:
        o_spec = pl.BlockSpec((tm, tn), lambda i, j, k: (i, j))
        o_shape = (M, N)
    outs = [(jax.ShapeDtypeStruct(o_shape, out_dtype), o_spec)]
    ins = [(a, a_spec), (b, b_spec)] + [(e, o_spec) for e in extras]
    return _matmul(name, ins, outs, grid, DIMS_NT, (tm, tn), epilogue, jobs)


def _mm_tn(name, a, b, out_dtype, *, a_seg=None, b_seg=None, out_blocks=None, tm=1024, tn=1024, tk=MM_TK_TOKENS,
           jobs=()):
    if a_seg:
        _, T, ms = a.shape
        M = ms * a_seg
    else:
        T, M = a.shape
        ms = M
    if b_seg:
        _, _, ns = b.shape
        N = ns * b_seg
    else:
        N = b.shape[1]
        ns = N
    nb_cols = N // out_blocks if out_blocks else N
    tm, tk = _tile(ms, tm), _tile(T, tk)
    tn = _tile(math.gcd(ns, nb_cols), tn)
    nms, nns, nbs = ms // tm, ns // tn, nb_cols // tn
    grid = (M // tm, N // tn, T // tk)
    if a_seg:
        a_spec = pl.BlockSpec((None, tk, tm), lambda i, j, k: (i // nms, k, i % nms))
    else:
        a_spec = pl.BlockSpec((tk, tm), lambda i, j, k: (k, i))
    if b_seg:
        b_spec = pl.BlockSpec((None, tk, tn), lambda i, j, k: (j // nns, k, j % nns))
    else:
        b_spec = pl.BlockSpec((tk, tn), lambda i, j, k: (k, j))
    if out_blocks:
        o_spec = pl.BlockSpec((None, tm, tn), lambda i, j, k: (j // nbs, i, j % nbs))
        o_shape = (out_blocks, M, nb_cols)
    else:
        o_spec = pl.BlockSpec((tm, tn), lambda i, j, k: (i, j))
        o_shape = (M, N)
    outs = [(jax.ShapeDtypeStruct(o_shape, out_dtype), o_spec)]
    return _matmul(name, [(a, a_spec), (b, b_spec)], outs, grid, DIMS_TN, (tm, tn), None, jobs)


def _rms(x):
    return lax.rsqrt(jnp.mean(x * x, axis=-1, keepdims=True) + NORM_EPS)


def _row_tile(T):
    return _tile(T, 256)


def _norm_fwd(name, x, g, jobs=()):
    T, D = x.shape
    tr = _row_tile(T)

    def body(x_ref, g_ref, h_ref):
        xv = x_ref[...]
        h_ref[...] = (xv * _rms(xv) * g_ref[...]).astype(h_ref.dtype)

    row = pl.BlockSpec((tr, D), lambda i: (i, 0))
    vec = pl.BlockSpec((1, D), lambda i: (0, 0))
    return _carry_call(body, name=name, steps=T // tr, in_specs=[row, vec], out_specs=[row],
                       out_shape=[jax.ShapeDtypeStruct((T, D), BF16)], scratch_shapes=[], args=(x, g), jobs=jobs)


def _resid_norm(name, x, br, g_post, g_next, jobs=()):
    T, D = x.shape
    tr = _row_tile(T)

    def body(x_ref, br_ref, gp_ref, gn_ref, xn_ref, h_ref):
        b = br_ref[...].astype(F32)
        xn = x_ref[...] + b * _rms(b) * gp_ref[...]
        xn_ref[...] = xn
        h_ref[...] = (xn * _rms(xn) * gn_ref[...]).astype(h_ref.dtype)

    row = pl.BlockSpec((tr, D), lambda i: (i, 0))
    vec = pl.BlockSpec((1, D), lambda i: (0, 0))
    return _carry_call(body, name=name, steps=T // tr, in_specs=[row, row, vec, vec], out_specs=[row, row],
                       out_shape=[jax.ShapeDtypeStruct((T, D), F32), jax.ShapeDtypeStruct((T, D), BF16)],
                       scratch_shapes=[], args=(x, br, g_post, g_next), jobs=jobs)


def _rms_bwd(xv, g, dyv):
    r = _rms(xv)
    xhat = xv * r
    gdy = dyv * g
    dx = r * (gdy - xhat * jnp.mean(gdy * xhat, axis=-1, keepdims=True))
    return dx, jnp.sum(dyv * xhat, axis=0, keepdims=True)


def _final_loss(name, x, br, g_post, target):
    T, D = x.shape
    tr = _row_tile(T)

    def body(x_ref, br_ref, gp_ref, t_ref, dy_ref, dbr_ref, dg_ref, ls_ref):
        b = br_ref[...].astype(F32)
        err = x_ref[...] + b * _rms(b) * gp_ref[...] - t_ref[...]
        dy = err * (1.0 / D)
        dy_ref[...] = dy
        dbr, dg = _rms_bwd(b, gp_ref[...], dy)
        dbr_ref[...] = dbr.astype(dbr_ref.dtype)

        @pl.when(pl.program_id(0) == 0)
        def _():
            ls_ref[...] = jnp.zeros_like(ls_ref)
            dg_ref[...] = jnp.zeros_like(dg_ref)

        ls_ref[...] += jnp.sum(err * err)
        dg_ref[...] += dg

    row = pl.BlockSpec((tr, D), lambda i: (i, 0))
    vec = pl.BlockSpec((1, D), lambda i: (0, 0))
    acc = pl.BlockSpec((SUBLANES, LANES), lambda i: (0, 0))
    return pl.pallas_call(body, name=name, grid=(T // tr,), in_specs=[row, row, vec, row],
                          out_specs=[row, row, vec, acc],
                          out_shape=[jax.ShapeDtypeStruct((T, D), F32), jax.ShapeDtypeStruct((T, D), BF16),
                                     jax.ShapeDtypeStruct((1, D), F32), jax.ShapeDtypeStruct((SUBLANES, LANES), F32)],
                          compiler_params=_params(("arbitrary",)))(x, br, g_post, target)


def _norm_bwd_pair(name, x, g_pre, dh, add, br, g_post):
    T, D = x.shape
    tr = _row_tile(T)

    def body(x_ref, gpre_ref, dh_ref, add_ref, br_ref, gpost_ref, dx_ref, dbr_ref, dgpre_ref, dgpost_ref):
        dx, dg_pre = _rms_bwd(x_ref[...], gpre_ref[...], dh_ref[...].astype(F32))
        dx = dx + add_ref[...]
        dx_ref[...] = dx
        dbr, dg_post = _rms_bwd(br_ref[...].astype(F32), gpost_ref[...], dx)
        dbr_ref[...] = dbr.astype(dbr_ref.dtype)

        @pl.when(pl.program_id(0) == 0)
        def _():
            dgpre_ref[...] = jnp.zeros_like(dgpre_ref)
            dgpost_ref[...] = jnp.zeros_like(dgpost_ref)

        dgpre_ref[...] += dg_pre
        dgpost_ref[...] += dg_post

    row = pl.BlockSpec((tr, D), lambda i: (i, 0))
    vec = pl.BlockSpec((1, D), lambda i: (0, 0))
    return pl.pallas_call(body, name=name, grid=(T // tr,), in_specs=[row, vec, row, row, row, vec],
                          out_specs=[row, row, vec, vec],
                          out_shape=[jax.ShapeDtypeStruct((T, D), F32), jax.ShapeDtypeStruct((T, D), BF16),
                                     jax.ShapeDtypeStruct((1, D), F32), jax.ShapeDtypeStruct((1, D), F32)],
                          compiler_params=_params(("arbitrary",)))(x, g_pre, dh, add, br, g_post)


def _norm_bwd(name, x, g, dy, add, out_dtype, jobs=()):
    T, D = x.shape
    tr = _row_tile(T)
    has_add = add is not None

    def body(*refs):
        if has_add:
            x_ref, g_ref, dy_ref, add_ref, dx_ref, dg_ref = refs
        else:
            x_ref, g_ref, dy_ref, dx_ref, dg_ref = refs
        dx, dg = _rms_bwd(x_ref[...].astype(F32), g_ref[...], dy_ref[...].astype(F32))
        if has_add:
            dx = dx + add_ref[...]
        dx_ref[...] = dx.astype(dx_ref.dtype)

        @pl.when(pl.program_id(0) == 0)
        def _():
            dg_ref[...] = jnp.zeros_like(dg_ref)

        dg_ref[...] += dg

    row = pl.BlockSpec((tr, D), lambda i: (i, 0))
    vec = pl.BlockSpec((1, D), lambda i: (0, 0))
    ins = [x, g, dy] + ([add] if has_add else [])
    specs = [row, vec, row] + ([row] if has_add else [])
    if not jobs:
        return pl.pallas_call(body, name=name, grid=(T // tr,), in_specs=specs, out_specs=[row, vec],
                              out_shape=[jax.ShapeDtypeStruct((T, D), out_dtype), jax.ShapeDtypeStruct((1, D), F32)],
                              compiler_params=_params(("arbitrary",)))(*ins)
    return _carry_call(body, name=name, steps=T // tr, in_specs=specs, out_specs=[row, vec],
                       out_shape=[jax.ShapeDtypeStruct((T, D), out_dtype), jax.ShapeDtypeStruct((1, D), F32)],
                       scratch_shapes=[], args=ins, jobs=jobs)


HALO = SUBLANES
TIME_CHUNK = 512


def _chunks(T):
    tc = min(TIME_CHUNK, T)
    assert T % tc == 0 and tc % SUBLANES == 0
    return [(t0, tc) for t0 in range(0, T, tc)]


def _log_sigmoid(x):
    return -(jnp.maximum(-x, 0.0) + jnp.log(1.0 + jnp.exp(-jnp.abs(x))))


def _one_minus_exp(x):
    series = -x * (1.0 + x * (0.5 + x * (1.0 / 6.0 + x * (1.0 / 24.0))))
    return jnp.where(x > -0.01, series, 1.0 - jnp.exp(x))


_GELU_C = math.sqrt(2.0 / math.pi)


def _gelu(x):
    return 0.5 * x * (1.0 + jnp.tanh(_GELU_C * (x + 0.044715 * x * x * x)))


def _gelu_grad(x):
    th = jnp.tanh(_GELU_C * (x + 0.044715 * x * x * x))
    return 0.5 * (1.0 + th) + 0.5 * x * (1.0 - th * th) * _GELU_C * (1.0 + 3.0 * 0.044715 * x * x)


def _tile_scan(a, b, reverse):
    rows = a.shape[0]
    pos = lax.broadcasted_iota(jnp.int32, a.shape, 0) & (SUBLANES - 1)
    for d in (1, 2, 4):
        if reverse:
            ok = pos < SUBLANES - d
            shift = rows - d
        else:
            ok = pos >= d
            shift = d
        a_sh = jnp.where(ok, pltpu.roll(a, shift, 0), 1.0)
        b_sh = jnp.where(ok, pltpu.roll(b, shift, 0), 0.0)
        b = a * b_sh + b
        a = a * a_sh
    return a, b


def _carry_scan(a_s, b_s, T, reverse):
    n = T // SUBLANES
    edge = 0 if reverse else SUBLANES - 1

    def step(j, carry):
        g = (n - 1 - j) if reverse else j
        r = pl.multiple_of(g * SUBLANES, SUBLANES)
        h = b_s[pl.ds(r, SUBLANES), :] + a_s[pl.ds(r, SUBLANES), :] * carry
        b_s[pl.ds(r, SUBLANES), :] = h
        return jnp.broadcast_to(h[edge:edge + 1, :], h.shape)

    lax.fori_loop(0, n, step, jnp.zeros((SUBLANES, a_s.shape[1]), F32))


def _seg_spec(T, seg, nblk):
    return pl.BlockSpec((None, T, LANES), lambda c: (seg, 0, c))


def _rows_to_tile(rows):
    idx = lax.broadcasted_iota(jnp.int32, (SUBLANES, LANES), 0)
    out = jnp.zeros((SUBLANES, LANES), F32)
    for k, r in enumerate(rows):
        out = jnp.where(idx == k, r, out)
    return out


def _mixer_a_fwd(proj, conv_a):
    _, T, C = proj.shape
    nblk = C // LANES
    chunks = _chunks(T)

    def body(bg_ref, cg_ref, ax_ref, w_ref, y_ref, p_s):
        p_s[pl.ds(0, HALO), :] = jnp.zeros((HALO, LANES), F32)
        for t0, tc in chunks:
            p_s[pl.ds(HALO + t0, tc), :] = cg_ref[pl.ds(t0, tc), :] * ax_ref[pl.ds(t0, tc), :]
        w = w_ref[...]
        for t0, tc in chunks:
            c = (w[2:3, :] * p_s[pl.ds(HALO + t0, tc), :] + w[1:2, :] * p_s[pl.ds(HALO + t0 - 1, tc), :]
                 + w[0:1, :] * p_s[pl.ds(HALO + t0 - 2, tc), :])
            y_ref[pl.ds(t0, tc), :] = (bg_ref[pl.ds(t0, tc), :] * c).astype(y_ref.dtype)

    return pl.pallas_call(
        body, name="mixer_a_fwd", grid=(nblk,),
        in_specs=[_seg_spec(T, 0, nblk), _seg_spec(T, 1, nblk), _seg_spec(T, 2, nblk),
                  pl.BlockSpec((3, LANES), lambda c: (0, c))],
        out_specs=_seg_spec(T, 0, nblk),
        out_shape=jax.ShapeDtypeStruct((2, T, C), BF16),
        scratch_shapes=[pltpu.VMEM((T + HALO, LANES), F32)],
        compiler_params=_params(("parallel",)))(proj, proj, proj, conv_a)


def _mixer_a_bwd(proj, conv_a, dy):
    _, T, C = proj.shape
    nblk = C // LANES
    chunks = _chunks(T)

    def body(bg_ref, cg_ref, ax_ref, w_ref, dy_ref, dp_ref, dw_ref, p_s, dc_s):
        p_s[pl.ds(0, HALO), :] = jnp.zeros((HALO, LANES), F32)
        dc_s[pl.ds(T, HALO), :] = jnp.zeros((HALO, LANES), F32)
        for t0, tc in chunks:
            p_s[pl.ds(HALO + t0, tc), :] = cg_ref[pl.ds(t0, tc), :] * ax_ref[pl.ds(t0, tc), :]
        w = w_ref[...]
        for t0, tc in chunks:
            c = (w[2:3, :] * p_s[pl.ds(HALO + t0, tc), :] + w[1:2, :] * p_s[pl.ds(HALO + t0 - 1, tc), :]
                 + w[0:1, :] * p_s[pl.ds(HALO + t0 - 2, tc), :])
            dyv = dy_ref[pl.ds(t0, tc), :]
            dp_ref[0, pl.ds(t0, tc), :] = (dyv * c).astype(dp_ref.dtype)
            dc_s[pl.ds(t0, tc), :] = dyv * bg_ref[pl.ds(t0, tc), :]
        dw = [jnp.zeros((1, LANES), F32) for _ in range(3)]
        for t0, tc in chunks:
            dc = dc_s[pl.ds(t0, tc), :]
            dpv = w[2:3, :] * dc + w[1:2, :] * dc_s[pl.ds(t0 + 1, tc), :] + w[0:1, :] * dc_s[pl.ds(t0 + 2, tc), :]
            dp_ref[1, pl.ds(t0, tc), :] = (dpv * ax_ref[pl.ds(t0, tc), :]).astype(dp_ref.dtype)
            dp_ref[2, pl.ds(t0, tc), :] = (dpv * cg_ref[pl.ds(t0, tc), :]).astype(dp_ref.dtype)
            for k in range(3):
                dw[k] = dw[k] + jnp.sum(dc * p_s[pl.ds(HALO + t0 - (2 - k), tc), :], axis=0, keepdims=True)
        dw_ref[...] = _rows_to_tile(dw)

    return pl.pallas_call(
        body, name="mixer_a_bwd", grid=(nblk,),
        in_specs=[_seg_spec(T, 0, nblk), _seg_spec(T, 1, nblk), _seg_spec(T, 2, nblk),
                  pl.BlockSpec((3, LANES), lambda c: (0, c)), _seg_spec(T, 0, nblk)],
        out_specs=[pl.BlockSpec((3, T, LANES), lambda c: (0, 0, c)),
                   pl.BlockSpec((None, SUBLANES, LANES), lambda c: (c, 0, 0))],
        out_shape=[jax.ShapeDtypeStruct((6, T, C), BF16), jax.ShapeDtypeStruct((nblk, SUBLANES, LANES), F32)],
        scratch_shapes=[pltpu.VMEM((T + HALO, LANES), F32), pltpu.VMEM((T + HALO, LANES), F32)],
        compiler_params=_params(("parallel",)))(proj, proj, proj, conv_a, dy)


def _rg_gates(xr, wa, ba, wx, bx, ls):
    xb = xr.astype(BF16)
    r = jax.nn.sigmoid(jnp.dot(xb, wa, preferred_element_type=F32) + ba)
    i = jax.nn.sigmoid(jnp.dot(xb, wx, preferred_element_type=F32) + bx)
    log_a = LRU_C * r * ls
    a = jnp.exp(log_a)
    mult = jnp.sqrt(_one_minus_exp(2.0 * log_a))
    return r, i, a, mult


def _conv4(xh_s, cw, bias, t0, tc):
    return (cw[3:4, :] * xh_s[pl.ds(HALO + t0, tc), :] + cw[2:3, :] * xh_s[pl.ds(HALO + t0 - 1, tc), :]
            + cw[1:2, :] * xh_s[pl.ds(HALO + t0 - 2, tc), :] + cw[0:1, :] * xh_s[pl.ds(HALO + t0 - 3, tc), :] + bias)


def _mixer_b_specs(T, nblk):
    vec = pl.BlockSpec((1, LANES), lambda c: (0, c))
    mat = pl.BlockSpec((None, LANES, LANES), lambda c: (c, 0, 0))
    return [_seg_spec(T, 3, nblk), _seg_spec(T, 4, nblk), pl.BlockSpec((4, LANES), lambda c: (0, c)),
            vec, mat, vec, mat, vec, vec]


def _mixer_b_fwd(name, proj, conv_b, bias, wa, ba, wx, bx, lam, y, jobs=()):
    _, T, C = proj.shape
    nblk = C // LANES
    chunks = _chunks(T)

    def body(gate_ref, x_ref, cw_ref, cb_ref, wa_ref, ba_ref, wx_ref, bx_ref, lam_ref, y_in, y_ref, xh_s, a_s, b_s):
        xh_s[pl.ds(0, HALO), :] = jnp.zeros((HALO, LANES), F32)
        for t0, tc in chunks:
            xh_s[pl.ds(HALO + t0, tc), :] = x_ref[pl.ds(t0, tc), :]
        cw, bias_v = cw_ref[...], cb_ref[...]
        ls = _log_sigmoid(lam_ref[...])
        for t0, tc in chunks:
            xr = _conv4(xh_s, cw, bias_v, t0, tc)
            r, i, a, mult = _rg_gates(xr, wa_ref[...], ba_ref[...], wx_ref[...], bx_ref[...], ls)
            ac, hc = _tile_scan(a, mult * i * xr, reverse=False)
            a_s[pl.ds(t0, tc), :] = ac
            b_s[pl.ds(t0, tc), :] = hc
        _carry_scan(a_s, b_s, T, reverse=False)
        for t0, tc in chunks:
            y_ref[pl.ds(t0, tc), :] = (b_s[pl.ds(t0, tc), :] * _gelu(gate_ref[pl.ds(t0, tc), :])).astype(y_ref.dtype)

    return _carry_call(
        body, name=name, steps=nblk, in_specs=_mixer_b_specs(T, nblk) + [_ANY],
        out_specs=[_seg_spec(T, 1, nblk)], out_shape=[jax.ShapeDtypeStruct(y.shape, y.dtype)],
        scratch_shapes=[pltpu.VMEM((T + HALO, LANES), F32), pltpu.VMEM((T, LANES), F32), pltpu.VMEM((T, LANES), F32)],
        args=(proj, proj, conv_b, bias, wa, ba, wx, bx, lam, y), jobs=jobs, aliases={9: 0})


_ROW_CONV, _ROW_BIAS, _ROW_BA, _ROW_BX, _ROW_LAM = 0, 4, 5, 6, 7


def _mixer_b_bwd(name, proj, conv_b, bias, wa, ba, wx, bx, lam, dy, dproj, jobs=()):
    _, T, C = proj.shape
    nblk = C // LANES
    chunks = _chunks(T)

    def body(gate_ref, x_ref, cw_ref, cb_ref, wa_ref, ba_ref, wx_ref, bx_ref, lam_ref, dy_ref, dp_in,
             dp_ref, sm_ref, dwa_ref, dwx_ref, xh_s, xr_s, r_s, i_s, a_s, h_s, sa_s, sb_s, dx_s):
        zero_halo = jnp.zeros((HALO, LANES), F32)
        xh_s[pl.ds(0, HALO), :] = zero_halo
        h_s[pl.ds(0, HALO), :] = zero_halo
        a_s[pl.ds(T, HALO), :] = zero_halo
        dx_s[pl.ds(T, HALO), :] = zero_halo
        for t0, tc in chunks:
            xh_s[pl.ds(HALO + t0, tc), :] = x_ref[pl.ds(t0, tc), :]
        cw, bias_v = cw_ref[...], cb_ref[...]
        lam_v = lam_ref[...]
        ls = _log_sigmoid(lam_v)
        wa_v, wx_v, ba_v, bx_v = wa_ref[...], wx_ref[...], ba_ref[...], bx_ref[...]
        for t0, tc in chunks:
            xr = _conv4(xh_s, cw, bias_v, t0, tc)
            r, i, a, mult = _rg_gates(xr, wa_v, ba_v, wx_v, bx_v, ls)
            xr_s[pl.ds(t0, tc), :] = xr
            r_s[pl.ds(t0, tc), :] = r
            i_s[pl.ds(t0, tc), :] = i
            a_s[pl.ds(t0, tc), :] = a
            ac, hc = _tile_scan(a, mult * i * xr, reverse=False)
            sa_s[pl.ds(t0, tc), :] = ac
            sb_s[pl.ds(t0, tc), :] = hc
        _carry_scan(sa_s, sb_s, T, reverse=False)
        for t0, tc in chunks:
            h_s[pl.ds(HALO + t0, tc), :] = sb_s[pl.ds(t0, tc), :]
        for t0, tc in chunks:
            gv = gate_ref[pl.ds(t0, tc), :]
            dyv = dy_ref[pl.ds(t0, tc), :]
            dp_ref[0, pl.ds(t0, tc), :] = (dyv * h_s[pl.ds(HALO + t0, tc), :] * _gelu_grad(gv)).astype(dp_ref.dtype)
            ac, gc = _tile_scan(a_s[pl.ds(t0 + 1, tc), :], dyv * _gelu(gv), reverse=True)
            sa_s[pl.ds(t0, tc), :] = ac
            sb_s[pl.ds(t0, tc), :] = gc
        _carry_scan(sa_s, sb_s, T, reverse=True)
        acc = {k: jnp.zeros((1, LANES), F32) for k in ("bias", "ba", "bx", "lam")}
        dwa = jnp.zeros((LANES, LANES), F32)
        dwx = jnp.zeros((LANES, LANES), F32)
        for t0, tc in chunks:
            dht = sb_s[pl.ds(t0, tc), :]
            xr, r, i, a = xr_s[pl.ds(t0, tc), :], r_s[pl.ds(t0, tc), :], i_s[pl.ds(t0, tc), :], a_s[pl.ds(t0, tc), :]
            mult = jnp.sqrt(_one_minus_exp(2.0 * LRU_C * r * ls))
            da = dht * h_s[pl.ds(HALO + t0 - 1, tc), :]
            dmult = dht * i * xr
            di = dht * mult * xr
            dlog_a = da * a - dmult * a * a / mult
            dpa = dlog_a * (LRU_C * ls) * r * (1.0 - r)
            dpx = di * i * (1.0 - i)
            acc["lam"] = acc["lam"] + jnp.sum(dlog_a * r, axis=0, keepdims=True)
            acc["ba"] = acc["ba"] + jnp.sum(dpa, axis=0, keepdims=True)
            acc["bx"] = acc["bx"] + jnp.sum(dpx, axis=0, keepdims=True)
            xb, dpab, dpxb = xr.astype(BF16), dpa.astype(BF16), dpx.astype(BF16)
            dwa = dwa + lax.dot_general(xb, dpab, DIMS_TN, preferred_element_type=F32)
            dwx = dwx + lax.dot_general(xb, dpxb, DIMS_TN, preferred_element_type=F32)
            dxr = (dht * mult * i + lax.dot_general(dpab, wa_v, DIMS_NT, preferred_element_type=F32)
                   + lax.dot_general(dpxb, wx_v, DIMS_NT, preferred_element_type=F32))
            acc["bias"] = acc["bias"] + jnp.sum(dxr, axis=0, keepdims=True)
            dx_s[pl.ds(t0, tc), :] = dxr
        dcw = [jnp.zeros((1, LANES), F32) for _ in range(4)]
        for t0, tc in chunks:
            dxr = dx_s[pl.ds(t0, tc), :]
            dxin = (cw[3:4, :] * dxr + cw[2:3, :] * dx_s[pl.ds(t0 + 1, tc), :] + cw[1:2, :] * dx_s[pl.ds(t0 + 2, tc), :]
                    + cw[0:1, :] * dx_s[pl.ds(t0 + 3, tc), :])
            dp_ref[1, pl.ds(t0, tc), :] = dxin.astype(dp_ref.dtype)
            for k in range(4):
                dcw[k] = dcw[k] + jnp.sum(dxr * xh_s[pl.ds(HALO + t0 - (3 - k), tc), :], axis=0, keepdims=True)
        dlam = acc["lam"] * LRU_C * jax.nn.sigmoid(-lam_v)
        sm_ref[...] = _rows_to_tile(dcw + [acc["bias"], acc["ba"], acc["bx"], dlam])
        dwa_ref[...] = dwa
        dwx_ref[...] = dwx

    big = lambda halo: pltpu.VMEM((T + halo, LANES), F32)
    mat = pl.BlockSpec((None, LANES, LANES), lambda c: (c, 0, 0))
    return _carry_call(
        body, name=name, steps=nblk,
        in_specs=_mixer_b_specs(T, nblk) + [_seg_spec(T, 1, nblk), _ANY],
        out_specs=[pl.BlockSpec((3, T, LANES), lambda c: (1, 0, c)),
                   pl.BlockSpec((None, SUBLANES, LANES), lambda c: (c, 0, 0)), mat, mat],
        out_shape=[jax.ShapeDtypeStruct(dproj.shape, dproj.dtype), jax.ShapeDtypeStruct((nblk, SUBLANES, LANES), F32),
                   jax.ShapeDtypeStruct((nblk, LANES, LANES), F32), jax.ShapeDtypeStruct((nblk, LANES, LANES), F32)],
        scratch_shapes=[big(HALO), big(0), big(0), big(0), big(HALO), big(HALO), big(0), big(0), big(HALO)],
        args=(proj, proj, conv_b, bias, wa, ba, wx, bx, lam, dy, dproj), jobs=jobs, aliases={10: 0})


ATT_BLOCK = 128
ATT_GROUP = 3
ATT_TILE = ATT_BLOCK * ATT_GROUP
ATT_UNDERFLOW = -110.0
ATT_UNVISITED = -1e30


def _split_dot(x, m):
    hi = x.astype(BF16)
    lo = (x - hi.astype(F32)).astype(BF16)
    return jnp.dot(hi, m, preferred_element_type=F32) + jnp.dot(lo, m, preferred_element_type=F32)


def _sub(x, j):
    return x[:, j * ATT_BLOCK:(j + 1) * ATT_BLOCK]


def _stack_rows(x):
    return jnp.concatenate([_sub(x, j) for j in range(ATT_GROUP)], axis=0)


def _unstack_rows(x, offsets):
    return jnp.concatenate([x[j * ATT_BLOCK:(j + 1) * ATT_BLOCK, :] + offsets[j] for j in range(ATT_GROUP)], axis=1)


def _att_tile(q, k_ref, q0, qb, it, scale):
    hi = (qb + 1 - ATT_GROUP * it) * ATT_BLOCK
    k0 = pl.multiple_of(jnp.maximum(hi - ATT_TILE, 0), ATT_BLOCK)
    kt = k_ref[pl.ds(k0, ATT_TILE), :]
    z = lax.dot_general(q, kt, DIMS_NT, preferred_element_type=F32) * scale
    key = k0 + lax.broadcasted_iota(jnp.int32, z.shape, 1)
    row = q0 + lax.broadcasted_iota(jnp.int32, z.shape, 0)
    mask = (key < row) & (key < hi)
    n = jnp.where(mask, -(jnp.maximum(z, 0.0) + jnp.log(1.0 + jnp.exp(-jnp.abs(z)))), 0.0)
    return k0, kt, z, mask, n


def _suffix_in_tile(n, upper, run):
    rs = [jnp.sum(_sub(n, j), axis=1, keepdims=True) for j in range(ATT_GROUP)]
    offs = [None] * ATT_GROUP
    offs[ATT_GROUP - 1] = run
    for j in range(ATT_GROUP - 2, -1, -1):
        offs[j] = offs[j + 1] + rs[j + 1]
    return _unstack_rows(_split_dot(_stack_rows(n), upper), offs), offs[0] + rs[0]


def _head_spec(T, seg, heads):
    return pl.BlockSpec((None, T, ATT_HEAD_DIM), lambda h: (seg, 0, h))


def _attention_fwd(name, qkv, jobs=()):
    _, T, D = qkv.shape
    heads = D // ATT_HEAD_DIM
    nq = T // ATT_BLOCK
    assert nq <= LANES and T >= ATT_TILE and nq % 2 == 0
    scale = 1.0 / math.sqrt(ATT_HEAD_DIM)

    def body(q_ref, k_ref, v_ref, o_ref, r_ref, acc_s, run_s):
        rr = lax.broadcasted_iota(jnp.int32, (ATT_BLOCK, ATT_BLOCK), 0)
        cc = lax.broadcasted_iota(jnp.int32, (ATT_BLOCK, ATT_BLOCK), 1)
        upper = jnp.where(rr > cc, 1.0, 0.0).astype(BF16)
        lane = lax.broadcasted_iota(jnp.int32, (ATT_BLOCK, LANES), 1)

        def tile(slot, qb, q0, q, it, first):
            k0, _, z, mask, n = _att_tile(q, k_ref, q0, qb, it, scale)
            run = jnp.zeros((ATT_BLOCK, LANES), F32) if first else run_s[slot]
            suffix, run_next = _suffix_in_tile(n, upper, run)
            w = jnp.where(mask, jnp.exp(z + n + suffix), 0.0)
            pv = jnp.dot(w.astype(BF16), v_ref[pl.ds(k0, ATT_TILE), :], preferred_element_type=F32)
            if first:
                acc_s[slot] = pv
            else:
                acc_s[slot] += pv
                r_ref[pl.ds(q0, ATT_BLOCK), :] = jnp.where(lane == it, run, r_ref[pl.ds(q0, ATT_BLOCK), :])
            run_s[slot] = run_next
            return jnp.max(run_next) >= ATT_UNDERFLOW

        def pair_loop(p, _):
            blocks = []
            for slot in range(2):
                qb = 2 * p + slot
                q0 = pl.multiple_of(qb * ATT_BLOCK, ATT_BLOCK)
                r_ref[pl.ds(q0, ATT_BLOCK), :] = jnp.where(lane == 0, 0.0, ATT_UNVISITED)
                blocks.append((qb, q0, q_ref[pl.ds(q0, ATT_BLOCK), :]))
            go = [tile(slot, *blocks[slot], 0, True) for slot in range(2)]
            for slot in range(2):
                qb, q0, q = blocks[slot]
                n_tiles = (qb + ATT_GROUP) // ATT_GROUP
                lax.while_loop(lambda c: (c[0] < n_tiles) & c[1],
                               lambda c: (c[0] + 1, tile(slot, qb, q0, q, c[0], False)), (jnp.int32(1), go[slot]))
                o_ref[pl.ds(q0, ATT_BLOCK), :] = acc_s[slot].astype(o_ref.dtype)
            return 0

        lax.fori_loop(0, nq // 2, pair_loop, 0)

    return _carry_call(
        body, name=name, steps=heads,
        in_specs=[_head_spec(T, 0, heads), _head_spec(T, 1, heads), _head_spec(T, 2, heads)],
        out_specs=[pl.BlockSpec((T, ATT_HEAD_DIM), lambda h: (0, h)), pl.BlockSpec((None, T, LANES), lambda h: (h, 0, 0))],
        out_shape=[jax.ShapeDtypeStruct((T, D), BF16), jax.ShapeDtypeStruct((heads, T, LANES), F32)],
        scratch_shapes=[pltpu.VMEM((2, ATT_BLOCK, ATT_HEAD_DIM), F32), pltpu.VMEM((2, ATT_BLOCK, LANES), F32)],
        args=(qkv, qkv, qkv), jobs=jobs)


def _attention_bwd(name, qkv, do, rmat, jobs=()):
    _, T, D = qkv.shape
    heads = D // ATT_HEAD_DIM
    nq = T // ATT_BLOCK
    scale = 1.0 / math.sqrt(ATT_HEAD_DIM)

    def body(q_ref, k_ref, v_ref, do_ref, r_ref, dqkv_ref, dk_s, dv_s, dq_s, left_s):
        rr = lax.broadcasted_iota(jnp.int32, (ATT_BLOCK, ATT_BLOCK), 0)
        cc = lax.broadcasted_iota(jnp.int32, (ATT_BLOCK, ATT_BLOCK), 1)
        upper = jnp.where(rr > cc, 1.0, 0.0).astype(BF16)
        lower = jnp.where(rr < cc, 1.0, 0.0).astype(BF16)
        lane = lax.broadcasted_iota(jnp.int32, (ATT_BLOCK, LANES), 1)
        dk_s[...] = jnp.zeros_like(dk_s)
        dv_s[...] = jnp.zeros_like(dv_s)

        def tile(slot, qb, q0, q, dov, it, first):
            k0, kt, z, mask, n = _att_tile(q, k_ref, q0, qb, it, scale)
            vt = v_ref[pl.ds(k0, ATT_TILE), :]
            run = jnp.sum(jnp.where(lane == it, r_ref[pl.ds(q0, ATT_BLOCK), :], 0.0), axis=1, keepdims=True)
            suffix, _ = _suffix_in_tile(n, upper, run)
            s = z + n
            w = jnp.where(mask, jnp.exp(s + suffix), 0.0)
            e = w * lax.dot_general(dov, vt, DIMS_NT, preferred_element_type=F32)
            es = [jnp.sum(_sub(e, g), axis=1, keepdims=True) for g in range(ATT_GROUP)]
            pre = [jnp.zeros((ATT_BLOCK, LANES), F32) if first else left_s[slot]]
            for g in range(ATT_GROUP):
                pre.append(pre[g] + es[g])
            before = _unstack_rows(_split_dot(_stack_rows(e), lower), pre)
            sig = jnp.exp(s)
            dz = (jnp.where(mask, e * (1.0 - sig) - before * sig, 0.0) * scale).astype(BF16)
            dq = jnp.dot(dz, kt, preferred_element_type=F32)
            if first:
                dq_s[slot] = dq
            else:
                dq_s[slot] += dq
            dk_s[pl.ds(k0, ATT_TILE), :] += lax.dot_general(dz, q, DIMS_TN, preferred_element_type=F32)
            dv_s[pl.ds(k0, ATT_TILE), :] += lax.dot_general(w.astype(BF16), dov, DIMS_TN, preferred_element_type=F32)
            left_s[slot] = pre[ATT_GROUP]

        def pair_loop(p, _):
            blocks = []
            for slot in range(2):
                qb = 2 * p + slot
                q0 = pl.multiple_of(qb * ATT_BLOCK, ATT_BLOCK)
                n_tiles = (qb + ATT_GROUP) // ATT_GROUP
                seen = ((jnp.max(r_ref[pl.ds(q0, ATT_BLOCK), :], axis=0, keepdims=True) > 0.5 * ATT_UNVISITED)
                        & (lane[0:1, :] < n_tiles))
                n_visited = jnp.maximum(jnp.sum(jnp.where(seen, 1.0, 0.0)).astype(jnp.int32), 1)
                blocks.append((qb, q0, q_ref[pl.ds(q0, ATT_BLOCK), :], do_ref[pl.ds(q0, ATT_BLOCK), :], n_visited))
            for slot in range(2):
                qb, q0, q, dov, n_visited = blocks[slot]
                tile(slot, qb, q0, q, dov, n_visited - 1, True)
            for slot in range(2):
                qb, q0, q, dov, n_visited = blocks[slot]
                lax.fori_loop(1, n_visited, lambda j, c: (tile(slot, qb, q0, q, dov, n_visited - 1 - j, False), c)[1], 0)
                dqkv_ref[0, pl.ds(q0, ATT_BLOCK), :] = dq_s[slot].astype(dqkv_ref.dtype)
            return 0

        lax.fori_loop(0, nq // 2, pair_loop, 0)
        dqkv_ref[1, :, :] = dk_s[...].astype(dqkv_ref.dtype)
        dqkv_ref[2, :, :] = dv_s[...].astype(dqkv_ref.dtype)

    return _carry_call(
        body, name=name, steps=heads,
        in_specs=[_head_spec(T, 0, heads), _head_spec(T, 1, heads), _head_spec(T, 2, heads),
                  pl.BlockSpec((T, ATT_HEAD_DIM), lambda h: (0, h)), pl.BlockSpec((None, T, LANES), lambda h: (h, 0, 0))],
        out_specs=[pl.BlockSpec((3, T, ATT_HEAD_DIM), lambda h: (0, 0, h))],
        out_shape=[jax.ShapeDtypeStruct((3, T, D), BF16)],
        scratch_shapes=[pltpu.VMEM((T, ATT_HEAD_DIM), F32), pltpu.VMEM((T, ATT_HEAD_DIM), F32),
                        pltpu.VMEM((2, ATT_BLOCK, ATT_HEAD_DIM), F32), pltpu.VMEM((2, ATT_BLOCK, LANES), F32)],
        args=(qkv, qkv, qkv, do, rmat), jobs=jobs)


def _block_diag_pairs(w):
    h = w.shape[0]
    wp = w.reshape(h // 2, 2, RG_HEAD_DIM, RG_HEAD_DIM)
    z = jnp.zeros_like(wp[:, 0])
    top = jnp.concatenate([wp[:, 0], z], axis=2)
    bot = jnp.concatenate([z, wp[:, 1]], axis=2)
    return jnp.concatenate([top, bot], axis=1)


def _diag_pairs(g):
    n = g.shape[0]
    a = g[:, :RG_HEAD_DIM, :RG_HEAD_DIM]
    b = g[:, RG_HEAD_DIM:, RG_HEAD_DIM:]
    return jnp.stack([a, b], axis=1).reshape(2 * n, RG_HEAD_DIM, RG_HEAD_DIM)


class _Weights:
    def __init__(self, full, shards=None, plan=None):
        self.full, self.shards, self.plan = dict(full), shards or {}, plan or {}
        self.partial, self.rows = {}, {}

    def __getitem__(self, name):
        return self.full[name]

    def jobs(self, call):
        return [_gather_job(self.shards[n], self.partial.get(n), lo, hi, parts)
                for n, lo, hi, parts in self.plan.get(call, ())]

    def deliver(self, call, outs):
        for (n, lo, hi, parts), g in zip(self.plan.get(call, ()), outs):
            self.partial[n] = g
            self.rows[n] = self.rows.get(n, 0) + hi - lo
            if self.rows[n] == parts:
                self.full[n] = _gathered_layout(n, g)


def _gathered_layout(name, g):
    if name in ("w_in", "w_qkv", "w_out", "w_o"):
        return g.reshape(g.shape[0] * g.shape[1], g.shape[2])
    return g


class _Grads:
    def __init__(self, lands=None, plan=None):
        self.lands, self.plan = dict(lands) if lands else None, plan or {}
        self.ready, self.sent = {}, {}

    def put(self, name, arr):
        self.ready[name] = arr

    def jobs(self, call):
        if self.lands is None:
            return []
        return [_exchange_job(self.ready[n], self.lands[n], lo, hi, parts) for n, lo, hi, parts in self.plan.get(call, ())]

    def deliver(self, call, outs):
        for (n, lo, hi, parts), o in zip(self.plan.get(call, ()), outs):
            assert self.sent.get(n, (0, parts)) == (lo, parts), (call, n)
            self.lands[n] = o
            self.sent[n] = (hi, parts)

    def flush(self, name):
        if self.lands is None:
            return
        rest = []
        for n in self.ready:
            lo, parts = self.sent.get(n, (0, 1))
            if lo < parts:
                rest.append((n, lo, parts, parts))
        if rest:
            outs = _run_jobs(name, [_exchange_job(self.ready[n], self.lands[n], lo, hi, parts) for n, lo, hi, parts in rest])
            for (n, _, hi, parts), o in zip(rest, outs):
                self.lands[n] = o
                self.sent[n] = (hi, parts)


def _mlp_fwd(tag, h, wts, run):
    T, D = h.shape
    w_up = wts["up" + tag]
    fb = w_up.shape[2]
    F = fb * N_DEV
    tm, tn, tk = _tile(T, MM_TM), _tile(fb, 1024), _tile(D, MM_TK)
    nb = fb // tn

    def up_epilogue(u):
        r = jnp.maximum(u, 0.0)
        return u, r * r

    o_spec = pl.BlockSpec((tm, tn), lambda i, j, k: (i, j))
    u, act = run(
        _matmul, f"mlp_up_l{tag}",
        [(h, pl.BlockSpec((tm, tk), lambda i, j, k: (i, k))),
         (w_up, pl.BlockSpec((None, tk, tn), lambda i, j, k: (j // nb, k, j % nb)))],
        [(jax.ShapeDtypeStruct((T, F), BF16), o_spec), (jax.ShapeDtypeStruct((T, F), BF16), o_spec)],
        (T // tm, F // tn, D // tk), DIMS_NN, (tm, tn), up_epilogue, n_main=2)
    w_down = wts["down" + tag].reshape(F, D)
    m = run(_mm_nn, f"mlp_down_l{tag}", act, w_down, BF16)
    return u, act, m


def _mlp_bwd(tag, h, u, act, dm, wts, grads, run):
    T, D = h.shape
    w_up, w_down = wts["up" + tag], wts["down" + tag]
    fb = w_up.shape[2]
    F = fb * N_DEV
    grads.put("down" + tag, run(_mm_tn, f"mlp_down_dw_l{tag}", act, dm, BF16).reshape(N_DEV, fb, D))
    tm, tn, tk = _tile(T, MM_TM), _tile(fb, 1024), _tile(D, MM_TK)
    nb = fb // tn
    o_spec = pl.BlockSpec((tm, tn), lambda i, j, k: (i, j))
    du = run(
        _matmul, f"mlp_down_dx_l{tag}",
        [(dm, pl.BlockSpec((tm, tk), lambda i, j, k: (i, k))),
         (w_down, pl.BlockSpec((None, tn, tk), lambda i, j, k: (j // nb, j % nb, k))),
         (u, o_spec)],
        [(jax.ShapeDtypeStruct((T, F), BF16), o_spec)],
        (T // tm, F // tn, D // tk), DIMS_NT, (tm, tn),
        lambda r, uv: (r * (2.0 * jnp.maximum(uv.astype(F32), 0.0)),))
    grads.put("up" + tag, run(_mm_tn, f"mlp_up_dw_l{tag}", h, du, BF16, out_blocks=N_DEV))
    tn2 = _tile(D, 1024)
    pair = 2 if MM_TK >= 2 * fb else 1
    return run(
        _matmul, f"mlp_up_dx_l{tag}",
        [(du, pl.BlockSpec((tm, pair * fb), lambda i, j, k: (i, k))),
         (w_up, pl.BlockSpec((pair, tn2, fb), lambda i, j, k: (k, j, 0)))],
        [(jax.ShapeDtypeStruct((T, D), BF16), pl.BlockSpec((tm, tn2), lambda i, j, k: (i, j)))],
        (T // tm, D // tn2, N_DEV // pair), DIMS_NT, (tm, tn2), None)


def _local_step(x, target, gains, conv_a, conv_b, conv_b_bias, rg_w_a, rg_b_a, rg_w_x, rg_b_x, rg_lambda, wts, grads):
    T, D = x.shape
    g = lambda l, i: gains[l, i][None, :]
    wa_p = _block_diag_pairs(rg_w_a).astype(BF16)
    wx_p = _block_diag_pairs(rg_w_x).astype(BF16)

    def run(fn, name, *args, n_main=1, **kw):
        jw, jg = wts.jobs(name), grads.jobs(name)
        res = fn(name, *args, jobs=jw + jg, **kw)
        main, jo = res[:n_main], res[n_main:]
        wts.deliver(name, jo[:len(jw)])
        grads.deliver(name, jo[len(jw):])
        return main[0] if n_main == 1 else main

    h0 = run(_norm_fwd, "norm_in", x, g(0, 0))
    proj = run(_mm_nt, "w_in_fwd", h0, wts["w_in"], F32, out_seg=5)
    y = run(_mixer_b_fwd, "mixer_b_fwd", proj, conv_b, conv_b_bias, wa_p, rg_b_a, wx_p, rg_b_x, rg_lambda,
            _mixer_a_fwd(proj, conv_a))
    mix0 = run(_mm_nn, "w_out_fwd", y, wts["w_out"], BF16, a_seg=2)
    x1, h1 = run(_resid_norm, "resid_mix0", x, mix0, g(0, 1), g(0, 2), n_main=2)
    u0, act0, m0 = _mlp_fwd("0", h1, wts, run)
    x2, h2 = run(_resid_norm, "resid_mlp0", x1, m0, g(0, 3), g(1, 0), n_main=2)
    qkv = run(_mm_nt, "w_qkv_fwd", h2, wts["w_qkv"], BF16, out_seg=3)
    o, rmat = run(_attention_fwd, "attention_fwd", qkv, n_main=2)
    mix1 = run(_mm_nn, "w_o_fwd", o, wts["w_o"], BF16)
    x3, h3 = run(_resid_norm, "resid_mix1", x2, mix1, g(1, 1), g(1, 2), n_main=2)
    u1, act1, m1 = _mlp_fwd("1", h3, wts, run)
    dx4, dm1, dg13, sq = _final_loss("loss", x3, m1, g(1, 3), target)

    dh3 = _mlp_bwd("1", h3, u1, act1, dm1, wts, grads, run)
    dx3, dmix1, dg12, dg11 = _norm_bwd_pair("norm_bwd_x3", x3, g(1, 2), dh3, dx4, mix1, g(1, 1))
    grads.put("w_o", run(_mm_tn, "w_o_dw", o, dmix1, BF16).reshape(N_DEV, D // N_DEV, D))
    do = run(_mm_nt, "w_o_dx", dmix1, wts["w_o"], BF16)
    dqkv = run(_attention_bwd, "attention_bwd", qkv, do, rmat)
    grads.put("w_qkv", run(_mm_tn, "w_qkv_dw", dqkv, h2, BF16, a_seg=3).reshape(N_DEV, 3 * D // N_DEV, D))
    dh2 = run(_mm_nn, "w_qkv_dx", dqkv, wts["w_qkv"], BF16, a_seg=3)
    dx2, dm0, dg10, dg03 = _norm_bwd_pair("norm_bwd_x2", x2, g(1, 0), dh2, dx3, m0, g(0, 3))
    dh1 = _mlp_bwd("0", h1, u0, act0, dm0, wts, grads, run)
    dx1, dmix0, dg02, dg01 = _norm_bwd_pair("norm_bwd_x1", x1, g(0, 2), dh1, dx2, mix0, g(0, 1))
    grads.put("w_out", run(_mm_tn, "w_out_dw", y, dmix0, BF16, a_seg=2).reshape(N_DEV, D // N_DEV, D))
    dy = run(_mm_nt, "w_out_dx", dmix0, wts["w_out"], F32, out_seg=2)
    dproj_a, dconv_a = _mixer_a_bwd(proj, conv_a, dy)
    dproj, sm_b, dwa_p, dwx_p = run(_mixer_b_bwd, "mixer_b_bwd", proj, conv_b, conv_b_bias, wa_p, rg_b_a, wx_p, rg_b_x,
                                    rg_lambda, dy, dproj_a, n_main=4)
    grads.put("w_in", run(_mm_tn, "w_in_dw", dproj, h0, BF16, a_seg=5).reshape(N_DEV, 5 * D // (2 * N_DEV), D))
    dh0 = run(_mm_nn, "w_in_dx", dproj, wts["w_in"], BF16, a_seg=5)
    dx0, dg00 = run(_norm_bwd, "norm_bwd_x0", x, g(0, 0), dh0, dx1, F32, n_main=2)

    C = D // 2
    lanes_to_vec = lambda t, row: t[:, row, :].reshape(1, C)
    small = {
        "norm_gains": jnp.concatenate([dg00, dg01, dg02, dg03, dg10, dg11, dg12, dg13], axis=0).reshape(2, 4, D),
        "conv_a": jnp.transpose(dconv_a[:, :3, :], (1, 0, 2)).reshape(3, C),
        "conv_b": jnp.transpose(sm_b[:, :4, :], (1, 0, 2)).reshape(4, C),
        "conv_b_bias": lanes_to_vec(sm_b, _ROW_BIAS),
        "rg_w_a": _diag_pairs(dwa_p),
        "rg_b_a": lanes_to_vec(sm_b, _ROW_BA),
        "rg_w_x": _diag_pairs(dwx_p),
        "rg_b_x": lanes_to_vec(sm_b, _ROW_BX),
        "rg_lambda": lanes_to_vec(sm_b, _ROW_LAM),
    }
    return sq[0, 0], dx0, small


def _my_index():
    return 4 * lax.axis_index("x") + 2 * lax.axis_index("y") + lax.axis_index("c")


def _peers():
    x, y, c = lax.axis_index("x"), lax.axis_index("y"), lax.axis_index("c")
    out = []
    for k in range(1, N_DEV):
        px = x ^ ((k >> 2) & 1)
        py = y ^ ((k >> 1) & 1)
        pc = c ^ (k & 1)
        out.append(((px, py, pc), 4 * px + 2 * py + pc))
    return out


GATHER_FIRST = ()
GATHER_PLAN = {
    "norm_in": (("w_in", 0, 1, 1),),
    "w_in_fwd": (("w_out", 0, 1, 1), ("up0", 0, 1, 4)),
    "mixer_b_fwd": (("up0", 1, 3, 4),),
    "w_out_fwd": (("up0", 3, 4, 4),),
    "resid_mix0": (("down0", 0, 1, 4),),
    "mlp_up_l0": (("down0", 1, 4, 4),),
    "mlp_down_l0": (("w_qkv", 0, 1, 1),),
    "resid_mlp0": (("up1", 0, 1, 4),),
    "w_qkv_fwd": (("w_o", 0, 1, 1), ("up1", 1, 2, 4)),
    "attention_fwd": (("up1", 2, 4, 4), ("down1", 0, 2, 4)),
    "mlp_up_l1": (("down1", 2, 4, 4),),
}
EXCHANGE_PLAN = {
    "mlp_down_dx_l1": (("down1", 0, 3, 8),), "mlp_up_dw_l1": (("down1", 3, 6, 8),),
    "mlp_up_dx_l1": (("down1", 6, 8, 8), ("up1", 0, 1, 8)),
    "w_o_dw": (("up1", 1, 2, 8),), "w_o_dx": (("up1", 2, 3, 8),),
    "attention_bwd": (("up1", 3, 8, 8), ("w_o", 0, 1, 1)),
    "w_qkv_dx": (("w_qkv", 0, 1, 2),), "mlp_down_dw_l0": (("w_qkv", 1, 2, 2),),
    "mlp_down_dx_l0": (("down0", 0, 3, 8),), "mlp_up_dw_l0": (("down0", 3, 6, 8),),
    "mlp_up_dx_l0": (("down0", 6, 8, 8), ("up0", 0, 1, 8)),
    "w_out_dw": (("up0", 1, 2, 8),), "w_out_dx": (("up0", 2, 3, 8),),
    "mixer_b_bwd": (("up0", 3, 7, 8),),
    "w_in_dw": (("up0", 7, 8, 8), ("w_out", 0, 1, 2)),
    "w_in_dx": (("w_out", 1, 2, 2), ("w_in", 0, 1, 4)),
    "norm_bwd_x0": (("w_in", 1, 2, 4),),
    "adamw_mlp_w_down": (("w_in", 2, 3, 4),), "adamw_mlp_w_up": (("w_in", 3, 4, 4),),
}


def _all_gather(name, shards):
    n = len(shards)

    def body(*refs):
        srcs, dsts = refs[:n], refs[n:2 * n]
        send_sems, recv_sems, local_sems = refs[2 * n:]
        me = _my_index()
        peers = _peers()
        copies = []
        for a in range(n):
            lc = pltpu.make_async_copy(srcs[a], dsts[a].at[me], local_sems.at[a])
            lc.start()
            copies.append(lc)
        remote = []
        for a in range(n):
            for k, (pos, _) in enumerate(peers):
                cp = pltpu.make_async_remote_copy(
                    src_ref=srcs[a], dst_ref=dsts[a].at[me], send_sem=send_sems.at[a, k], recv_sem=recv_sems.at[a, k],
                    device_id=pos, device_id_type=MESH)
                cp.start()
                remote.append(cp)
        for a in range(n):
            for k, (pos, idx) in enumerate(peers):
                pltpu.make_async_remote_copy(
                    src_ref=srcs[a], dst_ref=dsts[a].at[idx], send_sem=send_sems.at[a, k], recv_sem=recv_sems.at[a, k],
                    device_id=pos, device_id_type=MESH).wait_recv()
        for cp in remote:
            cp.wait_send()
        for lc in copies:
            lc.wait()

    return pl.pallas_call(
        body, name=name,
        in_specs=[_ANY] * n, out_specs=[_ANY] * n,
        out_shape=[jax.ShapeDtypeStruct((N_DEV,) + s.shape, s.dtype) for s in shards],
        scratch_shapes=[pltpu.SemaphoreType.DMA((n, N_DEV - 1)), pltpu.SemaphoreType.DMA((n, N_DEV - 1)),
                        pltpu.SemaphoreType.DMA((n,))],
    )(*shards)


def _job_sems():
    return [pltpu.SemaphoreType.DMA((N_DEV - 1,)), pltpu.SemaphoreType.DMA((N_DEV - 1,)), pltpu.SemaphoreType.DMA((1,))]


def _gather_job(shard, prev=None, lo=0, hi=1, parts=1):
    n = shard.shape[0] // parts
    assert n * parts == shard.shape[0]
    rows = pl.ds(lo * n, (hi - lo) * n)

    def ctx():
        x, y, c = lax.axis_index("x"), lax.axis_index("y"), lax.axis_index("c")
        chips = [(1 - x, y), (x, 1 - y), (1 - x, 1 - y)]
        return x, y, c, chips

    def idx(px, py, pc):
        return 4 * px + 2 * py + pc

    def copy(src, out, sems, k, block, to):
        return pltpu.make_async_remote_copy(
            src_ref=out.at[block, rows] if src is None else src.at[rows], dst_ref=out.at[block, rows],
            send_sem=sems[0].at[k], recv_sem=sems[1].at[k], device_id=to, device_id_type=MESH)

    def start(ins, outs, sems):
        x, y, c, chips = ctx()
        src, out = ins[0], outs[0]
        me = idx(x, y, c)
        pltpu.make_async_copy(src.at[rows], out.at[me, rows], sems[2].at[0]).start()
        copy(src, out, sems, 0, me, (x, y, 1 - c)).start()
        for j, (px, py) in enumerate(chips):
            copy(src, out, sems, 1 + j, me, (px, py, c)).start()

    def mid(ins, outs, sems):
        x, y, c, chips = ctx()
        out = outs[0]
        for j, (px, py) in enumerate(chips):
            copy(None, out, sems, 1 + j, idx(px, py, c), (x, y, c)).wait_recv()
            copy(None, out, sems, 4 + j, idx(px, py, c), (x, y, 1 - c)).start()

    def end(ins, outs, sems):
        x, y, c, chips = ctx()
        src, out = ins[0], outs[0]
        me = (x, y, c)
        copy(None, out, sems, 0, idx(x, y, 1 - c), me).wait_recv()
        for j, (px, py) in enumerate(chips):
            copy(None, out, sems, 4 + j, idx(px, py, 1 - c), me).wait_recv()
        for k in range(N_DEV - 1):
            copy(src, out, sems, k, idx(x, y, c), me).wait_send()
        pltpu.make_async_copy(src.at[rows], out.at[idx(x, y, c), rows], sems[2].at[0]).wait()

    out_shape = jax.ShapeDtypeStruct((N_DEV,) + shard.shape, shard.dtype)
    if prev is None:
        return _Job([shard], [out_shape], _job_sems(), start, mid, end)
    return _Job([shard, prev], [out_shape], _job_sems(), start, mid, end, alias={1: 0})


def _exchange_job(src, land, lo=0, hi=1, parts=1):
    n = src.shape[1] // parts
    assert n * parts == src.shape[1]

    def sl(ref, s):
        return ref.at[s, pl.ds(lo * n, (hi - lo) * n)]

    def start(ins, outs, sems):
        me = _my_index()
        pltpu.make_async_copy(sl(ins[0], me), sl(outs[0], me), sems[2].at[0]).start()
        for k, (pos, idx) in enumerate(_peers()):
            pltpu.make_async_remote_copy(
                src_ref=sl(ins[0], idx), dst_ref=sl(outs[0], me), send_sem=sems[0].at[k], recv_sem=sems[1].at[k],
                device_id=pos, device_id_type=MESH).start()

    def mid(ins, outs, sems):
        pass

    def end(ins, outs, sems):
        me = _my_index()
        for k, (pos, idx) in enumerate(_peers()):
            cp = pltpu.make_async_remote_copy(
                src_ref=sl(ins[0], idx), dst_ref=sl(outs[0], idx), send_sem=sems[0].at[k], recv_sem=sems[1].at[k],
                device_id=pos, device_id_type=MESH)
            cp.wait_recv()
            cp.wait_send()
        pltpu.make_async_copy(sl(ins[0], me), sl(outs[0], me), sems[2].at[0]).wait()

    return _Job([src, land], [jax.ShapeDtypeStruct(land.shape, land.dtype)], _job_sems(), start, mid, end, alias={1: 0})


def _adamw_math(w, g, m, v):
    m = ADAM_B1 * m + (1.0 - ADAM_B1) * g
    v = ADAM_B2 * v + (1.0 - ADAM_B2) * (g * g)
    m_hat = m / (1.0 - ADAM_B1 ** ADAM_STEP)
    v_hat = v / (1.0 - ADAM_B2 ** ADAM_STEP)
    delta = -ADAM_LR * (m_hat / (jnp.sqrt(v_hat) + ADAM_EPS) + ADAM_WD * w)
    return delta, m, v


def _sum_slots(ref):
    g = ref[0].astype(F32)
    for s in range(1, N_DEV):
        g = g + ref[s].astype(F32)
    return g


def _adamw_big(name, lands, w, m, v, jobs=(), transposed=False):
    L, R, C = w.shape
    assert len(lands) == L
    tr = _tile(R, max(LANES, (256 * 1024) // C))
    nr = R // tr

    def body(*refs):
        l_refs = refs[:L]
        w_ref, m_ref, v_ref, g_ref, d_ref, nm_ref, nv_ref = refs[L:]
        for li in range(L):
            @pl.when(pl.program_id(0) // nr == li)
            def _(li=li):
                g = _sum_slots(l_refs[li])
                if transposed:
                    g = g.T
                d, nm, nv = _adamw_math(w_ref[...], g, m_ref[...], v_ref[...])
                g_ref[...] = g
                d_ref[...] = d
                nm_ref[...] = nm
                nv_ref[...] = nv

    def land_spec(li):
        if transposed:
            return pl.BlockSpec((N_DEV, C, tr), lambda s: (0, 0, jnp.where(s // nr == li, s % nr, 0)))
        return pl.BlockSpec((N_DEV, tr, C), lambda s: (0, jnp.where(s // nr == li, s % nr, 0), 0))

    row = pl.BlockSpec((None, tr, C), lambda s: (s // nr, s % nr, 0))
    return _carry_call(
        body, name=name, steps=L * nr, in_specs=[land_spec(li) for li in range(L)] + [row, row, row],
        out_specs=[row] * 4, out_shape=[jax.ShapeDtypeStruct((L, R, C), F32)] * 4, scratch_shapes=[],
        args=(*lands, w, m, v), jobs=jobs)


def _sum8(name, slots):
    _, R, C = slots.shape

    def body(s_ref, o_ref):
        o_ref[...] = _sum_slots(s_ref)

    return pl.pallas_call(body, name=name, out_shape=jax.ShapeDtypeStruct((R, C), F32))(slots)


def _adamw_small(name, g, w, m, v):
    def body(g_ref, w_ref, m_ref, v_ref, d_ref, nm_ref, nv_ref):
        d, nm, nv = _adamw_math(w_ref[...], g_ref[...], m_ref[...], v_ref[...])
        d_ref[...] = d
        nm_ref[...] = nm
        nv_ref[...] = nv

    return pl.pallas_call(body, name=name, out_shape=[jax.ShapeDtypeStruct(w.shape, F32)] * 3)(g, w, m, v)


def _pack_rows(arrs):
    parts, spans, r0 = [], [], 0
    for a in arrs:
        flat = a.astype(F32).reshape(-1)
        rows = -(-flat.shape[0] // LANES)
        rows = -(-rows // SUBLANES) * SUBLANES
        flat = jnp.pad(flat, (0, rows * LANES - flat.shape[0]))
        parts.append(flat.reshape(rows, LANES))
        spans.append((r0, rows, a.shape))
        r0 += rows
    return jnp.concatenate(parts, axis=0), spans


def _unpack_rows(buf, span):
    r0, rows, shape = span
    n = math.prod(shape)
    return buf[..., r0:r0 + rows, :].reshape(buf.shape[:-2] + (rows * LANES,))[..., :n].reshape(buf.shape[:-2] + shape)


def _col_blocks(w, n_blocks):
    K, N = w.shape
    return jnp.transpose(w.reshape(K, n_blocks, N // n_blocks), (1, 0, 2))


def _from_col_blocks(wb):
    B, K, n = wb.shape
    return jnp.transpose(wb, (1, 0, 2)).reshape(K, B * n)


def kernel(x, norm_gains, hyb_w_in, hyb_conv_a, hyb_conv_b, hyb_conv_b_bias, hyb_rg_w_a, hyb_rg_b_a, hyb_rg_w_x, hyb_rg_b_x, hyb_rg_lambda, hyb_w_out, sb_w_qkv, sb_w_o, mlp_w_up, mlp_w_down, loss_target, m_norm_gains, m_hyb_w_in, m_hyb_conv_a, m_hyb_conv_b, m_hyb_conv_b_bias, m_hyb_rg_w_a, m_hyb_rg_b_a, m_hyb_rg_w_x, m_hyb_rg_b_x, m_hyb_rg_lambda, m_hyb_w_out, m_sb_w_qkv, m_sb_w_o, m_mlp_w_up, m_mlp_w_down, v_norm_gains, v_hyb_w_in, v_hyb_conv_a, v_hyb_conv_b, v_hyb_conv_b_bias, v_hyb_rg_w_a, v_hyb_rg_b_a, v_hyb_rg_w_x, v_hyb_rg_b_x, v_hyb_rg_lambda, v_hyb_w_out, v_sb_w_qkv, v_sb_w_o, v_mlp_w_up, v_mlp_w_down):
    T, D = x.shape[1], x.shape[2]
    me = _my_index()

    small_shards, small_spans = _pack_rows([norm_gains, hyb_conv_a[0], hyb_conv_b[0]])
    (small_all,) = _all_gather("gather_small", [small_shards])
    gains_b = _unpack_rows(small_all, small_spans[0])
    gains = jnp.transpose(gains_b, (1, 2, 0, 3)).reshape(2, 4, D)
    conv_a = _from_col_blocks(_unpack_rows(small_all, small_spans[1]))
    conv_b = _from_col_blocks(_unpack_rows(small_all, small_spans[2]))

    shards = {"w_in": hyb_w_in[0].T, "w_out": hyb_w_out[0], "w_qkv": sb_w_qkv[0].T, "w_o": sb_w_o[0],
              "up0": mlp_w_up[0], "up1": mlp_w_up[1], "down0": mlp_w_down[0], "down1": mlp_w_down[1]}
    shards = {n: s.astype(BF16) for n, s in shards.items()}
    first = _run_jobs("gather_first", [_gather_job(shards[n]) for n in GATHER_FIRST]) if GATHER_FIRST else []
    wts = _Weights({n: _gathered_layout(n, g) for n, g in zip(GATHER_FIRST, first)}, shards, GATHER_PLAN)
    assert not set(GATHER_FIRST) & {e[0] for es in GATHER_PLAN.values() for e in es}
    grads_big = _Grads({n: lax.empty((N_DEV,) + s.shape, BF16) for n, s in shards.items()}, EXCHANGE_PLAN)

    sq, grad_x, small = _local_step(
        x[0], loss_target[0], gains, conv_a, conv_b, hyb_conv_b_bias, hyb_rg_w_a[0], hyb_rg_b_a, hyb_rg_w_x[0],
        hyb_rg_b_x, hyb_rg_lambda, wts, grads_big)


    names = ["norm_gains", "hyb_w_in", "hyb_conv_a", "hyb_conv_b", "hyb_conv_b_bias", "hyb_rg_w_a", "hyb_rg_b_a",
             "hyb_rg_w_x", "hyb_rg_b_x", "hyb_rg_lambda", "hyb_w_out", "sb_w_qkv", "sb_w_o", "mlp_w_up", "mlp_w_down"]
    params = dict(zip(names, [norm_gains, hyb_w_in, hyb_conv_a, hyb_conv_b, hyb_conv_b_bias, hyb_rg_w_a, hyb_rg_b_a,
                              hyb_rg_w_x, hyb_rg_b_x, hyb_rg_lambda, hyb_w_out, sb_w_qkv, sb_w_o, mlp_w_up, mlp_w_down]))
    moms = dict(zip(names, [m_norm_gains, m_hyb_w_in, m_hyb_conv_a, m_hyb_conv_b, m_hyb_conv_b_bias, m_hyb_rg_w_a,
                            m_hyb_rg_b_a, m_hyb_rg_w_x, m_hyb_rg_b_x, m_hyb_rg_lambda, m_hyb_w_out, m_sb_w_qkv,
                            m_sb_w_o, m_mlp_w_up, m_mlp_w_down]))
    vars_ = dict(zip(names, [v_norm_gains, v_hyb_w_in, v_hyb_conv_a, v_hyb_conv_b, v_hyb_conv_b_bias, v_hyb_rg_w_a,
                             v_hyb_rg_b_a, v_hyb_rg_w_x, v_hyb_rg_b_x, v_hyb_rg_lambda, v_hyb_w_out, v_sb_w_qkv,
                             v_sb_w_o, v_mlp_w_up, v_mlp_w_down]))
    grads, deltas, new_m, new_v = {}, {}, {}, {}

    big_lands = {"mlp_w_down": ["down0", "down1"], "mlp_w_up": ["up0", "up1"], "sb_w_qkv": ["w_qkv"], "sb_w_o": ["w_o"],
                 "hyb_w_out": ["w_out"], "hyb_w_in": ["w_in"]}
    for nm, keys in big_lands.items():
        call = f"adamw_{nm}"
        if nm == "hyb_w_in":
            grads_big.flush("exchange_grads")
        jobs = grads_big.jobs(call)
        assert not {k for k in keys} & {e[0] for e in EXCHANGE_PLAN.get(call, ())}
        outs = _adamw_big(call, [grads_big.lands[k] for k in keys], params[nm], moms[nm], vars_[nm], jobs=jobs,
                          transposed=nm in ("hyb_w_in", "sb_w_qkv"))
        grads[nm], deltas[nm], new_m[nm], new_v[nm] = outs[:4]
        grads_big.deliver(call, outs[4:])

    small_names = ["norm_gains", "hyb_conv_a", "hyb_conv_b", "hyb_conv_b_bias", "hyb_rg_w_a", "hyb_rg_b_a",
                   "hyb_rg_w_x", "hyb_rg_b_x", "hyb_rg_lambda"]
    small_keys = ["norm_gains", "conv_a", "conv_b", "conv_b_bias", "rg_w_a", "rg_b_a", "rg_w_x", "rg_b_x", "rg_lambda"]
    sg_buf, sg_spans = _pack_rows([small[k] for k in small_keys] + [sq.reshape(1)])
    (sg_all,) = _all_gather("gather_small_grads", [sg_buf])
    sg_sum = _sum8("sum_small_grads", sg_all)
    full = {nm: _unpack_rows(sg_sum, sp) for nm, sp in zip(small_names, sg_spans)}
    loss = _unpack_rows(sg_sum, sg_spans[-1])[0] * (0.5 / D)
    cb = (D // 2) // N_DEV
    small_grads = {
        "norm_gains": lax.dynamic_slice_in_dim(full["norm_gains"], me * (D // N_DEV), D // N_DEV, axis=2),
        "hyb_conv_a": lax.dynamic_slice_in_dim(full["hyb_conv_a"], me * cb, cb, axis=1)[None],
        "hyb_conv_b": lax.dynamic_slice_in_dim(full["hyb_conv_b"], me * cb, cb, axis=1)[None],
        "hyb_conv_b_bias": full["hyb_conv_b_bias"],
        "hyb_rg_w_a": full["hyb_rg_w_a"][None],
        "hyb_rg_b_a": full["hyb_rg_b_a"],
        "hyb_rg_w_x": full["hyb_rg_w_x"][None],
        "hyb_rg_b_x": full["hyb_rg_b_x"],
        "hyb_rg_lambda": full["hyb_rg_lambda"],
    }
    pk = lambda d: _pack_rows([d[nm] for nm in small_names])
    g_buf, spans = pk(small_grads)
    w_buf, _ = pk(params)
    m_buf, _ = pk(moms)
    v_buf, _ = pk(vars_)
    d_buf, nm_buf, nv_buf = _adamw_small("adamw_small", g_buf, w_buf, m_buf, v_buf)
    for nm, sp in zip(small_names, spans):
        grads[nm] = small_grads[nm]
        deltas[nm], new_m[nm], new_v[nm] = _unpack_rows(d_buf, sp), _unpack_rows(nm_buf, sp), _unpack_rows(nv_buf, sp)

    return (loss, grad_x[None], *[grads[n] for n in names], *[deltas[n] for n in names],
            *[new_m[n] for n in names], *[new_v[n] for n in names])
```

```python
import math

import jax
import jax.numpy as jnp
from jax import lax
from jax.experimental import pallas as pl
from jax.experimental.pallas import tpu as pltpu

F32 = jnp.float32
BF16 = jnp.bfloat16

NORM_EPS = 1e-6
LRU_C = 8.0
ATT_HEAD_DIM = 128
RG_HEAD_DIM = 64
LANES = 128
SUBLANES = 8
N_DEV = 8
ADAM_LR = 0.001
ADAM_B1 = 0.9
ADAM_B2 = 0.999
ADAM_EPS = 1e-08
ADAM_WD = 0.01
ADAM_STEP = 10
VMEM_LIMIT = 56 * 1024 * 1024
MM_TK = 2048
MM_TM = 2048
MM_TK_TOKENS = 4096
MESH = pl.DeviceIdType.MESH


def _tile(n, pref):
    if n <= pref:
        return n
    t = (pref // LANES) * LANES
    while t > LANES and n % t:
        t -= LANES
    assert n % t == 0, (n, pref)
    return t


def _params(sem):
    return pltpu.CompilerParams(dimension_semantics=sem, vmem_limit_bytes=VMEM_LIMIT)


DIMS_NN = (((1,), (0,)), ((), ()))
DIMS_NT = (((1,), (1,)), ((), ()))
DIMS_TN = (((0,), (0,)), ((), ()))


_ANY = pl.BlockSpec(memory_space=pl.ANY)


class _Job:
    def __init__(self, ins, outs, sems, start, mid, end, alias=None):
        self.ins, self.outs, self.sems = ins, outs, sems
        self.start, self.mid, self.end = start, mid, end
        self.alias = alias or {}


def _job_plumbing(jobs, n_in, n_out):
    j_ins = [a for jb in jobs for a in jb.ins]
    j_outs = [o for jb in jobs for o in jb.outs]
    j_sems = [s for jb in jobs for s in jb.sems]
    aliases, pi, po = {}, 0, 0
    for jb in jobs:
        for i_in, i_out in jb.alias.items():
            aliases[n_in + pi + i_in] = n_out + po + i_out
        pi += len(jb.ins)
        po += len(jb.outs)
    return j_ins, j_outs, j_sems, aliases


def _job_phase(jobs, which, jin, jout, jsem):
    pi = po = ps = 0
    for jb in jobs:
        getattr(jb, which)(jin[pi:pi + len(jb.ins)], jout[po:po + len(jb.outs)], jsem[ps:ps + len(jb.sems)])
        pi, po, ps = pi + len(jb.ins), po + len(jb.outs), ps + len(jb.sems)


def _run_jobs(name, jobs):
    j_ins, j_outs, j_sems, aliases = _job_plumbing(jobs, 0, 0)
    n_ji, n_jo = len(j_ins), len(j_outs)

    def body(*refs):
        jin, jout, jsem = refs[:n_ji], refs[n_ji:n_ji + n_jo], refs[n_ji + n_jo:]
        for which in ("start", "mid", "end"):
            _job_phase(jobs, which, jin, jout, jsem)

    return pl.pallas_call(body, name=name, in_specs=[_ANY] * n_ji, out_specs=[_ANY] * n_jo, out_shape=j_outs,
                          scratch_shapes=j_sems, input_output_aliases=aliases)(*j_ins)


def _carry_call(body, *, name, steps, in_specs, out_specs, out_shape, scratch_shapes, args, jobs=(), aliases=None):
    n_in, n_out, n_sc = len(in_specs), len(out_shape), len(scratch_shapes)
    j_ins, j_outs, j_sems, j_aliases = _job_plumbing(jobs, n_in, n_out)
    n_ji, n_jo = len(j_ins), len(j_outs)

    def wrapped(*refs):
        ins, jin = refs[:n_in], refs[n_in:n_in + n_ji]
        o0 = n_in + n_ji
        outs, jout = refs[o0:o0 + n_out], refs[o0 + n_out:o0 + n_out + n_jo]
        s0 = o0 + n_out + n_jo
        scratch, jsem = refs[s0:s0 + n_sc], refs[s0 + n_sc:]
        step = pl.program_id(0)
        if jobs:
            pl.when(step == 0)(lambda: _job_phase(jobs, "start", jin, jout, jsem))
            pl.when(step == (4 * steps) // 5)(lambda: _job_phase(jobs, "mid", jin, jout, jsem))
        body(*ins, *outs, *scratch)
        if jobs:
            pl.when(step == steps - 1)(lambda: _job_phase(jobs, "end", jin, jout, jsem))

    return pl.pallas_call(
        wrapped, name=name, grid=(steps,),
        in_specs=list(in_specs) + [_ANY] * n_ji, out_specs=list(out_specs) + [_ANY] * n_jo,
        out_shape=list(out_shape) + j_outs, scratch_shapes=list(scratch_shapes) + j_sems,
        input_output_aliases={**(aliases or {}), **j_aliases},
        compiler_params=_params(("arbitrary",) if jobs else ("parallel",)))(*args, *j_ins)


def _matmul(name, ins, outs, grid, dims, acc_shape, epilogue=None, jobs=()):
    n_in, n_out, nk = len(ins), len(outs), grid[2]
    j_ins, j_outs, j_sems, aliases = _job_plumbing(jobs, n_in, n_out)
    n_ji, n_jo = len(j_ins), len(j_outs)
    total = grid[0] * grid[1] * grid[2]
    n_acc = 0 if nk == 1 else 1

    def body(*refs):
        a_ref, b_ref = refs[0], refs[1]
        extras = refs[2:n_in]
        jin = refs[n_in:n_in + n_ji]
        out_refs = refs[n_in + n_ji:n_in + n_ji + n_out]
        jout = refs[n_in + n_ji + n_out:n_in + n_ji + n_out + n_jo]
        jsem = refs[n_in + n_ji + n_out + n_jo + n_acc:]
        k = pl.program_id(2)
        step = (pl.program_id(0) * grid[1] + pl.program_id(1)) * grid[2] + k
        if jobs:
            pl.when(step == 0)(lambda: _job_phase(jobs, "start", jin, jout, jsem))
            pl.when(step == (4 * total) // 5)(lambda: _job_phase(jobs, "mid", jin, jout, jsem))

        def finish(r):
            res = epilogue(r, *[e[...] for e in extras]) if epilogue is not None else (r,)
            for o, v in zip(out_refs, res):
                o[...] = v.astype(o.dtype)

        if len(b_ref.shape) == 2:
            prod = lax.dot_general(a_ref[...], b_ref[...], dims, preferred_element_type=F32)
        else:
            kb = a_ref.shape[1] // b_ref.shape[0]
            prod = sum(lax.dot_general(a_ref[:, g * kb:(g + 1) * kb], b_ref[g], dims, preferred_element_type=F32)
                       for g in range(b_ref.shape[0]))
        if nk == 1:
            finish(prod)
        else:
            acc = refs[n_in + n_ji + n_out + n_jo]

            @pl.when(k == 0)
            def _():
                acc[...] = prod

            @pl.when((k > 0) & (k < nk - 1))
            def _():
                acc[...] += prod

            @pl.when(k == nk - 1)
            def _():
                finish(acc[...] + prod)

        if jobs:
            pl.when(step == total - 1)(lambda: _job_phase(jobs, "end", jin, jout, jsem))

    sem = ("arbitrary",) * 3 if jobs else ("parallel", "parallel", "arbitrary")
    res = pl.pallas_call(
        body, name=name, grid=grid,
        in_specs=[s for _, s in ins] + [_ANY] * n_ji,
        out_specs=[s for _, s in outs] + [_ANY] * n_jo,
        out_shape=[s for s, _ in outs] + j_outs,
        scratch_shapes=[pltpu.VMEM(acc_shape, F32)] * n_acc + j_sems,
        input_output_aliases=aliases,
        compiler_params=_params(sem),
    )(*[a for a, _ in ins], *j_ins)
    return res


def _mm_nn(name, a, b, out_dtype, *, a_seg=None, out_seg=None, tm=MM_TM, tn=1024, tk=MM_TK, epilogue=None,
           extras=(), n_out=1, out_dtypes=None, jobs=()):
    if a_seg:
        _, M, ks = a.shape
        K = ks * a_seg
    else:
        M, K = a.shape
        ks = K
    N = b.shape[1]
    ns = N // out_seg if out_seg else N
    tm, tn, tk = _tile(M, tm), _tile(ns, tn), _tile(ks, tk)
    nks, nns = ks // tk, ns // tn
    grid = (M // tm, N // tn, K // tk)
    if a_seg:
        a_spec = pl.BlockSpec((None, tm, tk), lambda i, j, k: (k // nks, i, k % nks))
    else:
        a_spec = pl.BlockSpec((tm, tk), lambda i, j, k: (i, k))
    b_spec = pl.BlockSpec((tk, tn), lambda i, j, k: (k, j))
    if out_seg:
        o_spec = pl.BlockSpec((None, tm, tn), lambda i, j, k: (j // nns, i, j % nns))
        o_shape = (out_seg, M, ns)
    else:
        o_spec = pl.BlockSpec((tm, tn), lambda i, j, k: (i, j))
        o_shape = (M, N)
    dts = out_dtypes or [out_dtype] * n_out
    outs = [(jax.ShapeDtypeStruct(o_shape, dt), o_spec) for dt in dts]
    ins = [(a, a_spec), (b, b_spec)] + [(e, o_spec) for e in extras]
    return _matmul(name, ins, outs, grid, DIMS_NN, (tm, tn), epilogue, jobs)


def _mm_nt(name, a, b, out_dtype, *, a_seg=None, out_seg=None, tm=MM_TM, tn=1024, tk=MM_TK, epilogue=None, extras=(),
           jobs=()):
    if a_seg:
        _, M, ks = a.shape
        K = ks * a_seg
    else:
        M, K = a.shape
        ks = K
    N = b.shape[0]
    ns = N // out_seg if out_seg else N
    tm, tn, tk = _tile(M, tm), _tile(ns, tn), _tile(ks, tk)
    nks, nns = ks // tk, ns // tn
    grid = (M // tm, N // tn, K // tk)
    if a_seg:
        a_spec = pl.BlockSpec((None, tm, tk), lambda i, j, k: (k // nks, i, k % nks))
    else:
        a_spec = pl.BlockSpec((tm, tk), lambda i, j, k: (i, k))
    b_spec = pl.BlockSpec((tn, tk), lambda i, j, k: (j, k))
    if out_seg:
        o_spec = pl.BlockSpec((None, tm, tn), lambda i, j, k: (j // nns, i, j % nns))
        o_shape = (out_seg, M, ns)
    else:
        o_spec = pl.BlockSpec((tm, tn), lambda i, j, k: (i, j))
        o_shape = (M, N)
    outs = [(jax.ShapeDtypeStruct(o_shape, out_dtype), o_spec)]
    ins = [(a, a_spec), (b, b_spec)] + [(e, o_spec) for e in extras]
    return _matmul(name, ins, outs, grid, DIMS_NT, (tm, tn), epilogue, jobs)


def _mm_tn(name, a, b, out_dtype, *, a_seg=None, b_seg=None, out_blocks=None, tm=1024, tn=1024, tk=MM_TK_TOKENS,
           jobs=()):
    if a_seg:
        _, T, ms = a.shape
        M = ms * a_seg
    else:
        T, M = a.shape
        ms = M
    if b_seg:
        _, _, ns = b.shape
        N = ns * b_seg
    else:
        N = b.shape[1]
        ns = N
    nb_cols = N // out_blocks if out_blocks else N
    tm, tk = _tile(ms, tm), _tile(T, tk)
    tn = _tile(math.gcd(ns, nb_cols), tn)
    nms, nns, nbs = ms // tm, ns // tn, nb_cols // tn
    grid = (M // tm, N // tn, T // tk)
    if a_seg:
        a_spec = pl.BlockSpec((None, tk, tm), lambda i, j, k: (i // nms, k, i % nms))
    else:
        a_spec = pl.BlockSpec((tk, tm), lambda i, j, k: (k, i))
    if b_seg:
        b_spec = pl.BlockSpec((None, tk, tn), lambda i, j, k: (j // nns, k, j % nns))
    else:
        b_spec = pl.BlockSpec((tk, tn), lambda i, j, k: (k, j))
    if out_blocks:
        o_spec = pl.BlockSpec((None, tm, tn), lambda i, j, k: (j // nbs, i, j % nbs))
        o_shape = (out_blocks, M, nb_cols)
    else:
        o_spec = pl.BlockSpec((tm, tn), lambda i, j, k: (i, j))
        o_shape = (M, N)
    outs = [(jax.ShapeDtypeStruct(o_shape, out_dtype), o_spec)]
    return _matmul(name, [(a, a_spec), (b, b_spec)], outs, grid, DIMS_TN, (tm, tn), None, jobs)


def _rms(x):
    return lax.rsqrt(jnp.mean(x * x, axis=-1, keepdims=True) + NORM_EPS)


def _row_tile(T):
    return _tile(T, 256)


def _norm_fwd(name, x, g, jobs=()):
    T, D = x.shape
    tr = _row_tile(T)

    def body(x_ref, g_ref, h_ref):
        xv = x_ref[...]
        h_ref[...] = (xv * _rms(xv) * g_ref[...]).astype(h_ref.dtype)

    row = pl.BlockSpec((tr, D), lambda i: (i, 0))
    vec = pl.BlockSpec((1, D), lambda i: (0, 0))
    return _carry_call(body, name=name, steps=T // tr, in_specs=[row, vec], out_specs=[row],
                       out_shape=[jax.ShapeDtypeStruct((T, D), BF16)], scratch_shapes=[], args=(x, g), jobs=jobs)


def _resid_norm(name, x, br, g_post, g_next, jobs=()):
    T, D = x.shape
    tr = _row_tile(T)

    def body(x_ref, br_ref, gp_ref, gn_ref, xn_ref, h_ref):
        b = br_ref[...].astype(F32)
        xn = x_ref[...] + b * _rms(b) * gp_ref[...]
        xn_ref[...] = xn
        h_ref[...] = (xn * _rms(xn) * gn_ref[...]).astype(h_ref.dtype)

    row = pl.BlockSpec((tr, D), lambda i: (i, 0))
    vec = pl.BlockSpec((1, D), lambda i: (0, 0))
    return _carry_call(body, name=name, steps=T // tr, in_specs=[row, row, vec, vec], out_specs=[row, row],
                       out_shape=[jax.ShapeDtypeStruct((T, D), F32), jax.ShapeDtypeStruct((T, D), BF16)],
                       scratch_shapes=[], args=(x, br, g_post, g_next), jobs=jobs)


def _rms_bwd(xv, g, dyv):
    r = _rms(xv)
    xhat = xv * r
    gdy = dyv * g
    dx = r * (gdy - xhat * jnp.mean(gdy * xhat, axis=-1, keepdims=True))
    return dx, jnp.sum(dyv * xhat, axis=0, keepdims=True)


def _final_loss(name, x, br, g_post, target):
    T, D = x.shape
    tr = _row_tile(T)

    def body(x_ref, br_ref, gp_ref, t_ref, dy_ref, dbr_ref, dg_ref, ls_ref):
        b = br_ref[...].astype(F32)
        err = x_ref[...] + b * _rms(b) * gp_ref[...] - t_ref[...]
        dy = err * (1.0 / D)
        dy_ref[...] = dy
        dbr, dg = _rms_bwd(b, gp_ref[...], dy)
        dbr_ref[...] = dbr.astype(dbr_ref.dtype)

        @pl.when(pl.program_id(0) == 0)
        def _():
            ls_ref[...] = jnp.zeros_like(ls_ref)
            dg_ref[...] = jnp.zeros_like(dg_ref)

        ls_ref[...] += jnp.sum(err * err)
        dg_ref[...] += dg

    row = pl.BlockSpec((tr, D), lambda i: (i, 0))
    vec = pl.BlockSpec((1, D), lambda i: (0, 0))
    acc = pl.BlockSpec((SUBLANES, LANES), lambda i: (0, 0))
    return pl.pallas_call(body, name=name, grid=(T // tr,), in_specs=[row, row, vec, row],
                          out_specs=[row, row, vec, acc],
                          out_shape=[jax.ShapeDtypeStruct((T, D), F32), jax.ShapeDtypeStruct((T, D), BF16),
                                     jax.ShapeDtypeStruct((1, D), F32), jax.ShapeDtypeStruct((SUBLANES, LANES), F32)],
                          compiler_params=_params(("arbitrary",)))(x, br, g_post, target)


def _norm_bwd_pair(name, x, g_pre, dh, add, br, g_post):
    T, D = x.shape
    tr = _row_tile(T)

    def body(x_ref, gpre_ref, dh_ref, add_ref, br_ref, gpost_ref, dx_ref, dbr_ref, dgpre_ref, dgpost_ref):
        dx, dg_pre = _rms_bwd(x_ref[...], gpre_ref[...], dh_ref[...].astype(F32))
        dx = dx + add_ref[...]
        dx_ref[...] = dx
        dbr, dg_post = _rms_bwd(br_ref[...].astype(F32), gpost_ref[...], dx)
        dbr_ref[...] = dbr.astype(dbr_ref.dtype)

        @pl.when(pl.program_id(0) == 0)
        def _():
            dgpre_ref[...] = jnp.zeros_like(dgpre_ref)
            dgpost_ref[...] = jnp.zeros_like(dgpost_ref)

        dgpre_ref[...] += dg_pre
        dgpost_ref[...] += dg_post

    row = pl.BlockSpec((tr, D), lambda i: (i, 0))
    vec = pl.BlockSpec((1, D), lambda i: (0, 0))
    return pl.pallas_call(body, name=name, grid=(T // tr,), in_specs=[row, vec, row, row, row, vec],
                          out_specs=[row, row, vec, vec],
                          out_shape=[jax.ShapeDtypeStruct((T, D), F32), jax.ShapeDtypeStruct((T, D), BF16),
                                     jax.ShapeDtypeStruct((1, D), F32), jax.ShapeDtypeStruct((1, D), F32)],
                          compiler_params=_params(("arbitrary",)))(x, g_pre, dh, add, br, g_post)


def _norm_bwd(name, x, g, dy, add, out_dtype, jobs=()):
    T, D = x.shape
    tr = _row_tile(T)
    has_add = add is not None

    def body(*refs):
        if has_add:
            x_ref, g_ref, dy_ref, add_ref, dx_ref, dg_ref = refs
        else:
            x_ref, g_ref, dy_ref, dx_ref, dg_ref = refs
        dx, dg = _rms_bwd(x_ref[...].astype(F32), g_ref[...], dy_ref[...].astype(F32))
        if has_add:
            dx = dx + add_ref[...]
        dx_ref[...] = dx.astype(dx_ref.dtype)

        @pl.when(pl.program_id(0) == 0)
        def _():
            dg_ref[...] = jnp.zeros_like(dg_ref)

        dg_ref[...] += dg

    row = pl.BlockSpec((tr, D), lambda i: (i, 0))
    vec = pl.BlockSpec((1, D), lambda i: (0, 0))
    ins = [x, g, dy] + ([add] if has_add else [])
    specs = [row, vec, row] + ([row] if has_add else [])
    if not jobs:
        return pl.pallas_call(body, name=name, grid=(T // tr,), in_specs=specs, out_specs=[row, vec],
                              out_shape=[jax.ShapeDtypeStruct((T, D), out_dtype), jax.ShapeDtypeStruct((1, D), F32)],
                              compiler_params=_params(("arbitrary",)))(*ins)
    return _carry_call(body, name=name, steps=T // tr, in_specs=specs, out_specs=[row, vec],
                       out_shape=[jax.ShapeDtypeStruct((T, D), out_dtype), jax.ShapeDtypeStruct((1, D), F32)],
                       scratch_shapes=[], args=ins, jobs=jobs)


HALO = SUBLANES
TIME_CHUNK = 512


def _chunks(T):
    tc = min(TIME_CHUNK, T)
    assert T % tc == 0 and tc % SUBLANES == 0
    return [(t0, tc) for t0 in range(0, T, tc)]


def _log_sigmoid(x):
    return -(jnp.maximum(-x, 0.0) + jnp.log(1.0 + jnp.exp(-jnp.abs(x))))


def _sigmoid(x):
    return 0.5 * jnp.tanh(0.5 * x) + 0.5


def _one_minus_exp(x):
    series = -x * (1.0 + x * (0.5 + x * (1.0 / 6.0 + x * (1.0 / 24.0))))
    return jnp.where(x > -0.01, series, 1.0 - jnp.exp(x))


_GELU_C = math.sqrt(2.0 / math.pi)


def _gelu(x):
    return 0.5 * x * (1.0 + jnp.tanh(_GELU_C * (x + 0.044715 * x * x * x)))


def _gelu_grad(x):
    th = jnp.tanh(_GELU_C * (x + 0.044715 * x * x * x))
    return 0.5 * (1.0 + th) + 0.5 * x * (1.0 - th * th) * _GELU_C * (1.0 + 3.0 * 0.044715 * x * x)


def _tile_scan(a, b, reverse):
    rows = a.shape[0]
    pos = lax.broadcasted_iota(jnp.int32, a.shape, 0) & (SUBLANES - 1)
    for d in (1, 2, 4):
        if reverse:
            ok = pos < SUBLANES - d
            shift = rows - d
        else:
            ok = pos >= d
            shift = d
        a_sh = jnp.where(ok, pltpu.roll(a, shift, 0), 1.0)
        b_sh = jnp.where(ok, pltpu.roll(b, shift, 0), 0.0)
        b = a * b_sh + b
        a = a * a_sh
    return a, b


def _carry_scan(a_s, b_s, T, reverse):
    n = T // SUBLANES
    edge = 0 if reverse else SUBLANES - 1

    def step(j, carry):
        g = (n - 1 - j) if reverse else j
        r = pl.multiple_of(g * SUBLANES, SUBLANES)
        h = b_s[pl.ds(r, SUBLANES), :] + a_s[pl.ds(r, SUBLANES), :] * carry
        b_s[pl.ds(r, SUBLANES), :] = h
        return jnp.broadcast_to(h[edge:edge + 1, :], h.shape)

    lax.fori_loop(0, n, step, jnp.zeros((SUBLANES, a_s.shape[1]), F32))


def _seg_spec(T, seg, nblk):
    return pl.BlockSpec((None, T, LANES), lambda c: (seg, 0, c))


def _rows_to_tile(rows):
    idx = lax.broadcasted_iota(jnp.int32, (SUBLANES, LANES), 0)
    out = jnp.zeros((SUBLANES, LANES), F32)
    for k, r in enumerate(rows):
        out = jnp.where(idx == k, r, out)
    return out


def _mixer_a_fwd(proj, conv_a):
    _, T, C = proj.shape
    nblk = C // LANES
    chunks = _chunks(T)

    def body(bg_ref, cg_ref, ax_ref, w_ref, y_ref, p_s):
        p_s[pl.ds(0, HALO), :] = jnp.zeros((HALO, LANES), F32)
        for t0, tc in chunks:
            p_s[pl.ds(HALO + t0, tc), :] = cg_ref[pl.ds(t0, tc), :] * ax_ref[pl.ds(t0, tc), :]
        w = w_ref[...]
        for t0, tc in chunks:
            c = (w[2:3, :] * p_s[pl.ds(HALO + t0, tc), :] + w[1:2, :] * p_s[pl.ds(HALO + t0 - 1, tc), :]
                 + w[0:1, :] * p_s[pl.ds(HALO + t0 - 2, tc), :])
            y_ref[pl.ds(t0, tc), :] = (bg_ref[pl.ds(t0, tc), :] * c).astype(y_ref.dtype)

    return pl.pallas_call(
        body, name="mixer_a_fwd", grid=(nblk,),
        in_specs=[_seg_spec(T, 0, nblk), _seg_spec(T, 1, nblk), _seg_spec(T, 2, nblk),
                  pl.BlockSpec((3, LANES), lambda c: (0, c))],
        out_specs=_seg_spec(T, 0, nblk),
        out_shape=jax.ShapeDtypeStruct((2, T, C), BF16),
        scratch_shapes=[pltpu.VMEM((T + HALO, LANES), F32)],
        compiler_params=_params(("parallel",)))(proj, proj, proj, conv_a)


def _mixer_a_bwd(proj, conv_a, dy):
    _, T, C = proj.shape
    nblk = C // LANES
    chunks = _chunks(T)

    def body(bg_ref, cg_ref, ax_ref, w_ref, dy_ref, dp_ref, dw_ref, p_s, dc_s):
        p_s[pl.ds(0, HALO), :] = jnp.zeros((HALO, LANES), F32)
        dc_s[pl.ds(T, HALO), :] = jnp.zeros((HALO, LANES), F32)
        for t0, tc in chunks:
            p_s[pl.ds(HALO + t0, tc), :] = cg_ref[pl.ds(t0, tc), :] * ax_ref[pl.ds(t0, tc), :]
        w = w_ref[...]
        for t0, tc in chunks:
            c = (w[2:3, :] * p_s[pl.ds(HALO + t0, tc), :] + w[1:2, :] * p_s[pl.ds(HALO + t0 - 1, tc), :]
                 + w[0:1, :] * p_s[pl.ds(HALO + t0 - 2, tc), :])
            dyv = dy_ref[pl.ds(t0, tc), :]
            dp_ref[0, pl.ds(t0, tc), :] = (dyv * c).astype(dp_ref.dtype)
            dc_s[pl.ds(t0, tc), :] = dyv * bg_ref[pl.ds(t0, tc), :]
        dw = [jnp.zeros((1, LANES), F32) for _ in range(3)]
        for t0, tc in chunks:
            dc = dc_s[pl.ds(t0, tc), :]
            dpv = w[2:3, :] * dc + w[1:2, :] * dc_s[pl.ds(t0 + 1, tc), :] + w[0:1, :] * dc_s[pl.ds(t0 + 2, tc), :]
            dp_ref[1, pl.ds(t0, tc), :] = (dpv * ax_ref[pl.ds(t0, tc), :]).astype(dp_ref.dtype)
            dp_ref[2, pl.ds(t0, tc), :] = (dpv * cg_ref[pl.ds(t0, tc), :]).astype(dp_ref.dtype)
            for k in range(3):
                dw[k] = dw[k] + jnp.sum(dc * p_s[pl.ds(HALO + t0 - (2 - k), tc), :], axis=0, keepdims=True)
        dw_ref[...] = _rows_to_tile(dw)

    return pl.pallas_call(
        body, name="mixer_a_bwd", grid=(nblk,),
        in_specs=[_seg_spec(T, 0, nblk), _seg_spec(T, 1, nblk), _seg_spec(T, 2, nblk),
                  pl.BlockSpec((3, LANES), lambda c: (0, c)), _seg_spec(T, 0, nblk)],
        out_specs=[pl.BlockSpec((3, T, LANES), lambda c: (0, 0, c)),
                   pl.BlockSpec((None, SUBLANES, LANES), lambda c: (c, 0, 0))],
        out_shape=[jax.ShapeDtypeStruct((6, T, C), BF16), jax.ShapeDtypeStruct((nblk, SUBLANES, LANES), F32)],
        scratch_shapes=[pltpu.VMEM((T + HALO, LANES), F32), pltpu.VMEM((T + HALO, LANES), F32)],
        compiler_params=_params(("parallel",)))(proj, proj, proj, conv_a, dy)


def _rg_gates(xr, wa, ba, wx, bx, ls):
    xb = xr.astype(BF16)
    r = _sigmoid(jnp.dot(xb, wa, preferred_element_type=F32) + ba)
    i = _sigmoid(jnp.dot(xb, wx, preferred_element_type=F32) + bx)
    log_a = LRU_C * r * ls
    a = jnp.exp(log_a)
    mult = jnp.sqrt(_one_minus_exp(2.0 * log_a))
    return r, i, a, mult


def _conv4(xh_s, cw, bias, t0, tc):
    return (cw[3:4, :] * xh_s[pl.ds(HALO + t0, tc), :] + cw[2:3, :] * xh_s[pl.ds(HALO + t0 - 1, tc), :]
            + cw[1:2, :] * xh_s[pl.ds(HALO + t0 - 2, tc), :] + cw[0:1, :] * xh_s[pl.ds(HALO + t0 - 3, tc), :] + bias)


def _mixer_b_specs(T, nblk):
    vec = pl.BlockSpec((1, LANES), lambda c: (0, c))
    mat = pl.BlockSpec((None, LANES, LANES), lambda c: (c, 0, 0))
    return [_seg_spec(T, 3, nblk), _seg_spec(T, 4, nblk), pl.BlockSpec((4, LANES), lambda c: (0, c)),
            vec, mat, vec, mat, vec, vec]


def _mixer_b_fwd(name, proj, conv_b, bias, wa, ba, wx, bx, lam, y, jobs=()):
    _, T, C = proj.shape
    nblk = C // LANES
    chunks = _chunks(T)

    def body(gate_ref, x_ref, cw_ref, cb_ref, wa_ref, ba_ref, wx_ref, bx_ref, lam_ref, y_in, y_ref, xh_s, a_s, b_s):
        xh_s[pl.ds(0, HALO), :] = jnp.zeros((HALO, LANES), F32)
        for t0, tc in chunks:
            xh_s[pl.ds(HALO + t0, tc), :] = x_ref[pl.ds(t0, tc), :]
        cw, bias_v = cw_ref[...], cb_ref[...]
        ls = _log_sigmoid(lam_ref[...])
        for t0, tc in chunks:
            xr = _conv4(xh_s, cw, bias_v, t0, tc)
            r, i, a, mult = _rg_gates(xr, wa_ref[...], ba_ref[...], wx_ref[...], bx_ref[...], ls)
            ac, hc = _tile_scan(a, mult * i * xr, reverse=False)
            a_s[pl.ds(t0, tc), :] = ac
            b_s[pl.ds(t0, tc), :] = hc
        _carry_scan(a_s, b_s, T, reverse=False)
        for t0, tc in chunks:
            y_ref[pl.ds(t0, tc), :] = (b_s[pl.ds(t0, tc), :] * _gelu(gate_ref[pl.ds(t0, tc), :])).astype(y_ref.dtype)

    return _carry_call(
        body, name=name, steps=nblk, in_specs=_mixer_b_specs(T, nblk) + [_ANY],
        out_specs=[_seg_spec(T, 1, nblk)], out_shape=[jax.ShapeDtypeStruct(y.shape, y.dtype)],
        scratch_shapes=[pltpu.VMEM((T + HALO, LANES), F32), pltpu.VMEM((T, LANES), F32), pltpu.VMEM((T, LANES), F32)],
        args=(proj, proj, conv_b, bias, wa, ba, wx, bx, lam, y), jobs=jobs, aliases={9: 0})


_ROW_CONV, _ROW_BIAS, _ROW_BA, _ROW_BX, _ROW_LAM = 0, 4, 5, 6, 7


def _mixer_b_bwd(name, proj, conv_b, bias, wa, ba, wx, bx, lam, dy, dproj, jobs=()):
    _, T, C = proj.shape
    nblk = C // LANES
    chunks = _chunks(T)

    def body(gate_ref, x_ref, cw_ref, cb_ref, wa_ref, ba_ref, wx_ref, bx_ref, lam_ref, dy_ref, dp_in,
             dp_ref, sm_ref, dwa_ref, dwx_ref, xh_s, xr_s, r_s, i_s, a_s, h_s, sa_s, sb_s, dx_s):
        zero_halo = jnp.zeros((HALO, LANES), F32)
        xh_s[pl.ds(0, HALO), :] = zero_halo
        h_s[pl.ds(0, HALO), :] = zero_halo
        a_s[pl.ds(T, HALO), :] = zero_halo
        dx_s[pl.ds(T, HALO), :] = zero_halo
        for t0, tc in chunks:
            xh_s[pl.ds(HALO + t0, tc), :] = x_ref[pl.ds(t0, tc), :]
        cw, bias_v = cw_ref[...], cb_ref[...]
        lam_v = lam_ref[...]
        ls = _log_sigmoid(lam_v)
        wa_v, wx_v, ba_v, bx_v = wa_ref[...], wx_ref[...], ba_ref[...], bx_ref[...]
        for t0, tc in chunks:
            xr = _conv4(xh_s, cw, bias_v, t0, tc)
            r, i, a, mult = _rg_gates(xr, wa_v, ba_v, wx_v, bx_v, ls)
            xr_s[pl.ds(t0, tc), :] = xr
            r_s[pl.ds(t0, tc), :] = r
            i_s[pl.ds(t0, tc), :] = i
            a_s[pl.ds(t0, tc), :] = a
            ac, hc = _tile_scan(a, mult * i * xr, reverse=False)
            sa_s[pl.ds(t0, tc), :] = ac
            sb_s[pl.ds(t0, tc), :] = hc
        _carry_scan(sa_s, sb_s, T, reverse=False)
        for t0, tc in chunks:
            h_s[pl.ds(HALO + t0, tc), :] = sb_s[pl.ds(t0, tc), :]
        for t0, tc in chunks:
            gv = gate_ref[pl.ds(t0, tc), :]
            dyv = dy_ref[pl.ds(t0, tc), :]
            dp_ref[0, pl.ds(t0, tc), :] = (dyv * h_s[pl.ds(HALO + t0, tc), :] * _gelu_grad(gv)).astype(dp_ref.dtype)
            ac, gc = _tile_scan(a_s[pl.ds(t0 + 1, tc), :], dyv * _gelu(gv), reverse=True)
            sa_s[pl.ds(t0, tc), :] = ac
            sb_s[pl.ds(t0, tc), :] = gc
        _carry_scan(sa_s, sb_s, T, reverse=True)
        acc = {k: jnp.zeros((1, LANES), F32) for k in ("bias", "ba", "bx", "lam")}
        dwa = jnp.zeros((LANES, LANES), F32)
        dwx = jnp.zeros((LANES, LANES), F32)
        for t0, tc in chunks:
            dht = sb_s[pl.ds(t0, tc), :]
            xr, r, i, a = xr_s[pl.ds(t0, tc), :], r_s[pl.ds(t0, tc), :], i_s[pl.ds(t0, tc), :], a_s[pl.ds(t0, tc), :]
            mult = jnp.sqrt(_one_minus_exp(2.0 * LRU_C * r * ls))
            da = dht * h_s[pl.ds(HALO + t0 - 1, tc), :]
            dmult = dht * i * xr
            di = dht * mult * xr
            dlog_a = da * a - dmult * a * a / mult
            dpa = dlog_a * (LRU_C * ls) * r * (1.0 - r)
            dpx = di * i * (1.0 - i)
            acc["lam"] = acc["lam"] + jnp.sum(dlog_a * r, axis=0, keepdims=True)
            acc["ba"] = acc["ba"] + jnp.sum(dpa, axis=0, keepdims=True)
            acc["bx"] = acc["bx"] + jnp.sum(dpx, axis=0, keepdims=True)
            xb, dpab, dpxb = xr.astype(BF16), dpa.astype(BF16), dpx.astype(BF16)
            dwa = dwa + lax.dot_general(xb, dpab, DIMS_TN, preferred_element_type=F32)
            dwx = dwx + lax.dot_general(xb, dpxb, DIMS_TN, preferred_element_type=F32)
            dxr = (dht * mult * i + lax.dot_general(dpab, wa_v, DIMS_NT, preferred_element_type=F32)
                   + lax.dot_general(dpxb, wx_v, DIMS_NT, preferred_element_type=F32))
            acc["bias"] = acc["bias"] + jnp.sum(dxr, axis=0, keepdims=True)
            dx_s[pl.ds(t0, tc), :] = dxr
        dcw = [jnp.zeros((1, LANES), F32) for _ in range(4)]
        for t0, tc in chunks:
            dxr = dx_s[pl.ds(t0, tc), :]
            dxin = (cw[3:4, :] * dxr + cw[2:3, :] * dx_s[pl.ds(t0 + 1, tc), :] + cw[1:2, :] * dx_s[pl.ds(t0 + 2, tc), :]
                    + cw[0:1, :] * dx_s[pl.ds(t0 + 3, tc), :])
            dp_ref[1, pl.ds(t0, tc), :] = dxin.astype(dp_ref.dtype)
            for k in range(4):
                dcw[k] = dcw[k] + jnp.sum(dxr * xh_s[pl.ds(HALO + t0 - (3 - k), tc), :], axis=0, keepdims=True)
        dlam = acc["lam"] * LRU_C * _sigmoid(-lam_v)
        sm_ref[...] = _rows_to_tile(dcw + [acc["bias"], acc["ba"], acc["bx"], dlam])
        dwa_ref[...] = dwa
        dwx_ref[...] = dwx

    big = lambda halo: pltpu.VMEM((T + halo, LANES), F32)
    mat = pl.BlockSpec((None, LANES, LANES), lambda c: (c, 0, 0))
    return _carry_call(
        body, name=name, steps=nblk,
        in_specs=_mixer_b_specs(T, nblk) + [_seg_spec(T, 1, nblk), _ANY],
        out_specs=[pl.BlockSpec((3, T, LANES), lambda c: (1, 0, c)),
                   pl.BlockSpec((None, SUBLANES, LANES), lambda c: (c, 0, 0)), mat, mat],
        out_shape=[jax.ShapeDtypeStruct(dproj.shape, dproj.dtype), jax.ShapeDtypeStruct((nblk, SUBLANES, LANES), F32),
                   jax.ShapeDtypeStruct((nblk, LANES, LANES), F32), jax.ShapeDtypeStruct((nblk, LANES, LANES), F32)],
        scratch_shapes=[big(HALO), big(0), big(0), big(0), big(HALO), big(HALO), big(0), big(0), big(HALO)],
        args=(proj, proj, conv_b, bias, wa, ba, wx, bx, lam, dy, dproj), jobs=jobs, aliases={10: 0})


ATT_BLOCK = 128
ATT_GROUP = 3
ATT_TILE = ATT_BLOCK * ATT_GROUP
ATT_UNDERFLOW = -110.0
ATT_UNVISITED = -1e30


def _split_dot(x, m):
    hi = x.astype(BF16)
    lo = (x - hi.astype(F32)).astype(BF16)
    return jnp.dot(hi, m, preferred_element_type=F32) + jnp.dot(lo, m, preferred_element_type=F32)


def _sub(x, j):
    return x[:, j * ATT_BLOCK:(j + 1) * ATT_BLOCK]


def _stack_rows(x):
    return jnp.concatenate([_sub(x, j) for j in range(ATT_GROUP)], axis=0)


def _unstack_rows(x, offsets):
    return jnp.concatenate([x[j * ATT_BLOCK:(j + 1) * ATT_BLOCK, :] + offsets[j] for j in range(ATT_GROUP)], axis=1)


def _att_tile(q, k_ref, q0, qb, it, scale):
    hi = (qb + 1 - ATT_GROUP * it) * ATT_BLOCK
    k0 = pl.multiple_of(jnp.maximum(hi - ATT_TILE, 0), ATT_BLOCK)
    kt = k_ref[pl.ds(k0, ATT_TILE), :]
    z = lax.dot_general(q, kt, DIMS_NT, preferred_element_type=F32) * scale
    key = k0 + lax.broadcasted_iota(jnp.int32, z.shape, 1)
    row = q0 + lax.broadcasted_iota(jnp.int32, z.shape, 0)
    mask = (key < row) & (key < hi)
    n = jnp.where(mask, -(jnp.maximum(z, 0.0) + jnp.log(1.0 + jnp.exp(-jnp.abs(z)))), 0.0)
    return k0, kt, z, mask, n


def _suffix_in_tile(n, upper, run):
    rs = [jnp.sum(_sub(n, j), axis=1, keepdims=True) for j in range(ATT_GROUP)]
    offs = [None] * ATT_GROUP
    offs[ATT_GROUP - 1] = run
    for j in range(ATT_GROUP - 2, -1, -1):
        offs[j] = offs[j + 1] + rs[j + 1]
    return _unstack_rows(_split_dot(_stack_rows(n), upper), offs), offs[0] + rs[0]


def _head_spec(T, seg, heads):
    return pl.BlockSpec((None, T, ATT_HEAD_DIM), lambda h: (seg, 0, h))


def _attention_fwd(name, qkv, jobs=()):
    _, T, D = qkv.shape
    heads = D // ATT_HEAD_DIM
    nq = T // ATT_BLOCK
    assert nq <= LANES and T >= ATT_TILE and nq % 2 == 0
    scale = 1.0 / math.sqrt(ATT_HEAD_DIM)

    def body(q_ref, k_ref, v_ref, o_ref, r_ref, acc_s, run_s):
        rr = lax.broadcasted_iota(jnp.int32, (ATT_BLOCK, ATT_BLOCK), 0)
        cc = lax.broadcasted_iota(jnp.int32, (ATT_BLOCK, ATT_BLOCK), 1)
        upper = jnp.where(rr > cc, 1.0, 0.0).astype(BF16)
        lane = lax.broadcasted_iota(jnp.int32, (ATT_BLOCK, LANES), 1)

        def tile(slot, qb, q0, q, it, first):
            k0, _, z, mask, n = _att_tile(q, k_ref, q0, qb, it, scale)
            run = jnp.zeros((ATT_BLOCK, LANES), F32) if first else run_s[slot]
            suffix, run_next = _suffix_in_tile(n, upper, run)
            w = jnp.where(mask, jnp.exp(z + n + suffix), 0.0)
            pv = jnp.dot(w.astype(BF16), v_ref[pl.ds(k0, ATT_TILE), :], preferred_element_type=F32)
            if first:
                acc_s[slot] = pv
            else:
                acc_s[slot] += pv
                r_ref[pl.ds(q0, ATT_BLOCK), :] = jnp.where(lane == it, run, r_ref[pl.ds(q0, ATT_BLOCK), :])
            run_s[slot] = run_next
            return jnp.max(run_next) >= ATT_UNDERFLOW

        def pair_loop(p, _):
            blocks = []
            for slot in range(2):
                qb = 2 * p + slot
                q0 = pl.multiple_of(qb * ATT_BLOCK, ATT_BLOCK)
                r_ref[pl.ds(q0, ATT_BLOCK), :] = jnp.where(lane == 0, 0.0, ATT_UNVISITED)
                blocks.append((qb, q0, q_ref[pl.ds(q0, ATT_BLOCK), :]))
            go = [tile(slot, *blocks[slot], 0, True) for slot in range(2)]
            for slot in range(2):
                qb, q0, q = blocks[slot]
                n_tiles = (qb + ATT_GROUP) // ATT_GROUP
                lax.while_loop(lambda c: (c[0] < n_tiles) & c[1],
                               lambda c: (c[0] + 1, tile(slot, qb, q0, q, c[0], False)), (jnp.int32(1), go[slot]))
                o_ref[pl.ds(q0, ATT_BLOCK), :] = acc_s[slot].astype(o_ref.dtype)
            return 0

        lax.fori_loop(0, nq // 2, pair_loop, 0)

    return _carry_call(
        body, name=name, steps=heads,
        in_specs=[_head_spec(T, 0, heads), _head_spec(T, 1, heads), _head_spec(T, 2, heads)],
        out_specs=[pl.BlockSpec((T, ATT_HEAD_DIM), lambda h: (0, h)), pl.BlockSpec((None, T, LANES), lambda h: (h, 0, 0))],
        out_shape=[jax.ShapeDtypeStruct((T, D), BF16), jax.ShapeDtypeStruct((heads, T, LANES), F32)],
        scratch_shapes=[pltpu.VMEM((2, ATT_BLOCK, ATT_HEAD_DIM), F32), pltpu.VMEM((2, ATT_BLOCK, LANES), F32)],
        args=(qkv, qkv, qkv), jobs=jobs)


def _attention_bwd(name, qkv, do, rmat, jobs=()):
    _, T, D = qkv.shape
    heads = D // ATT_HEAD_DIM
    nq = T // ATT_BLOCK
    scale = 1.0 / math.sqrt(ATT_HEAD_DIM)

    def body(q_ref, k_ref, v_ref, do_ref, r_ref, dqkv_ref, dk_s, dv_s, dq_s, left_s):
        rr = lax.broadcasted_iota(jnp.int32, (ATT_BLOCK, ATT_BLOCK), 0)
        cc = lax.broadcasted_iota(jnp.int32, (ATT_BLOCK, ATT_BLOCK), 1)
        upper = jnp.where(rr > cc, 1.0, 0.0).astype(BF16)
        lower = jnp.where(rr < cc, 1.0, 0.0).astype(BF16)
        lane = lax.broadcasted_iota(jnp.int32, (ATT_BLOCK, LANES), 1)
        dk_s[...] = jnp.zeros_like(dk_s)
        dv_s[...] = jnp.zeros_like(dv_s)

        def tile(slot, qb, q0, q, dov, it, first):
            k0, kt, z, mask, n = _att_tile(q, k_ref, q0, qb, it, scale)
            vt = v_ref[pl.ds(k0, ATT_TILE), :]
            run = jnp.sum(jnp.where(lane == it, r_ref[pl.ds(q0, ATT_BLOCK), :], 0.0), axis=1, keepdims=True)
            suffix, _ = _suffix_in_tile(n, upper, run)
            s = z + n
            w = jnp.where(mask, jnp.exp(s + suffix), 0.0)
            e = w * lax.dot_general(dov, vt, DIMS_NT, preferred_element_type=F32)
            es = [jnp.sum(_sub(e, g), axis=1, keepdims=True) for g in range(ATT_GROUP)]
            pre = [jnp.zeros((ATT_BLOCK, LANES), F32) if first else left_s[slot]]
            for g in range(ATT_GROUP):
                pre.append(pre[g] + es[g])
            before = _unstack_rows(_split_dot(_stack_rows(e), lower), pre)
            sig = jnp.exp(s)
            dz = (jnp.where(mask, e * (1.0 - sig) - before * sig, 0.0) * scale).astype(BF16)
            dq = jnp.dot(dz, kt, preferred_element_type=F32)
            if first:
                dq_s[slot] = dq
            else:
                dq_s[slot] += dq
            dk_s[pl.ds(k0, ATT_TILE), :] += lax.dot_general(dz, q, DIMS_TN, preferred_element_type=F32)
            dv_s[pl.ds(k0, ATT_TILE), :] += lax.dot_general(w.astype(BF16), dov, DIMS_TN, preferred_element_type=F32)
            left_s[slot] = pre[ATT_GROUP]

        def pair_loop(p, _):
            blocks = []
            for slot in range(2):
                qb = 2 * p + slot
                q0 = pl.multiple_of(qb * ATT_BLOCK, ATT_BLOCK)
                n_tiles = (qb + ATT_GROUP) // ATT_GROUP
                seen = ((jnp.max(r_ref[pl.ds(q0, ATT_BLOCK), :], axis=0, keepdims=True) > 0.5 * ATT_UNVISITED)
                        & (lane[0:1, :] < n_tiles))
                n_visited = jnp.maximum(jnp.sum(jnp.where(seen, 1.0, 0.0)).astype(jnp.int32), 1)
                blocks.append((qb, q0, q_ref[pl.ds(q0, ATT_BLOCK), :], do_ref[pl.ds(q0, ATT_BLOCK), :], n_visited))
            for slot in range(2):
                qb, q0, q, dov, n_visited = blocks[slot]
                tile(slot, qb, q0, q, dov, n_visited - 1, True)
            for slot in range(2):
                qb, q0, q, dov, n_visited = blocks[slot]
                lax.fori_loop(1, n_visited, lambda j, c: (tile(slot, qb, q0, q, dov, n_visited - 1 - j, False), c)[1], 0)
                dqkv_ref[0, pl.ds(q0, ATT_BLOCK), :] = dq_s[slot].astype(dqkv_ref.dtype)
            return 0

        lax.fori_loop(0, nq // 2, pair_loop, 0)
        dqkv_ref[1, :, :] = dk_s[...].astype(dqkv_ref.dtype)
        dqkv_ref[2, :, :] = dv_s[...].astype(dqkv_ref.dtype)

    return _carry_call(
        body, name=name, steps=heads,
        in_specs=[_head_spec(T, 0, heads), _head_spec(T, 1, heads), _head_spec(T, 2, heads),
                  pl.BlockSpec((T, ATT_HEAD_DIM), lambda h: (0, h)), pl.BlockSpec((None, T, LANES), lambda h: (h, 0, 0))],
        out_specs=[pl.BlockSpec((3, T, ATT_HEAD_DIM), lambda h: (0, 0, h))],
        out_shape=[jax.ShapeDtypeStruct((3, T, D), BF16)],
        scratch_shapes=[pltpu.VMEM((T, ATT_HEAD_DIM), F32), pltpu.VMEM((T, ATT_HEAD_DIM), F32),
                        pltpu.VMEM((2, ATT_BLOCK, ATT_HEAD_DIM), F32), pltpu.VMEM((2, ATT_BLOCK, LANES), F32)],
        args=(qkv, qkv, qkv, do, rmat), jobs=jobs)


def _block_diag_pairs(w):
    h = w.shape[0]
    wp = w.reshape(h // 2, 2, RG_HEAD_DIM, RG_HEAD_DIM)
    z = jnp.zeros_like(wp[:, 0])
    top = jnp.concatenate([wp[:, 0], z], axis=2)
    bot = jnp.concatenate([z, wp[:, 1]], axis=2)
    return jnp.concatenate([top, bot], axis=1)


def _diag_pairs(g):
    n = g.shape[0]
    a = g[:, :RG_HEAD_DIM, :RG_HEAD_DIM]
    b = g[:, RG_HEAD_DIM:, RG_HEAD_DIM:]
    return jnp.stack([a, b], axis=1).reshape(2 * n, RG_HEAD_DIM, RG_HEAD_DIM)


class _Weights:
    def __init__(self, full, shards=None, plan=None):
        self.full, self.shards, self.plan = dict(full), shards or {}, plan or {}
        self.partial, self.rows = {}, {}

    def __getitem__(self, name):
        return self.full[name]

    def jobs(self, call):
        return [_gather_job(self.shards[n], self.partial.get(n), lo, hi, parts)
                for n, lo, hi, parts in self.plan.get(call, ())]

    def deliver(self, call, outs):
        for (n, lo, hi, parts), g in zip(self.plan.get(call, ()), outs):
            self.partial[n] = g
            self.rows[n] = self.rows.get(n, 0) + hi - lo
            if self.rows[n] == parts:
                self.full[n] = _gathered_layout(n, g)


def _gathered_layout(name, g):
    if name in ("w_in", "w_qkv", "w_out", "w_o"):
        return g.reshape(g.shape[0] * g.shape[1], g.shape[2])
    return g


class _Grads:
    def __init__(self, lands=None, plan=None):
        self.lands, self.plan = dict(lands) if lands else None, plan or {}
        self.ready, self.sent = {}, {}

    def put(self, name, arr):
        self.ready[name] = arr

    def jobs(self, call):
        if self.lands is None:
            return []
        return [_exchange_job(self.ready[n], self.lands[n], lo, hi, parts) for n, lo, hi, parts in self.plan.get(call, ())]

    def deliver(self, call, outs):
        for (n, lo, hi, parts), o in zip(self.plan.get(call, ()), outs):
            assert self.sent.get(n, (0, parts)) == (lo, parts), (call, n)
            self.lands[n] = o
            self.sent[n] = (hi, parts)

    def flush(self, name):
        if self.lands is None:
            return
        rest = []
        for n in self.ready:
            lo, parts = self.sent.get(n, (0, 1))
            if lo < parts:
                rest.append((n, lo, parts, parts))
        if rest:
            outs = _run_jobs(name, [_exchange_job(self.ready[n], self.lands[n], lo, hi, parts) for n, lo, hi, parts in rest])
            for (n, _, hi, parts), o in zip(rest, outs):
                self.lands[n] = o
                self.sent[n] = (hi, parts)


def _mlp_fwd(tag, h, wts, run):
    T, D = h.shape
    w_up = wts["up" + tag]
    fb = w_up.shape[2]
    F = fb * N_DEV
    tm, tn, tk = _tile(T, MM_TM), _tile(fb, 1024), _tile(D, MM_TK)
    nb = fb // tn

    def up_epilogue(u):
        r = jnp.maximum(u, 0.0)
        return u, r * r

    o_spec = pl.BlockSpec((tm, tn), lambda i, j, k: (i, j))
    u, act = run(
        _matmul, f"mlp_up_l{tag}",
        [(h, pl.BlockSpec((tm, tk), lambda i, j, k: (i, k))),
         (w_up, pl.BlockSpec((None, tk, tn), lambda i, j, k: (j // nb, k, j % nb)))],
        [(jax.ShapeDtypeStruct((T, F), BF16), o_spec), (jax.ShapeDtypeStruct((T, F), BF16), o_spec)],
        (T // tm, F // tn, D // tk), DIMS_NN, (tm, tn), up_epilogue, n_main=2)
    w_down = wts["down" + tag].reshape(F, D)
    m = run(_mm_nn, f"mlp_down_l{tag}", act, w_down, BF16)
    return u, act, m


def _mlp_bwd(tag, h, u, act, dm, wts, grads, run):
    T, D = h.shape
    w_up, w_down = wts["up" + tag], wts["down" + tag]
    fb = w_up.shape[2]
    F = fb * N_DEV
    grads.put("down" + tag, run(_mm_tn, f"mlp_down_dw_l{tag}", act, dm, BF16).reshape(N_DEV, fb, D))
    tm, tn, tk = _tile(T, MM_TM), _tile(fb, 1024), _tile(D, MM_TK)
    nb = fb // tn
    o_spec = pl.BlockSpec((tm, tn), lambda i, j, k: (i, j))
    du = run(
        _matmul, f"mlp_down_dx_l{tag}",
        [(dm, pl.BlockSpec((tm, tk), lambda i, j, k: (i, k))),
         (w_down, pl.BlockSpec((None, tn, tk), lambda i, j, k: (j // nb, j % nb, k))),
         (u, o_spec)],
        [(jax.ShapeDtypeStruct((T, F), BF16), o_spec)],
        (T // tm, F // tn, D // tk), DIMS_NT, (tm, tn),
        lambda r, uv: (r * (2.0 * jnp.maximum(uv.astype(F32), 0.0)),))
    grads.put("up" + tag, run(_mm_tn, f"mlp_up_dw_l{tag}", h, du, BF16, out_blocks=N_DEV))
    tn2 = _tile(D, 1024)
    pair = 2 if MM_TK >= 2 * fb else 1
    return run(
        _matmul, f"mlp_up_dx_l{tag}",
        [(du, pl.BlockSpec((tm, pair * fb), lambda i, j, k: (i, k))),
         (w_up, pl.BlockSpec((pair, tn2, fb), lambda i, j, k: (k, j, 0)))],
        [(jax.ShapeDtypeStruct((T, D), BF16), pl.BlockSpec((tm, tn2), lambda i, j, k: (i, j)))],
        (T // tm, D // tn2, N_DEV // pair), DIMS_NT, (tm, tn2), None)


def _local_step(x, target, gains, conv_a, conv_b, conv_b_bias, rg_w_a, rg_b_a, rg_w_x, rg_b_x, rg_lambda, wts, grads):
    T, D = x.shape
    g = lambda l, i: gains[l, i][None, :]
    wa_p = _block_diag_pairs(rg_w_a).astype(BF16)
    wx_p = _block_diag_pairs(rg_w_x).astype(BF16)

    def run(fn, name, *args, n_main=1, **kw):
        jw, jg = wts.jobs(name), grads.jobs(name)
        res = fn(name, *args, jobs=jw + jg, **kw)
        main, jo = res[:n_main], res[n_main:]
        wts.deliver(name, jo[:len(jw)])
        grads.deliver(name, jo[len(jw):])
        return main[0] if n_main == 1 else main

    h0 = run(_norm_fwd, "norm_in", x, g(0, 0))
    proj = run(_mm_nt, "w_in_fwd", h0, wts["w_in"], F32, out_seg=5)
    y = run(_mixer_b_fwd, "mixer_b_fwd", proj, conv_b, conv_b_bias, wa_p, rg_b_a, wx_p, rg_b_x, rg_lambda,
            _mixer_a_fwd(proj, conv_a))
    mix0 = run(_mm_nn, "w_out_fwd", y, wts["w_out"], BF16, a_seg=2)
    x1, h1 = run(_resid_norm, "resid_mix0", x, mix0, g(0, 1), g(0, 2), n_main=2)
    u0, act0, m0 = _mlp_fwd("0", h1, wts, run)
    x2, h2 = run(_resid_norm, "resid_mlp0", x1, m0, g(0, 3), g(1, 0), n_main=2)
    qkv = run(_mm_nt, "w_qkv_fwd", h2, wts["w_qkv"], BF16, out_seg=3)
    o, rmat = run(_attention_fwd, "attention_fwd", qkv, n_main=2)
    mix1 = run(_mm_nn, "w_o_fwd", o, wts["w_o"], BF16)
    x3, h3 = run(_resid_norm, "resid_mix1", x2, mix1, g(1, 1), g(1, 2), n_main=2)
    u1, act1, m1 = _mlp_fwd("1", h3, wts, run)
    dx4, dm1, dg13, sq = _final_loss("loss", x3, m1, g(1, 3), target)

    dh3 = _mlp_bwd("1", h3, u1, act1, dm1, wts, grads, run)
    dx3, dmix1, dg12, dg11 = _norm_bwd_pair("norm_bwd_x3", x3, g(1, 2), dh3, dx4, mix1, g(1, 1))
    grads.put("w_o", run(_mm_tn, "w_o_dw", o, dmix1, BF16).reshape(N_DEV, D // N_DEV, D))
    do = run(_mm_nt, "w_o_dx", dmix1, wts["w_o"], BF16)
    dqkv = run(_attention_bwd, "attention_bwd", qkv, do, rmat)
    grads.put("w_qkv", run(_mm_tn, "w_qkv_dw", dqkv, h2, BF16, a_seg=3).reshape(N_DEV, 3 * D // N_DEV, D))
    dh2 = run(_mm_nn, "w_qkv_dx", dqkv, wts["w_qkv"], BF16, a_seg=3)
    dx2, dm0, dg10, dg03 = _norm_bwd_pair("norm_bwd_x2", x2, g(1, 0), dh2, dx3, m0, g(0, 3))
    dh1 = _mlp_bwd("0", h1, u0, act0, dm0, wts, grads, run)
    dx1, dmix0, dg02, dg01 = _norm_bwd_pair("norm_bwd_x1", x1, g(0, 2), dh1, dx2, mix0, g(0, 1))
    grads.put("w_out", run(_mm_tn, "w_out_dw", y, dmix0, BF16, a_seg=2).reshape(N_DEV, D // N_DEV, D))
    dy = run(_mm_nt, "w_out_dx", dmix0, wts["w_out"], F32, out_seg=2)
    dproj_a, dconv_a = _mixer_a_bwd(proj, conv_a, dy)
    dproj, sm_b, dwa_p, dwx_p = run(_mixer_b_bwd, "mixer_b_bwd", proj, conv_b, conv_b_bias, wa_p, rg_b_a, wx_p, rg_b_x,
                                    rg_lambda, dy, dproj_a, n_main=4)
    grads.put("w_in", run(_mm_tn, "w_in_dw", dproj, h0, BF16, a_seg=5).reshape(N_DEV, 5 * D // (2 * N_DEV), D))
    dh0 = run(_mm_nn, "w_in_dx", dproj, wts["w_in"], BF16, a_seg=5)
    dx0, dg00 = run(_norm_bwd, "norm_bwd_x0", x, g(0, 0), dh0, dx1, F32, n_main=2)

    C = D // 2
    lanes_to_vec = lambda t, row: t[:, row, :].reshape(1, C)
    small = {
        "norm_gains": jnp.concatenate([dg00, dg01, dg02, dg03, dg10, dg11, dg12, dg13], axis=0).reshape(2, 4, D),
        "conv_a": jnp.transpose(dconv_a[:, :3, :], (1, 0, 2)).reshape(3, C),
        "conv_b": jnp.transpose(sm_b[:, :4, :], (1, 0, 2)).reshape(4, C),
        "conv_b_bias": lanes_to_vec(sm_b, _ROW_BIAS),
        "rg_w_a": _diag_pairs(dwa_p),
        "rg_b_a": lanes_to_vec(sm_b, _ROW_BA),
        "rg_w_x": _diag_pairs(dwx_p),
        "rg_b_x": lanes_to_vec(sm_b, _ROW_BX),
        "rg_lambda": lanes_to_vec(sm_b, _ROW_LAM),
    }
    return sq[0, 0], dx0, small


def _my_index():
    return 4 * lax.axis_index("x") + 2 * lax.axis_index("y") + lax.axis_index("c")


def _peers():
    x, y, c = lax.axis_index("x"), lax.axis_index("y"), lax.axis_index("c")
    out = []
    for k in range(1, N_DEV):
        px = x ^ ((k >> 2) & 1)
        py = y ^ ((k >> 1) & 1)
        pc = c ^ (k & 1)
        out.append(((px, py, pc), 4 * px + 2 * py + pc))
    return out


GATHER_FIRST = ()
GATHER_PLAN = {
    "norm_in": (("w_in", 0, 1, 1),),
    "w_in_fwd": (("w_out", 0, 1, 1), ("up0", 0, 1, 4)),
    "mixer_b_fwd": (("up0", 1, 3, 4),),
    "w_out_fwd": (("up0", 3, 4, 4),),
    "resid_mix0": (("down0", 0, 1, 4),),
    "mlp_up_l0": (("down0", 1, 4, 4),),
    "mlp_down_l0": (("w_qkv", 0, 1, 1),),
    "resid_mlp0": (("up1", 0, 1, 4),),
    "w_qkv_fwd": (("w_o", 0, 1, 1), ("up1", 1, 2, 4)),
    "attention_fwd": (("up1", 2, 4, 4), ("down1", 0, 2, 4)),
    "mlp_up_l1": (("down1", 2, 4, 4),),
}
EXCHANGE_PLAN = {
    "mlp_down_dx_l1": (("down1", 0, 3, 8),), "mlp_up_dw_l1": (("down1", 3, 6, 8),),
    "mlp_up_dx_l1": (("down1", 6, 8, 8), ("up1", 0, 1, 8)),
    "w_o_dw": (("up1", 1, 2, 8),), "w_o_dx": (("up1", 2, 3, 8),),
    "attention_bwd": (("up1", 3, 8, 8), ("w_o", 0, 1, 1)),
    "w_qkv_dx": (("w_qkv", 0, 1, 2),), "mlp_down_dw_l0": (("w_qkv", 1, 2, 2),),
    "mlp_down_dx_l0": (("down0", 0, 3, 8),), "mlp_up_dw_l0": (("down0", 3, 6, 8),),
    "mlp_up_dx_l0": (("down0", 6, 8, 8), ("up0", 0, 1, 8)),
    "w_out_dw": (("up0", 1, 2, 8),), "w_out_dx": (("up0", 2, 3, 8),),
    "mixer_b_bwd": (("up0", 3, 7, 8),),
    "w_in_dw": (("up0", 7, 8, 8), ("w_out", 0, 1, 2)),
    "w_in_dx": (("w_out", 1, 2, 2), ("w_in", 0, 1, 4)),
    "norm_bwd_x0": (("w_in", 1, 2, 4),),
    "adamw_mlp_w_down": (("w_in", 2, 3, 4),), "adamw_mlp_w_up": (("w_in", 3, 4, 4),),
}


def _job_sems():
    return [pltpu.SemaphoreType.DMA((N_DEV - 1,)), pltpu.SemaphoreType.DMA((N_DEV - 1,)), pltpu.SemaphoreType.DMA((1,))]


def _gather_job(shard, prev=None, lo=0, hi=1, parts=1):
    n = shard.shape[0] // parts
    assert n * parts == shard.shape[0]
    rows = pl.ds(lo * n, (hi - lo) * n)

    def ctx():
        x, y, c = lax.axis_index("x"), lax.axis_index("y"), lax.axis_index("c")
        chips = [(1 - x, y), (x, 1 - y), (1 - x, 1 - y)]
        return x, y, c, chips

    def idx(px, py, pc):
        return 4 * px + 2 * py + pc

    def copy(src, out, sems, k, block, to):
        return pltpu.make_async_remote_copy(
            src_ref=out.at[block, rows] if src is None else src.at[rows], dst_ref=out.at[block, rows],
            send_sem=sems[0].at[k], recv_sem=sems[1].at[k], device_id=to, device_id_type=MESH)

    def start(ins, outs, sems):
        x, y, c, chips = ctx()
        src, out = ins[0], outs[0]
        me = idx(x, y, c)
        pltpu.make_async_copy(src.at[rows], out.at[me, rows], sems[2].at[0]).start()
        copy(src, out, sems, 0, me, (x, y, 1 - c)).start()
        for j, (px, py) in enumerate(chips):
            copy(src, out, sems, 1 + j, me, (px, py, c)).start()

    def mid(ins, outs, sems):
        x, y, c, chips = ctx()
        out = outs[0]
        for j, (px, py) in enumerate(chips):
            copy(None, out, sems, 1 + j, idx(px, py, c), (x, y, c)).wait_recv()
            copy(None, out, sems, 4 + j, idx(px, py, c), (x, y, 1 - c)).start()

    def end(ins, outs, sems):
        x, y, c, chips = ctx()
        src, out = ins[0], outs[0]
        me = (x, y, c)
        copy(None, out, sems, 0, idx(x, y, 1 - c), me).wait_recv()
        for j, (px, py) in enumerate(chips):
            copy(None, out, sems, 4 + j, idx(px, py, 1 - c), me).wait_recv()
        for k in range(N_DEV - 1):
            copy(src, out, sems, k, idx(x, y, c), me).wait_send()
        pltpu.make_async_copy(src.at[rows], out.at[idx(x, y, c), rows], sems[2].at[0]).wait()

    out_shape = jax.ShapeDtypeStruct((N_DEV,) + shard.shape, shard.dtype)
    if prev is None:
        return _Job([shard], [out_shape], _job_sems(), start, mid, end)
    return _Job([shard, prev], [out_shape], _job_sems(), start, mid, end, alias={1: 0})


def _exchange_job(src, land, lo=0, hi=1, parts=1):
    n = src.shape[1] // parts
    assert n * parts == src.shape[1]

    def sl(ref, s):
        return ref.at[s, pl.ds(lo * n, (hi - lo) * n)]

    def start(ins, outs, sems):
        me = _my_index()
        pltpu.make_async_copy(sl(ins[0], me), sl(outs[0], me), sems[2].at[0]).start()
        for k, (pos, idx) in enumerate(_peers()):
            pltpu.make_async_remote_copy(
                src_ref=sl(ins[0], idx), dst_ref=sl(outs[0], me), send_sem=sems[0].at[k], recv_sem=sems[1].at[k],
                device_id=pos, device_id_type=MESH).start()

    def mid(ins, outs, sems):
        pass

    def end(ins, outs, sems):
        me = _my_index()
        for k, (pos, idx) in enumerate(_peers()):
            cp = pltpu.make_async_remote_copy(
                src_ref=sl(ins[0], idx), dst_ref=sl(outs[0], idx), send_sem=sems[0].at[k], recv_sem=sems[1].at[k],
                device_id=pos, device_id_type=MESH)
            cp.wait_recv()
            cp.wait_send()
        pltpu.make_async_copy(sl(ins[0], me), sl(outs[0], me), sems[2].at[0]).wait()

    return _Job([src, land], [jax.ShapeDtypeStruct(land.shape, land.dtype)], _job_sems(), start, mid, end, alias={1: 0})


def _adamw_math(w, g, m, v):
    m = ADAM_B1 * m + (1.0 - ADAM_B1) * g
    v = ADAM_B2 * v + (1.0 - ADAM_B2) * (g * g)
    m_hat = m / (1.0 - ADAM_B1 ** ADAM_STEP)
    v_hat = v / (1.0 - ADAM_B2 ** ADAM_STEP)
    delta = -ADAM_LR * (m_hat / (jnp.sqrt(v_hat) + ADAM_EPS) + ADAM_WD * w)
    return delta, m, v


def _sum_slots(ref):
    g = ref[0].astype(F32)
    for s in range(1, N_DEV):
        g = g + ref[s].astype(F32)
    return g


def _adamw_big(name, lands, w, m, v, jobs=(), transposed=False):
    L, R, C = w.shape
    assert len(lands) == L
    tr = _tile(R, max(LANES, (256 * 1024) // C))
    nr = R // tr

    def body(*refs):
        l_refs = refs[:L]
        w_ref, m_ref, v_ref, g_ref, d_ref, nm_ref, nv_ref = refs[L:]
        for li in range(L):
            @pl.when(pl.program_id(0) // nr == li)
            def _(li=li):
                g = _sum_slots(l_refs[li])
                if transposed:
                    g = g.T
                d, nm, nv = _adamw_math(w_ref[...], g, m_ref[...], v_ref[...])
                g_ref[...] = g
                d_ref[...] = d
                nm_ref[...] = nm
                nv_ref[...] = nv

    def land_spec(li):
        if transposed:
            return pl.BlockSpec((N_DEV, C, tr), lambda s: (0, 0, jnp.where(s // nr == li, s % nr, 0)))
        return pl.BlockSpec((N_DEV, tr, C), lambda s: (0, jnp.where(s // nr == li, s % nr, 0), 0))

    row = pl.BlockSpec((None, tr, C), lambda s: (s // nr, s % nr, 0))
    return _carry_call(
        body, name=name, steps=L * nr, in_specs=[land_spec(li) for li in range(L)] + [row, row, row],
        out_specs=[row] * 4, out_shape=[jax.ShapeDtypeStruct((L, R, C), F32)] * 4, scratch_shapes=[],
        args=(*lands, w, m, v), jobs=jobs)


def _sum8(name, slots):
    _, R, C = slots.shape

    def body(s_ref, o_ref):
        o_ref[...] = _sum_slots(s_ref)

    return pl.pallas_call(body, name=name, out_shape=jax.ShapeDtypeStruct((R, C), F32))(slots)


def _adamw_small(name, g, w, m, v):
    def body(g_ref, w_ref, m_ref, v_ref, d_ref, nm_ref, nv_ref):
        d, nm, nv = _adamw_math(w_ref[...], g_ref[...], m_ref[...], v_ref[...])
        d_ref[...] = d
        nm_ref[...] = nm
        nv_ref[...] = nv

    return pl.pallas_call(body, name=name, out_shape=[jax.ShapeDtypeStruct(w.shape, F32)] * 3)(g, w, m, v)


def _pack_rows(arrs):
    parts, spans, r0 = [], [], 0
    for a in arrs:
        flat = a.astype(F32).reshape(-1)
        rows = -(-flat.shape[0] // LANES)
        rows = -(-rows // SUBLANES) * SUBLANES
        flat = jnp.pad(flat, (0, rows * LANES - flat.shape[0]))
        parts.append(flat.reshape(rows, LANES))
        spans.append((r0, rows, a.shape))
        r0 += rows
    return jnp.concatenate(parts, axis=0), spans


def _unpack_rows(buf, span):
    r0, rows, shape = span
    n = math.prod(shape)
    return buf[..., r0:r0 + rows, :].reshape(buf.shape[:-2] + (rows * LANES,))[..., :n].reshape(buf.shape[:-2] + shape)


def _from_col_blocks(wb):
    B, K, n = wb.shape
    return jnp.transpose(wb, (1, 0, 2)).reshape(K, B * n)


def kernel(x, norm_gains, hyb_w_in, hyb_conv_a, hyb_conv_b, hyb_conv_b_bias, hyb_rg_w_a, hyb_rg_b_a, hyb_rg_w_x, hyb_rg_b_x, hyb_rg_lambda, hyb_w_out, sb_w_qkv, sb_w_o, mlp_w_up, mlp_w_down, loss_target, m_norm_gains, m_hyb_w_in, m_hyb_conv_a, m_hyb_conv_b, m_hyb_conv_b_bias, m_hyb_rg_w_a, m_hyb_rg_b_a, m_hyb_rg_w_x, m_hyb_rg_b_x, m_hyb_rg_lambda, m_hyb_w_out, m_sb_w_qkv, m_sb_w_o, m_mlp_w_up, m_mlp_w_down, v_norm_gains, v_hyb_w_in, v_hyb_conv_a, v_hyb_conv_b, v_hyb_conv_b_bias, v_hyb_rg_w_a, v_hyb_rg_b_a, v_hyb_rg_w_x, v_hyb_rg_b_x, v_hyb_rg_lambda, v_hyb_w_out, v_sb_w_qkv, v_sb_w_o, v_mlp_w_up, v_mlp_w_down):
    T, D = x.shape[1], x.shape[2]
    me = _my_index()

    small_shards, small_spans = _pack_rows([norm_gains, hyb_conv_a[0], hyb_conv_b[0]])
    (small_all,) = _run_jobs("gather_small", [_gather_job(small_shards)])
    gains_b = _unpack_rows(small_all, small_spans[0])
    gains = jnp.transpose(gains_b, (1, 2, 0, 3)).reshape(2, 4, D)
    conv_a = _from_col_blocks(_unpack_rows(small_all, small_spans[1]))
    conv_b = _from_col_blocks(_unpack_rows(small_all, small_spans[2]))

    shards = {"w_in": hyb_w_in[0].T, "w_out": hyb_w_out[0], "w_qkv": sb_w_qkv[0].T, "w_o": sb_w_o[0],
              "up0": mlp_w_up[0], "up1": mlp_w_up[1], "down0": mlp_w_down[0], "down1": mlp_w_down[1]}
    shards = {n: s.astype(BF16) for n, s in shards.items()}
    first = _run_jobs("gather_first", [_gather_job(shards[n]) for n in GATHER_FIRST]) if GATHER_FIRST else []
    wts = _Weights({n: _gathered_layout(n, g) for n, g in zip(GATHER_FIRST, first)}, shards, GATHER_PLAN)
    assert not set(GATHER_FIRST) & {e[0] for es in GATHER_PLAN.values() for e in es}
    grads_big = _Grads({n: lax.empty((N_DEV,) + s.shape, BF16) for n, s in shards.items()}, EXCHANGE_PLAN)

    sq, grad_x, small = _local_step(
        x[0], loss_target[0], gains, conv_a, conv_b, hyb_conv_b_bias, hyb_rg_w_a[0], hyb_rg_b_a, hyb_rg_w_x[0],
        hyb_rg_b_x, hyb_rg_lambda, wts, grads_big)


    names = ["norm_gains", "hyb_w_in", "hyb_conv_a", "hyb_conv_b", "hyb_conv_b_bias", "hyb_rg_w_a", "hyb_rg_b_a",
             "hyb_rg_w_x", "hyb_rg_b_x", "hyb_rg_lambda", "hyb_w_out", "sb_w_qkv", "sb_w_o", "mlp_w_up", "mlp_w_down"]
    params = dict(zip(names, [norm_gains, hyb_w_in, hyb_conv_a, hyb_conv_b, hyb_conv_b_bias, hyb_rg_w_a, hyb_rg_b_a,
                              hyb_rg_w_x, hyb_rg_b_x, hyb_rg_lambda, hyb_w_out, sb_w_qkv, sb_w_o, mlp_w_up, mlp_w_down]))
    moms = dict(zip(names, [m_norm_gains, m_hyb_w_in, m_hyb_conv_a, m_hyb_conv_b, m_hyb_conv_b_bias, m_hyb_rg_w_a,
                            m_hyb_rg_b_a, m_hyb_rg_w_x, m_hyb_rg_b_x, m_hyb_rg_lambda, m_hyb_w_out, m_sb_w_qkv,
                            m_sb_w_o, m_mlp_w_up, m_mlp_w_down]))
    vars_ = dict(zip(names, [v_norm_gains, v_hyb_w_in, v_hyb_conv_a, v_hyb_conv_b, v_hyb_conv_b_bias, v_hyb_rg_w_a,
                             v_hyb_rg_b_a, v_hyb_rg_w_x, v_hyb_rg_b_x, v_hyb_rg_lambda, v_hyb_w_out, v_sb_w_qkv,
                             v_sb_w_o, v_mlp_w_up, v_mlp_w_down]))
    grads, deltas, new_m, new_v = {}, {}, {}, {}

    big_lands = {"mlp_w_down": ["down0", "down1"], "mlp_w_up": ["up0", "up1"], "sb_w_qkv": ["w_qkv"], "sb_w_o": ["w_o"],
                 "hyb_w_out": ["w_out"], "hyb_w_in": ["w_in"]}
    for nm, keys in big_lands.items():
        call = f"adamw_{nm}"
        if nm == "hyb_w_in":
            grads_big.flush("exchange_grads")
        jobs = grads_big.jobs(call)
        assert not {k for k in keys} & {e[0] for e in EXCHANGE_PLAN.get(call, ())}
        outs = _adamw_big(call, [grads_big.lands[k] for k in keys], params[nm], moms[nm], vars_[nm], jobs=jobs,
                          transposed=nm in ("hyb_w_in", "sb_w_qkv"))
        grads[nm], deltas[nm], new_m[nm], new_v[nm] = outs[:4]
        grads_big.deliver(call, outs[4:])

    small_names = ["norm_gains", "hyb_conv_a", "hyb_conv_b", "hyb_conv_b_bias", "hyb_rg_w_a", "hyb_rg_b_a",
                   "hyb_rg_w_x", "hyb_rg_b_x", "hyb_rg_lambda"]
    small_keys = ["norm_gains", "conv_a", "conv_b", "conv_b_bias", "rg_w_a", "rg_b_a", "rg_w_x", "rg_b_x", "rg_lambda"]
    sg_buf, sg_spans = _pack_rows([small[k] for k in small_keys] + [sq.reshape(1)])
    (sg_all,) = _run_jobs("gather_small_grads", [_gather_job(sg_buf)])
    sg_sum = _sum8("sum_small_grads", sg_all)
    full = {nm: _unpack_rows(sg_sum, sp) for nm, sp in zip(small_names, sg_spans)}
    loss = _unpack_rows(sg_sum, sg_spans[-1])[0] * (0.5 / D)
    cb = (D // 2) // N_DEV
    small_grads = {
        "norm_gains": lax.dynamic_slice_in_dim(full["norm_gains"], me * (D // N_DEV), D // N_DEV, axis=2),
        "hyb_conv_a": lax.dynamic_slice_in_dim(full["hyb_conv_a"], me * cb, cb, axis=1)[None],
        "hyb_conv_b": lax.dynamic_slice_in_dim(full["hyb_conv_b"], me * cb, cb, axis=1)[None],
        "hyb_conv_b_bias": full["hyb_conv_b_bias"],
        "hyb_rg_w_a": full["hyb_rg_w_a"][None],
        "hyb_rg_b_a": full["hyb_rg_b_a"],
        "hyb_rg_w_x": full["hyb_rg_w_x"][None],
        "hyb_rg_b_x": full["hyb_rg_b_x"],
        "hyb_rg_lambda": full["hyb_rg_lambda"],
    }
    pk = lambda d: _pack_rows([d[nm] for nm in small_names])
    g_buf, spans = pk(small_grads)
    w_buf, _ = pk(params)
    m_buf, _ = pk(moms)
    v_buf, _ = pk(vars_)
    d_buf, nm_buf, nv_buf = _adamw_small("adamw_small", g_buf, w_buf, m_buf, v_buf)
    for nm, sp in zip(small_names, spans):
        grads[nm] = small_grads[nm]
        deltas[nm], new_m[nm], new_v[nm] = _unpack_rows(d_buf, sp), _unpack_rows(nm_buf, sp), _unpack_rows(nv_buf, sp)

    return (loss, grad_x[None], *[grads[n] for n in names], *[deltas[n] for n in names],
            *[new_m[n] for n in names], *[new_v[n] for n in names])
```

```python
import math

import jax
import jax.numpy as jnp
from jax import lax
from jax.experimental import pallas as pl
from jax.experimental.pallas import tpu as pltpu

F32 = jnp.float32
BF16 = jnp.bfloat16

NORM_EPS = 1e-6
LRU_C = 8.0
ATT_HEAD_DIM = 128
RG_HEAD_DIM = 64
LANES = 128
SUBLANES = 8
N_DEV = 8
ADAM_LR = 0.001
ADAM_B1 = 0.9
ADAM_B2 = 0.999
ADAM_EPS = 1e-08
ADAM_WD = 0.01
ADAM_STEP = 10
VMEM_LIMIT = 56 * 1024 * 1024
MM_TK = 2048
MM_TM = 2048
MM_TN = 1024
MM_TK_TOKENS = 4096
MESH = pl.DeviceIdType.MESH


def _tile(n, pref):
    if n <= pref:
        return n
    t = (pref // LANES) * LANES
    while t > LANES and n % t:
        t -= LANES
    assert n % t == 0, (n, pref)
    return t


def _params(sem):
    return pltpu.CompilerParams(dimension_semantics=sem, vmem_limit_bytes=VMEM_LIMIT)


DIMS_NN = (((1,), (0,)), ((), ()))
DIMS_NT = (((1,), (1,)), ((), ()))
DIMS_TN = (((0,), (0,)), ((), ()))


_ANY = pl.BlockSpec(memory_space=pl.ANY)


class _Job:
    def __init__(self, ins, outs, sems, start, mid, end, alias=None):
        self.ins, self.outs, self.sems = ins, outs, sems
        self.start, self.mid, self.end = start, mid, end
        self.alias = alias or {}


def _job_plumbing(jobs, n_in, n_out):
    j_ins = [a for jb in jobs for a in jb.ins]
    j_outs = [o for jb in jobs for o in jb.outs]
    j_sems = [s for jb in jobs for s in jb.sems]
    aliases, pi, po = {}, 0, 0
    for jb in jobs:
        for i_in, i_out in jb.alias.items():
            aliases[n_in + pi + i_in] = n_out + po + i_out
        pi += len(jb.ins)
        po += len(jb.outs)
    return j_ins, j_outs, j_sems, aliases


def _job_phase(jobs, which, jin, jout, jsem):
    pi = po = ps = 0
    for jb in jobs:
        getattr(jb, which)(jin[pi:pi + len(jb.ins)], jout[po:po + len(jb.outs)], jsem[ps:ps + len(jb.sems)])
        pi, po, ps = pi + len(jb.ins), po + len(jb.outs), ps + len(jb.sems)


def _run_jobs(name, jobs):
    j_ins, j_outs, j_sems, aliases = _job_plumbing(jobs, 0, 0)
    n_ji, n_jo = len(j_ins), len(j_outs)

    def body(*refs):
        jin, jout, jsem = refs[:n_ji], refs[n_ji:n_ji + n_jo], refs[n_ji + n_jo:]
        for which in ("start", "mid", "end"):
            _job_phase(jobs, which, jin, jout, jsem)

    return pl.pallas_call(body, name=name, in_specs=[_ANY] * n_ji, out_specs=[_ANY] * n_jo, out_shape=j_outs,
                          scratch_shapes=j_sems, input_output_aliases=aliases)(*j_ins)


def _carry_call(body, *, name, steps, in_specs, out_specs, out_shape, scratch_shapes, args, jobs=(), aliases=None,
                sequential=False):
    n_in, n_out, n_sc = len(in_specs), len(out_shape), len(scratch_shapes)
    j_ins, j_outs, j_sems, j_aliases = _job_plumbing(jobs, n_in, n_out)
    n_ji, n_jo = len(j_ins), len(j_outs)

    def wrapped(*refs):
        ins, jin = refs[:n_in], refs[n_in:n_in + n_ji]
        o0 = n_in + n_ji
        outs, jout = refs[o0:o0 + n_out], refs[o0 + n_out:o0 + n_out + n_jo]
        s0 = o0 + n_out + n_jo
        scratch, jsem = refs[s0:s0 + n_sc], refs[s0 + n_sc:]
        step = pl.program_id(0)
        if jobs:
            pl.when(step == 0)(lambda: _job_phase(jobs, "start", jin, jout, jsem))
            pl.when(step == (4 * steps) // 5)(lambda: _job_phase(jobs, "mid", jin, jout, jsem))
        body(*ins, *outs, *scratch)
        if jobs:
            pl.when(step == steps - 1)(lambda: _job_phase(jobs, "end", jin, jout, jsem))

    return pl.pallas_call(
        wrapped, name=name, grid=(steps,),
        in_specs=list(in_specs) + [_ANY] * n_ji, out_specs=list(out_specs) + [_ANY] * n_jo,
        out_shape=list(out_shape) + j_outs, scratch_shapes=list(scratch_shapes) + j_sems,
        input_output_aliases={**(aliases or {}), **j_aliases},
        compiler_params=_params(("arbitrary",) if (jobs or sequential) else ("parallel",)))(*args, *j_ins)


def _matmul(name, ins, outs, grid, dims, acc_shape, epilogue=None, jobs=()):
    n_in, n_out, nk = len(ins), len(outs), grid[2]
    j_ins, j_outs, j_sems, aliases = _job_plumbing(jobs, n_in, n_out)
    n_ji, n_jo = len(j_ins), len(j_outs)
    total = grid[0] * grid[1] * grid[2]
    n_acc = 0 if nk == 1 else 1

    def body(*refs):
        a_ref, b_ref = refs[0], refs[1]
        extras = refs[2:n_in]
        jin = refs[n_in:n_in + n_ji]
        out_refs = refs[n_in + n_ji:n_in + n_ji + n_out]
        jout = refs[n_in + n_ji + n_out:n_in + n_ji + n_out + n_jo]
        jsem = refs[n_in + n_ji + n_out + n_jo + n_acc:]
        k = pl.program_id(2)
        step = (pl.program_id(0) * grid[1] + pl.program_id(1)) * grid[2] + k
        if jobs:
            pl.when(step == 0)(lambda: _job_phase(jobs, "start", jin, jout, jsem))
            pl.when(step == (4 * total) // 5)(lambda: _job_phase(jobs, "mid", jin, jout, jsem))

        def finish(r):
            res = epilogue(r, *[e[...] for e in extras]) if epilogue is not None else (r,)
            for o, v in zip(out_refs, res):
                o[...] = v.astype(o.dtype)

        if len(b_ref.shape) == 2:
            prod = lax.dot_general(a_ref[...], b_ref[...], dims, preferred_element_type=F32)
        else:
            kb = a_ref.shape[1] // b_ref.shape[0]
            prod = sum(lax.dot_general(a_ref[:, g * kb:(g + 1) * kb], b_ref[g], dims, preferred_element_type=F32)
                       for g in range(b_ref.shape[0]))
        if nk == 1:
            finish(prod)
        else:
            acc = refs[n_in + n_ji + n_out + n_jo]

            @pl.when(k == 0)
            def _():
                acc[...] = prod

            @pl.when((k > 0) & (k < nk - 1))
            def _():
                acc[...] += prod

            @pl.when(k == nk - 1)
            def _():
                finish(acc[...] + prod)

        if jobs:
            pl.when(step == total - 1)(lambda: _job_phase(jobs, "end", jin, jout, jsem))

    sem = ("arbitrary",) * 3 if jobs else ("parallel", "parallel", "arbitrary")
    res = pl.pallas_call(
        body, name=name, grid=grid,
        in_specs=[s for _, s in ins] + [_ANY] * n_ji,
        out_specs=[s for _, s in outs] + [_ANY] * n_jo,
        out_shape=[s for s, _ in outs] + j_outs,
        scratch_shapes=[pltpu.VMEM(acc_shape, F32)] * n_acc + j_sems,
        input_output_aliases=aliases,
        compiler_params=_params(sem),
    )(*[a for a, _ in ins], *j_ins)
    return res


def _mm_nn(name, a, b, out_dtype, *, a_seg=None, out_seg=None, tm=MM_TM, tn=MM_TN, tk=MM_TK, jobs=()):
    if a_seg:
        _, M, ks = a.shape
        K = ks * a_seg
    else:
        M, K = a.shape
        ks = K
    N = b.shape[1]
    ns = N // out_seg if out_seg else N
    tm, tn, tk = _tile(M, tm), _tile(ns, tn), _tile(ks, tk)
    nks, nns = ks // tk, ns // tn
    grid = (M // tm, N // tn, K // tk)
    if a_seg:
        a_spec = pl.BlockSpec((None, tm, tk), lambda i, j, k: (k // nks, i, k % nks))
    else:
        a_spec = pl.BlockSpec((tm, tk), lambda i, j, k: (i, k))
    b_spec = pl.BlockSpec((tk, tn), lambda i, j, k: (k, j))
    if out_seg:
        o_spec = pl.BlockSpec((None, tm, tn), lambda i, j, k: (j // nns, i, j % nns))
        o_shape = (out_seg, M, ns)
    else:
        o_spec = pl.BlockSpec((tm, tn), lambda i, j, k: (i, j))
        o_shape = (M, N)
    outs = [(jax.ShapeDtypeStruct(o_shape, out_dtype), o_spec)]
    return _matmul(name, [(a, a_spec), (b, b_spec)], outs, grid, DIMS_NN, (tm, tn), None, jobs)


def _mm_nt(name, a, b, out_dtype, *, a_seg=None, out_seg=None, tm=MM_TM, tn=MM_TN, tk=MM_TK, jobs=()):
    if a_seg:
        _, M, ks = a.shape
        K = ks * a_seg
    else:
        M, K = a.shape
        ks = K
    N = b.shape[0]
    ns = N // out_seg if out_seg else N
    tm, tn, tk = _tile(M, tm), _tile(ns, tn), _tile(ks, tk)
    nks, nns = ks // tk, ns // tn
    grid = (M // tm, N // tn, K // tk)
    if a_seg:
        a_spec = pl.BlockSpec((None, tm, tk), lambda i, j, k: (k // nks, i, k % nks))
    else:
        a_spec = pl.BlockSpec((tm, tk), lambda i, j, k: (i, k))
    b_spec = pl.BlockSpec((tn, tk), lambda i, j, k: (j, k))
    if out_seg:
        o_spec = pl.BlockSpec((None, tm, tn), lambda i, j, k: (j // nns, i, j % nns))
        o_shape = (out_seg, M, ns)
    else:
        o_spec = pl.BlockSpec((tm, tn), lambda i, j, k: (i, j))
        o_shape = (M, N)
    outs = [(jax.ShapeDtypeStruct(o_shape, out_dtype), o_spec)]
    return _matmul(name, [(a, a_spec), (b, b_spec)], outs, grid, DIMS_NT, (tm, tn), None, jobs)


def _mm_tn(name, a, b, out_dtype, *, a_seg=None, b_seg=None, out_blocks=None, tm=MM_TN, tn=MM_TN, tk=MM_TK_TOKENS,
           jobs=()):
    if a_seg:
        _, T, ms = a.shape
        M = ms * a_seg
    else:
        T, M = a.shape
        ms = M
    if b_seg:
        _, _, ns = b.shape
        N = ns * b_seg
    else:
        N = b.shape[1]
        ns = N
    nb_cols = N // out_blocks if out_blocks else N
    tm, tk = _tile(ms, tm), _tile(T, tk)
    tn = _tile(math.gcd(ns, nb_cols), tn)
    nms, nns, nbs = ms // tm, ns // tn, nb_cols // tn
    grid = (M // tm, N // tn, T // tk)
    if a_seg:
        a_spec = pl.BlockSpec((None, tk, tm), lambda i, j, k: (i // nms, k, i % nms))
    else:
        a_spec = pl.BlockSpec((tk, tm), lambda i, j, k: (k, i))
    if b_seg:
        b_spec = pl.BlockSpec((None, tk, tn), lambda i, j, k: (j // nns, k, j % nns))
    else:
        b_spec = pl.BlockSpec((tk, tn), lambda i, j, k: (k, j))
    if out_blocks:
        o_spec = pl.BlockSpec((None, tm, tn), lambda i, j, k: (j // nbs, i, j % nbs))
        o_shape = (out_blocks, M, nb_cols)
    else:
        o_spec = pl.BlockSpec((tm, tn), lambda i, j, k: (i, j))
        o_shape = (M, N)
    outs = [(jax.ShapeDtypeStruct(o_shape, out_dtype), o_spec)]
    return _matmul(name, [(a, a_spec), (b, b_spec)], outs, grid, DIMS_TN, (tm, tn), None, jobs)


def _rms(x):
    return lax.rsqrt(jnp.mean(x * x, axis=-1, keepdims=True) + NORM_EPS)


def _row_tile(T):
    return _tile(T, 256)


def _norm_fwd(name, x, g, jobs=()):
    T, D = x.shape
    tr = _row_tile(T)

    def body(x_ref, g_ref, h_ref):
        xv = x_ref[...]
        h_ref[...] = (xv * _rms(xv) * g_ref[...]).astype(h_ref.dtype)

    row = pl.BlockSpec((tr, D), lambda i: (i, 0))
    vec = pl.BlockSpec((1, D), lambda i: (0, 0))
    return _carry_call(body, name=name, steps=T // tr, in_specs=[row, vec], out_specs=[row],
                       out_shape=[jax.ShapeDtypeStruct((T, D), BF16)], scratch_shapes=[], args=(x, g), jobs=jobs)


def _resid_norm(name, x, br, g_post, g_next, jobs=()):
    T, D = x.shape
    tr = _row_tile(T)

    def body(x_ref, br_ref, gp_ref, gn_ref, xn_ref, h_ref):
        b = br_ref[...].astype(F32)
        xn = x_ref[...] + b * _rms(b) * gp_ref[...]
        xn_ref[...] = xn
        h_ref[...] = (xn * _rms(xn) * gn_ref[...]).astype(h_ref.dtype)

    row = pl.BlockSpec((tr, D), lambda i: (i, 0))
    vec = pl.BlockSpec((1, D), lambda i: (0, 0))
    return _carry_call(body, name=name, steps=T // tr, in_specs=[row, row, vec, vec], out_specs=[row, row],
                       out_shape=[jax.ShapeDtypeStruct((T, D), F32), jax.ShapeDtypeStruct((T, D), BF16)],
                       scratch_shapes=[], args=(x, br, g_post, g_next), jobs=jobs)


def _rms_bwd(xv, g, dyv):
    r = _rms(xv)
    xhat = xv * r
    gdy = dyv * g
    dx = r * (gdy - xhat * jnp.mean(gdy * xhat, axis=-1, keepdims=True))
    return dx, jnp.sum(dyv * xhat, axis=0, keepdims=True)


def _final_loss(name, x, br, g_post, target):
    T, D = x.shape
    tr = _row_tile(T)

    def body(x_ref, br_ref, gp_ref, t_ref, dy_ref, dbr_ref, dg_ref, ls_ref):
        b = br_ref[...].astype(F32)
        err = x_ref[...] + b * _rms(b) * gp_ref[...] - t_ref[...]
        dy = err * (1.0 / D)
        dy_ref[...] = dy
        dbr, dg = _rms_bwd(b, gp_ref[...], dy)
        dbr_ref[...] = dbr.astype(dbr_ref.dtype)

        @pl.when(pl.program_id(0) == 0)
        def _():
            ls_ref[...] = jnp.zeros_like(ls_ref)
            dg_ref[...] = jnp.zeros_like(dg_ref)

        ls_ref[...] += jnp.sum(err * err)
        dg_ref[...] += dg

    row = pl.BlockSpec((tr, D), lambda i: (i, 0))
    vec = pl.BlockSpec((1, D), lambda i: (0, 0))
    acc = pl.BlockSpec((SUBLANES, LANES), lambda i: (0, 0))
    return pl.pallas_call(body, name=name, grid=(T // tr,), in_specs=[row, row, vec, row],
                          out_specs=[row, row, vec, acc],
                          out_shape=[jax.ShapeDtypeStruct((T, D), F32), jax.ShapeDtypeStruct((T, D), BF16),
                                     jax.ShapeDtypeStruct((1, D), F32), jax.ShapeDtypeStruct((SUBLANES, LANES), F32)],
                          compiler_params=_params(("arbitrary",)))(x, br, g_post, target)


def _norm_bwd_pair(name, x, g_pre, dh, add, br, g_post):
    T, D = x.shape
    tr = _row_tile(T)

    def body(x_ref, gpre_ref, dh_ref, add_ref, br_ref, gpost_ref, dx_ref, dbr_ref, dgpre_ref, dgpost_ref):
        dx, dg_pre = _rms_bwd(x_ref[...], gpre_ref[...], dh_ref[...].astype(F32))
        dx = dx + add_ref[...]
        dx_ref[...] = dx
        dbr, dg_post = _rms_bwd(br_ref[...].astype(F32), gpost_ref[...], dx)
        dbr_ref[...] = dbr.astype(dbr_ref.dtype)

        @pl.when(pl.program_id(0) == 0)
        def _():
            dgpre_ref[...] = jnp.zeros_like(dgpre_ref)
            dgpost_ref[...] = jnp.zeros_like(dgpost_ref)

        dgpre_ref[...] += dg_pre
        dgpost_ref[...] += dg_post

    row = pl.BlockSpec((tr, D), lambda i: (i, 0))
    vec = pl.BlockSpec((1, D), lambda i: (0, 0))
    return pl.pallas_call(body, name=name, grid=(T // tr,), in_specs=[row, vec, row, row, row, vec],
                          out_specs=[row, row, vec, vec],
                          out_shape=[jax.ShapeDtypeStruct((T, D), F32), jax.ShapeDtypeStruct((T, D), BF16),
                                     jax.ShapeDtypeStruct((1, D), F32), jax.ShapeDtypeStruct((1, D), F32)],
                          compiler_params=_params(("arbitrary",)))(x, g_pre, dh, add, br, g_post)


def _norm_bwd(name, x, g, dy, add, jobs=()):
    T, D = x.shape
    tr = _row_tile(T)

    def body(x_ref, g_ref, dy_ref, add_ref, dx_ref, dg_ref):
        dx, dg = _rms_bwd(x_ref[...], g_ref[...], dy_ref[...].astype(F32))
        dx_ref[...] = dx + add_ref[...]

        @pl.when(pl.program_id(0) == 0)
        def _():
            dg_ref[...] = jnp.zeros_like(dg_ref)

        dg_ref[...] += dg

    row = pl.BlockSpec((tr, D), lambda i: (i, 0))
    vec = pl.BlockSpec((1, D), lambda i: (0, 0))
    return _carry_call(body, name=name, steps=T // tr, in_specs=[row, vec, row, row], out_specs=[row, vec],
                       out_shape=[jax.ShapeDtypeStruct((T, D), F32), jax.ShapeDtypeStruct((1, D), F32)],
                       scratch_shapes=[], args=(x, g, dy, add), jobs=jobs, sequential=True)


HALO = SUBLANES
TIME_CHUNK = 512


def _chunks(T):
    tc = min(TIME_CHUNK, T)
    assert T % tc == 0 and tc % SUBLANES == 0
    return [(t0, tc) for t0 in range(0, T, tc)]


def _log_sigmoid(x):
    return -(jnp.maximum(-x, 0.0) + jnp.log(1.0 + jnp.exp(-jnp.abs(x))))


def _sigmoid(x):
    return 0.5 * jnp.tanh(0.5 * x) + 0.5


def _one_minus_exp(x):
    series = -x * (1.0 + x * (0.5 + x * (1.0 / 6.0 + x * (1.0 / 24.0))))
    return jnp.where(x > -0.01, series, 1.0 - jnp.exp(x))


_GELU_C = math.sqrt(2.0 / math.pi)


def _gelu(x):
    return 0.5 * x * (1.0 + jnp.tanh(_GELU_C * (x + 0.044715 * x * x * x)))


def _gelu_grad(x):
    th = jnp.tanh(_GELU_C * (x + 0.044715 * x * x * x))
    return 0.5 * (1.0 + th) + 0.5 * x * (1.0 - th * th) * _GELU_C * (1.0 + 3.0 * 0.044715 * x * x)


def _tile_scan(a, b, reverse):
    rows = a.shape[0]
    pos = lax.broadcasted_iota(jnp.int32, a.shape, 0) & (SUBLANES - 1)
    for d in (1, 2, 4):
        if reverse:
            ok = pos < SUBLANES - d
            shift = rows - d
        else:
            ok = pos >= d
            shift = d
        a_sh = jnp.where(ok, pltpu.roll(a, shift, 0), 1.0)
        b_sh = jnp.where(ok, pltpu.roll(b, shift, 0), 0.0)
        b = a * b_sh + b
        a = a * a_sh
    return a, b


def _carry_scan(a_s, b_s, T, reverse):
    n = T // SUBLANES
    edge = 0 if reverse else SUBLANES - 1

    def step(j, carry):
        g = (n - 1 - j) if reverse else j
        r = pl.multiple_of(g * SUBLANES, SUBLANES)
        h = b_s[pl.ds(r, SUBLANES), :] + a_s[pl.ds(r, SUBLANES), :] * carry
        b_s[pl.ds(r, SUBLANES), :] = h
        return jnp.broadcast_to(h[edge:edge + 1, :], h.shape)

    lax.fori_loop(0, n, step, jnp.zeros((SUBLANES, a_s.shape[1]), F32))


def _seg_spec(T, seg, nblk):
    return pl.BlockSpec((None, T, LANES), lambda c: (seg, 0, c))


def _rows_to_tile(rows):
    idx = lax.broadcasted_iota(jnp.int32, (SUBLANES, LANES), 0)
    out = jnp.zeros((SUBLANES, LANES), F32)
    for k, r in enumerate(rows):
        out = jnp.where(idx == k, r, out)
    return out


def _mixer_a_fwd(proj, conv_a):
    _, T, C = proj.shape
    nblk = C // LANES
    chunks = _chunks(T)

    def body(bg_ref, cg_ref, ax_ref, w_ref, y_ref, p_s):
        p_s[pl.ds(0, HALO), :] = jnp.zeros((HALO, LANES), F32)
        for t0, tc in chunks:
            p_s[pl.ds(HALO + t0, tc), :] = cg_ref[pl.ds(t0, tc), :] * ax_ref[pl.ds(t0, tc), :]
        w = w_ref[...]
        for t0, tc in chunks:
            c = (w[2:3, :] * p_s[pl.ds(HALO + t0, tc), :] + w[1:2, :] * p_s[pl.ds(HALO + t0 - 1, tc), :]
                 + w[0:1, :] * p_s[pl.ds(HALO + t0 - 2, tc), :])
            y_ref[pl.ds(t0, tc), :] = (bg_ref[pl.ds(t0, tc), :] * c).astype(y_ref.dtype)

    return pl.pallas_call(
        body, name="mixer_a_fwd", grid=(nblk,),
        in_specs=[_seg_spec(T, 0, nblk), _seg_spec(T, 1, nblk), _seg_spec(T, 2, nblk),
                  pl.BlockSpec((3, LANES), lambda c: (0, c))],
        out_specs=_seg_spec(T, 0, nblk),
        out_shape=jax.ShapeDtypeStruct((2, T, C), BF16),
        scratch_shapes=[pltpu.VMEM((T + HALO, LANES), F32)],
        compiler_params=_params(("parallel",)))(proj, proj, proj, conv_a)


def _mixer_a_bwd(proj, conv_a, dy):
    _, T, C = proj.shape
    nblk = C // LANES
    chunks = _chunks(T)

    def body(bg_ref, cg_ref, ax_ref, w_ref, dy_ref, dp_ref, dw_ref, p_s, dc_s):
        p_s[pl.ds(0, HALO), :] = jnp.zeros((HALO, LANES), F32)
        dc_s[pl.ds(T, HALO), :] = jnp.zeros((HALO, LANES), F32)
        for t0, tc in chunks:
            p_s[pl.ds(HALO + t0, tc), :] = cg_ref[pl.ds(t0, tc), :] * ax_ref[pl.ds(t0, tc), :]
        w = w_ref[...]
        for t0, tc in chunks:
            c = (w[2:3, :] * p_s[pl.ds(HALO + t0, tc), :] + w[1:2, :] * p_s[pl.ds(HALO + t0 - 1, tc), :]
                 + w[0:1, :] * p_s[pl.ds(HALO + t0 - 2, tc), :])
            dyv = dy_ref[pl.ds(t0, tc), :]
            dp_ref[0, pl.ds(t0, tc), :] = (dyv * c).astype(dp_ref.dtype)
            dc_s[pl.ds(t0, tc), :] = dyv * bg_ref[pl.ds(t0, tc), :]
        dw = [jnp.zeros((1, LANES), F32) for _ in range(3)]
        for t0, tc in chunks:
            dc = dc_s[pl.ds(t0, tc), :]
            dpv = w[2:3, :] * dc + w[1:2, :] * dc_s[pl.ds(t0 + 1, tc), :] + w[0:1, :] * dc_s[pl.ds(t0 + 2, tc), :]
            dp_ref[1, pl.ds(t0, tc), :] = (dpv * ax_ref[pl.ds(t0, tc), :]).astype(dp_ref.dtype)
            dp_ref[2, pl.ds(t0, tc), :] = (dpv * cg_ref[pl.ds(t0, tc), :]).astype(dp_ref.dtype)
            for k in range(3):
                dw[k] = dw[k] + jnp.sum(dc * p_s[pl.ds(HALO + t0 - (2 - k), tc), :], axis=0, keepdims=True)
        dw_ref[...] = _rows_to_tile(dw)

    return pl.pallas_call(
        body, name="mixer_a_bwd", grid=(nblk,),
        in_specs=[_seg_spec(T, 0, nblk), _seg_spec(T, 1, nblk), _seg_spec(T, 2, nblk),
                  pl.BlockSpec((3, LANES), lambda c: (0, c)), _seg_spec(T, 0, nblk)],
        out_specs=[pl.BlockSpec((3, T, LANES), lambda c: (0, 0, c)),
                   pl.BlockSpec((None, SUBLANES, LANES), lambda c: (c, 0, 0))],
        out_shape=[jax.ShapeDtypeStruct((6, T, C), BF16), jax.ShapeDtypeStruct((nblk, SUBLANES, LANES), F32)],
        scratch_shapes=[pltpu.VMEM((T + HALO, LANES), F32), pltpu.VMEM((T + HALO, LANES), F32)],
        compiler_params=_params(("parallel",)))(proj, proj, proj, conv_a, dy)


def _rg_gates(xr, wa, ba, wx, bx, ls):
    xb = xr.astype(BF16)
    r = _sigmoid(jnp.dot(xb, wa, preferred_element_type=F32) + ba)
    i = _sigmoid(jnp.dot(xb, wx, preferred_element_type=F32) + bx)
    log_a = LRU_C * r * ls
    a = jnp.exp(log_a)
    mult = jnp.sqrt(_one_minus_exp(2.0 * log_a))
    return r, i, a, mult


def _conv4(xh_s, cw, bias, t0, tc):
    return (cw[3:4, :] * xh_s[pl.ds(HALO + t0, tc), :] + cw[2:3, :] * xh_s[pl.ds(HALO + t0 - 1, tc), :]
            + cw[1:2, :] * xh_s[pl.ds(HALO + t0 - 2, tc), :] + cw[0:1, :] * xh_s[pl.ds(HALO + t0 - 3, tc), :] + bias)


def _mixer_b_specs(T, nblk):
    vec = pl.BlockSpec((1, LANES), lambda c: (0, c))
    mat = pl.BlockSpec((None, LANES, LANES), lambda c: (c, 0, 0))
    return [_seg_spec(T, 3, nblk), _seg_spec(T, 4, nblk), pl.BlockSpec((4, LANES), lambda c: (0, c)),
            vec, mat, vec, mat, vec, vec]


def _mixer_b_fwd(name, proj, conv_b, bias, wa, ba, wx, bx, lam, y, jobs=()):
    _, T, C = proj.shape
    nblk = C // LANES
    chunks = _chunks(T)

    def body(gate_ref, x_ref, cw_ref, cb_ref, wa_ref, ba_ref, wx_ref, bx_ref, lam_ref, y_in, y_ref, xh_s, a_s, b_s):
        xh_s[pl.ds(0, HALO), :] = jnp.zeros((HALO, LANES), F32)
        for t0, tc in chunks:
            xh_s[pl.ds(HALO + t0, tc), :] = x_ref[pl.ds(t0, tc), :]
        cw, bias_v = cw_ref[...], cb_ref[...]
        ls = _log_sigmoid(lam_ref[...])
        for t0, tc in chunks:
            xr = _conv4(xh_s, cw, bias_v, t0, tc)
            r, i, a, mult = _rg_gates(xr, wa_ref[...], ba_ref[...], wx_ref[...], bx_ref[...], ls)
            ac, hc = _tile_scan(a, mult * i * xr, reverse=False)
            a_s[pl.ds(t0, tc), :] = ac
            b_s[pl.ds(t0, tc), :] = hc
        _carry_scan(a_s, b_s, T, reverse=False)
        for t0, tc in chunks:
            y_ref[pl.ds(t0, tc), :] = (b_s[pl.ds(t0, tc), :] * _gelu(gate_ref[pl.ds(t0, tc), :])).astype(y_ref.dtype)

    return _carry_call(
        body, name=name, steps=nblk, in_specs=_mixer_b_specs(T, nblk) + [_ANY],
        out_specs=[_seg_spec(T, 1, nblk)], out_shape=[jax.ShapeDtypeStruct(y.shape, y.dtype)],
        scratch_shapes=[pltpu.VMEM((T + HALO, LANES), F32), pltpu.VMEM((T, LANES), F32), pltpu.VMEM((T, LANES), F32)],
        args=(proj, proj, conv_b, bias, wa, ba, wx, bx, lam, y), jobs=jobs, aliases={9: 0})


_ROW_CONV, _ROW_BIAS, _ROW_BA, _ROW_BX, _ROW_LAM = 0, 4, 5, 6, 7


def _mixer_b_bwd(name, proj, conv_b, bias, wa, ba, wx, bx, lam, dy, dproj, jobs=()):
    _, T, C = proj.shape
    nblk = C // LANES
    chunks = _chunks(T)

    def body(gate_ref, x_ref, cw_ref, cb_ref, wa_ref, ba_ref, wx_ref, bx_ref, lam_ref, dy_ref, dp_in,
             dp_ref, sm_ref, dwa_ref, dwx_ref, xh_s, xr_s, r_s, i_s, a_s, h_s, sa_s, sb_s, dx_s):
        zero_halo = jnp.zeros((HALO, LANES), F32)
        xh_s[pl.ds(0, HALO), :] = zero_halo
        h_s[pl.ds(0, HALO), :] = zero_halo
        a_s[pl.ds(T, HALO), :] = zero_halo
        dx_s[pl.ds(T, HALO), :] = zero_halo
        for t0, tc in chunks:
            xh_s[pl.ds(HALO + t0, tc), :] = x_ref[pl.ds(t0, tc), :]
        cw, bias_v = cw_ref[...], cb_ref[...]
        lam_v = lam_ref[...]
        ls = _log_sigmoid(lam_v)
        wa_v, wx_v, ba_v, bx_v = wa_ref[...], wx_ref[...], ba_ref[...], bx_ref[...]
        for t0, tc in chunks:
            xr = _conv4(xh_s, cw, bias_v, t0, tc)
            r, i, a, mult = _rg_gates(xr, wa_v, ba_v, wx_v, bx_v, ls)
            xr_s[pl.ds(t0, tc), :] = xr
            r_s[pl.ds(t0, tc), :] = r
            i_s[pl.ds(t0, tc), :] = i
            a_s[pl.ds(t0, tc), :] = a
            ac, hc = _tile_scan(a, mult * i * xr, reverse=False)
            sa_s[pl.ds(t0, tc), :] = ac
            sb_s[pl.ds(t0, tc), :] = hc
        _carry_scan(sa_s, sb_s, T, reverse=False)
        for t0, tc in chunks:
            h_s[pl.ds(HALO + t0, tc), :] = sb_s[pl.ds(t0, tc), :]
        for t0, tc in chunks:
            gv = gate_ref[pl.ds(t0, tc), :]
            dyv = dy_ref[pl.ds(t0, tc), :]
            dp_ref[0, pl.ds(t0, tc), :] = (dyv * h_s[pl.ds(HALO + t0, tc), :] * _gelu_grad(gv)).astype(dp_ref.dtype)
            ac, gc = _tile_scan(a_s[pl.ds(t0 + 1, tc), :], dyv * _gelu(gv), reverse=True)
            sa_s[pl.ds(t0, tc), :] = ac
            sb_s[pl.ds(t0, tc), :] = gc
        _carry_scan(sa_s, sb_s, T, reverse=True)
        acc = {k: jnp.zeros((1, LANES), F32) for k in ("bias", "ba", "bx", "lam")}
        dwa = jnp.zeros((LANES, LANES), F32)
        dwx = jnp.zeros((LANES, LANES), F32)
        for t0, tc in chunks:
            dht = sb_s[pl.ds(t0, tc), :]
            xr, r, i, a = xr_s[pl.ds(t0, tc), :], r_s[pl.ds(t0, tc), :], i_s[pl.ds(t0, tc), :], a_s[pl.ds(t0, tc), :]
            mult = jnp.sqrt(_one_minus_exp(2.0 * LRU_C * r * ls))
            da = dht * h_s[pl.ds(HALO + t0 - 1, tc), :]
            dmult = dht * i * xr
            di = dht * mult * xr
            dlog_a = da * a - dmult * a * a / mult
            dpa = dlog_a * (LRU_C * ls) * r * (1.0 - r)
            dpx = di * i * (1.0 - i)
            acc["lam"] = acc["lam"] + jnp.sum(dlog_a * r, axis=0, keepdims=True)
            acc["ba"] = acc["ba"] + jnp.sum(dpa, axis=0, keepdims=True)
            acc["bx"] = acc["bx"] + jnp.sum(dpx, axis=0, keepdims=True)
            xb, dpab, dpxb = xr.astype(BF16), dpa.astype(BF16), dpx.astype(BF16)
            dwa = dwa + lax.dot_general(xb, dpab, DIMS_TN, preferred_element_type=F32)
            dwx = dwx + lax.dot_general(xb, dpxb, DIMS_TN, preferred_element_type=F32)
            dxr = (dht * mult * i + lax.dot_general(dpab, wa_v, DIMS_NT, preferred_element_type=F32)
                   + lax.dot_general(dpxb, wx_v, DIMS_NT, preferred_element_type=F32))
            acc["bias"] = acc["bias"] + jnp.sum(dxr, axis=0, keepdims=True)
            dx_s[pl.ds(t0, tc), :] = dxr
        dcw = [jnp.zeros((1, LANES), F32) for _ in range(4)]
        for t0, tc in chunks:
            dxr = dx_s[pl.ds(t0, tc), :]
            dxin = (cw[3:4, :] * dxr + cw[2:3, :] * dx_s[pl.ds(t0 + 1, tc), :] + cw[1:2, :] * dx_s[pl.ds(t0 + 2, tc), :]
                    + cw[0:1, :] * dx_s[pl.ds(t0 + 3, tc), :])
            dp_ref[1, pl.ds(t0, tc), :] = dxin.astype(dp_ref.dtype)
            for k in range(4):
                dcw[k] = dcw[k] + jnp.sum(dxr * xh_s[pl.ds(HALO + t0 - (3 - k), tc), :], axis=0, keepdims=True)
        dlam = acc["lam"] * LRU_C * _sigmoid(-lam_v)
        sm_ref[...] = _rows_to_tile(dcw + [acc["bias"], acc["ba"], acc["bx"], dlam])
        dwa_ref[...] = dwa
        dwx_ref[...] = dwx

    big = lambda halo: pltpu.VMEM((T + halo, LANES), F32)
    mat = pl.BlockSpec((None, LANES, LANES), lambda c: (c, 0, 0))
    return _carry_call(
        body, name=name, steps=nblk,
        in_specs=_mixer_b_specs(T, nblk) + [_seg_spec(T, 1, nblk), _ANY],
        out_specs=[pl.BlockSpec((3, T, LANES), lambda c: (1, 0, c)),
                   pl.BlockSpec((None, SUBLANES, LANES), lambda c: (c, 0, 0)), mat, mat],
        out_shape=[jax.ShapeDtypeStruct(dproj.shape, dproj.dtype), jax.ShapeDtypeStruct((nblk, SUBLANES, LANES), F32),
                   jax.ShapeDtypeStruct((nblk, LANES, LANES), F32), jax.ShapeDtypeStruct((nblk, LANES, LANES), F32)],
        scratch_shapes=[big(HALO), big(0), big(0), big(0), big(HALO), big(HALO), big(0), big(0), big(HALO)],
        args=(proj, proj, conv_b, bias, wa, ba, wx, bx, lam, dy, dproj), jobs=jobs, aliases={10: 0})


ATT_BLOCK = 128
ATT_GROUP = 3
ATT_TILE = ATT_BLOCK * ATT_GROUP
ATT_UNDERFLOW = -110.0
ATT_UNVISITED = -1e30


def _split_dot(x, m):
    hi = x.astype(BF16)
    lo = (x - hi.astype(F32)).astype(BF16)
    return jnp.dot(hi, m, preferred_element_type=F32) + jnp.dot(lo, m, preferred_element_type=F32)


def _sub(x, j):
    return x[:, j * ATT_BLOCK:(j + 1) * ATT_BLOCK]


def _stack_rows(x):
    return jnp.concatenate([_sub(x, j) for j in range(ATT_GROUP)], axis=0)


def _unstack_rows(x, offsets):
    return jnp.concatenate([x[j * ATT_BLOCK:(j + 1) * ATT_BLOCK, :] + offsets[j] for j in range(ATT_GROUP)], axis=1)


def _att_tile(q, k_ref, q0, qb, it, scale):
    hi = (qb + 1 - ATT_GROUP * it) * ATT_BLOCK
    k0 = pl.multiple_of(jnp.maximum(hi - ATT_TILE, 0), ATT_BLOCK)
    kt = k_ref[pl.ds(k0, ATT_TILE), :]
    z = lax.dot_general(q, kt, DIMS_NT, preferred_element_type=F32) * scale
    key = k0 + lax.broadcasted_iota(jnp.int32, z.shape, 1)
    row = q0 + lax.broadcasted_iota(jnp.int32, z.shape, 0)
    mask = (key < row) & (key < hi)
    n = jnp.where(mask, -(jnp.maximum(z, 0.0) + jnp.log(1.0 + jnp.exp(-jnp.abs(z)))), 0.0)
    return k0, kt, z, mask, n


def _suffix_in_tile(n, upper, run):
    rs = [jnp.sum(_sub(n, j), axis=1, keepdims=True) for j in range(ATT_GROUP)]
    offs = [None] * ATT_GROUP
    offs[ATT_GROUP - 1] = run
    for j in range(ATT_GROUP - 2, -1, -1):
        offs[j] = offs[j + 1] + rs[j + 1]
    return _unstack_rows(_split_dot(_stack_rows(n), upper), offs), offs[0] + rs[0]


def _head_spec(T, seg, heads):
    return pl.BlockSpec((None, T, ATT_HEAD_DIM), lambda h: (seg, 0, h))


def _attention_fwd(name, qkv, jobs=()):
    _, T, D = qkv.shape
    heads = D // ATT_HEAD_DIM
    nq = T // ATT_BLOCK
    assert nq <= LANES and T >= ATT_TILE and nq % 2 == 0
    scale = 1.0 / math.sqrt(ATT_HEAD_DIM)

    def body(q_ref, k_ref, v_ref, o_ref, r_ref, acc_s, run_s):
        rr = lax.broadcasted_iota(jnp.int32, (ATT_BLOCK, ATT_BLOCK), 0)
        cc = lax.broadcasted_iota(jnp.int32, (ATT_BLOCK, ATT_BLOCK), 1)
        upper = jnp.where(rr > cc, 1.0, 0.0).astype(BF16)
        lane = lax.broadcasted_iota(jnp.int32, (ATT_BLOCK, LANES), 1)

        def tile(slot, qb, q0, q, it, first):
            k0, _, z, mask, n = _att_tile(q, k_ref, q0, qb, it, scale)
            run = jnp.zeros((ATT_BLOCK, LANES), F32) if first else run_s[slot]
            suffix, run_next = _suffix_in_tile(n, upper, run)
            w = jnp.where(mask, jnp.exp(z + n + suffix), 0.0)
            pv = jnp.dot(w.astype(BF16), v_ref[pl.ds(k0, ATT_TILE), :], preferred_element_type=F32)
            if first:
                acc_s[slot] = pv
            else:
                acc_s[slot] += pv
                r_ref[pl.ds(q0, ATT_BLOCK), :] = jnp.where(lane == it, run, r_ref[pl.ds(q0, ATT_BLOCK), :])
            run_s[slot] = run_next
            return jnp.max(run_next) >= ATT_UNDERFLOW

        def pair_loop(p, _):
            blocks = []
            for slot in range(2):
                qb = 2 * p + slot
                q0 = pl.multiple_of(qb * ATT_BLOCK, ATT_BLOCK)
                r_ref[pl.ds(q0, ATT_BLOCK), :] = jnp.where(lane == 0, 0.0, ATT_UNVISITED)
                blocks.append((qb, q0, q_ref[pl.ds(q0, ATT_BLOCK), :]))
            go = [tile(slot, *blocks[slot], 0, True) for slot in range(2)]
            for slot in range(2):
                qb, q0, q = blocks[slot]
                n_tiles = (qb + ATT_GROUP) // ATT_GROUP
                lax.while_loop(lambda c: (c[0] < n_tiles) & c[1],
                               lambda c: (c[0] + 1, tile(slot, qb, q0, q, c[0], False)), (jnp.int32(1), go[slot]))
                o_ref[pl.ds(q0, ATT_BLOCK), :] = acc_s[slot].astype(o_ref.dtype)
            return 0

        lax.fori_loop(0, nq // 2, pair_loop, 0)

    return _carry_call(
        body, name=name, steps=heads,
        in_specs=[_head_spec(T, 0, heads), _head_spec(T, 1, heads), _head_spec(T, 2, heads)],
        out_specs=[pl.BlockSpec((T, ATT_HEAD_DIM), lambda h: (0, h)), pl.BlockSpec((None, T, LANES), lambda h: (h, 0, 0))],
        out_shape=[jax.ShapeDtypeStruct((T, D), BF16), jax.ShapeDtypeStruct((heads, T, LANES), F32)],
        scratch_shapes=[pltpu.VMEM((2, ATT_BLOCK, ATT_HEAD_DIM), F32), pltpu.VMEM((2, ATT_BLOCK, LANES), F32)],
        args=(qkv, qkv, qkv), jobs=jobs)


def _attention_bwd(name, qkv, do, rmat, jobs=()):
    _, T, D = qkv.shape
    heads = D // ATT_HEAD_DIM
    nq = T // ATT_BLOCK
    scale = 1.0 / math.sqrt(ATT_HEAD_DIM)

    def body(q_ref, k_ref, v_ref, do_ref, r_ref, dqkv_ref, dk_s, dv_s, dq_s, left_s):
        rr = lax.broadcasted_iota(jnp.int32, (ATT_BLOCK, ATT_BLOCK), 0)
        cc = lax.broadcasted_iota(jnp.int32, (ATT_BLOCK, ATT_BLOCK), 1)
        upper = jnp.where(rr > cc, 1.0, 0.0).astype(BF16)
        lower = jnp.where(rr < cc, 1.0, 0.0).astype(BF16)
        lane = lax.broadcasted_iota(jnp.int32, (ATT_BLOCK, LANES), 1)
        dk_s[...] = jnp.zeros_like(dk_s)
        dv_s[...] = jnp.zeros_like(dv_s)

        def tile(slot, qb, q0, q, dov, it, first):
            k0, kt, z, mask, n = _att_tile(q, k_ref, q0, qb, it, scale)
            vt = v_ref[pl.ds(k0, ATT_TILE), :]
            run = jnp.sum(jnp.where(lane == it, r_ref[pl.ds(q0, ATT_BLOCK), :], 0.0), axis=1, keepdims=True)
            suffix, _ = _suffix_in_tile(n, upper, run)
            s = z + n
            w = jnp.where(mask, jnp.exp(s + suffix), 0.0)
            e = w * lax.dot_general(dov, vt, DIMS_NT, preferred_element_type=F32)
            es = [jnp.sum(_sub(e, g), axis=1, keepdims=True) for g in range(ATT_GROUP)]
            pre = [jnp.zeros((ATT_BLOCK, LANES), F32) if first else left_s[slot]]
            for g in range(ATT_GROUP):
                pre.append(pre[g] + es[g])
            before = _unstack_rows(_split_dot(_stack_rows(e), lower), pre)
            sig = jnp.exp(s)
            dz = (jnp.where(mask, e * (1.0 - sig) - before * sig, 0.0) * scale).astype(BF16)
            dq = jnp.dot(dz, kt, preferred_element_type=F32)
            if first:
                dq_s[slot] = dq
            else:
                dq_s[slot] += dq
            dk_s[pl.ds(k0, ATT_TILE), :] += lax.dot_general(dz, q, DIMS_TN, preferred_element_type=F32)
            dv_s[pl.ds(k0, ATT_TILE), :] += lax.dot_general(w.astype(BF16), dov, DIMS_TN, preferred_element_type=F32)
            left_s[slot] = pre[ATT_GROUP]

        def pair_loop(p, _):
            blocks = []
            for slot in range(2):
                qb = 2 * p + slot
                q0 = pl.multiple_of(qb * ATT_BLOCK, ATT_BLOCK)
                n_tiles = (qb + ATT_GROUP) // ATT_GROUP
                seen = ((jnp.max(r_ref[pl.ds(q0, ATT_BLOCK), :], axis=0, keepdims=True) > 0.5 * ATT_UNVISITED)
                        & (lane[0:1, :] < n_tiles))
                n_visited = jnp.maximum(jnp.sum(jnp.where(seen, 1.0, 0.0)).astype(jnp.int32), 1)
                blocks.append((qb, q0, q_ref[pl.ds(q0, ATT_BLOCK), :], do_ref[pl.ds(q0, ATT_BLOCK), :], n_visited))
            for slot in range(2):
                qb, q0, q, dov, n_visited = blocks[slot]
                tile(slot, qb, q0, q, dov, n_visited - 1, True)
            for slot in range(2):
                qb, q0, q, dov, n_visited = blocks[slot]
                lax.fori_loop(1, n_visited, lambda j, c: (tile(slot, qb, q0, q, dov, n_visited - 1 - j, False), c)[1], 0)
                dqkv_ref[0, pl.ds(q0, ATT_BLOCK), :] = dq_s[slot].astype(dqkv_ref.dtype)
            return 0

        lax.fori_loop(0, nq // 2, pair_loop, 0)
        dqkv_ref[1, :, :] = dk_s[...].astype(dqkv_ref.dtype)
        dqkv_ref[2, :, :] = dv_s[...].astype(dqkv_ref.dtype)

    return _carry_call(
        body, name=name, steps=heads,
        in_specs=[_head_spec(T, 0, heads), _head_spec(T, 1, heads), _head_spec(T, 2, heads),
                  pl.BlockSpec((T, ATT_HEAD_DIM), lambda h: (0, h)), pl.BlockSpec((None, T, LANES), lambda h: (h, 0, 0))],
        out_specs=[pl.BlockSpec((3, T, ATT_HEAD_DIM), lambda h: (0, 0, h))],
        out_shape=[jax.ShapeDtypeStruct((3, T, D), BF16)],
        scratch_shapes=[pltpu.VMEM((T, ATT_HEAD_DIM), F32), pltpu.VMEM((T, ATT_HEAD_DIM), F32),
                        pltpu.VMEM((2, ATT_BLOCK, ATT_HEAD_DIM), F32), pltpu.VMEM((2, ATT_BLOCK, LANES), F32)],
        args=(qkv, qkv, qkv, do, rmat), jobs=jobs)


def _block_diag_pairs(w):
    h = w.shape[0]
    wp = w.reshape(h // 2, 2, RG_HEAD_DIM, RG_HEAD_DIM)
    z = jnp.zeros_like(wp[:, 0])
    top = jnp.concatenate([wp[:, 0], z], axis=2)
    bot = jnp.concatenate([z, wp[:, 1]], axis=2)
    return jnp.concatenate([top, bot], axis=1)


def _diag_pairs(g):
    n = g.shape[0]
    a = g[:, :RG_HEAD_DIM, :RG_HEAD_DIM]
    b = g[:, RG_HEAD_DIM:, RG_HEAD_DIM:]
    return jnp.stack([a, b], axis=1).reshape(2 * n, RG_HEAD_DIM, RG_HEAD_DIM)


class _Weights:
    def __init__(self, full, shards=None, plan=None):
        self.full, self.shards, self.plan = dict(full), shards or {}, plan or {}
        self.partial, self.rows = {}, {}

    def __getitem__(self, name):
        return self.full[name]

    def jobs(self, call):
        return [_gather_job(self.shards[n], self.partial.get(n), lo, hi, parts)
                for n, lo, hi, parts in self.plan.get(call, ())]

    def deliver(self, call, outs):
        for (n, lo, hi, parts), g in zip(self.plan.get(call, ()), outs):
            self.partial[n] = g
            self.rows[n] = self.rows.get(n, 0) + hi - lo
            if self.rows[n] == parts:
                self.full[n] = _gathered_layout(n, g)


def _gathered_layout(name, g):
    if name in ("w_in", "w_qkv", "w_out", "w_o"):
        return g.reshape(g.shape[0] * g.shape[1], g.shape[2])
    return g


class _Grads:
    def __init__(self, lands=None, plan=None):
        self.lands, self.plan = dict(lands) if lands else None, plan or {}
        self.ready, self.sent = {}, {}

    def put(self, name, arr):
        self.ready[name] = arr

    def jobs(self, call):
        if self.lands is None:
            return []
        return [_exchange_job(self.ready[n], self.lands[n], lo, hi, parts) for n, lo, hi, parts in self.plan.get(call, ())]

    def deliver(self, call, outs):
        for (n, lo, hi, parts), o in zip(self.plan.get(call, ()), outs):
            assert self.sent.get(n, (0, parts)) == (lo, parts), (call, n)
            self.lands[n] = o
            self.sent[n] = (hi, parts)

    def flush(self, name):
        if self.lands is None:
            return
        rest = []
        for n in self.ready:
            lo, parts = self.sent.get(n, (0, 1))
            if lo < parts:
                rest.append((n, lo, parts, parts))
        if rest:
            outs = _run_jobs(name, [_exchange_job(self.ready[n], self.lands[n], lo, hi, parts) for n, lo, hi, parts in rest])
            for (n, _, hi, parts), o in zip(rest, outs):
                self.lands[n] = o
                self.sent[n] = (hi, parts)


def _mlp_fwd(tag, h, wts, run):
    T, D = h.shape
    w_up = wts["up" + tag]
    fb = w_up.shape[2]
    F = fb * N_DEV
    tm, tn, tk = _tile(T, MM_TM), _tile(fb, MM_TN), _tile(D, MM_TK)
    nb = fb // tn

    def up_epilogue(u):
        r = jnp.maximum(u, 0.0)
        return u, r * r

    o_spec = pl.BlockSpec((tm, tn), lambda i, j, k: (i, j))
    u, act = run(
        _matmul, f"mlp_up_l{tag}",
        [(h, pl.BlockSpec((tm, tk), lambda i, j, k: (i, k))),
         (w_up, pl.BlockSpec((None, tk, tn), lambda i, j, k: (j // nb, k, j % nb)))],
        [(jax.ShapeDtypeStruct((T, F), BF16), o_spec), (jax.ShapeDtypeStruct((T, F), BF16), o_spec)],
        (T // tm, F // tn, D // tk), DIMS_NN, (tm, tn), up_epilogue, n_main=2)
    w_down = wts["down" + tag].reshape(F, D)
    m = run(_mm_nn, f"mlp_down_l{tag}", act, w_down, BF16)
    return u, act, m


def _mlp_bwd(tag, h, u, act, dm, wts, grads, run):
    T, D = h.shape
    w_up, w_down = wts["up" + tag], wts["down" + tag]
    fb = w_up.shape[2]
    F = fb * N_DEV
    grads.put("down" + tag, run(_mm_tn, f"mlp_down_dw_l{tag}", act, dm, BF16).reshape(N_DEV, fb, D))
    tm, tn, tk = _tile(T, MM_TM), _tile(fb, MM_TN), _tile(D, MM_TK)
    nb = fb // tn
    o_spec = pl.BlockSpec((tm, tn), lambda i, j, k: (i, j))
    du = run(
        _matmul, f"mlp_down_dx_l{tag}",
        [(dm, pl.BlockSpec((tm, tk), lambda i, j, k: (i, k))),
         (w_down, pl.BlockSpec((None, tn, tk), lambda i, j, k: (j // nb, j % nb, k))),
         (u, o_spec)],
        [(jax.ShapeDtypeStruct((T, F), BF16), o_spec)],
        (T // tm, F // tn, D // tk), DIMS_NT, (tm, tn),
        lambda r, uv: (r * (2.0 * jnp.maximum(uv.astype(F32), 0.0)),))
    grads.put("up" + tag, run(_mm_tn, f"mlp_up_dw_l{tag}", h, du, BF16, out_blocks=N_DEV))
    tn2 = _tile(D, MM_TN)
    pair = 2 if MM_TK >= 2 * fb else 1
    return run(
        _matmul, f"mlp_up_dx_l{tag}",
        [(du, pl.BlockSpec((tm, pair * fb), lambda i, j, k: (i, k))),
         (w_up, pl.BlockSpec((pair, tn2, fb), lambda i, j, k: (k, j, 0)))],
        [(jax.ShapeDtypeStruct((T, D), BF16), pl.BlockSpec((tm, tn2), lambda i, j, k: (i, j)))],
        (T // tm, D // tn2, N_DEV // pair), DIMS_NT, (tm, tn2), None)


def _local_step(x, target, gains, conv_a, conv_b, conv_b_bias, rg_w_a, rg_b_a, rg_w_x, rg_b_x, rg_lambda, wts, grads):
    T, D = x.shape
    g = lambda l, i: gains[l, i][None, :]
    wa_p = _block_diag_pairs(rg_w_a).astype(BF16)
    wx_p = _block_diag_pairs(rg_w_x).astype(BF16)

    def run(fn, name, *args, n_main=1, **kw):
        jw, jg = wts.jobs(name), grads.jobs(name)
        res = fn(name, *args, jobs=jw + jg, **kw)
        main, jo = res[:n_main], res[n_main:]
        wts.deliver(name, jo[:len(jw)])
        grads.deliver(name, jo[len(jw):])
        return main[0] if n_main == 1 else main

    h0 = run(_norm_fwd, "norm_in", x, g(0, 0))
    proj = run(_mm_nt, "w_in_fwd", h0, wts["w_in"], F32, out_seg=5)
    y = run(_mixer_b_fwd, "mixer_b_fwd", proj, conv_b, conv_b_bias, wa_p, rg_b_a, wx_p, rg_b_x, rg_lambda,
            _mixer_a_fwd(proj, conv_a))
    mix0 = run(_mm_nn, "w_out_fwd", y, wts["w_out"], BF16, a_seg=2)
    x1, h1 = run(_resid_norm, "resid_mix0", x, mix0, g(0, 1), g(0, 2), n_main=2)
    u0, act0, m0 = _mlp_fwd("0", h1, wts, run)
    x2, h2 = run(_resid_norm, "resid_mlp0", x1, m0, g(0, 3), g(1, 0), n_main=2)
    qkv = run(_mm_nt, "w_qkv_fwd", h2, wts["w_qkv"], BF16, out_seg=3)
    o, rmat = run(_attention_fwd, "attention_fwd", qkv, n_main=2)
    mix1 = run(_mm_nn, "w_o_fwd", o, wts["w_o"], BF16)
    x3, h3 = run(_resid_norm, "resid_mix1", x2, mix1, g(1, 1), g(1, 2), n_main=2)
    u1, act1, m1 = _mlp_fwd("1", h3, wts, run)
    dx4, dm1, dg13, sq = _final_loss("loss", x3, m1, g(1, 3), target)

    dh3 = _mlp_bwd("1", h3, u1, act1, dm1, wts, grads, run)
    dx3, dmix1, dg12, dg11 = _norm_bwd_pair("norm_bwd_x3", x3, g(1, 2), dh3, dx4, mix1, g(1, 1))
    grads.put("w_o", run(_mm_tn, "w_o_dw", o, dmix1, BF16).reshape(N_DEV, D // N_DEV, D))
    do = run(_mm_nt, "w_o_dx", dmix1, wts["w_o"], BF16)
    dqkv = run(_attention_bwd, "attention_bwd", qkv, do, rmat)
    grads.put("w_qkv", run(_mm_tn, "w_qkv_dw", dqkv, h2, BF16, a_seg=3).reshape(N_DEV, 3 * D // N_DEV, D))
    dh2 = run(_mm_nn, "w_qkv_dx", dqkv, wts["w_qkv"], BF16, a_seg=3)
    dx2, dm0, dg10, dg03 = _norm_bwd_pair("norm_bwd_x2", x2, g(1, 0), dh2, dx3, m0, g(0, 3))
    dh1 = _mlp_bwd("0", h1, u0, act0, dm0, wts, grads, run)
    dx1, dmix0, dg02, dg01 = _norm_bwd_pair("norm_bwd_x1", x1, g(0, 2), dh1, dx2, mix0, g(0, 1))
    grads.put("w_out", run(_mm_tn, "w_out_dw", y, dmix0, BF16, a_seg=2).reshape(N_DEV, D // N_DEV, D))
    dy = run(_mm_nt, "w_out_dx", dmix0, wts["w_out"], F32, out_seg=2)
    dproj_a, dconv_a = _mixer_a_bwd(proj, conv_a, dy)
    dproj, sm_b, dwa_p, dwx_p = run(_mixer_b_bwd, "mixer_b_bwd", proj, conv_b, conv_b_bias, wa_p, rg_b_a, wx_p, rg_b_x,
                                    rg_lambda, dy, dproj_a, n_main=4)
    grads.put("w_in", run(_mm_tn, "w_in_dw", dproj, h0, BF16, a_seg=5).reshape(N_DEV, 5 * D // (2 * N_DEV), D))
    dh0 = run(_mm_nn, "w_in_dx", dproj, wts["w_in"], BF16, a_seg=5)
    dx0, dg00 = run(_norm_bwd, "norm_bwd_x0", x, g(0, 0), dh0, dx1, n_main=2)

    C = D // 2
    lanes_to_vec = lambda t, row: t[:, row, :].reshape(1, C)
    small = {
        "norm_gains": jnp.concatenate([dg00, dg01, dg02, dg03, dg10, dg11, dg12, dg13], axis=0).reshape(2, 4, D),
        "conv_a": jnp.transpose(dconv_a[:, :3, :], (1, 0, 2)).reshape(3, C),
        "conv_b": jnp.transpose(sm_b[:, :4, :], (1, 0, 2)).reshape(4, C),
        "conv_b_bias": lanes_to_vec(sm_b, _ROW_BIAS),
        "rg_w_a": _diag_pairs(dwa_p),
        "rg_b_a": lanes_to_vec(sm_b, _ROW_BA),
        "rg_w_x": _diag_pairs(dwx_p),
        "rg_b_x": lanes_to_vec(sm_b, _ROW_BX),
        "rg_lambda": lanes_to_vec(sm_b, _ROW_LAM),
    }
    return sq[0, 0], dx0, small


def _my_index():
    return 4 * lax.axis_index("x") + 2 * lax.axis_index("y") + lax.axis_index("c")


def _peers():
    x, y, c = lax.axis_index("x"), lax.axis_index("y"), lax.axis_index("c")
    out = []
    for k in range(1, N_DEV):
        px = x ^ ((k >> 2) & 1)
        py = y ^ ((k >> 1) & 1)
        pc = c ^ (k & 1)
        out.append(((px, py, pc), 4 * px + 2 * py + pc))
    return out


GATHER_PLAN = {
    "norm_in": (("w_in", 0, 1, 1),),
    "w_in_fwd": (("w_out", 0, 1, 1), ("up0", 0, 1, 4)),
    "mixer_b_fwd": (("up0", 1, 3, 4),),
    "w_out_fwd": (("up0", 3, 4, 4),),
    "resid_mix0": (("down0", 0, 1, 4),),
    "mlp_up_l0": (("down0", 1, 4, 4),),
    "mlp_down_l0": (("w_qkv", 0, 1, 1),),
    "resid_mlp0": (("up1", 0, 1, 4),),
    "w_qkv_fwd": (("w_o", 0, 1, 1), ("up1", 1, 2, 4)),
    "attention_fwd": (("up1", 2, 4, 4), ("down1", 0, 2, 4)),
    "mlp_up_l1": (("down1", 2, 4, 4),),
}
EXCHANGE_PLAN = {
    "mlp_down_dx_l1": (("down1", 0, 3, 8),), "mlp_up_dw_l1": (("down1", 3, 6, 8),),
    "mlp_up_dx_l1": (("down1", 6, 8, 8), ("up1", 0, 1, 8)),
    "attention_bwd": (("up1", 1, 8, 8), ("w_o", 0, 1, 1)),
    "w_qkv_dx": (("w_qkv", 0, 1, 2),), "mlp_down_dw_l0": (("w_qkv", 1, 2, 2),),
    "mlp_down_dx_l0": (("down0", 0, 3, 8),), "mlp_up_dw_l0": (("down0", 3, 6, 8),),
    "mlp_up_dx_l0": (("down0", 6, 8, 8), ("up0", 0, 1, 8)),
    "w_out_dw": (("up0", 1, 2, 8),), "w_out_dx": (("up0", 2, 3, 8),),
    "mixer_b_bwd": (("up0", 3, 7, 8),),
    "w_in_dw": (("up0", 7, 8, 8), ("w_out", 0, 1, 2)),
    "w_in_dx": (("w_out", 1, 2, 2), ("w_in", 0, 1, 4)),
    "norm_bwd_x0": (("w_in", 1, 2, 4),),
    "adamw_mlp_w_down": (("w_in", 2, 3, 4),), "adamw_mlp_w_up": (("w_in", 3, 4, 4),),
}


def _job_sems():
    return [pltpu.SemaphoreType.DMA((N_DEV - 1,)), pltpu.SemaphoreType.DMA((N_DEV - 1,)), pltpu.SemaphoreType.DMA((1,))]


def _gather_job(shard, prev=None, lo=0, hi=1, parts=1):
    n = shard.shape[0] // parts
    assert n * parts == shard.shape[0]
    rows = pl.ds(lo * n, (hi - lo) * n)

    def ctx():
        x, y, c = lax.axis_index("x"), lax.axis_index("y"), lax.axis_index("c")
        chips = [(1 - x, y), (x, 1 - y), (1 - x, 1 - y)]
        return x, y, c, chips

    def idx(px, py, pc):
        return 4 * px + 2 * py + pc

    def copy(src, out, sems, k, block, to):
        return pltpu.make_async_remote_copy(
            src_ref=out.at[block, rows] if src is None else src.at[rows], dst_ref=out.at[block, rows],
            send_sem=sems[0].at[k], recv_sem=sems[1].at[k], device_id=to, device_id_type=MESH)

    def start(ins, outs, sems):
        x, y, c, chips = ctx()
        src, out = ins[0], outs[0]
        me = idx(x, y, c)
        pltpu.make_async_copy(src.at[rows], out.at[me, rows], sems[2].at[0]).start()
        copy(src, out, sems, 0, me, (x, y, 1 - c)).start()
        for j, (px, py) in enumerate(chips):
            copy(src, out, sems, 1 + j, me, (px, py, c)).start()

    def mid(ins, outs, sems):
        x, y, c, chips = ctx()
        out = outs[0]
        for j, (px, py) in enumerate(chips):
            copy(None, out, sems, 1 + j, idx(px, py, c), (x, y, c)).wait_recv()
            copy(None, out, sems, 4 + j, idx(px, py, c), (x, y, 1 - c)).start()

    def end(ins, outs, sems):
        x, y, c, chips = ctx()
        src, out = ins[0], outs[0]
        me = (x, y, c)
        copy(None, out, sems, 0, idx(x, y, 1 - c), me).wait_recv()
        for j, (px, py) in enumerate(chips):
            copy(None, out, sems, 4 + j, idx(px, py, 1 - c), me).wait_recv()
        for k in range(N_DEV - 1):
            copy(src, out, sems, k, idx(x, y, c), me).wait_send()
        pltpu.make_async_copy(src.at[rows], out.at[idx(x, y, c), rows], sems[2].at[0]).wait()

    out_shape = jax.ShapeDtypeStruct((N_DEV,) + shard.shape, shard.dtype)
    if prev is None:
        return _Job([shard], [out_shape], _job_sems(), start, mid, end)
    return _Job([shard, prev], [out_shape], _job_sems(), start, mid, end, alias={1: 0})


def _exchange_job(src, land, lo=0, hi=1, parts=1):
    n = src.shape[1] // parts
    assert n * parts == src.shape[1]

    def sl(ref, s):
        return ref.at[s, pl.ds(lo * n, (hi - lo) * n)]

    def start(ins, outs, sems):
        me = _my_index()
        pltpu.make_async_copy(sl(ins[0], me), sl(outs[0], me), sems[2].at[0]).start()
        for k, (pos, idx) in enumerate(_peers()):
            pltpu.make_async_remote_copy(
                src_ref=sl(ins[0], idx), dst_ref=sl(outs[0], me), send_sem=sems[0].at[k], recv_sem=sems[1].at[k],
                device_id=pos, device_id_type=MESH).start()

    def mid(ins, outs, sems):
        pass

    def end(ins, outs, sems):
        me = _my_index()
        for k, (pos, idx) in enumerate(_peers()):
            cp = pltpu.make_async_remote_copy(
                src_ref=sl(ins[0], idx), dst_ref=sl(outs[0], idx), send_sem=sems[0].at[k], recv_sem=sems[1].at[k],
                device_id=pos, device_id_type=MESH)
            cp.wait_recv()
            cp.wait_send()
        pltpu.make_async_copy(sl(ins[0], me), sl(outs[0], me), sems[2].at[0]).wait()

    return _Job([src, land], [jax.ShapeDtypeStruct(land.shape, land.dtype)], _job_sems(), start, mid, end, alias={1: 0})


def _adamw_math(w, g, m, v):
    m = ADAM_B1 * m + (1.0 - ADAM_B1) * g
    v = ADAM_B2 * v + (1.0 - ADAM_B2) * (g * g)
    m_hat = m / (1.0 - ADAM_B1 ** ADAM_STEP)
    v_hat = v / (1.0 - ADAM_B2 ** ADAM_STEP)
    delta = -ADAM_LR * (m_hat / (jnp.sqrt(v_hat) + ADAM_EPS) + ADAM_WD * w)
    return delta, m, v


def _sum_slots(ref):
    g = ref[0].astype(F32)
    for s in range(1, N_DEV):
        g = g + ref[s].astype(F32)
    return g


def _adamw_big(name, lands, w, m, v, jobs=(), transposed=False):
    L, R, C = w.shape
    assert len(lands) == L
    tr = _tile(R, max(LANES, (256 * 1024) // C))
    nr = R // tr

    def body(*refs):
        l_refs = refs[:L]
        w_ref, m_ref, v_ref, g_ref, d_ref, nm_ref, nv_ref = refs[L:]
        for li in range(L):
            @pl.when(pl.program_id(0) // nr == li)
            def _(li=li):
                g = _sum_slots(l_refs[li])
                if transposed:
                    g = g.T
                d, nm, nv = _adamw_math(w_ref[...], g, m_ref[...], v_ref[...])
                g_ref[...] = g
                d_ref[...] = d
                nm_ref[...] = nm
                nv_ref[...] = nv

    def land_spec(li):
        if transposed:
            return pl.BlockSpec((N_DEV, C, tr), lambda s: (0, 0, jnp.where(s // nr == li, s % nr, 0)))
        return pl.BlockSpec((N_DEV, tr, C), lambda s: (0, jnp.where(s // nr == li, s % nr, 0), 0))

    row = pl.BlockSpec((None, tr, C), lambda s: (s // nr, s % nr, 0))
    return _carry_call(
        body, name=name, steps=L * nr, in_specs=[land_spec(li) for li in range(L)] + [row, row, row],
        out_specs=[row] * 4, out_shape=[jax.ShapeDtypeStruct((L, R, C), F32)] * 4, scratch_shapes=[],
        args=(*lands, w, m, v), jobs=jobs)


def _sum8(name, slots):
    _, R, C = slots.shape

    def body(s_ref, o_ref):
        o_ref[...] = _sum_slots(s_ref)

    return pl.pallas_call(body, name=name, out_shape=jax.ShapeDtypeStruct((R, C), F32))(slots)


def _adamw_small(name, g, w, m, v):
    def body(g_ref, w_ref, m_ref, v_ref, d_ref, nm_ref, nv_ref):
        d, nm, nv = _adamw_math(w_ref[...], g_ref[...], m_ref[...], v_ref[...])
        d_ref[...] = d
        nm_ref[...] = nm
        nv_ref[...] = nv

    return pl.pallas_call(body, name=name, out_shape=[jax.ShapeDtypeStruct(w.shape, F32)] * 3)(g, w, m, v)


def _pack_rows(arrs):
    parts, spans, r0 = [], [], 0
    for a in arrs:
        flat = a.astype(F32).reshape(-1)
        rows = -(-flat.shape[0] // LANES)
        rows = -(-rows // SUBLANES) * SUBLANES
        flat = jnp.pad(flat, (0, rows * LANES - flat.shape[0]))
        parts.append(flat.reshape(rows, LANES))
        spans.append((r0, rows, a.shape))
        r0 += rows
    return jnp.concatenate(parts, axis=0), spans


def _unpack_rows(buf, span):
    r0, rows, shape = span
    n = math.prod(shape)
    return buf[..., r0:r0 + rows, :].reshape(buf.shape[:-2] + (rows * LANES,))[..., :n].reshape(buf.shape[:-2] + shape)


def _from_col_blocks(wb):
    B, K, n = wb.shape
    return jnp.transpose(wb, (1, 0, 2)).reshape(K, B * n)


def kernel(x, norm_gains, hyb_w_in, hyb_conv_a, hyb_conv_b, hyb_conv_b_bias, hyb_rg_w_a, hyb_rg_b_a, hyb_rg_w_x, hyb_rg_b_x, hyb_rg_lambda, hyb_w_out, sb_w_qkv, sb_w_o, mlp_w_up, mlp_w_down, loss_target, m_norm_gains, m_hyb_w_in, m_hyb_conv_a, m_hyb_conv_b, m_hyb_conv_b_bias, m_hyb_rg_w_a, m_hyb_rg_b_a, m_hyb_rg_w_x, m_hyb_rg_b_x, m_hyb_rg_lambda, m_hyb_w_out, m_sb_w_qkv, m_sb_w_o, m_mlp_w_up, m_mlp_w_down, v_norm_gains, v_hyb_w_in, v_hyb_conv_a, v_hyb_conv_b, v_hyb_conv_b_bias, v_hyb_rg_w_a, v_hyb_rg_b_a, v_hyb_rg_w_x, v_hyb_rg_b_x, v_hyb_rg_lambda, v_hyb_w_out, v_sb_w_qkv, v_sb_w_o, v_mlp_w_up, v_mlp_w_down):
    T, D = x.shape[1], x.shape[2]
    me = _my_index()

    small_shards, small_spans = _pack_rows([norm_gains, hyb_conv_a[0], hyb_conv_b[0]])
    (small_all,) = _run_jobs("gather_small", [_gather_job(small_shards)])
    gains_b = _unpack_rows(small_all, small_spans[0])
    gains = jnp.transpose(gains_b, (1, 2, 0, 3)).reshape(2, 4, D)
    conv_a = _from_col_blocks(_unpack_rows(small_all, small_spans[1]))
    conv_b = _from_col_blocks(_unpack_rows(small_all, small_spans[2]))

    shards = {"w_in": hyb_w_in[0].T, "w_out": hyb_w_out[0], "w_qkv": sb_w_qkv[0].T, "w_o": sb_w_o[0],
              "up0": mlp_w_up[0], "up1": mlp_w_up[1], "down0": mlp_w_down[0], "down1": mlp_w_down[1]}
    shards = {n: s.astype(BF16) for n, s in shards.items()}
    wts = _Weights({}, shards, GATHER_PLAN)
    grads_big = _Grads({n: lax.empty((N_DEV,) + s.shape, BF16) for n, s in shards.items()}, EXCHANGE_PLAN)

    sq, grad_x, small = _local_step(
        x[0], loss_target[0], gains, conv_a, conv_b, hyb_conv_b_bias, hyb_rg_w_a[0], hyb_rg_b_a, hyb_rg_w_x[0],
        hyb_rg_b_x, hyb_rg_lambda, wts, grads_big)


    names = ["norm_gains", "hyb_w_in", "hyb_conv_a", "hyb_conv_b", "hyb_conv_b_bias", "hyb_rg_w_a", "hyb_rg_b_a",
             "hyb_rg_w_x", "hyb_rg_b_x", "hyb_rg_lambda", "hyb_w_out", "sb_w_qkv", "sb_w_o", "mlp_w_up", "mlp_w_down"]
    params = dict(zip(names, [norm_gains, hyb_w_in, hyb_conv_a, hyb_conv_b, hyb_conv_b_bias, hyb_rg_w_a, hyb_rg_b_a,
                              hyb_rg_w_x, hyb_rg_b_x, hyb_rg_lambda, hyb_w_out, sb_w_qkv, sb_w_o, mlp_w_up, mlp_w_down]))
    moms = dict(zip(names, [m_norm_gains, m_hyb_w_in, m_hyb_conv_a, m_hyb_conv_b, m_hyb_conv_b_bias, m_hyb_rg_w_a,
                            m_hyb_rg_b_a, m_hyb_rg_w_x, m_hyb_rg_b_x, m_hyb_rg_lambda, m_hyb_w_out, m_sb_w_qkv,
                            m_sb_w_o, m_mlp_w_up, m_mlp_w_down]))
    vars_ = dict(zip(names, [v_norm_gains, v_hyb_w_in, v_hyb_conv_a, v_hyb_conv_b, v_hyb_conv_b_bias, v_hyb_rg_w_a,
                             v_hyb_rg_b_a, v_hyb_rg_w_x, v_hyb_rg_b_x, v_hyb_rg_lambda, v_hyb_w_out, v_sb_w_qkv,
                             v_sb_w_o, v_mlp_w_up, v_mlp_w_down]))
    grads, deltas, new_m, new_v = {}, {}, {}, {}

    big_lands = {"mlp_w_down": ["down0", "down1"], "mlp_w_up": ["up0", "up1"], "sb_w_qkv": ["w_qkv"], "sb_w_o": ["w_o"],
                 "hyb_w_out": ["w_out"], "hyb_w_in": ["w_in"]}
    for nm, keys in big_lands.items():
        call = f"adamw_{nm}"
        if nm == "hyb_w_in":
            grads_big.flush("exchange_grads")
        jobs = grads_big.jobs(call)
        assert not {k for k in keys} & {e[0] for e in EXCHANGE_PLAN.get(call, ())}
        outs = _adamw_big(call, [grads_big.lands[k] for k in keys], params[nm], moms[nm], vars_[nm], jobs=jobs,
                          transposed=nm in ("hyb_w_in", "sb_w_qkv"))
        grads[nm], deltas[nm], new_m[nm], new_v[nm] = outs[:4]
        grads_big.deliver(call, outs[4:])

    small_names = ["norm_gains", "hyb_conv_a", "hyb_conv_b", "hyb_conv_b_bias", "hyb_rg_w_a", "hyb_rg_b_a",
                   "hyb_rg_w_x", "hyb_rg_b_x", "hyb_rg_lambda"]
    small_keys = ["norm_gains", "conv_a", "conv_b", "conv_b_bias", "rg_w_a", "rg_b_a", "rg_w_x", "rg_b_x", "rg_lambda"]
    sg_buf, sg_spans = _pack_rows([small[k] for k in small_keys] + [sq.reshape(1)])
    (sg_all,) = _run_jobs("gather_small_grads", [_gather_job(sg_buf)])
    sg_sum = _sum8("sum_small_grads", sg_all)
    full = {nm: _unpack_rows(sg_sum, sp) for nm, sp in zip(small_names, sg_spans)}
    loss = _unpack_rows(sg_sum, sg_spans[-1])[0] * (0.5 / D)
    cb = (D // 2) // N_DEV
    small_grads = {
        "norm_gains": lax.dynamic_slice_in_dim(full["norm_gains"], me * (D // N_DEV), D // N_DEV, axis=2),
        "hyb_conv_a": lax.dynamic_slice_in_dim(full["hyb_conv_a"], me * cb, cb, axis=1)[None],
        "hyb_conv_b": lax.dynamic_slice_in_dim(full["hyb_conv_b"], me * cb, cb, axis=1)[None],
        "hyb_conv_b_bias": full["hyb_conv_b_bias"],
        "hyb_rg_w_a": full["hyb_rg_w_a"][None],
        "hyb_rg_b_a": full["hyb_rg_b_a"],
        "hyb_rg_w_x": full["hyb_rg_w_x"][None],
        "hyb_rg_b_x": full["hyb_rg_b_x"],
        "hyb_rg_lambda": full["hyb_rg_lambda"],
    }
    pk = lambda d: _pack_rows([d[nm] for nm in small_names])
    g_buf, spans = pk(small_grads)
    w_buf, _ = pk(params)
    m_buf, _ = pk(moms)
    v_buf, _ = pk(vars_)
    d_buf, nm_buf, nv_buf = _adamw_small("adamw_small", g_buf, w_buf, m_buf, v_buf)
    for nm, sp in zip(small_names, spans):
        grads[nm] = small_grads[nm]
        deltas[nm], new_m[nm], new_v[nm] = _unpack_rows(d_buf, sp), _unpack_rows(nm_buf, sp), _unpack_rows(nv_buf, sp)

    return (loss, grad_x[None], *[grads[n] for n in names], *[deltas[n] for n in names],
            *[new_m[n] for n in names], *[new_v[n] for n in names])
```

```python
import math

import jax
import jax.numpy as jnp
from jax import lax
from jax.experimental import pallas as pl
from jax.experimental.pallas import tpu as pltpu

F32 = jnp.float32
BF16 = jnp.bfloat16

NORM_EPS = 1e-6
LRU_C = 8.0
ATT_HEAD_DIM = 128
RG_HEAD_DIM = 64
LANES = 128
SUBLANES = 8
N_DEV = 8
ADAM_LR = 0.001
ADAM_B1 = 0.9
ADAM_B2 = 0.999
ADAM_EPS = 1e-08
ADAM_WD = 0.01
ADAM_STEP = 10
VMEM_LIMIT = 56 * 1024 * 1024
MM_TK = 2048
MM_TM = 2048
MM_TN = 1024
MM_TK_TOKENS = 4096
MESH = pl.DeviceIdType.MESH


def _tile(n, pref):
    if n <= pref:
        return n
    t = (pref // LANES) * LANES
    while t > LANES and n % t:
        t -= LANES
    assert n % t == 0, (n, pref)
    return t


def _params(sem):
    return pltpu.CompilerParams(dimension_semantics=sem, vmem_limit_bytes=VMEM_LIMIT)


DIMS_NN = (((1,), (0,)), ((), ()))
DIMS_NT = (((1,), (1,)), ((), ()))
DIMS_TN = (((0,), (0,)), ((), ()))


_ANY = pl.BlockSpec(memory_space=pl.ANY)


class _Job:
    def __init__(self, ins, outs, sems, start, mid, end, alias=None):
        self.ins, self.outs, self.sems = ins, outs, sems
        self.start, self.mid, self.end = start, mid, end
        self.alias = alias or {}


def _mid_step(steps):
    return (9 * steps) // 10


def _job_plumbing(jobs, n_in, n_out):
    j_ins = [a for jb in jobs for a in jb.ins]
    j_outs = [o for jb in jobs for o in jb.outs]
    j_sems = [s for jb in jobs for s in jb.sems]
    aliases, pi, po = {}, 0, 0
    for jb in jobs:
        for i_in, i_out in jb.alias.items():
            aliases[n_in + pi + i_in] = n_out + po + i_out
        pi += len(jb.ins)
        po += len(jb.outs)
    return j_ins, j_outs, j_sems, aliases


def _job_phase(jobs, which, jin, jout, jsem):
    pi = po = ps = 0
    for jb in jobs:
        getattr(jb, which)(jin[pi:pi + len(jb.ins)], jout[po:po + len(jb.outs)], jsem[ps:ps + len(jb.sems)])
        pi, po, ps = pi + len(jb.ins), po + len(jb.outs), ps + len(jb.sems)


def _run_jobs(name, jobs):
    j_ins, j_outs, j_sems, aliases = _job_plumbing(jobs, 0, 0)
    n_ji, n_jo = len(j_ins), len(j_outs)

    def body(*refs):
        jin, jout, jsem = refs[:n_ji], refs[n_ji:n_ji + n_jo], refs[n_ji + n_jo:]
        for which in ("start", "mid", "end"):
            _job_phase(jobs, which, jin, jout, jsem)

    return pl.pallas_call(body, name=name, in_specs=[_ANY] * n_ji, out_specs=[_ANY] * n_jo, out_shape=j_outs,
                          scratch_shapes=j_sems, input_output_aliases=aliases)(*j_ins)


def _carry_call(body, *, name, steps, in_specs, out_specs, out_shape, scratch_shapes, args, jobs=(), aliases=None,
                sequential=False):
    n_in, n_out, n_sc = len(in_specs), len(out_shape), len(scratch_shapes)
    j_ins, j_outs, j_sems, j_aliases = _job_plumbing(jobs, n_in, n_out)
    n_ji, n_jo = len(j_ins), len(j_outs)

    def wrapped(*refs):
        ins, jin = refs[:n_in], refs[n_in:n_in + n_ji]
        o0 = n_in + n_ji
        outs, jout = refs[o0:o0 + n_out], refs[o0 + n_out:o0 + n_out + n_jo]
        s0 = o0 + n_out + n_jo
        scratch, jsem = refs[s0:s0 + n_sc], refs[s0 + n_sc:]
        step = pl.program_id(0)
        if jobs:
            pl.when(step == 0)(lambda: _job_phase(jobs, "start", jin, jout, jsem))
            pl.when(step == _mid_step(steps))(lambda: _job_phase(jobs, "mid", jin, jout, jsem))
        body(*ins, *outs, *scratch)
        if jobs:
            pl.when(step == steps - 1)(lambda: _job_phase(jobs, "end", jin, jout, jsem))

    return pl.pallas_call(
        wrapped, name=name, grid=(steps,),
        in_specs=list(in_specs) + [_ANY] * n_ji, out_specs=list(out_specs) + [_ANY] * n_jo,
        out_shape=list(out_shape) + j_outs, scratch_shapes=list(scratch_shapes) + j_sems,
        input_output_aliases={**(aliases or {}), **j_aliases},
        compiler_params=_params(("arbitrary",) if (jobs or sequential) else ("parallel",)))(*args, *j_ins)


def _matmul(name, ins, outs, grid, dims, acc_shape, epilogue=None, jobs=()):
    n_in, n_out, nk = len(ins), len(outs), grid[2]
    j_ins, j_outs, j_sems, aliases = _job_plumbing(jobs, n_in, n_out)
    n_ji, n_jo = len(j_ins), len(j_outs)
    total = grid[0] * grid[1] * grid[2]
    n_acc = 0 if nk == 1 else 1

    def body(*refs):
        a_ref, b_ref = refs[0], refs[1]
        extras = refs[2:n_in]
        jin = refs[n_in:n_in + n_ji]
        out_refs = refs[n_in + n_ji:n_in + n_ji + n_out]
        jout = refs[n_in + n_ji + n_out:n_in + n_ji + n_out + n_jo]
        jsem = refs[n_in + n_ji + n_out + n_jo + n_acc:]
        k = pl.program_id(2)
        step = (pl.program_id(0) * grid[1] + pl.program_id(1)) * grid[2] + k
        if jobs:
            pl.when(step == 0)(lambda: _job_phase(jobs, "start", jin, jout, jsem))
            pl.when(step == _mid_step(total))(lambda: _job_phase(jobs, "mid", jin, jout, jsem))

        def finish(r):
            res = epilogue(r, *[e[...] for e in extras]) if epilogue is not None else (r,)
            for o, v in zip(out_refs, res):
                o[...] = v.astype(o.dtype)

        if len(b_ref.shape) == 2:
            prod = lax.dot_general(a_ref[...], b_ref[...], dims, preferred_element_type=F32)
        else:
            kb = a_ref.shape[1] // b_ref.shape[0]
            prod = sum(lax.dot_general(a_ref[:, g * kb:(g + 1) * kb], b_ref[g], dims, preferred_element_type=F32)
                       for g in range(b_ref.shape[0]))
        if nk == 1:
            finish(prod)
        else:
            acc = refs[n_in + n_ji + n_out + n_jo]

            @pl.when(k == 0)
            def _():
                acc[...] = prod

            @pl.when((k > 0) & (k < nk - 1))
            def _():
                acc[...] += prod

            @pl.when(k == nk - 1)
            def _():
                finish(acc[...] + prod)

        if jobs:
            pl.when(step == total - 1)(lambda: _job_phase(jobs, "end", jin, jout, jsem))

    sem = ("arbitrary",) * 3 if jobs else ("parallel", "parallel", "arbitrary")
    res = pl.pallas_call(
        body, name=name, grid=grid,
        in_specs=[s for _, s in ins] + [_ANY] * n_ji,
        out_specs=[s for _, s in outs] + [_ANY] * n_jo,
        out_shape=[s for s, _ in outs] + j_outs,
        scratch_shapes=[pltpu.VMEM(acc_shape, F32)] * n_acc + j_sems,
        input_output_aliases=aliases,
        compiler_params=_params(sem),
    )(*[a for a, _ in ins], *j_ins)
    return res


def _mm_nn(name, a, b, out_dtype, *, a_seg=None, out_seg=None, tm=MM_TM, tn=MM_TN, tk=MM_TK, jobs=()):
    if a_seg:
        _, M, ks = a.shape
        K = ks * a_seg
    else:
        M, K = a.shape
        ks = K
    N = b.shape[1]
    ns = N // out_seg if out_seg else N
    tm, tn, tk = _tile(M, tm), _tile(ns, tn), _tile(ks, tk)
    nks, nns = ks // tk, ns // tn
    grid = (M // tm, N // tn, K // tk)
    if a_seg:
        a_spec = pl.BlockSpec((None, tm, tk), lambda i, j, k: (k // nks, i, k % nks))
    else:
        a_spec = pl.BlockSpec((tm, tk), lambda i, j, k: (i, k))
    b_spec = pl.BlockSpec((tk, tn), lambda i, j, k: (k, j))
    if out_seg:
        o_spec = pl.BlockSpec((None, tm, tn), lambda i, j, k: (j // nns, i, j % nns))
        o_shape = (out_seg, M, ns)
    else:
        o_spec = pl.BlockSpec((tm, tn), lambda i, j, k: (i, j))
        o_shape = (M, N)
    outs = [(jax.ShapeDtypeStruct(o_shape, out_dtype), o_spec)]
    return _matmul(name, [(a, a_spec), (b, b_spec)], outs, grid, DIMS_NN, (tm, tn), None, jobs)


def _mm_nt(name, a, b, out_dtype, *, a_seg=None, out_seg=None, tm=MM_TM, tn=MM_TN, tk=MM_TK, jobs=()):
    if a_seg:
        _, M, ks = a.shape
        K = ks * a_seg
    else:
        M, K = a.shape
        ks = K
    N = b.shape[0]
    ns = N // out_seg if out_seg else N
    tm, tn, tk = _tile(M, tm), _tile(ns, tn), _tile(ks, tk)
    nks, nns = ks // tk, ns // tn
    grid = (M // tm, N // tn, K // tk)
    if a_seg:
        a_spec = pl.BlockSpec((None, tm, tk), lambda i, j, k: (k // nks, i, k % nks))
    else:
        a_spec = pl.BlockSpec((tm, tk), lambda i, j, k: (i, k))
    b_spec = pl.BlockSpec((tn, tk), lambda i, j, k: (j, k))
    if out_seg:
        o_spec = pl.BlockSpec((None, tm, tn), lambda i, j, k: (j // nns, i, j % nns))
        o_shape = (out_seg, M, ns)
    else:
        o_spec = pl.BlockSpec((tm, tn), lambda i, j, k: (i, j))
        o_shape = (M, N)
    outs = [(jax.ShapeDtypeStruct(o_shape, out_dtype), o_spec)]
    return _matmul(name, [(a, a_spec), (b, b_spec)], outs, grid, DIMS_NT, (tm, tn), None, jobs)


def _mm_tn(name, a, b, out_dtype, *, a_seg=None, b_seg=None, out_blocks=None, tm=MM_TN, tn=MM_TN, tk=MM_TK_TOKENS,
           jobs=()):
    if a_seg:
        _, T, ms = a.shape
        M = ms * a_seg
    else:
        T, M = a.shape
        ms = M
    if b_seg:
        _, _, ns = b.shape
        N = ns * b_seg
    else:
        N = b.shape[1]
        ns = N
    nb_cols = N // out_blocks if out_blocks else N
    tm, tk = _tile(ms, tm), _tile(T, tk)
    tn = _tile(math.gcd(ns, nb_cols), tn)
    nms, nns, nbs = ms // tm, ns // tn, nb_cols // tn
    grid = (M // tm, N // tn, T // tk)
    if a_seg:
        a_spec = pl.BlockSpec((None, tk, tm), lambda i, j, k: (i // nms, k, i % nms))
    else:
        a_spec = pl.BlockSpec((tk, tm), lambda i, j, k: (k, i))
    if b_seg:
        b_spec = pl.BlockSpec((None, tk, tn), lambda i, j, k: (j // nns, k, j % nns))
    else:
        b_spec = pl.BlockSpec((tk, tn), lambda i, j, k: (k, j))
    if out_blocks:
        o_spec = pl.BlockSpec((None, tm, tn), lambda i, j, k: (j // nbs, i, j % nbs))
        o_shape = (out_blocks, M, nb_cols)
    else:
        o_spec = pl.BlockSpec((tm, tn), lambda i, j, k: (i, j))
        o_shape = (M, N)
    outs = [(jax.ShapeDtypeStruct(o_shape, out_dtype), o_spec)]
    return _matmul(name, [(a, a_spec), (b, b_spec)], outs, grid, DIMS_TN, (tm, tn), None, jobs)


def _rms(x):
    return lax.rsqrt(jnp.mean(x * x, axis=-1, keepdims=True) + NORM_EPS)


def _row_tile(T):
    return _tile(T, 256)


def _norm_fwd(name, x, g, jobs=()):
    T, D = x.shape
    tr = _row_tile(T)

    def body(x_ref, g_ref, h_ref):
        xv = x_ref[...]
        h_ref[...] = (xv * _rms(xv) * g_ref[...]).astype(h_ref.dtype)

    row = pl.BlockSpec((tr, D), lambda i: (i, 0))
    vec = pl.BlockSpec((1, D), lambda i: (0, 0))
    return _carry_call(body, name=name, steps=T // tr, in_specs=[row, vec], out_specs=[row],
                       out_shape=[jax.ShapeDtypeStruct((T, D), BF16)], scratch_shapes=[], args=(x, g), jobs=jobs)


def _resid_norm(name, x, br, g_post, g_next, jobs=()):
    T, D = x.shape
    tr = _row_tile(T)

    def body(x_ref, br_ref, gp_ref, gn_ref, xn_ref, h_ref):
        b = br_ref[...].astype(F32)
        xn = x_ref[...] + b * _rms(b) * gp_ref[...]
        xn_ref[...] = xn
        h_ref[...] = (xn * _rms(xn) * gn_ref[...]).astype(h_ref.dtype)

    row = pl.BlockSpec((tr, D), lambda i: (i, 0))
    vec = pl.BlockSpec((1, D), lambda i: (0, 0))
    return _carry_call(body, name=name, steps=T // tr, in_specs=[row, row, vec, vec], out_specs=[row, row],
                       out_shape=[jax.ShapeDtypeStruct((T, D), F32), jax.ShapeDtypeStruct((T, D), BF16)],
                       scratch_shapes=[], args=(x, br, g_post, g_next), jobs=jobs)


def _rms_bwd(xv, g, dyv):
    r = _rms(xv)
    xhat = xv * r
    gdy = dyv * g
    dx = r * (gdy - xhat * jnp.mean(gdy * xhat, axis=-1, keepdims=True))
    return dx, jnp.sum(dyv * xhat, axis=0, keepdims=True)


def _final_loss(name, x, br, g_post, target):
    T, D = x.shape
    tr = _row_tile(T)

    def body(x_ref, br_ref, gp_ref, t_ref, dy_ref, dbr_ref, dg_ref, ls_ref):
        b = br_ref[...].astype(F32)
        err = x_ref[...] + b * _rms(b) * gp_ref[...] - t_ref[...]
        dy = err * (1.0 / D)
        dy_ref[...] = dy
        dbr, dg = _rms_bwd(b, gp_ref[...], dy)
        dbr_ref[...] = dbr.astype(dbr_ref.dtype)

        @pl.when(pl.program_id(0) == 0)
        def _():
            ls_ref[...] = jnp.zeros_like(ls_ref)
            dg_ref[...] = jnp.zeros_like(dg_ref)

        ls_ref[...] += jnp.sum(err * err)
        dg_ref[...] += dg

    row = pl.BlockSpec((tr, D), lambda i: (i, 0))
    vec = pl.BlockSpec((1, D), lambda i: (0, 0))
    acc = pl.BlockSpec((SUBLANES, LANES), lambda i: (0, 0))
    return pl.pallas_call(body, name=name, grid=(T // tr,), in_specs=[row, row, vec, row],
                          out_specs=[row, row, vec, acc],
                          out_shape=[jax.ShapeDtypeStruct((T, D), F32), jax.ShapeDtypeStruct((T, D), BF16),
                                     jax.ShapeDtypeStruct((1, D), F32), jax.ShapeDtypeStruct((SUBLANES, LANES), F32)],
                          compiler_params=_params(("arbitrary",)))(x, br, g_post, target)


def _norm_bwd_pair(name, x, g_pre, dh, add, br, g_post):
    T, D = x.shape
    tr = _row_tile(T)

    def body(x_ref, gpre_ref, dh_ref, add_ref, br_ref, gpost_ref, dx_ref, dbr_ref, dgpre_ref, dgpost_ref):
        dx, dg_pre = _rms_bwd(x_ref[...], gpre_ref[...], dh_ref[...].astype(F32))
        dx = dx + add_ref[...]
        dx_ref[...] = dx
        dbr, dg_post = _rms_bwd(br_ref[...].astype(F32), gpost_ref[...], dx)
        dbr_ref[...] = dbr.astype(dbr_ref.dtype)

        @pl.when(pl.program_id(0) == 0)
        def _():
            dgpre_ref[...] = jnp.zeros_like(dgpre_ref)
            dgpost_ref[...] = jnp.zeros_like(dgpost_ref)

        dgpre_ref[...] += dg_pre
        dgpost_ref[...] += dg_post

    row = pl.BlockSpec((tr, D), lambda i: (i, 0))
    vec = pl.BlockSpec((1, D), lambda i: (0, 0))
    return pl.pallas_call(body, name=name, grid=(T // tr,), in_specs=[row, vec, row, row, row, vec],
                          out_specs=[row, row, vec, vec],
                          out_shape=[jax.ShapeDtypeStruct((T, D), F32), jax.ShapeDtypeStruct((T, D), BF16),
                                     jax.ShapeDtypeStruct((1, D), F32), jax.ShapeDtypeStruct((1, D), F32)],
                          compiler_params=_params(("arbitrary",)))(x, g_pre, dh, add, br, g_post)


def _norm_bwd(name, x, g, dy, add, jobs=()):
    T, D = x.shape
    tr = _row_tile(T)

    def body(x_ref, g_ref, dy_ref, add_ref, dx_ref, dg_ref):
        dx, dg = _rms_bwd(x_ref[...], g_ref[...], dy_ref[...].astype(F32))
        dx_ref[...] = dx + add_ref[...]

        @pl.when(pl.program_id(0) == 0)
        def _():
            dg_ref[...] = jnp.zeros_like(dg_ref)

        dg_ref[...] += dg

    row = pl.BlockSpec((tr, D), lambda i: (i, 0))
    vec = pl.BlockSpec((1, D), lambda i: (0, 0))
    return _carry_call(body, name=name, steps=T // tr, in_specs=[row, vec, row, row], out_specs=[row, vec],
                       out_shape=[jax.ShapeDtypeStruct((T, D), F32), jax.ShapeDtypeStruct((1, D), F32)],
                       scratch_shapes=[], args=(x, g, dy, add), jobs=jobs, sequential=True)


HALO = SUBLANES
TIME_CHUNK = 512


def _chunks(T):
    tc = min(TIME_CHUNK, T)
    assert T % tc == 0 and tc % SUBLANES == 0
    return [(t0, tc) for t0 in range(0, T, tc)]


def _log_sigmoid(x):
    return -(jnp.maximum(-x, 0.0) + jnp.log(1.0 + jnp.exp(-jnp.abs(x))))


def _sigmoid(x):
    return 0.5 * jnp.tanh(0.5 * x) + 0.5


def _one_minus_exp(x):
    series = -x * (1.0 + x * (0.5 + x * (1.0 / 6.0 + x * (1.0 / 24.0))))
    return jnp.where(x > -0.01, series, 1.0 - jnp.exp(x))


_GELU_C = math.sqrt(2.0 / math.pi)


def _gelu(x):
    return 0.5 * x * (1.0 + jnp.tanh(_GELU_C * (x + 0.044715 * x * x * x)))


def _gelu_grad(x):
    th = jnp.tanh(_GELU_C * (x + 0.044715 * x * x * x))
    return 0.5 * (1.0 + th) + 0.5 * x * (1.0 - th * th) * _GELU_C * (1.0 + 3.0 * 0.044715 * x * x)


def _tile_scan(a, b, reverse):
    rows = a.shape[0]
    pos = lax.broadcasted_iota(jnp.int32, a.shape, 0) & (SUBLANES - 1)
    for d in (1, 2, 4):
        if reverse:
            ok = pos < SUBLANES - d
            shift = rows - d
        else:
            ok = pos >= d
            shift = d
        a_sh = jnp.where(ok, pltpu.roll(a, shift, 0), 1.0)
        b_sh = jnp.where(ok, pltpu.roll(b, shift, 0), 0.0)
        b = a * b_sh + b
        a = a * a_sh
    return a, b


def _carry_scan(a_s, b_s, T, reverse):
    n = T // SUBLANES
    edge = 0 if reverse else SUBLANES - 1

    def step(j, carry):
        g = (n - 1 - j) if reverse else j
        r = pl.multiple_of(g * SUBLANES, SUBLANES)
        h = b_s[pl.ds(r, SUBLANES), :] + a_s[pl.ds(r, SUBLANES), :] * carry
        b_s[pl.ds(r, SUBLANES), :] = h
        return jnp.broadcast_to(h[edge:edge + 1, :], h.shape)

    lax.fori_loop(0, n, step, jnp.zeros((SUBLANES, a_s.shape[1]), F32))


def _seg_spec(T, seg, nblk):
    return pl.BlockSpec((None, T, LANES), lambda c: (seg, 0, c))


def _rows_to_tile(rows):
    idx = lax.broadcasted_iota(jnp.int32, (SUBLANES, LANES), 0)
    out = jnp.zeros((SUBLANES, LANES), F32)
    for k, r in enumerate(rows):
        out = jnp.where(idx == k, r, out)
    return out


def _mixer_a_fwd(proj, conv_a):
    _, T, C = proj.shape
    nblk = C // LANES
    chunks = _chunks(T)

    def body(bg_ref, cg_ref, ax_ref, w_ref, y_ref, p_s):
        p_s[pl.ds(0, HALO), :] = jnp.zeros((HALO, LANES), F32)
        for t0, tc in chunks:
            p_s[pl.ds(HALO + t0, tc), :] = cg_ref[pl.ds(t0, tc), :] * ax_ref[pl.ds(t0, tc), :]
        w = w_ref[...]
        for t0, tc in chunks:
            c = (w[2:3, :] * p_s[pl.ds(HALO + t0, tc), :] + w[1:2, :] * p_s[pl.ds(HALO + t0 - 1, tc), :]
                 + w[0:1, :] * p_s[pl.ds(HALO + t0 - 2, tc), :])
            y_ref[pl.ds(t0, tc), :] = (bg_ref[pl.ds(t0, tc), :] * c).astype(y_ref.dtype)

    return pl.pallas_call(
        body, name="mixer_a_fwd", grid=(nblk,),
        in_specs=[_seg_spec(T, 0, nblk), _seg_spec(T, 1, nblk), _seg_spec(T, 2, nblk),
                  pl.BlockSpec((3, LANES), lambda c: (0, c))],
        out_specs=_seg_spec(T, 0, nblk),
        out_shape=jax.ShapeDtypeStruct((2, T, C), BF16),
        scratch_shapes=[pltpu.VMEM((T + HALO, LANES), F32)],
        compiler_params=_params(("parallel",)))(proj, proj, proj, conv_a)


def _mixer_a_bwd(proj, conv_a, dy):
    _, T, C = proj.shape
    nblk = C // LANES
    chunks = _chunks(T)

    def body(bg_ref, cg_ref, ax_ref, w_ref, dy_ref, dp_ref, dw_ref, p_s, dc_s):
        p_s[pl.ds(0, HALO), :] = jnp.zeros((HALO, LANES), F32)
        dc_s[pl.ds(T, HALO), :] = jnp.zeros((HALO, LANES), F32)
        for t0, tc in chunks:
            p_s[pl.ds(HALO + t0, tc), :] = cg_ref[pl.ds(t0, tc), :] * ax_ref[pl.ds(t0, tc), :]
        w = w_ref[...]
        for t0, tc in chunks:
            c = (w[2:3, :] * p_s[pl.ds(HALO + t0, tc), :] + w[1:2, :] * p_s[pl.ds(HALO + t0 - 1, tc), :]
                 + w[0:1, :] * p_s[pl.ds(HALO + t0 - 2, tc), :])
            dyv = dy_ref[pl.ds(t0, tc), :]
            dp_ref[0, pl.ds(t0, tc), :] = (dyv * c).astype(dp_ref.dtype)
            dc_s[pl.ds(t0, tc), :] = dyv * bg_ref[pl.ds(t0, tc), :]
        dw = [jnp.zeros((1, LANES), F32) for _ in range(3)]
        for t0, tc in chunks:
            dc = dc_s[pl.ds(t0, tc), :]
            dpv = w[2:3, :] * dc + w[1:2, :] * dc_s[pl.ds(t0 + 1, tc), :] + w[0:1, :] * dc_s[pl.ds(t0 + 2, tc), :]
            dp_ref[1, pl.ds(t0, tc), :] = (dpv * ax_ref[pl.ds(t0, tc), :]).astype(dp_ref.dtype)
            dp_ref[2, pl.ds(t0, tc), :] = (dpv * cg_ref[pl.ds(t0, tc), :]).astype(dp_ref.dtype)
            for k in range(3):
                dw[k] = dw[k] + jnp.sum(dc * p_s[pl.ds(HALO + t0 - (2 - k), tc), :], axis=0, keepdims=True)
        dw_ref[...] = _rows_to_tile(dw)

    return pl.pallas_call(
        body, name="mixer_a_bwd", grid=(nblk,),
        in_specs=[_seg_spec(T, 0, nblk), _seg_spec(T, 1, nblk), _seg_spec(T, 2, nblk),
                  pl.BlockSpec((3, LANES), lambda c: (0, c)), _seg_spec(T, 0, nblk)],
        out_specs=[pl.BlockSpec((3, T, LANES), lambda c: (0, 0, c)),
                   pl.BlockSpec((None, SUBLANES, LANES), lambda c: (c, 0, 0))],
        out_shape=[jax.ShapeDtypeStruct((6, T, C), BF16), jax.ShapeDtypeStruct((nblk, SUBLANES, LANES), F32)],
        scratch_shapes=[pltpu.VMEM((T + HALO, LANES), F32), pltpu.VMEM((T + HALO, LANES), F32)],
        compiler_params=_params(("parallel",)))(proj, proj, proj, conv_a, dy)


def _rg_gates(xr, wa, ba, wx, bx, ls):
    xb = xr.astype(BF16)
    r = _sigmoid(jnp.dot(xb, wa, preferred_element_type=F32) + ba)
    i = _sigmoid(jnp.dot(xb, wx, preferred_element_type=F32) + bx)
    log_a = LRU_C * r * ls
    a = jnp.exp(log_a)
    mult = jnp.sqrt(_one_minus_exp(2.0 * log_a))
    return r, i, a, mult


def _conv4(xh_s, cw, bias, t0, tc):
    return (cw[3:4, :] * xh_s[pl.ds(HALO + t0, tc), :] + cw[2:3, :] * xh_s[pl.ds(HALO + t0 - 1, tc), :]
            + cw[1:2, :] * xh_s[pl.ds(HALO + t0 - 2, tc), :] + cw[0:1, :] * xh_s[pl.ds(HALO + t0 - 3, tc), :] + bias)


def _mixer_b_specs(T, nblk):
    vec = pl.BlockSpec((1, LANES), lambda c: (0, c))
    mat = pl.BlockSpec((None, LANES, LANES), lambda c: (c, 0, 0))
    return [_seg_spec(T, 3, nblk), _seg_spec(T, 4, nblk), pl.BlockSpec((4, LANES), lambda c: (0, c)),
            vec, mat, vec, mat, vec, vec]


def _mixer_b_fwd(name, proj, conv_b, bias, wa, ba, wx, bx, lam, y, jobs=()):
    _, T, C = proj.shape
    nblk = C // LANES
    chunks = _chunks(T)

    def body(gate_ref, x_ref, cw_ref, cb_ref, wa_ref, ba_ref, wx_ref, bx_ref, lam_ref, y_in, y_ref, xh_s, a_s, b_s):
        xh_s[pl.ds(0, HALO), :] = jnp.zeros((HALO, LANES), F32)
        for t0, tc in chunks:
            xh_s[pl.ds(HALO + t0, tc), :] = x_ref[pl.ds(t0, tc), :]
        cw, bias_v = cw_ref[...], cb_ref[...]
        ls = _log_sigmoid(lam_ref[...])
        for t0, tc in chunks:
            xr = _conv4(xh_s, cw, bias_v, t0, tc)
            r, i, a, mult = _rg_gates(xr, wa_ref[...], ba_ref[...], wx_ref[...], bx_ref[...], ls)
            ac, hc = _tile_scan(a, mult * i * xr, reverse=False)
            a_s[pl.ds(t0, tc), :] = ac
            b_s[pl.ds(t0, tc), :] = hc
        _carry_scan(a_s, b_s, T, reverse=False)
        for t0, tc in chunks:
            y_ref[pl.ds(t0, tc), :] = (b_s[pl.ds(t0, tc), :] * _gelu(gate_ref[pl.ds(t0, tc), :])).astype(y_ref.dtype)

    return _carry_call(
        body, name=name, steps=nblk, in_specs=_mixer_b_specs(T, nblk) + [_ANY],
        out_specs=[_seg_spec(T, 1, nblk)], out_shape=[jax.ShapeDtypeStruct(y.shape, y.dtype)],
        scratch_shapes=[pltpu.VMEM((T + HALO, LANES), F32), pltpu.VMEM((T, LANES), F32), pltpu.VMEM((T, LANES), F32)],
        args=(proj, proj, conv_b, bias, wa, ba, wx, bx, lam, y), jobs=jobs, aliases={9: 0})


_ROW_CONV, _ROW_BIAS, _ROW_BA, _ROW_BX, _ROW_LAM = 0, 4, 5, 6, 7


def _mixer_b_bwd(name, proj, conv_b, bias, wa, ba, wx, bx, lam, dy, dproj, jobs=()):
    _, T, C = proj.shape
    nblk = C // LANES
    chunks = _chunks(T)

    def body(gate_ref, x_ref, cw_ref, cb_ref, wa_ref, ba_ref, wx_ref, bx_ref, lam_ref, dy_ref, dp_in,
             dp_ref, sm_ref, dwa_ref, dwx_ref, xh_s, xr_s, r_s, i_s, a_s, h_s, sa_s, sb_s, dx_s):
        zero_halo = jnp.zeros((HALO, LANES), F32)
        xh_s[pl.ds(0, HALO), :] = zero_halo
        h_s[pl.ds(0, HALO), :] = zero_halo
        a_s[pl.ds(T, HALO), :] = zero_halo
        dx_s[pl.ds(T, HALO), :] = zero_halo
        for t0, tc in chunks:
            xh_s[pl.ds(HALO + t0, tc), :] = x_ref[pl.ds(t0, tc), :]
        cw, bias_v = cw_ref[...], cb_ref[...]
        lam_v = lam_ref[...]
        ls = _log_sigmoid(lam_v)
        wa_v, wx_v, ba_v, bx_v = wa_ref[...], wx_ref[...], ba_ref[...], bx_ref[...]
        for t0, tc in chunks:
            xr = _conv4(xh_s, cw, bias_v, t0, tc)
            r, i, a, mult = _rg_gates(xr, wa_v, ba_v, wx_v, bx_v, ls)
            xr_s[pl.ds(t0, tc), :] = xr
            r_s[pl.ds(t0, tc), :] = r
            i_s[pl.ds(t0, tc), :] = i
            a_s[pl.ds(t0, tc), :] = a
            ac, hc = _tile_scan(a, mult * i * xr, reverse=False)
            sa_s[pl.ds(t0, tc), :] = ac
            sb_s[pl.ds(t0, tc), :] = hc
        _carry_scan(sa_s, sb_s, T, reverse=False)
        for t0, tc in chunks:
            h_s[pl.ds(HALO + t0, tc), :] = sb_s[pl.ds(t0, tc), :]
        for t0, tc in chunks:
            gv = gate_ref[pl.ds(t0, tc), :]
            dyv = dy_ref[pl.ds(t0, tc), :]
            dp_ref[0, pl.ds(t0, tc), :] = (dyv * h_s[pl.ds(HALO + t0, tc), :] * _gelu_grad(gv)).astype(dp_ref.dtype)
            ac, gc = _tile_scan(a_s[pl.ds(t0 + 1, tc), :], dyv * _gelu(gv), reverse=True)
            sa_s[pl.ds(t0, tc), :] = ac
            sb_s[pl.ds(t0, tc), :] = gc
        _carry_scan(sa_s, sb_s, T, reverse=True)
        acc = {k: jnp.zeros((1, LANES), F32) for k in ("bias", "ba", "bx", "lam")}
        dwa = jnp.zeros((LANES, LANES), F32)
        dwx = jnp.zeros((LANES, LANES), F32)
        for t0, tc in chunks:
            dht = sb_s[pl.ds(t0, tc), :]
            xr, r, i, a = xr_s[pl.ds(t0, tc), :], r_s[pl.ds(t0, tc), :], i_s[pl.ds(t0, tc), :], a_s[pl.ds(t0, tc), :]
            mult = jnp.sqrt(_one_minus_exp(2.0 * LRU_C * r * ls))
            da = dht * h_s[pl.ds(HALO + t0 - 1, tc), :]
            dmult = dht * i * xr
            di = dht * mult * xr
            dlog_a = da * a - dmult * a * a / mult
            dpa = dlog_a * (LRU_C * ls) * r * (1.0 - r)
            dpx = di * i * (1.0 - i)
            acc["lam"] = acc["lam"] + jnp.sum(dlog_a * r, axis=0, keepdims=True)
            acc["ba"] = acc["ba"] + jnp.sum(dpa, axis=0, keepdims=True)
            acc["bx"] = acc["bx"] + jnp.sum(dpx, axis=0, keepdims=True)
            xb, dpab, dpxb = xr.astype(BF16), dpa.astype(BF16), dpx.astype(BF16)
            dwa = dwa + lax.dot_general(xb, dpab, DIMS_TN, preferred_element_type=F32)
            dwx = dwx + lax.dot_general(xb, dpxb, DIMS_TN, preferred_element_type=F32)
            dxr = (dht * mult * i + lax.dot_general(dpab, wa_v, DIMS_NT, preferred_element_type=F32)
                   + lax.dot_general(dpxb, wx_v, DIMS_NT, preferred_element_type=F32))
            acc["bias"] = acc["bias"] + jnp.sum(dxr, axis=0, keepdims=True)
            dx_s[pl.ds(t0, tc), :] = dxr
        dcw = [jnp.zeros((1, LANES), F32) for _ in range(4)]
        for t0, tc in chunks:
            dxr = dx_s[pl.ds(t0, tc), :]
            dxin = (cw[3:4, :] * dxr + cw[2:3, :] * dx_s[pl.ds(t0 + 1, tc), :] + cw[1:2, :] * dx_s[pl.ds(t0 + 2, tc), :]
                    + cw[0:1, :] * dx_s[pl.ds(t0 + 3, tc), :])
            dp_ref[1, pl.ds(t0, tc), :] = dxin.astype(dp_ref.dtype)
            for k in range(4):
                dcw[k] = dcw[k] + jnp.sum(dxr * xh_s[pl.ds(HALO + t0 - (3 - k), tc), :], axis=0, keepdims=True)
        dlam = acc["lam"] * LRU_C * _sigmoid(-lam_v)
        sm_ref[...] = _rows_to_tile(dcw + [acc["bias"], acc["ba"], acc["bx"], dlam])
        dwa_ref[...] = dwa
        dwx_ref[...] = dwx

    big = lambda halo: pltpu.VMEM((T + halo, LANES), F32)
    mat = pl.BlockSpec((None, LANES, LANES), lambda c: (c, 0, 0))
    return _carry_call(
        body, name=name, steps=nblk,
        in_specs=_mixer_b_specs(T, nblk) + [_seg_spec(T, 1, nblk), _ANY],
        out_specs=[pl.BlockSpec((3, T, LANES), lambda c: (1, 0, c)),
                   pl.BlockSpec((None, SUBLANES, LANES), lambda c: (c, 0, 0)), mat, mat],
        out_shape=[jax.ShapeDtypeStruct(dproj.shape, dproj.dtype), jax.ShapeDtypeStruct((nblk, SUBLANES, LANES), F32),
                   jax.ShapeDtypeStruct((nblk, LANES, LANES), F32), jax.ShapeDtypeStruct((nblk, LANES, LANES), F32)],
        scratch_shapes=[big(HALO), big(0), big(0), big(0), big(HALO), big(HALO), big(0), big(0), big(HALO)],
        args=(proj, proj, conv_b, bias, wa, ba, wx, bx, lam, dy, dproj), jobs=jobs, aliases={10: 0})


ATT_BLOCK = 128
ATT_GROUP = 3
ATT_TILE = ATT_BLOCK * ATT_GROUP
ATT_UNDERFLOW = -110.0
ATT_UNVISITED = -1e30


def _split_dot(x, m):
    hi = x.astype(BF16)
    lo = (x - hi.astype(F32)).astype(BF16)
    return jnp.dot(hi, m, preferred_element_type=F32) + jnp.dot(lo, m, preferred_element_type=F32)


def _sub(x, j):
    return x[:, j * ATT_BLOCK:(j + 1) * ATT_BLOCK]


def _stack_rows(x):
    return jnp.concatenate([_sub(x, j) for j in range(ATT_GROUP)], axis=0)


def _unstack_rows(x, offsets):
    return jnp.concatenate([x[j * ATT_BLOCK:(j + 1) * ATT_BLOCK, :] + offsets[j] for j in range(ATT_GROUP)], axis=1)


def _att_tile(q, k_ref, q0, qb, it, scale):
    hi = (qb + 1 - ATT_GROUP * it) * ATT_BLOCK
    k0 = pl.multiple_of(jnp.maximum(hi - ATT_TILE, 0), ATT_BLOCK)
    kt = k_ref[pl.ds(k0, ATT_TILE), :]
    z = lax.dot_general(q, kt, DIMS_NT, preferred_element_type=F32) * scale
    key = k0 + lax.broadcasted_iota(jnp.int32, z.shape, 1)
    row = q0 + lax.broadcasted_iota(jnp.int32, z.shape, 0)
    mask = (key < row) & (key < hi)
    n = jnp.where(mask, -(jnp.maximum(z, 0.0) + jnp.log(1.0 + jnp.exp(-jnp.abs(z)))), 0.0)
    return k0, kt, z, mask, n


def _suffix_in_tile(n, upper, run):
    rs = [jnp.sum(_sub(n, j), axis=1, keepdims=True) for j in range(ATT_GROUP)]
    offs = [None] * ATT_GROUP
    offs[ATT_GROUP - 1] = run
    for j in range(ATT_GROUP - 2, -1, -1):
        offs[j] = offs[j + 1] + rs[j + 1]
    return _unstack_rows(_split_dot(_stack_rows(n), upper), offs), offs[0] + rs[0]


def _head_spec(T, seg, heads):
    return pl.BlockSpec((None, T, ATT_HEAD_DIM), lambda h: (seg, 0, h))


def _attention_fwd(name, qkv, jobs=()):
    _, T, D = qkv.shape
    heads = D // ATT_HEAD_DIM
    nq = T // ATT_BLOCK
    assert nq <= LANES and T >= ATT_TILE and nq % 2 == 0
    scale = 1.0 / math.sqrt(ATT_HEAD_DIM)

    def body(q_ref, k_ref, v_ref, o_ref, r_ref, acc_s, run_s):
        rr = lax.broadcasted_iota(jnp.int32, (ATT_BLOCK, ATT_BLOCK), 0)
        cc = lax.broadcasted_iota(jnp.int32, (ATT_BLOCK, ATT_BLOCK), 1)
        upper = jnp.where(rr > cc, 1.0, 0.0).astype(BF16)
        lane = lax.broadcasted_iota(jnp.int32, (ATT_BLOCK, LANES), 1)

        def tile(slot, qb, q0, q, it, first):
            k0, _, z, mask, n = _att_tile(q, k_ref, q0, qb, it, scale)
            run = jnp.zeros((ATT_BLOCK, LANES), F32) if first else run_s[slot]
            suffix, run_next = _suffix_in_tile(n, upper, run)
            w = jnp.where(mask, jnp.exp(z + n + suffix), 0.0)
            pv = jnp.dot(w.astype(BF16), v_ref[pl.ds(k0, ATT_TILE), :], preferred_element_type=F32)
            if first:
                acc_s[slot] = pv
            else:
                acc_s[slot] += pv
                r_ref[pl.ds(q0, ATT_BLOCK), :] = jnp.where(lane == it, run, r_ref[pl.ds(q0, ATT_BLOCK), :])
            run_s[slot] = run_next
            return jnp.max(run_next) >= ATT_UNDERFLOW

        def pair_loop(p, _):
            blocks = []
            for slot in range(2):
                qb = 2 * p + slot
                q0 = pl.multiple_of(qb * ATT_BLOCK, ATT_BLOCK)
                r_ref[pl.ds(q0, ATT_BLOCK), :] = jnp.where(lane == 0, 0.0, ATT_UNVISITED)
                blocks.append((qb, q0, q_ref[pl.ds(q0, ATT_BLOCK), :]))
            go = [tile(slot, *blocks[slot], 0, True) for slot in range(2)]
            for slot in range(2):
                qb, q0, q = blocks[slot]
                n_tiles = (qb + ATT_GROUP) // ATT_GROUP
                lax.while_loop(lambda c: (c[0] < n_tiles) & c[1],
                               lambda c: (c[0] + 1, tile(slot, qb, q0, q, c[0], False)), (jnp.int32(1), go[slot]))
                o_ref[pl.ds(q0, ATT_BLOCK), :] = acc_s[slot].astype(o_ref.dtype)
            return 0

        lax.fori_loop(0, nq // 2, pair_loop, 0)

    return _carry_call(
        body, name=name, steps=heads,
        in_specs=[_head_spec(T, 0, heads), _head_spec(T, 1, heads), _head_spec(T, 2, heads)],
        out_specs=[pl.BlockSpec((T, ATT_HEAD_DIM), lambda h: (0, h)), pl.BlockSpec((None, T, LANES), lambda h: (h, 0, 0))],
        out_shape=[jax.ShapeDtypeStruct((T, D), BF16), jax.ShapeDtypeStruct((heads, T, LANES), F32)],
        scratch_shapes=[pltpu.VMEM((2, ATT_BLOCK, ATT_HEAD_DIM), F32), pltpu.VMEM((2, ATT_BLOCK, LANES), F32)],
        args=(qkv, qkv, qkv), jobs=jobs)


def _attention_bwd(name, qkv, do, rmat, jobs=()):
    _, T, D = qkv.shape
    heads = D // ATT_HEAD_DIM
    nq = T // ATT_BLOCK
    scale = 1.0 / math.sqrt(ATT_HEAD_DIM)

    def body(q_ref, k_ref, v_ref, do_ref, r_ref, dqkv_ref, dk_s, dv_s, dq_s, left_s):
        rr = lax.broadcasted_iota(jnp.int32, (ATT_BLOCK, ATT_BLOCK), 0)
        cc = lax.broadcasted_iota(jnp.int32, (ATT_BLOCK, ATT_BLOCK), 1)
        upper = jnp.where(rr > cc, 1.0, 0.0).astype(BF16)
        lower = jnp.where(rr < cc, 1.0, 0.0).astype(BF16)
        lane = lax.broadcasted_iota(jnp.int32, (ATT_BLOCK, LANES), 1)
        dk_s[...] = jnp.zeros_like(dk_s)
        dv_s[...] = jnp.zeros_like(dv_s)

        def tile(slot, qb, q0, q, dov, it, first):
            k0, kt, z, mask, n = _att_tile(q, k_ref, q0, qb, it, scale)
            vt = v_ref[pl.ds(k0, ATT_TILE), :]
            run = jnp.sum(jnp.where(lane == it, r_ref[pl.ds(q0, ATT_BLOCK), :], 0.0), axis=1, keepdims=True)
            suffix, _ = _suffix_in_tile(n, upper, run)
            s = z + n
            w = jnp.where(mask, jnp.exp(s + suffix), 0.0)
            e = w * lax.dot_general(dov, vt, DIMS_NT, preferred_element_type=F32)
            es = [jnp.sum(_sub(e, g), axis=1, keepdims=True) for g in range(ATT_GROUP)]
            pre = [jnp.zeros((ATT_BLOCK, LANES), F32) if first else left_s[slot]]
            for g in range(ATT_GROUP):
                pre.append(pre[g] + es[g])
            before = _unstack_rows(_split_dot(_stack_rows(e), lower), pre)
            sig = jnp.exp(s)
            dz = (jnp.where(mask, e * (1.0 - sig) - before * sig, 0.0) * scale).astype(BF16)
            dq = jnp.dot(dz, kt, preferred_element_type=F32)
            if first:
                dq_s[slot] = dq
            else:
                dq_s[slot] += dq
            dk_s[pl.ds(k0, ATT_TILE), :] += lax.dot_general(dz, q, DIMS_TN, preferred_element_type=F32)
            dv_s[pl.ds(k0, ATT_TILE), :] += lax.dot_general(w.astype(BF16), dov, DIMS_TN, preferred_element_type=F32)
            left_s[slot] = pre[ATT_GROUP]

        def pair_loop(p, _):
            blocks = []
            for slot in range(2):
                qb = 2 * p + slot
                q0 = pl.multiple_of(qb * ATT_BLOCK, ATT_BLOCK)
                n_tiles = (qb + ATT_GROUP) // ATT_GROUP
                seen = ((jnp.max(r_ref[pl.ds(q0, ATT_BLOCK), :], axis=0, keepdims=True) > 0.5 * ATT_UNVISITED)
                        & (lane[0:1, :] < n_tiles))
                n_visited = jnp.maximum(jnp.sum(jnp.where(seen, 1.0, 0.0)).astype(jnp.int32), 1)
                blocks.append((qb, q0, q_ref[pl.ds(q0, ATT_BLOCK), :], do_ref[pl.ds(q0, ATT_BLOCK), :], n_visited))
            for slot in range(2):
                qb, q0, q, dov, n_visited = blocks[slot]
                tile(slot, qb, q0, q, dov, n_visited - 1, True)
            for slot in range(2):
                qb, q0, q, dov, n_visited = blocks[slot]
                lax.fori_loop(1, n_visited, lambda j, c: (tile(slot, qb, q0, q, dov, n_visited - 1 - j, False), c)[1], 0)
                dqkv_ref[0, pl.ds(q0, ATT_BLOCK), :] = dq_s[slot].astype(dqkv_ref.dtype)
            return 0

        lax.fori_loop(0, nq // 2, pair_loop, 0)
        dqkv_ref[1, :, :] = dk_s[...].astype(dqkv_ref.dtype)
        dqkv_ref[2, :, :] = dv_s[...].astype(dqkv_ref.dtype)

    return _carry_call(
        body, name=name, steps=heads,
        in_specs=[_head_spec(T, 0, heads), _head_spec(T, 1, heads), _head_spec(T, 2, heads),
                  pl.BlockSpec((T, ATT_HEAD_DIM), lambda h: (0, h)), pl.BlockSpec((None, T, LANES), lambda h: (h, 0, 0))],
        out_specs=[pl.BlockSpec((3, T, ATT_HEAD_DIM), lambda h: (0, 0, h))],
        out_shape=[jax.ShapeDtypeStruct((3, T, D), BF16)],
        scratch_shapes=[pltpu.VMEM((T, ATT_HEAD_DIM), F32), pltpu.VMEM((T, ATT_HEAD_DIM), F32),
                        pltpu.VMEM((2, ATT_BLOCK, ATT_HEAD_DIM), F32), pltpu.VMEM((2, ATT_BLOCK, LANES), F32)],
        args=(qkv, qkv, qkv, do, rmat), jobs=jobs)


def _block_diag_pairs(w):
    h = w.shape[0]
    wp = w.reshape(h // 2, 2, RG_HEAD_DIM, RG_HEAD_DIM)
    z = jnp.zeros_like(wp[:, 0])
    top = jnp.concatenate([wp[:, 0], z], axis=2)
    bot = jnp.concatenate([z, wp[:, 1]], axis=2)
    return jnp.concatenate([top, bot], axis=1)


def _diag_pairs(g):
    n = g.shape[0]
    a = g[:, :RG_HEAD_DIM, :RG_HEAD_DIM]
    b = g[:, RG_HEAD_DIM:, RG_HEAD_DIM:]
    return jnp.stack([a, b], axis=1).reshape(2 * n, RG_HEAD_DIM, RG_HEAD_DIM)


class _Weights:
    def __init__(self, full, shards=None, plan=None):
        self.full, self.shards, self.plan = dict(full), shards or {}, plan or {}
        self.partial, self.rows = {}, {}

    def __getitem__(self, name):
        return self.full[name]

    def jobs(self, call):
        return [_gather_job(self.shards[n], self.partial.get(n), lo, hi, parts)
                for n, lo, hi, parts in self.plan.get(call, ())]

    def deliver(self, call, outs):
        for (n, lo, hi, parts), g in zip(self.plan.get(call, ()), outs):
            self.partial[n] = g
            self.rows[n] = self.rows.get(n, 0) + hi - lo
            if self.rows[n] == parts:
                self.full[n] = _gathered_layout(n, g)


def _gathered_layout(name, g):
    if name in ("w_in", "w_qkv", "w_out", "w_o"):
        return g.reshape(g.shape[0] * g.shape[1], g.shape[2])
    return g


class _Grads:
    def __init__(self, lands=None, plan=None):
        self.lands, self.plan = dict(lands) if lands else None, plan or {}
        self.ready, self.sent = {}, {}

    def put(self, name, arr):
        self.ready[name] = arr

    def jobs(self, call):
        if self.lands is None:
            return []
        return [_exchange_job(self.ready[n], self.lands[n], lo, hi, parts) for n, lo, hi, parts in self.plan.get(call, ())]

    def deliver(self, call, outs):
        for (n, lo, hi, parts), o in zip(self.plan.get(call, ()), outs):
            assert self.sent.get(n, (0, parts)) == (lo, parts), (call, n)
            self.lands[n] = o
            self.sent[n] = (hi, parts)

    def flush(self, name):
        if self.lands is None:
            return
        rest = []
        for n in self.ready:
            lo, parts = self.sent.get(n, (0, 1))
            if lo < parts:
                rest.append((n, lo, parts, parts))
        if rest:
            outs = _run_jobs(name, [_exchange_job(self.ready[n], self.lands[n], lo, hi, parts) for n, lo, hi, parts in rest])
            for (n, _, hi, parts), o in zip(rest, outs):
                self.lands[n] = o
                self.sent[n] = (hi, parts)


def _mlp_fwd(tag, h, wts, run):
    T, D = h.shape
    w_up = wts["up" + tag]
    fb = w_up.shape[2]
    F = fb * N_DEV
    tm, tn, tk = _tile(T, MM_TM), _tile(fb, MM_TN), _tile(D, MM_TK)
    nb = fb // tn

    def up_epilogue(u):
        r = jnp.maximum(u, 0.0)
        return u, r * r

    o_spec = pl.BlockSpec((tm, tn), lambda i, j, k: (i, j))
    u, act = run(
        _matmul, f"mlp_up_l{tag}",
        [(h, pl.BlockSpec((tm, tk), lambda i, j, k: (i, k))),
         (w_up, pl.BlockSpec((None, tk, tn), lambda i, j, k: (j // nb, k, j % nb)))],
        [(jax.ShapeDtypeStruct((T, F), BF16), o_spec), (jax.ShapeDtypeStruct((T, F), BF16), o_spec)],
        (T // tm, F // tn, D // tk), DIMS_NN, (tm, tn), up_epilogue, n_main=2)
    w_down = wts["down" + tag].reshape(F, D)
    m = run(_mm_nn, f"mlp_down_l{tag}", act, w_down, BF16)
    return u, act, m


def _mlp_bwd(tag, h, u, act, dm, wts, grads, run):
    T, D = h.shape
    w_up, w_down = wts["up" + tag], wts["down" + tag]
    fb = w_up.shape[2]
    F = fb * N_DEV
    grads.put("down" + tag, run(_mm_tn, f"mlp_down_dw_l{tag}", act, dm, BF16).reshape(N_DEV, fb, D))
    tm, tn, tk = _tile(T, MM_TM), _tile(fb, MM_TN), _tile(D, MM_TK)
    nb = fb // tn
    o_spec = pl.BlockSpec((tm, tn), lambda i, j, k: (i, j))
    du = run(
        _matmul, f"mlp_down_dx_l{tag}",
        [(dm, pl.BlockSpec((tm, tk), lambda i, j, k: (i, k))),
         (w_down, pl.BlockSpec((None, tn, tk), lambda i, j, k: (j // nb, j % nb, k))),
         (u, o_spec)],
        [(jax.ShapeDtypeStruct((T, F), BF16), o_spec)],
        (T // tm, F // tn, D // tk), DIMS_NT, (tm, tn),
        lambda r, uv: (r * (2.0 * jnp.maximum(uv.astype(F32), 0.0)),))
    grads.put("up" + tag, run(_mm_tn, f"mlp_up_dw_l{tag}", h, du, BF16, out_blocks=N_DEV))
    tn2 = _tile(D, MM_TN)
    pair = 2 if MM_TK >= 2 * fb else 1
    return run(
        _matmul, f"mlp_up_dx_l{tag}",
        [(du, pl.BlockSpec((tm, pair * fb), lambda i, j, k: (i, k))),
         (w_up, pl.BlockSpec((pair, tn2, fb), lambda i, j, k: (k, j, 0)))],
        [(jax.ShapeDtypeStruct((T, D), BF16), pl.BlockSpec((tm, tn2), lambda i, j, k: (i, j)))],
        (T // tm, D // tn2, N_DEV // pair), DIMS_NT, (tm, tn2), None)


def _local_step(x, target, gains, conv_a, conv_b, conv_b_bias, rg_w_a, rg_b_a, rg_w_x, rg_b_x, rg_lambda, wts, grads):
    T, D = x.shape
    g = lambda l, i: gains[l, i][None, :]
    wa_p = _block_diag_pairs(rg_w_a).astype(BF16)
    wx_p = _block_diag_pairs(rg_w_x).astype(BF16)

    def run(fn, name, *args, n_main=1, **kw):
        jw, jg = wts.jobs(name), grads.jobs(name)
        res = fn(name, *args, jobs=jw + jg, **kw)
        main, jo = res[:n_main], res[n_main:]
        wts.deliver(name, jo[:len(jw)])
        grads.deliver(name, jo[len(jw):])
        return main[0] if n_main == 1 else main

    h0 = run(_norm_fwd, "norm_in", x, g(0, 0))
    proj = run(_mm_nt, "w_in_fwd", h0, wts["w_in"], F32, out_seg=5)
    y = run(_mixer_b_fwd, "mixer_b_fwd", proj, conv_b, conv_b_bias, wa_p, rg_b_a, wx_p, rg_b_x, rg_lambda,
            _mixer_a_fwd(proj, conv_a))
    mix0 = run(_mm_nn, "w_out_fwd", y, wts["w_out"], BF16, a_seg=2)
    x1, h1 = run(_resid_norm, "resid_mix0", x, mix0, g(0, 1), g(0, 2), n_main=2)
    u0, act0, m0 = _mlp_fwd("0", h1, wts, run)
    x2, h2 = run(_resid_norm, "resid_mlp0", x1, m0, g(0, 3), g(1, 0), n_main=2)
    qkv = run(_mm_nt, "w_qkv_fwd", h2, wts["w_qkv"], BF16, out_seg=3)
    o, rmat = run(_attention_fwd, "attention_fwd", qkv, n_main=2)
    mix1 = run(_mm_nn, "w_o_fwd", o, wts["w_o"], BF16)
    x3, h3 = run(_resid_norm, "resid_mix1", x2, mix1, g(1, 1), g(1, 2), n_main=2)
    u1, act1, m1 = _mlp_fwd("1", h3, wts, run)
    dx4, dm1, dg13, sq = _final_loss("loss", x3, m1, g(1, 3), target)

    dh3 = _mlp_bwd("1", h3, u1, act1, dm1, wts, grads, run)
    dx3, dmix1, dg12, dg11 = _norm_bwd_pair("norm_bwd_x3", x3, g(1, 2), dh3, dx4, mix1, g(1, 1))
    grads.put("w_o", run(_mm_tn, "w_o_dw", o, dmix1, BF16).reshape(N_DEV, D // N_DEV, D))
    do = run(_mm_nt, "w_o_dx", dmix1, wts["w_o"], BF16)
    dqkv = run(_attention_bwd, "attention_bwd", qkv, do, rmat)
    grads.put("w_qkv", run(_mm_tn, "w_qkv_dw", dqkv, h2, BF16, a_seg=3).reshape(N_DEV, 3 * D // N_DEV, D))
    dh2 = run(_mm_nn, "w_qkv_dx", dqkv, wts["w_qkv"], BF16, a_seg=3)
    dx2, dm0, dg10, dg03 = _norm_bwd_pair("norm_bwd_x2", x2, g(1, 0), dh2, dx3, m0, g(0, 3))
    dh1 = _mlp_bwd("0", h1, u0, act0, dm0, wts, grads, run)
    dx1, dmix0, dg02, dg01 = _norm_bwd_pair("norm_bwd_x1", x1, g(0, 2), dh1, dx2, mix0, g(0, 1))
    grads.put("w_out", run(_mm_tn, "w_out_dw", y, dmix0, BF16, a_seg=2).reshape(N_DEV, D // N_DEV, D))
    dy = run(_mm_nt, "w_out_dx", dmix0, wts["w_out"], F32, out_seg=2)
    dproj_a, dconv_a = _mixer_a_bwd(proj, conv_a, dy)
    dproj, sm_b, dwa_p, dwx_p = run(_mixer_b_bwd, "mixer_b_bwd", proj, conv_b, conv_b_bias, wa_p, rg_b_a, wx_p, rg_b_x,
                                    rg_lambda, dy, dproj_a, n_main=4)
    grads.put("w_in", run(_mm_tn, "w_in_dw", dproj, h0, BF16, a_seg=5).reshape(N_DEV, 5 * D // (2 * N_DEV), D))
    dh0 = run(_mm_nn, "w_in_dx", dproj, wts["w_in"], BF16, a_seg=5)
    dx0, dg00 = run(_norm_bwd, "norm_bwd_x0", x, g(0, 0), dh0, dx1, n_main=2)

    C = D // 2
    lanes_to_vec = lambda t, row: t[:, row, :].reshape(1, C)
    small = {
        "norm_gains": jnp.concatenate([dg00, dg01, dg02, dg03, dg10, dg11, dg12, dg13], axis=0).reshape(2, 4, D),
        "conv_a": jnp.transpose(dconv_a[:, :3, :], (1, 0, 2)).reshape(3, C),
        "conv_b": jnp.transpose(sm_b[:, :4, :], (1, 0, 2)).reshape(4, C),
        "conv_b_bias": lanes_to_vec(sm_b, _ROW_BIAS),
        "rg_w_a": _diag_pairs(dwa_p),
        "rg_b_a": lanes_to_vec(sm_b, _ROW_BA),
        "rg_w_x": _diag_pairs(dwx_p),
        "rg_b_x": lanes_to_vec(sm_b, _ROW_BX),
        "rg_lambda": lanes_to_vec(sm_b, _ROW_LAM),
    }
    return sq[0, 0], dx0, small


def _my_index():
    return 4 * lax.axis_index("x") + 2 * lax.axis_index("y") + lax.axis_index("c")


def _peers():
    x, y, c = lax.axis_index("x"), lax.axis_index("y"), lax.axis_index("c")
    out = []
    for k in range(1, N_DEV):
        px = x ^ ((k >> 2) & 1)
        py = y ^ ((k >> 1) & 1)
        pc = c ^ (k & 1)
        out.append(((px, py, pc), 4 * px + 2 * py + pc))
    return out


GATHER_PLAN = {
    "norm_in": (("w_in", 0, 1, 1),),
    "w_in_fwd": (("w_out", 0, 1, 1), ("up0", 0, 1, 4)),
    "mixer_b_fwd": (("up0", 1, 3, 4),),
    "w_out_fwd": (("up0", 3, 4, 4),),
    "resid_mix0": (("down0", 0, 1, 4),),
    "mlp_up_l0": (("down0", 1, 4, 4),),
    "mlp_down_l0": (("w_qkv", 0, 1, 1),),
    "resid_mlp0": (("up1", 0, 1, 4),),
    "w_qkv_fwd": (("w_o", 0, 1, 1), ("up1", 1, 2, 4)),
    "attention_fwd": (("up1", 2, 4, 4), ("down1", 0, 2, 4)),
    "mlp_up_l1": (("down1", 2, 4, 4),),
}
EXCHANGE_PLAN = {
    "mlp_down_dx_l1": (("down1", 0, 3, 8),), "mlp_up_dw_l1": (("down1", 3, 6, 8),),
    "mlp_up_dx_l1": (("down1", 6, 8, 8), ("up1", 0, 1, 8)),
    "attention_bwd": (("up1", 1, 8, 8), ("w_o", 0, 1, 1)),
    "w_qkv_dx": (("w_qkv", 0, 1, 2),), "mlp_down_dw_l0": (("w_qkv", 1, 2, 2),),
    "mlp_down_dx_l0": (("down0", 0, 3, 8),), "mlp_up_dw_l0": (("down0", 3, 6, 8),),
    "mlp_up_dx_l0": (("down0", 6, 8, 8), ("up0", 0, 1, 8)),
    "w_out_dw": (("up0", 1, 2, 8),), "w_out_dx": (("up0", 2, 3, 8),),
    "mixer_b_bwd": (("up0", 3, 7, 8),),
    "w_in_dw": (("up0", 7, 8, 8), ("w_out", 0, 1, 2)),
    "w_in_dx": (("w_out", 1, 2, 2), ("w_in", 0, 1, 4)),
    "norm_bwd_x0": (("w_in", 1, 2, 4),),
    "adamw_mlp_w_down": (("w_in", 2, 3, 4),), "adamw_mlp_w_up": (("w_in", 3, 4, 4),),
}


def _job_sems():
    return [pltpu.SemaphoreType.DMA((N_DEV - 1,)), pltpu.SemaphoreType.DMA((N_DEV - 1,)), pltpu.SemaphoreType.DMA((1,))]


def _gather_job(shard, prev=None, lo=0, hi=1, parts=1):
    n = shard.shape[0] // parts
    assert n * parts == shard.shape[0]
    rows = pl.ds(lo * n, (hi - lo) * n)

    def ctx():
        x, y, c = lax.axis_index("x"), lax.axis_index("y"), lax.axis_index("c")
        chips = [(1 - x, y), (x, 1 - y), (1 - x, 1 - y)]
        return x, y, c, chips

    def idx(px, py, pc):
        return 4 * px + 2 * py + pc

    def copy(src, out, sems, k, block, to):
        return pltpu.make_async_remote_copy(
            src_ref=out.at[block, rows] if src is None else src.at[rows], dst_ref=out.at[block, rows],
            send_sem=sems[0].at[k], recv_sem=sems[1].at[k], device_id=to, device_id_type=MESH)

    def start(ins, outs, sems):
        x, y, c, chips = ctx()
        src, out = ins[0], outs[0]
        me = idx(x, y, c)
        pltpu.make_async_copy(src.at[rows], out.at[me, rows], sems[2].at[0]).start()
        copy(src, out, sems, 0, me, (x, y, 1 - c)).start()
        for j, (px, py) in enumerate(chips):
            copy(src, out, sems, 1 + j, me, (px, py, c)).start()

    def mid(ins, outs, sems):
        x, y, c, chips = ctx()
        out = outs[0]
        for j, (px, py) in enumerate(chips):
            copy(None, out, sems, 1 + j, idx(px, py, c), (x, y, c)).wait_recv()
            copy(None, out, sems, 4 + j, idx(px, py, c), (x, y, 1 - c)).start()

    def end(ins, outs, sems):
        x, y, c, chips = ctx()
        src, out = ins[0], outs[0]
        me = (x, y, c)
        copy(None, out, sems, 0, idx(x, y, 1 - c), me).wait_recv()
        for j, (px, py) in enumerate(chips):
            copy(None, out, sems, 4 + j, idx(px, py, 1 - c), me).wait_recv()
        for k in range(N_DEV - 1):
            copy(src, out, sems, k, idx(x, y, c), me).wait_send()
        pltpu.make_async_copy(src.at[rows], out.at[idx(x, y, c), rows], sems[2].at[0]).wait()

    out_shape = jax.ShapeDtypeStruct((N_DEV,) + shard.shape, shard.dtype)
    if prev is None:
        return _Job([shard], [out_shape], _job_sems(), start, mid, end)
    return _Job([shard, prev], [out_shape], _job_sems(), start, mid, end, alias={1: 0})


def _exchange_job(src, land, lo=0, hi=1, parts=1):
    n = src.shape[1] // parts
    assert n * parts == src.shape[1]

    def sl(ref, s):
        return ref.at[s, pl.ds(lo * n, (hi - lo) * n)]

    def start(ins, outs, sems):
        me = _my_index()
        pltpu.make_async_copy(sl(ins[0], me), sl(outs[0], me), sems[2].at[0]).start()
        for k, (pos, idx) in enumerate(_peers()):
            pltpu.make_async_remote_copy(
                src_ref=sl(ins[0], idx), dst_ref=sl(outs[0], me), send_sem=sems[0].at[k], recv_sem=sems[1].at[k],
                device_id=pos, device_id_type=MESH).start()

    def mid(ins, outs, sems):
        pass

    def end(ins, outs, sems):
        me = _my_index()
        for k, (pos, idx) in enumerate(_peers()):
            cp = pltpu.make_async_remote_copy(
                src_ref=sl(ins[0], idx), dst_ref=sl(outs[0], idx), send_sem=sems[0].at[k], recv_sem=sems[1].at[k],
                device_id=pos, device_id_type=MESH)
            cp.wait_recv()
            cp.wait_send()
        pltpu.make_async_copy(sl(ins[0], me), sl(outs[0], me), sems[2].at[0]).wait()

    return _Job([src, land], [jax.ShapeDtypeStruct(land.shape, land.dtype)], _job_sems(), start, mid, end, alias={1: 0})


def _adamw_math(w, g, m, v):
    m = ADAM_B1 * m + (1.0 - ADAM_B1) * g
    v = ADAM_B2 * v + (1.0 - ADAM_B2) * (g * g)
    m_hat = m / (1.0 - ADAM_B1 ** ADAM_STEP)
    v_hat = v / (1.0 - ADAM_B2 ** ADAM_STEP)
    delta = -ADAM_LR * (m_hat / (jnp.sqrt(v_hat) + ADAM_EPS) + ADAM_WD * w)
    return delta, m, v


def _sum_slots(ref):
    g = ref[0].astype(F32)
    for s in range(1, N_DEV):
        g = g + ref[s].astype(F32)
    return g


def _adamw_big(name, lands, w, m, v, jobs=(), transposed=False):
    L, R, C = w.shape
    assert len(lands) == L
    tr = _tile(R, max(LANES, (256 * 1024) // C))
    nr = R // tr

    def body(*refs):
        l_refs = refs[:L]
        w_ref, m_ref, v_ref, g_ref, d_ref, nm_ref, nv_ref = refs[L:]
        for li in range(L):
            @pl.when(pl.program_id(0) // nr == li)
            def _(li=li):
                g = _sum_slots(l_refs[li])
                if transposed:
                    g = g.T
                d, nm, nv = _adamw_math(w_ref[...], g, m_ref[...], v_ref[...])
                g_ref[...] = g
                d_ref[...] = d
                nm_ref[...] = nm
                nv_ref[...] = nv

    def land_spec(li):
        if transposed:
            return pl.BlockSpec((N_DEV, C, tr), lambda s: (0, 0, jnp.where(s // nr == li, s % nr, 0)))
        return pl.BlockSpec((N_DEV, tr, C), lambda s: (0, jnp.where(s // nr == li, s % nr, 0), 0))

    row = pl.BlockSpec((None, tr, C), lambda s: (s // nr, s % nr, 0))
    return _carry_call(
        body, name=name, steps=L * nr, in_specs=[land_spec(li) for li in range(L)] + [row, row, row],
        out_specs=[row] * 4, out_shape=[jax.ShapeDtypeStruct((L, R, C), F32)] * 4, scratch_shapes=[],
        args=(*lands, w, m, v), jobs=jobs)


def _sum8(name, slots):
    _, R, C = slots.shape

    def body(s_ref, o_ref):
        o_ref[...] = _sum_slots(s_ref)

    return pl.pallas_call(body, name=name, out_shape=jax.ShapeDtypeStruct((R, C), F32))(slots)


def _adamw_small(name, g, w, m, v):
    def body(g_ref, w_ref, m_ref, v_ref, d_ref, nm_ref, nv_ref):
        d, nm, nv = _adamw_math(w_ref[...], g_ref[...], m_ref[...], v_ref[...])
        d_ref[...] = d
        nm_ref[...] = nm
        nv_ref[...] = nv

    return pl.pallas_call(body, name=name, out_shape=[jax.ShapeDtypeStruct(w.shape, F32)] * 3)(g, w, m, v)


def _pack_rows(arrs):
    parts, spans, r0 = [], [], 0
    for a in arrs:
        flat = a.astype(F32).reshape(-1)
        rows = -(-flat.shape[0] // LANES)
        rows = -(-rows // SUBLANES) * SUBLANES
        flat = jnp.pad(flat, (0, rows * LANES - flat.shape[0]))
        parts.append(flat.reshape(rows, LANES))
        spans.append((r0, rows, a.shape))
        r0 += rows
    return jnp.concatenate(parts, axis=0), spans


def _unpack_rows(buf, span):
    r0, rows, shape = span
    n = math.prod(shape)
    return buf[..., r0:r0 + rows, :].reshape(buf.shape[:-2] + (rows * LANES,))[..., :n].reshape(buf.shape[:-2] + shape)


def _from_col_blocks(wb):
    B, K, n = wb.shape
    return jnp.transpose(wb, (1, 0, 2)).reshape(K, B * n)


def kernel(x, norm_gains, hyb_w_in, hyb_conv_a, hyb_conv_b, hyb_conv_b_bias, hyb_rg_w_a, hyb_rg_b_a, hyb_rg_w_x, hyb_rg_b_x, hyb_rg_lambda, hyb_w_out, sb_w_qkv, sb_w_o, mlp_w_up, mlp_w_down, loss_target, m_norm_gains, m_hyb_w_in, m_hyb_conv_a, m_hyb_conv_b, m_hyb_conv_b_bias, m_hyb_rg_w_a, m_hyb_rg_b_a, m_hyb_rg_w_x, m_hyb_rg_b_x, m_hyb_rg_lambda, m_hyb_w_out, m_sb_w_qkv, m_sb_w_o, m_mlp_w_up, m_mlp_w_down, v_norm_gains, v_hyb_w_in, v_hyb_conv_a, v_hyb_conv_b, v_hyb_conv_b_bias, v_hyb_rg_w_a, v_hyb_rg_b_a, v_hyb_rg_w_x, v_hyb_rg_b_x, v_hyb_rg_lambda, v_hyb_w_out, v_sb_w_qkv, v_sb_w_o, v_mlp_w_up, v_mlp_w_down):
    T, D = x.shape[1], x.shape[2]
    me = _my_index()

    small_shards, small_spans = _pack_rows([norm_gains, hyb_conv_a[0], hyb_conv_b[0]])
    (small_all,) = _run_jobs("gather_small", [_gather_job(small_shards)])
    gains_b = _unpack_rows(small_all, small_spans[0])
    gains = jnp.transpose(gains_b, (1, 2, 0, 3)).reshape(2, 4, D)
    conv_a = _from_col_blocks(_unpack_rows(small_all, small_spans[1]))
    conv_b = _from_col_blocks(_unpack_rows(small_all, small_spans[2]))

    shards = {"w_in": hyb_w_in[0].T, "w_out": hyb_w_out[0], "w_qkv": sb_w_qkv[0].T, "w_o": sb_w_o[0],
              "up0": mlp_w_up[0], "up1": mlp_w_up[1], "down0": mlp_w_down[0], "down1": mlp_w_down[1]}
    shards = {n: s.astype(BF16) for n, s in shards.items()}
    wts = _Weights({}, shards, GATHER_PLAN)
    grads_big = _Grads({n: lax.empty((N_DEV,) + s.shape, BF16) for n, s in shards.items()}, EXCHANGE_PLAN)

    sq, grad_x, small = _local_step(
        x[0], loss_target[0], gains, conv_a, conv_b, hyb_conv_b_bias, hyb_rg_w_a[0], hyb_rg_b_a, hyb_rg_w_x[0],
        hyb_rg_b_x, hyb_rg_lambda, wts, grads_big)


    names = ["norm_gains", "hyb_w_in", "hyb_conv_a", "hyb_conv_b", "hyb_conv_b_bias", "hyb_rg_w_a", "hyb_rg_b_a",
             "hyb_rg_w_x", "hyb_rg_b_x", "hyb_rg_lambda", "hyb_w_out", "sb_w_qkv", "sb_w_o", "mlp_w_up", "mlp_w_down"]
    params = dict(zip(names, [norm_gains, hyb_w_in, hyb_conv_a, hyb_conv_b, hyb_conv_b_bias, hyb_rg_w_a, hyb_rg_b_a,
                              hyb_rg_w_x, hyb_rg_b_x, hyb_rg_lambda, hyb_w_out, sb_w_qkv, sb_w_o, mlp_w_up, mlp_w_down]))
    moms = dict(zip(names, [m_norm_gains, m_hyb_w_in, m_hyb_conv_a, m_hyb_conv_b, m_hyb_conv_b_bias, m_hyb_rg_w_a,
                            m_hyb_rg_b_a, m_hyb_rg_w_x, m_hyb_rg_b_x, m_hyb_rg_lambda, m_hyb_w_out, m_sb_w_qkv,
                            m_sb_w_o, m_mlp_w_up, m_mlp_w_down]))
    vars_ = dict(zip(names, [v_norm_gains, v_hyb_w_in, v_hyb_conv_a, v_hyb_conv_b, v_hyb_conv_b_bias, v_hyb_rg_w_a,
                             v_hyb_rg_b_a, v_hyb_rg_w_x, v_hyb_rg_b_x, v_hyb_rg_lambda, v_hyb_w_out, v_sb_w_qkv,
                             v_sb_w_o, v_mlp_w_up, v_mlp_w_down]))
    grads, deltas, new_m, new_v = {}, {}, {}, {}

    big_lands = {"mlp_w_down": ["down0", "down1"], "mlp_w_up": ["up0", "up1"], "sb_w_qkv": ["w_qkv"], "sb_w_o": ["w_o"],
                 "hyb_w_out": ["w_out"], "hyb_w_in": ["w_in"]}
    for nm, keys in big_lands.items():
        call = f"adamw_{nm}"
        if nm == "hyb_w_in":
            grads_big.flush("exchange_grads")
        jobs = grads_big.jobs(call)
        assert not {k for k in keys} & {e[0] for e in EXCHANGE_PLAN.get(call, ())}
        outs = _adamw_big(call, [grads_big.lands[k] for k in keys], params[nm], moms[nm], vars_[nm], jobs=jobs,
                          transposed=nm in ("hyb_w_in", "sb_w_qkv"))
        grads[nm], deltas[nm], new_m[nm], new_v[nm] = outs[:4]
        grads_big.deliver(call, outs[4:])

    small_names = ["norm_gains", "hyb_conv_a", "hyb_conv_b", "hyb_conv_b_bias", "hyb_rg_w_a", "hyb_rg_b_a",
                   "hyb_rg_w_x", "hyb_rg_b_x", "hyb_rg_lambda"]
    small_keys = ["norm_gains", "conv_a", "conv_b", "conv_b_bias", "rg_w_a", "rg_b_a", "rg_w_x", "rg_b_x", "rg_lambda"]
    sg_buf, sg_spans = _pack_rows([small[k] for k in small_keys] + [sq.reshape(1)])
    (sg_all,) = _run_jobs("gather_small_grads", [_gather_job(sg_buf)])
    sg_sum = _sum8("sum_small_grads", sg_all)
    full = {nm: _unpack_rows(sg_sum, sp) for nm, sp in zip(small_names, sg_spans)}
    loss = _unpack_rows(sg_sum, sg_spans[-1])[0] * (0.5 / D)
    cb = (D // 2) // N_DEV
    small_grads = {
        "norm_gains": lax.dynamic_slice_in_dim(full["norm_gains"], me * (D // N_DEV), D // N_DEV, axis=2),
        "hyb_conv_a": lax.dynamic_slice_in_dim(full["hyb_conv_a"], me * cb, cb, axis=1)[None],
        "hyb_conv_b": lax.dynamic_slice_in_dim(full["hyb_conv_b"], me * cb, cb, axis=1)[None],
        "hyb_conv_b_bias": full["hyb_conv_b_bias"],
        "hyb_rg_w_a": full["hyb_rg_w_a"][None],
        "hyb_rg_b_a": full["hyb_rg_b_a"],
        "hyb_rg_w_x": full["hyb_rg_w_x"][None],
        "hyb_rg_b_x": full["hyb_rg_b_x"],
        "hyb_rg_lambda": full["hyb_rg_lambda"],
    }
    pk = lambda d: _pack_rows([d[nm] for nm in small_names])
    g_buf, spans = pk(small_grads)
    w_buf, _ = pk(params)
    m_buf, _ = pk(moms)
    v_buf, _ = pk(vars_)
    d_buf, nm_buf, nv_buf = _adamw_small("adamw_small", g_buf, w_buf, m_buf, v_buf)
    for nm, sp in zip(small_names, spans):
        grads[nm] = small_grads[nm]
        deltas[nm], new_m[nm], new_v[nm] = _unpack_rows(d_buf, sp), _unpack_rows(nm_buf, sp), _unpack_rows(nv_buf, sp)

    return (loss, grad_x[None], *[grads[n] for n in names], *[deltas[n] for n in names],
            *[new_m[n] for n in names], *[new_v[n] for n in names])
```

```python
import math

import jax
import jax.numpy as jnp
from jax import lax
from jax.experimental import pallas as pl
from jax.experimental.pallas import tpu as pltpu

F32 = jnp.float32
BF16 = jnp.bfloat16

NORM_EPS = 1e-6
LRU_C = 8.0
ATT_HEAD_DIM = 128
RG_HEAD_DIM = 64
LANES = 128
SUBLANES = 8
N_DEV = 8
ADAM_LR = 0.001
ADAM_B1 = 0.9
ADAM_B2 = 0.999
ADAM_EPS = 1e-08
ADAM_WD = 0.01
ADAM_STEP = 10
VMEM_LIMIT = 56 * 1024 * 1024
MM_TK = 2048
MM_TM = 2048
MM_TN = 1024
MM_TK_TOKENS = 4096
MESH = pl.DeviceIdType.MESH


def _tile(n, pref):
    if n <= pref:
        return n
    t = (pref // LANES) * LANES
    while t > LANES and n % t:
        t -= LANES
    assert n % t == 0, (n, pref)
    return t


def _params(sem):
    return pltpu.CompilerParams(dimension_semantics=sem, vmem_limit_bytes=VMEM_LIMIT)


DIMS_NN = (((1,), (0,)), ((), ()))
DIMS_NT = (((1,), (1,)), ((), ()))
DIMS_TN = (((0,), (0,)), ((), ()))


_ANY = pl.BlockSpec(memory_space=pl.ANY)


class _Job:
    def __init__(self, ins, outs, sems, start, mid, end, alias=None):
        self.ins, self.outs, self.sems = ins, outs, sems
        self.start, self.mid, self.end = start, mid, end
        self.alias = alias or {}


def _mid_step(steps):
    return (9 * steps) // 10


def _job_plumbing(jobs, n_in, n_out):
    j_ins = [a for jb in jobs for a in jb.ins]
    j_outs = [o for jb in jobs for o in jb.outs]
    j_sems = [s for jb in jobs for s in jb.sems]
    aliases, pi, po = {}, 0, 0
    for jb in jobs:
        for i_in, i_out in jb.alias.items():
            aliases[n_in + pi + i_in] = n_out + po + i_out
        pi += len(jb.ins)
        po += len(jb.outs)
    return j_ins, j_outs, j_sems, aliases


def _job_phase(jobs, which, jin, jout, jsem):
    pi = po = ps = 0
    for jb in jobs:
        getattr(jb, which)(jin[pi:pi + len(jb.ins)], jout[po:po + len(jb.outs)], jsem[ps:ps + len(jb.sems)])
        pi, po, ps = pi + len(jb.ins), po + len(jb.outs), ps + len(jb.sems)


def _run_jobs(name, jobs):
    j_ins, j_outs, j_sems, aliases = _job_plumbing(jobs, 0, 0)
    n_ji, n_jo = len(j_ins), len(j_outs)

    def body(*refs):
        jin, jout, jsem = refs[:n_ji], refs[n_ji:n_ji + n_jo], refs[n_ji + n_jo:]
        for which in ("start", "mid", "end"):
            _job_phase(jobs, which, jin, jout, jsem)

    return pl.pallas_call(body, name=name, in_specs=[_ANY] * n_ji, out_specs=[_ANY] * n_jo, out_shape=j_outs,
                          scratch_shapes=j_sems, input_output_aliases=aliases)(*j_ins)


def _carry_call(body, *, name, steps, in_specs, out_specs, out_shape, scratch_shapes, args, jobs=(), aliases=None,
                sequential=False):
    n_in, n_out, n_sc = len(in_specs), len(out_shape), len(scratch_shapes)
    j_ins, j_outs, j_sems, j_aliases = _job_plumbing(jobs, n_in, n_out)
    n_ji, n_jo = len(j_ins), len(j_outs)

    def wrapped(*refs):
        ins, jin = refs[:n_in], refs[n_in:n_in + n_ji]
        o0 = n_in + n_ji
        outs, jout = refs[o0:o0 + n_out], refs[o0 + n_out:o0 + n_out + n_jo]
        s0 = o0 + n_out + n_jo
        scratch, jsem = refs[s0:s0 + n_sc], refs[s0 + n_sc:]
        step = pl.program_id(0)
        if jobs:
            pl.when(step == 0)(lambda: _job_phase(jobs, "start", jin, jout, jsem))
            pl.when(step == _mid_step(steps))(lambda: _job_phase(jobs, "mid", jin, jout, jsem))
        body(*ins, *outs, *scratch)
        if jobs:
            pl.when(step == steps - 1)(lambda: _job_phase(jobs, "end", jin, jout, jsem))

    return pl.pallas_call(
        wrapped, name=name, grid=(steps,),
        in_specs=list(in_specs) + [_ANY] * n_ji, out_specs=list(out_specs) + [_ANY] * n_jo,
        out_shape=list(out_shape) + j_outs, scratch_shapes=list(scratch_shapes) + j_sems,
        input_output_aliases={**(aliases or {}), **j_aliases},
        compiler_params=_params(("arbitrary",) if (jobs or sequential) else ("parallel",)))(*args, *j_ins)


def _matmul(name, ins, outs, grid, dims, acc_shape, epilogue=None, jobs=()):
    n_in, n_out, nk = len(ins), len(outs), grid[2]
    j_ins, j_outs, j_sems, aliases = _job_plumbing(jobs, n_in, n_out)
    n_ji, n_jo = len(j_ins), len(j_outs)
    total = grid[0] * grid[1] * grid[2]
    n_acc = 0 if nk == 1 else 1

    def body(*refs):
        a_ref, b_ref = refs[0], refs[1]
        extras = refs[2:n_in]
        jin = refs[n_in:n_in + n_ji]
        out_refs = refs[n_in + n_ji:n_in + n_ji + n_out]
        jout = refs[n_in + n_ji + n_out:n_in + n_ji + n_out + n_jo]
        jsem = refs[n_in + n_ji + n_out + n_jo + n_acc:]
        k = pl.program_id(2)
        step = (pl.program_id(0) * grid[1] + pl.program_id(1)) * grid[2] + k
        if jobs:
            pl.when(step == 0)(lambda: _job_phase(jobs, "start", jin, jout, jsem))
            pl.when(step == _mid_step(total))(lambda: _job_phase(jobs, "mid", jin, jout, jsem))

        def finish(r):
            res = epilogue(r, *[e[...] for e in extras]) if epilogue is not None else (r,)
            for o, v in zip(out_refs, res):
                o[...] = v.astype(o.dtype)

        if len(b_ref.shape) == 2:
            prod = lax.dot_general(a_ref[...], b_ref[...], dims, preferred_element_type=F32)
        else:
            kb = a_ref.shape[1] // b_ref.shape[0]
            prod = sum(lax.dot_general(a_ref[:, g * kb:(g + 1) * kb], b_ref[g], dims, preferred_element_type=F32)
                       for g in range(b_ref.shape[0]))
        if nk == 1:
            finish(prod)
        else:
            acc = refs[n_in + n_ji + n_out + n_jo]

            @pl.when(k == 0)
            def _():
                acc[...] = prod

            @pl.when((k > 0) & (k < nk - 1))
            def _():
                acc[...] += prod

            @pl.when(k == nk - 1)
            def _():
                finish(acc[...] + prod)

        if jobs:
            pl.when(step == total - 1)(lambda: _job_phase(jobs, "end", jin, jout, jsem))

    sem = ("arbitrary",) * 3 if jobs else ("parallel", "parallel", "arbitrary")
    res = pl.pallas_call(
        body, name=name, grid=grid,
        in_specs=[s for _, s in ins] + [_ANY] * n_ji,
        out_specs=[s for _, s in outs] + [_ANY] * n_jo,
        out_shape=[s for s, _ in outs] + j_outs,
        scratch_shapes=[pltpu.VMEM(acc_shape, F32)] * n_acc + j_sems,
        input_output_aliases=aliases,
        compiler_params=_params(sem),
    )(*[a for a, _ in ins], *j_ins)
    return res


def _mm_nn(name, a, b, out_dtype, *, a_seg=None, out_seg=None, tm=MM_TM, tn=MM_TN, tk=MM_TK, jobs=()):
    if a_seg:
        _, M, ks = a.shape
        K = ks * a_seg
    else:
        M, K = a.shape
        ks = K
    N = b.shape[1]
    ns = N // out_seg if out_seg else N
    tm, tn, tk = _tile(M, tm), _tile(ns, tn), _tile(ks, tk)
    nks, nns = ks // tk, ns // tn
    grid = (M // tm, N // tn, K // tk)
    if a_seg:
        a_spec = pl.BlockSpec((None, tm, tk), lambda i, j, k: (k // nks, i, k % nks))
    else:
        a_spec = pl.BlockSpec((tm, tk), lambda i, j, k: (i, k))
    b_spec = pl.BlockSpec((tk, tn), lambda i, j, k: (k, j))
    if out_seg:
        o_spec = pl.BlockSpec((None, tm, tn), lambda i, j, k: (j // nns, i, j % nns))
        o_shape = (out_seg, M, ns)
    else:
        o_spec = pl.BlockSpec((tm, tn), lambda i, j, k: (i, j))
        o_shape = (M, N)
    outs = [(jax.ShapeDtypeStruct(o_shape, out_dtype), o_spec)]
    return _matmul(name, [(a, a_spec), (b, b_spec)], outs, grid, DIMS_NN, (tm, tn), None, jobs)


def _mm_nt(name, a, b, out_dtype, *, a_seg=None, out_seg=None, tm=MM_TM, tn=MM_TN, tk=MM_TK, jobs=()):
    if a_seg:
        _, M, ks = a.shape
        K = ks * a_seg
    else:
        M, K = a.shape
        ks = K
    N = b.shape[0]
    ns = N // out_seg if out_seg else N
    tm, tn, tk = _tile(M, tm), _tile(ns, tn), _tile(ks, tk)
    nks, nns = ks // tk, ns // tn
    grid = (M // tm, N // tn, K // tk)
    if a_seg:
        a_spec = pl.BlockSpec((None, tm, tk), lambda i, j, k: (k // nks, i, k % nks))
    else:
        a_spec = pl.BlockSpec((tm, tk), lambda i, j, k: (i, k))
    b_spec = pl.BlockSpec((tn, tk), lambda i, j, k: (j, k))
    if out_seg:
        o_spec = pl.BlockSpec((None, tm, tn), lambda i, j, k: (j // nns, i, j % nns))
        o_shape = (out_seg, M, ns)
    else:
        o_spec = pl.BlockSpec((tm, tn), lambda i, j, k: (i, j))
        o_shape = (M, N)
    outs = [(jax.ShapeDtypeStruct(o_shape, out_dtype), o_spec)]
    return _matmul(name, [(a, a_spec), (b, b_spec)], outs, grid, DIMS_NT, (tm, tn), None, jobs)


def _mm_tn(name, a, b, out_dtype, *, a_seg=None, b_seg=None, out_blocks=None, tm=MM_TN, tn=MM_TN, tk=MM_TK_TOKENS,
           jobs=()):
    if a_seg:
        _, T, ms = a.shape
        M = ms * a_seg
    else:
        T, M = a.shape
        ms = M
    if b_seg:
        _, _, ns = b.shape
        N = ns * b_seg
    else:
        N = b.shape[1]
        ns = N
    nb_cols = N // out_blocks if out_blocks else N
    tm, tk = _tile(ms, tm), _tile(T, tk)
    tn = _tile(math.gcd(ns, nb_cols), tn)
    nms, nns, nbs = ms // tm, ns // tn, nb_cols // tn
    grid = (M // tm, N // tn, T // tk)
    if a_seg:
        a_spec = pl.BlockSpec((None, tk, tm), lambda i, j, k: (i // nms, k, i % nms))
    else:
        a_spec = pl.BlockSpec((tk, tm), lambda i, j, k: (k, i))
    if b_seg:
        b_spec = pl.BlockSpec((None, tk, tn), lambda i, j, k: (j // nns, k, j % nns))
    else:
        b_spec = pl.BlockSpec((tk, tn), lambda i, j, k: (k, j))
    if out_blocks:
        o_spec = pl.BlockSpec((None, tm, tn), lambda i, j, k: (j // nbs, i, j % nbs))
        o_shape = (out_blocks, M, nb_cols)
    else:
        o_spec = pl.BlockSpec((tm, tn), lambda i, j, k: (i, j))
        o_shape = (M, N)
    outs = [(jax.ShapeDtypeStruct(o_shape, out_dtype), o_spec)]
    return _matmul(name, [(a, a_spec), (b, b_spec)], outs, grid, DIMS_TN, (tm, tn), None, jobs)


def _rms(x):
    return lax.rsqrt(jnp.mean(x * x, axis=-1, keepdims=True) + NORM_EPS)


def _row_tile(T):
    return _tile(T, 256)


def _norm_fwd(name, x, g, jobs=()):
    T, D = x.shape
    tr = _row_tile(T)

    def body(x_ref, g_ref, h_ref):
        xv = x_ref[...]
        h_ref[...] = (xv * _rms(xv) * g_ref[...]).astype(h_ref.dtype)

    row = pl.BlockSpec((tr, D), lambda i: (i, 0))
    vec = pl.BlockSpec((1, D), lambda i: (0, 0))
    return _carry_call(body, name=name, steps=T // tr, in_specs=[row, vec], out_specs=[row],
                       out_shape=[jax.ShapeDtypeStruct((T, D), BF16)], scratch_shapes=[], args=(x, g), jobs=jobs)


def _resid_norm(name, x, br, g_post, g_next, jobs=()):
    T, D = x.shape
    tr = _row_tile(T)

    def body(x_ref, br_ref, gp_ref, gn_ref, xn_ref, h_ref):
        b = br_ref[...].astype(F32)
        xn = x_ref[...] + b * _rms(b) * gp_ref[...]
        xn_ref[...] = xn
        h_ref[...] = (xn * _rms(xn) * gn_ref[...]).astype(h_ref.dtype)

    row = pl.BlockSpec((tr, D), lambda i: (i, 0))
    vec = pl.BlockSpec((1, D), lambda i: (0, 0))
    return _carry_call(body, name=name, steps=T // tr, in_specs=[row, row, vec, vec], out_specs=[row, row],
                       out_shape=[jax.ShapeDtypeStruct((T, D), F32), jax.ShapeDtypeStruct((T, D), BF16)],
                       scratch_shapes=[], args=(x, br, g_post, g_next), jobs=jobs)


def _rms_bwd(xv, g, dyv):
    r = _rms(xv)
    xhat = xv * r
    gdy = dyv * g
    dx = r * (gdy - xhat * jnp.mean(gdy * xhat, axis=-1, keepdims=True))
    return dx, jnp.sum(dyv * xhat, axis=0, keepdims=True)


def _final_loss(name, x, br, g_post, target):
    T, D = x.shape
    tr = _row_tile(T)

    def body(x_ref, br_ref, gp_ref, t_ref, dy_ref, dbr_ref, dg_ref, ls_ref):
        b = br_ref[...].astype(F32)
        err = x_ref[...] + b * _rms(b) * gp_ref[...] - t_ref[...]
        dy = err * (1.0 / D)
        dy_ref[...] = dy
        dbr, dg = _rms_bwd(b, gp_ref[...], dy)
        dbr_ref[...] = dbr.astype(dbr_ref.dtype)

        @pl.when(pl.program_id(0) == 0)
        def _():
            ls_ref[...] = jnp.zeros_like(ls_ref)
            dg_ref[...] = jnp.zeros_like(dg_ref)

        ls_ref[...] += jnp.sum(err * err)
        dg_ref[...] += dg

    row = pl.BlockSpec((tr, D), lambda i: (i, 0))
    vec = pl.BlockSpec((1, D), lambda i: (0, 0))
    acc = pl.BlockSpec((SUBLANES, LANES), lambda i: (0, 0))
    return pl.pallas_call(body, name=name, grid=(T // tr,), in_specs=[row, row, vec, row],
                          out_specs=[row, row, vec, acc],
                          out_shape=[jax.ShapeDtypeStruct((T, D), F32), jax.ShapeDtypeStruct((T, D), BF16),
                                     jax.ShapeDtypeStruct((1, D), F32), jax.ShapeDtypeStruct((SUBLANES, LANES), F32)],
                          compiler_params=_params(("arbitrary",)))(x, br, g_post, target)


def _norm_bwd_pair(name, x, g_pre, dh, add, br, g_post, jobs=()):
    T, D = x.shape
    tr = _row_tile(T)

    def body(x_ref, gpre_ref, dh_ref, add_ref, br_ref, gpost_ref, dx_ref, dbr_ref, dgpre_ref, dgpost_ref):
        dx, dg_pre = _rms_bwd(x_ref[...], gpre_ref[...], dh_ref[...].astype(F32))
        dx = dx + add_ref[...]
        dx_ref[...] = dx
        dbr, dg_post = _rms_bwd(br_ref[...].astype(F32), gpost_ref[...], dx)
        dbr_ref[...] = dbr.astype(dbr_ref.dtype)

        @pl.when(pl.program_id(0) == 0)
        def _():
            dgpre_ref[...] = jnp.zeros_like(dgpre_ref)
            dgpost_ref[...] = jnp.zeros_like(dgpost_ref)

        dgpre_ref[...] += dg_pre
        dgpost_ref[...] += dg_post

    row = pl.BlockSpec((tr, D), lambda i: (i, 0))
    vec = pl.BlockSpec((1, D), lambda i: (0, 0))
    return _carry_call(body, name=name, steps=T // tr, in_specs=[row, vec, row, row, row, vec],
                       out_specs=[row, row, vec, vec],
                       out_shape=[jax.ShapeDtypeStruct((T, D), F32), jax.ShapeDtypeStruct((T, D), BF16),
                                  jax.ShapeDtypeStruct((1, D), F32), jax.ShapeDtypeStruct((1, D), F32)],
                       scratch_shapes=[], args=(x, g_pre, dh, add, br, g_post), jobs=jobs, sequential=True)


def _norm_bwd(name, x, g, dy, add, jobs=()):
    T, D = x.shape
    tr = _row_tile(T)

    def body(x_ref, g_ref, dy_ref, add_ref, dx_ref, dg_ref):
        dx, dg = _rms_bwd(x_ref[...], g_ref[...], dy_ref[...].astype(F32))
        dx_ref[...] = dx + add_ref[...]

        @pl.when(pl.program_id(0) == 0)
        def _():
            dg_ref[...] = jnp.zeros_like(dg_ref)

        dg_ref[...] += dg

    row = pl.BlockSpec((tr, D), lambda i: (i, 0))
    vec = pl.BlockSpec((1, D), lambda i: (0, 0))
    return _carry_call(body, name=name, steps=T // tr, in_specs=[row, vec, row, row], out_specs=[row, vec],
                       out_shape=[jax.ShapeDtypeStruct((T, D), F32), jax.ShapeDtypeStruct((1, D), F32)],
                       scratch_shapes=[], args=(x, g, dy, add), jobs=jobs, sequential=True)


HALO = SUBLANES
TIME_CHUNK = 512


def _chunks(T):
    tc = min(TIME_CHUNK, T)
    assert T % tc == 0 and tc % SUBLANES == 0
    return [(t0, tc) for t0 in range(0, T, tc)]


def _log_sigmoid(x):
    return -(jnp.maximum(-x, 0.0) + jnp.log(1.0 + jnp.exp(-jnp.abs(x))))


def _sigmoid(x):
    return 0.5 * jnp.tanh(0.5 * x) + 0.5


def _one_minus_exp(x):
    series = -x * (1.0 + x * (0.5 + x * (1.0 / 6.0 + x * (1.0 / 24.0))))
    return jnp.where(x > -0.01, series, 1.0 - jnp.exp(x))


_GELU_C = math.sqrt(2.0 / math.pi)


def _gelu(x):
    return 0.5 * x * (1.0 + jnp.tanh(_GELU_C * (x + 0.044715 * x * x * x)))


def _gelu_grad(x):
    th = jnp.tanh(_GELU_C * (x + 0.044715 * x * x * x))
    return 0.5 * (1.0 + th) + 0.5 * x * (1.0 - th * th) * _GELU_C * (1.0 + 3.0 * 0.044715 * x * x)


def _tile_scan(a, b, reverse):
    rows = a.shape[0]
    pos = lax.broadcasted_iota(jnp.int32, a.shape, 0) & (SUBLANES - 1)
    for d in (1, 2, 4):
        if reverse:
            ok = pos < SUBLANES - d
            shift = rows - d
        else:
            ok = pos >= d
            shift = d
        a_sh = jnp.where(ok, pltpu.roll(a, shift, 0), 1.0)
        b_sh = jnp.where(ok, pltpu.roll(b, shift, 0), 0.0)
        b = a * b_sh + b
        a = a * a_sh
    return a, b


def _carry_scan(a_s, b_s, T, reverse):
    n = T // SUBLANES
    edge = 0 if reverse else SUBLANES - 1

    def step(j, carry):
        g = (n - 1 - j) if reverse else j
        r = pl.multiple_of(g * SUBLANES, SUBLANES)
        h = b_s[pl.ds(r, SUBLANES), :] + a_s[pl.ds(r, SUBLANES), :] * carry
        b_s[pl.ds(r, SUBLANES), :] = h
        return jnp.broadcast_to(h[edge:edge + 1, :], h.shape)

    lax.fori_loop(0, n, step, jnp.zeros((SUBLANES, a_s.shape[1]), F32))


def _seg_spec(T, seg, nblk):
    return pl.BlockSpec((None, T, LANES), lambda c: (seg, 0, c))


def _rows_to_tile(rows):
    idx = lax.broadcasted_iota(jnp.int32, (SUBLANES, LANES), 0)
    out = jnp.zeros((SUBLANES, LANES), F32)
    for k, r in enumerate(rows):
        out = jnp.where(idx == k, r, out)
    return out


def _mixer_a_fwd(proj, conv_a):
    _, T, C = proj.shape
    nblk = C // LANES
    chunks = _chunks(T)

    def body(bg_ref, cg_ref, ax_ref, w_ref, y_ref, p_s):
        p_s[pl.ds(0, HALO), :] = jnp.zeros((HALO, LANES), F32)
        for t0, tc in chunks:
            p_s[pl.ds(HALO + t0, tc), :] = cg_ref[pl.ds(t0, tc), :] * ax_ref[pl.ds(t0, tc), :]
        w = w_ref[...]
        for t0, tc in chunks:
            c = (w[2:3, :] * p_s[pl.ds(HALO + t0, tc), :] + w[1:2, :] * p_s[pl.ds(HALO + t0 - 1, tc), :]
                 + w[0:1, :] * p_s[pl.ds(HALO + t0 - 2, tc), :])
            y_ref[pl.ds(t0, tc), :] = (bg_ref[pl.ds(t0, tc), :] * c).astype(y_ref.dtype)

    return pl.pallas_call(
        body, name="mixer_a_fwd", grid=(nblk,),
        in_specs=[_seg_spec(T, 0, nblk), _seg_spec(T, 1, nblk), _seg_spec(T, 2, nblk),
                  pl.BlockSpec((3, LANES), lambda c: (0, c))],
        out_specs=_seg_spec(T, 0, nblk),
        out_shape=jax.ShapeDtypeStruct((2, T, C), BF16),
        scratch_shapes=[pltpu.VMEM((T + HALO, LANES), F32)],
        compiler_params=_params(("parallel",)))(proj, proj, proj, conv_a)


def _mixer_a_bwd(proj, conv_a, dy):
    _, T, C = proj.shape
    nblk = C // LANES
    chunks = _chunks(T)

    def body(bg_ref, cg_ref, ax_ref, w_ref, dy_ref, dp_ref, dw_ref, p_s, dc_s):
        p_s[pl.ds(0, HALO), :] = jnp.zeros((HALO, LANES), F32)
        dc_s[pl.ds(T, HALO), :] = jnp.zeros((HALO, LANES), F32)
        for t0, tc in chunks:
            p_s[pl.ds(HALO + t0, tc), :] = cg_ref[pl.ds(t0, tc), :] * ax_ref[pl.ds(t0, tc), :]
        w = w_ref[...]
        for t0, tc in chunks:
            c = (w[2:3, :] * p_s[pl.ds(HALO + t0, tc), :] + w[1:2, :] * p_s[pl.ds(HALO + t0 - 1, tc), :]
                 + w[0:1, :] * p_s[pl.ds(HALO + t0 - 2, tc), :])
            dyv = dy_ref[pl.ds(t0, tc), :]
            dp_ref[0, pl.ds(t0, tc), :] = (dyv * c).astype(dp_ref.dtype)
            dc_s[pl.ds(t0, tc), :] = dyv * bg_ref[pl.ds(t0, tc), :]
        dw = [jnp.zeros((1, LANES), F32) for _ in range(3)]
        for t0, tc in chunks:
            dc = dc_s[pl.ds(t0, tc), :]
            dpv = w[2:3, :] * dc + w[1:2, :] * dc_s[pl.ds(t0 + 1, tc), :] + w[0:1, :] * dc_s[pl.ds(t0 + 2, tc), :]
            dp_ref[1, pl.ds(t0, tc), :] = (dpv * ax_ref[pl.ds(t0, tc), :]).astype(dp_ref.dtype)
            dp_ref[2, pl.ds(t0, tc), :] = (dpv * cg_ref[pl.ds(t0, tc), :]).astype(dp_ref.dtype)
            for k in range(3):
                dw[k] = dw[k] + jnp.sum(dc * p_s[pl.ds(HALO + t0 - (2 - k), tc), :], axis=0, keepdims=True)
        dw_ref[...] = _rows_to_tile(dw)

    return pl.pallas_call(
        body, name="mixer_a_bwd", grid=(nblk,),
        in_specs=[_seg_spec(T, 0, nblk), _seg_spec(T, 1, nblk), _seg_spec(T, 2, nblk),
                  pl.BlockSpec((3, LANES), lambda c: (0, c)), _seg_spec(T, 0, nblk)],
        out_specs=[pl.BlockSpec((3, T, LANES), lambda c: (0, 0, c)),
                   pl.BlockSpec((None, SUBLANES, LANES), lambda c: (c, 0, 0))],
        out_shape=[jax.ShapeDtypeStruct((6, T, C), BF16), jax.ShapeDtypeStruct((nblk, SUBLANES, LANES), F32)],
        scratch_shapes=[pltpu.VMEM((T + HALO, LANES), F32), pltpu.VMEM((T + HALO, LANES), F32)],
        compiler_params=_params(("parallel",)))(proj, proj, proj, conv_a, dy)


def _rg_gates(xr, wa, ba, wx, bx, ls):
    xb = xr.astype(BF16)
    r = _sigmoid(jnp.dot(xb, wa, preferred_element_type=F32) + ba)
    i = _sigmoid(jnp.dot(xb, wx, preferred_element_type=F32) + bx)
    log_a = LRU_C * r * ls
    a = jnp.exp(log_a)
    mult = jnp.sqrt(_one_minus_exp(2.0 * log_a))
    return r, i, a, mult


def _conv4(xh_s, cw, bias, t0, tc):
    return (cw[3:4, :] * xh_s[pl.ds(HALO + t0, tc), :] + cw[2:3, :] * xh_s[pl.ds(HALO + t0 - 1, tc), :]
            + cw[1:2, :] * xh_s[pl.ds(HALO + t0 - 2, tc), :] + cw[0:1, :] * xh_s[pl.ds(HALO + t0 - 3, tc), :] + bias)


def _mixer_b_specs(T, nblk):
    vec = pl.BlockSpec((1, LANES), lambda c: (0, c))
    mat = pl.BlockSpec((None, LANES, LANES), lambda c: (c, 0, 0))
    return [_seg_spec(T, 3, nblk), _seg_spec(T, 4, nblk), pl.BlockSpec((4, LANES), lambda c: (0, c)),
            vec, mat, vec, mat, vec, vec]


def _mixer_b_fwd(name, proj, conv_b, bias, wa, ba, wx, bx, lam, y, jobs=()):
    _, T, C = proj.shape
    nblk = C // LANES
    chunks = _chunks(T)

    def body(gate_ref, x_ref, cw_ref, cb_ref, wa_ref, ba_ref, wx_ref, bx_ref, lam_ref, y_in, y_ref, xh_s, a_s, b_s):
        xh_s[pl.ds(0, HALO), :] = jnp.zeros((HALO, LANES), F32)
        for t0, tc in chunks:
            xh_s[pl.ds(HALO + t0, tc), :] = x_ref[pl.ds(t0, tc), :]
        cw, bias_v = cw_ref[...], cb_ref[...]
        ls = _log_sigmoid(lam_ref[...])
        for t0, tc in chunks:
            xr = _conv4(xh_s, cw, bias_v, t0, tc)
            r, i, a, mult = _rg_gates(xr, wa_ref[...], ba_ref[...], wx_ref[...], bx_ref[...], ls)
            ac, hc = _tile_scan(a, mult * i * xr, reverse=False)
            a_s[pl.ds(t0, tc), :] = ac
            b_s[pl.ds(t0, tc), :] = hc
        _carry_scan(a_s, b_s, T, reverse=False)
        for t0, tc in chunks:
            y_ref[pl.ds(t0, tc), :] = (b_s[pl.ds(t0, tc), :] * _gelu(gate_ref[pl.ds(t0, tc), :])).astype(y_ref.dtype)

    return _carry_call(
        body, name=name, steps=nblk, in_specs=_mixer_b_specs(T, nblk) + [_ANY],
        out_specs=[_seg_spec(T, 1, nblk)], out_shape=[jax.ShapeDtypeStruct(y.shape, y.dtype)],
        scratch_shapes=[pltpu.VMEM((T + HALO, LANES), F32), pltpu.VMEM((T, LANES), F32), pltpu.VMEM((T, LANES), F32)],
        args=(proj, proj, conv_b, bias, wa, ba, wx, bx, lam, y), jobs=jobs, aliases={9: 0})


_ROW_CONV, _ROW_BIAS, _ROW_BA, _ROW_BX, _ROW_LAM = 0, 4, 5, 6, 7


def _mixer_b_bwd(name, proj, conv_b, bias, wa, ba, wx, bx, lam, dy, dproj, jobs=()):
    _, T, C = proj.shape
    nblk = C // LANES
    chunks = _chunks(T)

    def body(gate_ref, x_ref, cw_ref, cb_ref, wa_ref, ba_ref, wx_ref, bx_ref, lam_ref, dy_ref, dp_in,
             dp_ref, sm_ref, dwa_ref, dwx_ref, xh_s, xr_s, r_s, i_s, a_s, h_s, sa_s, sb_s, dx_s):
        zero_halo = jnp.zeros((HALO, LANES), F32)
        xh_s[pl.ds(0, HALO), :] = zero_halo
        h_s[pl.ds(0, HALO), :] = zero_halo
        a_s[pl.ds(T, HALO), :] = zero_halo
        dx_s[pl.ds(T, HALO), :] = zero_halo
        for t0, tc in chunks:
            xh_s[pl.ds(HALO + t0, tc), :] = x_ref[pl.ds(t0, tc), :]
        cw, bias_v = cw_ref[...], cb_ref[...]
        lam_v = lam_ref[...]
        ls = _log_sigmoid(lam_v)
        wa_v, wx_v, ba_v, bx_v = wa_ref[...], wx_ref[...], ba_ref[...], bx_ref[...]
        for t0, tc in chunks:
            xr = _conv4(xh_s, cw, bias_v, t0, tc)
            r, i, a, mult = _rg_gates(xr, wa_v, ba_v, wx_v, bx_v, ls)
            xr_s[pl.ds(t0, tc), :] = xr
            r_s[pl.ds(t0, tc), :] = r
            i_s[pl.ds(t0, tc), :] = i
            a_s[pl.ds(t0, tc), :] = a
            ac, hc = _tile_scan(a, mult * i * xr, reverse=False)
            sa_s[pl.ds(t0, tc), :] = ac
            sb_s[pl.ds(t0, tc), :] = hc
        _carry_scan(sa_s, sb_s, T, reverse=False)
        for t0, tc in chunks:
            h_s[pl.ds(HALO + t0, tc), :] = sb_s[pl.ds(t0, tc), :]
        for t0, tc in chunks:
            gv = gate_ref[pl.ds(t0, tc), :]
            dyv = dy_ref[pl.ds(t0, tc), :]
            dp_ref[0, pl.ds(t0, tc), :] = (dyv * h_s[pl.ds(HALO + t0, tc), :] * _gelu_grad(gv)).astype(dp_ref.dtype)
            ac, gc = _tile_scan(a_s[pl.ds(t0 + 1, tc), :], dyv * _gelu(gv), reverse=True)
            sa_s[pl.ds(t0, tc), :] = ac
            sb_s[pl.ds(t0, tc), :] = gc
        _carry_scan(sa_s, sb_s, T, reverse=True)
        acc = {k: jnp.zeros((1, LANES), F32) for k in ("bias", "ba", "bx", "lam")}
        dwa = jnp.zeros((LANES, LANES), F32)
        dwx = jnp.zeros((LANES, LANES), F32)
        for t0, tc in chunks:
            dht = sb_s[pl.ds(t0, tc), :]
            xr, r, i, a = xr_s[pl.ds(t0, tc), :], r_s[pl.ds(t0, tc), :], i_s[pl.ds(t0, tc), :], a_s[pl.ds(t0, tc), :]
            mult = jnp.sqrt(_one_minus_exp(2.0 * LRU_C * r * ls))
            da = dht * h_s[pl.ds(HALO + t0 - 1, tc), :]
            dmult = dht * i * xr
            di = dht * mult * xr
            dlog_a = da * a - dmult * a * a / mult
            dpa = dlog_a * (LRU_C * ls) * r * (1.0 - r)
            dpx = di * i * (1.0 - i)
            acc["lam"] = acc["lam"] + jnp.sum(dlog_a * r, axis=0, keepdims=True)
            acc["ba"] = acc["ba"] + jnp.sum(dpa, axis=0, keepdims=True)
            acc["bx"] = acc["bx"] + jnp.sum(dpx, axis=0, keepdims=True)
            xb, dpab, dpxb = xr.astype(BF16), dpa.astype(BF16), dpx.astype(BF16)
            dwa = dwa + lax.dot_general(xb, dpab, DIMS_TN, preferred_element_type=F32)
            dwx = dwx + lax.dot_general(xb, dpxb, DIMS_TN, preferred_element_type=F32)
            dxr = (dht * mult * i + lax.dot_general(dpab, wa_v, DIMS_NT, preferred_element_type=F32)
                   + lax.dot_general(dpxb, wx_v, DIMS_NT, preferred_element_type=F32))
            acc["bias"] = acc["bias"] + jnp.sum(dxr, axis=0, keepdims=True)
            dx_s[pl.ds(t0, tc), :] = dxr
        dcw = [jnp.zeros((1, LANES), F32) for _ in range(4)]
        for t0, tc in chunks:
            dxr = dx_s[pl.ds(t0, tc), :]
            dxin = (cw[3:4, :] * dxr + cw[2:3, :] * dx_s[pl.ds(t0 + 1, tc), :] + cw[1:2, :] * dx_s[pl.ds(t0 + 2, tc), :]
                    + cw[0:1, :] * dx_s[pl.ds(t0 + 3, tc), :])
            dp_ref[1, pl.ds(t0, tc), :] = dxin.astype(dp_ref.dtype)
            for k in range(4):
                dcw[k] = dcw[k] + jnp.sum(dxr * xh_s[pl.ds(HALO + t0 - (3 - k), tc), :], axis=0, keepdims=True)
        dlam = acc["lam"] * LRU_C * _sigmoid(-lam_v)
        sm_ref[...] = _rows_to_tile(dcw + [acc["bias"], acc["ba"], acc["bx"], dlam])
        dwa_ref[...] = dwa
        dwx_ref[...] = dwx

    big = lambda halo: pltpu.VMEM((T + halo, LANES), F32)
    mat = pl.BlockSpec((None, LANES, LANES), lambda c: (c, 0, 0))
    return _carry_call(
        body, name=name, steps=nblk,
        in_specs=_mixer_b_specs(T, nblk) + [_seg_spec(T, 1, nblk), _ANY],
        out_specs=[pl.BlockSpec((3, T, LANES), lambda c: (1, 0, c)),
                   pl.BlockSpec((None, SUBLANES, LANES), lambda c: (c, 0, 0)), mat, mat],
        out_shape=[jax.ShapeDtypeStruct(dproj.shape, dproj.dtype), jax.ShapeDtypeStruct((nblk, SUBLANES, LANES), F32),
                   jax.ShapeDtypeStruct((nblk, LANES, LANES), F32), jax.ShapeDtypeStruct((nblk, LANES, LANES), F32)],
        scratch_shapes=[big(HALO), big(0), big(0), big(0), big(HALO), big(HALO), big(0), big(0), big(HALO)],
        args=(proj, proj, conv_b, bias, wa, ba, wx, bx, lam, dy, dproj), jobs=jobs, aliases={10: 0})


ATT_BLOCK = 128
ATT_GROUP = 3
ATT_TILE = ATT_BLOCK * ATT_GROUP
ATT_UNDERFLOW = -110.0
ATT_UNVISITED = -1e30


def _split_dot(x, m):
    hi = x.astype(BF16)
    lo = (x - hi.astype(F32)).astype(BF16)
    return jnp.dot(hi, m, preferred_element_type=F32) + jnp.dot(lo, m, preferred_element_type=F32)


def _sub(x, j):
    return x[:, j * ATT_BLOCK:(j + 1) * ATT_BLOCK]


def _stack_rows(x):
    return jnp.concatenate([_sub(x, j) for j in range(ATT_GROUP)], axis=0)


def _unstack_rows(x, offsets):
    return jnp.concatenate([x[j * ATT_BLOCK:(j + 1) * ATT_BLOCK, :] + offsets[j] for j in range(ATT_GROUP)], axis=1)


def _att_tile(q, k_ref, q0, qb, it, scale):
    hi = (qb + 1 - ATT_GROUP * it) * ATT_BLOCK
    k0 = pl.multiple_of(jnp.maximum(hi - ATT_TILE, 0), ATT_BLOCK)
    kt = k_ref[pl.ds(k0, ATT_TILE), :]
    z = lax.dot_general(q, kt, DIMS_NT, preferred_element_type=F32) * scale
    key = k0 + lax.broadcasted_iota(jnp.int32, z.shape, 1)
    row = q0 + lax.broadcasted_iota(jnp.int32, z.shape, 0)
    mask = (key < row) & (key < hi)
    n = jnp.where(mask, -(jnp.maximum(z, 0.0) + jnp.log(1.0 + jnp.exp(-jnp.abs(z)))), 0.0)
    return k0, kt, z, mask, n


def _suffix_in_tile(n, upper, run):
    rs = [jnp.sum(_sub(n, j), axis=1, keepdims=True) for j in range(ATT_GROUP)]
    offs = [None] * ATT_GROUP
    offs[ATT_GROUP - 1] = run
    for j in range(ATT_GROUP - 2, -1, -1):
        offs[j] = offs[j + 1] + rs[j + 1]
    return _unstack_rows(_split_dot(_stack_rows(n), upper), offs), offs[0] + rs[0]


def _head_spec(T, seg, heads):
    return pl.BlockSpec((None, T, ATT_HEAD_DIM), lambda h: (seg, 0, h))


def _attention_fwd(name, qkv, jobs=()):
    _, T, D = qkv.shape
    heads = D // ATT_HEAD_DIM
    nq = T // ATT_BLOCK
    assert nq <= LANES and T >= ATT_TILE and nq % 2 == 0
    scale = 1.0 / math.sqrt(ATT_HEAD_DIM)

    def body(q_ref, k_ref, v_ref, o_ref, r_ref, acc_s, run_s):
        rr = lax.broadcasted_iota(jnp.int32, (ATT_BLOCK, ATT_BLOCK), 0)
        cc = lax.broadcasted_iota(jnp.int32, (ATT_BLOCK, ATT_BLOCK), 1)
        upper = jnp.where(rr > cc, 1.0, 0.0).astype(BF16)
        lane = lax.broadcasted_iota(jnp.int32, (ATT_BLOCK, LANES), 1)

        def tile(slot, qb, q0, q, it, first):
            k0, _, z, mask, n = _att_tile(q, k_ref, q0, qb, it, scale)
            run = jnp.zeros((ATT_BLOCK, LANES), F32) if first else run_s[slot]
            suffix, run_next = _suffix_in_tile(n, upper, run)
            w = jnp.where(mask, jnp.exp(z + n + suffix), 0.0)
            pv = jnp.dot(w.astype(BF16), v_ref[pl.ds(k0, ATT_TILE), :], preferred_element_type=F32)
            if first:
                acc_s[slot] = pv
            else:
                acc_s[slot] += pv
                r_ref[pl.ds(q0, ATT_BLOCK), :] = jnp.where(lane == it, run, r_ref[pl.ds(q0, ATT_BLOCK), :])
            run_s[slot] = run_next
            return jnp.max(run_next) >= ATT_UNDERFLOW

        def pair_loop(p, _):
            blocks = []
            for slot in range(2):
                qb = 2 * p + slot
                q0 = pl.multiple_of(qb * ATT_BLOCK, ATT_BLOCK)
                r_ref[pl.ds(q0, ATT_BLOCK), :] = jnp.where(lane == 0, 0.0, ATT_UNVISITED)
                blocks.append((qb, q0, q_ref[pl.ds(q0, ATT_BLOCK), :]))
            go = [tile(slot, *blocks[slot], 0, True) for slot in range(2)]
            for slot in range(2):
                qb, q0, q = blocks[slot]
                n_tiles = (qb + ATT_GROUP) // ATT_GROUP
                lax.while_loop(lambda c: (c[0] < n_tiles) & c[1],
                               lambda c: (c[0] + 1, tile(slot, qb, q0, q, c[0], False)), (jnp.int32(1), go[slot]))
                o_ref[pl.ds(q0, ATT_BLOCK), :] = acc_s[slot].astype(o_ref.dtype)
            return 0

        lax.fori_loop(0, nq // 2, pair_loop, 0)

    return _carry_call(
        body, name=name, steps=heads,
        in_specs=[_head_spec(T, 0, heads), _head_spec(T, 1, heads), _head_spec(T, 2, heads)],
        out_specs=[pl.BlockSpec((T, ATT_HEAD_DIM), lambda h: (0, h)), pl.BlockSpec((None, T, LANES), lambda h: (h, 0, 0))],
        out_shape=[jax.ShapeDtypeStruct((T, D), BF16), jax.ShapeDtypeStruct((heads, T, LANES), F32)],
        scratch_shapes=[pltpu.VMEM((2, ATT_BLOCK, ATT_HEAD_DIM), F32), pltpu.VMEM((2, ATT_BLOCK, LANES), F32)],
        args=(qkv, qkv, qkv), jobs=jobs)


def _attention_bwd(name, qkv, do, rmat, jobs=()):
    _, T, D = qkv.shape
    heads = D // ATT_HEAD_DIM
    nq = T // ATT_BLOCK
    scale = 1.0 / math.sqrt(ATT_HEAD_DIM)

    def body(q_ref, k_ref, v_ref, do_ref, r_ref, dqkv_ref, dk_s, dv_s, dq_s, left_s):
        rr = lax.broadcasted_iota(jnp.int32, (ATT_BLOCK, ATT_BLOCK), 0)
        cc = lax.broadcasted_iota(jnp.int32, (ATT_BLOCK, ATT_BLOCK), 1)
        upper = jnp.where(rr > cc, 1.0, 0.0).astype(BF16)
        lower = jnp.where(rr < cc, 1.0, 0.0).astype(BF16)
        lane = lax.broadcasted_iota(jnp.int32, (ATT_BLOCK, LANES), 1)
        dk_s[...] = jnp.zeros_like(dk_s)
        dv_s[...] = jnp.zeros_like(dv_s)

        def tile(slot, qb, q0, q, dov, it, first):
            k0, kt, z, mask, n = _att_tile(q, k_ref, q0, qb, it, scale)
            vt = v_ref[pl.ds(k0, ATT_TILE), :]
            run = jnp.sum(jnp.where(lane == it, r_ref[pl.ds(q0, ATT_BLOCK), :], 0.0), axis=1, keepdims=True)
            suffix, _ = _suffix_in_tile(n, upper, run)
            s = z + n
            w = jnp.where(mask, jnp.exp(s + suffix), 0.0)
            e = w * lax.dot_general(dov, vt, DIMS_NT, preferred_element_type=F32)
            es = [jnp.sum(_sub(e, g), axis=1, keepdims=True) for g in range(ATT_GROUP)]
            pre = [jnp.zeros((ATT_BLOCK, LANES), F32) if first else left_s[slot]]
            for g in range(ATT_GROUP):
                pre.append(pre[g] + es[g])
            before = _unstack_rows(_split_dot(_stack_rows(e), lower), pre)
            sig = jnp.exp(s)
            dz = (jnp.where(mask, e * (1.0 - sig) - before * sig, 0.0) * scale).astype(BF16)
            dq = jnp.dot(dz, kt, preferred_element_type=F32)
            if first:
                dq_s[slot] = dq
            else:
                dq_s[slot] += dq
            dk_s[pl.ds(k0, ATT_TILE), :] += lax.dot_general(dz, q, DIMS_TN, preferred_element_type=F32)
            dv_s[pl.ds(k0, ATT_TILE), :] += lax.dot_general(w.astype(BF16), dov, DIMS_TN, preferred_element_type=F32)
            left_s[slot] = pre[ATT_GROUP]

        def pair_loop(p, _):
            blocks = []
            for slot in range(2):
                qb = 2 * p + slot
                q0 = pl.multiple_of(qb * ATT_BLOCK, ATT_BLOCK)
                n_tiles = (qb + ATT_GROUP) // ATT_GROUP
                seen = ((jnp.max(r_ref[pl.ds(q0, ATT_BLOCK), :], axis=0, keepdims=True) > 0.5 * ATT_UNVISITED)
                        & (lane[0:1, :] < n_tiles))
                n_visited = jnp.maximum(jnp.sum(jnp.where(seen, 1.0, 0.0)).astype(jnp.int32), 1)
                blocks.append((qb, q0, q_ref[pl.ds(q0, ATT_BLOCK), :], do_ref[pl.ds(q0, ATT_BLOCK), :], n_visited))
            for slot in range(2):
                qb, q0, q, dov, n_visited = blocks[slot]
                tile(slot, qb, q0, q, dov, n_visited - 1, True)
            for slot in range(2):
                qb, q0, q, dov, n_visited = blocks[slot]
                lax.fori_loop(1, n_visited, lambda j, c: (tile(slot, qb, q0, q, dov, n_visited - 1 - j, False), c)[1], 0)
                dqkv_ref[0, pl.ds(q0, ATT_BLOCK), :] = dq_s[slot].astype(dqkv_ref.dtype)
            return 0

        lax.fori_loop(0, nq // 2, pair_loop, 0)
        dqkv_ref[1, :, :] = dk_s[...].astype(dqkv_ref.dtype)
        dqkv_ref[2, :, :] = dv_s[...].astype(dqkv_ref.dtype)

    return _carry_call(
        body, name=name, steps=heads,
        in_specs=[_head_spec(T, 0, heads), _head_spec(T, 1, heads), _head_spec(T, 2, heads),
                  pl.BlockSpec((T, ATT_HEAD_DIM), lambda h: (0, h)), pl.BlockSpec((None, T, LANES), lambda h: (h, 0, 0))],
        out_specs=[pl.BlockSpec((3, T, ATT_HEAD_DIM), lambda h: (0, 0, h))],
        out_shape=[jax.ShapeDtypeStruct((3, T, D), BF16)],
        scratch_shapes=[pltpu.VMEM((T, ATT_HEAD_DIM), F32), pltpu.VMEM((T, ATT_HEAD_DIM), F32),
                        pltpu.VMEM((2, ATT_BLOCK, ATT_HEAD_DIM), F32), pltpu.VMEM((2, ATT_BLOCK, LANES), F32)],
        args=(qkv, qkv, qkv, do, rmat), jobs=jobs)


def _block_diag_pairs(w):
    h = w.shape[0]
    wp = w.reshape(h // 2, 2, RG_HEAD_DIM, RG_HEAD_DIM)
    z = jnp.zeros_like(wp[:, 0])
    top = jnp.concatenate([wp[:, 0], z], axis=2)
    bot = jnp.concatenate([z, wp[:, 1]], axis=2)
    return jnp.concatenate([top, bot], axis=1)


def _diag_pairs(g):
    n = g.shape[0]
    a = g[:, :RG_HEAD_DIM, :RG_HEAD_DIM]
    b = g[:, RG_HEAD_DIM:, RG_HEAD_DIM:]
    return jnp.stack([a, b], axis=1).reshape(2 * n, RG_HEAD_DIM, RG_HEAD_DIM)


class _Weights:
    def __init__(self, full, shards=None, plan=None):
        self.full, self.shards, self.plan = dict(full), shards or {}, plan or {}
        self.partial, self.rows = {}, {}

    def __getitem__(self, name):
        return self.full[name]

    def jobs(self, call):
        return [_gather_job(self.shards[n], self.partial.get(n), lo, hi, parts)
                for n, lo, hi, parts in self.plan.get(call, ())]

    def deliver(self, call, outs):
        for (n, lo, hi, parts), g in zip(self.plan.get(call, ()), outs):
            self.partial[n] = g
            self.rows[n] = self.rows.get(n, 0) + hi - lo
            if self.rows[n] == parts:
                self.full[n] = _gathered_layout(n, g)


def _gathered_layout(name, g):
    if name in ("w_in", "w_qkv", "w_out", "w_o"):
        return g.reshape(g.shape[0] * g.shape[1], g.shape[2])
    return g


class _Grads:
    def __init__(self, lands=None, plan=None):
        self.lands, self.plan = dict(lands) if lands else None, plan or {}
        self.ready, self.sent = {}, {}

    def put(self, name, arr):
        self.ready[name] = arr

    def jobs(self, call):
        if self.lands is None:
            return []
        return [_exchange_job(self.ready[n], self.lands[n], lo, hi, parts) for n, lo, hi, parts in self.plan.get(call, ())]

    def deliver(self, call, outs):
        for (n, lo, hi, parts), o in zip(self.plan.get(call, ()), outs):
            assert self.sent.get(n, (0, parts)) == (lo, parts), (call, n)
            self.lands[n] = o
            self.sent[n] = (hi, parts)

    def flush(self, name):
        if self.lands is None:
            return
        rest = []
        for n in self.ready:
            lo, parts = self.sent.get(n, (0, 1))
            if lo < parts:
                rest.append((n, lo, parts, parts))
        if rest:
            outs = _run_jobs(name, [_exchange_job(self.ready[n], self.lands[n], lo, hi, parts) for n, lo, hi, parts in rest])
            for (n, _, hi, parts), o in zip(rest, outs):
                self.lands[n] = o
                self.sent[n] = (hi, parts)


def _mlp_fwd(tag, h, wts, run):
    T, D = h.shape
    w_up = wts["up" + tag]
    fb = w_up.shape[2]
    F = fb * N_DEV
    tm, tn, tk = _tile(T, MM_TM), _tile(fb, MM_TN), _tile(D, MM_TK)
    nb = fb // tn

    def up_epilogue(u):
        r = jnp.maximum(u, 0.0)
        return u, r * r

    o_spec = pl.BlockSpec((tm, tn), lambda i, j, k: (i, j))
    u, act = run(
        _matmul, f"mlp_up_l{tag}",
        [(h, pl.BlockSpec((tm, tk), lambda i, j, k: (i, k))),
         (w_up, pl.BlockSpec((None, tk, tn), lambda i, j, k: (j // nb, k, j % nb)))],
        [(jax.ShapeDtypeStruct((T, F), BF16), o_spec), (jax.ShapeDtypeStruct((T, F), BF16), o_spec)],
        (T // tm, F // tn, D // tk), DIMS_NN, (tm, tn), up_epilogue, n_main=2)
    w_down = wts["down" + tag].reshape(F, D)
    m = run(_mm_nn, f"mlp_down_l{tag}", act, w_down, BF16)
    return u, act, m


def _mlp_bwd(tag, h, u, act, dm, wts, grads, run):
    T, D = h.shape
    w_up, w_down = wts["up" + tag], wts["down" + tag]
    fb = w_up.shape[2]
    F = fb * N_DEV
    grads.put("down" + tag, run(_mm_tn, f"mlp_down_dw_l{tag}", act, dm, BF16).reshape(N_DEV, fb, D))
    tm, tn, tk = _tile(T, MM_TM), _tile(fb, MM_TN), _tile(D, MM_TK)
    nb = fb // tn
    o_spec = pl.BlockSpec((tm, tn), lambda i, j, k: (i, j))
    du = run(
        _matmul, f"mlp_down_dx_l{tag}",
        [(dm, pl.BlockSpec((tm, tk), lambda i, j, k: (i, k))),
         (w_down, pl.BlockSpec((None, tn, tk), lambda i, j, k: (j // nb, j % nb, k))),
         (u, o_spec)],
        [(jax.ShapeDtypeStruct((T, F), BF16), o_spec)],
        (T // tm, F // tn, D // tk), DIMS_NT, (tm, tn),
        lambda r, uv: (r * (2.0 * jnp.maximum(uv.astype(F32), 0.0)),))
    grads.put("up" + tag, run(_mm_tn, f"mlp_up_dw_l{tag}", h, du, BF16, out_blocks=N_DEV))
    tn2 = _tile(D, MM_TN)
    pair = 2 if MM_TK >= 2 * fb else 1
    return run(
        _matmul, f"mlp_up_dx_l{tag}",
        [(du, pl.BlockSpec((tm, pair * fb), lambda i, j, k: (i, k))),
         (w_up, pl.BlockSpec((pair, tn2, fb), lambda i, j, k: (k, j, 0)))],
        [(jax.ShapeDtypeStruct((T, D), BF16), pl.BlockSpec((tm, tn2), lambda i, j, k: (i, j)))],
        (T // tm, D // tn2, N_DEV // pair), DIMS_NT, (tm, tn2), None)


def _local_step(x, target, gains, conv_a, conv_b, conv_b_bias, rg_w_a, rg_b_a, rg_w_x, rg_b_x, rg_lambda, wts, grads):
    T, D = x.shape
    g = lambda l, i: gains[l, i][None, :]
    wa_p = _block_diag_pairs(rg_w_a).astype(BF16)
    wx_p = _block_diag_pairs(rg_w_x).astype(BF16)

    def run(fn, name, *args, n_main=1, **kw):
        jw, jg = wts.jobs(name), grads.jobs(name)
        res = fn(name, *args, jobs=jw + jg, **kw)
        main, jo = res[:n_main], res[n_main:]
        wts.deliver(name, jo[:len(jw)])
        grads.deliver(name, jo[len(jw):])
        return main[0] if n_main == 1 else main

    h0 = run(_norm_fwd, "norm_in", x, g(0, 0))
    proj = run(_mm_nt, "w_in_fwd", h0, wts["w_in"], F32, out_seg=5)
    y = run(_mixer_b_fwd, "mixer_b_fwd", proj, conv_b, conv_b_bias, wa_p, rg_b_a, wx_p, rg_b_x, rg_lambda,
            _mixer_a_fwd(proj, conv_a))
    mix0 = run(_mm_nn, "w_out_fwd", y, wts["w_out"], BF16, a_seg=2)
    x1, h1 = run(_resid_norm, "resid_mix0", x, mix0, g(0, 1), g(0, 2), n_main=2)
    u0, act0, m0 = _mlp_fwd("0", h1, wts, run)
    x2, h2 = run(_resid_norm, "resid_mlp0", x1, m0, g(0, 3), g(1, 0), n_main=2)
    qkv = run(_mm_nt, "w_qkv_fwd", h2, wts["w_qkv"], BF16, out_seg=3)
    o, rmat = run(_attention_fwd, "attention_fwd", qkv, n_main=2)
    mix1 = run(_mm_nn, "w_o_fwd", o, wts["w_o"], BF16)
    x3, h3 = run(_resid_norm, "resid_mix1", x2, mix1, g(1, 1), g(1, 2), n_main=2)
    u1, act1, m1 = _mlp_fwd("1", h3, wts, run)
    dx4, dm1, dg13, sq = _final_loss("loss", x3, m1, g(1, 3), target)

    dh3 = _mlp_bwd("1", h3, u1, act1, dm1, wts, grads, run)
    dx3, dmix1, dg12, dg11 = run(_norm_bwd_pair, "norm_bwd_x3", x3, g(1, 2), dh3, dx4, mix1, g(1, 1), n_main=4)
    grads.put("w_o", run(_mm_tn, "w_o_dw", o, dmix1, BF16).reshape(N_DEV, D // N_DEV, D))
    do = run(_mm_nt, "w_o_dx", dmix1, wts["w_o"], BF16)
    dqkv = run(_attention_bwd, "attention_bwd", qkv, do, rmat)
    grads.put("w_qkv", run(_mm_tn, "w_qkv_dw", dqkv, h2, BF16, a_seg=3).reshape(N_DEV, 3 * D // N_DEV, D))
    dh2 = run(_mm_nn, "w_qkv_dx", dqkv, wts["w_qkv"], BF16, a_seg=3)
    dx2, dm0, dg10, dg03 = run(_norm_bwd_pair, "norm_bwd_x2", x2, g(1, 0), dh2, dx3, m0, g(0, 3), n_main=4)
    dh1 = _mlp_bwd("0", h1, u0, act0, dm0, wts, grads, run)
    dx1, dmix0, dg02, dg01 = run(_norm_bwd_pair, "norm_bwd_x1", x1, g(0, 2), dh1, dx2, mix0, g(0, 1), n_main=4)
    grads.put("w_out", run(_mm_tn, "w_out_dw", y, dmix0, BF16, a_seg=2).reshape(N_DEV, D // N_DEV, D))
    dy = run(_mm_nt, "w_out_dx", dmix0, wts["w_out"], F32, out_seg=2)
    dproj_a, dconv_a = _mixer_a_bwd(proj, conv_a, dy)
    dproj, sm_b, dwa_p, dwx_p = run(_mixer_b_bwd, "mixer_b_bwd", proj, conv_b, conv_b_bias, wa_p, rg_b_a, wx_p, rg_b_x,
                                    rg_lambda, dy, dproj_a, n_main=4)
    grads.put("w_in", run(_mm_tn, "w_in_dw", dproj, h0, BF16, a_seg=5).reshape(N_DEV, 5 * D // (2 * N_DEV), D))
    dh0 = run(_mm_nn, "w_in_dx", dproj, wts["w_in"], BF16, a_seg=5)
    dx0, dg00 = run(_norm_bwd, "norm_bwd_x0", x, g(0, 0), dh0, dx1, n_main=2)

    C = D // 2
    lanes_to_vec = lambda t, row: t[:, row, :].reshape(1, C)
    small = {
        "norm_gains": jnp.concatenate([dg00, dg01, dg02, dg03, dg10, dg11, dg12, dg13], axis=0).reshape(2, 4, D),
        "conv_a": jnp.transpose(dconv_a[:, :3, :], (1, 0, 2)).reshape(3, C),
        "conv_b": jnp.transpose(sm_b[:, :4, :], (1, 0, 2)).reshape(4, C),
        "conv_b_bias": lanes_to_vec(sm_b, _ROW_BIAS),
        "rg_w_a": _diag_pairs(dwa_p),
        "rg_b_a": lanes_to_vec(sm_b, _ROW_BA),
        "rg_w_x": _diag_pairs(dwx_p),
        "rg_b_x": lanes_to_vec(sm_b, _ROW_BX),
        "rg_lambda": lanes_to_vec(sm_b, _ROW_LAM),
    }
    return sq[0, 0], dx0, small


def _my_index():
    return 4 * lax.axis_index("x") + 2 * lax.axis_index("y") + lax.axis_index("c")


def _peers():
    x, y, c = lax.axis_index("x"), lax.axis_index("y"), lax.axis_index("c")
    out = []
    for k in range(1, N_DEV):
        px = x ^ ((k >> 2) & 1)
        py = y ^ ((k >> 1) & 1)
        pc = c ^ (k & 1)
        out.append(((px, py, pc), 4 * px + 2 * py + pc))
    return out


GATHER_PLAN = {
    "norm_in": (("w_in", 0, 1, 1),),
    "w_in_fwd": (("w_out", 0, 1, 1), ("up0", 0, 1, 4)),
    "mixer_b_fwd": (("up0", 1, 3, 4),),
    "w_out_fwd": (("up0", 3, 4, 4),),
    "resid_mix0": (("down0", 0, 1, 4),),
    "mlp_up_l0": (("down0", 1, 4, 4),),
    "mlp_down_l0": (("w_qkv", 0, 1, 1),),
    "resid_mlp0": (("up1", 0, 1, 4),),
    "w_qkv_fwd": (("w_o", 0, 1, 1), ("up1", 1, 2, 4)),
    "attention_fwd": (("up1", 2, 4, 4), ("down1", 0, 2, 4)),
    "mlp_up_l1": (("down1", 2, 4, 4),),
}
EXCHANGE_PLAN = {
    "mlp_down_dx_l1": (("down1", 0, 3, 8),), "mlp_up_dw_l1": (("down1", 3, 6, 8),),
    "mlp_up_dx_l1": (("down1", 6, 8, 8), ("up1", 0, 1, 8)),
    "attention_bwd": (("up1", 1, 8, 8), ("w_o", 0, 1, 1)),
    "w_qkv_dx": (("w_qkv", 0, 2, 6),), "norm_bwd_x2": (("w_qkv", 2, 3, 6),), "mlp_down_dw_l0": (("w_qkv", 3, 6, 6),),
    "mlp_down_dx_l0": (("down0", 0, 3, 8),), "mlp_up_dw_l0": (("down0", 3, 6, 8),),
    "mlp_up_dx_l0": (("down0", 6, 8, 8), ("up0", 0, 1, 8)),
    "norm_bwd_x1": (("up0", 1, 2, 8),), "w_out_dw": (("up0", 2, 3, 8),),
    "mixer_b_bwd": (("up0", 3, 7, 8),),
    "w_in_dw": (("up0", 7, 8, 8), ("w_out", 0, 1, 2)),
    "w_in_dx": (("w_out", 1, 2, 2), ("w_in", 0, 1, 4)),
    "norm_bwd_x0": (("w_in", 1, 2, 4),),
    "adamw_mlp_w_down": (("w_in", 2, 3, 4),), "adamw_mlp_w_up": (("w_in", 3, 4, 4),),
}


def _job_sems():
    return [pltpu.SemaphoreType.DMA((N_DEV - 1,)), pltpu.SemaphoreType.DMA((N_DEV - 1,)), pltpu.SemaphoreType.DMA((1,))]


def _gather_job(shard, prev=None, lo=0, hi=1, parts=1):
    n = shard.shape[0] // parts
    assert n * parts == shard.shape[0]
    rows = pl.ds(lo * n, (hi - lo) * n)

    def ctx():
        x, y, c = lax.axis_index("x"), lax.axis_index("y"), lax.axis_index("c")
        chips = [(1 - x, y), (x, 1 - y), (1 - x, 1 - y)]
        return x, y, c, chips

    def idx(px, py, pc):
        return 4 * px + 2 * py + pc

    def copy(src, out, sems, k, block, to):
        return pltpu.make_async_remote_copy(
            src_ref=out.at[block, rows] if src is None else src.at[rows], dst_ref=out.at[block, rows],
            send_sem=sems[0].at[k], recv_sem=sems[1].at[k], device_id=to, device_id_type=MESH)

    def start(ins, outs, sems):
        x, y, c, chips = ctx()
        src, out = ins[0], outs[0]
        me = idx(x, y, c)
        pltpu.make_async_copy(src.at[rows], out.at[me, rows], sems[2].at[0]).start()
        copy(src, out, sems, 0, me, (x, y, 1 - c)).start()
        for j, (px, py) in enumerate(chips):
            copy(src, out, sems, 1 + j, me, (px, py, c)).start()

    def mid(ins, outs, sems):
        x, y, c, chips = ctx()
        out = outs[0]
        for j, (px, py) in enumerate(chips):
            copy(None, out, sems, 1 + j, idx(px, py, c), (x, y, c)).wait_recv()
            copy(None, out, sems, 4 + j, idx(px, py, c), (x, y, 1 - c)).start()

    def end(ins, outs, sems):
        x, y, c, chips = ctx()
        src, out = ins[0], outs[0]
        me = (x, y, c)
        copy(None, out, sems, 0, idx(x, y, 1 - c), me).wait_recv()
        for j, (px, py) in enumerate(chips):
            copy(None, out, sems, 4 + j, idx(px, py, 1 - c), me).wait_recv()
        for k in range(N_DEV - 1):
            copy(src, out, sems, k, idx(x, y, c), me).wait_send()
        pltpu.make_async_copy(src.at[rows], out.at[idx(x, y, c), rows], sems[2].at[0]).wait()

    out_shape = jax.ShapeDtypeStruct((N_DEV,) + shard.shape, shard.dtype)
    if prev is None:
        return _Job([shard], [out_shape], _job_sems(), start, mid, end)
    return _Job([shard, prev], [out_shape], _job_sems(), start, mid, end, alias={1: 0})


def _exchange_job(src, land, lo=0, hi=1, parts=1):
    n = src.shape[1] // parts
    assert n * parts == src.shape[1]

    def sl(ref, s):
        return ref.at[s, pl.ds(lo * n, (hi - lo) * n)]

    def start(ins, outs, sems):
        me = _my_index()
        pltpu.make_async_copy(sl(ins[0], me), sl(outs[0], me), sems[2].at[0]).start()
        for k, (pos, idx) in enumerate(_peers()):
            pltpu.make_async_remote_copy(
                src_ref=sl(ins[0], idx), dst_ref=sl(outs[0], me), send_sem=sems[0].at[k], recv_sem=sems[1].at[k],
                device_id=pos, device_id_type=MESH).start()

    def mid(ins, outs, sems):
        pass

    def end(ins, outs, sems):
        me = _my_index()
        for k, (pos, idx) in enumerate(_peers()):
            cp = pltpu.make_async_remote_copy(
                src_ref=sl(ins[0], idx), dst_ref=sl(outs[0], idx), send_sem=sems[0].at[k], recv_sem=sems[1].at[k],
                device_id=pos, device_id_type=MESH)
            cp.wait_recv()
            cp.wait_send()
        pltpu.make_async_copy(sl(ins[0], me), sl(outs[0], me), sems[2].at[0]).wait()

    return _Job([src, land], [jax.ShapeDtypeStruct(land.shape, land.dtype)], _job_sems(), start, mid, end, alias={1: 0})


def _adamw_math(w, g, m, v):
    m = ADAM_B1 * m + (1.0 - ADAM_B1) * g
    v = ADAM_B2 * v + (1.0 - ADAM_B2) * (g * g)
    m_hat = m / (1.0 - ADAM_B1 ** ADAM_STEP)
    v_hat = v / (1.0 - ADAM_B2 ** ADAM_STEP)
    delta = -ADAM_LR * (m_hat / (jnp.sqrt(v_hat) + ADAM_EPS) + ADAM_WD * w)
    return delta, m, v


def _sum_slots(ref):
    g = ref[0].astype(F32)
    for s in range(1, N_DEV):
        g = g + ref[s].astype(F32)
    return g


def _adamw_big(name, lands, w, m, v, jobs=(), transposed=False):
    L, R, C = w.shape
    assert len(lands) == L
    tr = _tile(R, max(LANES, (256 * 1024) // C))
    nr = R // tr

    def body(*refs):
        l_refs = refs[:L]
        w_ref, m_ref, v_ref, g_ref, d_ref, nm_ref, nv_ref = refs[L:]
        for li in range(L):
            @pl.when(pl.program_id(0) // nr == li)
            def _(li=li):
                g = _sum_slots(l_refs[li])
                if transposed:
                    g = g.T
                d, nm, nv = _adamw_math(w_ref[...], g, m_ref[...], v_ref[...])
                g_ref[...] = g
                d_ref[...] = d
                nm_ref[...] = nm
                nv_ref[...] = nv

    def land_spec(li):
        if transposed:
            return pl.BlockSpec((N_DEV, C, tr), lambda s: (0, 0, jnp.where(s // nr == li, s % nr, 0)))
        return pl.BlockSpec((N_DEV, tr, C), lambda s: (0, jnp.where(s // nr == li, s % nr, 0), 0))

    row = pl.BlockSpec((None, tr, C), lambda s: (s // nr, s % nr, 0))
    return _carry_call(
        body, name=name, steps=L * nr, in_specs=[land_spec(li) for li in range(L)] + [row, row, row],
        out_specs=[row] * 4, out_shape=[jax.ShapeDtypeStruct((L, R, C), F32)] * 4, scratch_shapes=[],
        args=(*lands, w, m, v), jobs=jobs)


def _sum8(name, slots):
    _, R, C = slots.shape

    def body(s_ref, o_ref):
        o_ref[...] = _sum_slots(s_ref)

    return pl.pallas_call(body, name=name, out_shape=jax.ShapeDtypeStruct((R, C), F32))(slots)


def _adamw_small(name, g, w, m, v):
    def body(g_ref, w_ref, m_ref, v_ref, d_ref, nm_ref, nv_ref):
        d, nm, nv = _adamw_math(w_ref[...], g_ref[...], m_ref[...], v_ref[...])
        d_ref[...] = d
        nm_ref[...] = nm
        nv_ref[...] = nv

    return pl.pallas_call(body, name=name, out_shape=[jax.ShapeDtypeStruct(w.shape, F32)] * 3)(g, w, m, v)


def _pack_rows(arrs):
    parts, spans, r0 = [], [], 0
    for a in arrs:
        flat = a.astype(F32).reshape(-1)
        rows = -(-flat.shape[0] // LANES)
        rows = -(-rows // SUBLANES) * SUBLANES
        flat = jnp.pad(flat, (0, rows * LANES - flat.shape[0]))
        parts.append(flat.reshape(rows, LANES))
        spans.append((r0, rows, a.shape))
        r0 += rows
    return jnp.concatenate(parts, axis=0), spans


def _unpack_rows(buf, span):
    r0, rows, shape = span
    n = math.prod(shape)
    return buf[..., r0:r0 + rows, :].reshape(buf.shape[:-2] + (rows * LANES,))[..., :n].reshape(buf.shape[:-2] + shape)


def _from_col_blocks(wb):
    B, K, n = wb.shape
    return jnp.transpose(wb, (1, 0, 2)).reshape(K, B * n)


def kernel(x, norm_gains, hyb_w_in, hyb_conv_a, hyb_conv_b, hyb_conv_b_bias, hyb_rg_w_a, hyb_rg_b_a, hyb_rg_w_x, hyb_rg_b_x, hyb_rg_lambda, hyb_w_out, sb_w_qkv, sb_w_o, mlp_w_up, mlp_w_down, loss_target, m_norm_gains, m_hyb_w_in, m_hyb_conv_a, m_hyb_conv_b, m_hyb_conv_b_bias, m_hyb_rg_w_a, m_hyb_rg_b_a, m_hyb_rg_w_x, m_hyb_rg_b_x, m_hyb_rg_lambda, m_hyb_w_out, m_sb_w_qkv, m_sb_w_o, m_mlp_w_up, m_mlp_w_down, v_norm_gains, v_hyb_w_in, v_hyb_conv_a, v_hyb_conv_b, v_hyb_conv_b_bias, v_hyb_rg_w_a, v_hyb_rg_b_a, v_hyb_rg_w_x, v_hyb_rg_b_x, v_hyb_rg_lambda, v_hyb_w_out, v_sb_w_qkv, v_sb_w_o, v_mlp_w_up, v_mlp_w_down):
    T, D = x.shape[1], x.shape[2]
    me = _my_index()

    small_shards, small_spans = _pack_rows([norm_gains, hyb_conv_a[0], hyb_conv_b[0]])
    (small_all,) = _run_jobs("gather_small", [_gather_job(small_shards)])
    gains_b = _unpack_rows(small_all, small_spans[0])
    gains = jnp.transpose(gains_b, (1, 2, 0, 3)).reshape(2, 4, D)
    conv_a = _from_col_blocks(_unpack_rows(small_all, small_spans[1]))
    conv_b = _from_col_blocks(_unpack_rows(small_all, small_spans[2]))

    shards = {"w_in": hyb_w_in[0].T, "w_out": hyb_w_out[0], "w_qkv": sb_w_qkv[0].T, "w_o": sb_w_o[0],
              "up0": mlp_w_up[0], "up1": mlp_w_up[1], "down0": mlp_w_down[0], "down1": mlp_w_down[1]}
    shards = {n: s.astype(BF16) for n, s in shards.items()}
    wts = _Weights({}, shards, GATHER_PLAN)
    grads_big = _Grads({n: lax.empty((N_DEV,) + s.shape, BF16) for n, s in shards.items()}, EXCHANGE_PLAN)

    sq, grad_x, small = _local_step(
        x[0], loss_target[0], gains, conv_a, conv_b, hyb_conv_b_bias, hyb_rg_w_a[0], hyb_rg_b_a, hyb_rg_w_x[0],
        hyb_rg_b_x, hyb_rg_lambda, wts, grads_big)


    names = ["norm_gains", "hyb_w_in", "hyb_conv_a", "hyb_conv_b", "hyb_conv_b_bias", "hyb_rg_w_a", "hyb_rg_b_a",
             "hyb_rg_w_x", "hyb_rg_b_x", "hyb_rg_lambda", "hyb_w_out", "sb_w_qkv", "sb_w_o", "mlp_w_up", "mlp_w_down"]
    params = dict(zip(names, [norm_gains, hyb_w_in, hyb_conv_a, hyb_conv_b, hyb_conv_b_bias, hyb_rg_w_a, hyb_rg_b_a,
                              hyb_rg_w_x, hyb_rg_b_x, hyb_rg_lambda, hyb_w_out, sb_w_qkv, sb_w_o, mlp_w_up, mlp_w_down]))
    moms = dict(zip(names, [m_norm_gains, m_hyb_w_in, m_hyb_conv_a, m_hyb_conv_b, m_hyb_conv_b_bias, m_hyb_rg_w_a,
                            m_hyb_rg_b_a, m_hyb_rg_w_x, m_hyb_rg_b_x, m_hyb_rg_lambda, m_hyb_w_out, m_sb_w_qkv,
                            m_sb_w_o, m_mlp_w_up, m_mlp_w_down]))
    vars_ = dict(zip(names, [v_norm_gains, v_hyb_w_in, v_hyb_conv_a, v_hyb_conv_b, v_hyb_conv_b_bias, v_hyb_rg_w_a,
                             v_hyb_rg_b_a, v_hyb_rg_w_x, v_hyb_rg_b_x, v_hyb_rg_lambda, v_hyb_w_out, v_sb_w_qkv,
                             v_sb_w_o, v_mlp_w_up, v_mlp_w_down]))
    grads, deltas, new_m, new_v = {}, {}, {}, {}

    big_lands = {"mlp_w_down": ["down0", "down1"], "mlp_w_up": ["up0", "up1"], "sb_w_qkv": ["w_qkv"], "sb_w_o": ["w_o"],
                 "hyb_w_out": ["w_out"], "hyb_w_in": ["w_in"]}
    for nm, keys in big_lands.items():
        call = f"adamw_{nm}"
        if nm == "hyb_w_in":
            grads_big.flush("exchange_grads")
        jobs = grads_big.jobs(call)
        assert not {k for k in keys} & {e[0] for e in EXCHANGE_PLAN.get(call, ())}
        outs = _adamw_big(call, [grads_big.lands[k] for k in keys], params[nm], moms[nm], vars_[nm], jobs=jobs,
                          transposed=nm in ("hyb_w_in", "sb_w_qkv"))
        grads[nm], deltas[nm], new_m[nm], new_v[nm] = outs[:4]
        grads_big.deliver(call, outs[4:])

    small_names = ["norm_gains", "hyb_conv_a", "hyb_conv_b", "hyb_conv_b_bias", "hyb_rg_w_a", "hyb_rg_b_a",
                   "hyb_rg_w_x", "hyb_rg_b_x", "hyb_rg_lambda"]
    small_keys = ["norm_gains", "conv_a", "conv_b", "conv_b_bias", "rg_w_a", "rg_b_a", "rg_w_x", "rg_b_x", "rg_lambda"]
    sg_buf, sg_spans = _pack_rows([small[k] for k in small_keys] + [sq.reshape(1)])
    (sg_all,) = _run_jobs("gather_small_grads", [_gather_job(sg_buf)])
    sg_sum = _sum8("sum_small_grads", sg_all)
    full = {nm: _unpack_rows(sg_sum, sp) for nm, sp in zip(small_names, sg_spans)}
    loss = _unpack_rows(sg_sum, sg_spans[-1])[0] * (0.5 / D)
    cb = (D // 2) // N_DEV
    small_grads = {
        "norm_gains": lax.dynamic_slice_in_dim(full["norm_gains"], me * (D // N_DEV), D // N_DEV, axis=2),
        "hyb_conv_a": lax.dynamic_slice_in_dim(full["hyb_conv_a"], me * cb, cb, axis=1)[None],
        "hyb_conv_b": lax.dynamic_slice_in_dim(full["hyb_conv_b"], me * cb, cb, axis=1)[None],
        "hyb_conv_b_bias": full["hyb_conv_b_bias"],
        "hyb_rg_w_a": full["hyb_rg_w_a"][None],
        "hyb_rg_b_a": full["hyb_rg_b_a"],
        "hyb_rg_w_x": full["hyb_rg_w_x"][None],
        "hyb_rg_b_x": full["hyb_rg_b_x"],
        "hyb_rg_lambda": full["hyb_rg_lambda"],
    }
    pk = lambda d: _pack_rows([d[nm] for nm in small_names])
    g_buf, spans = pk(small_grads)
    w_buf, _ = pk(params)
    m_buf, _ = pk(moms)
    v_buf, _ = pk(vars_)
    d_buf, nm_buf, nv_buf = _adamw_small("adamw_small", g_buf, w_buf, m_buf, v_buf)
    for nm, sp in zip(small_names, spans):
        grads[nm] = small_grads[nm]
        deltas[nm], new_m[nm], new_v[nm] = _unpack_rows(d_buf, sp), _unpack_rows(nm_buf, sp), _unpack_rows(nv_buf, sp)

    return (loss, grad_x[None], *[grads[n] for n in names], *[deltas[n] for n in names],
            *[new_m[n] for n in names], *[new_v[n] for n in names])
```

```python
import math

import jax
import jax.numpy as jnp
from jax import lax
from jax.experimental import pallas as pl
from jax.experimental.pallas import tpu as pltpu

F32 = jnp.float32
BF16 = jnp.bfloat16

NORM_EPS = 1e-6
LRU_C = 8.0
ATT_HEAD_DIM = 128
RG_HEAD_DIM = 64
LANES = 128
SUBLANES = 8
N_DEV = 8
ADAM_LR = 0.001
ADAM_B1 = 0.9
ADAM_B2 = 0.999
ADAM_EPS = 1e-08
ADAM_WD = 0.01
ADAM_STEP = 10
VMEM_LIMIT = 56 * 1024 * 1024
MM_TK = 2048
MM_TM = 2048
MM_TN = 1024
MM_TK_TOKENS = 4096
MESH = pl.DeviceIdType.MESH


def _tile(n, pref):
    if n <= pref:
        return n
    t = (pref // LANES) * LANES
    while t > LANES and n % t:
        t -= LANES
    assert n % t == 0, (n, pref)
    return t


def _params(sem):
    return pltpu.CompilerParams(dimension_semantics=sem, vmem_limit_bytes=VMEM_LIMIT)


DIMS_NN = (((1,), (0,)), ((), ()))
DIMS_NT = (((1,), (1,)), ((), ()))
DIMS_TN = (((0,), (0,)), ((), ()))


_ANY = pl.BlockSpec(memory_space=pl.ANY)


class _Job:
    def __init__(self, ins, outs, sems, start, mid, end, alias=None):
        self.ins, self.outs, self.sems = ins, outs, sems
        self.start, self.mid, self.end = start, mid, end
        self.alias = alias or {}


def _mid_step(steps):
    return (9 * steps) // 10


def _job_plumbing(jobs, n_in, n_out):
    j_ins = [a for jb in jobs for a in jb.ins]
    j_outs = [o for jb in jobs for o in jb.outs]
    j_sems = [s for jb in jobs for s in jb.sems]
    aliases, pi, po = {}, 0, 0
    for jb in jobs:
        for i_in, i_out in jb.alias.items():
            aliases[n_in + pi + i_in] = n_out + po + i_out
        pi += len(jb.ins)
        po += len(jb.outs)
    return j_ins, j_outs, j_sems, aliases


def _job_phase(jobs, which, jin, jout, jsem):
    pi = po = ps = 0
    for jb in jobs:
        getattr(jb, which)(jin[pi:pi + len(jb.ins)], jout[po:po + len(jb.outs)], jsem[ps:ps + len(jb.sems)])
        pi, po, ps = pi + len(jb.ins), po + len(jb.outs), ps + len(jb.sems)


def _run_jobs(name, jobs):
    j_ins, j_outs, j_sems, aliases = _job_plumbing(jobs, 0, 0)
    n_ji, n_jo = len(j_ins), len(j_outs)

    def body(*refs):
        jin, jout, jsem = refs[:n_ji], refs[n_ji:n_ji + n_jo], refs[n_ji + n_jo:]
        for which in ("start", "mid", "end"):
            _job_phase(jobs, which, jin, jout, jsem)

    return pl.pallas_call(body, name=name, in_specs=[_ANY] * n_ji, out_specs=[_ANY] * n_jo, out_shape=j_outs,
                          scratch_shapes=j_sems, input_output_aliases=aliases)(*j_ins)


def _carry_call(body, *, name, steps, in_specs, out_specs, out_shape, scratch_shapes, args, jobs=(), aliases=None,
                sequential=False):
    n_in, n_out, n_sc = len(in_specs), len(out_shape), len(scratch_shapes)
    j_ins, j_outs, j_sems, j_aliases = _job_plumbing(jobs, n_in, n_out)
    n_ji, n_jo = len(j_ins), len(j_outs)

    def wrapped(*refs):
        ins, jin = refs[:n_in], refs[n_in:n_in + n_ji]
        o0 = n_in + n_ji
        outs, jout = refs[o0:o0 + n_out], refs[o0 + n_out:o0 + n_out + n_jo]
        s0 = o0 + n_out + n_jo
        scratch, jsem = refs[s0:s0 + n_sc], refs[s0 + n_sc:]
        step = pl.program_id(0)
        if jobs:
            pl.when(step == 0)(lambda: _job_phase(jobs, "start", jin, jout, jsem))
            pl.when(step == _mid_step(steps))(lambda: _job_phase(jobs, "mid", jin, jout, jsem))
        body(*ins, *outs, *scratch)
        if jobs:
            pl.when(step == steps - 1)(lambda: _job_phase(jobs, "end", jin, jout, jsem))

    return pl.pallas_call(
        wrapped, name=name, grid=(steps,),
        in_specs=list(in_specs) + [_ANY] * n_ji, out_specs=list(out_specs) + [_ANY] * n_jo,
        out_shape=list(out_shape) + j_outs, scratch_shapes=list(scratch_shapes) + j_sems,
        input_output_aliases={**(aliases or {}), **j_aliases},
        compiler_params=_params(("arbitrary",) if (jobs or sequential) else ("parallel",)))(*args, *j_ins)


def _matmul(name, ins, outs, grid, dims, acc_shape, epilogue=None, jobs=()):
    n_in, n_out, nk = len(ins), len(outs), grid[2]
    j_ins, j_outs, j_sems, aliases = _job_plumbing(jobs, n_in, n_out)
    n_ji, n_jo = len(j_ins), len(j_outs)
    total = grid[0] * grid[1] * grid[2]
    n_acc = 0 if nk == 1 else 1

    def body(*refs):
        a_ref, b_ref = refs[0], refs[1]
        extras = refs[2:n_in]
        jin = refs[n_in:n_in + n_ji]
        out_refs = refs[n_in + n_ji:n_in + n_ji + n_out]
        jout = refs[n_in + n_ji + n_out:n_in + n_ji + n_out + n_jo]
        jsem = refs[n_in + n_ji + n_out + n_jo + n_acc:]
        k = pl.program_id(2)
        step = (pl.program_id(0) * grid[1] + pl.program_id(1)) * grid[2] + k
        if jobs:
            pl.when(step == 0)(lambda: _job_phase(jobs, "start", jin, jout, jsem))
            pl.when(step == _mid_step(total))(lambda: _job_phase(jobs, "mid", jin, jout, jsem))

        def finish(r):
            res = epilogue(r, *[e[...] for e in extras]) if epilogue is not None else (r,)
            for o, v in zip(out_refs, res):
                o[...] = v.astype(o.dtype)

        if len(b_ref.shape) == 2:
            prod = lax.dot_general(a_ref[...], b_ref[...], dims, preferred_element_type=F32)
        else:
            kb = a_ref.shape[1] // b_ref.shape[0]
            prod = sum(lax.dot_general(a_ref[:, g * kb:(g + 1) * kb], b_ref[g], dims, preferred_element_type=F32)
                       for g in range(b_ref.shape[0]))
        if nk == 1:
            finish(prod)
        else:
            acc = refs[n_in + n_ji + n_out + n_jo]

            @pl.when(k == 0)
            def _():
                acc[...] = prod

            @pl.when((k > 0) & (k < nk - 1))
            def _():
                acc[...] += prod

            @pl.when(k == nk - 1)
            def _():
                finish(acc[...] + prod)

        if jobs:
            pl.when(step == total - 1)(lambda: _job_phase(jobs, "end", jin, jout, jsem))

    sem = ("arbitrary",) * 3 if jobs else ("parallel", "parallel", "arbitrary")
    res = pl.pallas_call(
        body, name=name, grid=grid,
        in_specs=[s for _, s in ins] + [_ANY] * n_ji,
        out_specs=[s for _, s in outs] + [_ANY] * n_jo,
        out_shape=[s for s, _ in outs] + j_outs,
        scratch_shapes=[pltpu.VMEM(acc_shape, F32)] * n_acc + j_sems,
        input_output_aliases=aliases,
        compiler_params=_params(sem),
    )(*[a for a, _ in ins], *j_ins)
    return res


def _mm_nn(name, a, b, out_dtype, *, a_seg=None, out_seg=None, tm=MM_TM, tn=MM_TN, tk=MM_TK, jobs=()):
    if a_seg:
        _, M, ks = a.shape
        K = ks * a_seg
    else:
        M, K = a.shape
        ks = K
    N = b.shape[1]
    ns = N // out_seg if out_seg else N
    tm, tn, tk = _tile(M, tm), _tile(ns, tn), _tile(ks, tk)
    nks, nns = ks // tk, ns // tn
    grid = (M // tm, N // tn, K // tk)
    if a_seg:
        a_spec = pl.BlockSpec((None, tm, tk), lambda i, j, k: (k // nks, i, k % nks))
    else:
        a_spec = pl.BlockSpec((tm, tk), lambda i, j, k: (i, k))
    b_spec = pl.BlockSpec((tk, tn), lambda i, j, k: (k, j))
    if out_seg:
        o_spec = pl.BlockSpec((None, tm, tn), lambda i, j, k: (j // nns, i, j % nns))
        o_shape = (out_seg, M, ns)
    else:
        o_spec = pl.BlockSpec((tm, tn), lambda i, j, k: (i, j))
        o_shape = (M, N)
    outs = [(jax.ShapeDtypeStruct(o_shape, out_dtype), o_spec)]
    return _matmul(name, [(a, a_spec), (b, b_spec)], outs, grid, DIMS_NN, (tm, tn), None, jobs)


def _mm_nt(name, a, b, out_dtype, *, a_seg=None, out_seg=None, tm=MM_TM, tn=MM_TN, tk=MM_TK, jobs=()):
    if a_seg:
        _, M, ks = a.shape
        K = ks * a_seg
    else:
        M, K = a.shape
        ks = K
    N = b.shape[0]
    ns = N // out_seg if out_seg else N
    tm, tn, tk = _tile(M, tm), _tile(ns, tn), _tile(ks, tk)
    nks, nns = ks // tk, ns // tn
    grid = (M // tm, N // tn, K // tk)
    if a_seg:
        a_spec = pl.BlockSpec((None, tm, tk), lambda i, j, k: (k // nks, i, k % nks))
    else:
        a_spec = pl.BlockSpec((tm, tk), lambda i, j, k: (i, k))
    b_spec = pl.BlockSpec((tn, tk), lambda i, j, k: (j, k))
    if out_seg:
        o_spec = pl.BlockSpec((None, tm, tn), lambda i, j, k: (j // nns, i, j % nns))
        o_shape = (out_seg, M, ns)
    else:
        o_spec = pl.BlockSpec((tm, tn), lambda i, j, k: (i, j))
        o_shape = (M, N)
    outs = [(jax.ShapeDtypeStruct(o_shape, out_dtype), o_spec)]
    return _matmul(name, [(a, a_spec), (b, b_spec)], outs, grid, DIMS_NT, (tm, tn), None, jobs)


def _mm_tn(name, a, b, out_dtype, *, a_seg=None, b_seg=None, out_blocks=None, tm=MM_TN, tn=MM_TN, tk=MM_TK_TOKENS,
           jobs=()):
    if a_seg:
        _, T, ms = a.shape
        M = ms * a_seg
    else:
        T, M = a.shape
        ms = M
    if b_seg:
        _, _, ns = b.shape
        N = ns * b_seg
    else:
        N = b.shape[1]
        ns = N
    nb_cols = N // out_blocks if out_blocks else N
    tm, tk = _tile(ms, tm), _tile(T, tk)
    tn = _tile(math.gcd(ns, nb_cols), tn)
    nms, nns, nbs = ms // tm, ns // tn, nb_cols // tn
    grid = (M // tm, N // tn, T // tk)
    if a_seg:
        a_spec = pl.BlockSpec((None, tk, tm), lambda i, j, k: (i // nms, k, i % nms))
    else:
        a_spec = pl.BlockSpec((tk, tm), lambda i, j, k: (k, i))
    if b_seg:
        b_spec = pl.BlockSpec((None, tk, tn), lambda i, j, k: (j // nns, k, j % nns))
    else:
        b_spec = pl.BlockSpec((tk, tn), lambda i, j, k: (k, j))
    if out_blocks:
        o_spec = pl.BlockSpec((None, tm, tn), lambda i, j, k: (j // nbs, i, j % nbs))
        o_shape = (out_blocks, M, nb_cols)
    else:
        o_spec = pl.BlockSpec((tm, tn), lambda i, j, k: (i, j))
        o_shape = (M, N)
    outs = [(jax.ShapeDtypeStruct(o_shape, out_dtype), o_spec)]
    return _matmul(name, [(a, a_spec), (b, b_spec)], outs, grid, DIMS_TN, (tm, tn), None, jobs)


def _rms(x):
    return lax.rsqrt(jnp.mean(x * x, axis=-1, keepdims=True) + NORM_EPS)


def _row_tile(T):
    return _tile(T, 256)


def _norm_fwd(name, x, g, jobs=()):
    T, D = x.shape
    tr = _row_tile(T)

    def body(x_ref, g_ref, h_ref):
        xv = x_ref[...]
        h_ref[...] = (xv * _rms(xv) * g_ref[...]).astype(h_ref.dtype)

    row = pl.BlockSpec((tr, D), lambda i: (i, 0))
    vec = pl.BlockSpec((1, D), lambda i: (0, 0))
    return _carry_call(body, name=name, steps=T // tr, in_specs=[row, vec], out_specs=[row],
                       out_shape=[jax.ShapeDtypeStruct((T, D), BF16)], scratch_shapes=[], args=(x, g), jobs=jobs)


def _resid_norm(name, x, br, g_post, g_next, jobs=()):
    T, D = x.shape
    tr = _row_tile(T)

    def body(x_ref, br_ref, gp_ref, gn_ref, xn_ref, h_ref):
        b = br_ref[...].astype(F32)
        xn = x_ref[...] + b * _rms(b) * gp_ref[...]
        xn_ref[...] = xn
        h_ref[...] = (xn * _rms(xn) * gn_ref[...]).astype(h_ref.dtype)

    row = pl.BlockSpec((tr, D), lambda i: (i, 0))
    vec = pl.BlockSpec((1, D), lambda i: (0, 0))
    return _carry_call(body, name=name, steps=T // tr, in_specs=[row, row, vec, vec], out_specs=[row, row],
                       out_shape=[jax.ShapeDtypeStruct((T, D), F32), jax.ShapeDtypeStruct((T, D), BF16)],
                       scratch_shapes=[], args=(x, br, g_post, g_next), jobs=jobs)


def _rms_bwd(xv, g, dyv):
    r = _rms(xv)
    xhat = xv * r
    gdy = dyv * g
    dx = r * (gdy - xhat * jnp.mean(gdy * xhat, axis=-1, keepdims=True))
    return dx, jnp.sum(dyv * xhat, axis=0, keepdims=True)


def _final_loss(name, x, br, g_post, target):
    T, D = x.shape
    tr = _row_tile(T)

    def body(x_ref, br_ref, gp_ref, t_ref, dy_ref, dbr_ref, dg_ref, ls_ref):
        b = br_ref[...].astype(F32)
        err = x_ref[...] + b * _rms(b) * gp_ref[...] - t_ref[...]
        dy = err * (1.0 / D)
        dy_ref[...] = dy
        dbr, dg = _rms_bwd(b, gp_ref[...], dy)
        dbr_ref[...] = dbr.astype(dbr_ref.dtype)

        @pl.when(pl.program_id(0) == 0)
        def _():
            ls_ref[...] = jnp.zeros_like(ls_ref)
            dg_ref[...] = jnp.zeros_like(dg_ref)

        ls_ref[...] += jnp.sum(err * err)
        dg_ref[...] += dg

    row = pl.BlockSpec((tr, D), lambda i: (i, 0))
    vec = pl.BlockSpec((1, D), lambda i: (0, 0))
    acc = pl.BlockSpec((SUBLANES, LANES), lambda i: (0, 0))
    return pl.pallas_call(body, name=name, grid=(T // tr,), in_specs=[row, row, vec, row],
                          out_specs=[row, row, vec, acc],
                          out_shape=[jax.ShapeDtypeStruct((T, D), F32), jax.ShapeDtypeStruct((T, D), BF16),
                                     jax.ShapeDtypeStruct((1, D), F32), jax.ShapeDtypeStruct((SUBLANES, LANES), F32)],
                          compiler_params=_params(("arbitrary",)))(x, br, g_post, target)


def _norm_bwd_pair(name, x, g_pre, dh, add, br, g_post, jobs=()):
    T, D = x.shape
    tr = _row_tile(T)

    def body(x_ref, gpre_ref, dh_ref, add_ref, br_ref, gpost_ref, dx_ref, dbr_ref, dgpre_ref, dgpost_ref):
        dx, dg_pre = _rms_bwd(x_ref[...], gpre_ref[...], dh_ref[...].astype(F32))
        dx = dx + add_ref[...]
        dx_ref[...] = dx
        dbr, dg_post = _rms_bwd(br_ref[...].astype(F32), gpost_ref[...], dx)
        dbr_ref[...] = dbr.astype(dbr_ref.dtype)

        @pl.when(pl.program_id(0) == 0)
        def _():
            dgpre_ref[...] = jnp.zeros_like(dgpre_ref)
            dgpost_ref[...] = jnp.zeros_like(dgpost_ref)

        dgpre_ref[...] += dg_pre
        dgpost_ref[...] += dg_post

    row = pl.BlockSpec((tr, D), lambda i: (i, 0))
    vec = pl.BlockSpec((1, D), lambda i: (0, 0))
    return _carry_call(body, name=name, steps=T // tr, in_specs=[row, vec, row, row, row, vec],
                       out_specs=[row, row, vec, vec],
                       out_shape=[jax.ShapeDtypeStruct((T, D), F32), jax.ShapeDtypeStruct((T, D), BF16),
                                  jax.ShapeDtypeStruct((1, D), F32), jax.ShapeDtypeStruct((1, D), F32)],
                       scratch_shapes=[], args=(x, g_pre, dh, add, br, g_post), jobs=jobs, sequential=True)


def _norm_bwd(name, x, g, dy, add, jobs=()):
    T, D = x.shape
    tr = _row_tile(T)

    def body(x_ref, g_ref, dy_ref, add_ref, dx_ref, dg_ref):
        dx, dg = _rms_bwd(x_ref[...], g_ref[...], dy_ref[...].astype(F32))
        dx_ref[...] = dx + add_ref[...]

        @pl.when(pl.program_id(0) == 0)
        def _():
            dg_ref[...] = jnp.zeros_like(dg_ref)

        dg_ref[...] += dg

    row = pl.BlockSpec((tr, D), lambda i: (i, 0))
    vec = pl.BlockSpec((1, D), lambda i: (0, 0))
    return _carry_call(body, name=name, steps=T // tr, in_specs=[row, vec, row, row], out_specs=[row, vec],
                       out_shape=[jax.ShapeDtypeStruct((T, D), F32), jax.ShapeDtypeStruct((1, D), F32)],
                       scratch_shapes=[], args=(x, g, dy, add), jobs=jobs, sequential=True)


HALO = SUBLANES
TIME_CHUNK = 512


def _chunks(T):
    tc = min(TIME_CHUNK, T)
    assert T % tc == 0 and tc % SUBLANES == 0
    return [(t0, tc) for t0 in range(0, T, tc)]


def _log_sigmoid(x):
    return -(jnp.maximum(-x, 0.0) + jnp.log(1.0 + jnp.exp(-jnp.abs(x))))


def _sigmoid(x):
    return 0.5 * jnp.tanh(0.5 * x) + 0.5


def _one_minus_exp(x):
    series = -x * (1.0 + x * (0.5 + x * (1.0 / 6.0 + x * (1.0 / 24.0))))
    return jnp.where(x > -0.01, series, 1.0 - jnp.exp(x))


_GELU_C = math.sqrt(2.0 / math.pi)


def _gelu(x):
    return 0.5 * x * (1.0 + jnp.tanh(_GELU_C * (x + 0.044715 * x * x * x)))


def _gelu_grad(x):
    th = jnp.tanh(_GELU_C * (x + 0.044715 * x * x * x))
    return 0.5 * (1.0 + th) + 0.5 * x * (1.0 - th * th) * _GELU_C * (1.0 + 3.0 * 0.044715 * x * x)


def _tile_scan(a, b, reverse):
    rows = a.shape[0]
    pos = lax.broadcasted_iota(jnp.int32, a.shape, 0) & (SUBLANES - 1)
    for d in (1, 2, 4):
        if reverse:
            ok = pos < SUBLANES - d
            shift = rows - d
        else:
            ok = pos >= d
            shift = d
        a_sh = jnp.where(ok, pltpu.roll(a, shift, 0), 1.0)
        b_sh = jnp.where(ok, pltpu.roll(b, shift, 0), 0.0)
        b = a * b_sh + b
        a = a * a_sh
    return a, b


def _carry_scan(a_s, b_s, T, reverse):
    n = T // SUBLANES
    edge = 0 if reverse else SUBLANES - 1

    def step(j, carry):
        g = (n - 1 - j) if reverse else j
        r = pl.multiple_of(g * SUBLANES, SUBLANES)
        h = b_s[pl.ds(r, SUBLANES), :] + a_s[pl.ds(r, SUBLANES), :] * carry
        b_s[pl.ds(r, SUBLANES), :] = h
        return jnp.broadcast_to(h[edge:edge + 1, :], h.shape)

    lax.fori_loop(0, n, step, jnp.zeros((SUBLANES, a_s.shape[1]), F32))


def _seg_spec(T, seg, nblk):
    return pl.BlockSpec((None, T, LANES), lambda c: (seg, 0, c))


def _rows_to_tile(rows):
    idx = lax.broadcasted_iota(jnp.int32, (SUBLANES, LANES), 0)
    out = jnp.zeros((SUBLANES, LANES), F32)
    for k, r in enumerate(rows):
        out = jnp.where(idx == k, r, out)
    return out


def _mixer_a_fwd(proj, conv_a):
    _, T, C = proj.shape
    nblk = C // LANES
    chunks = _chunks(T)

    def body(bg_ref, cg_ref, ax_ref, w_ref, y_ref, p_s):
        p_s[pl.ds(0, HALO), :] = jnp.zeros((HALO, LANES), F32)
        for t0, tc in chunks:
            p_s[pl.ds(HALO + t0, tc), :] = cg_ref[pl.ds(t0, tc), :] * ax_ref[pl.ds(t0, tc), :]
        w = w_ref[...]
        for t0, tc in chunks:
            c = (w[2:3, :] * p_s[pl.ds(HALO + t0, tc), :] + w[1:2, :] * p_s[pl.ds(HALO + t0 - 1, tc), :]
                 + w[0:1, :] * p_s[pl.ds(HALO + t0 - 2, tc), :])
            y_ref[pl.ds(t0, tc), :] = (bg_ref[pl.ds(t0, tc), :] * c).astype(y_ref.dtype)

    return pl.pallas_call(
        body, name="mixer_a_fwd", grid=(nblk,),
        in_specs=[_seg_spec(T, 0, nblk), _seg_spec(T, 1, nblk), _seg_spec(T, 2, nblk),
                  pl.BlockSpec((3, LANES), lambda c: (0, c))],
        out_specs=_seg_spec(T, 0, nblk),
        out_shape=jax.ShapeDtypeStruct((2, T, C), BF16),
        scratch_shapes=[pltpu.VMEM((T + HALO, LANES), F32)],
        compiler_params=_params(("parallel",)))(proj, proj, proj, conv_a)


def _mixer_a_bwd(proj, conv_a, dy):
    _, T, C = proj.shape
    nblk = C // LANES
    chunks = _chunks(T)

    def body(bg_ref, cg_ref, ax_ref, w_ref, dy_ref, dp_ref, dw_ref, p_s, dc_s):
        p_s[pl.ds(0, HALO), :] = jnp.zeros((HALO, LANES), F32)
        dc_s[pl.ds(T, HALO), :] = jnp.zeros((HALO, LANES), F32)
        for t0, tc in chunks:
            p_s[pl.ds(HALO + t0, tc), :] = cg_ref[pl.ds(t0, tc), :] * ax_ref[pl.ds(t0, tc), :]
        w = w_ref[...]
        for t0, tc in chunks:
            c = (w[2:3, :] * p_s[pl.ds(HALO + t0, tc), :] + w[1:2, :] * p_s[pl.ds(HALO + t0 - 1, tc), :]
                 + w[0:1, :] * p_s[pl.ds(HALO + t0 - 2, tc), :])
            dyv = dy_ref[pl.ds(t0, tc), :]
            dp_ref[0, pl.ds(t0, tc), :] = (dyv * c).astype(dp_ref.dtype)
            dc_s[pl.ds(t0, tc), :] = dyv * bg_ref[pl.ds(t0, tc), :]
        dw = [jnp.zeros((1, LANES), F32) for _ in range(3)]
        for t0, tc in chunks:
            dc = dc_s[pl.ds(t0, tc), :]
            dpv = w[2:3, :] * dc + w[1:2, :] * dc_s[pl.ds(t0 + 1, tc), :] + w[0:1, :] * dc_s[pl.ds(t0 + 2, tc), :]
            dp_ref[1, pl.ds(t0, tc), :] = (dpv * ax_ref[pl.ds(t0, tc), :]).astype(dp_ref.dtype)
            dp_ref[2, pl.ds(t0, tc), :] = (dpv * cg_ref[pl.ds(t0, tc), :]).astype(dp_ref.dtype)
            for k in range(3):
                dw[k] = dw[k] + jnp.sum(dc * p_s[pl.ds(HALO + t0 - (2 - k), tc), :], axis=0, keepdims=True)
        dw_ref[...] = _rows_to_tile(dw)

    return pl.pallas_call(
        body, name="mixer_a_bwd", grid=(nblk,),
        in_specs=[_seg_spec(T, 0, nblk), _seg_spec(T, 1, nblk), _seg_spec(T, 2, nblk),
                  pl.BlockSpec((3, LANES), lambda c: (0, c)), _seg_spec(T, 0, nblk)],
        out_specs=[pl.BlockSpec((3, T, LANES), lambda c: (0, 0, c)),
                   pl.BlockSpec((None, SUBLANES, LANES), lambda c: (c, 0, 0))],
        out_shape=[jax.ShapeDtypeStruct((6, T, C), BF16), jax.ShapeDtypeStruct((nblk, SUBLANES, LANES), F32)],
        scratch_shapes=[pltpu.VMEM((T + HALO, LANES), F32), pltpu.VMEM((T + HALO, LANES), F32)],
        compiler_params=_params(("parallel",)))(proj, proj, proj, conv_a, dy)


def _rg_gates(xr, wa, ba, wx, bx, ls):
    xb = xr.astype(BF16)
    r = _sigmoid(jnp.dot(xb, wa, preferred_element_type=F32) + ba)
    i = _sigmoid(jnp.dot(xb, wx, preferred_element_type=F32) + bx)
    log_a = LRU_C * r * ls
    a = jnp.exp(log_a)
    mult = jnp.sqrt(_one_minus_exp(2.0 * log_a))
    return r, i, a, mult


def _conv4(xh_s, cw, bias, t0, tc):
    return (cw[3:4, :] * xh_s[pl.ds(HALO + t0, tc), :] + cw[2:3, :] * xh_s[pl.ds(HALO + t0 - 1, tc), :]
            + cw[1:2, :] * xh_s[pl.ds(HALO + t0 - 2, tc), :] + cw[0:1, :] * xh_s[pl.ds(HALO + t0 - 3, tc), :] + bias)


def _mixer_b_specs(T, nblk):
    vec = pl.BlockSpec((1, LANES), lambda c: (0, c))
    mat = pl.BlockSpec((None, LANES, LANES), lambda c: (c, 0, 0))
    return [_seg_spec(T, 3, nblk), _seg_spec(T, 4, nblk), pl.BlockSpec((4, LANES), lambda c: (0, c)),
            vec, mat, vec, mat, vec, vec]


def _mixer_b_fwd(name, proj, conv_b, bias, wa, ba, wx, bx, lam, y, jobs=()):
    _, T, C = proj.shape
    nblk = C // LANES
    chunks = _chunks(T)

    def body(gate_ref, x_ref, cw_ref, cb_ref, wa_ref, ba_ref, wx_ref, bx_ref, lam_ref, y_in, y_ref, xh_s, a_s, b_s):
        xh_s[pl.ds(0, HALO), :] = jnp.zeros((HALO, LANES), F32)
        for t0, tc in chunks:
            xh_s[pl.ds(HALO + t0, tc), :] = x_ref[pl.ds(t0, tc), :]
        cw, bias_v = cw_ref[...], cb_ref[...]
        ls = _log_sigmoid(lam_ref[...])
        for t0, tc in chunks:
            xr = _conv4(xh_s, cw, bias_v, t0, tc)
            r, i, a, mult = _rg_gates(xr, wa_ref[...], ba_ref[...], wx_ref[...], bx_ref[...], ls)
            ac, hc = _tile_scan(a, mult * i * xr, reverse=False)
            a_s[pl.ds(t0, tc), :] = ac
            b_s[pl.ds(t0, tc), :] = hc
        _carry_scan(a_s, b_s, T, reverse=False)
        for t0, tc in chunks:
            y_ref[pl.ds(t0, tc), :] = (b_s[pl.ds(t0, tc), :] * _gelu(gate_ref[pl.ds(t0, tc), :])).astype(y_ref.dtype)

    return _carry_call(
        body, name=name, steps=nblk, in_specs=_mixer_b_specs(T, nblk) + [_ANY],
        out_specs=[_seg_spec(T, 1, nblk)], out_shape=[jax.ShapeDtypeStruct(y.shape, y.dtype)],
        scratch_shapes=[pltpu.VMEM((T + HALO, LANES), F32), pltpu.VMEM((T, LANES), F32), pltpu.VMEM((T, LANES), F32)],
        args=(proj, proj, conv_b, bias, wa, ba, wx, bx, lam, y), jobs=jobs, aliases={9: 0})


_ROW_CONV, _ROW_BIAS, _ROW_BA, _ROW_BX, _ROW_LAM = 0, 4, 5, 6, 7


def _mixer_b_bwd(name, proj, conv_b, bias, wa, ba, wx, bx, lam, dy, dproj, jobs=()):
    _, T, C = proj.shape
    nblk = C // LANES
    chunks = _chunks(T)

    def body(gate_ref, x_ref, cw_ref, cb_ref, wa_ref, ba_ref, wx_ref, bx_ref, lam_ref, dy_ref, dp_in,
             dp_ref, sm_ref, dwa_ref, dwx_ref, xh_s, xr_s, r_s, i_s, a_s, h_s, sa_s, sb_s, dx_s):
        zero_halo = jnp.zeros((HALO, LANES), F32)
        xh_s[pl.ds(0, HALO), :] = zero_halo
        h_s[pl.ds(0, HALO), :] = zero_halo
        a_s[pl.ds(T, HALO), :] = zero_halo
        dx_s[pl.ds(T, HALO), :] = zero_halo
        for t0, tc in chunks:
            xh_s[pl.ds(HALO + t0, tc), :] = x_ref[pl.ds(t0, tc), :]
        cw, bias_v = cw_ref[...], cb_ref[...]
        lam_v = lam_ref[...]
        ls = _log_sigmoid(lam_v)
        wa_v, wx_v, ba_v, bx_v = wa_ref[...], wx_ref[...], ba_ref[...], bx_ref[...]
        for t0, tc in chunks:
            xr = _conv4(xh_s, cw, bias_v, t0, tc)
            r, i, a, mult = _rg_gates(xr, wa_v, ba_v, wx_v, bx_v, ls)
            xr_s[pl.ds(t0, tc), :] = xr
            r_s[pl.ds(t0, tc), :] = r
            i_s[pl.ds(t0, tc), :] = i
            a_s[pl.ds(t0, tc), :] = a
            ac, hc = _tile_scan(a, mult * i * xr, reverse=False)
            sa_s[pl.ds(t0, tc), :] = ac
            sb_s[pl.ds(t0, tc), :] = hc
        _carry_scan(sa_s, sb_s, T, reverse=False)
        for t0, tc in chunks:
            h_s[pl.ds(HALO + t0, tc), :] = sb_s[pl.ds(t0, tc), :]
        for t0, tc in chunks:
            gv = gate_ref[pl.ds(t0, tc), :]
            dyv = dy_ref[pl.ds(t0, tc), :]
            dp_ref[0, pl.ds(t0, tc), :] = (dyv * h_s[pl.ds(HALO + t0, tc), :] * _gelu_grad(gv)).astype(dp_ref.dtype)
            ac, gc = _tile_scan(a_s[pl.ds(t0 + 1, tc), :], dyv * _gelu(gv), reverse=True)
            sa_s[pl.ds(t0, tc), :] = ac
            sb_s[pl.ds(t0, tc), :] = gc
        _carry_scan(sa_s, sb_s, T, reverse=True)
        acc = {k: jnp.zeros((1, LANES), F32) for k in ("bias", "ba", "bx", "lam")}
        dwa = jnp.zeros((LANES, LANES), F32)
        dwx = jnp.zeros((LANES, LANES), F32)
        for t0, tc in chunks:
            dht = sb_s[pl.ds(t0, tc), :]
            xr, r, i, a = xr_s[pl.ds(t0, tc), :], r_s[pl.ds(t0, tc), :], i_s[pl.ds(t0, tc), :], a_s[pl.ds(t0, tc), :]
            mult = jnp.sqrt(_one_minus_exp(2.0 * LRU_C * r * ls))
            da = dht * h_s[pl.ds(HALO + t0 - 1, tc), :]
            dmult = dht * i * xr
            di = dht * mult * xr
            dlog_a = da * a - dmult * a * a / mult
            dpa = dlog_a * (LRU_C * ls) * r * (1.0 - r)
            dpx = di * i * (1.0 - i)
            acc["lam"] = acc["lam"] + jnp.sum(dlog_a * r, axis=0, keepdims=True)
            acc["ba"] = acc["ba"] + jnp.sum(dpa, axis=0, keepdims=True)
            acc["bx"] = acc["bx"] + jnp.sum(dpx, axis=0, keepdims=True)
            xb, dpab, dpxb = xr.astype(BF16), dpa.astype(BF16), dpx.astype(BF16)
            dwa = dwa + lax.dot_general(xb, dpab, DIMS_TN, preferred_element_type=F32)
            dwx = dwx + lax.dot_general(xb, dpxb, DIMS_TN, preferred_element_type=F32)
            dxr = (dht * mult * i + lax.dot_general(dpab, wa_v, DIMS_NT, preferred_element_type=F32)
                   + lax.dot_general(dpxb, wx_v, DIMS_NT, preferred_element_type=F32))
            acc["bias"] = acc["bias"] + jnp.sum(dxr, axis=0, keepdims=True)
            dx_s[pl.ds(t0, tc), :] = dxr
        dcw = [jnp.zeros((1, LANES), F32) for _ in range(4)]
        for t0, tc in chunks:
            dxr = dx_s[pl.ds(t0, tc), :]
            dxin = (cw[3:4, :] * dxr + cw[2:3, :] * dx_s[pl.ds(t0 + 1, tc), :] + cw[1:2, :] * dx_s[pl.ds(t0 + 2, tc), :]
                    + cw[0:1, :] * dx_s[pl.ds(t0 + 3, tc), :])
            dp_ref[1, pl.ds(t0, tc), :] = dxin.astype(dp_ref.dtype)
            for k in range(4):
                dcw[k] = dcw[k] + jnp.sum(dxr * xh_s[pl.ds(HALO + t0 - (3 - k), tc), :], axis=0, keepdims=True)
        dlam = acc["lam"] * LRU_C * _sigmoid(-lam_v)
        sm_ref[...] = _rows_to_tile(dcw + [acc["bias"], acc["ba"], acc["bx"], dlam])
        dwa_ref[...] = dwa
        dwx_ref[...] = dwx

    big = lambda halo: pltpu.VMEM((T + halo, LANES), F32)
    mat = pl.BlockSpec((None, LANES, LANES), lambda c: (c, 0, 0))
    return _carry_call(
        body, name=name, steps=nblk,
        in_specs=_mixer_b_specs(T, nblk) + [_seg_spec(T, 1, nblk), _ANY],
        out_specs=[pl.BlockSpec((3, T, LANES), lambda c: (1, 0, c)),
                   pl.BlockSpec((None, SUBLANES, LANES), lambda c: (c, 0, 0)), mat, mat],
        out_shape=[jax.ShapeDtypeStruct(dproj.shape, dproj.dtype), jax.ShapeDtypeStruct((nblk, SUBLANES, LANES), F32),
                   jax.ShapeDtypeStruct((nblk, LANES, LANES), F32), jax.ShapeDtypeStruct((nblk, LANES, LANES), F32)],
        scratch_shapes=[big(HALO), big(0), big(0), big(0), big(HALO), big(HALO), big(0), big(0), big(HALO)],
        args=(proj, proj, conv_b, bias, wa, ba, wx, bx, lam, dy, dproj), jobs=jobs, aliases={10: 0})


ATT_BLOCK = 128
ATT_GROUP = 3
ATT_TILE = ATT_BLOCK * ATT_GROUP
ATT_UNDERFLOW = -110.0
ATT_UNVISITED = -1e30


def _split_dot(x, m):
    hi = x.astype(BF16)
    lo = (x - hi.astype(F32)).astype(BF16)
    return jnp.dot(hi, m, preferred_element_type=F32) + jnp.dot(lo, m, preferred_element_type=F32)


def _sub(x, j):
    return x[:, j * ATT_BLOCK:(j + 1) * ATT_BLOCK]


def _stack_rows(x):
    return jnp.concatenate([_sub(x, j) for j in range(ATT_GROUP)], axis=0)


def _unstack_rows(x, offsets):
    return jnp.concatenate([x[j * ATT_BLOCK:(j + 1) * ATT_BLOCK, :] + offsets[j] for j in range(ATT_GROUP)], axis=1)


def _att_tile(q, k_ref, q0, qb, it, scale):
    hi = (qb + 1 - ATT_GROUP * it) * ATT_BLOCK
    k0 = pl.multiple_of(jnp.maximum(hi - ATT_TILE, 0), ATT_BLOCK)
    kt = k_ref[pl.ds(k0, ATT_TILE), :]
    z = lax.dot_general(q, kt, DIMS_NT, preferred_element_type=F32) * scale
    key = k0 + lax.broadcasted_iota(jnp.int32, z.shape, 1)
    row = q0 + lax.broadcasted_iota(jnp.int32, z.shape, 0)
    mask = (key < row) & (key < hi)
    n = jnp.where(mask, -(jnp.maximum(z, 0.0) + jnp.log(1.0 + jnp.exp(-jnp.abs(z)))), 0.0)
    return k0, kt, z, mask, n


def _suffix_in_tile(n, upper, run):
    rs = [jnp.sum(_sub(n, j), axis=1, keepdims=True) for j in range(ATT_GROUP)]
    offs = [None] * ATT_GROUP
    offs[ATT_GROUP - 1] = run
    for j in range(ATT_GROUP - 2, -1, -1):
        offs[j] = offs[j + 1] + rs[j + 1]
    return _unstack_rows(_split_dot(_stack_rows(n), upper), offs), offs[0] + rs[0]


def _head_spec(T, seg, heads):
    return pl.BlockSpec((None, T, ATT_HEAD_DIM), lambda h: (seg, 0, h))


def _attention_fwd(name, qkv, jobs=()):
    _, T, D = qkv.shape
    heads = D // ATT_HEAD_DIM
    nq = T // ATT_BLOCK
    assert nq <= LANES and T >= ATT_TILE and nq % 2 == 0
    scale = 1.0 / math.sqrt(ATT_HEAD_DIM)

    def body(q_ref, k_ref, v_ref, o_ref, r_ref, acc_s, run_s):
        rr = lax.broadcasted_iota(jnp.int32, (ATT_BLOCK, ATT_BLOCK), 0)
        cc = lax.broadcasted_iota(jnp.int32, (ATT_BLOCK, ATT_BLOCK), 1)
        upper = jnp.where(rr > cc, 1.0, 0.0).astype(BF16)
        lane = lax.broadcasted_iota(jnp.int32, (ATT_BLOCK, LANES), 1)

        def tile(slot, qb, q0, q, it, first):
            k0, _, z, mask, n = _att_tile(q, k_ref, q0, qb, it, scale)
            run = jnp.zeros((ATT_BLOCK, LANES), F32) if first else run_s[slot]
            suffix, run_next = _suffix_in_tile(n, upper, run)
            w = jnp.where(mask, jnp.exp(z + n + suffix), 0.0)
            pv = jnp.dot(w.astype(BF16), v_ref[pl.ds(k0, ATT_TILE), :], preferred_element_type=F32)
            if first:
                acc_s[slot] = pv
            else:
                acc_s[slot] += pv
                r_ref[pl.ds(q0, ATT_BLOCK), :] = jnp.where(lane == it, run, r_ref[pl.ds(q0, ATT_BLOCK), :])
            run_s[slot] = run_next
            return jnp.max(run_next) >= ATT_UNDERFLOW

        def pair_loop(p, _):
            blocks = []
            for slot in range(2):
                qb = 2 * p + slot
                q0 = pl.multiple_of(qb * ATT_BLOCK, ATT_BLOCK)
                r_ref[pl.ds(q0, ATT_BLOCK), :] = jnp.where(lane == 0, 0.0, ATT_UNVISITED)
                blocks.append((qb, q0, q_ref[pl.ds(q0, ATT_BLOCK), :]))
            go = [tile(slot, *blocks[slot], 0, True) for slot in range(2)]
            for slot in range(2):
                qb, q0, q = blocks[slot]
                n_tiles = (qb + ATT_GROUP) // ATT_GROUP
                lax.while_loop(lambda c: (c[0] < n_tiles) & c[1],
                               lambda c: (c[0] + 1, tile(slot, qb, q0, q, c[0], False)), (jnp.int32(1), go[slot]))
                o_ref[pl.ds(q0, ATT_BLOCK), :] = acc_s[slot].astype(o_ref.dtype)
            return 0

        lax.fori_loop(0, nq // 2, pair_loop, 0)

    return _carry_call(
        body, name=name, steps=heads,
        in_specs=[_head_spec(T, 0, heads), _head_spec(T, 1, heads), _head_spec(T, 2, heads)],
        out_specs=[pl.BlockSpec((T, ATT_HEAD_DIM), lambda h: (0, h)), pl.BlockSpec((None, T, LANES), lambda h: (h, 0, 0))],
        out_shape=[jax.ShapeDtypeStruct((T, D), BF16), jax.ShapeDtypeStruct((heads, T, LANES), F32)],
        scratch_shapes=[pltpu.VMEM((2, ATT_BLOCK, ATT_HEAD_DIM), F32), pltpu.VMEM((2, ATT_BLOCK, LANES), F32)],
        args=(qkv, qkv, qkv), jobs=jobs)


def _attention_bwd(name, qkv, do, rmat, jobs=()):
    _, T, D = qkv.shape
    heads = D // ATT_HEAD_DIM
    nq = T // ATT_BLOCK
    scale = 1.0 / math.sqrt(ATT_HEAD_DIM)

    def body(q_ref, k_ref, v_ref, do_ref, r_ref, dqkv_ref, dk_s, dv_s, dq_s, left_s):
        rr = lax.broadcasted_iota(jnp.int32, (ATT_BLOCK, ATT_BLOCK), 0)
        cc = lax.broadcasted_iota(jnp.int32, (ATT_BLOCK, ATT_BLOCK), 1)
        upper = jnp.where(rr > cc, 1.0, 0.0).astype(BF16)
        lower = jnp.where(rr < cc, 1.0, 0.0).astype(BF16)
        lane = lax.broadcasted_iota(jnp.int32, (ATT_BLOCK, LANES), 1)
        dk_s[...] = jnp.zeros_like(dk_s)
        dv_s[...] = jnp.zeros_like(dv_s)

        def tile(slot, qb, q0, q, dov, it, first):
            k0, kt, z, mask, n = _att_tile(q, k_ref, q0, qb, it, scale)
            vt = v_ref[pl.ds(k0, ATT_TILE), :]
            run = jnp.sum(jnp.where(lane == it, r_ref[pl.ds(q0, ATT_BLOCK), :], 0.0), axis=1, keepdims=True)
            suffix, _ = _suffix_in_tile(n, upper, run)
            s = z + n
            w = jnp.where(mask, jnp.exp(s + suffix), 0.0)
            e = w * lax.dot_general(dov, vt, DIMS_NT, preferred_element_type=F32)
            es = [jnp.sum(_sub(e, g), axis=1, keepdims=True) for g in range(ATT_GROUP)]
            pre = [jnp.zeros((ATT_BLOCK, LANES), F32) if first else left_s[slot]]
            for g in range(ATT_GROUP):
                pre.append(pre[g] + es[g])
            before = _unstack_rows(_split_dot(_stack_rows(e), lower), pre)
            sig = jnp.exp(s)
            dz = (jnp.where(mask, e * (1.0 - sig) - before * sig, 0.0) * scale).astype(BF16)
            dq = jnp.dot(dz, kt, preferred_element_type=F32)
            if first:
                dq_s[slot] = dq
            else:
                dq_s[slot] += dq
            dk_s[pl.ds(k0, ATT_TILE), :] += lax.dot_general(dz, q, DIMS_TN, preferred_element_type=F32)
            dv_s[pl.ds(k0, ATT_TILE), :] += lax.dot_general(w.astype(BF16), dov, DIMS_TN, preferred_element_type=F32)
            left_s[slot] = pre[ATT_GROUP]

        def pair_loop(p, _):
            blocks = []
            for slot in range(2):
                qb = 2 * p + slot
                q0 = pl.multiple_of(qb * ATT_BLOCK, ATT_BLOCK)
                n_tiles = (qb + ATT_GROUP) // ATT_GROUP
                seen = ((jnp.max(r_ref[pl.ds(q0, ATT_BLOCK), :], axis=0, keepdims=True) > 0.5 * ATT_UNVISITED)
                        & (lane[0:1, :] < n_tiles))
                n_visited = jnp.maximum(jnp.sum(jnp.where(seen, 1.0, 0.0)).astype(jnp.int32), 1)
                blocks.append((qb, q0, q_ref[pl.ds(q0, ATT_BLOCK), :], do_ref[pl.ds(q0, ATT_BLOCK), :], n_visited))
            for slot in range(2):
                qb, q0, q, dov, n_visited = blocks[slot]
                tile(slot, qb, q0, q, dov, n_visited - 1, True)
            for slot in range(2):
                qb, q0, q, dov, n_visited = blocks[slot]
                lax.fori_loop(1, n_visited, lambda j, c: (tile(slot, qb, q0, q, dov, n_visited - 1 - j, False), c)[1], 0)
                dqkv_ref[0, pl.ds(q0, ATT_BLOCK), :] = dq_s[slot].astype(dqkv_ref.dtype)
            return 0

        lax.fori_loop(0, nq // 2, pair_loop, 0)
        dqkv_ref[1, :, :] = dk_s[...].astype(dqkv_ref.dtype)
        dqkv_ref[2, :, :] = dv_s[...].astype(dqkv_ref.dtype)

    return _carry_call(
        body, name=name, steps=heads,
        in_specs=[_head_spec(T, 0, heads), _head_spec(T, 1, heads), _head_spec(T, 2, heads),
                  pl.BlockSpec((T, ATT_HEAD_DIM), lambda h: (0, h)), pl.BlockSpec((None, T, LANES), lambda h: (h, 0, 0))],
        out_specs=[pl.BlockSpec((3, T, ATT_HEAD_DIM), lambda h: (0, 0, h))],
        out_shape=[jax.ShapeDtypeStruct((3, T, D), BF16)],
        scratch_shapes=[pltpu.VMEM((T, ATT_HEAD_DIM), F32), pltpu.VMEM((T, ATT_HEAD_DIM), F32),
                        pltpu.VMEM((2, ATT_BLOCK, ATT_HEAD_DIM), F32), pltpu.VMEM((2, ATT_BLOCK, LANES), F32)],
        args=(qkv, qkv, qkv, do, rmat), jobs=jobs)


def _block_diag_pairs(w):
    h = w.shape[0]
    wp = w.reshape(h // 2, 2, RG_HEAD_DIM, RG_HEAD_DIM)
    z = jnp.zeros_like(wp[:, 0])
    top = jnp.concatenate([wp[:, 0], z], axis=2)
    bot = jnp.concatenate([z, wp[:, 1]], axis=2)
    return jnp.concatenate([top, bot], axis=1)


def _diag_pairs(g):
    n = g.shape[0]
    a = g[:, :RG_HEAD_DIM, :RG_HEAD_DIM]
    b = g[:, RG_HEAD_DIM:, RG_HEAD_DIM:]
    return jnp.stack([a, b], axis=1).reshape(2 * n, RG_HEAD_DIM, RG_HEAD_DIM)


class _Weights:
    def __init__(self, full, shards=None, plan=None):
        self.full, self.shards, self.plan = dict(full), shards or {}, plan or {}
        self.partial, self.rows = {}, {}

    def __getitem__(self, name):
        return self.full[name]

    def jobs(self, call):
        return [_gather_job(self.shards[n], self.partial.get(n), lo, hi, parts)
                for n, lo, hi, parts in self.plan.get(call, ())]

    def deliver(self, call, outs):
        for (n, lo, hi, parts), g in zip(self.plan.get(call, ()), outs):
            self.partial[n] = g
            self.rows[n] = self.rows.get(n, 0) + hi - lo
            if self.rows[n] == parts:
                self.full[n] = _gathered_layout(n, g)


def _gathered_layout(name, g):
    if name in ("w_in", "w_qkv", "w_out", "w_o"):
        return g.reshape(g.shape[0] * g.shape[1], g.shape[2])
    return g


class _Grads:
    def __init__(self, lands=None, plan=None):
        self.lands, self.plan = dict(lands) if lands else None, plan or {}
        self.ready, self.sent = {}, {}

    def put(self, name, arr):
        self.ready[name] = arr

    def jobs(self, call):
        if self.lands is None:
            return []
        return [_exchange_job(self.ready[n], self.lands[n], lo, hi, parts) for n, lo, hi, parts in self.plan.get(call, ())]

    def deliver(self, call, outs):
        for (n, lo, hi, parts), o in zip(self.plan.get(call, ()), outs):
            assert self.sent.get(n, (0, parts)) == (lo, parts), (call, n)
            self.lands[n] = o
            self.sent[n] = (hi, parts)

    def flush(self, name):
        if self.lands is None:
            return
        rest = []
        for n in self.ready:
            lo, parts = self.sent.get(n, (0, 1))
            if lo < parts:
                rest.append((n, lo, parts, parts))
        if rest:
            outs = _run_jobs(name, [_exchange_job(self.ready[n], self.lands[n], lo, hi, parts) for n, lo, hi, parts in rest])
            for (n, _, hi, parts), o in zip(rest, outs):
                self.lands[n] = o
                self.sent[n] = (hi, parts)


def _mlp_fwd(tag, h, wts, run):
    T, D = h.shape
    w_up = wts["up" + tag]
    fb = w_up.shape[2]
    F = fb * N_DEV
    tm, tn, tk = _tile(T, MM_TM), _tile(fb, MM_TN), _tile(D, MM_TK)
    nb = fb // tn

    def up_epilogue(u):
        r = jnp.maximum(u, 0.0)
        return u, r * r

    o_spec = pl.BlockSpec((tm, tn), lambda i, j, k: (i, j))
    u, act = run(
        _matmul, f"mlp_up_l{tag}",
        [(h, pl.BlockSpec((tm, tk), lambda i, j, k: (i, k))),
         (w_up, pl.BlockSpec((None, tk, tn), lambda i, j, k: (j // nb, k, j % nb)))],
        [(jax.ShapeDtypeStruct((T, F), BF16), o_spec), (jax.ShapeDtypeStruct((T, F), BF16), o_spec)],
        (T // tm, F // tn, D // tk), DIMS_NN, (tm, tn), up_epilogue, n_main=2)
    w_down = wts["down" + tag].reshape(F, D)
    m = run(_mm_nn, f"mlp_down_l{tag}", act, w_down, BF16)
    return u, act, m


def _mlp_bwd(tag, h, u, act, dm, wts, grads, run):
    T, D = h.shape
    w_up, w_down = wts["up" + tag], wts["down" + tag]
    fb = w_up.shape[2]
    F = fb * N_DEV
    grads.put("down" + tag, run(_mm_tn, f"mlp_down_dw_l{tag}", act, dm, BF16).reshape(N_DEV, fb, D))
    tm, tn, tk = _tile(T, MM_TM), _tile(fb, MM_TN), _tile(D, MM_TK)
    nb = fb // tn
    o_spec = pl.BlockSpec((tm, tn), lambda i, j, k: (i, j))
    du = run(
        _matmul, f"mlp_down_dx_l{tag}",
        [(dm, pl.BlockSpec((tm, tk), lambda i, j, k: (i, k))),
         (w_down, pl.BlockSpec((None, tn, tk), lambda i, j, k: (j // nb, j % nb, k))),
         (u, o_spec)],
        [(jax.ShapeDtypeStruct((T, F), BF16), o_spec)],
        (T // tm, F // tn, D // tk), DIMS_NT, (tm, tn),
        lambda r, uv: (r * (2.0 * jnp.maximum(uv.astype(F32), 0.0)),))
    grads.put("up" + tag, run(_mm_tn, f"mlp_up_dw_l{tag}", h, du, BF16, out_blocks=N_DEV))
    tn2 = _tile(D, MM_TN)
    pair = 2 if MM_TK >= 2 * fb else 1
    return run(
        _matmul, f"mlp_up_dx_l{tag}",
        [(du, pl.BlockSpec((tm, pair * fb), lambda i, j, k: (i, k))),
         (w_up, pl.BlockSpec((pair, tn2, fb), lambda i, j, k: (k, j, 0)))],
        [(jax.ShapeDtypeStruct((T, D), BF16), pl.BlockSpec((tm, tn2), lambda i, j, k: (i, j)))],
        (T // tm, D // tn2, N_DEV // pair), DIMS_NT, (tm, tn2), None)


def _local_step(x, target, gains, conv_a, conv_b, conv_b_bias, rg_w_a, rg_b_a, rg_w_x, rg_b_x, rg_lambda, wts, grads):
    T, D = x.shape
    g = lambda l, i: gains[l, i][None, :]
    wa_p = _block_diag_pairs(rg_w_a).astype(BF16)
    wx_p = _block_diag_pairs(rg_w_x).astype(BF16)

    def run(fn, name, *args, n_main=1, **kw):
        jw, jg = wts.jobs(name), grads.jobs(name)
        res = fn(name, *args, jobs=jw + jg, **kw)
        main, jo = res[:n_main], res[n_main:]
        wts.deliver(name, jo[:len(jw)])
        grads.deliver(name, jo[len(jw):])
        return main[0] if n_main == 1 else main

    h0 = run(_norm_fwd, "norm_in", x, g(0, 0))
    proj = run(_mm_nt, "w_in_fwd", h0, wts["w_in"], F32, out_seg=5)
    y = run(_mixer_b_fwd, "mixer_b_fwd", proj, conv_b, conv_b_bias, wa_p, rg_b_a, wx_p, rg_b_x, rg_lambda,
            _mixer_a_fwd(proj, conv_a))
    mix0 = run(_mm_nn, "w_out_fwd", y, wts["w_out"], BF16, a_seg=2)
    x1, h1 = run(_resid_norm, "resid_mix0", x, mix0, g(0, 1), g(0, 2), n_main=2)
    u0, act0, m0 = _mlp_fwd("0", h1, wts, run)
    x2, h2 = run(_resid_norm, "resid_mlp0", x1, m0, g(0, 3), g(1, 0), n_main=2)
    qkv = run(_mm_nt, "w_qkv_fwd", h2, wts["w_qkv"], BF16, out_seg=3)
    o, rmat = run(_attention_fwd, "attention_fwd", qkv, n_main=2)
    mix1 = run(_mm_nn, "w_o_fwd", o, wts["w_o"], BF16)
    x3, h3 = run(_resid_norm, "resid_mix1", x2, mix1, g(1, 1), g(1, 2), n_main=2)
    u1, act1, m1 = _mlp_fwd("1", h3, wts, run)
    dx4, dm1, dg13, sq = _final_loss("loss", x3, m1, g(1, 3), target)

    dh3 = _mlp_bwd("1", h3, u1, act1, dm1, wts, grads, run)
    dx3, dmix1, dg12, dg11 = run(_norm_bwd_pair, "norm_bwd_x3", x3, g(1, 2), dh3, dx4, mix1, g(1, 1), n_main=4)
    grads.put("w_o", run(_mm_tn, "w_o_dw", o, dmix1, BF16).reshape(N_DEV, D // N_DEV, D))
    do = run(_mm_nt, "w_o_dx", dmix1, wts["w_o"], BF16)
    dqkv = run(_attention_bwd, "attention_bwd", qkv, do, rmat)
    grads.put("w_qkv", run(_mm_tn, "w_qkv_dw", dqkv, h2, BF16, a_seg=3).reshape(N_DEV, 3 * D // N_DEV, D))
    dh2 = run(_mm_nn, "w_qkv_dx", dqkv, wts["w_qkv"], BF16, a_seg=3)
    dx2, dm0, dg10, dg03 = run(_norm_bwd_pair, "norm_bwd_x2", x2, g(1, 0), dh2, dx3, m0, g(0, 3), n_main=4)
    dh1 = _mlp_bwd("0", h1, u0, act0, dm0, wts, grads, run)
    dx1, dmix0, dg02, dg01 = run(_norm_bwd_pair, "norm_bwd_x1", x1, g(0, 2), dh1, dx2, mix0, g(0, 1), n_main=4)
    grads.put("w_out", run(_mm_tn, "w_out_dw", y, dmix0, BF16, a_seg=2).reshape(N_DEV, D // N_DEV, D))
    dy = run(_mm_nt, "w_out_dx", dmix0, wts["w_out"], F32, out_seg=2)
    dproj_a, dconv_a = _mixer_a_bwd(proj, conv_a, dy)
    dproj, sm_b, dwa_p, dwx_p = run(_mixer_b_bwd, "mixer_b_bwd", proj, conv_b, conv_b_bias, wa_p, rg_b_a, wx_p, rg_b_x,
                                    rg_lambda, dy, dproj_a, n_main=4)
    grads.put("w_in", run(_mm_tn, "w_in_dw", dproj, h0, BF16, a_seg=5).reshape(N_DEV, 5 * D // (2 * N_DEV), D))
    dh0 = run(_mm_nn, "w_in_dx", dproj, wts["w_in"], BF16, a_seg=5)
    dx0, dg00 = run(_norm_bwd, "norm_bwd_x0", x, g(0, 0), dh0, dx1, n_main=2)

    C = D // 2
    lanes_to_vec = lambda t, row: t[:, row, :].reshape(1, C)
    small = {
        "norm_gains": jnp.concatenate([dg00, dg01, dg02, dg03, dg10, dg11, dg12, dg13], axis=0).reshape(2, 4, D),
        "conv_a": jnp.transpose(dconv_a[:, :3, :], (1, 0, 2)).reshape(3, C),
        "conv_b": jnp.transpose(sm_b[:, :4, :], (1, 0, 2)).reshape(4, C),
        "conv_b_bias": lanes_to_vec(sm_b, _ROW_BIAS),
        "rg_w_a": _diag_pairs(dwa_p),
        "rg_b_a": lanes_to_vec(sm_b, _ROW_BA),
        "rg_w_x": _diag_pairs(dwx_p),
        "rg_b_x": lanes_to_vec(sm_b, _ROW_BX),
        "rg_lambda": lanes_to_vec(sm_b, _ROW_LAM),
    }
    return sq[0, 0], dx0, small


def _my_index():
    return 4 * lax.axis_index("x") + 2 * lax.axis_index("y") + lax.axis_index("c")


def _peers():
    x, y, c = lax.axis_index("x"), lax.axis_index("y"), lax.axis_index("c")
    out = []
    for k in range(1, N_DEV):
        px = x ^ ((k >> 2) & 1)
        py = y ^ ((k >> 1) & 1)
        pc = c ^ (k & 1)
        out.append(((px, py, pc), 4 * px + 2 * py + pc))
    return out


GATHER_PLAN = {
    "norm_in": (("w_in", 0, 1, 1),),
    "w_in_fwd": (("w_out", 0, 1, 1), ("up0", 0, 1, 4)),
    "mixer_b_fwd": (("up0", 1, 3, 4),),
    "w_out_fwd": (("up0", 3, 4, 4),),
    "resid_mix0": (("down0", 0, 1, 4),),
    "mlp_up_l0": (("down0", 1, 4, 4),),
    "mlp_down_l0": (("w_qkv", 0, 1, 1),),
    "resid_mlp0": (("up1", 0, 1, 4),),
    "w_qkv_fwd": (("w_o", 0, 1, 1), ("up1", 1, 2, 4)),
    "attention_fwd": (("up1", 2, 4, 4), ("down1", 0, 2, 4)),
    "mlp_up_l1": (("down1", 2, 4, 4),),
}
EXCHANGE_PLAN = {
    "mlp_down_dx_l1": (("down1", 0, 3, 8),), "mlp_up_dw_l1": (("down1", 3, 6, 8),),
    "mlp_up_dx_l1": (("down1", 6, 8, 8), ("up1", 0, 1, 8)),
    "attention_bwd": (("up1", 1, 8, 8), ("w_o", 0, 1, 1)),
    "w_qkv_dx": (("w_qkv", 0, 2, 6),), "norm_bwd_x2": (("w_qkv", 2, 3, 6),), "mlp_down_dw_l0": (("w_qkv", 3, 6, 6),),
    "mlp_down_dx_l0": (("down0", 0, 3, 8),), "mlp_up_dw_l0": (("down0", 3, 6, 8),),
    "mlp_up_dx_l0": (("down0", 6, 8, 8), ("up0", 0, 1, 8)),
    "norm_bwd_x1": (("up0", 1, 2, 8),), "w_out_dw": (("up0", 2, 3, 8),), "w_out_dx": (("w_out", 0, 1, 2),),
    "mixer_b_bwd": (("up0", 3, 7, 8),),
    "w_in_dw": (("up0", 7, 8, 8), ("w_out", 1, 2, 2)),
    "w_in_dx": (("w_in", 0, 2, 4),),
    "norm_bwd_x0": (("w_in", 2, 3, 4),),
    "adamw_mlp_w_down": (("w_in", 3, 4, 4),),
}


def _job_sems():
    return [pltpu.SemaphoreType.DMA((N_DEV - 1,)), pltpu.SemaphoreType.DMA((N_DEV - 1,)), pltpu.SemaphoreType.DMA((1,))]


def _gather_job(shard, prev=None, lo=0, hi=1, parts=1):
    n = shard.shape[0] // parts
    assert n * parts == shard.shape[0]
    rows = pl.ds(lo * n, (hi - lo) * n)

    def ctx():
        x, y, c = lax.axis_index("x"), lax.axis_index("y"), lax.axis_index("c")
        chips = [(1 - x, y), (x, 1 - y), (1 - x, 1 - y)]
        return x, y, c, chips

    def idx(px, py, pc):
        return 4 * px + 2 * py + pc

    def copy(src, out, sems, k, block, to):
        return pltpu.make_async_remote_copy(
            src_ref=out.at[block, rows] if src is None else src.at[rows], dst_ref=out.at[block, rows],
            send_sem=sems[0].at[k], recv_sem=sems[1].at[k], device_id=to, device_id_type=MESH)

    def start(ins, outs, sems):
        x, y, c, chips = ctx()
        src, out = ins[0], outs[0]
        me = idx(x, y, c)
        pltpu.make_async_copy(src.at[rows], out.at[me, rows], sems[2].at[0]).start()
        copy(src, out, sems, 0, me, (x, y, 1 - c)).start()
        for j, (px, py) in enumerate(chips):
            copy(src, out, sems, 1 + j, me, (px, py, c)).start()

    def mid(ins, outs, sems):
        x, y, c, chips = ctx()
        out = outs[0]
        for j, (px, py) in enumerate(chips):
            copy(None, out, sems, 1 + j, idx(px, py, c), (x, y, c)).wait_recv()
            copy(None, out, sems, 4 + j, idx(px, py, c), (x, y, 1 - c)).start()

    def end(ins, outs, sems):
        x, y, c, chips = ctx()
        src, out = ins[0], outs[0]
        me = (x, y, c)
        copy(None, out, sems, 0, idx(x, y, 1 - c), me).wait_recv()
        for j, (px, py) in enumerate(chips):
            copy(None, out, sems, 4 + j, idx(px, py, 1 - c), me).wait_recv()
        for k in range(N_DEV - 1):
            copy(src, out, sems, k, idx(x, y, c), me).wait_send()
        pltpu.make_async_copy(src.at[rows], out.at[idx(x, y, c), rows], sems[2].at[0]).wait()

    out_shape = jax.ShapeDtypeStruct((N_DEV,) + shard.shape, shard.dtype)
    if prev is None:
        return _Job([shard], [out_shape], _job_sems(), start, mid, end)
    return _Job([shard, prev], [out_shape], _job_sems(), start, mid, end, alias={1: 0})


def _exchange_job(src, land, lo=0, hi=1, parts=1):
    n = src.shape[1] // parts
    assert n * parts == src.shape[1]

    def sl(ref, s):
        return ref.at[s, pl.ds(lo * n, (hi - lo) * n)]

    def start(ins, outs, sems):
        me = _my_index()
        pltpu.make_async_copy(sl(ins[0], me), sl(outs[0], me), sems[2].at[0]).start()
        for k, (pos, idx) in enumerate(_peers()):
            pltpu.make_async_remote_copy(
                src_ref=sl(ins[0], idx), dst_ref=sl(outs[0], me), send_sem=sems[0].at[k], recv_sem=sems[1].at[k],
                device_id=pos, device_id_type=MESH).start()

    def mid(ins, outs, sems):
        pass

    def end(ins, outs, sems):
        me = _my_index()
        for k, (pos, idx) in enumerate(_peers()):
            cp = pltpu.make_async_remote_copy(
                src_ref=sl(ins[0], idx), dst_ref=sl(outs[0], idx), send_sem=sems[0].at[k], recv_sem=sems[1].at[k],
                device_id=pos, device_id_type=MESH)
            cp.wait_recv()
            cp.wait_send()
        pltpu.make_async_copy(sl(ins[0], me), sl(outs[0], me), sems[2].at[0]).wait()

    return _Job([src, land], [jax.ShapeDtypeStruct(land.shape, land.dtype)], _job_sems(), start, mid, end, alias={1: 0})


def _adamw_math(w, g, m, v):
    m = ADAM_B1 * m + (1.0 - ADAM_B1) * g
    v = ADAM_B2 * v + (1.0 - ADAM_B2) * (g * g)
    m_hat = m / (1.0 - ADAM_B1 ** ADAM_STEP)
    v_hat = v / (1.0 - ADAM_B2 ** ADAM_STEP)
    delta = -ADAM_LR * (m_hat / (jnp.sqrt(v_hat) + ADAM_EPS) + ADAM_WD * w)
    return delta, m, v


def _sum_slots(ref):
    g = ref[0].astype(F32)
    for s in range(1, N_DEV):
        g = g + ref[s].astype(F32)
    return g


def _adamw_big(name, lands, w, m, v, jobs=(), transposed=False):
    L, R, C = w.shape
    assert len(lands) == L
    tr = _tile(R, max(LANES, (256 * 1024) // C))
    nr = R // tr

    def body(*refs):
        l_refs = refs[:L]
        w_ref, m_ref, v_ref, g_ref, d_ref, nm_ref, nv_ref = refs[L:]
        for li in range(L):
            @pl.when(pl.program_id(0) // nr == li)
            def _(li=li):
                g = _sum_slots(l_refs[li])
                if transposed:
                    g = g.T
                d, nm, nv = _adamw_math(w_ref[...], g, m_ref[...], v_ref[...])
                g_ref[...] = g
                d_ref[...] = d
                nm_ref[...] = nm
                nv_ref[...] = nv

    def land_spec(li):
        if transposed:
            return pl.BlockSpec((N_DEV, C, tr), lambda s: (0, 0, jnp.where(s // nr == li, s % nr, 0)))
        return pl.BlockSpec((N_DEV, tr, C), lambda s: (0, jnp.where(s // nr == li, s % nr, 0), 0))

    row = pl.BlockSpec((None, tr, C), lambda s: (s // nr, s % nr, 0))
    return _carry_call(
        body, name=name, steps=L * nr, in_specs=[land_spec(li) for li in range(L)] + [row, row, row],
        out_specs=[row] * 4, out_shape=[jax.ShapeDtypeStruct((L, R, C), F32)] * 4, scratch_shapes=[],
        args=(*lands, w, m, v), jobs=jobs)


def _sum8(name, slots):
    _, R, C = slots.shape

    def body(s_ref, o_ref):
        o_ref[...] = _sum_slots(s_ref)

    return pl.pallas_call(body, name=name, out_shape=jax.ShapeDtypeStruct((R, C), F32))(slots)


def _adamw_small(name, g, w, m, v):
    def body(g_ref, w_ref, m_ref, v_ref, d_ref, nm_ref, nv_ref):
        d, nm, nv = _adamw_math(w_ref[...], g_ref[...], m_ref[...], v_ref[...])
        d_ref[...] = d
        nm_ref[...] = nm
        nv_ref[...] = nv

    return pl.pallas_call(body, name=name, out_shape=[jax.ShapeDtypeStruct(w.shape, F32)] * 3)(g, w, m, v)


def _pack_rows(arrs):
    parts, spans, r0 = [], [], 0
    for a in arrs:
        flat = a.astype(F32).reshape(-1)
        rows = -(-flat.shape[0] // LANES)
        rows = -(-rows // SUBLANES) * SUBLANES
        flat = jnp.pad(flat, (0, rows * LANES - flat.shape[0]))
        parts.append(flat.reshape(rows, LANES))
        spans.append((r0, rows, a.shape))
        r0 += rows
    return jnp.concatenate(parts, axis=0), spans


def _unpack_rows(buf, span):
    r0, rows, shape = span
    n = math.prod(shape)
    return buf[..., r0:r0 + rows, :].reshape(buf.shape[:-2] + (rows * LANES,))[..., :n].reshape(buf.shape[:-2] + shape)


def _from_col_blocks(wb):
    B, K, n = wb.shape
    return jnp.transpose(wb, (1, 0, 2)).reshape(K, B * n)


def kernel(x, norm_gains, hyb_w_in, hyb_conv_a, hyb_conv_b, hyb_conv_b_bias, hyb_rg_w_a, hyb_rg_b_a, hyb_rg_w_x, hyb_rg_b_x, hyb_rg_lambda, hyb_w_out, sb_w_qkv, sb_w_o, mlp_w_up, mlp_w_down, loss_target, m_norm_gains, m_hyb_w_in, m_hyb_conv_a, m_hyb_conv_b, m_hyb_conv_b_bias, m_hyb_rg_w_a, m_hyb_rg_b_a, m_hyb_rg_w_x, m_hyb_rg_b_x, m_hyb_rg_lambda, m_hyb_w_out, m_sb_w_qkv, m_sb_w_o, m_mlp_w_up, m_mlp_w_down, v_norm_gains, v_hyb_w_in, v_hyb_conv_a, v_hyb_conv_b, v_hyb_conv_b_bias, v_hyb_rg_w_a, v_hyb_rg_b_a, v_hyb_rg_w_x, v_hyb_rg_b_x, v_hyb_rg_lambda, v_hyb_w_out, v_sb_w_qkv, v_sb_w_o, v_mlp_w_up, v_mlp_w_down):
    T, D = x.shape[1], x.shape[2]
    me = _my_index()

    small_shards, small_spans = _pack_rows([norm_gains, hyb_conv_a[0], hyb_conv_b[0]])
    (small_all,) = _run_jobs("gather_small", [_gather_job(small_shards)])
    gains_b = _unpack_rows(small_all, small_spans[0])
    gains = jnp.transpose(gains_b, (1, 2, 0, 3)).reshape(2, 4, D)
    conv_a = _from_col_blocks(_unpack_rows(small_all, small_spans[1]))
    conv_b = _from_col_blocks(_unpack_rows(small_all, small_spans[2]))

    shards = {"w_in": hyb_w_in[0].T, "w_out": hyb_w_out[0], "w_qkv": sb_w_qkv[0].T, "w_o": sb_w_o[0],
              "up0": mlp_w_up[0], "up1": mlp_w_up[1], "down0": mlp_w_down[0], "down1": mlp_w_down[1]}
    shards = {n: s.astype(BF16) for n, s in shards.items()}
    wts = _Weights({}, shards, GATHER_PLAN)
    grads_big = _Grads({n: lax.empty((N_DEV,) + s.shape, BF16) for n, s in shards.items()}, EXCHANGE_PLAN)

    sq, grad_x, small = _local_step(
        x[0], loss_target[0], gains, conv_a, conv_b, hyb_conv_b_bias, hyb_rg_w_a[0], hyb_rg_b_a, hyb_rg_w_x[0],
        hyb_rg_b_x, hyb_rg_lambda, wts, grads_big)


    names = ["norm_gains", "hyb_w_in", "hyb_conv_a", "hyb_conv_b", "hyb_conv_b_bias", "hyb_rg_w_a", "hyb_rg_b_a",
             "hyb_rg_w_x", "hyb_rg_b_x", "hyb_rg_lambda", "hyb_w_out", "sb_w_qkv", "sb_w_o", "mlp_w_up", "mlp_w_down"]
    params = dict(zip(names, [norm_gains, hyb_w_in, hyb_conv_a, hyb_conv_b, hyb_conv_b_bias, hyb_rg_w_a, hyb_rg_b_a,
                              hyb_rg_w_x, hyb_rg_b_x, hyb_rg_lambda, hyb_w_out, sb_w_qkv, sb_w_o, mlp_w_up, mlp_w_down]))
    moms = dict(zip(names, [m_norm_gains, m_hyb_w_in, m_hyb_conv_a, m_hyb_conv_b, m_hyb_conv_b_bias, m_hyb_rg_w_a,
                            m_hyb_rg_b_a, m_hyb_rg_w_x, m_hyb_rg_b_x, m_hyb_rg_lambda, m_hyb_w_out, m_sb_w_qkv,
                            m_sb_w_o, m_mlp_w_up, m_mlp_w_down]))
    vars_ = dict(zip(names, [v_norm_gains, v_hyb_w_in, v_hyb_conv_a, v_hyb_conv_b, v_hyb_conv_b_bias, v_hyb_rg_w_a,
                             v_hyb_rg_b_a, v_hyb_rg_w_x, v_hyb_rg_b_x, v_hyb_rg_lambda, v_hyb_w_out, v_sb_w_qkv,
                             v_sb_w_o, v_mlp_w_up, v_mlp_w_down]))
    grads, deltas, new_m, new_v = {}, {}, {}, {}

    big_lands = {"mlp_w_down": ["down0", "down1"], "mlp_w_up": ["up0", "up1"], "sb_w_qkv": ["w_qkv"], "sb_w_o": ["w_o"],
                 "hyb_w_out": ["w_out"], "hyb_w_in": ["w_in"]}
    for nm, keys in big_lands.items():
        call = f"adamw_{nm}"
        if nm == "hyb_w_in":
            grads_big.flush("exchange_grads")
        jobs = grads_big.jobs(call)
        assert not {k for k in keys} & {e[0] for e in EXCHANGE_PLAN.get(call, ())}
        outs = _adamw_big(call, [grads_big.lands[k] for k in keys], params[nm], moms[nm], vars_[nm], jobs=jobs,
                          transposed=nm in ("hyb_w_in", "sb_w_qkv"))
        grads[nm], deltas[nm], new_m[nm], new_v[nm] = outs[:4]
        grads_big.deliver(call, outs[4:])

    small_names = ["norm_gains", "hyb_conv_a", "hyb_conv_b", "hyb_conv_b_bias", "hyb_rg_w_a", "hyb_rg_b_a",
                   "hyb_rg_w_x", "hyb_rg_b_x", "hyb_rg_lambda"]
    small_keys = ["norm_gains", "conv_a", "conv_b", "conv_b_bias", "rg_w_a", "rg_b_a", "rg_w_x", "rg_b_x", "rg_lambda"]
    sg_buf, sg_spans = _pack_rows([small[k] for k in small_keys] + [sq.reshape(1)])
    (sg_all,) = _run_jobs("gather_small_grads", [_gather_job(sg_buf)])
    sg_sum = _sum8("sum_small_grads", sg_all)
    full = {nm: _unpack_rows(sg_sum, sp) for nm, sp in zip(small_names, sg_spans)}
    loss = _unpack_rows(sg_sum, sg_spans[-1])[0] * (0.5 / D)
    cb = (D // 2) // N_DEV
    small_grads = {
        "norm_gains": lax.dynamic_slice_in_dim(full["norm_gains"], me * (D // N_DEV), D // N_DEV, axis=2),
        "hyb_conv_a": lax.dynamic_slice_in_dim(full["hyb_conv_a"], me * cb, cb, axis=1)[None],
        "hyb_conv_b": lax.dynamic_slice_in_dim(full["hyb_conv_b"], me * cb, cb, axis=1)[None],
        "hyb_conv_b_bias": full["hyb_conv_b_bias"],
        "hyb_rg_w_a": full["hyb_rg_w_a"][None],
        "hyb_rg_b_a": full["hyb_rg_b_a"],
        "hyb_rg_w_x": full["hyb_rg_w_x"][None],
        "hyb_rg_b_x": full["hyb_rg_b_x"],
        "hyb_rg_lambda": full["hyb_rg_lambda"],
    }
    pk = lambda d: _pack_rows([d[nm] for nm in small_names])
    g_buf, spans = pk(small_grads)
    w_buf, _ = pk(params)
    m_buf, _ = pk(moms)
    v_buf, _ = pk(vars_)
    d_buf, nm_buf, nv_buf = _adamw_small("adamw_small", g_buf, w_buf, m_buf, v_buf)
    for nm, sp in zip(small_names, spans):
        grads[nm] = small_grads[nm]
        deltas[nm], new_m[nm], new_v[nm] = _unpack_rows(d_buf, sp), _unpack_rows(nm_buf, sp), _unpack_rows(nv_buf, sp)

    return (loss, grad_x[None], *[grads[n] for n in names], *[deltas[n] for n in names],
            *[new_m[n] for n in names], *[new_v[n] for n in names])
```

```python
import math

import jax
import jax.numpy as jnp
from jax import lax
from jax.experimental import pallas as pl
from jax.experimental.pallas import tpu as pltpu

F32 = jnp.float32
BF16 = jnp.bfloat16

NORM_EPS = 1e-6
LRU_C = 8.0
ATT_HEAD_DIM = 128
RG_HEAD_DIM = 64
LANES = 128
SUBLANES = 8
N_DEV = 8
ADAM_LR = 0.001
ADAM_B1 = 0.9
ADAM_B2 = 0.999
ADAM_EPS = 1e-08
ADAM_WD = 0.01
ADAM_STEP = 10
VMEM_LIMIT = 56 * 1024 * 1024
MM_TK = 2048
MM_TM = 2048
MM_TN = 1024
MM_TK_TOKENS = 4096
MESH = pl.DeviceIdType.MESH


def _tile(n, pref):
    if n <= pref:
        return n
    t = (pref // LANES) * LANES
    while t > LANES and n % t:
        t -= LANES
    assert n % t == 0, (n, pref)
    return t


def _params(sem):
    return pltpu.CompilerParams(dimension_semantics=sem, vmem_limit_bytes=VMEM_LIMIT)


DIMS_NN = (((1,), (0,)), ((), ()))
DIMS_NT = (((1,), (1,)), ((), ()))
DIMS_TN = (((0,), (0,)), ((), ()))


_ANY = pl.BlockSpec(memory_space=pl.ANY)


class _Job:
    def __init__(self, ins, outs, sems, start, mid, end, alias=None):
        self.ins, self.outs, self.sems = ins, outs, sems
        self.start, self.mid, self.end = start, mid, end
        self.alias = alias or {}


def _mid_step(steps):
    return (9 * steps) // 10


def _job_plumbing(jobs, n_in, n_out):
    j_ins = [a for jb in jobs for a in jb.ins]
    j_outs = [o for jb in jobs for o in jb.outs]
    j_sems = [s for jb in jobs for s in jb.sems]
    aliases, pi, po = {}, 0, 0
    for jb in jobs:
        for i_in, i_out in jb.alias.items():
            aliases[n_in + pi + i_in] = n_out + po + i_out
        pi += len(jb.ins)
        po += len(jb.outs)
    return j_ins, j_outs, j_sems, aliases


def _job_phase(jobs, which, jin, jout, jsem):
    pi = po = ps = 0
    for jb in jobs:
        getattr(jb, which)(jin[pi:pi + len(jb.ins)], jout[po:po + len(jb.outs)], jsem[ps:ps + len(jb.sems)])
        pi, po, ps = pi + len(jb.ins), po + len(jb.outs), ps + len(jb.sems)


def _run_jobs(name, jobs):
    j_ins, j_outs, j_sems, aliases = _job_plumbing(jobs, 0, 0)
    n_ji, n_jo = len(j_ins), len(j_outs)

    def body(*refs):
        jin, jout, jsem = refs[:n_ji], refs[n_ji:n_ji + n_jo], refs[n_ji + n_jo:]
        for which in ("start", "mid", "end"):
            _job_phase(jobs, which, jin, jout, jsem)

    return pl.pallas_call(body, name=name, in_specs=[_ANY] * n_ji, out_specs=[_ANY] * n_jo, out_shape=j_outs,
                          scratch_shapes=j_sems, input_output_aliases=aliases)(*j_ins)


def _carry_call(body, *, name, steps, in_specs, out_specs, out_shape, scratch_shapes, args, jobs=(), aliases=None,
                sequential=False):
    n_in, n_out, n_sc = len(in_specs), len(out_shape), len(scratch_shapes)
    j_ins, j_outs, j_sems, j_aliases = _job_plumbing(jobs, n_in, n_out)
    n_ji, n_jo = len(j_ins), len(j_outs)

    def wrapped(*refs):
        ins, jin = refs[:n_in], refs[n_in:n_in + n_ji]
        o0 = n_in + n_ji
        outs, jout = refs[o0:o0 + n_out], refs[o0 + n_out:o0 + n_out + n_jo]
        s0 = o0 + n_out + n_jo
        scratch, jsem = refs[s0:s0 + n_sc], refs[s0 + n_sc:]
        step = pl.program_id(0)
        if jobs:
            pl.when(step == 0)(lambda: _job_phase(jobs, "start", jin, jout, jsem))
            pl.when(step == _mid_step(steps))(lambda: _job_phase(jobs, "mid", jin, jout, jsem))
        body(*ins, *outs, *scratch)
        if jobs:
            pl.when(step == steps - 1)(lambda: _job_phase(jobs, "end", jin, jout, jsem))

    return pl.pallas_call(
        wrapped, name=name, grid=(steps,),
        in_specs=list(in_specs) + [_ANY] * n_ji, out_specs=list(out_specs) + [_ANY] * n_jo,
        out_shape=list(out_shape) + j_outs, scratch_shapes=list(scratch_shapes) + j_sems,
        input_output_aliases={**(aliases or {}), **j_aliases},
        compiler_params=_params(("arbitrary",) if (jobs or sequential) else ("parallel",)))(*args, *j_ins)


def _matmul(name, ins, outs, grid, dims, acc_shape, epilogue=None, jobs=()):
    n_in, n_out, nk = len(ins), len(outs), grid[2]
    j_ins, j_outs, j_sems, aliases = _job_plumbing(jobs, n_in, n_out)
    n_ji, n_jo = len(j_ins), len(j_outs)
    total = grid[0] * grid[1] * grid[2]
    n_acc = 0 if nk == 1 else 1

    def body(*refs):
        a_ref, b_ref = refs[0], refs[1]
        extras = refs[2:n_in]
        jin = refs[n_in:n_in + n_ji]
        out_refs = refs[n_in + n_ji:n_in + n_ji + n_out]
        jout = refs[n_in + n_ji + n_out:n_in + n_ji + n_out + n_jo]
        jsem = refs[n_in + n_ji + n_out + n_jo + n_acc:]
        k = pl.program_id(2)
        step = (pl.program_id(0) * grid[1] + pl.program_id(1)) * grid[2] + k
        if jobs:
            pl.when(step == 0)(lambda: _job_phase(jobs, "start", jin, jout, jsem))
            pl.when(step == _mid_step(total))(lambda: _job_phase(jobs, "mid", jin, jout, jsem))

        def finish(r):
            res = epilogue(r, *[e[...] for e in extras]) if epilogue is not None else (r,)
            for o, v in zip(out_refs, res):
                o[...] = v.astype(o.dtype)

        if len(b_ref.shape) == 2:
            prod = lax.dot_general(a_ref[...], b_ref[...], dims, preferred_element_type=F32)
        else:
            kb = a_ref.shape[1] // b_ref.shape[0]
            prod = sum(lax.dot_general(a_ref[:, g * kb:(g + 1) * kb], b_ref[g], dims, preferred_element_type=F32)
                       for g in range(b_ref.shape[0]))
        if nk == 1:
            finish(prod)
        else:
            acc = refs[n_in + n_ji + n_out + n_jo]

            @pl.when(k == 0)
            def _():
                acc[...] = prod

            @pl.when((k > 0) & (k < nk - 1))
            def _():
                acc[...] += prod

            @pl.when(k == nk - 1)
            def _():
                finish(acc[...] + prod)

        if jobs:
            pl.when(step == total - 1)(lambda: _job_phase(jobs, "end", jin, jout, jsem))

    sem = ("arbitrary",) * 3 if jobs else ("parallel", "parallel", "arbitrary")
    res = pl.pallas_call(
        body, name=name, grid=grid,
        in_specs=[s for _, s in ins] + [_ANY] * n_ji,
        out_specs=[s for _, s in outs] + [_ANY] * n_jo,
        out_shape=[s for s, _ in outs] + j_outs,
        scratch_shapes=[pltpu.VMEM(acc_shape, F32)] * n_acc + j_sems,
        input_output_aliases=aliases,
        compiler_params=_params(sem),
    )(*[a for a, _ in ins], *j_ins)
    return res


def _mm_nn(name, a, b, out_dtype, *, a_seg=None, out_seg=None, tm=MM_TM, tn=MM_TN, tk=MM_TK, jobs=()):
    if a_seg:
        _, M, ks = a.shape
        K = ks * a_seg
    else:
        M, K = a.shape
        ks = K
    N = b.shape[1]
    ns = N // out_seg if out_seg else N
    tm, tn, tk = _tile(M, tm), _tile(ns, tn), _tile(ks, tk)
    nks, nns = ks // tk, ns // tn
    grid = (M // tm, N // tn, K // tk)
    if a_seg:
        a_spec = pl.BlockSpec((None, tm, tk), lambda i, j, k: (k // nks, i, k % nks))
    else:
        a_spec = pl.BlockSpec((tm, tk), lambda i, j, k: (i, k))
    b_spec = pl.BlockSpec((tk, tn), lambda i, j, k: (k, j))
    if out_seg:
        o_spec = pl.BlockSpec((None, tm, tn), lambda i, j, k: (j // nns, i, j % nns))
        o_shape = (out_seg, M, ns)
    else:
        o_spec = pl.BlockSpec((tm, tn), lambda i, j, k: (i, j))
        o_shape = (M, N)
    outs = [(jax.ShapeDtypeStruct(o_shape, out_dtype), o_spec)]
    return _matmul(name, [(a, a_spec), (b, b_spec)], outs, grid, DIMS_NN, (tm, tn), None, jobs)


def _mm_nt(name, a, b, out_dtype, *, a_seg=None, out_seg=None, tm=MM_TM, tn=MM_TN, tk=MM_TK, jobs=()):
    if a_seg:
        _, M, ks = a.shape
        K = ks * a_seg
    else:
        M, K = a.shape
        ks = K
    N = b.shape[0]
    ns = N // out_seg if out_seg else N
    tm, tn, tk = _tile(M, tm), _tile(ns, tn), _tile(ks, tk)
    nks, nns = ks // tk, ns // tn
    grid = (M // tm, N // tn, K // tk)
    if a_seg:
        a_spec = pl.BlockSpec((None, tm, tk), lambda i, j, k: (k // nks, i, k % nks))
    else:
        a_spec = pl.BlockSpec((tm, tk), lambda i, j, k: (i, k))
    b_spec = pl.BlockSpec((tn, tk), lambda i, j, k: (j, k))
    if out_seg:
        o_spec = pl.BlockSpec((None, tm, tn), lambda i, j, k: (j // nns, i, j % nns))
        o_shape = (out_seg, M, ns)
    else:
        o_spec = pl.BlockSpec((tm, tn), lambda i, j, k: (i, j))
        o_shape = (M, N)
    outs = [(jax.ShapeDtypeStruct(o_shape, out_dtype), o_spec)]
    return _matmul(name, [(a, a_spec), (b, b_spec)], outs, grid, DIMS_NT, (tm, tn), None, jobs)


def _mm_tn(name, a, b, out_dtype, *, a_seg=None, b_seg=None, out_blocks=None, tm=MM_TN, tn=MM_TN, tk=MM_TK_TOKENS,
           jobs=()):
    if a_seg:
        _, T, ms = a.shape
        M = ms * a_seg
    else:
        T, M = a.shape
        ms = M
    if b_seg:
        _, _, ns = b.shape
        N = ns * b_seg
    else:
        N = b.shape[1]
        ns = N
    nb_cols = N // out_blocks if out_blocks else N
    tm, tk = _tile(ms, tm), _tile(T, tk)
    tn = _tile(math.gcd(ns, nb_cols), tn)
    nms, nns, nbs = ms // tm, ns // tn, nb_cols // tn
    grid = (M // tm, N // tn, T // tk)
    if a_seg:
        a_spec = pl.BlockSpec((None, tk, tm), lambda i, j, k: (i // nms, k, i % nms))
    else:
        a_spec = pl.BlockSpec((tk, tm), lambda i, j, k: (k, i))
    if b_seg:
        b_spec = pl.BlockSpec((None, tk, tn), lambda i, j, k: (j // nns, k, j % nns))
    else:
        b_spec = pl.BlockSpec((tk, tn), lambda i, j, k: (k, j))
    if out_blocks:
        o_spec = pl.BlockSpec((None, tm, tn), lambda i, j, k: (j // nbs, i, j % nbs))
        o_shape = (out_blocks, M, nb_cols)
    else:
        o_spec = pl.BlockSpec((tm, tn), lambda i, j, k: (i, j))
        o_shape = (M, N)
    outs = [(jax.ShapeDtypeStruct(o_shape, out_dtype), o_spec)]
    return _matmul(name, [(a, a_spec), (b, b_spec)], outs, grid, DIMS_TN, (tm, tn), None, jobs)


def _rms(x):
    return lax.rsqrt(jnp.mean(x * x, axis=-1, keepdims=True) + NORM_EPS)


def _row_tile(T):
    return _tile(T, 256)


def _norm_fwd(name, x, g, jobs=()):
    T, D = x.shape
    tr = _row_tile(T)

    def body(x_ref, g_ref, h_ref):
        xv = x_ref[...]
        h_ref[...] = (xv * _rms(xv) * g_ref[...]).astype(h_ref.dtype)

    row = pl.BlockSpec((tr, D), lambda i: (i, 0))
    vec = pl.BlockSpec((1, D), lambda i: (0, 0))
    return _carry_call(body, name=name, steps=T // tr, in_specs=[row, vec], out_specs=[row],
                       out_shape=[jax.ShapeDtypeStruct((T, D), BF16)], scratch_shapes=[], args=(x, g), jobs=jobs)


def _resid_norm(name, x, br, g_post, g_next, jobs=()):
    T, D = x.shape
    tr = _row_tile(T)

    def body(x_ref, br_ref, gp_ref, gn_ref, xn_ref, h_ref):
        b = br_ref[...].astype(F32)
        xn = x_ref[...] + b * _rms(b) * gp_ref[...]
        xn_ref[...] = xn
        h_ref[...] = (xn * _rms(xn) * gn_ref[...]).astype(h_ref.dtype)

    row = pl.BlockSpec((tr, D), lambda i: (i, 0))
    vec = pl.BlockSpec((1, D), lambda i: (0, 0))
    return _carry_call(body, name=name, steps=T // tr, in_specs=[row, row, vec, vec], out_specs=[row, row],
                       out_shape=[jax.ShapeDtypeStruct((T, D), F32), jax.ShapeDtypeStruct((T, D), BF16)],
                       scratch_shapes=[], args=(x, br, g_post, g_next), jobs=jobs)


def _rms_bwd(xv, g, dyv):
    r = _rms(xv)
    xhat = xv * r
    gdy = dyv * g
    dx = r * (gdy - xhat * jnp.mean(gdy * xhat, axis=-1, keepdims=True))
    return dx, jnp.sum(dyv * xhat, axis=0, keepdims=True)


def _final_loss(name, x, br, g_post, target):
    T, D = x.shape
    tr = _row_tile(T)

    def body(x_ref, br_ref, gp_ref, t_ref, dy_ref, dbr_ref, dg_ref, ls_ref):
        b = br_ref[...].astype(F32)
        err = x_ref[...] + b * _rms(b) * gp_ref[...] - t_ref[...]
        dy = err * (1.0 / D)
        dy_ref[...] = dy
        dbr, dg = _rms_bwd(b, gp_ref[...], dy)
        dbr_ref[...] = dbr.astype(dbr_ref.dtype)

        @pl.when(pl.program_id(0) == 0)
        def _():
            ls_ref[...] = jnp.zeros_like(ls_ref)
            dg_ref[...] = jnp.zeros_like(dg_ref)

        ls_ref[...] += jnp.sum(err * err)
        dg_ref[...] += dg

    row = pl.BlockSpec((tr, D), lambda i: (i, 0))
    vec = pl.BlockSpec((1, D), lambda i: (0, 0))
    acc = pl.BlockSpec((SUBLANES, LANES), lambda i: (0, 0))
    return pl.pallas_call(body, name=name, grid=(T // tr,), in_specs=[row, row, vec, row],
                          out_specs=[row, row, vec, acc],
                          out_shape=[jax.ShapeDtypeStruct((T, D), F32), jax.ShapeDtypeStruct((T, D), BF16),
                                     jax.ShapeDtypeStruct((1, D), F32), jax.ShapeDtypeStruct((SUBLANES, LANES), F32)],
                          compiler_params=_params(("arbitrary",)))(x, br, g_post, target)


def _norm_bwd_pair(name, x, g_pre, dh, add, br, g_post, jobs=()):
    T, D = x.shape
    tr = _row_tile(T)

    def body(x_ref, gpre_ref, dh_ref, add_ref, br_ref, gpost_ref, dx_ref, dbr_ref, dgpre_ref, dgpost_ref):
        dx, dg_pre = _rms_bwd(x_ref[...], gpre_ref[...], dh_ref[...].astype(F32))
        dx = dx + add_ref[...]
        dx_ref[...] = dx
        dbr, dg_post = _rms_bwd(br_ref[...].astype(F32), gpost_ref[...], dx)
        dbr_ref[...] = dbr.astype(dbr_ref.dtype)

        @pl.when(pl.program_id(0) == 0)
        def _():
            dgpre_ref[...] = jnp.zeros_like(dgpre_ref)
            dgpost_ref[...] = jnp.zeros_like(dgpost_ref)

        dgpre_ref[...] += dg_pre
        dgpost_ref[...] += dg_post

    row = pl.BlockSpec((tr, D), lambda i: (i, 0))
    vec = pl.BlockSpec((1, D), lambda i: (0, 0))
    return _carry_call(body, name=name, steps=T // tr, in_specs=[row, vec, row, row, row, vec],
                       out_specs=[row, row, vec, vec],
                       out_shape=[jax.ShapeDtypeStruct((T, D), F32), jax.ShapeDtypeStruct((T, D), BF16),
                                  jax.ShapeDtypeStruct((1, D), F32), jax.ShapeDtypeStruct((1, D), F32)],
                       scratch_shapes=[], args=(x, g_pre, dh, add, br, g_post), jobs=jobs, sequential=True)


def _norm_bwd(name, x, g, dy, add, jobs=()):
    T, D = x.shape
    tr = _row_tile(T)

    def body(x_ref, g_ref, dy_ref, add_ref, dx_ref, dg_ref):
        dx, dg = _rms_bwd(x_ref[...], g_ref[...], dy_ref[...].astype(F32))
        dx_ref[...] = dx + add_ref[...]

        @pl.when(pl.program_id(0) == 0)
        def _():
            dg_ref[...] = jnp.zeros_like(dg_ref)

        dg_ref[...] += dg

    row = pl.BlockSpec((tr, D), lambda i: (i, 0))
    vec = pl.BlockSpec((1, D), lambda i: (0, 0))
    return _carry_call(body, name=name, steps=T // tr, in_specs=[row, vec, row, row], out_specs=[row, vec],
                       out_shape=[jax.ShapeDtypeStruct((T, D), F32), jax.ShapeDtypeStruct((1, D), F32)],
                       scratch_shapes=[], args=(x, g, dy, add), jobs=jobs, sequential=True)


HALO = SUBLANES
TIME_CHUNK = 512


def _chunks(T):
    tc = min(TIME_CHUNK, T)
    assert T % tc == 0 and tc % SUBLANES == 0
    return [(t0, tc) for t0 in range(0, T, tc)]


def _log_sigmoid(x):
    return -(jnp.maximum(-x, 0.0) + jnp.log(1.0 + jnp.exp(-jnp.abs(x))))


def _sigmoid(x):
    return 0.5 * jnp.tanh(0.5 * x) + 0.5


def _one_minus_exp(x):
    series = -x * (1.0 + x * (0.5 + x * (1.0 / 6.0 + x * (1.0 / 24.0))))
    return jnp.where(x > -0.01, series, 1.0 - jnp.exp(x))


_GELU_C = math.sqrt(2.0 / math.pi)


def _gelu(x):
    return 0.5 * x * (1.0 + jnp.tanh(_GELU_C * (x + 0.044715 * x * x * x)))


def _gelu_grad(x):
    th = jnp.tanh(_GELU_C * (x + 0.044715 * x * x * x))
    return 0.5 * (1.0 + th) + 0.5 * x * (1.0 - th * th) * _GELU_C * (1.0 + 3.0 * 0.044715 * x * x)


def _tile_scan(a, b, reverse):
    rows = a.shape[0]
    pos = lax.broadcasted_iota(jnp.int32, a.shape, 0) & (SUBLANES - 1)
    for d in (1, 2, 4):
        if reverse:
            ok = pos < SUBLANES - d
            shift = rows - d
        else:
            ok = pos >= d
            shift = d
        a_sh = jnp.where(ok, pltpu.roll(a, shift, 0), 1.0)
        b_sh = jnp.where(ok, pltpu.roll(b, shift, 0), 0.0)
        b = a * b_sh + b
        a = a * a_sh
    return a, b


def _carry_scan(a_s, b_s, T, reverse):
    n = T // SUBLANES
    edge = 0 if reverse else SUBLANES - 1

    def step(j, carry):
        g = (n - 1 - j) if reverse else j
        r = pl.multiple_of(g * SUBLANES, SUBLANES)
        h = b_s[pl.ds(r, SUBLANES), :] + a_s[pl.ds(r, SUBLANES), :] * carry
        b_s[pl.ds(r, SUBLANES), :] = h
        return jnp.broadcast_to(h[edge:edge + 1, :], h.shape)

    lax.fori_loop(0, n, step, jnp.zeros((SUBLANES, a_s.shape[1]), F32))


def _seg_spec(T, seg, nblk):
    return pl.BlockSpec((None, T, LANES), lambda c: (seg, 0, c))


def _rows_to_tile(rows):
    idx = lax.broadcasted_iota(jnp.int32, (SUBLANES, LANES), 0)
    out = jnp.zeros((SUBLANES, LANES), F32)
    for k, r in enumerate(rows):
        out = jnp.where(idx == k, r, out)
    return out


def _mixer_a_fwd(proj, conv_a):
    _, T, C = proj.shape
    nblk = C // LANES
    chunks = _chunks(T)

    def body(bg_ref, cg_ref, ax_ref, w_ref, y_ref, p_s):
        p_s[pl.ds(0, HALO), :] = jnp.zeros((HALO, LANES), F32)
        for t0, tc in chunks:
            p_s[pl.ds(HALO + t0, tc), :] = cg_ref[pl.ds(t0, tc), :] * ax_ref[pl.ds(t0, tc), :]
        w = w_ref[...]
        for t0, tc in chunks:
            c = (w[2:3, :] * p_s[pl.ds(HALO + t0, tc), :] + w[1:2, :] * p_s[pl.ds(HALO + t0 - 1, tc), :]
                 + w[0:1, :] * p_s[pl.ds(HALO + t0 - 2, tc), :])
            y_ref[pl.ds(t0, tc), :] = (bg_ref[pl.ds(t0, tc), :] * c).astype(y_ref.dtype)

    return pl.pallas_call(
        body, name="mixer_a_fwd", grid=(nblk,),
        in_specs=[_seg_spec(T, 0, nblk), _seg_spec(T, 1, nblk), _seg_spec(T, 2, nblk),
                  pl.BlockSpec((3, LANES), lambda c: (0, c))],
        out_specs=_seg_spec(T, 0, nblk),
        out_shape=jax.ShapeDtypeStruct((2, T, C), BF16),
        scratch_shapes=[pltpu.VMEM((T + HALO, LANES), F32)],
        compiler_params=_params(("parallel",)))(proj, proj, proj, conv_a)


def _mixer_a_bwd(proj, conv_a, dy):
    _, T, C = proj.shape
    nblk = C // LANES
    chunks = _chunks(T)

    def body(bg_ref, cg_ref, ax_ref, w_ref, dy_ref, dp_ref, dw_ref, p_s, dc_s):
        p_s[pl.ds(0, HALO), :] = jnp.zeros((HALO, LANES), F32)
        dc_s[pl.ds(T, HALO), :] = jnp.zeros((HALO, LANES), F32)
        for t0, tc in chunks:
            p_s[pl.ds(HALO + t0, tc), :] = cg_ref[pl.ds(t0, tc), :] * ax_ref[pl.ds(t0, tc), :]
        w = w_ref[...]
        for t0, tc in chunks:
            c = (w[2:3, :] * p_s[pl.ds(HALO + t0, tc), :] + w[1:2, :] * p_s[pl.ds(HALO + t0 - 1, tc), :]
                 + w[0:1, :] * p_s[pl.ds(HALO + t0 - 2, tc), :])
            dyv = dy_ref[pl.ds(t0, tc), :]
            dp_ref[0, pl.ds(t0, tc), :] = (dyv * c).astype(dp_ref.dtype)
            dc_s[pl.ds(t0, tc), :] = dyv * bg_ref[pl.ds(t0, tc), :]
        dw = [jnp.zeros((1, LANES), F32) for _ in range(3)]
        for t0, tc in chunks:
            dc = dc_s[pl.ds(t0, tc), :]
            dpv = w[2:3, :] * dc + w[1:2, :] * dc_s[pl.ds(t0 + 1, tc), :] + w[0:1, :] * dc_s[pl.ds(t0 + 2, tc), :]
            dp_ref[1, pl.ds(t0, tc), :] = (dpv * ax_ref[pl.ds(t0, tc), :]).astype(dp_ref.dtype)
            dp_ref[2, pl.ds(t0, tc), :] = (dpv * cg_ref[pl.ds(t0, tc), :]).astype(dp_ref.dtype)
            for k in range(3):
                dw[k] = dw[k] + jnp.sum(dc * p_s[pl.ds(HALO + t0 - (2 - k), tc), :], axis=0, keepdims=True)
        dw_ref[...] = _rows_to_tile(dw)

    return pl.pallas_call(
        body, name="mixer_a_bwd", grid=(nblk,),
        in_specs=[_seg_spec(T, 0, nblk), _seg_spec(T, 1, nblk), _seg_spec(T, 2, nblk),
                  pl.BlockSpec((3, LANES), lambda c: (0, c)), _seg_spec(T, 0, nblk)],
        out_specs=[pl.BlockSpec((3, T, LANES), lambda c: (0, 0, c)),
                   pl.BlockSpec((None, SUBLANES, LANES), lambda c: (c, 0, 0))],
        out_shape=[jax.ShapeDtypeStruct((6, T, C), BF16), jax.ShapeDtypeStruct((nblk, SUBLANES, LANES), F32)],
        scratch_shapes=[pltpu.VMEM((T + HALO, LANES), F32), pltpu.VMEM((T + HALO, LANES), F32)],
        compiler_params=_params(("parallel",)))(proj, proj, proj, conv_a, dy)


def _rg_gates(xr, wa, ba, wx, bx, ls):
    xb = xr.astype(BF16)
    r = _sigmoid(jnp.dot(xb, wa, preferred_element_type=F32) + ba)
    i = _sigmoid(jnp.dot(xb, wx, preferred_element_type=F32) + bx)
    log_a = LRU_C * r * ls
    a = jnp.exp(log_a)
    mult = jnp.sqrt(_one_minus_exp(2.0 * log_a))
    return r, i, a, mult


def _conv4(xh_s, cw, bias, t0, tc):
    return (cw[3:4, :] * xh_s[pl.ds(HALO + t0, tc), :] + cw[2:3, :] * xh_s[pl.ds(HALO + t0 - 1, tc), :]
            + cw[1:2, :] * xh_s[pl.ds(HALO + t0 - 2, tc), :] + cw[0:1, :] * xh_s[pl.ds(HALO + t0 - 3, tc), :] + bias)


def _mixer_b_specs(T, nblk):
    vec = pl.BlockSpec((1, LANES), lambda c: (0, c))
    mat = pl.BlockSpec((None, LANES, LANES), lambda c: (c, 0, 0))
    return [_seg_spec(T, 3, nblk), _seg_spec(T, 4, nblk), pl.BlockSpec((4, LANES), lambda c: (0, c)),
            vec, mat, vec, mat, vec, vec]


def _mixer_b_fwd(name, proj, conv_b, bias, wa, ba, wx, bx, lam, y, jobs=()):
    _, T, C = proj.shape
    nblk = C // LANES
    chunks = _chunks(T)

    def body(gate_ref, x_ref, cw_ref, cb_ref, wa_ref, ba_ref, wx_ref, bx_ref, lam_ref, y_in, y_ref, xh_s, a_s, b_s):
        xh_s[pl.ds(0, HALO), :] = jnp.zeros((HALO, LANES), F32)
        for t0, tc in chunks:
            xh_s[pl.ds(HALO + t0, tc), :] = x_ref[pl.ds(t0, tc), :]
        cw, bias_v = cw_ref[...], cb_ref[...]
        ls = _log_sigmoid(lam_ref[...])
        for t0, tc in chunks:
            xr = _conv4(xh_s, cw, bias_v, t0, tc)
            r, i, a, mult = _rg_gates(xr, wa_ref[...], ba_ref[...], wx_ref[...], bx_ref[...], ls)
            ac, hc = _tile_scan(a, mult * i * xr, reverse=False)
            a_s[pl.ds(t0, tc), :] = ac
            b_s[pl.ds(t0, tc), :] = hc
        _carry_scan(a_s, b_s, T, reverse=False)
        for t0, tc in chunks:
            y_ref[pl.ds(t0, tc), :] = (b_s[pl.ds(t0, tc), :] * _gelu(gate_ref[pl.ds(t0, tc), :])).astype(y_ref.dtype)

    return _carry_call(
        body, name=name, steps=nblk, in_specs=_mixer_b_specs(T, nblk) + [_ANY],
        out_specs=[_seg_spec(T, 1, nblk)], out_shape=[jax.ShapeDtypeStruct(y.shape, y.dtype)],
        scratch_shapes=[pltpu.VMEM((T + HALO, LANES), F32), pltpu.VMEM((T, LANES), F32), pltpu.VMEM((T, LANES), F32)],
        args=(proj, proj, conv_b, bias, wa, ba, wx, bx, lam, y), jobs=jobs, aliases={9: 0})


_ROW_CONV, _ROW_BIAS, _ROW_BA, _ROW_BX, _ROW_LAM = 0, 4, 5, 6, 7


def _mixer_b_bwd(name, proj, conv_b, bias, wa, ba, wx, bx, lam, dy, dproj, jobs=()):
    _, T, C = proj.shape
    nblk = C // LANES
    chunks = _chunks(T)

    def body(gate_ref, x_ref, cw_ref, cb_ref, wa_ref, ba_ref, wx_ref, bx_ref, lam_ref, dy_ref, dp_in,
             dp_ref, sm_ref, dwa_ref, dwx_ref, xh_s, xr_s, r_s, i_s, a_s, h_s, sa_s, sb_s, dx_s):
        zero_halo = jnp.zeros((HALO, LANES), F32)
        xh_s[pl.ds(0, HALO), :] = zero_halo
        h_s[pl.ds(0, HALO), :] = zero_halo
        a_s[pl.ds(T, HALO), :] = zero_halo
        dx_s[pl.ds(T, HALO), :] = zero_halo
        for t0, tc in chunks:
            xh_s[pl.ds(HALO + t0, tc), :] = x_ref[pl.ds(t0, tc), :]
        cw, bias_v = cw_ref[...], cb_ref[...]
        lam_v = lam_ref[...]
        ls = _log_sigmoid(lam_v)
        wa_v, wx_v, ba_v, bx_v = wa_ref[...], wx_ref[...], ba_ref[...], bx_ref[...]
        for t0, tc in chunks:
            xr = _conv4(xh_s, cw, bias_v, t0, tc)
            r, i, a, mult = _rg_gates(xr, wa_v, ba_v, wx_v, bx_v, ls)
            xr_s[pl.ds(t0, tc), :] = xr
            r_s[pl.ds(t0, tc), :] = r
            i_s[pl.ds(t0, tc), :] = i
            a_s[pl.ds(t0, tc), :] = a
            ac, hc = _tile_scan(a, mult * i * xr, reverse=False)
            sa_s[pl.ds(t0, tc), :] = ac
            sb_s[pl.ds(t0, tc), :] = hc
        _carry_scan(sa_s, sb_s, T, reverse=False)
        for t0, tc in chunks:
            h_s[pl.ds(HALO + t0, tc), :] = sb_s[pl.ds(t0, tc), :]
        for t0, tc in chunks:
            gv = gate_ref[pl.ds(t0, tc), :]
            dyv = dy_ref[pl.ds(t0, tc), :]
            dp_ref[0, pl.ds(t0, tc), :] = (dyv * h_s[pl.ds(HALO + t0, tc), :] * _gelu_grad(gv)).astype(dp_ref.dtype)
            ac, gc = _tile_scan(a_s[pl.ds(t0 + 1, tc), :], dyv * _gelu(gv), reverse=True)
            sa_s[pl.ds(t0, tc), :] = ac
            sb_s[pl.ds(t0, tc), :] = gc
        _carry_scan(sa_s, sb_s, T, reverse=True)
        acc = {k: jnp.zeros((1, LANES), F32) for k in ("bias", "ba", "bx", "lam")}
        dwa = jnp.zeros((LANES, LANES), F32)
        dwx = jnp.zeros((LANES, LANES), F32)
        for t0, tc in chunks:
            dht = sb_s[pl.ds(t0, tc), :]
            xr, r, i, a = xr_s[pl.ds(t0, tc), :], r_s[pl.ds(t0, tc), :], i_s[pl.ds(t0, tc), :], a_s[pl.ds(t0, tc), :]
            mult = jnp.sqrt(_one_minus_exp(2.0 * LRU_C * r * ls))
            da = dht * h_s[pl.ds(HALO + t0 - 1, tc), :]
            dmult = dht * i * xr
            di = dht * mult * xr
            dlog_a = da * a - dmult * a * a / mult
            dpa = dlog_a * (LRU_C * ls) * r * (1.0 - r)
            dpx = di * i * (1.0 - i)
            acc["lam"] = acc["lam"] + jnp.sum(dlog_a * r, axis=0, keepdims=True)
            acc["ba"] = acc["ba"] + jnp.sum(dpa, axis=0, keepdims=True)
            acc["bx"] = acc["bx"] + jnp.sum(dpx, axis=0, keepdims=True)
            xb, dpab, dpxb = xr.astype(BF16), dpa.astype(BF16), dpx.astype(BF16)
            dwa = dwa + lax.dot_general(xb, dpab, DIMS_TN, preferred_element_type=F32)
            dwx = dwx + lax.dot_general(xb, dpxb, DIMS_TN, preferred_element_type=F32)
            dxr = (dht * mult * i + lax.dot_general(dpab, wa_v, DIMS_NT, preferred_element_type=F32)
                   + lax.dot_general(dpxb, wx_v, DIMS_NT, preferred_element_type=F32))
            acc["bias"] = acc["bias"] + jnp.sum(dxr, axis=0, keepdims=True)
            dx_s[pl.ds(t0, tc), :] = dxr
        dcw = [jnp.zeros((1, LANES), F32) for _ in range(4)]
        for t0, tc in chunks:
            dxr = dx_s[pl.ds(t0, tc), :]
            dxin = (cw[3:4, :] * dxr + cw[2:3, :] * dx_s[pl.ds(t0 + 1, tc), :] + cw[1:2, :] * dx_s[pl.ds(t0 + 2, tc), :]
                    + cw[0:1, :] * dx_s[pl.ds(t0 + 3, tc), :])
            dp_ref[1, pl.ds(t0, tc), :] = dxin.astype(dp_ref.dtype)
            for k in range(4):
                dcw[k] = dcw[k] + jnp.sum(dxr * xh_s[pl.ds(HALO + t0 - (3 - k), tc), :], axis=0, keepdims=True)
        dlam = acc["lam"] * LRU_C * _sigmoid(-lam_v)
        sm_ref[...] = _rows_to_tile(dcw + [acc["bias"], acc["ba"], acc["bx"], dlam])
        dwa_ref[...] = dwa
        dwx_ref[...] = dwx

    big = lambda halo: pltpu.VMEM((T + halo, LANES), F32)
    mat = pl.BlockSpec((None, LANES, LANES), lambda c: (c, 0, 0))
    return _carry_call(
        body, name=name, steps=nblk,
        in_specs=_mixer_b_specs(T, nblk) + [_seg_spec(T, 1, nblk), _ANY],
        out_specs=[pl.BlockSpec((3, T, LANES), lambda c: (1, 0, c)),
                   pl.BlockSpec((None, SUBLANES, LANES), lambda c: (c, 0, 0)), mat, mat],
        out_shape=[jax.ShapeDtypeStruct(dproj.shape, dproj.dtype), jax.ShapeDtypeStruct((nblk, SUBLANES, LANES), F32),
                   jax.ShapeDtypeStruct((nblk, LANES, LANES), F32), jax.ShapeDtypeStruct((nblk, LANES, LANES), F32)],
        scratch_shapes=[big(HALO), big(0), big(0), big(0), big(HALO), big(HALO), big(0), big(0), big(HALO)],
        args=(proj, proj, conv_b, bias, wa, ba, wx, bx, lam, dy, dproj), jobs=jobs, aliases={10: 0})


ATT_BLOCK = 128
ATT_GROUP = 3
ATT_TILE = ATT_BLOCK * ATT_GROUP
ATT_UNDERFLOW = -110.0
ATT_UNVISITED = -1e30


def _split_dot(x, m):
    hi = x.astype(BF16)
    lo = (x - hi.astype(F32)).astype(BF16)
    return jnp.dot(hi, m, preferred_element_type=F32) + jnp.dot(lo, m, preferred_element_type=F32)


def _sub(x, j):
    return x[:, j * ATT_BLOCK:(j + 1) * ATT_BLOCK]


def _stack_rows(x):
    return jnp.concatenate([_sub(x, j) for j in range(ATT_GROUP)], axis=0)


def _unstack_rows(x, offsets):
    return jnp.concatenate([x[j * ATT_BLOCK:(j + 1) * ATT_BLOCK, :] + offsets[j] for j in range(ATT_GROUP)], axis=1)


def _att_tile(q, k_ref, q0, qb, it, scale):
    hi = (qb + 1 - ATT_GROUP * it) * ATT_BLOCK
    k0 = pl.multiple_of(jnp.maximum(hi - ATT_TILE, 0), ATT_BLOCK)
    kt = k_ref[pl.ds(k0, ATT_TILE), :]
    z = lax.dot_general(q, kt, DIMS_NT, preferred_element_type=F32) * scale
    key = k0 + lax.broadcasted_iota(jnp.int32, z.shape, 1)
    row = q0 + lax.broadcasted_iota(jnp.int32, z.shape, 0)
    mask = (key < row) & (key < hi)
    n = jnp.where(mask, -(jnp.maximum(z, 0.0) + jnp.log(1.0 + jnp.exp(-jnp.abs(z)))), 0.0)
    return k0, kt, z, mask, n


def _suffix_in_tile(n, upper, run):
    rs = [jnp.sum(_sub(n, j), axis=1, keepdims=True) for j in range(ATT_GROUP)]
    offs = [None] * ATT_GROUP
    offs[ATT_GROUP - 1] = run
    for j in range(ATT_GROUP - 2, -1, -1):
        offs[j] = offs[j + 1] + rs[j + 1]
    return _unstack_rows(_split_dot(_stack_rows(n), upper), offs), offs[0] + rs[0]


def _head_spec(T, seg, heads):
    return pl.BlockSpec((None, T, ATT_HEAD_DIM), lambda h: (seg, 0, h))


def _attention_fwd(name, qkv, jobs=()):
    _, T, D = qkv.shape
    heads = D // ATT_HEAD_DIM
    nq = T // ATT_BLOCK
    assert nq <= LANES and T >= ATT_TILE and nq % 2 == 0
    scale = 1.0 / math.sqrt(ATT_HEAD_DIM)

    def body(q_ref, k_ref, v_ref, o_ref, r_ref, acc_s, run_s):
        rr = lax.broadcasted_iota(jnp.int32, (ATT_BLOCK, ATT_BLOCK), 0)
        cc = lax.broadcasted_iota(jnp.int32, (ATT_BLOCK, ATT_BLOCK), 1)
        upper = jnp.where(rr > cc, 1.0, 0.0).astype(BF16)
        lane = lax.broadcasted_iota(jnp.int32, (ATT_BLOCK, LANES), 1)

        def tile(slot, qb, q0, q, it, first):
            k0, _, z, mask, n = _att_tile(q, k_ref, q0, qb, it, scale)
            run = jnp.zeros((ATT_BLOCK, LANES), F32) if first else run_s[slot]
            suffix, run_next = _suffix_in_tile(n, upper, run)
            w = jnp.where(mask, jnp.exp(z + n + suffix), 0.0)
            pv = jnp.dot(w.astype(BF16), v_ref[pl.ds(k0, ATT_TILE), :], preferred_element_type=F32)
            if first:
                acc_s[slot] = pv
            else:
                acc_s[slot] += pv
                r_ref[pl.ds(q0, ATT_BLOCK), :] = jnp.where(lane == it, run, r_ref[pl.ds(q0, ATT_BLOCK), :])
            run_s[slot] = run_next
            return jnp.max(run_next) >= ATT_UNDERFLOW

        def pair_loop(p, _):
            blocks = []
            for slot in range(2):
                qb = 2 * p + slot
                q0 = pl.multiple_of(qb * ATT_BLOCK, ATT_BLOCK)
                r_ref[pl.ds(q0, ATT_BLOCK), :] = jnp.where(lane == 0, 0.0, ATT_UNVISITED)
                blocks.append((qb, q0, q_ref[pl.ds(q0, ATT_BLOCK), :]))
            go = [tile(slot, *blocks[slot], 0, True) for slot in range(2)]
            for slot in range(2):
                qb, q0, q = blocks[slot]
                n_tiles = (qb + ATT_GROUP) // ATT_GROUP
                lax.while_loop(lambda c: (c[0] < n_tiles) & c[1],
                               lambda c: (c[0] + 1, tile(slot, qb, q0, q, c[0], False)), (jnp.int32(1), go[slot]))
                o_ref[pl.ds(q0, ATT_BLOCK), :] = acc_s[slot].astype(o_ref.dtype)
            return 0

        lax.fori_loop(0, nq // 2, pair_loop, 0)

    return _carry_call(
        body, name=name, steps=heads,
        in_specs=[_head_spec(T, 0, heads), _head_spec(T, 1, heads), _head_spec(T, 2, heads)],
        out_specs=[pl.BlockSpec((T, ATT_HEAD_DIM), lambda h: (0, h)), pl.BlockSpec((None, T, LANES), lambda h: (h, 0, 0))],
        out_shape=[jax.ShapeDtypeStruct((T, D), BF16), jax.ShapeDtypeStruct((heads, T, LANES), F32)],
        scratch_shapes=[pltpu.VMEM((2, ATT_BLOCK, ATT_HEAD_DIM), F32), pltpu.VMEM((2, ATT_BLOCK, LANES), F32)],
        args=(qkv, qkv, qkv), jobs=jobs)


def _attention_bwd(name, qkv, do, rmat, jobs=()):
    _, T, D = qkv.shape
    heads = D // ATT_HEAD_DIM
    nq = T // ATT_BLOCK
    scale = 1.0 / math.sqrt(ATT_HEAD_DIM)

    def body(q_ref, k_ref, v_ref, do_ref, r_ref, dqkv_ref, dk_s, dv_s, dq_s, left_s):
        rr = lax.broadcasted_iota(jnp.int32, (ATT_BLOCK, ATT_BLOCK), 0)
        cc = lax.broadcasted_iota(jnp.int32, (ATT_BLOCK, ATT_BLOCK), 1)
        upper = jnp.where(rr > cc, 1.0, 0.0).astype(BF16)
        lower = jnp.where(rr < cc, 1.0, 0.0).astype(BF16)
        lane = lax.broadcasted_iota(jnp.int32, (ATT_BLOCK, LANES), 1)
        dk_s[...] = jnp.zeros_like(dk_s)
        dv_s[...] = jnp.zeros_like(dv_s)

        def tile(slot, qb, q0, q, dov, it, first):
            k0, kt, z, mask, n = _att_tile(q, k_ref, q0, qb, it, scale)
            vt = v_ref[pl.ds(k0, ATT_TILE), :]
            run = jnp.sum(jnp.where(lane == it, r_ref[pl.ds(q0, ATT_BLOCK), :], 0.0), axis=1, keepdims=True)
            suffix, _ = _suffix_in_tile(n, upper, run)
            s = z + n
            w = jnp.where(mask, jnp.exp(s + suffix), 0.0)
            e = w * lax.dot_general(dov, vt, DIMS_NT, preferred_element_type=F32)
            es = [jnp.sum(_sub(e, g), axis=1, keepdims=True) for g in range(ATT_GROUP)]
            pre = [jnp.zeros((ATT_BLOCK, LANES), F32) if first else left_s[slot]]
            for g in range(ATT_GROUP):
                pre.append(pre[g] + es[g])
            before = _unstack_rows(_split_dot(_stack_rows(e), lower), pre)
            sig = jnp.exp(s)
            dz = (jnp.where(mask, e * (1.0 - sig) - before * sig, 0.0) * scale).astype(BF16)
            dq = jnp.dot(dz, kt, preferred_element_type=F32)
            if first:
                dq_s[slot] = dq
            else:
                dq_s[slot] += dq
            dk_s[pl.ds(k0, ATT_TILE), :] += lax.dot_general(dz, q, DIMS_TN, preferred_element_type=F32)
            dv_s[pl.ds(k0, ATT_TILE), :] += lax.dot_general(w.astype(BF16), dov, DIMS_TN, preferred_element_type=F32)
            left_s[slot] = pre[ATT_GROUP]

        def pair_loop(p, _):
            blocks = []
            for slot in range(2):
                qb = 2 * p + slot
                q0 = pl.multiple_of(qb * ATT_BLOCK, ATT_BLOCK)
                n_tiles = (qb + ATT_GROUP) // ATT_GROUP
                seen = ((jnp.max(r_ref[pl.ds(q0, ATT_BLOCK), :], axis=0, keepdims=True) > 0.5 * ATT_UNVISITED)
                        & (lane[0:1, :] < n_tiles))
                n_visited = jnp.maximum(jnp.sum(jnp.where(seen, 1.0, 0.0)).astype(jnp.int32), 1)
                blocks.append((qb, q0, q_ref[pl.ds(q0, ATT_BLOCK), :], do_ref[pl.ds(q0, ATT_BLOCK), :], n_visited))
            for slot in range(2):
                qb, q0, q, dov, n_visited = blocks[slot]
                tile(slot, qb, q0, q, dov, n_visited - 1, True)
            for slot in range(2):
                qb, q0, q, dov, n_visited = blocks[slot]
                lax.fori_loop(1, n_visited, lambda j, c: (tile(slot, qb, q0, q, dov, n_visited - 1 - j, False), c)[1], 0)
                dqkv_ref[0, pl.ds(q0, ATT_BLOCK), :] = dq_s[slot].astype(dqkv_ref.dtype)
            return 0

        lax.fori_loop(0, nq // 2, pair_loop, 0)
        dqkv_ref[1, :, :] = dk_s[...].astype(dqkv_ref.dtype)
        dqkv_ref[2, :, :] = dv_s[...].astype(dqkv_ref.dtype)

    return _carry_call(
        body, name=name, steps=heads,
        in_specs=[_head_spec(T, 0, heads), _head_spec(T, 1, heads), _head_spec(T, 2, heads),
                  pl.BlockSpec((T, ATT_HEAD_DIM), lambda h: (0, h)), pl.BlockSpec((None, T, LANES), lambda h: (h, 0, 0))],
        out_specs=[pl.BlockSpec((3, T, ATT_HEAD_DIM), lambda h: (0, 0, h))],
        out_shape=[jax.ShapeDtypeStruct((3, T, D), BF16)],
        scratch_shapes=[pltpu.VMEM((T, ATT_HEAD_DIM), F32), pltpu.VMEM((T, ATT_HEAD_DIM), F32),
                        pltpu.VMEM((2, ATT_BLOCK, ATT_HEAD_DIM), F32), pltpu.VMEM((2, ATT_BLOCK, LANES), F32)],
        args=(qkv, qkv, qkv, do, rmat), jobs=jobs)


def _block_diag_pairs(w):
    h = w.shape[0]
    wp = w.reshape(h // 2, 2, RG_HEAD_DIM, RG_HEAD_DIM)
    z = jnp.zeros_like(wp[:, 0])
    top = jnp.concatenate([wp[:, 0], z], axis=2)
    bot = jnp.concatenate([z, wp[:, 1]], axis=2)
    return jnp.concatenate([top, bot], axis=1)


def _diag_pairs(g):
    n = g.shape[0]
    a = g[:, :RG_HEAD_DIM, :RG_HEAD_DIM]
    b = g[:, RG_HEAD_DIM:, RG_HEAD_DIM:]
    return jnp.stack([a, b], axis=1).reshape(2 * n, RG_HEAD_DIM, RG_HEAD_DIM)


class _Weights:
    def __init__(self, full, shards=None, plan=None):
        self.full, self.shards, self.plan = dict(full), shards or {}, plan or {}
        self.partial, self.rows = {}, {}

    def __getitem__(self, name):
        return self.full[name]

    def jobs(self, call):
        return [_gather_job(self.shards[n], self.partial.get(n), lo, hi, parts)
                for n, lo, hi, parts in self.plan.get(call, ())]

    def deliver(self, call, outs):
        for (n, lo, hi, parts), g in zip(self.plan.get(call, ()), outs):
            self.partial[n] = g
            self.rows[n] = self.rows.get(n, 0) + hi - lo
            if self.rows[n] == parts:
                self.full[n] = _gathered_layout(n, g)


def _gathered_layout(name, g):
    if name in ("w_in", "w_qkv", "w_out", "w_o"):
        return g.reshape(g.shape[0] * g.shape[1], g.shape[2])
    return g


class _Grads:
    def __init__(self, lands=None, plan=None):
        self.lands, self.plan = dict(lands) if lands else None, plan or {}
        self.ready, self.sent = {}, {}

    def put(self, name, arr):
        self.ready[name] = arr

    def jobs(self, call):
        if self.lands is None:
            return []
        return [_exchange_job(self.ready[n], self.lands[n], lo, hi, parts) for n, lo, hi, parts in self.plan.get(call, ())]

    def deliver(self, call, outs):
        for (n, lo, hi, parts), o in zip(self.plan.get(call, ()), outs):
            assert self.sent.get(n, (0, parts)) == (lo, parts), (call, n)
            self.lands[n] = o
            self.sent[n] = (hi, parts)

    def flush(self, name):
        if self.lands is None:
            return
        rest = []
        for n in self.ready:
            lo, parts = self.sent.get(n, (0, 1))
            if lo < parts:
                rest.append((n, lo, parts, parts))
        if rest:
            outs = _run_jobs(name, [_exchange_job(self.ready[n], self.lands[n], lo, hi, parts) for n, lo, hi, parts in rest])
            for (n, _, hi, parts), o in zip(rest, outs):
                self.lands[n] = o
                self.sent[n] = (hi, parts)


def _mlp_fwd(tag, h, wts, run):
    T, D = h.shape
    w_up = wts["up" + tag]
    fb = w_up.shape[2]
    F = fb * N_DEV
    tm, tn, tk = _tile(T, MM_TM), _tile(fb, MM_TN), _tile(D, MM_TK)
    nb = fb // tn

    def up_epilogue(u):
        r = jnp.maximum(u, 0.0)
        return u, r * r

    o_spec = pl.BlockSpec((tm, tn), lambda i, j, k: (i, j))
    u, act = run(
        _matmul, f"mlp_up_l{tag}",
        [(h, pl.BlockSpec((tm, tk), lambda i, j, k: (i, k))),
         (w_up, pl.BlockSpec((None, tk, tn), lambda i, j, k: (j // nb, k, j % nb)))],
        [(jax.ShapeDtypeStruct((T, F), BF16), o_spec), (jax.ShapeDtypeStruct((T, F), BF16), o_spec)],
        (T // tm, F // tn, D // tk), DIMS_NN, (tm, tn), up_epilogue, n_main=2)
    w_down = wts["down" + tag].reshape(F, D)
    m = run(_mm_nn, f"mlp_down_l{tag}", act, w_down, BF16)
    return u, act, m


def _mlp_bwd(tag, h, u, act, dm, wts, grads, run):
    T, D = h.shape
    w_up, w_down = wts["up" + tag], wts["down" + tag]
    fb = w_up.shape[2]
    F = fb * N_DEV
    grads.put("down" + tag, run(_mm_tn, f"mlp_down_dw_l{tag}", act, dm, BF16).reshape(N_DEV, fb, D))
    tm, tn, tk = _tile(T, MM_TM), _tile(fb, MM_TN), _tile(D, MM_TK)
    nb = fb // tn
    o_spec = pl.BlockSpec((tm, tn), lambda i, j, k: (i, j))
    du = run(
        _matmul, f"mlp_down_dx_l{tag}",
        [(dm, pl.BlockSpec((tm, tk), lambda i, j, k: (i, k))),
         (w_down, pl.BlockSpec((None, tn, tk), lambda i, j, k: (j // nb, j % nb, k))),
         (u, o_spec)],
        [(jax.ShapeDtypeStruct((T, F), BF16), o_spec)],
        (T // tm, F // tn, D // tk), DIMS_NT, (tm, tn),
        lambda r, uv: (r * (2.0 * jnp.maximum(uv.astype(F32), 0.0)),))
    grads.put("up" + tag, run(_mm_tn, f"mlp_up_dw_l{tag}", h, du, BF16, out_blocks=N_DEV))
    tn2 = _tile(D, MM_TN)
    pair = 2 if MM_TK >= 2 * fb else 1
    return run(
        _matmul, f"mlp_up_dx_l{tag}",
        [(du, pl.BlockSpec((tm, pair * fb), lambda i, j, k: (i, k))),
         (w_up, pl.BlockSpec((pair, tn2, fb), lambda i, j, k: (k, j, 0)))],
        [(jax.ShapeDtypeStruct((T, D), BF16), pl.BlockSpec((tm, tn2), lambda i, j, k: (i, j)))],
        (T // tm, D // tn2, N_DEV // pair), DIMS_NT, (tm, tn2), None)


def _local_step(x, target, gains, conv_a, conv_b, conv_b_bias, rg_w_a, rg_b_a, rg_w_x, rg_b_x, rg_lambda, wts, grads):
    T, D = x.shape
    g = lambda l, i: gains[l, i][None, :]
    wa_p = _block_diag_pairs(rg_w_a).astype(BF16)
    wx_p = _block_diag_pairs(rg_w_x).astype(BF16)

    def run(fn, name, *args, n_main=1, **kw):
        jw, jg = wts.jobs(name), grads.jobs(name)
        res = fn(name, *args, jobs=jw + jg, **kw)
        main, jo = res[:n_main], res[n_main:]
        wts.deliver(name, jo[:len(jw)])
        grads.deliver(name, jo[len(jw):])
        return main[0] if n_main == 1 else main

    h0 = run(_norm_fwd, "norm_in", x, g(0, 0))
    proj = run(_mm_nt, "w_in_fwd", h0, wts["w_in"], F32, out_seg=5)
    y = run(_mixer_b_fwd, "mixer_b_fwd", proj, conv_b, conv_b_bias, wa_p, rg_b_a, wx_p, rg_b_x, rg_lambda,
            _mixer_a_fwd(proj, conv_a))
    mix0 = run(_mm_nn, "w_out_fwd", y, wts["w_out"], BF16, a_seg=2)
    x1, h1 = run(_resid_norm, "resid_mix0", x, mix0, g(0, 1), g(0, 2), n_main=2)
    u0, act0, m0 = _mlp_fwd("0", h1, wts, run)
    x2, h2 = run(_resid_norm, "resid_mlp0", x1, m0, g(0, 3), g(1, 0), n_main=2)
    qkv = run(_mm_nt, "w_qkv_fwd", h2, wts["w_qkv"], BF16, out_seg=3)
    o, rmat = run(_attention_fwd, "attention_fwd", qkv, n_main=2)
    mix1 = run(_mm_nn, "w_o_fwd", o, wts["w_o"], BF16)
    x3, h3 = run(_resid_norm, "resid_mix1", x2, mix1, g(1, 1), g(1, 2), n_main=2)
    u1, act1, m1 = _mlp_fwd("1", h3, wts, run)
    dx4, dm1, dg13, sq = _final_loss("loss", x3, m1, g(1, 3), target)

    dh3 = _mlp_bwd("1", h3, u1, act1, dm1, wts, grads, run)
    dx3, dmix1, dg12, dg11 = run(_norm_bwd_pair, "norm_bwd_x3", x3, g(1, 2), dh3, dx4, mix1, g(1, 1), n_main=4)
    grads.put("w_o", run(_mm_tn, "w_o_dw", o, dmix1, BF16).reshape(N_DEV, D // N_DEV, D))
    do = run(_mm_nt, "w_o_dx", dmix1, wts["w_o"], BF16)
    dqkv = run(_attention_bwd, "attention_bwd", qkv, do, rmat)
    grads.put("w_qkv", run(_mm_tn, "w_qkv_dw", dqkv, h2, BF16, a_seg=3).reshape(N_DEV, 3 * D // N_DEV, D))
    dh2 = run(_mm_nn, "w_qkv_dx", dqkv, wts["w_qkv"], BF16, a_seg=3)
    dx2, dm0, dg10, dg03 = run(_norm_bwd_pair, "norm_bwd_x2", x2, g(1, 0), dh2, dx3, m0, g(0, 3), n_main=4)
    dh1 = _mlp_bwd("0", h1, u0, act0, dm0, wts, grads, run)
    dx1, dmix0, dg02, dg01 = run(_norm_bwd_pair, "norm_bwd_x1", x1, g(0, 2), dh1, dx2, mix0, g(0, 1), n_main=4)
    grads.put("w_out", run(_mm_tn, "w_out_dw", y, dmix0, BF16, a_seg=2).reshape(N_DEV, D // N_DEV, D))
    dy = run(_mm_nt, "w_out_dx", dmix0, wts["w_out"], F32, out_seg=2)
    dproj_a, dconv_a = _mixer_a_bwd(proj, conv_a, dy)
    dproj, sm_b, dwa_p, dwx_p = run(_mixer_b_bwd, "mixer_b_bwd", proj, conv_b, conv_b_bias, wa_p, rg_b_a, wx_p, rg_b_x,
                                    rg_lambda, dy, dproj_a, n_main=4)
    grads.put("w_in", run(_mm_tn, "w_in_dw", dproj, h0, BF16, a_seg=5).reshape(N_DEV, 5 * D // (2 * N_DEV), D))
    dh0 = run(_mm_nn, "w_in_dx", dproj, wts["w_in"], BF16, a_seg=5)
    dx0, dg00 = run(_norm_bwd, "norm_bwd_x0", x, g(0, 0), dh0, dx1, n_main=2)

    C = D // 2
    lanes_to_vec = lambda t, row: t[:, row, :].reshape(1, C)
    small = {
        "norm_gains": jnp.concatenate([dg00, dg01, dg02, dg03, dg10, dg11, dg12, dg13], axis=0).reshape(2, 4, D),
        "conv_a": jnp.transpose(dconv_a[:, :3, :], (1, 0, 2)).reshape(3, C),
        "conv_b": jnp.transpose(sm_b[:, :4, :], (1, 0, 2)).reshape(4, C),
        "conv_b_bias": lanes_to_vec(sm_b, _ROW_BIAS),
        "rg_w_a": _diag_pairs(dwa_p),
        "rg_b_a": lanes_to_vec(sm_b, _ROW_BA),
        "rg_w_x": _diag_pairs(dwx_p),
        "rg_b_x": lanes_to_vec(sm_b, _ROW_BX),
        "rg_lambda": lanes_to_vec(sm_b, _ROW_LAM),
    }
    return sq[0, 0], dx0, small


def _my_index():
    return 4 * lax.axis_index("x") + 2 * lax.axis_index("y") + lax.axis_index("c")


def _peers():
    x, y, c = lax.axis_index("x"), lax.axis_index("y"), lax.axis_index("c")
    out = []
    for k in range(1, N_DEV):
        px = x ^ ((k >> 2) & 1)
        py = y ^ ((k >> 1) & 1)
        pc = c ^ (k & 1)
        out.append(((px, py, pc), 4 * px + 2 * py + pc))
    return out


GATHER_PLAN = {
    "norm_in": (("w_in", 0, 1, 1),),
    "w_in_fwd": (("w_out", 0, 1, 1), ("up0", 0, 1, 4)),
    "mixer_b_fwd": (("up0", 1, 3, 4),),
    "w_out_fwd": (("up0", 3, 4, 4),),
    "resid_mix0": (("down0", 0, 1, 4),),
    "mlp_up_l0": (("down0", 1, 4, 4),),
    "mlp_down_l0": (("w_qkv", 0, 1, 1),),
    "resid_mlp0": (("up1", 0, 1, 4),),
    "w_qkv_fwd": (("w_o", 0, 1, 1), ("up1", 1, 2, 4)),
    "attention_fwd": (("up1", 2, 4, 4), ("down1", 0, 2, 4)),
    "mlp_up_l1": (("down1", 2, 4, 4),),
}
EXCHANGE_PLAN = {
    "mlp_down_dx_l1": (("down1", 0, 3, 8),), "mlp_up_dw_l1": (("down1", 3, 6, 8),),
    "mlp_up_dx_l1": (("down1", 6, 8, 8), ("up1", 0, 1, 8)),
    "attention_bwd": (("up1", 1, 8, 8), ("w_o", 0, 1, 1)),
    "w_qkv_dx": (("w_qkv", 0, 2, 6),), "norm_bwd_x2": (("w_qkv", 2, 3, 6),), "mlp_down_dw_l0": (("w_qkv", 3, 6, 6),),
    "mlp_down_dx_l0": (("down0", 0, 3, 8),), "mlp_up_dw_l0": (("down0", 3, 6, 8),),
    "mlp_up_dx_l0": (("down0", 6, 8, 8), ("up0", 0, 1, 8)),
    "norm_bwd_x1": (("up0", 1, 2, 8),), "w_out_dw": (("up0", 2, 3, 8),), "w_out_dx": (("w_out", 0, 1, 2),),
    "mixer_b_bwd": (("up0", 3, 7, 8),),
    "w_in_dw": (("up0", 7, 8, 8), ("w_out", 1, 2, 2)),
    "w_in_dx": (("w_in", 0, 2, 4),),
    "norm_bwd_x0": (("w_in", 2, 3, 4),),
    "gather_small_grads": (("w_in", 3, 4, 4),),
}


def _job_sems():
    return [pltpu.SemaphoreType.DMA((N_DEV - 1,)), pltpu.SemaphoreType.DMA((N_DEV - 1,)), pltpu.SemaphoreType.DMA((1,))]


def _gather_job(shard, prev=None, lo=0, hi=1, parts=1):
    n = shard.shape[0] // parts
    assert n * parts == shard.shape[0]
    rows = pl.ds(lo * n, (hi - lo) * n)

    def ctx():
        x, y, c = lax.axis_index("x"), lax.axis_index("y"), lax.axis_index("c")
        chips = [(1 - x, y), (x, 1 - y), (1 - x, 1 - y)]
        return x, y, c, chips

    def idx(px, py, pc):
        return 4 * px + 2 * py + pc

    def copy(src, out, sems, k, block, to):
        return pltpu.make_async_remote_copy(
            src_ref=out.at[block, rows] if src is None else src.at[rows], dst_ref=out.at[block, rows],
            send_sem=sems[0].at[k], recv_sem=sems[1].at[k], device_id=to, device_id_type=MESH)

    def start(ins, outs, sems):
        x, y, c, chips = ctx()
        src, out = ins[0], outs[0]
        me = idx(x, y, c)
        pltpu.make_async_copy(src.at[rows], out.at[me, rows], sems[2].at[0]).start()
        copy(src, out, sems, 0, me, (x, y, 1 - c)).start()
        for j, (px, py) in enumerate(chips):
            copy(src, out, sems, 1 + j, me, (px, py, c)).start()

    def mid(ins, outs, sems):
        x, y, c, chips = ctx()
        out = outs[0]
        for j, (px, py) in enumerate(chips):
            copy(None, out, sems, 1 + j, idx(px, py, c), (x, y, c)).wait_recv()
            copy(None, out, sems, 4 + j, idx(px, py, c), (x, y, 1 - c)).start()

    def end(ins, outs, sems):
        x, y, c, chips = ctx()
        src, out = ins[0], outs[0]
        me = (x, y, c)
        copy(None, out, sems, 0, idx(x, y, 1 - c), me).wait_recv()
        for j, (px, py) in enumerate(chips):
            copy(None, out, sems, 4 + j, idx(px, py, 1 - c), me).wait_recv()
        for k in range(N_DEV - 1):
            copy(src, out, sems, k, idx(x, y, c), me).wait_send()
        pltpu.make_async_copy(src.at[rows], out.at[idx(x, y, c), rows], sems[2].at[0]).wait()

    out_shape = jax.ShapeDtypeStruct((N_DEV,) + shard.shape, shard.dtype)
    if prev is None:
        return _Job([shard], [out_shape], _job_sems(), start, mid, end)
    return _Job([shard, prev], [out_shape], _job_sems(), start, mid, end, alias={1: 0})


def _exchange_job(src, land, lo=0, hi=1, parts=1):
    n = src.shape[1] // parts
    assert n * parts == src.shape[1]

    def sl(ref, s):
        return ref.at[s, pl.ds(lo * n, (hi - lo) * n)]

    def start(ins, outs, sems):
        me = _my_index()
        pltpu.make_async_copy(sl(ins[0], me), sl(outs[0], me), sems[2].at[0]).start()
        for k, (pos, idx) in enumerate(_peers()):
            pltpu.make_async_remote_copy(
                src_ref=sl(ins[0], idx), dst_ref=sl(outs[0], me), send_sem=sems[0].at[k], recv_sem=sems[1].at[k],
                device_id=pos, device_id_type=MESH).start()

    def mid(ins, outs, sems):
        pass

    def end(ins, outs, sems):
        me = _my_index()
        for k, (pos, idx) in enumerate(_peers()):
            cp = pltpu.make_async_remote_copy(
                src_ref=sl(ins[0], idx), dst_ref=sl(outs[0], idx), send_sem=sems[0].at[k], recv_sem=sems[1].at[k],
                device_id=pos, device_id_type=MESH)
            cp.wait_recv()
            cp.wait_send()
        pltpu.make_async_copy(sl(ins[0], me), sl(outs[0], me), sems[2].at[0]).wait()

    return _Job([src, land], [jax.ShapeDtypeStruct(land.shape, land.dtype)], _job_sems(), start, mid, end, alias={1: 0})


def _adamw_math(w, g, m, v):
    m = ADAM_B1 * m + (1.0 - ADAM_B1) * g
    v = ADAM_B2 * v + (1.0 - ADAM_B2) * (g * g)
    m_hat = m / (1.0 - ADAM_B1 ** ADAM_STEP)
    v_hat = v / (1.0 - ADAM_B2 ** ADAM_STEP)
    delta = -ADAM_LR * (m_hat / (jnp.sqrt(v_hat) + ADAM_EPS) + ADAM_WD * w)
    return delta, m, v


def _sum_slots(ref):
    g = ref[0].astype(F32)
    for s in range(1, N_DEV):
        g = g + ref[s].astype(F32)
    return g


def _adamw_big(name, lands, w, m, v, jobs=(), transposed=False):
    L, R, C = w.shape
    assert len(lands) == L
    tr = _tile(R, max(LANES, (256 * 1024) // C))
    nr = R // tr

    def body(*refs):
        l_refs = refs[:L]
        w_ref, m_ref, v_ref, g_ref, d_ref, nm_ref, nv_ref = refs[L:]
        for li in range(L):
            @pl.when(pl.program_id(0) // nr == li)
            def _(li=li):
                g = _sum_slots(l_refs[li])
                if transposed:
                    g = g.T
                d, nm, nv = _adamw_math(w_ref[...], g, m_ref[...], v_ref[...])
                g_ref[...] = g
                d_ref[...] = d
                nm_ref[...] = nm
                nv_ref[...] = nv

    def land_spec(li):
        if transposed:
            return pl.BlockSpec((N_DEV, C, tr), lambda s: (0, 0, jnp.where(s // nr == li, s % nr, 0)))
        return pl.BlockSpec((N_DEV, tr, C), lambda s: (0, jnp.where(s // nr == li, s % nr, 0), 0))

    row = pl.BlockSpec((None, tr, C), lambda s: (s // nr, s % nr, 0))
    return _carry_call(
        body, name=name, steps=L * nr, in_specs=[land_spec(li) for li in range(L)] + [row, row, row],
        out_specs=[row] * 4, out_shape=[jax.ShapeDtypeStruct((L, R, C), F32)] * 4, scratch_shapes=[],
        args=(*lands, w, m, v), jobs=jobs)


def _sum8(name, slots):
    _, R, C = slots.shape

    def body(s_ref, o_ref):
        o_ref[...] = _sum_slots(s_ref)

    return pl.pallas_call(body, name=name, out_shape=jax.ShapeDtypeStruct((R, C), F32))(slots)


def _adamw_small(name, g, w, m, v):
    def body(g_ref, w_ref, m_ref, v_ref, d_ref, nm_ref, nv_ref):
        d, nm, nv = _adamw_math(w_ref[...], g_ref[...], m_ref[...], v_ref[...])
        d_ref[...] = d
        nm_ref[...] = nm
        nv_ref[...] = nv

    return pl.pallas_call(body, name=name, out_shape=[jax.ShapeDtypeStruct(w.shape, F32)] * 3)(g, w, m, v)


def _pack_rows(arrs):
    parts, spans, r0 = [], [], 0
    for a in arrs:
        flat = a.astype(F32).reshape(-1)
        rows = -(-flat.shape[0] // LANES)
        rows = -(-rows // SUBLANES) * SUBLANES
        flat = jnp.pad(flat, (0, rows * LANES - flat.shape[0]))
        parts.append(flat.reshape(rows, LANES))
        spans.append((r0, rows, a.shape))
        r0 += rows
    return jnp.concatenate(parts, axis=0), spans


def _unpack_rows(buf, span):
    r0, rows, shape = span
    n = math.prod(shape)
    return buf[..., r0:r0 + rows, :].reshape(buf.shape[:-2] + (rows * LANES,))[..., :n].reshape(buf.shape[:-2] + shape)


def _from_col_blocks(wb):
    B, K, n = wb.shape
    return jnp.transpose(wb, (1, 0, 2)).reshape(K, B * n)


def kernel(x, norm_gains, hyb_w_in, hyb_conv_a, hyb_conv_b, hyb_conv_b_bias, hyb_rg_w_a, hyb_rg_b_a, hyb_rg_w_x, hyb_rg_b_x, hyb_rg_lambda, hyb_w_out, sb_w_qkv, sb_w_o, mlp_w_up, mlp_w_down, loss_target, m_norm_gains, m_hyb_w_in, m_hyb_conv_a, m_hyb_conv_b, m_hyb_conv_b_bias, m_hyb_rg_w_a, m_hyb_rg_b_a, m_hyb_rg_w_x, m_hyb_rg_b_x, m_hyb_rg_lambda, m_hyb_w_out, m_sb_w_qkv, m_sb_w_o, m_mlp_w_up, m_mlp_w_down, v_norm_gains, v_hyb_w_in, v_hyb_conv_a, v_hyb_conv_b, v_hyb_conv_b_bias, v_hyb_rg_w_a, v_hyb_rg_b_a, v_hyb_rg_w_x, v_hyb_rg_b_x, v_hyb_rg_lambda, v_hyb_w_out, v_sb_w_qkv, v_sb_w_o, v_mlp_w_up, v_mlp_w_down):
    T, D = x.shape[1], x.shape[2]
    me = _my_index()

    small_shards, small_spans = _pack_rows([norm_gains, hyb_conv_a[0], hyb_conv_b[0]])
    (small_all,) = _run_jobs("gather_small", [_gather_job(small_shards)])
    gains_b = _unpack_rows(small_all, small_spans[0])
    gains = jnp.transpose(gains_b, (1, 2, 0, 3)).reshape(2, 4, D)
    conv_a = _from_col_blocks(_unpack_rows(small_all, small_spans[1]))
    conv_b = _from_col_blocks(_unpack_rows(small_all, small_spans[2]))

    shards = {"w_in": hyb_w_in[0].T, "w_out": hyb_w_out[0], "w_qkv": sb_w_qkv[0].T, "w_o": sb_w_o[0],
              "up0": mlp_w_up[0], "up1": mlp_w_up[1], "down0": mlp_w_down[0], "down1": mlp_w_down[1]}
    shards = {n: s.astype(BF16) for n, s in shards.items()}
    wts = _Weights({}, shards, GATHER_PLAN)
    grads_big = _Grads({n: lax.empty((N_DEV,) + s.shape, BF16) for n, s in shards.items()}, EXCHANGE_PLAN)

    sq, grad_x, small = _local_step(
        x[0], loss_target[0], gains, conv_a, conv_b, hyb_conv_b_bias, hyb_rg_w_a[0], hyb_rg_b_a, hyb_rg_w_x[0],
        hyb_rg_b_x, hyb_rg_lambda, wts, grads_big)


    names = ["norm_gains", "hyb_w_in", "hyb_conv_a", "hyb_conv_b", "hyb_conv_b_bias", "hyb_rg_w_a", "hyb_rg_b_a",
             "hyb_rg_w_x", "hyb_rg_b_x", "hyb_rg_lambda", "hyb_w_out", "sb_w_qkv", "sb_w_o", "mlp_w_up", "mlp_w_down"]
    params = dict(zip(names, [norm_gains, hyb_w_in, hyb_conv_a, hyb_conv_b, hyb_conv_b_bias, hyb_rg_w_a, hyb_rg_b_a,
                              hyb_rg_w_x, hyb_rg_b_x, hyb_rg_lambda, hyb_w_out, sb_w_qkv, sb_w_o, mlp_w_up, mlp_w_down]))
    moms = dict(zip(names, [m_norm_gains, m_hyb_w_in, m_hyb_conv_a, m_hyb_conv_b, m_hyb_conv_b_bias, m_hyb_rg_w_a,
                            m_hyb_rg_b_a, m_hyb_rg_w_x, m_hyb_rg_b_x, m_hyb_rg_lambda, m_hyb_w_out, m_sb_w_qkv,
                            m_sb_w_o, m_mlp_w_up, m_mlp_w_down]))
    vars_ = dict(zip(names, [v_norm_gains, v_hyb_w_in, v_hyb_conv_a, v_hyb_conv_b, v_hyb_conv_b_bias, v_hyb_rg_w_a,
                             v_hyb_rg_b_a, v_hyb_rg_w_x, v_hyb_rg_b_x, v_hyb_rg_lambda, v_hyb_w_out, v_sb_w_qkv,
                             v_sb_w_o, v_mlp_w_up, v_mlp_w_down]))
    grads, deltas, new_m, new_v = {}, {}, {}, {}

    small_names = ["norm_gains", "hyb_conv_a", "hyb_conv_b", "hyb_conv_b_bias", "hyb_rg_w_a", "hyb_rg_b_a",
                   "hyb_rg_w_x", "hyb_rg_b_x", "hyb_rg_lambda"]
    small_keys = ["norm_gains", "conv_a", "conv_b", "conv_b_bias", "rg_w_a", "rg_b_a", "rg_w_x", "rg_b_x", "rg_lambda"]
    sg_buf, sg_spans = _pack_rows([small[k] for k in small_keys] + [sq.reshape(1)])
    sg_all, *tail = _run_jobs("gather_small_grads", [_gather_job(sg_buf)] + grads_big.jobs("gather_small_grads"))
    grads_big.deliver("gather_small_grads", tail)

    big_lands = {"mlp_w_down": ["down0", "down1"], "mlp_w_up": ["up0", "up1"], "sb_w_qkv": ["w_qkv"], "sb_w_o": ["w_o"],
                 "hyb_w_out": ["w_out"], "hyb_w_in": ["w_in"]}
    for nm, keys in big_lands.items():
        call = f"adamw_{nm}"
        if nm == "hyb_w_in":
            grads_big.flush("exchange_grads")
        jobs = grads_big.jobs(call)
        assert not {k for k in keys} & {e[0] for e in EXCHANGE_PLAN.get(call, ())}
        outs = _adamw_big(call, [grads_big.lands[k] for k in keys], params[nm], moms[nm], vars_[nm], jobs=jobs,
                          transposed=nm in ("hyb_w_in", "sb_w_qkv"))
        grads[nm], deltas[nm], new_m[nm], new_v[nm] = outs[:4]
        grads_big.deliver(call, outs[4:])

    sg_sum = _sum8("sum_small_grads", sg_all)
    full = {nm: _unpack_rows(sg_sum, sp) for nm, sp in zip(small_names, sg_spans)}
    loss = _unpack_rows(sg_sum, sg_spans[-1])[0] * (0.5 / D)
    cb = (D // 2) // N_DEV
    small_grads = {
        "norm_gains": lax.dynamic_slice_in_dim(full["norm_gains"], me * (D // N_DEV), D // N_DEV, axis=2),
        "hyb_conv_a": lax.dynamic_slice_in_dim(full["hyb_conv_a"], me * cb, cb, axis=1)[None],
        "hyb_conv_b": lax.dynamic_slice_in_dim(full["hyb_conv_b"], me * cb, cb, axis=1)[None],
        "hyb_conv_b_bias": full["hyb_conv_b_bias"],
        "hyb_rg_w_a": full["hyb_rg_w_a"][None],
        "hyb_rg_b_a": full["hyb_rg_b_a"],
        "hyb_rg_w_x": full["hyb_rg_w_x"][None],
        "hyb_rg_b_x": full["hyb_rg_b_x"],
        "hyb_rg_lambda": full["hyb_rg_lambda"],
    }
    pk = lambda d: _pack_rows([d[nm] for nm in small_names])
    g_buf, spans = pk(small_grads)
    w_buf, _ = pk(params)
    m_buf, _ = pk(moms)
    v_buf, _ = pk(vars_)
    d_buf, nm_buf, nv_buf = _adamw_small("adamw_small", g_buf, w_buf, m_buf, v_buf)
    for nm, sp in zip(small_names, spans):
        grads[nm] = small_grads[nm]
        deltas[nm], new_m[nm], new_v[nm] = _unpack_rows(d_buf, sp), _unpack_rows(nm_buf, sp), _unpack_rows(nv_buf, sp)

    return (loss, grad_x[None], *[grads[n] for n in names], *[deltas[n] for n in names],
            *[new_m[n] for n in names], *[new_v[n] for n in names])
```

```python
import math

import jax
import jax.numpy as jnp
from jax import lax
from jax.experimental import pallas as pl
from jax.experimental.pallas import tpu as pltpu

F32 = jnp.float32
BF16 = jnp.bfloat16

NORM_EPS = 1e-6
LRU_C = 8.0
ATT_HEAD_DIM = 128
RG_HEAD_DIM = 64
LANES = 128
SUBLANES = 8
N_DEV = 8
ADAM_LR = 0.001
ADAM_B1 = 0.9
ADAM_B2 = 0.999
ADAM_EPS = 1e-08
ADAM_WD = 0.01
ADAM_STEP = 10
VMEM_LIMIT = 56 * 1024 * 1024
MM_TK = 2048
MM_TM = 2048
MM_TN = 1024
MM_TK_TOKENS = 4096
MESH = pl.DeviceIdType.MESH


def _tile(n, pref):
    if n <= pref:
        return n
    t = (pref // LANES) * LANES
    while t > LANES and n % t:
        t -= LANES
    assert n % t == 0, (n, pref)
    return t


def _params(sem):
    return pltpu.CompilerParams(dimension_semantics=sem, vmem_limit_bytes=VMEM_LIMIT)


DIMS_NN = (((1,), (0,)), ((), ()))
DIMS_NT = (((1,), (1,)), ((), ()))
DIMS_TN = (((0,), (0,)), ((), ()))


_ANY = pl.BlockSpec(memory_space=pl.ANY)


class _Job:
    def __init__(self, ins, outs, sems, start, mid, end, alias=None):
        self.ins, self.outs, self.sems = ins, outs, sems
        self.start, self.mid, self.end = start, mid, end
        self.alias = alias or {}


def _mid_step(steps):
    return (9 * steps) // 10


def _job_plumbing(jobs, n_in, n_out):
    j_ins = [a for jb in jobs for a in jb.ins]
    j_outs = [o for jb in jobs for o in jb.outs]
    j_sems = [s for jb in jobs for s in jb.sems]
    aliases, pi, po = {}, 0, 0
    for jb in jobs:
        for i_in, i_out in jb.alias.items():
            aliases[n_in + pi + i_in] = n_out + po + i_out
        pi += len(jb.ins)
        po += len(jb.outs)
    return j_ins, j_outs, j_sems, aliases


def _job_phase(jobs, which, jin, jout, jsem):
    pi = po = ps = 0
    for jb in jobs:
        getattr(jb, which)(jin[pi:pi + len(jb.ins)], jout[po:po + len(jb.outs)], jsem[ps:ps + len(jb.sems)])
        pi, po, ps = pi + len(jb.ins), po + len(jb.outs), ps + len(jb.sems)


def _run_jobs(name, jobs):
    j_ins, j_outs, j_sems, aliases = _job_plumbing(jobs, 0, 0)
    n_ji, n_jo = len(j_ins), len(j_outs)

    def body(*refs):
        jin, jout, jsem = refs[:n_ji], refs[n_ji:n_ji + n_jo], refs[n_ji + n_jo:]
        for which in ("start", "mid", "end"):
            _job_phase(jobs, which, jin, jout, jsem)

    return pl.pallas_call(body, name=name, in_specs=[_ANY] * n_ji, out_specs=[_ANY] * n_jo, out_shape=j_outs,
                          scratch_shapes=j_sems, input_output_aliases=aliases)(*j_ins)


def _carry_call(body, *, name, steps, in_specs, out_specs, out_shape, scratch_shapes, args, jobs=(), aliases=None,
                sequential=False):
    n_in, n_out, n_sc = len(in_specs), len(out_shape), len(scratch_shapes)
    j_ins, j_outs, j_sems, j_aliases = _job_plumbing(jobs, n_in, n_out)
    n_ji, n_jo = len(j_ins), len(j_outs)

    def wrapped(*refs):
        ins, jin = refs[:n_in], refs[n_in:n_in + n_ji]
        o0 = n_in + n_ji
        outs, jout = refs[o0:o0 + n_out], refs[o0 + n_out:o0 + n_out + n_jo]
        s0 = o0 + n_out + n_jo
        scratch, jsem = refs[s0:s0 + n_sc], refs[s0 + n_sc:]
        step = pl.program_id(0)
        if jobs:
            pl.when(step == 0)(lambda: _job_phase(jobs, "start", jin, jout, jsem))
            pl.when(step == _mid_step(steps))(lambda: _job_phase(jobs, "mid", jin, jout, jsem))
        body(*ins, *outs, *scratch)
        if jobs:
            pl.when(step == steps - 1)(lambda: _job_phase(jobs, "end", jin, jout, jsem))

    return pl.pallas_call(
        wrapped, name=name, grid=(steps,),
        in_specs=list(in_specs) + [_ANY] * n_ji, out_specs=list(out_specs) + [_ANY] * n_jo,
        out_shape=list(out_shape) + j_outs, scratch_shapes=list(scratch_shapes) + j_sems,
        input_output_aliases={**(aliases or {}), **j_aliases},
        compiler_params=_params(("arbitrary",) if (jobs or sequential) else ("parallel",)))(*args, *j_ins)


def _matmul(name, ins, outs, grid, dims, acc_shape, epilogue=None, jobs=()):
    n_in, n_out, nk = len(ins), len(outs), grid[2]
    j_ins, j_outs, j_sems, aliases = _job_plumbing(jobs, n_in, n_out)
    n_ji, n_jo = len(j_ins), len(j_outs)
    total = grid[0] * grid[1] * grid[2]
    n_acc = 0 if nk == 1 else 1

    def body(*refs):
        a_ref, b_ref = refs[0], refs[1]
        extras = refs[2:n_in]
        jin = refs[n_in:n_in + n_ji]
        out_refs = refs[n_in + n_ji:n_in + n_ji + n_out]
        jout = refs[n_in + n_ji + n_out:n_in + n_ji + n_out + n_jo]
        jsem = refs[n_in + n_ji + n_out + n_jo + n_acc:]
        k = pl.program_id(2)
        step = (pl.program_id(0) * grid[1] + pl.program_id(1)) * grid[2] + k
        if jobs:
            pl.when(step == 0)(lambda: _job_phase(jobs, "start", jin, jout, jsem))
            pl.when(step == _mid_step(total))(lambda: _job_phase(jobs, "mid", jin, jout, jsem))

        def finish(r):
            res = epilogue(r, *[e[...] for e in extras]) if epilogue is not None else (r,)
            for o, v in zip(out_refs, res):
                o[...] = v.astype(o.dtype)

        if len(b_ref.shape) == 2:
            prod = lax.dot_general(a_ref[...], b_ref[...], dims, preferred_element_type=F32)
        else:
            kb = a_ref.shape[1] // b_ref.shape[0]
            prod = sum(lax.dot_general(a_ref[:, g * kb:(g + 1) * kb], b_ref[g], dims, preferred_element_type=F32)
                       for g in range(b_ref.shape[0]))
        if nk == 1:
            finish(prod)
        else:
            acc = refs[n_in + n_ji + n_out + n_jo]

            @pl.when(k == 0)
            def _():
                acc[...] = prod

            @pl.when((k > 0) & (k < nk - 1))
            def _():
                acc[...] += prod

            @pl.when(k == nk - 1)
            def _():
                finish(acc[...] + prod)

        if jobs:
            pl.when(step == total - 1)(lambda: _job_phase(jobs, "end", jin, jout, jsem))

    sem = ("arbitrary",) * 3 if jobs else ("parallel", "parallel", "arbitrary")
    res = pl.pallas_call(
        body, name=name, grid=grid,
        in_specs=[s for _, s in ins] + [_ANY] * n_ji,
        out_specs=[s for _, s in outs] + [_ANY] * n_jo,
        out_shape=[s for s, _ in outs] + j_outs,
        scratch_shapes=[pltpu.VMEM(acc_shape, F32)] * n_acc + j_sems,
        input_output_aliases=aliases,
        compiler_params=_params(sem),
    )(*[a for a, _ in ins], *j_ins)
    return res


def _mm_nn(name, a, b, out_dtype, *, a_seg=None, out_seg=None, tm=MM_TM, tn=MM_TN, tk=MM_TK, jobs=()):
    if a_seg:
        _, M, ks = a.shape
        K = ks * a_seg
    else:
        M, K = a.shape
        ks = K
    N = b.shape[1]
    ns = N // out_seg if out_seg else N
    tm, tn, tk = _tile(M, tm), _tile(ns, tn), _tile(ks, tk)
    nks, nns = ks // tk, ns // tn
    grid = (M // tm, N // tn, K // tk)
    if a_seg:
        a_spec = pl.BlockSpec((None, tm, tk), lambda i, j, k: (k // nks, i, k % nks))
    else:
        a_spec = pl.BlockSpec((tm, tk), lambda i, j, k: (i, k))
    b_spec = pl.BlockSpec((tk, tn), lambda i, j, k: (k, j))
    if out_seg:
        o_spec = pl.BlockSpec((None, tm, tn), lambda i, j, k: (j // nns, i, j % nns))
        o_shape = (out_seg, M, ns)
    else:
        o_spec = pl.BlockSpec((tm, tn), lambda i, j, k: (i, j))
        o_shape = (M, N)
    outs = [(jax.ShapeDtypeStruct(o_shape, out_dtype), o_spec)]
    return _matmul(name, [(a, a_spec), (b, b_spec)], outs, grid, DIMS_NN, (tm, tn), None, jobs)


def _mm_nt(name, a, b, out_dtype, *, a_seg=None, out_seg=None, tm=MM_TM, tn=MM_TN, tk=MM_TK, jobs=()):
    if a_seg:
        _, M, ks = a.shape
        K = ks * a_seg
    else:
        M, K = a.shape
        ks = K
    N = b.shape[0]
    ns = N // out_seg if out_seg else N
    tm, tn, tk = _tile(M, tm), _tile(ns, tn), _tile(ks, tk)
    nks, nns = ks // tk, ns // tn
    grid = (M // tm, N // tn, K // tk)
    if a_seg:
        a_spec = pl.BlockSpec((None, tm, tk), lambda i, j, k: (k // nks, i, k % nks))
    else:
        a_spec = pl.BlockSpec((tm, tk), lambda i, j, k: (i, k))
    b_spec = pl.BlockSpec((tn, tk), lambda i, j, k: (j, k))
    if out_seg:
        o_spec = pl.BlockSpec((None, tm, tn), lambda i, j, k: (j // nns, i, j % nns))
        o_shape = (out_seg, M, ns)
    else:
        o_spec = pl.BlockSpec((tm, tn), lambda i, j, k: (i, j))
        o_shape = (M, N)
    outs = [(jax.ShapeDtypeStruct(o_shape, out_dtype), o_spec)]
    return _matmul(name, [(a, a_spec), (b, b_spec)], outs, grid, DIMS_NT, (tm, tn), None, jobs)


def _mm_tn(name, a, b, out_dtype, *, a_seg=None, b_seg=None, out_blocks=None, tm=MM_TN, tn=MM_TN, tk=MM_TK_TOKENS,
           jobs=()):
    if a_seg:
        _, T, ms = a.shape
        M = ms * a_seg
    else:
        T, M = a.shape
        ms = M
    if b_seg:
        _, _, ns = b.shape
        N = ns * b_seg
    else:
        N = b.shape[1]
        ns = N
    nb_cols = N // out_blocks if out_blocks else N
    tm, tk = _tile(ms, tm), _tile(T, tk)
    tn = _tile(math.gcd(ns, nb_cols), tn)
    nms, nns, nbs = ms // tm, ns // tn, nb_cols // tn
    grid = (M // tm, N // tn, T // tk)
    if a_seg:
        a_spec = pl.BlockSpec((None, tk, tm), lambda i, j, k: (i // nms, k, i % nms))
    else:
        a_spec = pl.BlockSpec((tk, tm), lambda i, j, k: (k, i))
    if b_seg:
        b_spec = pl.BlockSpec((None, tk, tn), lambda i, j, k: (j // nns, k, j % nns))
    else:
        b_spec = pl.BlockSpec((tk, tn), lambda i, j, k: (k, j))
    if out_blocks:
        o_spec = pl.BlockSpec((None, tm, tn), lambda i, j, k: (j // nbs, i, j % nbs))
        o_shape = (out_blocks, M, nb_cols)
    else:
        o_spec = pl.BlockSpec((tm, tn), lambda i, j, k: (i, j))
        o_shape = (M, N)
    outs = [(jax.ShapeDtypeStruct(o_shape, out_dtype), o_spec)]
    return _matmul(name, [(a, a_spec), (b, b_spec)], outs, grid, DIMS_TN, (tm, tn), None, jobs)


def _rms(x):
    return lax.rsqrt(jnp.mean(x * x, axis=-1, keepdims=True) + NORM_EPS)


def _row_tile(T):
    return _tile(T, 256)


def _norm_fwd(name, x, g, jobs=()):
    T, D = x.shape
    tr = _row_tile(T)

    def body(x_ref, g_ref, h_ref):
        xv = x_ref[...]
        h_ref[...] = (xv * _rms(xv) * g_ref[...]).astype(h_ref.dtype)

    row = pl.BlockSpec((tr, D), lambda i: (i, 0))
    vec = pl.BlockSpec((1, D), lambda i: (0, 0))
    return _carry_call(body, name=name, steps=T // tr, in_specs=[row, vec], out_specs=[row],
                       out_shape=[jax.ShapeDtypeStruct((T, D), BF16)], scratch_shapes=[], args=(x, g), jobs=jobs)


def _resid_norm(name, x, br, g_post, g_next, jobs=()):
    T, D = x.shape
    tr = _row_tile(T)

    def body(x_ref, br_ref, gp_ref, gn_ref, xn_ref, h_ref):
        b = br_ref[...].astype(F32)
        xn = x_ref[...] + b * _rms(b) * gp_ref[...]
        xn_ref[...] = xn
        h_ref[...] = (xn * _rms(xn) * gn_ref[...]).astype(h_ref.dtype)

    row = pl.BlockSpec((tr, D), lambda i: (i, 0))
    vec = pl.BlockSpec((1, D), lambda i: (0, 0))
    return _carry_call(body, name=name, steps=T // tr, in_specs=[row, row, vec, vec], out_specs=[row, row],
                       out_shape=[jax.ShapeDtypeStruct((T, D), F32), jax.ShapeDtypeStruct((T, D), BF16)],
                       scratch_shapes=[], args=(x, br, g_post, g_next), jobs=jobs)


def _rms_bwd(xv, g, dyv):
    r = _rms(xv)
    xhat = xv * r
    gdy = dyv * g
    dx = r * (gdy - xhat * jnp.mean(gdy * xhat, axis=-1, keepdims=True))
    return dx, jnp.sum(dyv * xhat, axis=0, keepdims=True)


def _final_loss(name, x, br, g_post, target):
    T, D = x.shape
    tr = _row_tile(T)

    def body(x_ref, br_ref, gp_ref, t_ref, dy_ref, dbr_ref, dg_ref, ls_ref):
        b = br_ref[...].astype(F32)
        err = x_ref[...] + b * _rms(b) * gp_ref[...] - t_ref[...]
        dy = err * (1.0 / D)
        dy_ref[...] = dy
        dbr, dg = _rms_bwd(b, gp_ref[...], dy)
        dbr_ref[...] = dbr.astype(dbr_ref.dtype)

        @pl.when(pl.program_id(0) == 0)
        def _():
            ls_ref[...] = jnp.zeros_like(ls_ref)
            dg_ref[...] = jnp.zeros_like(dg_ref)

        ls_ref[...] += jnp.sum(err * err)
        dg_ref[...] += dg

    row = pl.BlockSpec((tr, D), lambda i: (i, 0))
    vec = pl.BlockSpec((1, D), lambda i: (0, 0))
    acc = pl.BlockSpec((SUBLANES, LANES), lambda i: (0, 0))
    return pl.pallas_call(body, name=name, grid=(T // tr,), in_specs=[row, row, vec, row],
                          out_specs=[row, row, vec, acc],
                          out_shape=[jax.ShapeDtypeStruct((T, D), F32), jax.ShapeDtypeStruct((T, D), BF16),
                                     jax.ShapeDtypeStruct((1, D), F32), jax.ShapeDtypeStruct((SUBLANES, LANES), F32)],
                          compiler_params=_params(("arbitrary",)))(x, br, g_post, target)


def _norm_bwd_pair(name, x, g_pre, dh, add, br, g_post, jobs=()):
    T, D = x.shape
    tr = _row_tile(T)

    def body(x_ref, gpre_ref, dh_ref, add_ref, br_ref, gpost_ref, dx_ref, dbr_ref, dgpre_ref, dgpost_ref):
        dx, dg_pre = _rms_bwd(x_ref[...], gpre_ref[...], dh_ref[...].astype(F32))
        dx = dx + add_ref[...]
        dx_ref[...] = dx
        dbr, dg_post = _rms_bwd(br_ref[...].astype(F32), gpost_ref[...], dx)
        dbr_ref[...] = dbr.astype(dbr_ref.dtype)

        @pl.when(pl.program_id(0) == 0)
        def _():
            dgpre_ref[...] = jnp.zeros_like(dgpre_ref)
            dgpost_ref[...] = jnp.zeros_like(dgpost_ref)

        dgpre_ref[...] += dg_pre
        dgpost_ref[...] += dg_post

    row = pl.BlockSpec((tr, D), lambda i: (i, 0))
    vec = pl.BlockSpec((1, D), lambda i: (0, 0))
    return _carry_call(body, name=name, steps=T // tr, in_specs=[row, vec, row, row, row, vec],
                       out_specs=[row, row, vec, vec],
                       out_shape=[jax.ShapeDtypeStruct((T, D), F32), jax.ShapeDtypeStruct((T, D), BF16),
                                  jax.ShapeDtypeStruct((1, D), F32), jax.ShapeDtypeStruct((1, D), F32)],
                       scratch_shapes=[], args=(x, g_pre, dh, add, br, g_post), jobs=jobs, sequential=True)


def _norm_bwd(name, x, g, dy, add, jobs=()):
    T, D = x.shape
    tr = _row_tile(T)

    def body(x_ref, g_ref, dy_ref, add_ref, dx_ref, dg_ref):
        dx, dg = _rms_bwd(x_ref[...], g_ref[...], dy_ref[...].astype(F32))
        dx_ref[...] = dx + add_ref[...]

        @pl.when(pl.program_id(0) == 0)
        def _():
            dg_ref[...] = jnp.zeros_like(dg_ref)

        dg_ref[...] += dg

    row = pl.BlockSpec((tr, D), lambda i: (i, 0))
    vec = pl.BlockSpec((1, D), lambda i: (0, 0))
    return _carry_call(body, name=name, steps=T // tr, in_specs=[row, vec, row, row], out_specs=[row, vec],
                       out_shape=[jax.ShapeDtypeStruct((T, D), F32), jax.ShapeDtypeStruct((1, D), F32)],
                       scratch_shapes=[], args=(x, g, dy, add), jobs=jobs, sequential=True)


HALO = SUBLANES
TIME_CHUNK = 512


def _chunks(T):
    tc = min(TIME_CHUNK, T)
    assert T % tc == 0 and tc % SUBLANES == 0
    return [(t0, tc) for t0 in range(0, T, tc)]


def _log_sigmoid(x):
    return -(jnp.maximum(-x, 0.0) + jnp.log(1.0 + jnp.exp(-jnp.abs(x))))


def _sigmoid(x):
    return 0.5 * jnp.tanh(0.5 * x) + 0.5


def _one_minus_exp(x):
    series = -x * (1.0 + x * (0.5 + x * (1.0 / 6.0 + x * (1.0 / 24.0))))
    return jnp.where(x > -0.01, series, 1.0 - jnp.exp(x))


_GELU_C = math.sqrt(2.0 / math.pi)


def _gelu(x):
    return 0.5 * x * (1.0 + jnp.tanh(_GELU_C * (x + 0.044715 * x * x * x)))


def _gelu_grad(x):
    th = jnp.tanh(_GELU_C * (x + 0.044715 * x * x * x))
    return 0.5 * (1.0 + th) + 0.5 * x * (1.0 - th * th) * _GELU_C * (1.0 + 3.0 * 0.044715 * x * x)


def _tile_scan(a, b, reverse):
    rows = a.shape[0]
    pos = lax.broadcasted_iota(jnp.int32, a.shape, 0) & (SUBLANES - 1)
    for d in (1, 2, 4):
        if reverse:
            ok = pos < SUBLANES - d
            shift = rows - d
        else:
            ok = pos >= d
            shift = d
        a_sh = jnp.where(ok, pltpu.roll(a, shift, 0), 1.0)
        b_sh = jnp.where(ok, pltpu.roll(b, shift, 0), 0.0)
        b = a * b_sh + b
        a = a * a_sh
    return a, b


def _carry_scan(a_s, b_s, T, reverse):
    n = T // SUBLANES
    edge = 0 if reverse else SUBLANES - 1

    def step(j, carry):
        g = (n - 1 - j) if reverse else j
        r = pl.multiple_of(g * SUBLANES, SUBLANES)
        h = b_s[pl.ds(r, SUBLANES), :] + a_s[pl.ds(r, SUBLANES), :] * carry
        b_s[pl.ds(r, SUBLANES), :] = h
        return jnp.broadcast_to(h[edge:edge + 1, :], h.shape)

    lax.fori_loop(0, n, step, jnp.zeros((SUBLANES, a_s.shape[1]), F32))


def _seg_spec(T, seg, nblk):
    return pl.BlockSpec((None, T, LANES), lambda c: (seg, 0, c))


def _rows_to_tile(rows):
    idx = lax.broadcasted_iota(jnp.int32, (SUBLANES, LANES), 0)
    out = jnp.zeros((SUBLANES, LANES), F32)
    for k, r in enumerate(rows):
        out = jnp.where(idx == k, r, out)
    return out


def _mixer_a_fwd(proj, conv_a):
    _, T, C = proj.shape
    nblk = C // LANES
    chunks = _chunks(T)

    def body(bg_ref, cg_ref, ax_ref, w_ref, y_ref, p_s):
        p_s[pl.ds(0, HALO), :] = jnp.zeros((HALO, LANES), F32)
        for t0, tc in chunks:
            p_s[pl.ds(HALO + t0, tc), :] = cg_ref[pl.ds(t0, tc), :] * ax_ref[pl.ds(t0, tc), :]
        w = w_ref[...]
        for t0, tc in chunks:
            c = (w[2:3, :] * p_s[pl.ds(HALO + t0, tc), :] + w[1:2, :] * p_s[pl.ds(HALO + t0 - 1, tc), :]
                 + w[0:1, :] * p_s[pl.ds(HALO + t0 - 2, tc), :])
            y_ref[pl.ds(t0, tc), :] = (bg_ref[pl.ds(t0, tc), :] * c).astype(y_ref.dtype)

    return pl.pallas_call(
        body, name="mixer_a_fwd", grid=(nblk,),
        in_specs=[_seg_spec(T, 0, nblk), _seg_spec(T, 1, nblk), _seg_spec(T, 2, nblk),
                  pl.BlockSpec((3, LANES), lambda c: (0, c))],
        out_specs=_seg_spec(T, 0, nblk),
        out_shape=jax.ShapeDtypeStruct((2, T, C), BF16),
        scratch_shapes=[pltpu.VMEM((T + HALO, LANES), F32)],
        compiler_params=_params(("parallel",)))(proj, proj, proj, conv_a)


def _mixer_a_bwd(proj, conv_a, dy):
    _, T, C = proj.shape
    nblk = C // LANES
    chunks = _chunks(T)

    def body(bg_ref, cg_ref, ax_ref, w_ref, dy_ref, dp_ref, dw_ref, p_s, dc_s):
        p_s[pl.ds(0, HALO), :] = jnp.zeros((HALO, LANES), F32)
        dc_s[pl.ds(T, HALO), :] = jnp.zeros((HALO, LANES), F32)
        for t0, tc in chunks:
            p_s[pl.ds(HALO + t0, tc), :] = cg_ref[pl.ds(t0, tc), :] * ax_ref[pl.ds(t0, tc), :]
        w = w_ref[...]
        for t0, tc in chunks:
            c = (w[2:3, :] * p_s[pl.ds(HALO + t0, tc), :] + w[1:2, :] * p_s[pl.ds(HALO + t0 - 1, tc), :]
                 + w[0:1, :] * p_s[pl.ds(HALO + t0 - 2, tc), :])
            dyv = dy_ref[pl.ds(t0, tc), :]
            dp_ref[0, pl.ds(t0, tc), :] = (dyv * c).astype(dp_ref.dtype)
            dc_s[pl.ds(t0, tc), :] = dyv * bg_ref[pl.ds(t0, tc), :]
        dw = [jnp.zeros((1, LANES), F32) for _ in range(3)]
        for t0, tc in chunks:
            dc = dc_s[pl.ds(t0, tc), :]
            dpv = w[2:3, :] * dc + w[1:2, :] * dc_s[pl.ds(t0 + 1, tc), :] + w[0:1, :] * dc_s[pl.ds(t0 + 2, tc), :]
            dp_ref[1, pl.ds(t0, tc), :] = (dpv * ax_ref[pl.ds(t0, tc), :]).astype(dp_ref.dtype)
            dp_ref[2, pl.ds(t0, tc), :] = (dpv * cg_ref[pl.ds(t0, tc), :]).astype(dp_ref.dtype)
            for k in range(3):
                dw[k] = dw[k] + jnp.sum(dc * p_s[pl.ds(HALO + t0 - (2 - k), tc), :], axis=0, keepdims=True)
        dw_ref[...] = _rows_to_tile(dw)

    return pl.pallas_call(
        body, name="mixer_a_bwd", grid=(nblk,),
        in_specs=[_seg_spec(T, 0, nblk), _seg_spec(T, 1, nblk), _seg_spec(T, 2, nblk),
                  pl.BlockSpec((3, LANES), lambda c: (0, c)), _seg_spec(T, 0, nblk)],
        out_specs=[pl.BlockSpec((3, T, LANES), lambda c: (0, 0, c)),
                   pl.BlockSpec((None, SUBLANES, LANES), lambda c: (c, 0, 0))],
        out_shape=[jax.ShapeDtypeStruct((6, T, C), BF16), jax.ShapeDtypeStruct((nblk, SUBLANES, LANES), F32)],
        scratch_shapes=[pltpu.VMEM((T + HALO, LANES), F32), pltpu.VMEM((T + HALO, LANES), F32)],
        compiler_params=_params(("parallel",)))(proj, proj, proj, conv_a, dy)


def _rg_gates(xr, wa, ba, wx, bx, ls):
    xb = xr.astype(BF16)
    r = _sigmoid(jnp.dot(xb, wa, preferred_element_type=F32) + ba)
    i = _sigmoid(jnp.dot(xb, wx, preferred_element_type=F32) + bx)
    log_a = LRU_C * r * ls
    a = jnp.exp(log_a)
    mult = jnp.sqrt(_one_minus_exp(2.0 * log_a))
    return r, i, a, mult


def _conv4(xh_s, cw, bias, t0, tc):
    return (cw[3:4, :] * xh_s[pl.ds(HALO + t0, tc), :] + cw[2:3, :] * xh_s[pl.ds(HALO + t0 - 1, tc), :]
            + cw[1:2, :] * xh_s[pl.ds(HALO + t0 - 2, tc), :] + cw[0:1, :] * xh_s[pl.ds(HALO + t0 - 3, tc), :] + bias)


def _mixer_b_specs(T, nblk):
    vec = pl.BlockSpec((1, LANES), lambda c: (0, c))
    mat = pl.BlockSpec((None, LANES, LANES), lambda c: (c, 0, 0))
    return [_seg_spec(T, 3, nblk), _seg_spec(T, 4, nblk), pl.BlockSpec((4, LANES), lambda c: (0, c)),
            vec, mat, vec, mat, vec, vec]


def _mixer_b_fwd(name, proj, conv_b, bias, wa, ba, wx, bx, lam, y, jobs=()):
    _, T, C = proj.shape
    nblk = C // LANES
    chunks = _chunks(T)

    def body(gate_ref, x_ref, cw_ref, cb_ref, wa_ref, ba_ref, wx_ref, bx_ref, lam_ref, y_in, y_ref, xh_s, a_s, b_s):
        xh_s[pl.ds(0, HALO), :] = jnp.zeros((HALO, LANES), F32)
        for t0, tc in chunks:
            xh_s[pl.ds(HALO + t0, tc), :] = x_ref[pl.ds(t0, tc), :]
        cw, bias_v = cw_ref[...], cb_ref[...]
        ls = _log_sigmoid(lam_ref[...])
        for t0, tc in chunks:
            xr = _conv4(xh_s, cw, bias_v, t0, tc)
            r, i, a, mult = _rg_gates(xr, wa_ref[...], ba_ref[...], wx_ref[...], bx_ref[...], ls)
            ac, hc = _tile_scan(a, mult * i * xr, reverse=False)
            a_s[pl.ds(t0, tc), :] = ac
            b_s[pl.ds(t0, tc), :] = hc
        _carry_scan(a_s, b_s, T, reverse=False)
        for t0, tc in chunks:
            y_ref[pl.ds(t0, tc), :] = (b_s[pl.ds(t0, tc), :] * _gelu(gate_ref[pl.ds(t0, tc), :])).astype(y_ref.dtype)

    return _carry_call(
        body, name=name, steps=nblk, in_specs=_mixer_b_specs(T, nblk) + [_ANY],
        out_specs=[_seg_spec(T, 1, nblk)], out_shape=[jax.ShapeDtypeStruct(y.shape, y.dtype)],
        scratch_shapes=[pltpu.VMEM((T + HALO, LANES), F32), pltpu.VMEM((T, LANES), F32), pltpu.VMEM((T, LANES), F32)],
        args=(proj, proj, conv_b, bias, wa, ba, wx, bx, lam, y), jobs=jobs, aliases={9: 0})


_ROW_CONV, _ROW_BIAS, _ROW_BA, _ROW_BX, _ROW_LAM = 0, 4, 5, 6, 7


def _mixer_b_bwd(name, proj, conv_b, bias, wa, ba, wx, bx, lam, dy, dproj, jobs=()):
    _, T, C = proj.shape
    nblk = C // LANES
    chunks = _chunks(T)

    def body(gate_ref, x_ref, cw_ref, cb_ref, wa_ref, ba_ref, wx_ref, bx_ref, lam_ref, dy_ref, dp_in,
             dp_ref, sm_ref, dwa_ref, dwx_ref, xh_s, xr_s, r_s, i_s, a_s, h_s, sa_s, sb_s, dx_s):
        zero_halo = jnp.zeros((HALO, LANES), F32)
        xh_s[pl.ds(0, HALO), :] = zero_halo
        h_s[pl.ds(0, HALO), :] = zero_halo
        a_s[pl.ds(T, HALO), :] = zero_halo
        dx_s[pl.ds(T, HALO), :] = zero_halo
        for t0, tc in chunks:
            xh_s[pl.ds(HALO + t0, tc), :] = x_ref[pl.ds(t0, tc), :]
        cw, bias_v = cw_ref[...], cb_ref[...]
        lam_v = lam_ref[...]
        ls = _log_sigmoid(lam_v)
        wa_v, wx_v, ba_v, bx_v = wa_ref[...], wx_ref[...], ba_ref[...], bx_ref[...]
        for t0, tc in chunks:
            xr = _conv4(xh_s, cw, bias_v, t0, tc)
            r, i, a, mult = _rg_gates(xr, wa_v, ba_v, wx_v, bx_v, ls)
            xr_s[pl.ds(t0, tc), :] = xr
            r_s[pl.ds(t0, tc), :] = r
            i_s[pl.ds(t0, tc), :] = i
            a_s[pl.ds(t0, tc), :] = a
            ac, hc = _tile_scan(a, mult * i * xr, reverse=False)
            sa_s[pl.ds(t0, tc), :] = ac
            sb_s[pl.ds(t0, tc), :] = hc
        _carry_scan(sa_s, sb_s, T, reverse=False)
        for t0, tc in chunks:
            h_s[pl.ds(HALO + t0, tc), :] = sb_s[pl.ds(t0, tc), :]
        for t0, tc in chunks:
            gv = gate_ref[pl.ds(t0, tc), :]
            dyv = dy_ref[pl.ds(t0, tc), :]
            dp_ref[0, pl.ds(t0, tc), :] = (dyv * h_s[pl.ds(HALO + t0, tc), :] * _gelu_grad(gv)).astype(dp_ref.dtype)
            ac, gc = _tile_scan(a_s[pl.ds(t0 + 1, tc), :], dyv * _gelu(gv), reverse=True)
            sa_s[pl.ds(t0, tc), :] = ac
            sb_s[pl.ds(t0, tc), :] = gc
        _carry_scan(sa_s, sb_s, T, reverse=True)
        acc = {k: jnp.zeros((1, LANES), F32) for k in ("bias", "ba", "bx", "lam")}
        dwa = jnp.zeros((LANES, LANES), F32)
        dwx = jnp.zeros((LANES, LANES), F32)
        for t0, tc in chunks:
            dht = sb_s[pl.ds(t0, tc), :]
            xr, r, i, a = xr_s[pl.ds(t0, tc), :], r_s[pl.ds(t0, tc), :], i_s[pl.ds(t0, tc), :], a_s[pl.ds(t0, tc), :]
            mult = jnp.sqrt(_one_minus_exp(2.0 * LRU_C * r * ls))
            da = dht * h_s[pl.ds(HALO + t0 - 1, tc), :]
            dmult = dht * i * xr
            di = dht * mult * xr
            dlog_a = da * a - dmult * a * a / mult
            dpa = dlog_a * (LRU_C * ls) * r * (1.0 - r)
            dpx = di * i * (1.0 - i)
            acc["lam"] = acc["lam"] + jnp.sum(dlog_a * r, axis=0, keepdims=True)
            acc["ba"] = acc["ba"] + jnp.sum(dpa, axis=0, keepdims=True)
            acc["bx"] = acc["bx"] + jnp.sum(dpx, axis=0, keepdims=True)
            xb, dpab, dpxb = xr.astype(BF16), dpa.astype(BF16), dpx.astype(BF16)
            dwa = dwa + lax.dot_general(xb, dpab, DIMS_TN, preferred_element_type=F32)
            dwx = dwx + lax.dot_general(xb, dpxb, DIMS_TN, preferred_element_type=F32)
            dxr = (dht * mult * i + lax.dot_general(dpab, wa_v, DIMS_NT, preferred_element_type=F32)
                   + lax.dot_general(dpxb, wx_v, DIMS_NT, preferred_element_type=F32))
            acc["bias"] = acc["bias"] + jnp.sum(dxr, axis=0, keepdims=True)
            dx_s[pl.ds(t0, tc), :] = dxr
        dcw = [jnp.zeros((1, LANES), F32) for _ in range(4)]
        for t0, tc in chunks:
            dxr = dx_s[pl.ds(t0, tc), :]
            dxin = (cw[3:4, :] * dxr + cw[2:3, :] * dx_s[pl.ds(t0 + 1, tc), :] + cw[1:2, :] * dx_s[pl.ds(t0 + 2, tc), :]
                    + cw[0:1, :] * dx_s[pl.ds(t0 + 3, tc), :])
            dp_ref[1, pl.ds(t0, tc), :] = dxin.astype(dp_ref.dtype)
            for k in range(4):
                dcw[k] = dcw[k] + jnp.sum(dxr * xh_s[pl.ds(HALO + t0 - (3 - k), tc), :], axis=0, keepdims=True)
        dlam = acc["lam"] * LRU_C * _sigmoid(-lam_v)
        sm_ref[...] = _rows_to_tile(dcw + [acc["bias"], acc["ba"], acc["bx"], dlam])
        dwa_ref[...] = dwa
        dwx_ref[...] = dwx

    big = lambda halo: pltpu.VMEM((T + halo, LANES), F32)
    mat = pl.BlockSpec((None, LANES, LANES), lambda c: (c, 0, 0))
    return _carry_call(
        body, name=name, steps=nblk,
        in_specs=_mixer_b_specs(T, nblk) + [_seg_spec(T, 1, nblk), _ANY],
        out_specs=[pl.BlockSpec((3, T, LANES), lambda c: (1, 0, c)),
                   pl.BlockSpec((None, SUBLANES, LANES), lambda c: (c, 0, 0)), mat, mat],
        out_shape=[jax.ShapeDtypeStruct(dproj.shape, dproj.dtype), jax.ShapeDtypeStruct((nblk, SUBLANES, LANES), F32),
                   jax.ShapeDtypeStruct((nblk, LANES, LANES), F32), jax.ShapeDtypeStruct((nblk, LANES, LANES), F32)],
        scratch_shapes=[big(HALO), big(0), big(0), big(0), big(HALO), big(HALO), big(0), big(0), big(HALO)],
        args=(proj, proj, conv_b, bias, wa, ba, wx, bx, lam, dy, dproj), jobs=jobs, aliases={10: 0})


ATT_BLOCK = 128
ATT_GROUP = 3
ATT_TILE = ATT_BLOCK * ATT_GROUP
ATT_UNDERFLOW = -110.0
ATT_UNVISITED = -1e30


def _split_dot(x, m):
    hi = x.astype(BF16)
    lo = (x - hi.astype(F32)).astype(BF16)
    return jnp.dot(hi, m, preferred_element_type=F32) + jnp.dot(lo, m, preferred_element_type=F32)


def _sub(x, j):
    return x[:, j * ATT_BLOCK:(j + 1) * ATT_BLOCK]


def _stack_rows(x):
    return jnp.concatenate([_sub(x, j) for j in range(ATT_GROUP)], axis=0)


def _unstack_rows(x, offsets):
    return jnp.concatenate([x[j * ATT_BLOCK:(j + 1) * ATT_BLOCK, :] + offsets[j] for j in range(ATT_GROUP)], axis=1)


def _att_tile(q, k_ref, q0, qb, it, scale):
    hi = (qb + 1 - ATT_GROUP * it) * ATT_BLOCK
    k0 = pl.multiple_of(jnp.maximum(hi - ATT_TILE, 0), ATT_BLOCK)
    kt = k_ref[pl.ds(k0, ATT_TILE), :]
    z = lax.dot_general(q, kt, DIMS_NT, preferred_element_type=F32) * scale
    key = k0 + lax.broadcasted_iota(jnp.int32, z.shape, 1)
    row = q0 + lax.broadcasted_iota(jnp.int32, z.shape, 0)
    mask = (key < row) & (key < hi)
    n = jnp.where(mask, -(jnp.maximum(z, 0.0) + jnp.log(1.0 + jnp.exp(-jnp.abs(z)))), 0.0)
    return k0, kt, z, mask, n


def _suffix_in_tile(n, upper, run):
    rs = [jnp.sum(_sub(n, j), axis=1, keepdims=True) for j in range(ATT_GROUP)]
    offs = [None] * ATT_GROUP
    offs[ATT_GROUP - 1] = run
    for j in range(ATT_GROUP - 2, -1, -1):
        offs[j] = offs[j + 1] + rs[j + 1]
    return _unstack_rows(_split_dot(_stack_rows(n), upper), offs), offs[0] + rs[0]


def _head_spec(T, seg, heads):
    return pl.BlockSpec((None, T, ATT_HEAD_DIM), lambda h: (seg, 0, h))


def _attention_fwd(name, qkv, jobs=()):
    _, T, D = qkv.shape
    heads = D // ATT_HEAD_DIM
    nq = T // ATT_BLOCK
    assert nq <= LANES and T >= ATT_TILE and nq % 2 == 0
    scale = 1.0 / math.sqrt(ATT_HEAD_DIM)

    def body(q_ref, k_ref, v_ref, o_ref, r_ref, acc_s, run_s):
        rr = lax.broadcasted_iota(jnp.int32, (ATT_BLOCK, ATT_BLOCK), 0)
        cc = lax.broadcasted_iota(jnp.int32, (ATT_BLOCK, ATT_BLOCK), 1)
        upper = jnp.where(rr > cc, 1.0, 0.0).astype(BF16)
        lane = lax.broadcasted_iota(jnp.int32, (ATT_BLOCK, LANES), 1)

        def tile(slot, qb, q0, q, it, first):
            k0, _, z, mask, n = _att_tile(q, k_ref, q0, qb, it, scale)
            run = jnp.zeros((ATT_BLOCK, LANES), F32) if first else run_s[slot]
            suffix, run_next = _suffix_in_tile(n, upper, run)
            w = jnp.where(mask, jnp.exp(z + n + suffix), 0.0)
            pv = jnp.dot(w.astype(BF16), v_ref[pl.ds(k0, ATT_TILE), :], preferred_element_type=F32)
            if first:
                acc_s[slot] = pv
            else:
                acc_s[slot] += pv
                r_ref[pl.ds(q0, ATT_BLOCK), :] = jnp.where(lane == it, run, r_ref[pl.ds(q0, ATT_BLOCK), :])
            run_s[slot] = run_next
            return jnp.max(run_next) >= ATT_UNDERFLOW

        def pair_loop(p, _):
            blocks = []
            for slot in range(2):
                qb = 2 * p + slot
                q0 = pl.multiple_of(qb * ATT_BLOCK, ATT_BLOCK)
                r_ref[pl.ds(q0, ATT_BLOCK), :] = jnp.where(lane == 0, 0.0, ATT_UNVISITED)
                blocks.append((qb, q0, q_ref[pl.ds(q0, ATT_BLOCK), :]))
            go = [tile(slot, *blocks[slot], 0, True) for slot in range(2)]
            for slot in range(2):
                qb, q0, q = blocks[slot]
                n_tiles = (qb + ATT_GROUP) // ATT_GROUP
                lax.while_loop(lambda c: (c[0] < n_tiles) & c[1],
                               lambda c: (c[0] + 1, tile(slot, qb, q0, q, c[0], False)), (jnp.int32(1), go[slot]))
                o_ref[pl.ds(q0, ATT_BLOCK), :] = acc_s[slot].astype(o_ref.dtype)
            return 0

        lax.fori_loop(0, nq // 2, pair_loop, 0)

    return _carry_call(
        body, name=name, steps=heads,
        in_specs=[_head_spec(T, 0, heads), _head_spec(T, 1, heads), _head_spec(T, 2, heads)],
        out_specs=[pl.BlockSpec((T, ATT_HEAD_DIM), lambda h: (0, h)), pl.BlockSpec((None, T, LANES), lambda h: (h, 0, 0))],
        out_shape=[jax.ShapeDtypeStruct((T, D), BF16), jax.ShapeDtypeStruct((heads, T, LANES), F32)],
        scratch_shapes=[pltpu.VMEM((2, ATT_BLOCK, ATT_HEAD_DIM), F32), pltpu.VMEM((2, ATT_BLOCK, LANES), F32)],
        args=(qkv, qkv, qkv), jobs=jobs)


def _attention_bwd(name, qkv, do, rmat, jobs=()):
    _, T, D = qkv.shape
    heads = D // ATT_HEAD_DIM
    nq = T // ATT_BLOCK
    scale = 1.0 / math.sqrt(ATT_HEAD_DIM)

    def body(q_ref, k_ref, v_ref, do_ref, r_ref, dqkv_ref, dk_s, dv_s, dq_s, left_s):
        rr = lax.broadcasted_iota(jnp.int32, (ATT_BLOCK, ATT_BLOCK), 0)
        cc = lax.broadcasted_iota(jnp.int32, (ATT_BLOCK, ATT_BLOCK), 1)
        upper = jnp.where(rr > cc, 1.0, 0.0).astype(BF16)
        lower = jnp.where(rr < cc, 1.0, 0.0).astype(BF16)
        lane = lax.broadcasted_iota(jnp.int32, (ATT_BLOCK, LANES), 1)
        dk_s[...] = jnp.zeros_like(dk_s)
        dv_s[...] = jnp.zeros_like(dv_s)

        def tile(slot, qb, q0, q, dov, it, first):
            k0, kt, z, mask, n = _att_tile(q, k_ref, q0, qb, it, scale)
            vt = v_ref[pl.ds(k0, ATT_TILE), :]
            run = jnp.sum(jnp.where(lane == it, r_ref[pl.ds(q0, ATT_BLOCK), :], 0.0), axis=1, keepdims=True)
            suffix, _ = _suffix_in_tile(n, upper, run)
            s = z + n
            w = jnp.where(mask, jnp.exp(s + suffix), 0.0)
            e = w * lax.dot_general(dov, vt, DIMS_NT, preferred_element_type=F32)
            es = [jnp.sum(_sub(e, g), axis=1, keepdims=True) for g in range(ATT_GROUP)]
            pre = [jnp.zeros((ATT_BLOCK, LANES), F32) if first else left_s[slot]]
            for g in range(ATT_GROUP):
                pre.append(pre[g] + es[g])
            before = _unstack_rows(jnp.dot(_stack_rows(e).astype(BF16), lower, preferred_element_type=F32), pre)
            sig = jnp.exp(s)
            dz = (jnp.where(mask, e * (1.0 - sig) - before * sig, 0.0) * scale).astype(BF16)
            dq = jnp.dot(dz, kt, preferred_element_type=F32)
            if first:
                dq_s[slot] = dq
            else:
                dq_s[slot] += dq
            dk_s[pl.ds(k0, ATT_TILE), :] += lax.dot_general(dz, q, DIMS_TN, preferred_element_type=F32)
            dv_s[pl.ds(k0, ATT_TILE), :] += lax.dot_general(w.astype(BF16), dov, DIMS_TN, preferred_element_type=F32)
            left_s[slot] = pre[ATT_GROUP]

        def pair_loop(p, _):
            blocks = []
            for slot in range(2):
                qb = 2 * p + slot
                q0 = pl.multiple_of(qb * ATT_BLOCK, ATT_BLOCK)
                n_tiles = (qb + ATT_GROUP) // ATT_GROUP
                seen = ((jnp.max(r_ref[pl.ds(q0, ATT_BLOCK), :], axis=0, keepdims=True) > 0.5 * ATT_UNVISITED)
                        & (lane[0:1, :] < n_tiles))
                n_visited = jnp.maximum(jnp.sum(jnp.where(seen, 1.0, 0.0)).astype(jnp.int32), 1)
                blocks.append((qb, q0, q_ref[pl.ds(q0, ATT_BLOCK), :], do_ref[pl.ds(q0, ATT_BLOCK), :], n_visited))
            for slot in range(2):
                qb, q0, q, dov, n_visited = blocks[slot]
                tile(slot, qb, q0, q, dov, n_visited - 1, True)
            for slot in range(2):
                qb, q0, q, dov, n_visited = blocks[slot]
                lax.fori_loop(1, n_visited, lambda j, c: (tile(slot, qb, q0, q, dov, n_visited - 1 - j, False), c)[1], 0)
                dqkv_ref[0, pl.ds(q0, ATT_BLOCK), :] = dq_s[slot].astype(dqkv_ref.dtype)
            return 0

        lax.fori_loop(0, nq // 2, pair_loop, 0)
        dqkv_ref[1, :, :] = dk_s[...].astype(dqkv_ref.dtype)
        dqkv_ref[2, :, :] = dv_s[...].astype(dqkv_ref.dtype)

    return _carry_call(
        body, name=name, steps=heads,
        in_specs=[_head_spec(T, 0, heads), _head_spec(T, 1, heads), _head_spec(T, 2, heads),
                  pl.BlockSpec((T, ATT_HEAD_DIM), lambda h: (0, h)), pl.BlockSpec((None, T, LANES), lambda h: (h, 0, 0))],
        out_specs=[pl.BlockSpec((3, T, ATT_HEAD_DIM), lambda h: (0, 0, h))],
        out_shape=[jax.ShapeDtypeStruct((3, T, D), BF16)],
        scratch_shapes=[pltpu.VMEM((T, ATT_HEAD_DIM), F32), pltpu.VMEM((T, ATT_HEAD_DIM), F32),
                        pltpu.VMEM((2, ATT_BLOCK, ATT_HEAD_DIM), F32), pltpu.VMEM((2, ATT_BLOCK, LANES), F32)],
        args=(qkv, qkv, qkv, do, rmat), jobs=jobs)


def _block_diag_pairs(w):
    h = w.shape[0]
    wp = w.reshape(h // 2, 2, RG_HEAD_DIM, RG_HEAD_DIM)
    z = jnp.zeros_like(wp[:, 0])
    top = jnp.concatenate([wp[:, 0], z], axis=2)
    bot = jnp.concatenate([z, wp[:, 1]], axis=2)
    return jnp.concatenate([top, bot], axis=1)


def _diag_pairs(g):
    n = g.shape[0]
    a = g[:, :RG_HEAD_DIM, :RG_HEAD_DIM]
    b = g[:, RG_HEAD_DIM:, RG_HEAD_DIM:]
    return jnp.stack([a, b], axis=1).reshape(2 * n, RG_HEAD_DIM, RG_HEAD_DIM)


class _Weights:
    def __init__(self, full, shards=None, plan=None):
        self.full, self.shards, self.plan = dict(full), shards or {}, plan or {}
        self.partial, self.rows = {}, {}

    def __getitem__(self, name):
        return self.full[name]

    def jobs(self, call):
        return [_gather_job(self.shards[n], self.partial.get(n), lo, hi, parts)
                for n, lo, hi, parts in self.plan.get(call, ())]

    def deliver(self, call, outs):
        for (n, lo, hi, parts), g in zip(self.plan.get(call, ()), outs):
            self.partial[n] = g
            self.rows[n] = self.rows.get(n, 0) + hi - lo
            if self.rows[n] == parts:
                self.full[n] = _gathered_layout(n, g)


def _gathered_layout(name, g):
    if name in ("w_in", "w_qkv", "w_out", "w_o"):
        return g.reshape(g.shape[0] * g.shape[1], g.shape[2])
    return g


class _Grads:
    def __init__(self, lands=None, plan=None):
        self.lands, self.plan = dict(lands) if lands else None, plan or {}
        self.ready, self.sent = {}, {}

    def put(self, name, arr):
        self.ready[name] = arr

    def jobs(self, call):
        if self.lands is None:
            return []
        return [_exchange_job(self.ready[n], self.lands[n], lo, hi, parts) for n, lo, hi, parts in self.plan.get(call, ())]

    def deliver(self, call, outs):
        for (n, lo, hi, parts), o in zip(self.plan.get(call, ()), outs):
            assert self.sent.get(n, (0, parts)) == (lo, parts), (call, n)
            self.lands[n] = o
            self.sent[n] = (hi, parts)

    def flush(self, name):
        if self.lands is None:
            return
        rest = []
        for n in self.ready:
            lo, parts = self.sent.get(n, (0, 1))
            if lo < parts:
                rest.append((n, lo, parts, parts))
        if rest:
            outs = _run_jobs(name, [_exchange_job(self.ready[n], self.lands[n], lo, hi, parts) for n, lo, hi, parts in rest])
            for (n, _, hi, parts), o in zip(rest, outs):
                self.lands[n] = o
                self.sent[n] = (hi, parts)


def _mlp_fwd(tag, h, wts, run):
    T, D = h.shape
    w_up = wts["up" + tag]
    fb = w_up.shape[2]
    F = fb * N_DEV
    tm, tn, tk = _tile(T, MM_TM), _tile(fb, MM_TN), _tile(D, MM_TK)
    nb = fb // tn

    def up_epilogue(u):
        r = jnp.maximum(u, 0.0)
        return u, r * r

    o_spec = pl.BlockSpec((tm, tn), lambda i, j, k: (i, j))
    u, act = run(
        _matmul, f"mlp_up_l{tag}",
        [(h, pl.BlockSpec((tm, tk), lambda i, j, k: (i, k))),
         (w_up, pl.BlockSpec((None, tk, tn), lambda i, j, k: (j // nb, k, j % nb)))],
        [(jax.ShapeDtypeStruct((T, F), BF16), o_spec), (jax.ShapeDtypeStruct((T, F), BF16), o_spec)],
        (T // tm, F // tn, D // tk), DIMS_NN, (tm, tn), up_epilogue, n_main=2)
    w_down = wts["down" + tag].reshape(F, D)
    m = run(_mm_nn, f"mlp_down_l{tag}", act, w_down, BF16)
    return u, act, m


def _mlp_bwd(tag, h, u, act, dm, wts, grads, run):
    T, D = h.shape
    w_up, w_down = wts["up" + tag], wts["down" + tag]
    fb = w_up.shape[2]
    F = fb * N_DEV
    grads.put("down" + tag, run(_mm_tn, f"mlp_down_dw_l{tag}", act, dm, BF16).reshape(N_DEV, fb, D))
    tm, tn, tk = _tile(T, MM_TM), _tile(fb, MM_TN), _tile(D, MM_TK)
    nb = fb // tn
    o_spec = pl.BlockSpec((tm, tn), lambda i, j, k: (i, j))
    du = run(
        _matmul, f"mlp_down_dx_l{tag}",
        [(dm, pl.BlockSpec((tm, tk), lambda i, j, k: (i, k))),
         (w_down, pl.BlockSpec((None, tn, tk), lambda i, j, k: (j // nb, j % nb, k))),
         (u, o_spec)],
        [(jax.ShapeDtypeStruct((T, F), BF16), o_spec)],
        (T // tm, F // tn, D // tk), DIMS_NT, (tm, tn),
        lambda r, uv: (r * (2.0 * jnp.maximum(uv.astype(F32), 0.0)),))
    grads.put("up" + tag, run(_mm_tn, f"mlp_up_dw_l{tag}", h, du, BF16, out_blocks=N_DEV))
    tn2 = _tile(D, MM_TN)
    pair = 2 if MM_TK >= 2 * fb else 1
    return run(
        _matmul, f"mlp_up_dx_l{tag}",
        [(du, pl.BlockSpec((tm, pair * fb), lambda i, j, k: (i, k))),
         (w_up, pl.BlockSpec((pair, tn2, fb), lambda i, j, k: (k, j, 0)))],
        [(jax.ShapeDtypeStruct((T, D), BF16), pl.BlockSpec((tm, tn2), lambda i, j, k: (i, j)))],
        (T // tm, D // tn2, N_DEV // pair), DIMS_NT, (tm, tn2), None)


def _local_step(x, target, gains, conv_a, conv_b, conv_b_bias, rg_w_a, rg_b_a, rg_w_x, rg_b_x, rg_lambda, wts, grads):
    T, D = x.shape
    g = lambda l, i: gains[l, i][None, :]
    wa_p = _block_diag_pairs(rg_w_a).astype(BF16)
    wx_p = _block_diag_pairs(rg_w_x).astype(BF16)

    def run(fn, name, *args, n_main=1, **kw):
        jw, jg = wts.jobs(name), grads.jobs(name)
        res = fn(name, *args, jobs=jw + jg, **kw)
        main, jo = res[:n_main], res[n_main:]
        wts.deliver(name, jo[:len(jw)])
        grads.deliver(name, jo[len(jw):])
        return main[0] if n_main == 1 else main

    h0 = run(_norm_fwd, "norm_in", x, g(0, 0))
    proj = run(_mm_nt, "w_in_fwd", h0, wts["w_in"], F32, out_seg=5)
    y = run(_mixer_b_fwd, "mixer_b_fwd", proj, conv_b, conv_b_bias, wa_p, rg_b_a, wx_p, rg_b_x, rg_lambda,
            _mixer_a_fwd(proj, conv_a))
    mix0 = run(_mm_nn, "w_out_fwd", y, wts["w_out"], BF16, a_seg=2)
    x1, h1 = run(_resid_norm, "resid_mix0", x, mix0, g(0, 1), g(0, 2), n_main=2)
    u0, act0, m0 = _mlp_fwd("0", h1, wts, run)
    x2, h2 = run(_resid_norm, "resid_mlp0", x1, m0, g(0, 3), g(1, 0), n_main=2)
    qkv = run(_mm_nt, "w_qkv_fwd", h2, wts["w_qkv"], BF16, out_seg=3)
    o, rmat = run(_attention_fwd, "attention_fwd", qkv, n_main=2)
    mix1 = run(_mm_nn, "w_o_fwd", o, wts["w_o"], BF16)
    x3, h3 = run(_resid_norm, "resid_mix1", x2, mix1, g(1, 1), g(1, 2), n_main=2)
    u1, act1, m1 = _mlp_fwd("1", h3, wts, run)
    dx4, dm1, dg13, sq = _final_loss("loss", x3, m1, g(1, 3), target)

    dh3 = _mlp_bwd("1", h3, u1, act1, dm1, wts, grads, run)
    dx3, dmix1, dg12, dg11 = run(_norm_bwd_pair, "norm_bwd_x3", x3, g(1, 2), dh3, dx4, mix1, g(1, 1), n_main=4)
    grads.put("w_o", run(_mm_tn, "w_o_dw", o, dmix1, BF16).reshape(N_DEV, D // N_DEV, D))
    do = run(_mm_nt, "w_o_dx", dmix1, wts["w_o"], BF16)
    dqkv = run(_attention_bwd, "attention_bwd", qkv, do, rmat)
    grads.put("w_qkv", run(_mm_tn, "w_qkv_dw", dqkv, h2, BF16, a_seg=3).reshape(N_DEV, 3 * D // N_DEV, D))
    dh2 = run(_mm_nn, "w_qkv_dx", dqkv, wts["w_qkv"], BF16, a_seg=3)
    dx2, dm0, dg10, dg03 = run(_norm_bwd_pair, "norm_bwd_x2", x2, g(1, 0), dh2, dx3, m0, g(0, 3), n_main=4)
    dh1 = _mlp_bwd("0", h1, u0, act0, dm0, wts, grads, run)
    dx1, dmix0, dg02, dg01 = run(_norm_bwd_pair, "norm_bwd_x1", x1, g(0, 2), dh1, dx2, mix0, g(0, 1), n_main=4)
    grads.put("w_out", run(_mm_tn, "w_out_dw", y, dmix0, BF16, a_seg=2).reshape(N_DEV, D // N_DEV, D))
    dy = run(_mm_nt, "w_out_dx", dmix0, wts["w_out"], F32, out_seg=2)
    dproj_a, dconv_a = _mixer_a_bwd(proj, conv_a, dy)
    dproj, sm_b, dwa_p, dwx_p = run(_mixer_b_bwd, "mixer_b_bwd", proj, conv_b, conv_b_bias, wa_p, rg_b_a, wx_p, rg_b_x,
                                    rg_lambda, dy, dproj_a, n_main=4)
    grads.put("w_in", run(_mm_tn, "w_in_dw", dproj, h0, BF16, a_seg=5).reshape(N_DEV, 5 * D // (2 * N_DEV), D))
    dh0 = run(_mm_nn, "w_in_dx", dproj, wts["w_in"], BF16, a_seg=5)
    dx0, dg00 = run(_norm_bwd, "norm_bwd_x0", x, g(0, 0), dh0, dx1, n_main=2)

    C = D // 2
    lanes_to_vec = lambda t, row: t[:, row, :].reshape(1, C)
    small = {
        "norm_gains": jnp.concatenate([dg00, dg01, dg02, dg03, dg10, dg11, dg12, dg13], axis=0).reshape(2, 4, D),
        "conv_a": jnp.transpose(dconv_a[:, :3, :], (1, 0, 2)).reshape(3, C),
        "conv_b": jnp.transpose(sm_b[:, :4, :], (1, 0, 2)).reshape(4, C),
        "conv_b_bias": lanes_to_vec(sm_b, _ROW_BIAS),
        "rg_w_a": _diag_pairs(dwa_p),
        "rg_b_a": lanes_to_vec(sm_b, _ROW_BA),
        "rg_w_x": _diag_pairs(dwx_p),
        "rg_b_x": lanes_to_vec(sm_b, _ROW_BX),
        "rg_lambda": lanes_to_vec(sm_b, _ROW_LAM),
    }
    return sq[0, 0], dx0, small


def _my_index():
    return 4 * lax.axis_index("x") + 2 * lax.axis_index("y") + lax.axis_index("c")


def _peers():
    x, y, c = lax.axis_index("x"), lax.axis_index("y"), lax.axis_index("c")
    out = []
    for k in range(1, N_DEV):
        px = x ^ ((k >> 2) & 1)
        py = y ^ ((k >> 1) & 1)
        pc = c ^ (k & 1)
        out.append(((px, py, pc), 4 * px + 2 * py + pc))
    return out


GATHER_PLAN = {
    "norm_in": (("w_in", 0, 1, 1),),
    "w_in_fwd": (("w_out", 0, 1, 1), ("up0", 0, 1, 4)),
    "mixer_b_fwd": (("up0", 1, 3, 4),),
    "w_out_fwd": (("up0", 3, 4, 4),),
    "resid_mix0": (("down0", 0, 1, 4),),
    "mlp_up_l0": (("down0", 1, 4, 4),),
    "mlp_down_l0": (("w_qkv", 0, 1, 1),),
    "resid_mlp0": (("up1", 0, 1, 4),),
    "w_qkv_fwd": (("w_o", 0, 1, 1), ("up1", 1, 2, 4)),
    "attention_fwd": (("up1", 2, 4, 4), ("down1", 0, 2, 4)),
    "mlp_up_l1": (("down1", 2, 4, 4),),
}
EXCHANGE_PLAN = {
    "mlp_down_dx_l1": (("down1", 0, 3, 8),), "mlp_up_dw_l1": (("down1", 3, 6, 8),),
    "mlp_up_dx_l1": (("down1", 6, 8, 8), ("up1", 0, 1, 8)),
    "attention_bwd": (("up1", 1, 8, 8), ("w_o", 0, 1, 1)),
    "w_qkv_dx": (("w_qkv", 0, 2, 6),), "norm_bwd_x2": (("w_qkv", 2, 3, 6),), "mlp_down_dw_l0": (("w_qkv", 3, 6, 6),),
    "mlp_down_dx_l0": (("down0", 0, 3, 8),), "mlp_up_dw_l0": (("down0", 3, 6, 8),),
    "mlp_up_dx_l0": (("down0", 6, 8, 8), ("up0", 0, 1, 8)),
    "norm_bwd_x1": (("up0", 1, 2, 8),), "w_out_dw": (("up0", 2, 3, 8),), "w_out_dx": (("w_out", 0, 1, 2),),
    "mixer_b_bwd": (("up0", 3, 7, 8),),
    "w_in_dw": (("up0", 7, 8, 8), ("w_out", 1, 2, 2)),
    "w_in_dx": (("w_in", 0, 2, 4),),
    "norm_bwd_x0": (("w_in", 2, 3, 4),),
    "adamw_mlp_w_down": (("w_in", 3, 4, 4),),
}


def _job_sems():
    return [pltpu.SemaphoreType.DMA((N_DEV - 1,)), pltpu.SemaphoreType.DMA((N_DEV - 1,)), pltpu.SemaphoreType.DMA((1,))]


def _gather_job(shard, prev=None, lo=0, hi=1, parts=1):
    n = shard.shape[0] // parts
    assert n * parts == shard.shape[0]
    rows = pl.ds(lo * n, (hi - lo) * n)

    def ctx():
        x, y, c = lax.axis_index("x"), lax.axis_index("y"), lax.axis_index("c")
        chips = [(1 - x, y), (x, 1 - y), (1 - x, 1 - y)]
        return x, y, c, chips

    def idx(px, py, pc):
        return 4 * px + 2 * py + pc

    def copy(src, out, sems, k, block, to):
        return pltpu.make_async_remote_copy(
            src_ref=out.at[block, rows] if src is None else src.at[rows], dst_ref=out.at[block, rows],
            send_sem=sems[0].at[k], recv_sem=sems[1].at[k], device_id=to, device_id_type=MESH)

    def start(ins, outs, sems):
        x, y, c, chips = ctx()
        src, out = ins[0], outs[0]
        me = idx(x, y, c)
        pltpu.make_async_copy(src.at[rows], out.at[me, rows], sems[2].at[0]).start()
        copy(src, out, sems, 0, me, (x, y, 1 - c)).start()
        for j, (px, py) in enumerate(chips):
            copy(src, out, sems, 1 + j, me, (px, py, c)).start()

    def mid(ins, outs, sems):
        x, y, c, chips = ctx()
        out = outs[0]
        for j, (px, py) in enumerate(chips):
            copy(None, out, sems, 1 + j, idx(px, py, c), (x, y, c)).wait_recv()
            copy(None, out, sems, 4 + j, idx(px, py, c), (x, y, 1 - c)).start()

    def end(ins, outs, sems):
        x, y, c, chips = ctx()
        src, out = ins[0], outs[0]
        me = (x, y, c)
        copy(None, out, sems, 0, idx(x, y, 1 - c), me).wait_recv()
        for j, (px, py) in enumerate(chips):
            copy(None, out, sems, 4 + j, idx(px, py, 1 - c), me).wait_recv()
        for k in range(N_DEV - 1):
            copy(src, out, sems, k, idx(x, y, c), me).wait_send()
        pltpu.make_async_copy(src.at[rows], out.at[idx(x, y, c), rows], sems[2].at[0]).wait()

    out_shape = jax.ShapeDtypeStruct((N_DEV,) + shard.shape, shard.dtype)
    if prev is None:
        return _Job([shard], [out_shape], _job_sems(), start, mid, end)
    return _Job([shard, prev], [out_shape], _job_sems(), start, mid, end, alias={1: 0})


def _exchange_job(src, land, lo=0, hi=1, parts=1):
    n = src.shape[1] // parts
    assert n * parts == src.shape[1]

    def sl(ref, s):
        return ref.at[s, pl.ds(lo * n, (hi - lo) * n)]

    def start(ins, outs, sems):
        me = _my_index()
        pltpu.make_async_copy(sl(ins[0], me), sl(outs[0], me), sems[2].at[0]).start()
        for k, (pos, idx) in enumerate(_peers()):
            pltpu.make_async_remote_copy(
                src_ref=sl(ins[0], idx), dst_ref=sl(outs[0], me), send_sem=sems[0].at[k], recv_sem=sems[1].at[k],
                device_id=pos, device_id_type=MESH).start()

    def mid(ins, outs, sems):
        pass

    def end(ins, outs, sems):
        me = _my_index()
        for k, (pos, idx) in enumerate(_peers()):
            cp = pltpu.make_async_remote_copy(
                src_ref=sl(ins[0], idx), dst_ref=sl(outs[0], idx), send_sem=sems[0].at[k], recv_sem=sems[1].at[k],
                device_id=pos, device_id_type=MESH)
            cp.wait_recv()
            cp.wait_send()
        pltpu.make_async_copy(sl(ins[0], me), sl(outs[0], me), sems[2].at[0]).wait()

    return _Job([src, land], [jax.ShapeDtypeStruct(land.shape, land.dtype)], _job_sems(), start, mid, end, alias={1: 0})


def _adamw_math(w, g, m, v):
    m = ADAM_B1 * m + (1.0 - ADAM_B1) * g
    v = ADAM_B2 * v + (1.0 - ADAM_B2) * (g * g)
    m_hat = m / (1.0 - ADAM_B1 ** ADAM_STEP)
    v_hat = v / (1.0 - ADAM_B2 ** ADAM_STEP)
    delta = -ADAM_LR * (m_hat / (jnp.sqrt(v_hat) + ADAM_EPS) + ADAM_WD * w)
    return delta, m, v


def _sum_slots(ref):
    g = ref[0].astype(F32)
    for s in range(1, N_DEV):
        g = g + ref[s].astype(F32)
    return g


def _adamw_big(name, lands, w, m, v, jobs=(), transposed=False):
    L, R, C = w.shape
    assert len(lands) == L
    tr = _tile(R, max(LANES, (256 * 1024) // C))
    nr = R // tr

    def body(*refs):
        l_refs = refs[:L]
        w_ref, m_ref, v_ref, g_ref, d_ref, nm_ref, nv_ref = refs[L:]
        for li in range(L):
            @pl.when(pl.program_id(0) // nr == li)
            def _(li=li):
                g = _sum_slots(l_refs[li])
                if transposed:
                    g = g.T
                d, nm, nv = _adamw_math(w_ref[...], g, m_ref[...], v_ref[...])
                g_ref[...] = g
                d_ref[...] = d
                nm_ref[...] = nm
                nv_ref[...] = nv

    def land_spec(li):
        if transposed:
            return pl.BlockSpec((N_DEV, C, tr), lambda s: (0, 0, jnp.where(s // nr == li, s % nr, 0)))
        return pl.BlockSpec((N_DEV, tr, C), lambda s: (0, jnp.where(s // nr == li, s % nr, 0), 0))

    row = pl.BlockSpec((None, tr, C), lambda s: (s // nr, s % nr, 0))
    return _carry_call(
        body, name=name, steps=L * nr, in_specs=[land_spec(li) for li in range(L)] + [row, row, row],
        out_specs=[row] * 4, out_shape=[jax.ShapeDtypeStruct((L, R, C), F32)] * 4, scratch_shapes=[],
        args=(*lands, w, m, v), jobs=jobs)


def _sum8(name, slots):
    _, R, C = slots.shape

    def body(s_ref, o_ref):
        o_ref[...] = _sum_slots(s_ref)

    return pl.pallas_call(body, name=name, out_shape=jax.ShapeDtypeStruct((R, C), F32))(slots)


def _adamw_small(name, g, w, m, v):
    def body(g_ref, w_ref, m_ref, v_ref, d_ref, nm_ref, nv_ref):
        d, nm, nv = _adamw_math(w_ref[...], g_ref[...], m_ref[...], v_ref[...])
        d_ref[...] = d
        nm_ref[...] = nm
        nv_ref[...] = nv

    return pl.pallas_call(body, name=name, out_shape=[jax.ShapeDtypeStruct(w.shape, F32)] * 3)(g, w, m, v)


def _pack_rows(arrs):
    parts, spans, r0 = [], [], 0
    for a in arrs:
        flat = a.astype(F32).reshape(-1)
        rows = -(-flat.shape[0] // LANES)
        rows = -(-rows // SUBLANES) * SUBLANES
        flat = jnp.pad(flat, (0, rows * LANES - flat.shape[0]))
        parts.append(flat.reshape(rows, LANES))
        spans.append((r0, rows, a.shape))
        r0 += rows
    return jnp.concatenate(parts, axis=0), spans


def _unpack_rows(buf, span):
    r0, rows, shape = span
    n = math.prod(shape)
    return buf[..., r0:r0 + rows, :].reshape(buf.shape[:-2] + (rows * LANES,))[..., :n].reshape(buf.shape[:-2] + shape)


def _from_col_blocks(wb):
    B, K, n = wb.shape
    return jnp.transpose(wb, (1, 0, 2)).reshape(K, B * n)


def kernel(x, norm_gains, hyb_w_in, hyb_conv_a, hyb_conv_b, hyb_conv_b_bias, hyb_rg_w_a, hyb_rg_b_a, hyb_rg_w_x, hyb_rg_b_x, hyb_rg_lambda, hyb_w_out, sb_w_qkv, sb_w_o, mlp_w_up, mlp_w_down, loss_target, m_norm_gains, m_hyb_w_in, m_hyb_conv_a, m_hyb_conv_b, m_hyb_conv_b_bias, m_hyb_rg_w_a, m_hyb_rg_b_a, m_hyb_rg_w_x, m_hyb_rg_b_x, m_hyb_rg_lambda, m_hyb_w_out, m_sb_w_qkv, m_sb_w_o, m_mlp_w_up, m_mlp_w_down, v_norm_gains, v_hyb_w_in, v_hyb_conv_a, v_hyb_conv_b, v_hyb_conv_b_bias, v_hyb_rg_w_a, v_hyb_rg_b_a, v_hyb_rg_w_x, v_hyb_rg_b_x, v_hyb_rg_lambda, v_hyb_w_out, v_sb_w_qkv, v_sb_w_o, v_mlp_w_up, v_mlp_w_down):
    T, D = x.shape[1], x.shape[2]
    me = _my_index()

    small_shards, small_spans = _pack_rows([norm_gains, hyb_conv_a[0], hyb_conv_b[0]])
    (small_all,) = _run_jobs("gather_small", [_gather_job(small_shards)])
    gains_b = _unpack_rows(small_all, small_spans[0])
    gains = jnp.transpose(gains_b, (1, 2, 0, 3)).reshape(2, 4, D)
    conv_a = _from_col_blocks(_unpack_rows(small_all, small_spans[1]))
    conv_b = _from_col_blocks(_unpack_rows(small_all, small_spans[2]))

    shards = {"w_in": hyb_w_in[0].T, "w_out": hyb_w_out[0], "w_qkv": sb_w_qkv[0].T, "w_o": sb_w_o[0],
              "up0": mlp_w_up[0], "up1": mlp_w_up[1], "down0": mlp_w_down[0], "down1": mlp_w_down[1]}
    shards = {n: s.astype(BF16) for n, s in shards.items()}
    wts = _Weights({}, shards, GATHER_PLAN)
    grads_big = _Grads({n: lax.empty((N_DEV,) + s.shape, BF16) for n, s in shards.items()}, EXCHANGE_PLAN)

    sq, grad_x, small = _local_step(
        x[0], loss_target[0], gains, conv_a, conv_b, hyb_conv_b_bias, hyb_rg_w_a[0], hyb_rg_b_a, hyb_rg_w_x[0],
        hyb_rg_b_x, hyb_rg_lambda, wts, grads_big)


    names = ["norm_gains", "hyb_w_in", "hyb_conv_a", "hyb_conv_b", "hyb_conv_b_bias", "hyb_rg_w_a", "hyb_rg_b_a",
             "hyb_rg_w_x", "hyb_rg_b_x", "hyb_rg_lambda", "hyb_w_out", "sb_w_qkv", "sb_w_o", "mlp_w_up", "mlp_w_down"]
    params = dict(zip(names, [norm_gains, hyb_w_in, hyb_conv_a, hyb_conv_b, hyb_conv_b_bias, hyb_rg_w_a, hyb_rg_b_a,
                              hyb_rg_w_x, hyb_rg_b_x, hyb_rg_lambda, hyb_w_out, sb_w_qkv, sb_w_o, mlp_w_up, mlp_w_down]))
    moms = dict(zip(names, [m_norm_gains, m_hyb_w_in, m_hyb_conv_a, m_hyb_conv_b, m_hyb_conv_b_bias, m_hyb_rg_w_a,
                            m_hyb_rg_b_a, m_hyb_rg_w_x, m_hyb_rg_b_x, m_hyb_rg_lambda, m_hyb_w_out, m_sb_w_qkv,
                            m_sb_w_o, m_mlp_w_up, m_mlp_w_down]))
    vars_ = dict(zip(names, [v_norm_gains, v_hyb_w_in, v_hyb_conv_a, v_hyb_conv_b, v_hyb_conv_b_bias, v_hyb_rg_w_a,
                             v_hyb_rg_b_a, v_hyb_rg_w_x, v_hyb_rg_b_x, v_hyb_rg_lambda, v_hyb_w_out, v_sb_w_qkv,
                             v_sb_w_o, v_mlp_w_up, v_mlp_w_down]))
    grads, deltas, new_m, new_v = {}, {}, {}, {}

    big_lands = {"mlp_w_down": ["down0", "down1"], "mlp_w_up": ["up0", "up1"], "sb_w_qkv": ["w_qkv"], "sb_w_o": ["w_o"],
                 "hyb_w_out": ["w_out"], "hyb_w_in": ["w_in"]}
    for nm, keys in big_lands.items():
        call = f"adamw_{nm}"
        if nm == "hyb_w_in":
            grads_big.flush("exchange_grads")
        jobs = grads_big.jobs(call)
        assert not {k for k in keys} & {e[0] for e in EXCHANGE_PLAN.get(call, ())}
        outs = _adamw_big(call, [grads_big.lands[k] for k in keys], params[nm], moms[nm], vars_[nm], jobs=jobs,
                          transposed=nm in ("hyb_w_in", "sb_w_qkv"))
        grads[nm], deltas[nm], new_m[nm], new_v[nm] = outs[:4]
        grads_big.deliver(call, outs[4:])

    small_names = ["norm_gains", "hyb_conv_a", "hyb_conv_b", "hyb_conv_b_bias", "hyb_rg_w_a", "hyb_rg_b_a",
                   "hyb_rg_w_x", "hyb_rg_b_x", "hyb_rg_lambda"]
    small_keys = ["norm_gains", "conv_a", "conv_b", "conv_b_bias", "rg_w_a", "rg_b_a", "rg_w_x", "rg_b_x", "rg_lambda"]
    sg_buf, sg_spans = _pack_rows([small[k] for k in small_keys] + [sq.reshape(1)])
    (sg_all,) = _run_jobs("gather_small_grads", [_gather_job(sg_buf)])
    sg_sum = _sum8("sum_small_grads", sg_all)
    full = {nm: _unpack_rows(sg_sum, sp) for nm, sp in zip(small_names, sg_spans)}
    loss = _unpack_rows(sg_sum, sg_spans[-1])[0] * (0.5 / D)
    cb = (D // 2) // N_DEV
    small_grads = {
        "norm_gains": lax.dynamic_slice_in_dim(full["norm_gains"], me * (D // N_DEV), D // N_DEV, axis=2),
        "hyb_conv_a": lax.dynamic_slice_in_dim(full["hyb_conv_a"], me * cb, cb, axis=1)[None],
        "hyb_conv_b": lax.dynamic_slice_in_dim(full["hyb_conv_b"], me * cb, cb, axis=1)[None],
        "hyb_conv_b_bias": full["hyb_conv_b_bias"],
        "hyb_rg_w_a": full["hyb_rg_w_a"][None],
        "hyb_rg_b_a": full["hyb_rg_b_a"],
        "hyb_rg_w_x": full["hyb_rg_w_x"][None],
        "hyb_rg_b_x": full["hyb_rg_b_x"],
        "hyb_rg_lambda": full["hyb_rg_lambda"],
    }
    pk = lambda d: _pack_rows([d[nm] for nm in small_names])
    g_buf, spans = pk(small_grads)
    w_buf, _ = pk(params)
    m_buf, _ = pk(moms)
    v_buf, _ = pk(vars_)
    d_buf, nm_buf, nv_buf = _adamw_small("adamw_small", g_buf, w_buf, m_buf, v_buf)
    for nm, sp in zip(small_names, spans):
        grads[nm] = small_grads[nm]
        deltas[nm], new_m[nm], new_v[nm] = _unpack_rows(d_buf, sp), _unpack_rows(nm_buf, sp), _unpack_rows(nv_buf, sp)

    return (loss, grad_x[None], *[grads[n] for n in names], *[deltas[n] for n in names],
            *[new_m[n] for n in names], *[new_v[n] for n in names])
```
